```python
import jax, jax.numpy as jnp
from jax import lax
import numpy as np

D_MODEL = 2048
BATCH = 8
SEQ = 8192
DEPTH = 1

D_MIX = D_MODEL
HEAD_DIM = 128
D_A = D_MIX // 2
D_B = D_MIX - D_A
N_HEADS_A = D_A // HEAD_DIM
N_GROUPS_B = D_B // HEAD_DIM
CHUNK = 128
CONV_W = 3
D_FF = 5632
D_IN = 2 * D_A + 3 * D_B
LN_EPS = 1e-5
ALPHA = float((2 * DEPTH) ** 0.25)
BETA = float((8 * DEPTH) ** -0.25)

kernel_name = "hybrid_sgu_shortconv_macaron_deepnorm"


def layer_norm(x, g, b):
    xf = x.astype(jnp.float32)
    mu = jnp.mean(xf, axis=-1, keepdims=True)
    d = xf - mu
    var = jnp.mean(d * d, axis=-1, keepdims=True)
    y = d * lax.rsqrt(var + LN_EPS)
    return (y * g.astype(jnp.float32) + b.astype(jnp.float32)).astype(x.dtype)


def swiglu(x, w_gate, w_up, w_down):
    return (jax.nn.silu(x @ w_gate) * (x @ w_up)) @ w_down


def chunked_sgu(z, w_s, b_s, g_v, b_v):
    bsz, seq, _ = z.shape
    n_chunks = seq // CHUNK
    u, v = z[..., :D_A], z[..., D_A:]
    shp = (bsz, n_chunks, CHUNK, N_HEADS_A, HEAD_DIM)
    v = v.reshape(shp)
    v = layer_norm(v, g_v.reshape(N_HEADS_A, HEAD_DIM), b_v.reshape(N_HEADS_A, HEAD_DIM))
    causal = jnp.tril(jnp.ones((CHUNK, CHUNK), dtype=bool))
    w = jnp.where(causal[None], w_s, jnp.zeros((), w_s.dtype))
    mixed = jnp.einsum('hts,bnshd->bnthd', w, v)
    mixed = mixed + jnp.transpose(b_s)[:, :, None]
    return (u.reshape(shp) * mixed).reshape(bsz, seq, D_A)


def gated_short_conv(gate_b, gate_c, xt, conv_w):
    h = gate_c * xt
    rhs = conv_w[:, None, :]
    y = lax.conv_general_dilated(
        h, rhs, window_strides=(1,), padding=[(CONV_W - 1, 0)],
        dimension_numbers=('NWC', 'WIO', 'NWC'), feature_group_count=D_B)
    return gate_b * y


def _fwd_setup_inputs(seed: int = 0) -> dict:
    key = jax.random.key(seed)
    ks = jax.random.split(key, 24)
    f32 = jnp.float32
    n = lambda k, shape, s: jax.random.normal(k, shape, f32) * s
    L = DEPTH
    return {
        "x": jax.random.normal(ks[0], (BATCH, SEQ, D_MODEL), f32),
        "ffa_gate": n(ks[1], (L, D_MODEL, D_FF), D_MODEL ** -0.5),
        "ffa_up": n(ks[2], (L, D_MODEL, D_FF), D_MODEL ** -0.5),
        "ffa_down": n(ks[3], (L, D_FF, D_MODEL), BETA * D_FF ** -0.5),
        "ln_a_g": 1.0 + n(ks[4], (L, D_MODEL), 0.01),
        "ln_a_b": n(ks[5], (L, D_MODEL), 0.01),
        "w_in": n(ks[6], (L, D_MODEL, D_IN), D_MODEL ** -0.5),
        "b_in": n(ks[7], (L, D_IN), 0.01),
        "w_s": n(ks[8], (L, N_HEADS_A, CHUNK, CHUNK), 0.5 * CHUNK ** -0.5),
        "b_s": 1.0 + n(ks[9], (L, N_HEADS_A, CHUNK), 0.01),
        "ln_v_g": 1.0 + n(ks[10], (L, D_A), 0.01),
        "ln_v_b": n(ks[11], (L, D_A), 0.01),
        "conv_w": n(ks[12], (L, CONV_W, D_B), CONV_W ** -0.5),
        "w_out": n(ks[13], (L, D_MIX, D_MODEL), BETA * D_MIX ** -0.5),
        "b_out": n(ks[14], (L, D_MODEL), 0.01),
        "ln_m_g": 1.0 + n(ks[15], (L, D_MODEL), 0.01),
        "ln_m_b": n(ks[16], (L, D_MODEL), 0.01),
        "ffc_gate": n(ks[17], (L, D_MODEL, D_FF), D_MODEL ** -0.5),
        "ffc_up": n(ks[18], (L, D_MODEL, D_FF), D_MODEL ** -0.5),
        "ffc_down": n(ks[19], (L, D_FF, D_MODEL), BETA * D_FF ** -0.5),
        "ln_c_g": 1.0 + n(ks[20], (L, D_MODEL), 0.01),
        "ln_c_b": n(ks[21], (L, D_MODEL), 0.01),
    }


def _fwd_reference(x, ffa_gate, ffa_up, ffa_down, ln_a_g, ln_a_b, w_in, b_in, w_s, b_s,
              ln_v_g, ln_v_b, conv_w, w_out, b_out, ln_m_g, ln_m_b,
              ffc_gate, ffc_up, ffc_down, ln_c_g, ln_c_b):
    h = x
    for l in range(DEPTH):
        h = layer_norm(ALPHA * h + 0.5 * swiglu(h, ffa_gate[l], ffa_up[l], ffa_down[l]),
                       ln_a_g[l], ln_a_b[l])
        z = h @ w_in[l] + b_in[l]
        z_a = jax.nn.gelu(z[..., :2 * D_A])
        o = 2 * D_A
        gate_b = z[..., o:o + D_B]
        gate_c = z[..., o + D_B:o + 2 * D_B]
        xt = z[..., o + 2 * D_B:]
        y_a = chunked_sgu(z_a, w_s[l], b_s[l], ln_v_g[l], ln_v_b[l])
        y_b = gated_short_conv(gate_b, gate_c, xt, conv_w[l])
        mix = jnp.concatenate([y_a, y_b], axis=-1) @ w_out[l] + b_out[l]
        h = layer_norm(ALPHA * h + mix, ln_m_g[l], ln_m_b[l])
        h = layer_norm(ALPHA * h + 0.5 * swiglu(h, ffc_gate[l], ffc_up[l], ffc_down[l]),
                       ln_c_g[l], ln_c_b[l])
    return h


import jax as _jax
import jax.numpy as _jnp

TWIN_FORMAT = 'train_step'
FWD_PARAMS = ['x', 'ffa_gate', 'ffa_up', 'ffa_down', 'ln_a_g', 'ln_a_b', 'w_in', 'b_in', 'w_s', 'b_s', 'ln_v_g', 'ln_v_b', 'conv_w', 'w_out', 'b_out', 'ln_m_g', 'ln_m_b', 'ffc_gate', 'ffc_up', 'ffc_down', 'ln_c_g', 'ln_c_b']
TWIN_WEIGHTS = ['ffa_gate', 'ffa_up', 'ffa_down', 'ln_a_g', 'ln_a_b', 'w_in', 'b_in', 'w_s', 'b_s', 'ln_v_g', 'ln_v_b', 'conv_w', 'w_out', 'b_out', 'ln_m_g', 'ln_m_b', 'ffc_gate', 'ffc_up', 'ffc_down', 'ln_c_g', 'ln_c_b']
TWIN_DIFF_INPUT = 'x'
TWIN_INPUTS = ['x', 'ffa_gate', 'ffa_up', 'ffa_down', 'ln_a_g', 'ln_a_b', 'w_in', 'b_in', 'w_s', 'b_s', 'ln_v_g', 'ln_v_b', 'conv_w', 'w_out', 'b_out', 'ln_m_g', 'ln_m_b', 'ffc_gate', 'ffc_up', 'ffc_down', 'ln_c_g', 'ln_c_b', 'loss_target', 'm_ffa_gate', 'm_ffa_up', 'm_ffa_down', 'm_ln_a_g', 'm_ln_a_b', 'm_w_in', 'm_b_in', 'm_w_s', 'm_b_s', 'm_ln_v_g', 'm_ln_v_b', 'm_conv_w', 'm_w_out', 'm_b_out', 'm_ln_m_g', 'm_ln_m_b', 'm_ffc_gate', 'm_ffc_up', 'm_ffc_down', 'm_ln_c_g', 'm_ln_c_b', 'v_ffa_gate', 'v_ffa_up', 'v_ffa_down', 'v_ln_a_g', 'v_ln_a_b', 'v_w_in', 'v_b_in', 'v_w_s', 'v_b_s', 'v_ln_v_g', 'v_ln_v_b', 'v_conv_w', 'v_w_out', 'v_b_out', 'v_ln_m_g', 'v_ln_m_b', 'v_ffc_gate', 'v_ffc_up', 'v_ffc_down', 'v_ln_c_g', 'v_ln_c_b']
TWIN_OUTPUTS = ['loss', 'grad_x', 'grad_ffa_gate', 'grad_ffa_up', 'grad_ffa_down', 'grad_ln_a_g', 'grad_ln_a_b', 'grad_w_in', 'grad_b_in', 'grad_w_s', 'grad_b_s', 'grad_ln_v_g', 'grad_ln_v_b', 'grad_conv_w', 'grad_w_out', 'grad_b_out', 'grad_ln_m_g', 'grad_ln_m_b', 'grad_ffc_gate', 'grad_ffc_up', 'grad_ffc_down', 'grad_ln_c_g', 'grad_ln_c_b', 'delta_ffa_gate', 'delta_ffa_up', 'delta_ffa_down', 'delta_ln_a_g', 'delta_ln_a_b', 'delta_w_in', 'delta_b_in', 'delta_w_s', 'delta_b_s', 'delta_ln_v_g', 'delta_ln_v_b', 'delta_conv_w', 'delta_w_out', 'delta_b_out', 'delta_ln_m_g', 'delta_ln_m_b', 'delta_ffc_gate', 'delta_ffc_up', 'delta_ffc_down', 'delta_ln_c_g', 'delta_ln_c_b', 'new_m_ffa_gate', 'new_m_ffa_up', 'new_m_ffa_down', 'new_m_ln_a_g', 'new_m_ln_a_b', 'new_m_w_in', 'new_m_b_in', 'new_m_w_s', 'new_m_b_s', 'new_m_ln_v_g', 'new_m_ln_v_b', 'new_m_conv_w', 'new_m_w_out', 'new_m_b_out', 'new_m_ln_m_g', 'new_m_ln_m_b', 'new_m_ffc_gate', 'new_m_ffc_up', 'new_m_ffc_down', 'new_m_ln_c_g', 'new_m_ln_c_b', 'new_v_ffa_gate', 'new_v_ffa_up', 'new_v_ffa_down', 'new_v_ln_a_g', 'new_v_ln_a_b', 'new_v_w_in', 'new_v_b_in', 'new_v_w_s', 'new_v_b_s', 'new_v_ln_v_g', 'new_v_ln_v_b', 'new_v_conv_w', 'new_v_w_out', 'new_v_b_out', 'new_v_ln_m_g', 'new_v_ln_m_b', 'new_v_ffc_gate', 'new_v_ffc_up', 'new_v_ffc_down', 'new_v_ln_c_g', 'new_v_ln_c_b']
TWIN_LEAF_KINDS = {'loss': 'loss', 'grad_x': 'grad_x', 'grad_ffa_gate': 'grad_w', 'grad_ffa_up': 'grad_w', 'grad_ffa_down': 'grad_w', 'grad_ln_a_g': 'grad_w', 'grad_ln_a_b': 'grad_w', 'grad_w_in': 'grad_w', 'grad_b_in': 'grad_w', 'grad_w_s': 'grad_w', 'grad_b_s': 'grad_w', 'grad_ln_v_g': 'grad_w', 'grad_ln_v_b': 'grad_w', 'grad_conv_w': 'grad_w', 'grad_w_out': 'grad_w', 'grad_b_out': 'grad_w', 'grad_ln_m_g': 'grad_w', 'grad_ln_m_b': 'grad_w', 'grad_ffc_gate': 'grad_w', 'grad_ffc_up': 'grad_w', 'grad_ffc_down': 'grad_w', 'grad_ln_c_g': 'grad_w', 'grad_ln_c_b': 'grad_w', 'delta_ffa_gate': 'delta_w', 'delta_ffa_up': 'delta_w', 'delta_ffa_down': 'delta_w', 'delta_ln_a_g': 'delta_w', 'delta_ln_a_b': 'delta_w', 'delta_w_in': 'delta_w', 'delta_b_in': 'delta_w', 'delta_w_s': 'delta_w', 'delta_b_s': 'delta_w', 'delta_ln_v_g': 'delta_w', 'delta_ln_v_b': 'delta_w', 'delta_conv_w': 'delta_w', 'delta_w_out': 'delta_w', 'delta_b_out': 'delta_w', 'delta_ln_m_g': 'delta_w', 'delta_ln_m_b': 'delta_w', 'delta_ffc_gate': 'delta_w', 'delta_ffc_up': 'delta_w', 'delta_ffc_down': 'delta_w', 'delta_ln_c_g': 'delta_w', 'delta_ln_c_b': 'delta_w', 'new_m_ffa_gate': 'new_m', 'new_m_ffa_up': 'new_m', 'new_m_ffa_down': 'new_m', 'new_m_ln_a_g': 'new_m', 'new_m_ln_a_b': 'new_m', 'new_m_w_in': 'new_m', 'new_m_b_in': 'new_m', 'new_m_w_s': 'new_m', 'new_m_b_s': 'new_m', 'new_m_ln_v_g': 'new_m', 'new_m_ln_v_b': 'new_m', 'new_m_conv_w': 'new_m', 'new_m_w_out': 'new_m', 'new_m_b_out': 'new_m', 'new_m_ln_m_g': 'new_m', 'new_m_ln_m_b': 'new_m', 'new_m_ffc_gate': 'new_m', 'new_m_ffc_up': 'new_m', 'new_m_ffc_down': 'new_m', 'new_m_ln_c_g': 'new_m', 'new_m_ln_c_b': 'new_m', 'new_v_ffa_gate': 'new_v', 'new_v_ffa_up': 'new_v', 'new_v_ffa_down': 'new_v', 'new_v_ln_a_g': 'new_v', 'new_v_ln_a_b': 'new_v', 'new_v_w_in': 'new_v', 'new_v_b_in': 'new_v', 'new_v_w_s': 'new_v', 'new_v_b_s': 'new_v', 'new_v_ln_v_g': 'new_v', 'new_v_ln_v_b': 'new_v', 'new_v_conv_w': 'new_v', 'new_v_w_out': 'new_v', 'new_v_b_out': 'new_v', 'new_v_ln_m_g': 'new_v', 'new_v_ln_m_b': 'new_v', 'new_v_ffc_gate': 'new_v', 'new_v_ffc_up': 'new_v', 'new_v_ffc_down': 'new_v', 'new_v_ln_c_g': 'new_v', 'new_v_ln_c_b': 'new_v'}


def _forward(args):
    return _fwd_reference(*[args[k] for k in FWD_PARAMS])


def _output_shape():
    def fwd():
        inp = _fwd_setup_inputs(0)
        return _fwd_reference(*[inp[k] for k in FWD_PARAMS])
    out = _jax.eval_shape(fwd)
    return out.shape, out.dtype

N_MICROBATCH = 1
ADAM_LR = 0.001
ADAM_B1 = 0.9
ADAM_B2 = 0.999
ADAM_EPS = 1e-08
ADAM_WD = 0.01
ADAM_STEP = 10
PER_EXAMPLE_BATCH_AXIS = {'x': 0, 'loss_target': 0}
SHARED_INPUTS = []
_WEIGHT_DTYPES = {'ffa_gate': _jnp.float32, 'ffa_up': _jnp.float32, 'ffa_down': _jnp.float32, 'ln_a_g': _jnp.float32, 'ln_a_b': _jnp.float32, 'w_in': _jnp.float32, 'b_in': _jnp.float32, 'w_s': _jnp.float32, 'b_s': _jnp.float32, 'ln_v_g': _jnp.float32, 'ln_v_b': _jnp.float32, 'conv_w': _jnp.float32, 'w_out': _jnp.float32, 'b_out': _jnp.float32, 'ln_m_g': _jnp.float32, 'ln_m_b': _jnp.float32, 'ffc_gate': _jnp.float32, 'ffc_up': _jnp.float32, 'ffc_down': _jnp.float32, 'ln_c_g': _jnp.float32, 'ln_c_b': _jnp.float32}
MOMENT_SCALE = {'ffa_gate': 1.280881e-02, 'ffa_up': 1.240113e-02, 'ffa_down': 3.458955e-02, 'ln_a_g': 3.873923e-01, 'ln_a_b': 2.258184e-01, 'w_in': 4.867948e-02, 'b_in': 5.160420e-02, 'w_s': 2.708000e-02, 'b_s': 3.912824e-02, 'ln_v_g': 1.359030e-02, 'ln_v_b': 1.391426e-02, 'conv_w': 5.854150e-02, 'w_out': 8.721637e-02, 'b_out': 1.817888e-01, 'ln_m_g': 4.655102e-01, 'ln_m_b': 2.341360e-01, 'ffc_gate': 1.148925e-02, 'ffc_up': 1.114477e-02, 'ffc_down': 3.106663e-02, 'ln_c_g': 3.198277e+01, 'ln_c_b': 2.809773e+00}


def _to_microbatches(a, axis):
    t = _jnp.moveaxis(a, axis, 0)
    t = t.reshape((N_MICROBATCH, t.shape[0] // N_MICROBATCH) + t.shape[1:])
    return _jnp.moveaxis(t, 1, axis + 1)


def setup_inputs(seed: int = 0) -> dict:
    inp = _fwd_setup_inputs(seed)
    key = _jax.random.fold_in(_jax.random.key(seed), 7919)
    shape, _ = _output_shape()
    out = dict(inp)
    out["loss_target"] = _jax.random.normal(_jax.random.fold_in(key, 0), shape, _jnp.float32)
    for i, name in enumerate(TWIN_WEIGHTS):
        w = inp[name].astype(_jnp.float32)
        if MOMENT_SCALE is None:
            s = _jnp.sqrt(_jnp.mean(_jnp.square(w)) + 1e-30)
        else:
            s = MOMENT_SCALE[name]
        km, kv = _jax.random.split(_jax.random.fold_in(key, i + 1))
        out[name] = w
        out["m_" + name] = s * _jax.random.normal(km, w.shape, _jnp.float32)
        out["v_" + name] = (s * s) * _jax.random.uniform(kv, w.shape, _jnp.float32, 0.5, 1.5)
    if N_MICROBATCH > 1:
        for name, axis in PER_EXAMPLE_BATCH_AXIS.items():
            out[name] = _to_microbatches(out[name], axis)
    return {'x': out['x'], 'ffa_gate': out['ffa_gate'], 'ffa_up': out['ffa_up'], 'ffa_down': out['ffa_down'], 'ln_a_g': out['ln_a_g'], 'ln_a_b': out['ln_a_b'], 'w_in': out['w_in'], 'b_in': out['b_in'], 'w_s': out['w_s'], 'b_s': out['b_s'], 'ln_v_g': out['ln_v_g'], 'ln_v_b': out['ln_v_b'], 'conv_w': out['conv_w'], 'w_out': out['w_out'], 'b_out': out['b_out'], 'ln_m_g': out['ln_m_g'], 'ln_m_b': out['ln_m_b'], 'ffc_gate': out['ffc_gate'], 'ffc_up': out['ffc_up'], 'ffc_down': out['ffc_down'], 'ln_c_g': out['ln_c_g'], 'ln_c_b': out['ln_c_b'], 'loss_target': out['loss_target'], 'm_ffa_gate': out['m_ffa_gate'], 'm_ffa_up': out['m_ffa_up'], 'm_ffa_down': out['m_ffa_down'], 'm_ln_a_g': out['m_ln_a_g'], 'm_ln_a_b': out['m_ln_a_b'], 'm_w_in': out['m_w_in'], 'm_b_in': out['m_b_in'], 'm_w_s': out['m_w_s'], 'm_b_s': out['m_b_s'], 'm_ln_v_g': out['m_ln_v_g'], 'm_ln_v_b': out['m_ln_v_b'], 'm_conv_w': out['m_conv_w'], 'm_w_out': out['m_w_out'], 'm_b_out': out['m_b_out'], 'm_ln_m_g': out['m_ln_m_g'], 'm_ln_m_b': out['m_ln_m_b'], 'm_ffc_gate': out['m_ffc_gate'], 'm_ffc_up': out['m_ffc_up'], 'm_ffc_down': out['m_ffc_down'], 'm_ln_c_g': out['m_ln_c_g'], 'm_ln_c_b': out['m_ln_c_b'], 'v_ffa_gate': out['v_ffa_gate'], 'v_ffa_up': out['v_ffa_up'], 'v_ffa_down': out['v_ffa_down'], 'v_ln_a_g': out['v_ln_a_g'], 'v_ln_a_b': out['v_ln_a_b'], 'v_w_in': out['v_w_in'], 'v_b_in': out['v_b_in'], 'v_w_s': out['v_w_s'], 'v_b_s': out['v_b_s'], 'v_ln_v_g': out['v_ln_v_g'], 'v_ln_v_b': out['v_ln_v_b'], 'v_conv_w': out['v_conv_w'], 'v_w_out': out['v_w_out'], 'v_b_out': out['v_b_out'], 'v_ln_m_g': out['v_ln_m_g'], 'v_ln_m_b': out['v_ln_m_b'], 'v_ffc_gate': out['v_ffc_gate'], 'v_ffc_up': out['v_ffc_up'], 'v_ffc_down': out['v_ffc_down'], 'v_ln_c_g': out['v_ln_c_g'], 'v_ln_c_b': out['v_ln_c_b']}


def _loss(weights, diff, rest, loss_target):
    with _jax.named_scope("forward"):
        args = {**rest, TWIN_DIFF_INPUT: diff, **{k: w.astype(_WEIGHT_DTYPES[k]) for k, w in weights.items()}}
        y = _forward(args)
    with _jax.named_scope("loss_head"):
        err = _jnp.square(y.astype(_jnp.float32) - loss_target)
        return 0.5 * _jnp.sum(_jnp.mean(err, axis=-1)) if err.ndim else 0.5 * err


def _adamw(w, g, m, v):
    m = ADAM_B1 * m + (1.0 - ADAM_B1) * g
    v = ADAM_B2 * v + (1.0 - ADAM_B2) * _jnp.square(g)
    m_hat = m / (1.0 - ADAM_B1 ** ADAM_STEP)
    v_hat = v / (1.0 - ADAM_B2 ** ADAM_STEP)
    delta = -ADAM_LR * (m_hat / (_jnp.sqrt(v_hat) + ADAM_EPS) + ADAM_WD * w)
    return delta, m, v


def reference(x, ffa_gate, ffa_up, ffa_down, ln_a_g, ln_a_b, w_in, b_in, w_s, b_s, ln_v_g, ln_v_b, conv_w, w_out, b_out, ln_m_g, ln_m_b, ffc_gate, ffc_up, ffc_down, ln_c_g, ln_c_b, loss_target, m_ffa_gate, m_ffa_up, m_ffa_down, m_ln_a_g, m_ln_a_b, m_w_in, m_b_in, m_w_s, m_b_s, m_ln_v_g, m_ln_v_b, m_conv_w, m_w_out, m_b_out, m_ln_m_g, m_ln_m_b, m_ffc_gate, m_ffc_up, m_ffc_down, m_ln_c_g, m_ln_c_b, v_ffa_gate, v_ffa_up, v_ffa_down, v_ln_a_g, v_ln_a_b, v_w_in, v_b_in, v_w_s, v_b_s, v_ln_v_g, v_ln_v_b, v_conv_w, v_w_out, v_b_out, v_ln_m_g, v_ln_m_b, v_ffc_gate, v_ffc_up, v_ffc_down, v_ln_c_g, v_ln_c_b):
    given = dict(x=x, ffa_gate=ffa_gate, ffa_up=ffa_up, ffa_down=ffa_down, ln_a_g=ln_a_g, ln_a_b=ln_a_b, w_in=w_in, b_in=b_in, w_s=w_s, b_s=b_s, ln_v_g=ln_v_g, ln_v_b=ln_v_b, conv_w=conv_w, w_out=w_out, b_out=b_out, ln_m_g=ln_m_g, ln_m_b=ln_m_b, ffc_gate=ffc_gate, ffc_up=ffc_up, ffc_down=ffc_down, ln_c_g=ln_c_g, ln_c_b=ln_c_b, loss_target=loss_target, m_ffa_gate=m_ffa_gate, m_ffa_up=m_ffa_up, m_ffa_down=m_ffa_down, m_ln_a_g=m_ln_a_g, m_ln_a_b=m_ln_a_b, m_w_in=m_w_in, m_b_in=m_b_in, m_w_s=m_w_s, m_b_s=m_b_s, m_ln_v_g=m_ln_v_g, m_ln_v_b=m_ln_v_b, m_conv_w=m_conv_w, m_w_out=m_w_out, m_b_out=m_b_out, m_ln_m_g=m_ln_m_g, m_ln_m_b=m_ln_m_b, m_ffc_gate=m_ffc_gate, m_ffc_up=m_ffc_up, m_ffc_down=m_ffc_down, m_ln_c_g=m_ln_c_g, m_ln_c_b=m_ln_c_b, v_ffa_gate=v_ffa_gate, v_ffa_up=v_ffa_up, v_ffa_down=v_ffa_down, v_ln_a_g=v_ln_a_g, v_ln_a_b=v_ln_a_b, v_w_in=v_w_in, v_b_in=v_b_in, v_w_s=v_w_s, v_b_s=v_b_s, v_ln_v_g=v_ln_v_g, v_ln_v_b=v_ln_v_b, v_conv_w=v_conv_w, v_w_out=v_w_out, v_b_out=v_b_out, v_ln_m_g=v_ln_m_g, v_ln_m_b=v_ln_m_b, v_ffc_gate=v_ffc_gate, v_ffc_up=v_ffc_up, v_ffc_down=v_ffc_down, v_ln_c_g=v_ln_c_g, v_ln_c_b=v_ln_c_b)
    weights = {n: given[n] for n in TWIN_WEIGHTS}
    shared = {n: given[n] for n in SHARED_INPUTS}
    per_example = {n: given[n] for n in ['x']}
    grad_fn = _jax.value_and_grad(_loss, argnums=(0, 1))

    def one_microbatch(ex, loss_target):
        ex = dict(ex)
        diff = ex.pop(TWIN_DIFF_INPUT)
        return grad_fn(weights, diff, {**shared, **ex}, loss_target)

    if N_MICROBATCH == 1:
        loss, (grad_w, grad_x) = one_microbatch(per_example, given["loss_target"])
    else:
        def body(carry, xs):
            loss_sum, grad_sum = carry
            l_k, (gw_k, gx_k) = one_microbatch(xs[0], xs[1])
            with _jax.named_scope("update"):
                return (loss_sum + l_k, _jax.tree.map(_jnp.add, grad_sum, gw_k)), gx_k

        init = (_jnp.zeros((), _jnp.float32), _jax.tree.map(_jnp.zeros_like, weights))
        (loss, grad_w), grad_x = _jax.lax.scan(body, init, (per_example, given["loss_target"]))
    with _jax.named_scope("update"):
        delta_w, new_m, new_v = {}, {}, {}
        for n in TWIN_WEIGHTS:
            delta_w[n], new_m[n], new_v[n] = _adamw(weights[n], grad_w[n], given["m_" + n], given["v_" + n])
    return (loss, grad_x, *[grad_w[n] for n in TWIN_WEIGHTS], *[delta_w[n] for n in TWIN_WEIGHTS],
            *[new_m[n] for n in TWIN_WEIGHTS], *[new_v[n] for n in TWIN_WEIGHTS])
```

```python
import functools
import math

import jax
import jax.numpy as jnp
from jax import lax
from jax.experimental import pallas as pl
from jax.experimental.pallas import tpu as pltpu

F32 = jnp.float32
BF16 = jnp.bfloat16
MESH = pl.DeviceIdType.MESH
ANY = pl.BlockSpec(memory_space=pl.ANY)

HEAD = 128
CONV_TAPS = 3
LN_EPS = 1e-5
ALPHA = float(2.0 ** 0.25)
ADAM_LR, ADAM_B1, ADAM_B2, ADAM_EPS, ADAM_WD, ADAM_STEP = 0.001, 0.9, 0.999, 1e-08, 0.01, 10

V7X_VMEM_BYTES = 64 * 2 ** 20
N_CHIPS = 4

BIG = ['ffa_gate', 'ffa_up', 'ffa_down', 'w_in', 'w_out', 'ffc_gate', 'ffc_up', 'ffc_down']
WEIGHTS = ['ffa_gate', 'ffa_up', 'ffa_down', 'ln_a_g', 'ln_a_b', 'w_in', 'b_in', 'w_s', 'b_s', 'ln_v_g', 'ln_v_b',
           'conv_w', 'w_out', 'b_out', 'ln_m_g', 'ln_m_b', 'ffc_gate', 'ffc_up', 'ffc_down', 'ln_c_g', 'ln_c_b']
SMALL = [n for n in WEIGHTS if n not in BIG and n != 'conv_w']


def _vmem_limit(estimate_bytes):
    return int(min(max(estimate_bytes * 1.3, 32 * 2 ** 20), V7X_VMEM_BYTES - 6 * 2 ** 20))


def _blk(dim, pref, mult=128):
    if dim <= pref:
        return dim
    best = None
    for d in range(mult, pref + 1, mult):
        if dim % d == 0:
            best = d
    assert best is not None, (dim, pref)
    return best


def _rows(n_rows, row_bytes, target=4 * 2 ** 20):
    return _blk(n_rows, max(16, target // row_bytes), 16)


def _gelu(x):
    c = math.sqrt(2.0 / math.pi)
    return 0.5 * x * (1.0 + jnp.tanh(c * (x + 0.044715 * (x * x * x))))


def _gelu_and_grad(x):
    c = math.sqrt(2.0 / math.pi)
    t = jnp.tanh(c * (x + 0.044715 * (x * x * x)))
    g = 0.5 * x * (1.0 + t)
    dg = 0.5 * (1.0 + t) + 0.5 * x * (1.0 - t * t) * (c * (1.0 + 3.0 * 0.044715 * (x * x)))
    return g, dg


def _ln_stats(r):
    mu = jnp.mean(r, axis=-1, keepdims=True)
    d = r - mu
    var = jnp.mean(d * d, axis=-1, keepdims=True)
    rstd = lax.rsqrt(var + LN_EPS)
    return d * rstd, rstd


_DIMS = {'nn': (((1,), (0,)), ((), ())), 'nt': (((1,), (1,)), ((), ())), 'tn': (((0,), (0,)), ((), ()))}


def _mm(name, mode, a_list, b_list, pairs, n_acc, epilogue, out_dtypes, bm, bn, bk,
        tile_extras=(), row_extras=(), b_blocked=False, out_blocked=False, n_outer=False):
    a0, b0 = a_list[0], b_list[0]
    if mode == 'nn':
        M, K = a0.shape
        N = b0.shape[0] * b0.shape[2] if b_blocked else b0.shape[1]
    elif mode == 'nt':
        M, K = a0.shape
        N = b0.shape[1] if b_blocked else b0.shape[0]
    else:
        K, M = a0.shape
        N = b0.shape[1]
        assert not b_blocked
    assert M % bm == 0 and N % bn == 0 and K % bk == 0, (name, M, N, K, bm, bn, bk)
    gm, gn, gk = M // bm, N // bn, K // bk
    if b_blocked:
        assert (bn if mode == 'nn' else bk) == b0.shape[2], name
    dims = _DIMS[mode]

    def spec(shape, f):
        if n_outer:
            return pl.BlockSpec(shape, lambda g0, g1, g2: f(g1, g0, g2))
        return pl.BlockSpec(shape, f)

    if mode == 'tn':
        a_spec = spec((bk, bm), lambda i, j, k: (k, i))
    else:
        a_spec = spec((bm, bk), lambda i, j, k: (i, k))
    if mode == 'nn':
        b_spec = (spec((None, bk, bn), lambda i, j, k: (j, k, 0)) if b_blocked
                  else spec((bk, bn), lambda i, j, k: (k, j)))
    elif mode == 'nt':
        b_spec = (spec((None, bn, bk), lambda i, j, k: (k, j, 0)) if b_blocked
                  else spec((bn, bk), lambda i, j, k: (j, k)))
    else:
        b_spec = spec((bk, bn), lambda i, j, k: (k, j))
    te_spec = spec((bm, bn), lambda i, j, k: (i, j))
    re_spec = spec((1, bn), lambda i, j, k: (0, j))
    if out_blocked:
        o_spec = spec((None, bm, bn), lambda i, j, k: (j, i, 0))
        o_shape = (gn, M, bn)
    else:
        o_spec = te_spec
        o_shape = (M, N)

    na, nb, nte, nre, no = len(a_list), len(b_list), len(tile_extras), len(row_extras), len(out_dtypes)

    def body(*refs):
        a_refs = refs[:na]
        b_refs = refs[na:na + nb]
        te_refs = refs[na + nb:na + nb + nte]
        re_refs = refs[na + nb + nte:na + nb + nte + nre]
        o_refs = refs[na + nb + nte + nre:na + nb + nte + nre + no]
        acc_refs = refs[na + nb + nte + nre + no:]

        def finish(accs):
            outs = epilogue(accs, [r[...] for r in te_refs], [r[...] for r in re_refs])
            for o_ref, o in zip(o_refs, outs):
                o_ref[...] = o.astype(o_ref.dtype)

        if gk == 1:
            parts = [None] * n_acc
            for ai, bi, ci in pairs:
                d = lax.dot_general(a_refs[ai][...], b_refs[bi][...], dims, preferred_element_type=F32)
                parts[ci] = d if parts[ci] is None else parts[ci] + d
            finish(parts)
        else:
            kk = pl.program_id(2)

            @pl.when(kk == 0)
            def _():
                for r in acc_refs:
                    r[...] = jnp.zeros_like(r)

            for ai, bi, ci in pairs:
                acc_refs[ci][...] += lax.dot_general(a_refs[ai][...], b_refs[bi][...], dims,
                                                     preferred_element_type=F32)

            @pl.when(kk == gk - 1)
            def _():
                finish([r[...] for r in acc_refs])

    isz = lambda a: jnp.dtype(a.dtype).itemsize
    est = 2 * (sum(bm * bk * isz(a) for a in a_list) + sum(bk * bn * isz(b) for b in b_list)
               + sum(bm * bn * isz(t) for t in tile_extras)
               + sum(bm * bn * jnp.dtype(d).itemsize for d in out_dtypes))
    est += (2 * n_acc + 2) * bm * bn * 4
    grid = (gn, gm, gk) if n_outer else (gm, gn, gk)
    outs = pl.pallas_call(
        body, name=name, grid=grid,
        in_specs=[a_spec] * na + [b_spec] * nb + [te_spec] * nte + [re_spec] * nre,
        out_specs=[o_spec] * no,
        out_shape=[jax.ShapeDtypeStruct(o_shape, d) for d in out_dtypes],
        scratch_shapes=[pltpu.VMEM((bm, bn), F32)] * (n_acc if gk > 1 else 0),
        compiler_params=pltpu.CompilerParams(dimension_semantics=("parallel", "parallel", "arbitrary"),
                                             vmem_limit_bytes=_vmem_limit(est)),
    )(*a_list, *b_list, *tile_extras, *row_extras)
    return outs


def _rowwise(name, fn, br, row_ins, vec_ins, row_outs, acc_outs=()):
    S = row_ins[0].shape[0]
    assert S % br == 0, (name, S, br)
    nr, nv, no, nacc = len(row_ins), len(vec_ins), len(row_outs), len(acc_outs)

    def body(*refs):
        rin = refs[:nr]
        vin = refs[nr:nr + nv]
        rout = refs[nr + nv:nr + nv + no]
        aout = refs[nr + nv + no:]
        if nacc:
            @pl.when(pl.program_id(0) == 0)
            def _():
                for a in aout:
                    a[...] = jnp.zeros_like(a)
        fn(rin, vin, rout, aout)

    def full_spec(shape):
        nd = len(shape)
        return pl.BlockSpec(shape, lambda i: (0,) * nd)

    est = 2 * (sum(br * a.shape[1] * jnp.dtype(a.dtype).itemsize for a in row_ins)
               + sum(br * c * jnp.dtype(d).itemsize for c, d in row_outs))
    est += 6 * br * max(a.shape[1] for a in row_ins) * 4
    outs = pl.pallas_call(
        body, name=name, grid=(S // br,),
        in_specs=[pl.BlockSpec((br, a.shape[1]), lambda i: (i, 0)) for a in row_ins]
        + [full_spec(v.shape) for v in vec_ins],
        out_specs=[pl.BlockSpec((br, c), lambda i: (i, 0)) for c, _ in row_outs]
        + [full_spec(s) for s, _ in acc_outs],
        out_shape=[jax.ShapeDtypeStruct((S, c), d) for c, d in row_outs]
        + [jax.ShapeDtypeStruct(s, d) for s, d in acc_outs],
        compiler_params=pltpu.CompilerParams(dimension_semantics=("arbitrary",),
                                             vmem_limit_bytes=_vmem_limit(est)),
    )(*row_ins, *vec_ins)
    return outs


def _cast_bf16(name, a):
    def fn(rin, vin, rout, aout):
        rout[0][...] = rin[0][...].astype(BF16)
    return _rowwise(name, fn, _rows(a.shape[0], a.shape[1] * 4), [a], [], [(a.shape[1], BF16)])[0]


def _ln_fwd(name, r, g, b):
    def fn(rin, vin, rout, aout):
        xhat, _ = _ln_stats(rin[0][...])
        h = xhat * vin[0][...] + vin[1][...]
        rout[0][...] = h
        rout[1][...] = h.astype(BF16)
    D = r.shape[1]
    return _rowwise(name, fn, _rows(r.shape[0], D * 4, 2 * 2 ** 20), [r], [g, b], [(D, F32), (D, BF16)])


def _ln_loss(name, r, target, g, b):
    D = r.shape[1]

    def fn(rin, vin, rout, aout):
        xhat, _ = _ln_stats(rin[0][...])
        d = xhat * vin[0][...] + vin[1][...] - rin[1][...]
        rout[0][...] = d * (1.0 / D)
        aout[0][...] += jnp.sum(d * d, axis=0, keepdims=True)
    return _rowwise(name, fn, _rows(r.shape[0], D * 4, 2 * 2 ** 20), [r, target], [g, b],
                    [(D, F32)], [((1, D), F32)])


def _ln_bwd(name, r, dh, g, sub_scale):
    D = r.shape[1]

    def fn(rin, vin, rout, aout):
        xhat, rstd = _ln_stats(rin[0][...])
        dhv = rin[1][...]
        dxh = dhv * vin[0][...]
        m1 = jnp.mean(dxh, axis=-1, keepdims=True)
        m2 = jnp.mean(dxh * xhat, axis=-1, keepdims=True)
        dr = rstd * (dxh - m1 - xhat * m2)
        rout[0][...] = ALPHA * dr
        rout[1][...] = (sub_scale * dr).astype(BF16)
        aout[0][...] += jnp.sum(dhv * xhat, axis=0, keepdims=True)
        aout[1][...] += jnp.sum(dhv, axis=0, keepdims=True)
        aout[2][...] += jnp.sum(dr, axis=0, keepdims=True)
    return _rowwise(name, fn, _rows(r.shape[0], D * 4, 2 * 2 ** 20), [r, dh], [g],
                    [(D, F32), (D, BF16)], [((1, D), F32)] * 3)


def _adamw_math(w, g, m, v):
    m2 = ADAM_B1 * m + (1.0 - ADAM_B1) * g
    v2 = ADAM_B2 * v + (1.0 - ADAM_B2) * (g * g)
    m_hat = m2 / (1.0 - ADAM_B1 ** ADAM_STEP)
    v_hat = v2 / (1.0 - ADAM_B2 ** ADAM_STEP)
    delta = -ADAM_LR * (m_hat / (jnp.sqrt(v_hat) + ADAM_EPS) + ADAM_WD * w)
    return delta, m2, v2


def _adamw(name, w, g, m, v):
    def fn(rin, vin, rout, aout):
        d, m2, v2 = _adamw_math(rin[0][...], rin[1][...], rin[2][...], rin[3][...])
        rout[0][...] = d
        rout[1][...] = m2
        rout[2][...] = v2
    C = w.shape[1]
    return _rowwise(name, fn, _rows(w.shape[0], C * 4, 2 * 2 ** 20), [w, g, m, v], [], [(C, F32)] * 3)


def _conv_shifted(hc, p1, p2, rowi):
    r1 = jnp.where(rowi == 0, p1, pltpu.roll(hc, 1, 0))
    r2 = jnp.where(rowi == 0, p2, jnp.where(rowi == 1, p1, pltpu.roll(hc, 2, 0)))
    return r1, r2


def _mix_fwd(z, ws, bsb, gv, bv, cw8, tm):
    S, DIN = z.shape
    DA = gv.shape[1]
    DB = DA
    H = DA // HEAD
    o = 2 * DA
    nch = tm // HEAD

    def body(z_ref, zp_ref, ws_ref, bsb_ref, gv_ref, bv_ref, cw_ref, y_ref):
        i = pl.program_id(0)
        ri = lax.broadcasted_iota(jnp.int32, (HEAD, HEAD), 0)
        ci = lax.broadcasted_iota(jnp.int32, (HEAD, HEAD), 1)
        tril = ri >= ci
        for h in range(H):
            cs = slice(h * HEAD, (h + 1) * HEAD)
            vs = slice(DA + h * HEAD, DA + (h + 1) * HEAD)
            wb = jnp.where(tril, ws_ref[h], 0.0).astype(BF16)
            for n in range(nch):
                rs = slice(n * HEAD, (n + 1) * HEAD)
                u = _gelu(z_ref[rs, cs])
                vhat, _ = _ln_stats(_gelu(z_ref[rs, vs]))
                vn = vhat * gv_ref[:, cs] + bv_ref[:, cs]
                mixed = jnp.dot(wb, vn.astype(BF16), preferred_element_type=F32) + bsb_ref[h]
                y_ref[rs, cs] = (u * mixed).astype(BF16)
        zb = z_ref[:, o:o + DB]
        hc = z_ref[:, o + DB:o + 2 * DB] * z_ref[:, o + 2 * DB:]
        keep = (i > 0).astype(F32)
        p1 = zp_ref[7:8, o + DB:o + 2 * DB] * zp_ref[7:8, o + 2 * DB:] * keep
        p2 = zp_ref[6:7, o + DB:o + 2 * DB] * zp_ref[6:7, o + 2 * DB:] * keep
        rowi = lax.broadcasted_iota(jnp.int32, (tm, DB), 0)
        r1, r2 = _conv_shifted(hc, p1, p2, rowi)
        cv = cw_ref[0:1, :] * r2 + cw_ref[1:2, :] * r1 + cw_ref[2:3, :] * hc
        y_ref[:, DA:] = (zb * cv).astype(BF16)

    full3 = lambda i: (0, 0, 0)
    full2 = lambda i: (0, 0)
    est = 2 * (tm * DIN * 4 + tm * (DA + DB) * 2) + 10 * tm * DB * 4
    return pl.pallas_call(
        body, name="mix_fwd", grid=(S // tm,),
        in_specs=[pl.BlockSpec((tm, DIN), lambda i: (i, 0)),
                  pl.BlockSpec((8, DIN), lambda i: (jnp.maximum(i * (tm // 8) - 1, 0), 0)),
                  pl.BlockSpec(ws.shape, full3), pl.BlockSpec(bsb.shape, full3),
                  pl.BlockSpec(gv.shape, full2), pl.BlockSpec(bv.shape, full2), pl.BlockSpec(cw8.shape, full2)],
        out_specs=pl.BlockSpec((tm, DA + DB), lambda i: (i, 0)),
        out_shape=jax.ShapeDtypeStruct((S, DA + DB), BF16),
        compiler_params=pltpu.CompilerParams(dimension_semantics=("arbitrary",),
                                             vmem_limit_bytes=_vmem_limit(est)),
    )(z, z, ws, bsb, gv, bv, cw8)


def _mix_bwd(z, dy, ws, bsb, gv, bv, cw8, tm):
    S, DIN = z.shape
    DA = gv.shape[1]
    DB = DA
    H = DA // HEAD
    o = 2 * DA
    nch = tm // HEAD
    nblk = S // tm

    def body(z_ref, zp_ref, zn_ref, dy_ref, dyn_ref, ws_ref, bsb_ref, gv_ref, bv_ref, cw_ref,
             dz_ref, dws_ref, dbs_ref, dgv_ref, dbv_ref, dcw_ref, dbin_ref):
        i = pl.program_id(0)

        @pl.when(i == 0)
        def _():
            for r in (dws_ref, dbs_ref, dgv_ref, dbv_ref, dcw_ref, dbin_ref):
                r[...] = jnp.zeros_like(r)

        ri = lax.broadcasted_iota(jnp.int32, (HEAD, HEAD), 0)
        ci = lax.broadcasted_iota(jnp.int32, (HEAD, HEAD), 1)
        tril = ri >= ci
        for h in range(H):
            cs = slice(h * HEAD, (h + 1) * HEAD)
            vs = slice(DA + h * HEAD, DA + (h + 1) * HEAD)
            wb = jnp.where(tril, ws_ref[h], 0.0).astype(BF16)
            gvh = gv_ref[:, cs]
            dws_acc = jnp.zeros((HEAD, HEAD), F32)
            dbs_acc = jnp.zeros((HEAD, HEAD), F32)
            dgv_acc = jnp.zeros((1, HEAD), F32)
            dbv_acc = jnp.zeros((1, HEAD), F32)
            dbu_acc = jnp.zeros((1, HEAD), F32)
            dbvv_acc = jnp.zeros((1, HEAD), F32)
            for n in range(nch):
                rs = slice(n * HEAD, (n + 1) * HEAD)
                u, du_dz = _gelu_and_grad(z_ref[rs, cs])
                v, dv_dz = _gelu_and_grad(z_ref[rs, vs])
                vhat, rstd = _ln_stats(v)
                vnb = (vhat * gvh + bv_ref[:, cs]).astype(BF16)
                mixed = jnp.dot(wb, vnb, preferred_element_type=F32) + bsb_ref[h]
                dya = dy_ref[rs, cs]
                dzu = dya * mixed * du_dz
                dmx = dya * u
                dmxb = dmx.astype(BF16)
                dbs_acc += jnp.broadcast_to(jnp.sum(dmx, axis=1, keepdims=True), (HEAD, HEAD))
                dws_acc += lax.dot_general(dmxb, vnb, _DIMS['nt'], preferred_element_type=F32)
                dvn = lax.dot_general(wb, dmxb, _DIMS['tn'], preferred_element_type=F32)
                dgv_acc += jnp.sum(dvn * vhat, axis=0, keepdims=True)
                dbv_acc += jnp.sum(dvn, axis=0, keepdims=True)
                dxh = dvn * gvh
                m1 = jnp.mean(dxh, axis=-1, keepdims=True)
                m2 = jnp.mean(dxh * vhat, axis=-1, keepdims=True)
                dzv = rstd * (dxh - m1 - vhat * m2) * dv_dz
                dz_ref[rs, cs] = dzu.astype(BF16)
                dz_ref[rs, vs] = dzv.astype(BF16)
                dbu_acc += jnp.sum(dzu, axis=0, keepdims=True)
                dbvv_acc += jnp.sum(dzv, axis=0, keepdims=True)
            dws_ref[h] += jnp.where(tril, dws_acc, 0.0)
            dbs_ref[h] += dbs_acc
            dgv_ref[:, cs] += dgv_acc
            dbv_ref[:, cs] += dbv_acc
            dbin_ref[:, cs] += dbu_acc
            dbin_ref[:, vs] += dbvv_acc

        zb = z_ref[:, o:o + DB]
        zc = z_ref[:, o + DB:o + 2 * DB]
        zx = z_ref[:, o + 2 * DB:]
        hc = zc * zx
        keep = (i > 0).astype(F32)
        p1 = zp_ref[7:8, o + DB:o + 2 * DB] * zp_ref[7:8, o + 2 * DB:] * keep
        p2 = zp_ref[6:7, o + DB:o + 2 * DB] * zp_ref[6:7, o + 2 * DB:] * keep
        rowi = lax.broadcasted_iota(jnp.int32, (tm, DB), 0)
        r1, r2 = _conv_shifted(hc, p1, p2, rowi)
        w0, w1, w2 = cw_ref[0:1, :], cw_ref[1:2, :], cw_ref[2:3, :]
        cv = w0 * r2 + w1 * r1 + w2 * hc
        dyb = dy_ref[:, DA:]
        dzb = dyb * cv
        dcv = dyb * zb
        more = (i < nblk - 1).astype(F32)
        n0 = dyn_ref[0:1, DA:] * zn_ref[0:1, o:o + DB] * more
        n1 = dyn_ref[1:2, DA:] * zn_ref[1:2, o:o + DB] * more
        f1 = jnp.where(rowi == tm - 1, n0, pltpu.roll(dcv, tm - 1, 0))
        f2 = jnp.where(rowi == tm - 1, n1, jnp.where(rowi == tm - 2, n0, pltpu.roll(dcv, tm - 2, 0)))
        dhc = w2 * dcv + w1 * f1 + w0 * f2
        dzc = dhc * zx
        dzx = dhc * zc
        dcw_ref[0:1, :] += jnp.sum(dcv * r2, axis=0, keepdims=True)
        dcw_ref[1:2, :] += jnp.sum(dcv * r1, axis=0, keepdims=True)
        dcw_ref[2:3, :] += jnp.sum(dcv * hc, axis=0, keepdims=True)
        dz_ref[:, o:o + DB] = dzb.astype(BF16)
        dz_ref[:, o + DB:o + 2 * DB] = dzc.astype(BF16)
        dz_ref[:, o + 2 * DB:] = dzx.astype(BF16)
        dbin_ref[:, o:o + DB] += jnp.sum(dzb, axis=0, keepdims=True)
        dbin_ref[:, o + DB:o + 2 * DB] += jnp.sum(dzc, axis=0, keepdims=True)
        dbin_ref[:, o + 2 * DB:] += jnp.sum(dzx, axis=0, keepdims=True)

    full3 = lambda i: (0, 0, 0)
    full2 = lambda i: (0, 0)
    prev8 = lambda i: (jnp.maximum(i * (tm // 8) - 1, 0), 0)
    next8 = lambda i: (jnp.minimum((i + 1) * (tm // 8), S // 8 - 1), 0)
    est = 2 * (tm * DIN * 4 + tm * (DA + DB) * 4 + tm * DIN * 2) + 16 * tm * DB * 4
    return pl.pallas_call(
        body, name="mix_bwd", grid=(nblk,),
        in_specs=[pl.BlockSpec((tm, DIN), lambda i: (i, 0)),
                  pl.BlockSpec((8, DIN), prev8), pl.BlockSpec((8, DIN), next8),
                  pl.BlockSpec((tm, DA + DB), lambda i: (i, 0)), pl.BlockSpec((8, DA + DB), next8),
                  pl.BlockSpec(ws.shape, full3), pl.BlockSpec(bsb.shape, full3),
                  pl.BlockSpec(gv.shape, full2), pl.BlockSpec(bv.shape, full2), pl.BlockSpec(cw8.shape, full2)],
        out_specs=[pl.BlockSpec((tm, DIN), lambda i: (i, 0)),
                   pl.BlockSpec(ws.shape, full3), pl.BlockSpec(ws.shape, full3),
                   pl.BlockSpec(gv.shape, full2), pl.BlockSpec(gv.shape, full2),
                   pl.BlockSpec(cw8.shape, full2), pl.BlockSpec((1, DIN), full2)],
        out_shape=[jax.ShapeDtypeStruct((S, DIN), BF16),
                   jax.ShapeDtypeStruct(ws.shape, F32), jax.ShapeDtypeStruct(ws.shape, F32),
                   jax.ShapeDtypeStruct(gv.shape, F32), jax.ShapeDtypeStruct(gv.shape, F32),
                   jax.ShapeDtypeStruct(cw8.shape, F32), jax.ShapeDtypeStruct((1, DIN), F32)],
        compiler_params=pltpu.CompilerParams(dimension_semantics=("arbitrary",),
                                             vmem_limit_bytes=_vmem_limit(est)),
    )(z, z, z, dy, dy, ws, bsb, gv, bv, cw8)


def _place():
    x, y, c = lax.axis_index("x"), lax.axis_index("y"), lax.axis_index("c")
    others = [(1 - x, y), (x, 1 - y), (1 - x, 1 - y)]
    return x, y, c, 2 * x + y, others, [2 * ox + oy for ox, oy in others]


def _half(c, rows_half):
    return pl.ds(pl.multiple_of(c * rows_half, 16), rows_half)


def _gather_weights(shards, conv8):
    n = len(shards)

    def body(*refs):
        sh = refs[:n]
        cv = refs[n]
        full = refs[n + 1:2 * n + 1]
        cvfull = refs[2 * n + 1]
        send, recv, fsend, frecv, lsem, csend, crecv = refs[2 * n + 2:]
        x, y, c, chip, others, ochips = _place()
        me, sib = (x, y, c), (x, y, 1 - c)

        def rcopy(src, dst, ssem, rsem, dev):
            return pltpu.make_async_remote_copy(src_ref=src, dst_ref=dst, send_sem=ssem, recv_sem=rsem,
                                                device_id=dev, device_id_type=MESH)

        local = [pltpu.make_async_copy(sh[m], full[m].at[chip], lsem.at[m]) for m in range(n)]
        local.append(pltpu.make_async_copy(cv, cvfull.at[chip], lsem.at[n]))
        for cp in local:
            cp.start()
        sends = []
        for m in range(n):
            ah = sh[m].shape[0] // 2
            for k in range(3):
                sends.append(rcopy(sh[m].at[_half(c, ah)], full[m].at[chip, _half(c, ah)],
                                   send.at[3 * m + k], recv.at[3 * m + k], (*others[k], c)))
        for k in range(3):
            sends.append(rcopy(cv, cvfull.at[chip], csend.at[k], crecv.at[k], (*others[k], c)))
        for cp in sends:
            cp.start()
        fwd = []
        for k in range(3):
            for m in range(n):
                ah = sh[m].shape[0] // 2
                piece = full[m].at[ochips[k], _half(c, ah)]
                rcopy(piece, piece, send.at[3 * m + k], recv.at[3 * m + k], me).wait_recv()
                f = rcopy(piece, piece, fsend.at[3 * m + k], frecv.at[3 * m + k], sib)
                f.start()
                fwd.append(f)
        for k in range(3):
            for m in range(n):
                ah = sh[m].shape[0] // 2
                piece = full[m].at[ochips[k], _half(1 - c, ah)]
                rcopy(piece, piece, fsend.at[3 * m + k], frecv.at[3 * m + k], me).wait_recv()
            rcopy(cv, cvfull.at[ochips[k]], csend.at[k], crecv.at[k], me).wait_recv()
        for cp in sends + fwd:
            cp.wait_send()
        for cp in local:
            cp.wait()

    out_shape = [jax.ShapeDtypeStruct((N_CHIPS,) + s.shape, s.dtype) for s in shards]
    out_shape.append(jax.ShapeDtypeStruct((N_CHIPS,) + conv8.shape, conv8.dtype))
    outs = pl.pallas_call(
        body, name="gather_weights",
        in_specs=[ANY] * (n + 1), out_specs=[ANY] * (n + 1), out_shape=out_shape,
        scratch_shapes=[pltpu.SemaphoreType.DMA((3 * n,))] * 4
        + [pltpu.SemaphoreType.DMA((n + 1,)), pltpu.SemaphoreType.DMA((3,)), pltpu.SemaphoreType.DMA((3,))],
    )(*shards, conv8)
    return outs[:n], outs[n]


def _rs_core_pair(parts):
    n = len(parts)

    def body(*refs):
        src = refs[:n]
        dst = refs[n:2 * n]
        ssem, rsem = refs[2 * n:]
        x, y, c, _, _, _ = _place()
        cps = []
        for m in range(n):
            ah = src[m].shape[1] // 2
            cps.append(pltpu.make_async_remote_copy(
                src_ref=src[m].at[:, _half(1 - c, ah), :], dst_ref=dst[m], send_sem=ssem.at[m], recv_sem=rsem.at[m],
                device_id=(x, y, 1 - c), device_id_type=MESH))
        for cp in cps:
            cp.start()
        for cp in cps:
            cp.wait()

    return pl.pallas_call(
        body, name="rs_core_pair", in_specs=[ANY] * n, out_specs=[ANY] * n,
        out_shape=[jax.ShapeDtypeStruct((p.shape[0], p.shape[1] // 2, p.shape[2]), p.dtype) for p in parts],
        scratch_shapes=[pltpu.SemaphoreType.DMA((n,)), pltpu.SemaphoreType.DMA((n,))],
    )(*parts)


def _rs_chips(psums):
    n = len(psums)

    def body(*refs):
        src = refs[:n]
        dst = refs[n:2 * n]
        ssem, rsem = refs[2 * n:]
        x, y, c, _, others, ochips = _place()
        cps = []
        for m in range(n):
            for k in range(3):
                cps.append(pltpu.make_async_remote_copy(
                    src_ref=src[m].at[ochips[k]], dst_ref=dst[m].at[k], send_sem=ssem.at[3 * m + k],
                    recv_sem=rsem.at[3 * m + k], device_id=(*others[k], c), device_id_type=MESH))
        for cp in cps:
            cp.start()
        for cp in cps:
            cp.wait()

    return pl.pallas_call(
        body, name="rs_chips", in_specs=[ANY] * n, out_specs=[ANY] * n,
        out_shape=[jax.ShapeDtypeStruct((3,) + p.shape[1:], p.dtype) for p in psums],
        scratch_shapes=[pltpu.SemaphoreType.DMA((3 * n,)), pltpu.SemaphoreType.DMA((3 * n,))],
    )(*psums)


def _exchange_halves(halves):
    n = len(halves)

    def body(*refs):
        src = refs[:n]
        dst = refs[n:2 * n]
        ssem, rsem, lsem = refs[2 * n:]
        x, y, c, _, _, _ = _place()
        cps, loc = [], []
        for m in range(n):
            ah = src[m].shape[0]
            mine = dst[m].at[_half(c, ah)]
            loc.append(pltpu.make_async_copy(src[m], mine, lsem.at[m]))
            cps.append(pltpu.make_async_remote_copy(
                src_ref=src[m], dst_ref=mine, send_sem=ssem.at[m], recv_sem=rsem.at[m],
                device_id=(x, y, 1 - c), device_id_type=MESH))
        for cp in loc + cps:
            cp.start()
        for m in range(n):
            ah = src[m].shape[0]
            theirs = dst[m].at[_half(1 - c, ah)]
            pltpu.make_async_remote_copy(src_ref=src[m], dst_ref=theirs, send_sem=ssem.at[m], recv_sem=rsem.at[m],
                                         device_id=(x, y, c), device_id_type=MESH).wait_recv()
        for cp in cps:
            cp.wait_send()
        for cp in loc:
            cp.wait()

    return pl.pallas_call(
        body, name="exchange_halves", in_specs=[ANY] * n, out_specs=[ANY] * n,
        out_shape=[jax.ShapeDtypeStruct((2 * h.shape[0], h.shape[1]), h.dtype) for h in halves],
        scratch_shapes=[pltpu.SemaphoreType.DMA((n,))] * 3,
    )(*halves)


def _allreduce_small(v):
    m_per, n = v.shape

    def body(x_ref, all_ref, sum_ref, send_sems, recv_sems, local_sem):
        x, y, c, _, chips, _ = _place()
        me, sibling = (x, y, c), (x, y, 1 - c)

        def rows(px, py, pc):
            return all_ref.at[pl.ds(pl.multiple_of((4 * px + 2 * py + pc) * m_per, 8), m_per), :]

        def copy(k, block, to, src=None):
            return pltpu.make_async_remote_copy(
                src_ref=rows(*block) if src is None else src, dst_ref=rows(*block),
                send_sem=send_sems.at[k], recv_sem=recv_sems.at[k], device_id=to, device_id_type=MESH)

        mine = pltpu.make_async_copy(x_ref, rows(*me), local_sem)
        mine.start()
        first = [copy(0, me, sibling, src=x_ref)]
        first += [copy(1 + j, me, (*chip, c), src=x_ref) for j, chip in enumerate(chips)]
        for cp in first:
            cp.start()
        passed = [copy(4 + j, (*chip, c), sibling) for j, chip in enumerate(chips)]
        for j, chip in enumerate(chips):
            copy(1 + j, (*chip, c), me).wait_recv()
            passed[j].start()
        copy(0, sibling, me).wait_recv()
        for j, chip in enumerate(chips):
            copy(4 + j, (*chip, 1 - c), me).wait_recv()
        for cp in first + passed:
            cp.wait_send()
        mine.wait()
        acc = all_ref[0:m_per, :]
        for d in range(1, 8):
            acc = acc + all_ref[d * m_per:(d + 1) * m_per, :]
        sum_ref[...] = acc

    vm = pl.BlockSpec(memory_space=pltpu.VMEM)
    return pl.pallas_call(
        body, name="allreduce_small",
        out_shape=[jax.ShapeDtypeStruct((8 * m_per, n), v.dtype), jax.ShapeDtypeStruct((m_per, n), v.dtype)],
        in_specs=[vm], out_specs=[vm, vm],
        scratch_shapes=[pltpu.SemaphoreType.DMA((7,)), pltpu.SemaphoreType.DMA((7,)), pltpu.SemaphoreType.DMA],
    )(v)[1]


def _sum_pair(name, part, sib, c_idx):
    _, A, B = part.shape
    ah = A // 2
    br = _rows(ah, B * 2, 2 * 2 ** 20)

    def body(c_ref, a_ref, b_ref, o_ref):
        o_ref[...] = (a_ref[...].astype(F32) + b_ref[...].astype(F32)).astype(BF16)

    gs = pltpu.PrefetchScalarGridSpec(
        num_scalar_prefetch=1, grid=(N_CHIPS, ah // br),
        in_specs=[pl.BlockSpec((None, None, br, B), lambda j, i, c: (j, c[0], i, 0)),
                  pl.BlockSpec((None, br, B), lambda j, i, c: (j, i, 0))],
        out_specs=pl.BlockSpec((None, br, B), lambda j, i, c: (j, i, 0)))
    return pl.pallas_call(body, name=name, grid_spec=gs,
                          out_shape=jax.ShapeDtypeStruct((N_CHIPS, ah, B), BF16))(
                              c_idx, part.reshape(N_CHIPS, 2, ah, B), sib)


def _sum_chips(name, psum, recv, chip_idx):
    _, ah, B = psum.shape
    br = _rows(ah, B * 4, 2 * 2 ** 20)

    def body(c_ref, p_ref, r0_ref, r1_ref, r2_ref, o_ref):
        o_ref[...] = ((p_ref[...].astype(F32) + r0_ref[...].astype(F32)) + r1_ref[...].astype(F32)) \
            + r2_ref[...].astype(F32)

    gs = pltpu.PrefetchScalarGridSpec(
        num_scalar_prefetch=1, grid=(ah // br,),
        in_specs=[pl.BlockSpec((None, br, B), lambda i, c: (c[0], i, 0))]
        + [pl.BlockSpec((None, br, B), functools.partial(lambda i, c, k: (k, i, 0), k=k)) for k in range(3)],
        out_specs=pl.BlockSpec((br, B), lambda i, c: (i, 0)))
    return pl.pallas_call(body, name=name, grid_spec=gs,
                          out_shape=jax.ShapeDtypeStruct((ah, B), F32))(chip_idx, psum, recv, recv, recv)


def _adamw_small(gsum, gconv, offsets, ws, ms, vs):
    n = len(ws)

    def body(*refs):
        g_ref, gc_ref = refs[0], refs[1]
        w_refs = refs[2:2 + n]
        m_refs = refs[2 + n:2 + 2 * n]
        v_refs = refs[2 + 2 * n:2 + 3 * n]
        outs = refs[2 + 3 * n:]
        for i in range(n):
            rows = w_refs[i].shape[0]
            g = gc_ref[...] if offsets[i] is None else g_ref[offsets[i]:offsets[i] + rows, :]
            d, m2, v2 = _adamw_math(w_refs[i][...], g, m_refs[i][...], v_refs[i][...])
            outs[4 * i][...] = g
            outs[4 * i + 1][...] = d
            outs[4 * i + 2][...] = m2
            outs[4 * i + 3][...] = v2

    vm = pl.BlockSpec(memory_space=pltpu.VMEM)
    out_shape = []
    for w in ws:
        out_shape += [jax.ShapeDtypeStruct(w.shape, F32)] * 4
    return pl.pallas_call(body, name="adamw_small", in_specs=[vm] * (2 + 3 * n), out_specs=[vm] * (4 * n),
                          out_shape=out_shape)(gsum, gconv, *ws, *ms, *vs)


def _ffn_fwd(tag, hb, hres, wg4, wu4, wd):
    S, D = hb.shape
    cs = wg4.shape[2]
    F = wd.shape[0]

    def epi_up(accs, tes, res):
        g, u = accs
        a = g * jax.nn.sigmoid(g) * u
        return [g, u, a]

    gb, ub, ab = _mm(f"{tag}_up", 'nn', [hb], [wg4, wu4], [(0, 0, 0), (0, 1, 1)], 2, epi_up, [BF16] * 3,
                     bm=_blk(S, 256), bn=cs, bk=D, b_blocked=True, n_outer=True)

    def epi_down(accs, tes, res):
        return [ALPHA * tes[0] + 0.5 * accs[0]]

    r = _mm(f"{tag}_down", 'nn', [ab], [wd], [(0, 0, 0)], 1, epi_down, [F32],
            bm=_blk(S, 1024), bn=_blk(D, 512), bk=F, tile_extras=[hres])[0]
    return gb, ub, ab, r


def _ffn_bwd(tag, hb, gb, ub, ab, dsub, dres, wg4, wu4, wd):
    S, D = hb.shape
    cs = wg4.shape[2]
    F = wd.shape[0]

    def epi_da(accs, tes, res):
        da = accs[0]
        g = tes[0].astype(F32)
        u = tes[1].astype(F32)
        s = jax.nn.sigmoid(g)
        return [da * u * (s * (1.0 + g * (1.0 - s))), da * (g * s)]

    dgb, dub = _mm(f"{tag}_da", 'nt', [dsub], [wd], [(0, 0, 0)], 1, epi_da, [BF16, BF16],
                   bm=_blk(S, 1024), bn=_blk(F, 512), bk=D, tile_extras=[gb, ub])
    ident = lambda accs, tes, res: [accs[0]]
    dwd = _mm(f"{tag}_dwd", 'tn', [ab], [dsub], [(0, 0, 0)], 1, ident, [BF16],
              bm=cs, bn=_blk(D, 1024), bk=_blk(S, 1024))[0]
    dwg = _mm(f"{tag}_dwg", 'tn', [hb], [dgb], [(0, 0, 0)], 1, ident, [BF16],
              bm=_blk(D, 1024), bn=cs, bk=_blk(S, 1024), out_blocked=True)[0]
    dwu = _mm(f"{tag}_dwu", 'tn', [hb], [dub], [(0, 0, 0)], 1, ident, [BF16],
              bm=_blk(D, 1024), bn=cs, bk=_blk(S, 1024), out_blocked=True)[0]

    def epi_dh(accs, tes, res):
        return [accs[0] + tes[0]]

    dh = _mm(f"{tag}_dh", 'nt', [dgb, dub], [wg4, wu4], [(0, 0, 0), (1, 1, 0)], 1, epi_dh, [F32],
             bm=_blk(S, 1024), bn=_blk(D, 1024), bk=cs, tile_extras=[dres], b_blocked=True)[0]
    return dh, dwg, dwu, dwd.reshape(N_CHIPS, cs, D)


def _step(p):
    x = p['x'][0]
    target = p['loss_target'][0]
    S, D = x.shape
    cx, cy, cc = lax.axis_index("x"), lax.axis_index("y"), lax.axis_index("c")
    c_idx = jnp.reshape(cc, (1,)).astype(jnp.int32)
    chip_idx = jnp.reshape(2 * cx + cy, (1,)).astype(jnp.int32)

    w_s = p['w_s'][0]
    H = w_s.shape[0]
    DA = H * HEAD
    bsb = jnp.broadcast_to(p['b_s'][0][:, :, None], (H, HEAD, HEAD))
    conv_shard = p['conv_w'][0]
    conv8 = jnp.zeros((8, conv_shard.shape[1]), F32).at[:CONV_TAPS].set(conv_shard)

    shards_b = [_cast_bf16(f"cast_{n}", p[n][0]) for n in BIG]
    fulls, convfull = _gather_weights(shards_b, conv8)
    W = dict(zip(BIG, fulls))
    cw8 = jnp.transpose(convfull, (1, 0, 2)).reshape(8, DA)
    F = W['ffa_down'].shape[0] * W['ffa_down'].shape[1]
    wd_a = W['ffa_down'].reshape(F, D)
    wd_c = W['ffc_down'].reshape(F, D)
    w_out = W['w_out'].reshape(DA * 2, D)
    w_in4 = W['w_in']
    cs_in = w_in4.shape[2]
    DIN = N_CHIPS * cs_in

    xb = _cast_bf16("cast_x", x)
    ga, ua, aa, r1 = _ffn_fwd("ffa", xb, x, W['ffa_gate'], W['ffa_up'], wd_a)
    h1, h1b = _ln_fwd("ln_a", r1, p['ln_a_g'], p['ln_a_b'])

    z = _mm("w_in", 'nn', [h1b], [w_in4], [(0, 0, 0)], 1, lambda accs, tes, res: [accs[0] + res[0]], [F32],
            bm=_blk(S, 512), bn=cs_in, bk=D, row_extras=[p['b_in']], b_blocked=True, n_outer=True)[0]
    tm = _blk(S, 256)
    yb = _mix_fwd(z, w_s, bsb, p['ln_v_g'], p['ln_v_b'], cw8, tm)
    r2 = _mm("w_out", 'nn', [yb], [w_out], [(0, 0, 0)], 1,
             lambda accs, tes, res: [accs[0] + res[0] + ALPHA * tes[0]], [F32],
             bm=_blk(S, 1024), bn=_blk(D, 1024), bk=2 * DA, tile_extras=[h1], row_extras=[p['b_out']])[0]
    h2, h2b = _ln_fwd("ln_m", r2, p['ln_m_g'], p['ln_m_b'])
    gc, uc, ac, r3 = _ffn_fwd("ffc", h2b, h2, W['ffc_gate'], W['ffc_up'], wd_c)
    dh3, lsum = _ln_loss("ln_c_loss", r3, target, p['ln_c_g'], p['ln_c_b'])
    loss = lax.psum(0.5 * jnp.sum(lsum) / D, ("x", "y", "c"))

    pg, sg = {}, {}
    dres3, dsub3, sg['ln_c_g'], sg['ln_c_b'], _ = _ln_bwd("ln_c_bwd", r3, dh3, p['ln_c_g'], 0.5)
    dh2, pg['ffc_gate'], pg['ffc_up'], pg['ffc_down'] = _ffn_bwd(
        "ffc", h2b, gc, uc, ac, dsub3, dres3, W['ffc_gate'], W['ffc_up'], wd_c)
    dres2, dmix, sg['ln_m_g'], sg['ln_m_b'], sg['b_out'] = _ln_bwd("ln_m_bwd", r2, dh2, p['ln_m_g'], 1.0)
    ident = lambda accs, tes, res: [accs[0]]
    pg['w_out'] = _mm("dw_out", 'tn', [yb], [dmix], [(0, 0, 0)], 1, ident, [BF16],
                      bm=_blk(2 * DA, 1024), bn=_blk(D, 1024), bk=_blk(S, 2048))[0].reshape(N_CHIPS, 2 * DA // 4, D)
    dy = _mm("dy", 'nt', [dmix], [w_out], [(0, 0, 0)], 1, ident, [F32],
             bm=_blk(S, 1024), bn=_blk(2 * DA, 1024), bk=D)[0]
    dz, dws, dbs, sg['ln_v_g'], sg['ln_v_b'], dcw, sg['b_in'] = _mix_bwd(
        z, dy, w_s, bsb, p['ln_v_g'], p['ln_v_b'], cw8, tm)
    sg['w_s'] = dws
    sg['b_s'] = dbs[:, :, 0]
    pg['w_in'] = _mm("dw_in", 'tn', [h1b], [dz], [(0, 0, 0)], 1, ident, [BF16],
                     bm=_blk(D, 1024), bn=cs_in, bk=_blk(S, 1024), out_blocked=True)[0]
    dh1 = _mm("dh1", 'nt', [dz], [w_in4], [(0, 0, 0)], 1, lambda accs, tes, res: [accs[0] + tes[0]], [F32],
              bm=_blk(S, 1024), bn=_blk(D, 1024), bk=cs_in, tile_extras=[dres2], b_blocked=True)[0]
    dres1, dsub1, sg['ln_a_g'], sg['ln_a_b'], _ = _ln_bwd("ln_a_bwd", r1, dh1, p['ln_a_g'], 0.5)
    dx, pg['ffa_gate'], pg['ffa_up'], pg['ffa_down'] = _ffn_bwd(
        "ffa", xb, ga, ua, aa, dsub1, dres1, W['ffa_gate'], W['ffa_up'], wd_a)

    parts = [pg[n] for n in BIG]
    from_sib = _rs_core_pair(parts)
    psums = [_sum_pair(f"sum_pair_{n}", a, b, c_idx) for n, a, b in zip(BIG, parts, from_sib)]
    from_chips = _rs_chips(psums)
    halves = [_sum_chips(f"sum_chips_{n}", a, b, chip_idx) for n, a, b in zip(BIG, psums, from_chips)]
    grads_big = _exchange_halves(halves)
    out = {}
    for n, g in zip(BIG, grads_big):
        d, m2, v2 = _adamw(f"adamw_{n}", p[n][0], g, p['m_' + n][0], p['v_' + n][0])
        shp = p[n].shape
        out[n] = (g.reshape(shp), d.reshape(shp), m2.reshape(shp), v2.reshape(shp))

    def as_rows(a):
        a = a.reshape(-1, 128)
        pad = (-a.shape[0]) % 8
        return jnp.pad(a, ((0, pad), (0, 0))) if pad else a

    pieces, offsets, off = [], {}, 0
    for n in SMALL + ['conv_w']:
        piece = as_rows(dcw[:CONV_TAPS] if n == 'conv_w' else sg[n])
        offsets[n] = off
        off += piece.shape[0]
        pieces.append(piece)
    gsum = _allreduce_small(jnp.concatenate(pieces, axis=0))
    n_conv = CONV_TAPS * DA // 128
    conv_sum = gsum[offsets['conv_w']:offsets['conv_w'] + n_conv].reshape(CONV_TAPS, DA)
    cw = conv_shard.shape[1]
    gconv = lax.dynamic_slice_in_dim(conv_sum, (2 * cx + cy) * cw, cw, axis=1).reshape(-1, 128)
    names = SMALL + ['conv_w']
    rows128 = lambda a: a.reshape(-1, 128)
    res = _adamw_small(gsum, gconv, [offsets[n] if n != 'conv_w' else None for n in names],
                       [rows128(p[n]) for n in names], [rows128(p['m_' + n]) for n in names],
                       [rows128(p['v_' + n]) for n in names])
    for i, n in enumerate(names):
        out[n] = tuple(r.reshape(p[n].shape) for r in res[4 * i:4 * i + 4])

    return (loss, dx.reshape(p['x'].shape), *[out[n][0] for n in WEIGHTS], *[out[n][1] for n in WEIGHTS],
            *[out[n][2] for n in WEIGHTS], *[out[n][3] for n in WEIGHTS])


def kernel(x, ffa_gate, ffa_up, ffa_down, ln_a_g, ln_a_b, w_in, b_in, w_s, b_s, ln_v_g, ln_v_b, conv_w, w_out, b_out, ln_m_g, ln_m_b, ffc_gate, ffc_up, ffc_down, ln_c_g, ln_c_b, loss_target, m_ffa_gate, m_ffa_up, m_ffa_down, m_ln_a_g, m_ln_a_b, m_w_in, m_b_in, m_w_s, m_b_s, m_ln_v_g, m_ln_v_b, m_conv_w, m_w_out, m_b_out, m_ln_m_g, m_ln_m_b, m_ffc_gate, m_ffc_up, m_ffc_down, m_ln_c_g, m_ln_c_b, v_ffa_gate, v_ffa_up, v_ffa_down, v_ln_a_g, v_ln_a_b, v_w_in, v_b_in, v_w_s, v_b_s, v_ln_v_g, v_ln_v_b, v_conv_w, v_w_out, v_b_out, v_ln_m_g, v_ln_m_b, v_ffc_gate, v_ffc_up, v_ffc_down, v_ln_c_g, v_ln_c_b):
    return _step(dict(locals()))
```

```python
import functools
import math

import jax
import jax.numpy as jnp
from jax import lax
from jax.experimental import pallas as pl
from jax.experimental.pallas import tpu as pltpu

F32 = jnp.float32
BF16 = jnp.bfloat16
MESH = pl.DeviceIdType.MESH
ANY = pl.BlockSpec(memory_space=pl.ANY)

HEAD = 128
CONV_TAPS = 3
LN_EPS = 1e-5
ALPHA = float(2.0 ** 0.25)
ADAM_LR, ADAM_B1, ADAM_B2, ADAM_EPS, ADAM_WD, ADAM_STEP = 0.001, 0.9, 0.999, 1e-08, 0.01, 10

V7X_VMEM_BYTES = 64 * 2 ** 20
N_CHIPS = 4

BIG = ['ffa_gate', 'ffa_up', 'ffa_down', 'w_in', 'w_out', 'ffc_gate', 'ffc_up', 'ffc_down']
WEIGHTS = ['ffa_gate', 'ffa_up', 'ffa_down', 'ln_a_g', 'ln_a_b', 'w_in', 'b_in', 'w_s', 'b_s', 'ln_v_g', 'ln_v_b',
           'conv_w', 'w_out', 'b_out', 'ln_m_g', 'ln_m_b', 'ffc_gate', 'ffc_up', 'ffc_down', 'ln_c_g', 'ln_c_b']
SMALL = [n for n in WEIGHTS if n not in BIG and n != 'conv_w']


def _vmem_limit(estimate_bytes):
    return int(min(max(estimate_bytes * 1.3, 32 * 2 ** 20), V7X_VMEM_BYTES - 6 * 2 ** 20))


def _blk(dim, pref, mult=128):
    if dim <= pref:
        return dim
    best = None
    for d in range(mult, pref + 1, mult):
        if dim % d == 0:
            best = d
    assert best is not None, (dim, pref)
    return best


def _rows(n_rows, row_bytes, target=4 * 2 ** 20):
    return _blk(n_rows, max(16, target // row_bytes), 16)


def _gelu(x):
    c = math.sqrt(2.0 / math.pi)
    return 0.5 * x * (1.0 + jnp.tanh(c * (x + 0.044715 * (x * x * x))))


def _gelu_and_grad(x):
    c = math.sqrt(2.0 / math.pi)
    t = jnp.tanh(c * (x + 0.044715 * (x * x * x)))
    g = 0.5 * x * (1.0 + t)
    dg = 0.5 * (1.0 + t) + 0.5 * x * (1.0 - t * t) * (c * (1.0 + 3.0 * 0.044715 * (x * x)))
    return g, dg


def _ln_stats(r):
    mu = jnp.mean(r, axis=-1, keepdims=True)
    d = r - mu
    var = jnp.mean(d * d, axis=-1, keepdims=True)
    rstd = lax.rsqrt(var + LN_EPS)
    return d * rstd, rstd


_DIMS = {'nn': (((1,), (0,)), ((), ())), 'nt': (((1,), (1,)), ((), ())), 'tn': (((0,), (0,)), ((), ()))}


def _mm(name, mode, a_list, b_list, pairs, n_acc, epilogue, out_dtypes, bm, bn, bk,
        tile_extras=(), row_extras=(), b_blocked=False, out_blocked=False, n_outer=False, comms=()):
    a0, b0 = a_list[0], b_list[0]
    if mode == 'nn':
        M, K = a0.shape
        N = b0.shape[0] * b0.shape[2] if b_blocked else b0.shape[1]
    elif mode == 'nt':
        M, K = a0.shape
        N = b0.shape[1] if b_blocked else b0.shape[0]
    else:
        K, M = a0.shape
        N = b0.shape[1]
        assert not b_blocked
    assert M % bm == 0 and N % bn == 0 and K % bk == 0, (name, M, N, K, bm, bn, bk)
    gm, gn, gk = M // bm, N // bn, K // bk
    if b_blocked:
        assert (bn if mode == 'nn' else bk) == b0.shape[2], name
    dims = _DIMS[mode]

    def spec(shape, f):
        if n_outer:
            return pl.BlockSpec(shape, lambda g0, g1, g2: f(g1, g0, g2))
        return pl.BlockSpec(shape, f)

    if mode == 'tn':
        a_spec = spec((bk, bm), lambda i, j, k: (k, i))
    else:
        a_spec = spec((bm, bk), lambda i, j, k: (i, k))
    if mode == 'nn':
        b_spec = (spec((None, bk, bn), lambda i, j, k: (j, k, 0)) if b_blocked
                  else spec((bk, bn), lambda i, j, k: (k, j)))
    elif mode == 'nt':
        b_spec = (spec((None, bn, bk), lambda i, j, k: (k, j, 0)) if b_blocked
                  else spec((bn, bk), lambda i, j, k: (j, k)))
    else:
        b_spec = spec((bk, bn), lambda i, j, k: (k, j))
    te_spec = spec((bm, bn), lambda i, j, k: (i, j))
    re_spec = spec((1, bn), lambda i, j, k: (0, j))
    if out_blocked:
        o_spec = spec((None, bm, bn), lambda i, j, k: (j, i, 0))
        o_shape = (gn, M, bn)
    else:
        o_spec = te_spec
        o_shape = (M, N)

    na, nb, nte, nre, no = len(a_list), len(b_list), len(tile_extras), len(row_extras), len(out_dtypes)

    n_scratch_acc = n_acc if gk > 1 else 0
    grid = (gn, gm, gk) if n_outer else (gm, gn, gk)

    def body(*refs):
        a_refs = refs[:na]
        b_refs = refs[na:na + nb]
        te_refs = refs[na + nb:na + nb + nte]
        re_refs = refs[na + nb + nte:na + nb + nte + nre]
        pos = na + nb + nte + nre
        cm_ins = []
        for cm in comms:
            pos += len(cm.aliased)
            cm_ins.append(refs[pos:pos + len(cm.ins)])
            pos += len(cm.ins)
        o_refs = refs[pos:pos + no]
        pos += no
        cm_al, cm_fr = [], []
        for cm in comms:
            cm_al.append(refs[pos:pos + len(cm.aliased)])
            pos += len(cm.aliased)
            cm_fr.append(refs[pos:pos + len(cm.fresh)])
            pos += len(cm.fresh)
        acc_refs = refs[pos:pos + n_scratch_acc]
        pos += n_scratch_acc
        cm_sems = [(refs[pos + 2 * i], refs[pos + 2 * i + 1]) for i in range(len(comms))]

        if comms:
            pids = [pl.program_id(d) for d in range(3)]

            @pl.when((pids[0] == 0) & (pids[1] == 0) & (pids[2] == 0))
            def _():
                for i, cm in enumerate(comms):
                    for d in cm.stages[0](cm_al[i], cm_ins[i], cm_fr[i], *cm_sems[i], False):
                        d.start()

        def finish(accs):
            outs = epilogue(accs, [r[...] for r in te_refs], [r[...] for r in re_refs])
            for o_ref, o in zip(o_refs, outs):
                o_ref[...] = o.astype(o_ref.dtype)

        if gk == 1:
            parts = [None] * n_acc
            for ai, bi, ci in pairs:
                d = lax.dot_general(a_refs[ai][...], b_refs[bi][...], dims, preferred_element_type=F32)
                parts[ci] = d if parts[ci] is None else parts[ci] + d
            finish(parts)
        else:
            kk = pl.program_id(2)

            @pl.when(kk == 0)
            def _():
                for r in acc_refs:
                    r[...] = jnp.zeros_like(r)

            for ai, bi, ci in pairs:
                acc_refs[ci][...] += lax.dot_general(a_refs[ai][...], b_refs[bi][...], dims,
                                                     preferred_element_type=F32)

            @pl.when(kk == gk - 1)
            def _():
                finish([r[...] for r in acc_refs])

        if comms:
            @pl.when((pids[0] == grid[0] - 1) & (pids[1] == grid[1] - 1) & (pids[2] == grid[2] - 1))
            def _():
                for i, cm in enumerate(comms):
                    for d in cm.stages[0](cm_al[i], cm_ins[i], cm_fr[i], *cm_sems[i], True):
                        d.wait()

    isz = lambda a: jnp.dtype(a.dtype).itemsize
    est = 2 * (sum(bm * bk * isz(a) for a in a_list) + sum(bk * bn * isz(b) for b in b_list)
               + sum(bm * bn * isz(t) for t in tile_extras)
               + sum(bm * bn * jnp.dtype(d).itemsize for d in out_dtypes))
    est += (2 * n_acc + 2) * bm * bn * 4
    cm_in_arrays, cm_out_shapes, cm_scratch, aliases = [], [], [], {}
    n_in = na + nb + nte + nre
    for cm in comms:
        assert len(cm.stages) == 1, name
        for a in cm.aliased:
            aliases[n_in + len(cm_in_arrays)] = no + len(cm_out_shapes)
            cm_in_arrays.append(a)
            cm_out_shapes.append(jax.ShapeDtypeStruct(a.shape, a.dtype))
        cm_in_arrays += list(cm.ins)
        cm_out_shapes += list(cm.fresh)
        cm_scratch += [pltpu.SemaphoreType.DMA((cm.n_sems,))] * 2
    outs = pl.pallas_call(
        body, name=name, grid=grid,
        in_specs=[a_spec] * na + [b_spec] * nb + [te_spec] * nte + [re_spec] * nre + [ANY] * len(cm_in_arrays),
        out_specs=[o_spec] * no + [ANY] * len(cm_out_shapes),
        out_shape=[jax.ShapeDtypeStruct(o_shape, d) for d in out_dtypes] + cm_out_shapes,
        scratch_shapes=[pltpu.VMEM((bm, bn), F32)] * n_scratch_acc + cm_scratch,
        input_output_aliases=aliases,
        compiler_params=pltpu.CompilerParams(dimension_semantics=("parallel", "parallel", "arbitrary"),
                                             vmem_limit_bytes=_vmem_limit(est)),
    )(*a_list, *b_list, *tile_extras, *row_extras, *cm_in_arrays)
    _deliver(comms, outs[no:])
    return outs[:no]


def _rowwise(name, fn, br, row_ins, vec_ins, row_outs, acc_outs=()):
    S = row_ins[0].shape[0]
    assert S % br == 0, (name, S, br)
    nr, nv, no, nacc = len(row_ins), len(vec_ins), len(row_outs), len(acc_outs)

    def body(*refs):
        rin = refs[:nr]
        vin = refs[nr:nr + nv]
        rout = refs[nr + nv:nr + nv + no]
        aout = refs[nr + nv + no:]
        if nacc:
            @pl.when(pl.program_id(0) == 0)
            def _():
                for a in aout:
                    a[...] = jnp.zeros_like(a)
        fn(rin, vin, rout, aout)

    def full_spec(shape):
        nd = len(shape)
        return pl.BlockSpec(shape, lambda i: (0,) * nd)

    est = 2 * (sum(br * a.shape[1] * jnp.dtype(a.dtype).itemsize for a in row_ins)
               + sum(br * c * jnp.dtype(d).itemsize for c, d in row_outs))
    est += 6 * br * max(a.shape[1] for a in row_ins) * 4
    outs = pl.pallas_call(
        body, name=name, grid=(S // br,),
        in_specs=[pl.BlockSpec((br, a.shape[1]), lambda i: (i, 0)) for a in row_ins]
        + [full_spec(v.shape) for v in vec_ins],
        out_specs=[pl.BlockSpec((br, c), lambda i: (i, 0)) for c, _ in row_outs]
        + [full_spec(s) for s, _ in acc_outs],
        out_shape=[jax.ShapeDtypeStruct((S, c), d) for c, d in row_outs]
        + [jax.ShapeDtypeStruct(s, d) for s, d in acc_outs],
        compiler_params=pltpu.CompilerParams(dimension_semantics=("arbitrary",),
                                             vmem_limit_bytes=_vmem_limit(est)),
    )(*row_ins, *vec_ins)
    return outs


def _cast_bf16(name, a):
    def fn(rin, vin, rout, aout):
        rout[0][...] = rin[0][...].astype(BF16)
    return _rowwise(name, fn, _rows(a.shape[0], a.shape[1] * 4), [a], [], [(a.shape[1], BF16)])[0]


def _ln_fwd(name, r, g, b):
    def fn(rin, vin, rout, aout):
        xhat, _ = _ln_stats(rin[0][...])
        h = xhat * vin[0][...] + vin[1][...]
        rout[0][...] = h
        rout[1][...] = h.astype(BF16)
    D = r.shape[1]
    return _rowwise(name, fn, _rows(r.shape[0], D * 4, 2 * 2 ** 20), [r], [g, b], [(D, F32), (D, BF16)])


def _ln_loss_bwd(name, r, target, g, b, sub_scale):
    D = r.shape[1]

    def fn(rin, vin, rout, aout):
        xhat, rstd = _ln_stats(rin[0][...])
        gain = vin[0][...]
        err = xhat * gain + vin[1][...] - rin[1][...]
        dhv = err * (1.0 / D)
        dxh = dhv * gain
        m1 = jnp.mean(dxh, axis=-1, keepdims=True)
        m2 = jnp.mean(dxh * xhat, axis=-1, keepdims=True)
        dr = rstd * (dxh - m1 - xhat * m2)
        rout[0][...] = ALPHA * dr
        rout[1][...] = (sub_scale * dr).astype(BF16)
        aout[0][...] += jnp.sum(dhv * xhat, axis=0, keepdims=True)
        aout[1][...] += jnp.sum(dhv, axis=0, keepdims=True)
        aout[2][...] += jnp.sum(err * err, axis=0, keepdims=True)
    return _rowwise(name, fn, _rows(r.shape[0], D * 4, 2 * 2 ** 20), [r, target], [g, b],
                    [(D, F32), (D, BF16)], [((1, D), F32)] * 3)


def _ln_bwd(name, r, dh, g, sub_scale):
    D = r.shape[1]

    def fn(rin, vin, rout, aout):
        xhat, rstd = _ln_stats(rin[0][...])
        dhv = rin[1][...]
        dxh = dhv * vin[0][...]
        m1 = jnp.mean(dxh, axis=-1, keepdims=True)
        m2 = jnp.mean(dxh * xhat, axis=-1, keepdims=True)
        dr = rstd * (dxh - m1 - xhat * m2)
        rout[0][...] = ALPHA * dr
        rout[1][...] = (sub_scale * dr).astype(BF16)
        aout[0][...] += jnp.sum(dhv * xhat, axis=0, keepdims=True)
        aout[1][...] += jnp.sum(dhv, axis=0, keepdims=True)
        aout[2][...] += jnp.sum(dr, axis=0, keepdims=True)
    return _rowwise(name, fn, _rows(r.shape[0], D * 4, 2 * 2 ** 20), [r, dh], [g],
                    [(D, F32), (D, BF16)], [((1, D), F32)] * 3)


def _adamw_math(w, g, m, v):
    m2 = ADAM_B1 * m + (1.0 - ADAM_B1) * g
    v2 = ADAM_B2 * v + (1.0 - ADAM_B2) * (g * g)
    m_hat = m2 / (1.0 - ADAM_B1 ** ADAM_STEP)
    v_hat = v2 / (1.0 - ADAM_B2 ** ADAM_STEP)
    delta = -ADAM_LR * (m_hat / (jnp.sqrt(v_hat) + ADAM_EPS) + ADAM_WD * w)
    return delta, m2, v2


def _adamw(name, w, g, m, v):
    def fn(rin, vin, rout, aout):
        d, m2, v2 = _adamw_math(rin[0][...], rin[1][...], rin[2][...], rin[3][...])
        rout[0][...] = d
        rout[1][...] = m2
        rout[2][...] = v2
    C = w.shape[1]
    return _rowwise(name, fn, _rows(w.shape[0], C * 4, 2 * 2 ** 20), [w, g, m, v], [], [(C, F32)] * 3)


def _conv_shifted(hc, p1, p2, rowi):
    r1 = jnp.where(rowi == 0, p1, pltpu.roll(hc, 1, 0))
    r2 = jnp.where(rowi == 0, p2, jnp.where(rowi == 1, p1, pltpu.roll(hc, 2, 0)))
    return r1, r2


def _mix_fwd(z, ws, bsb, gv, bv, cw8, tm):
    S, DIN = z.shape
    DA = gv.shape[1]
    DB = DA
    H = DA // HEAD
    o = 2 * DA
    nch = tm // HEAD

    def body(z_ref, zp_ref, ws_ref, bsb_ref, gv_ref, bv_ref, cw_ref, y_ref):
        i = pl.program_id(0)
        ri = lax.broadcasted_iota(jnp.int32, (HEAD, HEAD), 0)
        ci = lax.broadcasted_iota(jnp.int32, (HEAD, HEAD), 1)
        tril = ri >= ci
        for h in range(H):
            cs = slice(h * HEAD, (h + 1) * HEAD)
            vs = slice(DA + h * HEAD, DA + (h + 1) * HEAD)
            wb = jnp.where(tril, ws_ref[h], 0.0).astype(BF16)
            for n in range(nch):
                rs = slice(n * HEAD, (n + 1) * HEAD)
                u = _gelu(z_ref[rs, cs])
                vhat, _ = _ln_stats(_gelu(z_ref[rs, vs]))
                vn = vhat * gv_ref[:, cs] + bv_ref[:, cs]
                mixed = jnp.dot(wb, vn.astype(BF16), preferred_element_type=F32) + bsb_ref[h]
                y_ref[rs, cs] = (u * mixed).astype(BF16)
        zb = z_ref[:, o:o + DB]
        hc = z_ref[:, o + DB:o + 2 * DB] * z_ref[:, o + 2 * DB:]
        keep = (i > 0).astype(F32)
        p1 = zp_ref[7:8, o + DB:o + 2 * DB] * zp_ref[7:8, o + 2 * DB:] * keep
        p2 = zp_ref[6:7, o + DB:o + 2 * DB] * zp_ref[6:7, o + 2 * DB:] * keep
        rowi = lax.broadcasted_iota(jnp.int32, (tm, DB), 0)
        r1, r2 = _conv_shifted(hc, p1, p2, rowi)
        cv = cw_ref[0:1, :] * r2 + cw_ref[1:2, :] * r1 + cw_ref[2:3, :] * hc
        y_ref[:, DA:] = (zb * cv).astype(BF16)

    full3 = lambda i: (0, 0, 0)
    full2 = lambda i: (0, 0)
    est = 2 * (tm * DIN * 4 + tm * (DA + DB) * 2) + 10 * tm * DB * 4
    return pl.pallas_call(
        body, name="mix_fwd", grid=(S // tm,),
        in_specs=[pl.BlockSpec((tm, DIN), lambda i: (i, 0)),
                  pl.BlockSpec((8, DIN), lambda i: (jnp.maximum(i * (tm // 8) - 1, 0), 0)),
                  pl.BlockSpec(ws.shape, full3), pl.BlockSpec(bsb.shape, full3),
                  pl.BlockSpec(gv.shape, full2), pl.BlockSpec(bv.shape, full2), pl.BlockSpec(cw8.shape, full2)],
        out_specs=pl.BlockSpec((tm, DA + DB), lambda i: (i, 0)),
        out_shape=jax.ShapeDtypeStruct((S, DA + DB), BF16),
        compiler_params=pltpu.CompilerParams(dimension_semantics=("arbitrary",),
                                             vmem_limit_bytes=_vmem_limit(est)),
    )(z, z, ws, bsb, gv, bv, cw8)


def _mix_bwd(z, dy, ws, bsb, gv, bv, cw8, tm):
    S, DIN = z.shape
    DA = gv.shape[1]
    DB = DA
    H = DA // HEAD
    o = 2 * DA
    nch = tm // HEAD
    nblk = S // tm

    def body(z_ref, zp_ref, zn_ref, dy_ref, dyn_ref, ws_ref, bsb_ref, gv_ref, bv_ref, cw_ref,
             dz_ref, dws_ref, dbs_ref, dgv_ref, dbv_ref, dcw_ref, dbin_ref):
        i = pl.program_id(0)

        @pl.when(i == 0)
        def _():
            for r in (dws_ref, dbs_ref, dgv_ref, dbv_ref, dcw_ref, dbin_ref):
                r[...] = jnp.zeros_like(r)

        ri = lax.broadcasted_iota(jnp.int32, (HEAD, HEAD), 0)
        ci = lax.broadcasted_iota(jnp.int32, (HEAD, HEAD), 1)
        tril = ri >= ci
        for h in range(H):
            cs = slice(h * HEAD, (h + 1) * HEAD)
            vs = slice(DA + h * HEAD, DA + (h + 1) * HEAD)
            wb = jnp.where(tril, ws_ref[h], 0.0).astype(BF16)
            gvh = gv_ref[:, cs]
            dws_acc = jnp.zeros((HEAD, HEAD), F32)
            dbs_acc = jnp.zeros((HEAD, HEAD), F32)
            dgv_acc = jnp.zeros((1, HEAD), F32)
            dbv_acc = jnp.zeros((1, HEAD), F32)
            dbu_acc = jnp.zeros((1, HEAD), F32)
            dbvv_acc = jnp.zeros((1, HEAD), F32)
            for n in range(nch):
                rs = slice(n * HEAD, (n + 1) * HEAD)
                u, du_dz = _gelu_and_grad(z_ref[rs, cs])
                v, dv_dz = _gelu_and_grad(z_ref[rs, vs])
                vhat, rstd = _ln_stats(v)
                vnb = (vhat * gvh + bv_ref[:, cs]).astype(BF16)
                mixed = jnp.dot(wb, vnb, preferred_element_type=F32) + bsb_ref[h]
                dya = dy_ref[rs, cs]
                dzu = dya * mixed * du_dz
                dmx = dya * u
                dmxb = dmx.astype(BF16)
                dbs_acc += jnp.broadcast_to(jnp.sum(dmx, axis=1, keepdims=True), (HEAD, HEAD))
                dws_acc += lax.dot_general(dmxb, vnb, _DIMS['nt'], preferred_element_type=F32)
                dvn = lax.dot_general(wb, dmxb, _DIMS['tn'], preferred_element_type=F32)
                dgv_acc += jnp.sum(dvn * vhat, axis=0, keepdims=True)
                dbv_acc += jnp.sum(dvn, axis=0, keepdims=True)
                dxh = dvn * gvh
                m1 = jnp.mean(dxh, axis=-1, keepdims=True)
                m2 = jnp.mean(dxh * vhat, axis=-1, keepdims=True)
                dzv = rstd * (dxh - m1 - vhat * m2) * dv_dz
                dz_ref[rs, cs] = dzu.astype(BF16)
                dz_ref[rs, vs] = dzv.astype(BF16)
                dbu_acc += jnp.sum(dzu, axis=0, keepdims=True)
                dbvv_acc += jnp.sum(dzv, axis=0, keepdims=True)
            dws_ref[h] += jnp.where(tril, dws_acc, 0.0)
            dbs_ref[h] += dbs_acc
            dgv_ref[:, cs] += dgv_acc
            dbv_ref[:, cs] += dbv_acc
            dbin_ref[:, cs] += dbu_acc
            dbin_ref[:, vs] += dbvv_acc

        zb = z_ref[:, o:o + DB]
        zc = z_ref[:, o + DB:o + 2 * DB]
        zx = z_ref[:, o + 2 * DB:]
        hc = zc * zx
        keep = (i > 0).astype(F32)
        p1 = zp_ref[7:8, o + DB:o + 2 * DB] * zp_ref[7:8, o + 2 * DB:] * keep
        p2 = zp_ref[6:7, o + DB:o + 2 * DB] * zp_ref[6:7, o + 2 * DB:] * keep
        rowi = lax.broadcasted_iota(jnp.int32, (tm, DB), 0)
        r1, r2 = _conv_shifted(hc, p1, p2, rowi)
        w0, w1, w2 = cw_ref[0:1, :], cw_ref[1:2, :], cw_ref[2:3, :]
        cv = w0 * r2 + w1 * r1 + w2 * hc
        dyb = dy_ref[:, DA:]
        dzb = dyb * cv
        dcv = dyb * zb
        more = (i < nblk - 1).astype(F32)
        n0 = dyn_ref[0:1, DA:] * zn_ref[0:1, o:o + DB] * more
        n1 = dyn_ref[1:2, DA:] * zn_ref[1:2, o:o + DB] * more
        f1 = jnp.where(rowi == tm - 1, n0, pltpu.roll(dcv, tm - 1, 0))
        f2 = jnp.where(rowi == tm - 1, n1, jnp.where(rowi == tm - 2, n0, pltpu.roll(dcv, tm - 2, 0)))
        dhc = w2 * dcv + w1 * f1 + w0 * f2
        dzc = dhc * zx
        dzx = dhc * zc
        dcw_ref[0:1, :] += jnp.sum(dcv * r2, axis=0, keepdims=True)
        dcw_ref[1:2, :] += jnp.sum(dcv * r1, axis=0, keepdims=True)
        dcw_ref[2:3, :] += jnp.sum(dcv * hc, axis=0, keepdims=True)
        dz_ref[:, o:o + DB] = dzb.astype(BF16)
        dz_ref[:, o + DB:o + 2 * DB] = dzc.astype(BF16)
        dz_ref[:, o + 2 * DB:] = dzx.astype(BF16)
        dbin_ref[:, o:o + DB] += jnp.sum(dzb, axis=0, keepdims=True)
        dbin_ref[:, o + DB:o + 2 * DB] += jnp.sum(dzc, axis=0, keepdims=True)
        dbin_ref[:, o + 2 * DB:] += jnp.sum(dzx, axis=0, keepdims=True)

    full3 = lambda i: (0, 0, 0)
    full2 = lambda i: (0, 0)
    prev8 = lambda i: (jnp.maximum(i * (tm // 8) - 1, 0), 0)
    next8 = lambda i: (jnp.minimum((i + 1) * (tm // 8), S // 8 - 1), 0)
    est = 2 * (tm * DIN * 4 + tm * (DA + DB) * 4 + tm * DIN * 2) + 16 * tm * DB * 4
    return pl.pallas_call(
        body, name="mix_bwd", grid=(nblk,),
        in_specs=[pl.BlockSpec((tm, DIN), lambda i: (i, 0)),
                  pl.BlockSpec((8, DIN), prev8), pl.BlockSpec((8, DIN), next8),
                  pl.BlockSpec((tm, DA + DB), lambda i: (i, 0)), pl.BlockSpec((8, DA + DB), next8),
                  pl.BlockSpec(ws.shape, full3), pl.BlockSpec(bsb.shape, full3),
                  pl.BlockSpec(gv.shape, full2), pl.BlockSpec(bv.shape, full2), pl.BlockSpec(cw8.shape, full2)],
        out_specs=[pl.BlockSpec((tm, DIN), lambda i: (i, 0)),
                   pl.BlockSpec(ws.shape, full3), pl.BlockSpec(ws.shape, full3),
                   pl.BlockSpec(gv.shape, full2), pl.BlockSpec(gv.shape, full2),
                   pl.BlockSpec(cw8.shape, full2), pl.BlockSpec((1, DIN), full2)],
        out_shape=[jax.ShapeDtypeStruct((S, DIN), BF16),
                   jax.ShapeDtypeStruct(ws.shape, F32), jax.ShapeDtypeStruct(ws.shape, F32),
                   jax.ShapeDtypeStruct(gv.shape, F32), jax.ShapeDtypeStruct(gv.shape, F32),
                   jax.ShapeDtypeStruct(cw8.shape, F32), jax.ShapeDtypeStruct((1, DIN), F32)],
        compiler_params=pltpu.CompilerParams(dimension_semantics=("arbitrary",),
                                             vmem_limit_bytes=_vmem_limit(est)),
    )(z, z, z, dy, dy, ws, bsb, gv, bv, cw8)


def _place():
    x, y, c = lax.axis_index("x"), lax.axis_index("y"), lax.axis_index("c")
    others = [(1 - x, y), (x, 1 - y), (1 - x, 1 - y)]
    return x, y, c, 2 * x + y, others, [2 * ox + oy for ox, oy in others]


def _half(c, rows_half):
    return pl.ds(pl.multiple_of(c * rows_half, 16), rows_half)


def _rcopy(src, dst, ssem, rsem, dev):
    return pltpu.make_async_remote_copy(src_ref=src, dst_ref=dst, send_sem=ssem, recv_sem=rsem,
                                        device_id=dev, device_id_type=MESH)


class _Comm:
    def __init__(self, aliased, ins, fresh, n_sems, stages, sink):
        self.aliased, self.ins, self.fresh = list(aliased), list(ins), list(fresh)
        self.n_sems, self.stages, self.sink = n_sems, stages, sink


def _deliver(comms, outs):
    pos = 0
    for cm in comms:
        al = outs[pos:pos + len(cm.aliased)]
        pos += len(cm.aliased)
        fr = outs[pos:pos + len(cm.fresh)]
        pos += len(cm.fresh)
        cm.sink(list(al), list(fr))


def _comm_call(name, comms):
    def body(*refs):
        pos = 0
        cm_ins, cm_al, cm_fr = [], [], []
        for cm in comms:
            pos += len(cm.aliased)
            cm_ins.append(refs[pos:pos + len(cm.ins)])
            pos += len(cm.ins)
        for cm in comms:
            cm_al.append(refs[pos:pos + len(cm.aliased)])
            pos += len(cm.aliased)
            cm_fr.append(refs[pos:pos + len(cm.fresh)])
            pos += len(cm.fresh)
        for i, cm in enumerate(comms):
            ssem, rsem = refs[pos + 2 * i], refs[pos + 2 * i + 1]
            for stage in cm.stages:
                for d in stage(cm_al[i], cm_ins[i], cm_fr[i], ssem, rsem, False):
                    d.start()
                for d in stage(cm_al[i], cm_ins[i], cm_fr[i], ssem, rsem, True):
                    d.wait()

    in_arrays, out_shapes, scratch, aliases = [], [], [], {}
    for cm in comms:
        for a in cm.aliased:
            aliases[len(in_arrays)] = len(out_shapes)
            in_arrays.append(a)
            out_shapes.append(jax.ShapeDtypeStruct(a.shape, a.dtype))
        in_arrays += cm.ins
        out_shapes += cm.fresh
        scratch += [pltpu.SemaphoreType.DMA((cm.n_sems,))] * 2
    outs = pl.pallas_call(body, name=name, in_specs=[ANY] * len(in_arrays), out_specs=[ANY] * len(out_shapes),
                          out_shape=out_shapes, scratch_shapes=scratch, input_output_aliases=aliases)(*in_arrays)
    _deliver(comms, outs)


def _c_gather(fulls, sink, ici, fwd, conv=None):
    n = len(fulls)
    n_ici = 3 * (n + (conv is not None)) if ici else 0

    def stage_ici(al, ins, fr, ssem, rsem, wait):
        x, y, c, chip, others, ochips = _place()
        ds = []
        for m in range(n + (conv is not None)):
            for k in range(3):
                i = 3 * m + k
                if m < n:
                    ah = al[m].shape[1] // 2
                    mine, lands = al[m].at[chip, _half(c, ah)], al[m].at[ochips[k], _half(c, ah)]
                else:
                    mine, lands = al[m].at[chip], al[m].at[ochips[k]]
                ds.append(_rcopy(mine, lands, ssem.at[i], rsem.at[i], (x, y, c)) if wait
                          else _rcopy(mine, mine, ssem.at[i], rsem.at[i], (*others[k], c)))
        return ds

    def stage_fwd(al, ins, fr, ssem, rsem, wait):
        x, y, c, chip, others, ochips = _place()
        ds = []
        for m in range(n):
            ah = al[m].shape[1] // 2
            for k in range(3):
                i = n_ici + 3 * m + k
                got = al[m].at[ochips[k], _half(c, ah)]
                ds.append(_rcopy(got, al[m].at[ochips[k], _half(1 - c, ah)], ssem.at[i], rsem.at[i], (x, y, c)) if wait
                          else _rcopy(got, got, ssem.at[i], rsem.at[i], (x, y, 1 - c)))
        return ds

    stages = ([stage_ici] if ici else []) + ([stage_fwd] if fwd else [])
    return _Comm(list(fulls) + ([conv] if conv is not None else []), [], [], n_ici + (3 * n if fwd else 0),
                 stages, sink)


def _c_rs_pair(parts, sink):
    def stage(al, ins, fr, ssem, rsem, wait):
        x, y, c, _, _, _ = _place()
        return [_rcopy(ins[m].at[:, _half(1 - c, ins[m].shape[1] // 2), :], fr[m], ssem.at[m], rsem.at[m],
                       (x, y, 1 - c)) for m in range(len(ins))]
    fresh = [jax.ShapeDtypeStruct((p.shape[0], p.shape[1] // 2, p.shape[2]), p.dtype) for p in parts]
    return _Comm([], parts, fresh, len(parts), [stage], sink)


def _c_rs_chips(psums, sink):
    def stage(al, ins, fr, ssem, rsem, wait):
        x, y, c, _, others, ochips = _place()
        return [_rcopy(ins[m].at[ochips[k]], fr[m].at[k], ssem.at[3 * m + k], rsem.at[3 * m + k], (*others[k], c))
                for m in range(len(ins)) for k in range(3)]
    fresh = [jax.ShapeDtypeStruct((3,) + p.shape[1:], p.dtype) for p in psums]
    return _Comm([], psums, fresh, 3 * len(psums), [stage], sink)


def _c_exchange(g2s, sink):
    def stage(al, ins, fr, ssem, rsem, wait):
        x, y, c, _, _, _ = _place()
        if wait:
            return [_rcopy(al[m].at[c], al[m].at[1 - c], ssem.at[m], rsem.at[m], (x, y, c)) for m in range(len(al))]
        return [_rcopy(al[m].at[c], al[m].at[c], ssem.at[m], rsem.at[m], (x, y, 1 - c)) for m in range(len(al))]
    return _Comm(g2s, [], [], len(g2s), [stage], sink)


def _allreduce_small(v):
    m_per, n = v.shape

    def body(x_ref, all_ref, sum_ref, send_sems, recv_sems, local_sem):
        x, y, c, _, chips, _ = _place()
        me, sibling = (x, y, c), (x, y, 1 - c)

        def rows(px, py, pc):
            return all_ref.at[pl.ds(pl.multiple_of((4 * px + 2 * py + pc) * m_per, 8), m_per), :]

        def copy(k, block, to, src=None):
            return pltpu.make_async_remote_copy(
                src_ref=rows(*block) if src is None else src, dst_ref=rows(*block),
                send_sem=send_sems.at[k], recv_sem=recv_sems.at[k], device_id=to, device_id_type=MESH)

        mine = pltpu.make_async_copy(x_ref, rows(*me), local_sem)
        mine.start()
        first = [copy(0, me, sibling, src=x_ref)]
        first += [copy(1 + j, me, (*chip, c), src=x_ref) for j, chip in enumerate(chips)]
        for cp in first:
            cp.start()
        passed = [copy(4 + j, (*chip, c), sibling) for j, chip in enumerate(chips)]
        for j, chip in enumerate(chips):
            copy(1 + j, (*chip, c), me).wait_recv()
            passed[j].start()
        copy(0, sibling, me).wait_recv()
        for j, chip in enumerate(chips):
            copy(4 + j, (*chip, 1 - c), me).wait_recv()
        for cp in first + passed:
            cp.wait_send()
        mine.wait()
        acc = all_ref[0:m_per, :]
        for d in range(1, 8):
            acc = acc + all_ref[d * m_per:(d + 1) * m_per, :]
        sum_ref[...] = acc

    vm = pl.BlockSpec(memory_space=pltpu.VMEM)
    return pl.pallas_call(
        body, name="allreduce_small",
        out_shape=[jax.ShapeDtypeStruct((8 * m_per, n), v.dtype), jax.ShapeDtypeStruct((m_per, n), v.dtype)],
        in_specs=[vm], out_specs=[vm, vm],
        scratch_shapes=[pltpu.SemaphoreType.DMA((7,)), pltpu.SemaphoreType.DMA((7,)), pltpu.SemaphoreType.DMA],
    )(v)[1]


def _sum_pair(name, part, sib, c_idx):
    _, A, B = part.shape
    ah = A // 2
    br = _rows(ah, B * 2, 2 * 2 ** 20)

    def body(c_ref, a_ref, b_ref, o_ref):
        o_ref[...] = (a_ref[...].astype(F32) + b_ref[...].astype(F32)).astype(BF16)

    gs = pltpu.PrefetchScalarGridSpec(
        num_scalar_prefetch=1, grid=(N_CHIPS, ah // br),
        in_specs=[pl.BlockSpec((None, None, br, B), lambda j, i, c: (j, c[0], i, 0)),
                  pl.BlockSpec((None, br, B), lambda j, i, c: (j, i, 0))],
        out_specs=pl.BlockSpec((None, br, B), lambda j, i, c: (j, i, 0)))
    return pl.pallas_call(body, name=name, grid_spec=gs,
                          out_shape=jax.ShapeDtypeStruct((N_CHIPS, ah, B), BF16))(
                              c_idx, part.reshape(N_CHIPS, 2, ah, B), sib)


def _sum_chips(name, psum, recv, place_idx):
    _, ah, B = psum.shape
    br = _rows(ah, B * 4, 2 * 2 ** 20)

    def body(c_ref, p_ref, r0_ref, r1_ref, r2_ref, o_ref):
        o_ref[...] = ((p_ref[...].astype(F32) + r0_ref[...].astype(F32)) + r1_ref[...].astype(F32)) \
            + r2_ref[...].astype(F32)

    gs = pltpu.PrefetchScalarGridSpec(
        num_scalar_prefetch=1, grid=(ah // br,),
        in_specs=[pl.BlockSpec((None, br, B), lambda i, c: (c[0], i, 0))]
        + [pl.BlockSpec((None, br, B), functools.partial(lambda i, c, k: (k, i, 0), k=k)) for k in range(3)],
        out_specs=pl.BlockSpec((None, br, B), lambda i, c: (c[1], i, 0)))
    return pl.pallas_call(body, name=name, grid_spec=gs,
                          out_shape=jax.ShapeDtypeStruct((2, ah, B), F32))(place_idx, psum, recv, recv, recv)


def _cast_into(name, a, place_idx):
    A, B = a.shape
    br = _rows(A, B * 4)

    def body(c_ref, a_ref, o_ref):
        o_ref[...] = a_ref[...].astype(BF16)

    gs = pltpu.PrefetchScalarGridSpec(
        num_scalar_prefetch=1, grid=(A // br,),
        in_specs=[pl.BlockSpec((br, B), lambda i, c: (i, 0))],
        out_specs=pl.BlockSpec((None, br, B), lambda i, c: (c[0], i, 0)))
    return pl.pallas_call(body, name=name, grid_spec=gs,
                          out_shape=jax.ShapeDtypeStruct((N_CHIPS, A, B), BF16))(place_idx, a)


def _adamw_small(gsum, gconv, offsets, ws, ms, vs):
    n = len(ws)

    def body(*refs):
        g_ref, gc_ref = refs[0], refs[1]
        w_refs = refs[2:2 + n]
        m_refs = refs[2 + n:2 + 2 * n]
        v_refs = refs[2 + 2 * n:2 + 3 * n]
        outs = refs[2 + 3 * n:]
        for i in range(n):
            rows = w_refs[i].shape[0]
            g = gc_ref[...] if offsets[i] is None else g_ref[offsets[i]:offsets[i] + rows, :]
            d, m2, v2 = _adamw_math(w_refs[i][...], g, m_refs[i][...], v_refs[i][...])
            outs[4 * i][...] = g
            outs[4 * i + 1][...] = d
            outs[4 * i + 2][...] = m2
            outs[4 * i + 3][...] = v2

    vm = pl.BlockSpec(memory_space=pltpu.VMEM)
    out_shape = []
    for w in ws:
        out_shape += [jax.ShapeDtypeStruct(w.shape, F32)] * 4
    return pl.pallas_call(body, name="adamw_small", in_specs=[vm] * (2 + 3 * n), out_specs=[vm] * (4 * n),
                          out_shape=out_shape)(gsum, gconv, *ws, *ms, *vs)


def _ident(accs, tes, res):
    return [accs[0]]


def _ffn_up(name, hb, wg4, wu4, comms=()):
    S, D = hb.shape

    def epi(accs, tes, res):
        g, u = accs
        return [g, u, g * jax.nn.sigmoid(g) * u]

    return _mm(name, 'nn', [hb], [wg4, wu4], [(0, 0, 0), (0, 1, 1)], 2, epi, [BF16] * 3,
               bm=_blk(S, 256), bn=wg4.shape[2], bk=D, b_blocked=True, n_outer=True, comms=comms)


def _ffn_down(name, ab, wd, hres, comms=()):
    S, F = ab.shape
    D = wd.shape[1]
    return _mm(name, 'nn', [ab], [wd], [(0, 0, 0)], 1, lambda accs, tes, res: [ALPHA * tes[0] + 0.5 * accs[0]],
               [F32], bm=_blk(S, 1024), bn=_blk(D, 512), bk=F, tile_extras=[hres], comms=comms)[0]


def _ffn_da(name, dsub, wd, gb, ub, comms=()):
    S, D = dsub.shape
    F = wd.shape[0]

    def epi(accs, tes, res):
        da = accs[0]
        g = tes[0].astype(F32)
        u = tes[1].astype(F32)
        s = jax.nn.sigmoid(g)
        return [da * u * (s * (1.0 + g * (1.0 - s))), da * (g * s)]

    return _mm(name, 'nt', [dsub], [wd], [(0, 0, 0)], 1, epi, [BF16, BF16],
               bm=_blk(S, 1024), bn=_blk(F, 512), bk=D, tile_extras=[gb, ub], comms=comms)


def _dw_rows(name, ab, dsub, cs, comms=()):
    S, D = dsub.shape
    return _mm(name, 'tn', [ab], [dsub], [(0, 0, 0)], 1, _ident, [BF16],
               bm=cs, bn=_blk(D, 1024), bk=_blk(S, 1024), comms=comms)[0].reshape(N_CHIPS, cs, D)


def _dw_cols(name, hb, dxb, cs, comms=()):
    S, D = hb.shape
    return _mm(name, 'tn', [hb], [dxb], [(0, 0, 0)], 1, _ident, [BF16],
               bm=_blk(D, 1024), bn=cs, bk=_blk(S, 1024), out_blocked=True, comms=comms)[0]


def _ffn_dh(name, dgb, dub, wg4, wu4, dres, comms=()):
    S = dgb.shape[0]
    D = wg4.shape[1]
    return _mm(name, 'nt', [dgb, dub], [wg4, wu4], [(0, 0, 0), (1, 1, 0)], 1,
               lambda accs, tes, res: [accs[0] + tes[0]], [F32],
               bm=_blk(S, 1024), bn=_blk(D, 1024), bk=wg4.shape[2], tile_extras=[dres], b_blocked=True,
               comms=comms)[0]


def _step(p):
    x = p['x'][0]
    target = p['loss_target'][0]
    S, D = x.shape
    cx, cy, cc = lax.axis_index("x"), lax.axis_index("y"), lax.axis_index("c")
    chip = 2 * cx + cy
    c_idx = jnp.reshape(cc, (1,)).astype(jnp.int32)
    place_idx = jnp.stack([chip, cc]).astype(jnp.int32)

    w_s = p['w_s'][0]
    H = w_s.shape[0]
    DA = H * HEAD
    bsb = jnp.broadcast_to(p['b_s'][0][:, :, None], (H, HEAD, HEAD))
    conv_shard = p['conv_w'][0]
    conv8 = jnp.zeros((8, conv_shard.shape[1]), F32).at[:CONV_TAPS].set(conv_shard)

    W = {n: _cast_into(f"cast_{n}", p[n][0], place_idx) for n in BIG}
    conv = {'full': lax.dynamic_update_slice(jnp.zeros((N_CHIPS,) + conv8.shape, F32), conv8[None], (chip, 0, 0))}

    def gathered(names, with_conv=False):
        def sink(al, fr):
            W.update(zip(names, al))
            if with_conv:
                conv['full'] = al[len(names)]
        return sink

    def c_gather(names, ici, fwd, with_conv=False):
        return _c_gather([W[n] for n in names], gathered(names, with_conv), ici, fwd,
                         conv['full'] if with_conv else None)

    ffa = ['ffa_gate', 'ffa_up', 'ffa_down']
    later = ['w_in', 'w_out', 'ffc_gate', 'ffc_up']
    _comm_call("gather_ffa", [c_gather(ffa, True, True, with_conv=True)])
    cw8 = jnp.transpose(conv['full'], (1, 0, 2)).reshape(8, DA)
    cs_ff = W['ffa_gate'].shape[2]
    F = N_CHIPS * cs_ff
    wd_a = W['ffa_down'].reshape(F, D)

    xb = _cast_bf16("cast_x", x)
    ga, ua, aa = _ffn_up("ffa_up", xb, W['ffa_gate'], W['ffa_up'], comms=[c_gather(later, True, False)])
    r1 = _ffn_down("ffa_down", aa, wd_a, x,
                   comms=[c_gather(later, False, True), c_gather(['ffc_down'], True, False)])
    h1, h1b = _ln_fwd("ln_a", r1, p['ln_a_g'], p['ln_a_b'])
    w_in4 = W['w_in']
    cs_in = w_in4.shape[2]
    z = _mm("w_in", 'nn', [h1b], [w_in4], [(0, 0, 0)], 1, lambda accs, tes, res: [accs[0] + res[0]], [F32],
            bm=_blk(S, 512), bn=cs_in, bk=D, row_extras=[p['b_in']], b_blocked=True, n_outer=True,
            comms=[c_gather(['ffc_down'], False, True)])[0]
    w_out = W['w_out'].reshape(DA * 2, D)
    wd_c = W['ffc_down'].reshape(F, D)
    tm = _blk(S, 256)
    yb = _mix_fwd(z, w_s, bsb, p['ln_v_g'], p['ln_v_b'], cw8, tm)
    r2 = _mm("w_out", 'nn', [yb], [w_out], [(0, 0, 0)], 1,
             lambda accs, tes, res: [accs[0] + res[0] + ALPHA * tes[0]], [F32],
             bm=_blk(S, 1024), bn=_blk(D, 1024), bk=2 * DA, tile_extras=[h1], row_extras=[p['b_out']])[0]
    h2, h2b = _ln_fwd("ln_m", r2, p['ln_m_g'], p['ln_m_b'])
    gc, uc, ac = _ffn_up("ffc_up", h2b, W['ffc_gate'], W['ffc_up'])
    r3 = _ffn_down("ffc_down", ac, wd_c, h2)

    pg, sg, sib, psum, recv, g2 = {}, {}, {}, {}, {}, {}

    def c_pair(names):
        return _c_rs_pair([pg[n] for n in names], lambda al, fr: sib.update(zip(names, fr)))

    def c_chips(names):
        return _c_rs_chips([psum[n] for n in names], lambda al, fr: recv.update(zip(names, fr)))

    def c_swap(names):
        return _c_exchange([g2[n] for n in names], lambda al, fr: g2.update(zip(names, al)))

    def sum_pair(*names):
        for n in names:
            psum[n] = _sum_pair(f"sum_pair_{n}", pg[n], sib[n], c_idx)

    def sum_chips(*names):
        for n in names:
            g2[n] = _sum_chips(f"sum_chips_{n}", psum[n], recv[n], place_idx)

    dres3, dsub3, sg['ln_c_g'], sg['ln_c_b'], lsum = _ln_loss_bwd(
        "ln_c_loss_bwd", r3, target, p['ln_c_g'], p['ln_c_b'], 0.5)

    dgc, duc = _ffn_da("ffc_da", dsub3, wd_c, gc, uc)
    pg['ffc_down'] = _dw_rows("ffc_dwd", ac, dsub3, cs_ff)
    pg['ffc_gate'] = _dw_cols("ffc_dwg", h2b, dgc, cs_ff, comms=[c_pair(['ffc_down'])])
    sum_pair('ffc_down')
    pg['ffc_up'] = _dw_cols("ffc_dwu", h2b, duc, cs_ff, comms=[c_pair(['ffc_gate']), c_chips(['ffc_down'])])
    sum_pair('ffc_gate')
    dh2 = _ffn_dh("ffc_dh", dgc, duc, W['ffc_gate'], W['ffc_up'], dres3,
                  comms=[c_pair(['ffc_up']), c_chips(['ffc_gate'])])
    sum_pair('ffc_up')
    sum_chips('ffc_down', 'ffc_gate')

    dres2, dmix, sg['ln_m_g'], sg['ln_m_b'], sg['b_out'] = _ln_bwd("ln_m_bwd", r2, dh2, p['ln_m_g'], 1.0)
    pg['w_out'] = _mm("dw_out", 'tn', [yb], [dmix], [(0, 0, 0)], 1, _ident, [BF16],
                      bm=_blk(2 * DA, 1024), bn=_blk(D, 1024), bk=_blk(S, 2048),
                      comms=[c_swap(['ffc_down', 'ffc_gate'])])[0].reshape(N_CHIPS, 2 * DA // 4, D)
    dy = _mm("dy", 'nt', [dmix], [w_out], [(0, 0, 0)], 1, _ident, [F32],
             bm=_blk(S, 1024), bn=_blk(2 * DA, 1024), bk=D, comms=[c_pair(['w_out'])])[0]
    sum_pair('w_out')
    dz, dws, dbs, sg['ln_v_g'], sg['ln_v_b'], dcw, sg['b_in'] = _mix_bwd(
        z, dy, w_s, bsb, p['ln_v_g'], p['ln_v_b'], cw8, tm)
    sg['w_s'] = dws
    sg['b_s'] = dbs[:, :, 0]
    pg['w_in'] = _dw_cols("dw_in", h1b, dz, cs_in, comms=[c_chips(['ffc_up']), c_chips(['w_out'])])
    sum_chips('ffc_up', 'w_out')
    dh1 = _mm("dh1", 'nt', [dz], [w_in4], [(0, 0, 0)], 1, lambda accs, tes, res: [accs[0] + tes[0]], [F32],
              bm=_blk(S, 1024), bn=_blk(D, 1024), bk=cs_in, tile_extras=[dres2], b_blocked=True,
              comms=[c_pair(['w_in']), c_swap(['ffc_up', 'w_out'])])[0]
    sum_pair('w_in')

    dres1, dsub1, sg['ln_a_g'], sg['ln_a_b'], _ = _ln_bwd("ln_a_bwd", r1, dh1, p['ln_a_g'], 0.5)
    dga, dua = _ffn_da("ffa_da", dsub1, wd_a, ga, ua, comms=[c_chips(['w_in'])])
    sum_chips('w_in')
    pg['ffa_down'] = _dw_rows("ffa_dwd", aa, dsub1, cs_ff, comms=[c_swap(['w_in'])])
    pg['ffa_gate'] = _dw_cols("ffa_dwg", xb, dga, cs_ff, comms=[c_pair(['ffa_down'])])
    sum_pair('ffa_down')
    pg['ffa_up'] = _dw_cols("ffa_dwu", xb, dua, cs_ff, comms=[c_pair(['ffa_gate']), c_chips(['ffa_down'])])
    sum_pair('ffa_gate')
    _comm_call("pair_ffa_up", [c_pair(['ffa_up'])])
    sum_pair('ffa_up')
    sum_chips('ffa_down')
    dx = _ffn_dh("ffa_dh", dga, dua, W['ffa_gate'], W['ffa_up'], dres1,
                 comms=[c_chips(['ffa_gate']), c_chips(['ffa_up']), c_swap(['ffa_down'])])
    sum_chips('ffa_gate', 'ffa_up')
    _comm_call("swap_ffa", [c_swap(['ffa_gate', 'ffa_up'])])

    out = {}
    for n in BIG:
        g = g2[n].reshape(p[n][0].shape)
        d, m2, v2 = _adamw(f"adamw_{n}", p[n][0], g, p['m_' + n][0], p['v_' + n][0])
        shp = p[n].shape
        out[n] = (g.reshape(shp), d.reshape(shp), m2.reshape(shp), v2.reshape(shp))

    def as_rows(a):
        a = a.reshape(-1, 128)
        pad = (-a.shape[0]) % 8
        return jnp.pad(a, ((0, pad), (0, 0))) if pad else a

    pieces, offsets, off = [], {}, 0
    for n in SMALL + ['conv_w']:
        piece = as_rows(dcw[:CONV_TAPS] if n == 'conv_w' else sg[n])
        offsets[n] = off
        off += piece.shape[0]
        pieces.append(piece)
    packed, lsum, _ = lax.optimization_barrier((jnp.concatenate(pieces, axis=0), lsum, g2['ffa_up']))
    gsum = _allreduce_small(packed)
    gsum, lsum = lax.optimization_barrier((gsum, lsum))
    loss = lax.psum(0.5 * jnp.sum(lsum) / D, ("x", "y", "c"))
    n_conv = CONV_TAPS * DA // 128
    conv_sum = gsum[offsets['conv_w']:offsets['conv_w'] + n_conv].reshape(CONV_TAPS, DA)
    cw = conv_shard.shape[1]
    gconv = lax.dynamic_slice_in_dim(conv_sum, (2 * cx + cy) * cw, cw, axis=1).reshape(-1, 128)
    names = SMALL + ['conv_w']
    rows128 = lambda a: a.reshape(-1, 128)
    res = _adamw_small(gsum, gconv, [offsets[n] if n != 'conv_w' else None for n in names],
                       [rows128(p[n]) for n in names], [rows128(p['m_' + n]) for n in names],
                       [rows128(p['v_' + n]) for n in names])
    for i, n in enumerate(names):
        out[n] = tuple(r.reshape(p[n].shape) for r in res[4 * i:4 * i + 4])

    return (loss, dx.reshape(p['x'].shape), *[out[n][0] for n in WEIGHTS], *[out[n][1] for n in WEIGHTS],
            *[out[n][2] for n in WEIGHTS], *[out[n][3] for n in WEIGHTS])


def kernel(x, ffa_gate, ffa_up, ffa_down, ln_a_g, ln_a_b, w_in, b_in, w_s, b_s, ln_v_g, ln_v_b, conv_w, w_out, b_out, ln_m_g, ln_m_b, ffc_gate, ffc_up, ffc_down, ln_c_g, ln_c_b, loss_target, m_ffa_gate, m_ffa_up, m_ffa_down, m_ln_a_g, m_ln_a_b, m_w_in, m_b_in, m_w_s, m_b_s, m_ln_v_g, m_ln_v_b, m_conv_w, m_w_out, m_b_out, m_ln_m_g, m_ln_m_b, m_ffc_gate, m_ffc_up, m_ffc_down, m_ln_c_g, m_ln_c_b, v_ffa_gate, v_ffa_up, v_ffa_down, v_ln_a_g, v_ln_a_b, v_w_in, v_b_in, v_w_s, v_b_s, v_ln_v_g, v_ln_v_b, v_conv_w, v_w_out, v_b_out, v_ln_m_g, v_ln_m_b, v_ffc_gate, v_ffc_up, v_ffc_down, v_ln_c_g, v_ln_c_b):
    return _step(dict(locals()))
```

```python
import functools
import math

import jax
import jax.numpy as jnp
from jax import lax
from jax.experimental import pallas as pl
from jax.experimental.pallas import tpu as pltpu

F32 = jnp.float32
BF16 = jnp.bfloat16
MESH = pl.DeviceIdType.MESH
ANY = pl.BlockSpec(memory_space=pl.ANY)

HEAD = 128
CONV_TAPS = 3
LN_EPS = 1e-5
ALPHA = float(2.0 ** 0.25)
ADAM_LR, ADAM_B1, ADAM_B2, ADAM_EPS, ADAM_WD, ADAM_STEP = 0.001, 0.9, 0.999, 1e-08, 0.01, 10

V7X_VMEM_BYTES = 64 * 2 ** 20
N_CHIPS = 4

BIG = ['ffa_gate', 'ffa_up', 'ffa_down', 'w_in', 'w_out', 'ffc_gate', 'ffc_up', 'ffc_down']
WEIGHTS = ['ffa_gate', 'ffa_up', 'ffa_down', 'ln_a_g', 'ln_a_b', 'w_in', 'b_in', 'w_s', 'b_s', 'ln_v_g', 'ln_v_b',
           'conv_w', 'w_out', 'b_out', 'ln_m_g', 'ln_m_b', 'ffc_gate', 'ffc_up', 'ffc_down', 'ln_c_g', 'ln_c_b']
SMALL = [n for n in WEIGHTS if n not in BIG and n != 'conv_w']


def _vmem_limit(estimate_bytes):
    return int(min(max(estimate_bytes * 1.3, 32 * 2 ** 20), V7X_VMEM_BYTES - 6 * 2 ** 20))


def _blk(dim, pref, mult=128):
    if dim <= pref:
        return dim
    best = None
    for d in range(mult, pref + 1, mult):
        if dim % d == 0:
            best = d
    assert best is not None, (dim, pref)
    return best


def _rows(n_rows, row_bytes, target=4 * 2 ** 20):
    return _blk(n_rows, max(16, target // row_bytes), 16)


def _gelu(x):
    c = math.sqrt(2.0 / math.pi)
    return 0.5 * x * (1.0 + jnp.tanh(c * (x + 0.044715 * (x * x * x))))


def _gelu_and_grad(x):
    c = math.sqrt(2.0 / math.pi)
    t = jnp.tanh(c * (x + 0.044715 * (x * x * x)))
    g = 0.5 * x * (1.0 + t)
    dg = 0.5 * (1.0 + t) + 0.5 * x * (1.0 - t * t) * (c * (1.0 + 3.0 * 0.044715 * (x * x)))
    return g, dg


def _ln_stats(r):
    mu = jnp.mean(r, axis=-1, keepdims=True)
    d = r - mu
    var = jnp.mean(d * d, axis=-1, keepdims=True)
    rstd = lax.rsqrt(var + LN_EPS)
    return d * rstd, rstd


_DIMS = {'nn': (((1,), (0,)), ((), ())), 'nt': (((1,), (1,)), ((), ())), 'tn': (((0,), (0,)), ((), ()))}


def _mm(name, mode, a_list, b_list, pairs, n_acc, epilogue, out_dtypes, bm, bn, bk,
        tile_extras=(), row_extras=(), b_blocked=False, out_blocked=False, n_outer=False, comms=()):
    a0, b0 = a_list[0], b_list[0]
    if mode == 'nn':
        M, K = a0.shape
        N = b0.shape[0] * b0.shape[2] if b_blocked else b0.shape[1]
    elif mode == 'nt':
        M, K = a0.shape
        N = b0.shape[1] if b_blocked else b0.shape[0]
    else:
        K, M = a0.shape
        N = b0.shape[1]
        assert not b_blocked
    assert M % bm == 0 and N % bn == 0 and K % bk == 0, (name, M, N, K, bm, bn, bk)
    gm, gn, gk = M // bm, N // bn, K // bk
    if b_blocked:
        assert (bn if mode == 'nn' else bk) == b0.shape[2], name
    dims = _DIMS[mode]

    def spec(shape, f):
        if n_outer:
            return pl.BlockSpec(shape, lambda g0, g1, g2: f(g1, g0, g2))
        return pl.BlockSpec(shape, f)

    if mode == 'tn':
        a_spec = spec((bk, bm), lambda i, j, k: (k, i))
    else:
        a_spec = spec((bm, bk), lambda i, j, k: (i, k))
    if mode == 'nn':
        b_spec = (spec((None, bk, bn), lambda i, j, k: (j, k, 0)) if b_blocked
                  else spec((bk, bn), lambda i, j, k: (k, j)))
    elif mode == 'nt':
        b_spec = (spec((None, bn, bk), lambda i, j, k: (k, j, 0)) if b_blocked
                  else spec((bn, bk), lambda i, j, k: (j, k)))
    else:
        b_spec = spec((bk, bn), lambda i, j, k: (k, j))
    te_spec = spec((bm, bn), lambda i, j, k: (i, j))
    re_spec = spec((1, bn), lambda i, j, k: (0, j))
    if out_blocked:
        o_spec = spec((None, bm, bn), lambda i, j, k: (j, i, 0))
        o_shape = (gn, M, bn)
    else:
        o_spec = te_spec
        o_shape = (M, N)

    na, nb, nte, nre, no = len(a_list), len(b_list), len(tile_extras), len(row_extras), len(out_dtypes)

    n_scratch_acc = n_acc if gk > 1 else 0
    grid = (gn, gm, gk) if n_outer else (gm, gn, gk)

    def body(*refs):
        a_refs = refs[:na]
        b_refs = refs[na:na + nb]
        te_refs = refs[na + nb:na + nb + nte]
        re_refs = refs[na + nb + nte:na + nb + nte + nre]
        pos = na + nb + nte + nre
        cm_ins = []
        for cm in comms:
            pos += len(cm.aliased)
            cm_ins.append(refs[pos:pos + len(cm.ins)])
            pos += len(cm.ins)
        o_refs = refs[pos:pos + no]
        pos += no
        cm_al, cm_fr = [], []
        for cm in comms:
            cm_al.append(refs[pos:pos + len(cm.aliased)])
            pos += len(cm.aliased)
            cm_fr.append(refs[pos:pos + len(cm.fresh)])
            pos += len(cm.fresh)
        acc_refs = refs[pos:pos + n_scratch_acc]
        pos += n_scratch_acc
        cm_sems = [(refs[pos + 2 * i], refs[pos + 2 * i + 1]) for i in range(len(comms))]

        if comms:
            pids = [pl.program_id(d) for d in range(3)]

            @pl.when((pids[0] == 0) & (pids[1] == 0) & (pids[2] == 0))
            def _():
                for i, cm in enumerate(comms):
                    for d in cm.stages[0](cm_al[i], cm_ins[i], cm_fr[i], *cm_sems[i], False):
                        d.start()

        def finish(accs):
            outs = epilogue(accs, [r[...] for r in te_refs], [r[...] for r in re_refs])
            for o_ref, o in zip(o_refs, outs):
                o_ref[...] = o.astype(o_ref.dtype)

        if gk == 1:
            parts = [None] * n_acc
            for ai, bi, ci in pairs:
                d = lax.dot_general(a_refs[ai][...], b_refs[bi][...], dims, preferred_element_type=F32)
                parts[ci] = d if parts[ci] is None else parts[ci] + d
            finish(parts)
        else:
            kk = pl.program_id(2)

            @pl.when(kk == 0)
            def _():
                for r in acc_refs:
                    r[...] = jnp.zeros_like(r)

            for ai, bi, ci in pairs:
                acc_refs[ci][...] += lax.dot_general(a_refs[ai][...], b_refs[bi][...], dims,
                                                     preferred_element_type=F32)

            @pl.when(kk == gk - 1)
            def _():
                finish([r[...] for r in acc_refs])

        if comms:
            @pl.when((pids[0] == grid[0] - 1) & (pids[1] == grid[1] - 1) & (pids[2] == grid[2] - 1))
            def _():
                for i, cm in enumerate(comms):
                    for d in cm.stages[0](cm_al[i], cm_ins[i], cm_fr[i], *cm_sems[i], True):
                        d.wait()
                for i, cm in enumerate(comms):
                    for stage in cm.stages[1:]:
                        for d in stage(cm_al[i], cm_ins[i], cm_fr[i], *cm_sems[i], False):
                            d.start()
                        for d in stage(cm_al[i], cm_ins[i], cm_fr[i], *cm_sems[i], True):
                            d.wait()

    isz = lambda a: jnp.dtype(a.dtype).itemsize
    est = 2 * (sum(bm * bk * isz(a) for a in a_list) + sum(bk * bn * isz(b) for b in b_list)
               + sum(bm * bn * isz(t) for t in tile_extras)
               + sum(bm * bn * jnp.dtype(d).itemsize for d in out_dtypes))
    est += (2 * n_acc + 2) * bm * bn * 4
    cm_in_arrays, cm_out_shapes, cm_scratch, aliases = [], [], [], {}
    n_in = na + nb + nte + nre
    for cm in comms:
        for a in cm.aliased:
            aliases[n_in + len(cm_in_arrays)] = no + len(cm_out_shapes)
            cm_in_arrays.append(a)
            cm_out_shapes.append(jax.ShapeDtypeStruct(a.shape, a.dtype))
        cm_in_arrays += list(cm.ins)
        cm_out_shapes += list(cm.fresh)
        cm_scratch += [pltpu.SemaphoreType.DMA((cm.n_sems,))] * 2
    outs = pl.pallas_call(
        body, name=name, grid=grid,
        in_specs=[a_spec] * na + [b_spec] * nb + [te_spec] * nte + [re_spec] * nre + [ANY] * len(cm_in_arrays),
        out_specs=[o_spec] * no + [ANY] * len(cm_out_shapes),
        out_shape=[jax.ShapeDtypeStruct(o_shape, d) for d in out_dtypes] + cm_out_shapes,
        scratch_shapes=[pltpu.VMEM((bm, bn), F32)] * n_scratch_acc + cm_scratch,
        input_output_aliases=aliases,
        compiler_params=pltpu.CompilerParams(dimension_semantics=("parallel", "parallel", "arbitrary"),
                                             vmem_limit_bytes=_vmem_limit(est)),
    )(*a_list, *b_list, *tile_extras, *row_extras, *cm_in_arrays)
    _deliver(comms, outs[no:])
    return outs[:no]


def _rowwise(name, fn, br, row_ins, vec_ins, row_outs, acc_outs=()):
    S = row_ins[0].shape[0]
    assert S % br == 0, (name, S, br)
    nr, nv, no, nacc = len(row_ins), len(vec_ins), len(row_outs), len(acc_outs)

    def body(*refs):
        rin = refs[:nr]
        vin = refs[nr:nr + nv]
        rout = refs[nr + nv:nr + nv + no]
        aout = refs[nr + nv + no:]
        if nacc:
            @pl.when(pl.program_id(0) == 0)
            def _():
                for a in aout:
                    a[...] = jnp.zeros_like(a)
        fn(rin, vin, rout, aout)

    def full_spec(shape):
        nd = len(shape)
        return pl.BlockSpec(shape, lambda i: (0,) * nd)

    est = 2 * (sum(br * a.shape[1] * jnp.dtype(a.dtype).itemsize for a in row_ins)
               + sum(br * c * jnp.dtype(d).itemsize for c, d in row_outs))
    est += 6 * br * max(a.shape[1] for a in row_ins) * 4
    outs = pl.pallas_call(
        body, name=name, grid=(S // br,),
        in_specs=[pl.BlockSpec((br, a.shape[1]), lambda i: (i, 0)) for a in row_ins]
        + [full_spec(v.shape) for v in vec_ins],
        out_specs=[pl.BlockSpec((br, c), lambda i: (i, 0)) for c, _ in row_outs]
        + [full_spec(s) for s, _ in acc_outs],
        out_shape=[jax.ShapeDtypeStruct((S, c), d) for c, d in row_outs]
        + [jax.ShapeDtypeStruct(s, d) for s, d in acc_outs],
        compiler_params=pltpu.CompilerParams(dimension_semantics=("arbitrary",),
                                             vmem_limit_bytes=_vmem_limit(est)),
    )(*row_ins, *vec_ins)
    return outs


def _cast_bf16(name, a):
    def fn(rin, vin, rout, aout):
        rout[0][...] = rin[0][...].astype(BF16)
    return _rowwise(name, fn, _rows(a.shape[0], a.shape[1] * 4), [a], [], [(a.shape[1], BF16)])[0]


def _ln_fwd(name, r, g, b):
    def fn(rin, vin, rout, aout):
        xhat, _ = _ln_stats(rin[0][...])
        h = xhat * vin[0][...] + vin[1][...]
        rout[0][...] = h
        rout[1][...] = h.astype(BF16)
    D = r.shape[1]
    return _rowwise(name, fn, _rows(r.shape[0], D * 4, 2 * 2 ** 20), [r], [g, b], [(D, F32), (D, BF16)])


def _ln_loss_bwd(name, r, target, g, b, sub_scale):
    D = r.shape[1]

    def fn(rin, vin, rout, aout):
        xhat, rstd = _ln_stats(rin[0][...])
        gain = vin[0][...]
        err = xhat * gain + vin[1][...] - rin[1][...]
        dhv = err * (1.0 / D)
        dxh = dhv * gain
        m1 = jnp.mean(dxh, axis=-1, keepdims=True)
        m2 = jnp.mean(dxh * xhat, axis=-1, keepdims=True)
        dr = rstd * (dxh - m1 - xhat * m2)
        rout[0][...] = ALPHA * dr
        rout[1][...] = (sub_scale * dr).astype(BF16)
        aout[0][...] += jnp.sum(dhv * xhat, axis=0, keepdims=True)
        aout[1][...] += jnp.sum(dhv, axis=0, keepdims=True)
        aout[2][...] += jnp.sum(err * err, axis=0, keepdims=True)
    return _rowwise(name, fn, _rows(r.shape[0], D * 4, 2 * 2 ** 20), [r, target], [g, b],
                    [(D, F32), (D, BF16)], [((1, D), F32)] * 3)


def _ln_bwd(name, r, dh, g, sub_scale):
    D = r.shape[1]

    def fn(rin, vin, rout, aout):
        xhat, rstd = _ln_stats(rin[0][...])
        dhv = rin[1][...]
        dxh = dhv * vin[0][...]
        m1 = jnp.mean(dxh, axis=-1, keepdims=True)
        m2 = jnp.mean(dxh * xhat, axis=-1, keepdims=True)
        dr = rstd * (dxh - m1 - xhat * m2)
        rout[0][...] = ALPHA * dr
        rout[1][...] = (sub_scale * dr).astype(BF16)
        aout[0][...] += jnp.sum(dhv * xhat, axis=0, keepdims=True)
        aout[1][...] += jnp.sum(dhv, axis=0, keepdims=True)
        aout[2][...] += jnp.sum(dr, axis=0, keepdims=True)
    return _rowwise(name, fn, _rows(r.shape[0], D * 4, 2 * 2 ** 20), [r, dh], [g],
                    [(D, F32), (D, BF16)], [((1, D), F32)] * 3)


def _adamw_math(w, g, m, v):
    m2 = ADAM_B1 * m + (1.0 - ADAM_B1) * g
    v2 = ADAM_B2 * v + (1.0 - ADAM_B2) * (g * g)
    m_hat = m2 / (1.0 - ADAM_B1 ** ADAM_STEP)
    v_hat = v2 / (1.0 - ADAM_B2 ** ADAM_STEP)
    delta = -ADAM_LR * (m_hat / (jnp.sqrt(v_hat) + ADAM_EPS) + ADAM_WD * w)
    return delta, m2, v2


def _adamw(name, w, g, m, v):
    def fn(rin, vin, rout, aout):
        gv = rin[1][...]
        d, m2, v2 = _adamw_math(rin[0][...], gv, rin[2][...], rin[3][...])
        rout[0][...] = gv
        rout[1][...] = d
        rout[2][...] = m2
        rout[3][...] = v2
    C = w.shape[1]
    return _rowwise(name, fn, _rows(w.shape[0], C * 4, 2 * 2 ** 20), [w, g, m, v], [], [(C, F32)] * 4)


def _conv_shifted(hc, p1, p2, rowi):
    r1 = jnp.where(rowi == 0, p1, pltpu.roll(hc, 1, 0))
    r2 = jnp.where(rowi == 0, p2, jnp.where(rowi == 1, p1, pltpu.roll(hc, 2, 0)))
    return r1, r2


def _mix_fwd(z, ws, bsb, gv, bv, cw8, tm):
    S, DIN = z.shape
    DA = gv.shape[1]
    DB = DA
    H = DA // HEAD
    o = 2 * DA
    nch = tm // HEAD

    def body(z_ref, zp_ref, ws_ref, bsb_ref, gv_ref, bv_ref, cw_ref, y_ref):
        i = pl.program_id(0)
        ri = lax.broadcasted_iota(jnp.int32, (HEAD, HEAD), 0)
        ci = lax.broadcasted_iota(jnp.int32, (HEAD, HEAD), 1)
        tril = ri >= ci
        for h in range(H):
            cs = slice(h * HEAD, (h + 1) * HEAD)
            vs = slice(DA + h * HEAD, DA + (h + 1) * HEAD)
            wb = jnp.where(tril, ws_ref[h], 0.0).astype(BF16)
            for n in range(nch):
                rs = slice(n * HEAD, (n + 1) * HEAD)
                u = _gelu(z_ref[rs, cs])
                vhat, _ = _ln_stats(_gelu(z_ref[rs, vs]))
                vn = vhat * gv_ref[:, cs] + bv_ref[:, cs]
                mixed = jnp.dot(wb, vn.astype(BF16), preferred_element_type=F32) + bsb_ref[h]
                y_ref[rs, cs] = (u * mixed).astype(BF16)
        zb = z_ref[:, o:o + DB]
        hc = z_ref[:, o + DB:o + 2 * DB] * z_ref[:, o + 2 * DB:]
        keep = (i > 0).astype(F32)
        p1 = zp_ref[7:8, o + DB:o + 2 * DB] * zp_ref[7:8, o + 2 * DB:] * keep
        p2 = zp_ref[6:7, o + DB:o + 2 * DB] * zp_ref[6:7, o + 2 * DB:] * keep
        rowi = lax.broadcasted_iota(jnp.int32, (tm, DB), 0)
        r1, r2 = _conv_shifted(hc, p1, p2, rowi)
        cv = cw_ref[0:1, :] * r2 + cw_ref[1:2, :] * r1 + cw_ref[2:3, :] * hc
        y_ref[:, DA:] = (zb * cv).astype(BF16)

    full3 = lambda i: (0, 0, 0)
    full2 = lambda i: (0, 0)
    est = 2 * (tm * DIN * 4 + tm * (DA + DB) * 2) + 10 * tm * DB * 4
    return pl.pallas_call(
        body, name="mix_fwd", grid=(S // tm,),
        in_specs=[pl.BlockSpec((tm, DIN), lambda i: (i, 0)),
                  pl.BlockSpec((8, DIN), lambda i: (jnp.maximum(i * (tm // 8) - 1, 0), 0)),
                  pl.BlockSpec(ws.shape, full3), pl.BlockSpec(bsb.shape, full3),
                  pl.BlockSpec(gv.shape, full2), pl.BlockSpec(bv.shape, full2), pl.BlockSpec(cw8.shape, full2)],
        out_specs=pl.BlockSpec((tm, DA + DB), lambda i: (i, 0)),
        out_shape=jax.ShapeDtypeStruct((S, DA + DB), BF16),
        compiler_params=pltpu.CompilerParams(dimension_semantics=("arbitrary",),
                                             vmem_limit_bytes=_vmem_limit(est)),
    )(z, z, ws, bsb, gv, bv, cw8)


def _mix_bwd(z, dy, ws, bsb, gv, bv, cw8, tm):
    S, DIN = z.shape
    DA = gv.shape[1]
    DB = DA
    H = DA // HEAD
    o = 2 * DA
    nch = tm // HEAD
    nblk = S // tm

    def body(z_ref, zp_ref, zn_ref, dy_ref, dyn_ref, ws_ref, bsb_ref, gv_ref, bv_ref, cw_ref,
             dz_ref, dws_ref, dbs_ref, dgv_ref, dbv_ref, dcw_ref, dbin_ref):
        i = pl.program_id(0)

        @pl.when(i == 0)
        def _():
            for r in (dws_ref, dbs_ref, dgv_ref, dbv_ref, dcw_ref, dbin_ref):
                r[...] = jnp.zeros_like(r)

        ri = lax.broadcasted_iota(jnp.int32, (HEAD, HEAD), 0)
        ci = lax.broadcasted_iota(jnp.int32, (HEAD, HEAD), 1)
        tril = ri >= ci
        for h in range(H):
            cs = slice(h * HEAD, (h + 1) * HEAD)
            vs = slice(DA + h * HEAD, DA + (h + 1) * HEAD)
            wb = jnp.where(tril, ws_ref[h], 0.0).astype(BF16)
            gvh = gv_ref[:, cs]
            dws_acc = jnp.zeros((HEAD, HEAD), F32)
            dbs_acc = jnp.zeros((HEAD, HEAD), F32)
            dgv_acc = jnp.zeros((1, HEAD), F32)
            dbv_acc = jnp.zeros((1, HEAD), F32)
            dbu_acc = jnp.zeros((1, HEAD), F32)
            dbvv_acc = jnp.zeros((1, HEAD), F32)
            for n in range(nch):
                rs = slice(n * HEAD, (n + 1) * HEAD)
                u, du_dz = _gelu_and_grad(z_ref[rs, cs])
                v, dv_dz = _gelu_and_grad(z_ref[rs, vs])
                vhat, rstd = _ln_stats(v)
                vnb = (vhat * gvh + bv_ref[:, cs]).astype(BF16)
                mixed = jnp.dot(wb, vnb, preferred_element_type=F32) + bsb_ref[h]
                dya = dy_ref[rs, cs]
                dzu = dya * mixed * du_dz
                dmx = dya * u
                dmxb = dmx.astype(BF16)
                dbs_acc += jnp.broadcast_to(jnp.sum(dmx, axis=1, keepdims=True), (HEAD, HEAD))
                dws_acc += lax.dot_general(dmxb, vnb, _DIMS['nt'], preferred_element_type=F32)
                dvn = lax.dot_general(wb, dmxb, _DIMS['tn'], preferred_element_type=F32)
                dgv_acc += jnp.sum(dvn * vhat, axis=0, keepdims=True)
                dbv_acc += jnp.sum(dvn, axis=0, keepdims=True)
                dxh = dvn * gvh
                m1 = jnp.mean(dxh, axis=-1, keepdims=True)
                m2 = jnp.mean(dxh * vhat, axis=-1, keepdims=True)
                dzv = rstd * (dxh - m1 - vhat * m2) * dv_dz
                dz_ref[rs, cs] = dzu.astype(BF16)
                dz_ref[rs, vs] = dzv.astype(BF16)
                dbu_acc += jnp.sum(dzu, axis=0, keepdims=True)
                dbvv_acc += jnp.sum(dzv, axis=0, keepdims=True)
            dws_ref[h] += jnp.where(tril, dws_acc, 0.0)
            dbs_ref[h] += dbs_acc
            dgv_ref[:, cs] += dgv_acc
            dbv_ref[:, cs] += dbv_acc
            dbin_ref[:, cs] += dbu_acc
            dbin_ref[:, vs] += dbvv_acc

        zb = z_ref[:, o:o + DB]
        zc = z_ref[:, o + DB:o + 2 * DB]
        zx = z_ref[:, o + 2 * DB:]
        hc = zc * zx
        keep = (i > 0).astype(F32)
        p1 = zp_ref[7:8, o + DB:o + 2 * DB] * zp_ref[7:8, o + 2 * DB:] * keep
        p2 = zp_ref[6:7, o + DB:o + 2 * DB] * zp_ref[6:7, o + 2 * DB:] * keep
        rowi = lax.broadcasted_iota(jnp.int32, (tm, DB), 0)
        r1, r2 = _conv_shifted(hc, p1, p2, rowi)
        w0, w1, w2 = cw_ref[0:1, :], cw_ref[1:2, :], cw_ref[2:3, :]
        cv = w0 * r2 + w1 * r1 + w2 * hc
        dyb = dy_ref[:, DA:]
        dzb = dyb * cv
        dcv = dyb * zb
        more = (i < nblk - 1).astype(F32)
        n0 = dyn_ref[0:1, DA:] * zn_ref[0:1, o:o + DB] * more
        n1 = dyn_ref[1:2, DA:] * zn_ref[1:2, o:o + DB] * more
        f1 = jnp.where(rowi == tm - 1, n0, pltpu.roll(dcv, tm - 1, 0))
        f2 = jnp.where(rowi == tm - 1, n1, jnp.where(rowi == tm - 2, n0, pltpu.roll(dcv, tm - 2, 0)))
        dhc = w2 * dcv + w1 * f1 + w0 * f2
        dzc = dhc * zx
        dzx = dhc * zc
        dcw_ref[0:1, :] += jnp.sum(dcv * r2, axis=0, keepdims=True)
        dcw_ref[1:2, :] += jnp.sum(dcv * r1, axis=0, keepdims=True)
        dcw_ref[2:3, :] += jnp.sum(dcv * hc, axis=0, keepdims=True)
        dz_ref[:, o:o + DB] = dzb.astype(BF16)
        dz_ref[:, o + DB:o + 2 * DB] = dzc.astype(BF16)
        dz_ref[:, o + 2 * DB:] = dzx.astype(BF16)
        dbin_ref[:, o:o + DB] += jnp.sum(dzb, axis=0, keepdims=True)
        dbin_ref[:, o + DB:o + 2 * DB] += jnp.sum(dzc, axis=0, keepdims=True)
        dbin_ref[:, o + 2 * DB:] += jnp.sum(dzx, axis=0, keepdims=True)

    full3 = lambda i: (0, 0, 0)
    full2 = lambda i: (0, 0)
    prev8 = lambda i: (jnp.maximum(i * (tm // 8) - 1, 0), 0)
    next8 = lambda i: (jnp.minimum((i + 1) * (tm // 8), S // 8 - 1), 0)
    est = 2 * (tm * DIN * 4 + tm * (DA + DB) * 4 + tm * DIN * 2) + 16 * tm * DB * 4
    return pl.pallas_call(
        body, name="mix_bwd", grid=(nblk,),
        in_specs=[pl.BlockSpec((tm, DIN), lambda i: (i, 0)),
                  pl.BlockSpec((8, DIN), prev8), pl.BlockSpec((8, DIN), next8),
                  pl.BlockSpec((tm, DA + DB), lambda i: (i, 0)), pl.BlockSpec((8, DA + DB), next8),
                  pl.BlockSpec(ws.shape, full3), pl.BlockSpec(bsb.shape, full3),
                  pl.BlockSpec(gv.shape, full2), pl.BlockSpec(bv.shape, full2), pl.BlockSpec(cw8.shape, full2)],
        out_specs=[pl.BlockSpec((tm, DIN), lambda i: (i, 0)),
                   pl.BlockSpec(ws.shape, full3), pl.BlockSpec(ws.shape, full3),
                   pl.BlockSpec(gv.shape, full2), pl.BlockSpec(gv.shape, full2),
                   pl.BlockSpec(cw8.shape, full2), pl.BlockSpec((1, DIN), full2)],
        out_shape=[jax.ShapeDtypeStruct((S, DIN), BF16),
                   jax.ShapeDtypeStruct(ws.shape, F32), jax.ShapeDtypeStruct(ws.shape, F32),
                   jax.ShapeDtypeStruct(gv.shape, F32), jax.ShapeDtypeStruct(gv.shape, F32),
                   jax.ShapeDtypeStruct(cw8.shape, F32), jax.ShapeDtypeStruct((1, DIN), F32)],
        compiler_params=pltpu.CompilerParams(dimension_semantics=("arbitrary",),
                                             vmem_limit_bytes=_vmem_limit(est)),
    )(z, z, z, dy, dy, ws, bsb, gv, bv, cw8)


def _place():
    x, y, c = lax.axis_index("x"), lax.axis_index("y"), lax.axis_index("c")
    others = [(1 - x, y), (x, 1 - y), (1 - x, 1 - y)]
    return x, y, c, 2 * x + y, others, [2 * ox + oy for ox, oy in others]


def _half(c, rows_half):
    return pl.ds(pl.multiple_of(c * rows_half, 16), rows_half)


def _rcopy(src, dst, ssem, rsem, dev):
    return pltpu.make_async_remote_copy(src_ref=src, dst_ref=dst, send_sem=ssem, recv_sem=rsem,
                                        device_id=dev, device_id_type=MESH)


class _Comm:
    def __init__(self, aliased, ins, fresh, n_sems, stages, sink):
        self.aliased, self.ins, self.fresh = list(aliased), list(ins), list(fresh)
        self.n_sems, self.stages, self.sink = n_sems, stages, sink


def _deliver(comms, outs):
    pos = 0
    for cm in comms:
        al = outs[pos:pos + len(cm.aliased)]
        pos += len(cm.aliased)
        fr = outs[pos:pos + len(cm.fresh)]
        pos += len(cm.fresh)
        cm.sink(list(al), list(fr))


def _comm_call(name, comms):
    def body(*refs):
        pos = 0
        cm_ins, cm_al, cm_fr = [], [], []
        for cm in comms:
            pos += len(cm.aliased)
            cm_ins.append(refs[pos:pos + len(cm.ins)])
            pos += len(cm.ins)
        for cm in comms:
            cm_al.append(refs[pos:pos + len(cm.aliased)])
            pos += len(cm.aliased)
            cm_fr.append(refs[pos:pos + len(cm.fresh)])
            pos += len(cm.fresh)
        for i, cm in enumerate(comms):
            ssem, rsem = refs[pos + 2 * i], refs[pos + 2 * i + 1]
            for stage in cm.stages:
                for d in stage(cm_al[i], cm_ins[i], cm_fr[i], ssem, rsem, False):
                    d.start()
                for d in stage(cm_al[i], cm_ins[i], cm_fr[i], ssem, rsem, True):
                    d.wait()

    in_arrays, out_shapes, scratch, aliases = [], [], [], {}
    for cm in comms:
        for a in cm.aliased:
            aliases[len(in_arrays)] = len(out_shapes)
            in_arrays.append(a)
            out_shapes.append(jax.ShapeDtypeStruct(a.shape, a.dtype))
        in_arrays += cm.ins
        out_shapes += cm.fresh
        scratch += [pltpu.SemaphoreType.DMA((cm.n_sems,))] * 2
    outs = pl.pallas_call(body, name=name, in_specs=[ANY] * len(in_arrays), out_specs=[ANY] * len(out_shapes),
                          out_shape=out_shapes, scratch_shapes=scratch, input_output_aliases=aliases)(*in_arrays)
    _deliver(comms, outs)


def _c_gather(fulls, sink, ici, fwd, conv=None):
    n = len(fulls)
    n_ici = 3 * (n + (conv is not None)) if ici else 0

    def stage_ici(al, ins, fr, ssem, rsem, wait):
        x, y, c, chip, others, ochips = _place()
        ds = []
        for m in range(n + (conv is not None)):
            for k in range(3):
                i = 3 * m + k
                if m < n:
                    ah = al[m].shape[1] // 2
                    mine, lands = al[m].at[chip, _half(c, ah)], al[m].at[ochips[k], _half(c, ah)]
                else:
                    mine, lands = al[m].at[chip], al[m].at[ochips[k]]
                ds.append(_rcopy(mine, lands, ssem.at[i], rsem.at[i], (x, y, c)) if wait
                          else _rcopy(mine, mine, ssem.at[i], rsem.at[i], (*others[k], c)))
        return ds

    def stage_fwd(al, ins, fr, ssem, rsem, wait):
        x, y, c, chip, others, ochips = _place()
        ds = []
        for m in range(n):
            ah = al[m].shape[1] // 2
            for k in range(3):
                i = n_ici + 3 * m + k
                got = al[m].at[ochips[k], _half(c, ah)]
                ds.append(_rcopy(got, al[m].at[ochips[k], _half(1 - c, ah)], ssem.at[i], rsem.at[i], (x, y, c)) if wait
                          else _rcopy(got, got, ssem.at[i], rsem.at[i], (x, y, 1 - c)))
        return ds

    stages = ([stage_ici] if ici else []) + ([stage_fwd] if fwd else [])
    return _Comm(list(fulls) + ([conv] if conv is not None else []), [], [], n_ici + (3 * n if fwd else 0),
                 stages, sink)


def _c_rs_pair(parts, sink):
    def stage(al, ins, fr, ssem, rsem, wait):
        x, y, c, _, _, _ = _place()
        return [_rcopy(ins[m].at[:, _half(1 - c, ins[m].shape[1] // 2), :], fr[m], ssem.at[m], rsem.at[m],
                       (x, y, 1 - c)) for m in range(len(ins))]
    fresh = [jax.ShapeDtypeStruct((p.shape[0], p.shape[1] // 2, p.shape[2]), p.dtype) for p in parts]
    return _Comm([], parts, fresh, len(parts), [stage], sink)


def _c_rs_chips(psums, sink):
    def stage(al, ins, fr, ssem, rsem, wait):
        x, y, c, _, others, ochips = _place()
        return [_rcopy(ins[m].at[ochips[k]], fr[m].at[k], ssem.at[3 * m + k], rsem.at[3 * m + k], (*others[k], c))
                for m in range(len(ins)) for k in range(3)]
    fresh = [jax.ShapeDtypeStruct((3,) + p.shape[1:], p.dtype) for p in psums]
    return _Comm([], psums, fresh, 3 * len(psums), [stage], sink)


def _c_exchange(g2s, sink):
    def stage(al, ins, fr, ssem, rsem, wait):
        x, y, c, _, _, _ = _place()
        if wait:
            return [_rcopy(al[m].at[c], al[m].at[1 - c], ssem.at[m], rsem.at[m], (x, y, c)) for m in range(len(al))]
        return [_rcopy(al[m].at[c], al[m].at[c], ssem.at[m], rsem.at[m], (x, y, 1 - c)) for m in range(len(al))]
    return _Comm(g2s, [], [], len(g2s), [stage], sink)


def _allreduce_small(v):
    m_per, n = v.shape

    def body(x_ref, all_ref, sum_ref, send_sems, recv_sems, local_sem):
        x, y, c, _, chips, _ = _place()
        me, sibling = (x, y, c), (x, y, 1 - c)

        def rows(px, py, pc):
            return all_ref.at[pl.ds(pl.multiple_of((4 * px + 2 * py + pc) * m_per, 8), m_per), :]

        def copy(k, block, to, src=None):
            return pltpu.make_async_remote_copy(
                src_ref=rows(*block) if src is None else src, dst_ref=rows(*block),
                send_sem=send_sems.at[k], recv_sem=recv_sems.at[k], device_id=to, device_id_type=MESH)

        mine = pltpu.make_async_copy(x_ref, rows(*me), local_sem)
        mine.start()
        first = [copy(0, me, sibling, src=x_ref)]
        first += [copy(1 + j, me, (*chip, c), src=x_ref) for j, chip in enumerate(chips)]
        for cp in first:
            cp.start()
        passed = [copy(4 + j, (*chip, c), sibling) for j, chip in enumerate(chips)]
        for j, chip in enumerate(chips):
            copy(1 + j, (*chip, c), me).wait_recv()
            passed[j].start()
        copy(0, sibling, me).wait_recv()
        for j, chip in enumerate(chips):
            copy(4 + j, (*chip, 1 - c), me).wait_recv()
        for cp in first + passed:
            cp.wait_send()
        mine.wait()
        acc = all_ref[0:m_per, :]
        for d in range(1, 8):
            acc = acc + all_ref[d * m_per:(d + 1) * m_per, :]
        sum_ref[...] = acc

    vm = pl.BlockSpec(memory_space=pltpu.VMEM)
    return pl.pallas_call(
        body, name="allreduce_small",
        out_shape=[jax.ShapeDtypeStruct((8 * m_per, n), v.dtype), jax.ShapeDtypeStruct((m_per, n), v.dtype)],
        in_specs=[vm], out_specs=[vm, vm],
        scratch_shapes=[pltpu.SemaphoreType.DMA((7,)), pltpu.SemaphoreType.DMA((7,)), pltpu.SemaphoreType.DMA],
    )(v)[1]


def _sum_pair(name, part, sib, c_idx):
    _, A, B = part.shape
    ah = A // 2
    br = _rows(ah, B * 2, 2 * 2 ** 20)

    def body(c_ref, a_ref, b_ref, o_ref):
        o_ref[...] = (a_ref[...].astype(F32) + b_ref[...].astype(F32)).astype(BF16)

    gs = pltpu.PrefetchScalarGridSpec(
        num_scalar_prefetch=1, grid=(N_CHIPS, ah // br),
        in_specs=[pl.BlockSpec((None, None, br, B), lambda j, i, c: (j, c[0], i, 0)),
                  pl.BlockSpec((None, br, B), lambda j, i, c: (j, i, 0))],
        out_specs=pl.BlockSpec((None, br, B), lambda j, i, c: (j, i, 0)))
    return pl.pallas_call(body, name=name, grid_spec=gs,
                          out_shape=jax.ShapeDtypeStruct((N_CHIPS, ah, B), BF16))(
                              c_idx, part.reshape(N_CHIPS, 2, ah, B), sib)


def _sum_chips(name, psum, recv, place_idx):
    _, ah, B = psum.shape
    br = _rows(ah, B * 4, 2 * 2 ** 20)

    def body(c_ref, p_ref, r0_ref, r1_ref, r2_ref, o_ref):
        o_ref[...] = ((p_ref[...].astype(F32) + r0_ref[...].astype(F32)) + r1_ref[...].astype(F32)) \
            + r2_ref[...].astype(F32)

    gs = pltpu.PrefetchScalarGridSpec(
        num_scalar_prefetch=1, grid=(ah // br,),
        in_specs=[pl.BlockSpec((None, br, B), lambda i, c: (c[0], i, 0))]
        + [pl.BlockSpec((None, br, B), functools.partial(lambda i, c, k: (k, i, 0), k=k)) for k in range(3)],
        out_specs=pl.BlockSpec((None, br, B), lambda i, c: (c[1], i, 0)))
    return pl.pallas_call(body, name=name, grid_spec=gs,
                          out_shape=jax.ShapeDtypeStruct((2, ah, B), F32))(place_idx, psum, recv, recv, recv)


def _cast_into(name, a, place_idx):
    A, B = a.shape
    br = _rows(A, B * 4)

    def body(c_ref, a_ref, o_ref):
        o_ref[...] = a_ref[...].astype(BF16)

    gs = pltpu.PrefetchScalarGridSpec(
        num_scalar_prefetch=1, grid=(A // br,),
        in_specs=[pl.BlockSpec((br, B), lambda i, c: (i, 0))],
        out_specs=pl.BlockSpec((None, br, B), lambda i, c: (c[0], i, 0)))
    return pl.pallas_call(body, name=name, grid_spec=gs,
                          out_shape=jax.ShapeDtypeStruct((N_CHIPS, A, B), BF16))(place_idx, a)


def _adamw_small(gsum, gconv, offsets, ws, ms, vs):
    n = len(ws)

    def body(*refs):
        g_ref, gc_ref = refs[0], refs[1]
        w_refs = refs[2:2 + n]
        m_refs = refs[2 + n:2 + 2 * n]
        v_refs = refs[2 + 2 * n:2 + 3 * n]
        outs = refs[2 + 3 * n:]
        for i in range(n):
            rows = w_refs[i].shape[0]
            g = gc_ref[...] if offsets[i] is None else g_ref[offsets[i]:offsets[i] + rows, :]
            d, m2, v2 = _adamw_math(w_refs[i][...], g, m_refs[i][...], v_refs[i][...])
            outs[4 * i][...] = g
            outs[4 * i + 1][...] = d
            outs[4 * i + 2][...] = m2
            outs[4 * i + 3][...] = v2

    vm = pl.BlockSpec(memory_space=pltpu.VMEM)
    out_shape = []
    for w in ws:
        out_shape += [jax.ShapeDtypeStruct(w.shape, F32)] * 4
    return pl.pallas_call(body, name="adamw_small", in_specs=[vm] * (2 + 3 * n), out_specs=[vm] * (4 * n),
                          out_shape=out_shape)(gsum, gconv, *ws, *ms, *vs)


def _ident(accs, tes, res):
    return [accs[0]]


def _ffn_up(name, hb, wg4, wu4, comms=()):
    S, D = hb.shape

    def epi(accs, tes, res):
        g, u = accs
        return [g, u, g * jax.nn.sigmoid(g) * u]

    return _mm(name, 'nn', [hb], [wg4, wu4], [(0, 0, 0), (0, 1, 1)], 2, epi, [BF16] * 3,
               bm=_blk(S, 512), bn=wg4.shape[2], bk=D, b_blocked=True, n_outer=True, comms=comms)


def _ffn_gate(name, hb, wg4, comms=()):
    S, D = hb.shape
    return _mm(name, 'nn', [hb], [wg4], [(0, 0, 0)], 1, _ident, [BF16],
               bm=_blk(S, 1024), bn=wg4.shape[2], bk=D, b_blocked=True, n_outer=True, comms=comms)[0]


def _ffn_upmul(name, hb, wu4, gb, comms=()):
    S, D = hb.shape

    def epi(accs, tes, res):
        g = tes[0].astype(F32)
        return [accs[0], g * jax.nn.sigmoid(g) * accs[0]]

    return _mm(name, 'nn', [hb], [wu4], [(0, 0, 0)], 1, epi, [BF16, BF16],
               bm=_blk(S, 1024), bn=wu4.shape[2], bk=D, tile_extras=[gb], b_blocked=True, n_outer=True, comms=comms)


def _ffn_down(name, ab, wd, hres, comms=()):
    S, F = ab.shape
    D = wd.shape[1]
    return _mm(name, 'nn', [ab], [wd], [(0, 0, 0)], 1, lambda accs, tes, res: [ALPHA * tes[0] + 0.5 * accs[0]],
               [F32], bm=_blk(S, 1024), bn=_blk(D, 512), bk=F, tile_extras=[hres], comms=comms)[0]


def _ffn_da(name, dsub, wd, gb, ub, comms=()):
    S, D = dsub.shape
    F = wd.shape[0]

    def epi(accs, tes, res):
        da = accs[0]
        g = tes[0].astype(F32)
        u = tes[1].astype(F32)
        s = jax.nn.sigmoid(g)
        return [da * u * (s * (1.0 + g * (1.0 - s))), da * (g * s)]

    return _mm(name, 'nt', [dsub], [wd], [(0, 0, 0)], 1, epi, [BF16, BF16],
               bm=_blk(S, 512), bn=F // N_CHIPS, bk=D, tile_extras=[gb, ub], comms=comms)


def _dw_rows(name, ab, dsub, cs, comms=()):
    S, D = dsub.shape
    return _mm(name, 'tn', [ab], [dsub], [(0, 0, 0)], 1, _ident, [BF16],
               bm=cs, bn=_blk(D, 1024), bk=_blk(S, 2048), comms=comms)[0].reshape(N_CHIPS, cs, D)


def _dw_cols(name, hb, dxb, cs, comms=()):
    S, D = hb.shape
    return _mm(name, 'tn', [hb], [dxb], [(0, 0, 0)], 1, _ident, [BF16],
               bm=_blk(D, 1024), bn=cs, bk=_blk(S, 2048), out_blocked=True, comms=comms)[0]


def _ffn_dh(name, dgb, dub, wg4, wu4, dres, comms=()):
    S = dgb.shape[0]
    D = wg4.shape[1]
    return _mm(name, 'nt', [dgb, dub], [wg4, wu4], [(0, 0, 0), (1, 1, 0)], 1,
               lambda accs, tes, res: [accs[0] + tes[0]], [F32],
               bm=_blk(S, 1024), bn=_blk(D, 1024), bk=wg4.shape[2], tile_extras=[dres], b_blocked=True,
               comms=comms)[0]


def _step(p):
    x = p['x'][0]
    target = p['loss_target'][0]
    S, D = x.shape
    cx, cy, cc = lax.axis_index("x"), lax.axis_index("y"), lax.axis_index("c")
    chip = 2 * cx + cy
    c_idx = jnp.reshape(cc, (1,)).astype(jnp.int32)
    place_idx = jnp.stack([chip, cc]).astype(jnp.int32)

    w_s = p['w_s'][0]
    H = w_s.shape[0]
    DA = H * HEAD
    bsb = jnp.broadcast_to(p['b_s'][0][:, :, None], (H, HEAD, HEAD))
    conv_shard = p['conv_w'][0]
    conv8 = jnp.zeros((8, conv_shard.shape[1]), F32).at[:CONV_TAPS].set(conv_shard)

    W = {n: _cast_into(f"cast_{n}", p[n][0], place_idx) for n in BIG}
    conv = {'full': lax.dynamic_update_slice(jnp.zeros((N_CHIPS,) + conv8.shape, F32), conv8[None], (chip, 0, 0))}

    def gathered(names, with_conv=False):
        def sink(al, fr):
            W.update(zip(names, al))
            if with_conv:
                conv['full'] = al[len(names)]
        return sink

    def c_gather(names, ici, fwd, with_conv=False):
        return _c_gather([W[n] for n in names], gathered(names, with_conv), ici, fwd,
                         conv['full'] if with_conv else None)

    _comm_call("gather_gate", [c_gather(['ffa_gate'], True, True)])
    cs_ff = W['ffa_gate'].shape[2]
    F = N_CHIPS * cs_ff

    xb = _cast_bf16("cast_x", x)
    ga = _ffn_gate("ffa_gate", xb, W['ffa_gate'],
                   comms=[c_gather(['ffa_up'], True, True), c_gather(['ffa_down'], True, False)])
    ua, aa = _ffn_upmul("ffa_up", xb, W['ffa_up'], ga,
                        comms=[c_gather(['ffa_down'], False, True),
                               c_gather(['w_in', 'w_out'], True, False, with_conv=True)])
    cw8 = jnp.transpose(conv['full'], (1, 0, 2)).reshape(8, DA)
    wd_a = W['ffa_down'].reshape(F, D)
    r1 = _ffn_down("ffa_down", aa, wd_a, x,
                   comms=[c_gather(['w_in', 'w_out'], False, True), c_gather(['ffc_gate'], True, False)])
    h1, h1b = _ln_fwd("ln_a", r1, p['ln_a_g'], p['ln_a_b'])
    w_in4 = W['w_in']
    cs_in = w_in4.shape[2]
    z = _mm("w_in", 'nn', [h1b], [w_in4], [(0, 0, 0)], 1, lambda accs, tes, res: [accs[0] + res[0]], [F32],
            bm=_blk(S, 1024), bn=cs_in, bk=D, row_extras=[p['b_in']], b_blocked=True, n_outer=True,
            comms=[c_gather(['ffc_gate'], False, True), c_gather(['ffc_up'], True, False)])[0]
    w_out = W['w_out'].reshape(DA * 2, D)
    tm = _blk(S, 256)
    yb = _mix_fwd(z, w_s, bsb, p['ln_v_g'], p['ln_v_b'], cw8, tm)
    r2 = _mm("w_out", 'nn', [yb], [w_out], [(0, 0, 0)], 1,
             lambda accs, tes, res: [accs[0] + res[0] + ALPHA * tes[0]], [F32],
             bm=_blk(S, 1024), bn=_blk(D, 1024), bk=2 * DA, tile_extras=[h1], row_extras=[p['b_out']],
             comms=[c_gather(['ffc_up'], False, True), c_gather(['ffc_down'], True, False)])[0]
    h2, h2b = _ln_fwd("ln_m", r2, p['ln_m_g'], p['ln_m_b'])
    gc, uc, ac = _ffn_up("ffc_up", h2b, W['ffc_gate'], W['ffc_up'], comms=[c_gather(['ffc_down'], False, True)])
    wd_c = W['ffc_down'].reshape(F, D)
    r3 = _ffn_down("ffc_down", ac, wd_c, h2)

    pg, sg, sib, psum, recv, g2 = {}, {}, {}, {}, {}, {}

    def c_pair(names):
        return _c_rs_pair([pg[n] for n in names], lambda al, fr: sib.update(zip(names, fr)))

    def c_chips(names):
        return _c_rs_chips([psum[n] for n in names], lambda al, fr: recv.update(zip(names, fr)))

    def c_swap(names):
        return _c_exchange([g2[n] for n in names], lambda al, fr: g2.update(zip(names, al)))

    def sum_pair(*names):
        for n in names:
            psum[n] = _sum_pair(f"sum_pair_{n}", pg[n], sib[n], c_idx)

    def sum_chips(*names):
        for n in names:
            g2[n] = _sum_chips(f"sum_chips_{n}", psum[n], recv[n], place_idx)

    dres3, dsub3, sg['ln_c_g'], sg['ln_c_b'], lsum = _ln_loss_bwd(
        "ln_c_loss_bwd", r3, target, p['ln_c_g'], p['ln_c_b'], 0.5)

    dgc, duc = _ffn_da("ffc_da", dsub3, wd_c, gc, uc)
    pg['ffc_down'] = _dw_rows("ffc_dwd", ac, dsub3, cs_ff)
    pg['ffc_gate'] = _dw_cols("ffc_dwg", h2b, dgc, cs_ff, comms=[c_pair(['ffc_down'])])
    sum_pair('ffc_down')
    pg['ffc_up'] = _dw_cols("ffc_dwu", h2b, duc, cs_ff, comms=[c_pair(['ffc_gate']), c_chips(['ffc_down'])])
    sum_pair('ffc_gate')
    dh2 = _ffn_dh("ffc_dh", dgc, duc, W['ffc_gate'], W['ffc_up'], dres3,
                  comms=[c_pair(['ffc_up']), c_chips(['ffc_gate'])])
    sum_pair('ffc_up')
    sum_chips('ffc_down', 'ffc_gate')

    dres2, dmix, sg['ln_m_g'], sg['ln_m_b'], sg['b_out'] = _ln_bwd("ln_m_bwd", r2, dh2, p['ln_m_g'], 1.0)
    pg['w_out'] = _mm("dw_out", 'tn', [yb], [dmix], [(0, 0, 0)], 1, _ident, [BF16],
                      bm=_blk(2 * DA, 1024), bn=_blk(D, 1024), bk=_blk(S, 2048),
                      comms=[c_swap(['ffc_down', 'ffc_gate'])])[0].reshape(N_CHIPS, 2 * DA // 4, D)
    dy = _mm("dy", 'nt', [dmix], [w_out], [(0, 0, 0)], 1, _ident, [F32],
             bm=_blk(S, 1024), bn=_blk(2 * DA, 1024), bk=D, comms=[c_pair(['w_out'])])[0]
    sum_pair('w_out')
    dz, dws, dbs, sg['ln_v_g'], sg['ln_v_b'], dcw, sg['b_in'] = _mix_bwd(
        z, dy, w_s, bsb, p['ln_v_g'], p['ln_v_b'], cw8, tm)
    sg['w_s'] = dws
    sg['b_s'] = dbs[:, :, 0]
    pg['w_in'] = _dw_cols("dw_in", h1b, dz, cs_in, comms=[c_chips(['ffc_up']), c_chips(['w_out'])])
    sum_chips('ffc_up', 'w_out')
    dh1 = _mm("dh1", 'nt', [dz], [w_in4], [(0, 0, 0)], 1, lambda accs, tes, res: [accs[0] + tes[0]], [F32],
              bm=_blk(S, 1024), bn=_blk(D, 1024), bk=cs_in, tile_extras=[dres2], b_blocked=True,
              comms=[c_pair(['w_in']), c_swap(['ffc_up', 'w_out'])])[0]
    sum_pair('w_in')

    dres1, dsub1, sg['ln_a_g'], sg['ln_a_b'], _ = _ln_bwd("ln_a_bwd", r1, dh1, p['ln_a_g'], 0.5)
    dga, dua = _ffn_da("ffa_da", dsub1, wd_a, ga, ua, comms=[c_chips(['w_in'])])
    sum_chips('w_in')
    pg['ffa_down'] = _dw_rows("ffa_dwd", aa, dsub1, cs_ff, comms=[c_swap(['w_in'])])
    pg['ffa_gate'] = _dw_cols("ffa_dwg", xb, dga, cs_ff, comms=[c_pair(['ffa_down'])])
    sum_pair('ffa_down')
    pg['ffa_up'] = _dw_cols("ffa_dwu", xb, dua, cs_ff, comms=[c_pair(['ffa_gate']), c_chips(['ffa_down'])])
    sum_pair('ffa_gate')
    _comm_call("pair_ffa_up", [c_pair(['ffa_up'])])
    sum_pair('ffa_up')
    sum_chips('ffa_down')
    dx = _ffn_dh("ffa_dh", dga, dua, W['ffa_gate'], W['ffa_up'], dres1,
                 comms=[c_chips(['ffa_gate']), c_chips(['ffa_up']), c_swap(['ffa_down'])])
    sum_chips('ffa_gate', 'ffa_up')
    _comm_call("swap_ffa", [c_swap(['ffa_gate', 'ffa_up'])])

    out = {}
    for n in BIG:
        g, d, m2, v2 = _adamw(f"adamw_{n}", p[n][0], g2[n].reshape(p[n][0].shape), p['m_' + n][0], p['v_' + n][0])
        shp = p[n].shape
        out[n] = (g.reshape(shp), d.reshape(shp), m2.reshape(shp), v2.reshape(shp))

    def as_rows(a):
        a = a.reshape(-1, 128)
        pad = (-a.shape[0]) % 8
        return jnp.pad(a, ((0, pad), (0, 0))) if pad else a

    pieces, offsets, off = [], {}, 0
    for n in SMALL + ['conv_w']:
        piece = as_rows(dcw[:CONV_TAPS] if n == 'conv_w' else sg[n])
        offsets[n] = off
        off += piece.shape[0]
        pieces.append(piece)
    packed, lsum, _ = lax.optimization_barrier((jnp.concatenate(pieces, axis=0), lsum, g2['ffa_up']))
    gsum = _allreduce_small(packed)
    gsum, lsum = lax.optimization_barrier((gsum, lsum))
    loss = lax.psum(0.5 * jnp.sum(lsum) / D, ("x", "y", "c"))
    n_conv = CONV_TAPS * DA // 128
    conv_sum = gsum[offsets['conv_w']:offsets['conv_w'] + n_conv].reshape(CONV_TAPS, DA)
    cw = conv_shard.shape[1]
    gconv = lax.dynamic_slice_in_dim(conv_sum, (2 * cx + cy) * cw, cw, axis=1).reshape(-1, 128)
    names = SMALL + ['conv_w']
    rows128 = lambda a: a.reshape(-1, 128)
    res = _adamw_small(gsum, gconv, [offsets[n] if n != 'conv_w' else None for n in names],
                       [rows128(p[n]) for n in names], [rows128(p['m_' + n]) for n in names],
                       [rows128(p['v_' + n]) for n in names])
    for i, n in enumerate(names):
        out[n] = tuple(r.reshape(p[n].shape) for r in res[4 * i:4 * i + 4])

    return (loss, dx.reshape(p['x'].shape), *[out[n][0] for n in WEIGHTS], *[out[n][1] for n in WEIGHTS],
            *[out[n][2] for n in WEIGHTS], *[out[n][3] for n in WEIGHTS])


def kernel(x, ffa_gate, ffa_up, ffa_down, ln_a_g, ln_a_b, w_in, b_in, w_s, b_s, ln_v_g, ln_v_b, conv_w, w_out, b_out, ln_m_g, ln_m_b, ffc_gate, ffc_up, ffc_down, ln_c_g, ln_c_b, loss_target, m_ffa_gate, m_ffa_up, m_ffa_down, m_ln_a_g, m_ln_a_b, m_w_in, m_b_in, m_w_s, m_b_s, m_ln_v_g, m_ln_v_b, m_conv_w, m_w_out, m_b_out, m_ln_m_g, m_ln_m_b, m_ffc_gate, m_ffc_up, m_ffc_down, m_ln_c_g, m_ln_c_b, v_ffa_gate, v_ffa_up, v_ffa_down, v_ln_a_g, v_ln_a_b, v_w_in, v_b_in, v_w_s, v_b_s, v_ln_v_g, v_ln_v_b, v_conv_w, v_w_out, v_b_out, v_ln_m_g, v_ln_m_b, v_ffc_gate, v_ffc_up, v_ffc_down, v_ln_c_g, v_ln_c_b):
    return _step(dict(locals()))
```

```python
import functools
import math

import jax
import jax.numpy as jnp
from jax import lax
from jax.experimental import pallas as pl
from jax.experimental.pallas import tpu as pltpu

F32 = jnp.float32
BF16 = jnp.bfloat16
MESH = pl.DeviceIdType.MESH
ANY = pl.BlockSpec(memory_space=pl.ANY)

HEAD = 128
CONV_TAPS = 3
LN_EPS = 1e-5
ALPHA = float(2.0 ** 0.25)
ADAM_LR, ADAM_B1, ADAM_B2, ADAM_EPS, ADAM_WD, ADAM_STEP = 0.001, 0.9, 0.999, 1e-08, 0.01, 10

V7X_VMEM_BYTES = 64 * 2 ** 20
N_CHIPS = 4

BIG = ['ffa_gate', 'ffa_up', 'ffa_down', 'w_in', 'w_out', 'ffc_gate', 'ffc_up', 'ffc_down']
WEIGHTS = ['ffa_gate', 'ffa_up', 'ffa_down', 'ln_a_g', 'ln_a_b', 'w_in', 'b_in', 'w_s', 'b_s', 'ln_v_g', 'ln_v_b',
           'conv_w', 'w_out', 'b_out', 'ln_m_g', 'ln_m_b', 'ffc_gate', 'ffc_up', 'ffc_down', 'ln_c_g', 'ln_c_b']
SMALL = [n for n in WEIGHTS if n not in BIG and n != 'conv_w']


def _vmem_limit(estimate_bytes):
    return int(min(max(estimate_bytes * 1.3, 32 * 2 ** 20), V7X_VMEM_BYTES - 6 * 2 ** 20))


def _blk(dim, pref, mult=128):
    if dim <= pref:
        return dim
    best = None
    for d in range(mult, pref + 1, mult):
        if dim % d == 0:
            best = d
    assert best is not None, (dim, pref)
    return best


def _rows(n_rows, row_bytes, target=4 * 2 ** 20):
    return _blk(n_rows, max(16, target // row_bytes), 16)


def _sigmoid(x):
    return 0.5 * jnp.tanh(0.5 * x) + 0.5


def _gelu(x):
    c = math.sqrt(2.0 / math.pi)
    return 0.5 * x * (1.0 + jnp.tanh(c * (x + 0.044715 * (x * x * x))))


def _gelu_and_grad(x):
    c = math.sqrt(2.0 / math.pi)
    t = jnp.tanh(c * (x + 0.044715 * (x * x * x)))
    g = 0.5 * x * (1.0 + t)
    dg = 0.5 * (1.0 + t) + 0.5 * x * (1.0 - t * t) * (c * (1.0 + 3.0 * 0.044715 * (x * x)))
    return g, dg


def _ln_stats(r):
    mu = jnp.mean(r, axis=-1, keepdims=True)
    d = r - mu
    var = jnp.mean(d * d, axis=-1, keepdims=True)
    rstd = lax.rsqrt(var + LN_EPS)
    return d * rstd, rstd


_DIMS = {'nn': (((1,), (0,)), ((), ())), 'nt': (((1,), (1,)), ((), ())), 'tn': (((0,), (0,)), ((), ()))}


def _mm(name, mode, a_list, b_list, pairs, n_acc, epilogue, out_dtypes, bm, bn, bk,
        tile_extras=(), row_extras=(), b_blocked=False, out_blocked=False, n_outer=False, comms=()):
    a0, b0 = a_list[0], b_list[0]
    if mode == 'nn':
        M, K = a0.shape
        N = b0.shape[0] * b0.shape[2] if b_blocked else b0.shape[1]
    elif mode == 'nt':
        M, K = a0.shape
        N = b0.shape[1] if b_blocked else b0.shape[0]
    else:
        K, M = a0.shape
        N = b0.shape[1]
        assert not b_blocked
    assert M % bm == 0 and N % bn == 0 and K % bk == 0, (name, M, N, K, bm, bn, bk)
    gm, gn, gk = M // bm, N // bn, K // bk
    if b_blocked:
        assert (bn if mode == 'nn' else bk) == b0.shape[2], name
    dims = _DIMS[mode]

    def spec(shape, f):
        if n_outer:
            return pl.BlockSpec(shape, lambda g0, g1, g2: f(g1, g0, g2))
        return pl.BlockSpec(shape, f)

    if mode == 'tn':
        a_spec = spec((bk, bm), lambda i, j, k: (k, i))
    else:
        a_spec = spec((bm, bk), lambda i, j, k: (i, k))
    if mode == 'nn':
        b_spec = (spec((None, bk, bn), lambda i, j, k: (j, k, 0)) if b_blocked
                  else spec((bk, bn), lambda i, j, k: (k, j)))
    elif mode == 'nt':
        b_spec = (spec((None, bn, bk), lambda i, j, k: (k, j, 0)) if b_blocked
                  else spec((bn, bk), lambda i, j, k: (j, k)))
    else:
        b_spec = spec((bk, bn), lambda i, j, k: (k, j))
    te_spec = spec((bm, bn), lambda i, j, k: (i, j))
    re_spec = spec((1, bn), lambda i, j, k: (0, j))
    if out_blocked:
        o_spec = spec((None, bm, bn), lambda i, j, k: (j, i, 0))
        o_shape = (gn, M, bn)
    else:
        o_spec = te_spec
        o_shape = (M, N)

    na, nb, nte, nre, no = len(a_list), len(b_list), len(tile_extras), len(row_extras), len(out_dtypes)

    n_scratch_acc = n_acc if gk > 1 else 0
    grid = (gn, gm, gk) if n_outer else (gm, gn, gk)
    cm_in_arrays, cm_out_shapes, cm_scratch, aliases = _comm_operands(comms, na + nb + nte + nre, no)
    n_cm_in, n_cm_out = len(cm_in_arrays), len(cm_out_shapes)

    def body(*refs):
        a_refs = refs[:na]
        b_refs = refs[na:na + nb]
        te_refs = refs[na + nb:na + nb + nte]
        re_refs = refs[na + nb + nte:na + nb + nte + nre]
        n_in = na + nb + nte + nre
        o_refs = refs[n_in + n_cm_in:n_in + n_cm_in + no]
        acc_refs = refs[n_in + n_cm_in + no + n_cm_out:n_in + n_cm_in + no + n_cm_out + n_scratch_acc]
        hosted = _comm_refs(comms, refs[n_in:n_in + n_cm_in], refs[n_in + n_cm_in + no:n_in + n_cm_in + no + n_cm_out],
                            refs[n_in + n_cm_in + no + n_cm_out + n_scratch_acc:])
        _comm_first_step(comms, hosted, grid)

        def finish(accs):
            outs = epilogue(accs, [r[...] for r in te_refs], [r[...] for r in re_refs])
            for o_ref, o in zip(o_refs, outs):
                o_ref[...] = o.astype(o_ref.dtype)

        if gk == 1:
            parts = [None] * n_acc
            for ai, bi, ci in pairs:
                d = lax.dot_general(a_refs[ai][...], b_refs[bi][...], dims, preferred_element_type=F32)
                parts[ci] = d if parts[ci] is None else parts[ci] + d
            finish(parts)
        else:
            kk = pl.program_id(2)

            @pl.when(kk == 0)
            def _():
                for r in acc_refs:
                    r[...] = jnp.zeros_like(r)

            for ai, bi, ci in pairs:
                acc_refs[ci][...] += lax.dot_general(a_refs[ai][...], b_refs[bi][...], dims,
                                                     preferred_element_type=F32)

            @pl.when(kk == gk - 1)
            def _():
                finish([r[...] for r in acc_refs])

        _comm_last_step(comms, hosted, grid)

    isz = lambda a: jnp.dtype(a.dtype).itemsize
    est = 2 * (sum(bm * bk * isz(a) for a in a_list) + sum(bk * bn * isz(b) for b in b_list)
               + sum(bm * bn * isz(t) for t in tile_extras)
               + sum(bm * bn * jnp.dtype(d).itemsize for d in out_dtypes))
    est += (2 * n_acc + 2) * bm * bn * 4
    outs = pl.pallas_call(
        body, name=name, grid=grid,
        in_specs=[a_spec] * na + [b_spec] * nb + [te_spec] * nte + [re_spec] * nre + [ANY] * len(cm_in_arrays),
        out_specs=[o_spec] * no + [ANY] * len(cm_out_shapes),
        out_shape=[jax.ShapeDtypeStruct(o_shape, d) for d in out_dtypes] + cm_out_shapes,
        scratch_shapes=[pltpu.VMEM((bm, bn), F32)] * n_scratch_acc + cm_scratch,
        input_output_aliases=aliases,
        compiler_params=pltpu.CompilerParams(dimension_semantics=("parallel", "parallel", "arbitrary"),
                                             vmem_limit_bytes=_vmem_limit(est)),
    )(*a_list, *b_list, *tile_extras, *row_extras, *cm_in_arrays)
    _deliver(comms, outs[no:])
    return outs[:no]


def _rowwise(name, fn, br, row_ins, vec_ins, row_outs, acc_outs=(), comms=()):
    S = row_ins[0].shape[0]
    assert S % br == 0, (name, S, br)
    nr, nv, no, nacc = len(row_ins), len(vec_ins), len(row_outs), len(acc_outs)
    grid = (S // br,)
    cm_in_arrays, cm_out_shapes, cm_scratch, aliases = _comm_operands(comms, nr + nv, no + nacc)
    n_cm_in, n_cm_out = len(cm_in_arrays), len(cm_out_shapes)

    def body(*refs):
        rin = refs[:nr]
        vin = refs[nr:nr + nv]
        o0 = nr + nv + n_cm_in
        rout = refs[o0:o0 + no]
        aout = refs[o0 + no:o0 + no + nacc]
        hosted = _comm_refs(comms, refs[nr + nv:o0], refs[o0 + no + nacc:o0 + no + nacc + n_cm_out],
                            refs[o0 + no + nacc + n_cm_out:])
        _comm_first_step(comms, hosted, grid)
        if nacc:
            @pl.when(pl.program_id(0) == 0)
            def _():
                for a in aout:
                    a[...] = jnp.zeros_like(a)
        fn(rin, vin, rout, aout)
        _comm_last_step(comms, hosted, grid)

    def full_spec(shape):
        nd = len(shape)
        return pl.BlockSpec(shape, lambda i: (0,) * nd)

    est = 2 * (sum(br * a.shape[1] * jnp.dtype(a.dtype).itemsize for a in row_ins)
               + sum(br * c * jnp.dtype(d).itemsize for c, d in row_outs))
    est += 6 * br * max(a.shape[1] for a in row_ins) * 4
    outs = pl.pallas_call(
        body, name=name, grid=grid,
        in_specs=[pl.BlockSpec((br, a.shape[1]), lambda i: (i, 0)) for a in row_ins]
        + [full_spec(v.shape) for v in vec_ins] + [ANY] * n_cm_in,
        out_specs=[pl.BlockSpec((br, c), lambda i: (i, 0)) for c, _ in row_outs]
        + [full_spec(s) for s, _ in acc_outs] + [ANY] * n_cm_out,
        out_shape=[jax.ShapeDtypeStruct((S, c), d) for c, d in row_outs]
        + [jax.ShapeDtypeStruct(s, d) for s, d in acc_outs] + cm_out_shapes,
        scratch_shapes=cm_scratch,
        input_output_aliases=aliases,
        compiler_params=pltpu.CompilerParams(dimension_semantics=("arbitrary",),
                                             vmem_limit_bytes=_vmem_limit(est)),
    )(*row_ins, *vec_ins, *cm_in_arrays)
    _deliver(comms, outs[no + nacc:])
    return outs[:no + nacc]


def _cast_bf16(name, a, comms=()):
    def fn(rin, vin, rout, aout):
        rout[0][...] = rin[0][...].astype(BF16)
    return _rowwise(name, fn, _rows(a.shape[0], a.shape[1] * 4), [a], [], [(a.shape[1], BF16)], comms=comms)[0]


def _ln_fwd(name, r, g, b):
    def fn(rin, vin, rout, aout):
        xhat, _ = _ln_stats(rin[0][...])
        h = xhat * vin[0][...] + vin[1][...]
        rout[0][...] = h
        rout[1][...] = h.astype(BF16)
    D = r.shape[1]
    return _rowwise(name, fn, _rows(r.shape[0], D * 4, 2 * 2 ** 20), [r], [g, b], [(D, F32), (D, BF16)])


def _ln_loss_bwd(name, r, target, g, b, sub_scale):
    D = r.shape[1]

    def fn(rin, vin, rout, aout):
        xhat, rstd = _ln_stats(rin[0][...])
        gain = vin[0][...]
        err = xhat * gain + vin[1][...] - rin[1][...]
        dhv = err * (1.0 / D)
        dxh = dhv * gain
        m1 = jnp.mean(dxh, axis=-1, keepdims=True)
        m2 = jnp.mean(dxh * xhat, axis=-1, keepdims=True)
        dr = rstd * (dxh - m1 - xhat * m2)
        rout[0][...] = ALPHA * dr
        rout[1][...] = (sub_scale * dr).astype(BF16)
        aout[0][...] += jnp.sum(dhv * xhat, axis=0, keepdims=True)
        aout[1][...] += jnp.sum(dhv, axis=0, keepdims=True)
        aout[2][...] += jnp.sum(err * err, axis=0, keepdims=True)
    return _rowwise(name, fn, _rows(r.shape[0], D * 4, 2 * 2 ** 20), [r, target], [g, b],
                    [(D, F32), (D, BF16)], [((1, D), F32)] * 3)


def _ln_bwd(name, r, dh, g, sub_scale):
    D = r.shape[1]

    def fn(rin, vin, rout, aout):
        xhat, rstd = _ln_stats(rin[0][...])
        dhv = rin[1][...]
        dxh = dhv * vin[0][...]
        m1 = jnp.mean(dxh, axis=-1, keepdims=True)
        m2 = jnp.mean(dxh * xhat, axis=-1, keepdims=True)
        dr = rstd * (dxh - m1 - xhat * m2)
        rout[0][...] = ALPHA * dr
        rout[1][...] = (sub_scale * dr).astype(BF16)
        aout[0][...] += jnp.sum(dhv * xhat, axis=0, keepdims=True)
        aout[1][...] += jnp.sum(dhv, axis=0, keepdims=True)
        aout[2][...] += jnp.sum(dr, axis=0, keepdims=True)
    return _rowwise(name, fn, _rows(r.shape[0], D * 4, 2 * 2 ** 20), [r, dh], [g],
                    [(D, F32), (D, BF16)], [((1, D), F32)] * 3)


def _adamw_math(w, g, m, v):
    m2 = ADAM_B1 * m + (1.0 - ADAM_B1) * g
    v2 = ADAM_B2 * v + (1.0 - ADAM_B2) * (g * g)
    m_hat = m2 / (1.0 - ADAM_B1 ** ADAM_STEP)
    v_hat = v2 / (1.0 - ADAM_B2 ** ADAM_STEP)
    delta = -ADAM_LR * (m_hat / (jnp.sqrt(v_hat) + ADAM_EPS) + ADAM_WD * w)
    return delta, m2, v2


def _adamw(name, w, g, m, v):
    def fn(rin, vin, rout, aout):
        gv = rin[1][...]
        d, m2, v2 = _adamw_math(rin[0][...], gv, rin[2][...], rin[3][...])
        rout[0][...] = gv
        rout[1][...] = d
        rout[2][...] = m2
        rout[3][...] = v2
    C = w.shape[1]
    return _rowwise(name, fn, _rows(w.shape[0], C * 4, 2 * 2 ** 20), [w, g, m, v], [], [(C, F32)] * 4)


def _conv_shifted(hc, p1, p2, rowi):
    r1 = jnp.where(rowi == 0, p1, pltpu.roll(hc, 1, 0))
    r2 = jnp.where(rowi == 0, p2, jnp.where(rowi == 1, p1, pltpu.roll(hc, 2, 0)))
    return r1, r2


def _mix_fwd(z, ws, bsb, gv, bv, cw8, tm):
    S, DIN = z.shape
    DA = gv.shape[1]
    DB = DA
    H = DA // HEAD
    o = 2 * DA
    nch = tm // HEAD

    def body(z_ref, zp_ref, ws_ref, bsb_ref, gv_ref, bv_ref, cw_ref, y_ref):
        i = pl.program_id(0)
        ri = lax.broadcasted_iota(jnp.int32, (HEAD, HEAD), 0)
        ci = lax.broadcasted_iota(jnp.int32, (HEAD, HEAD), 1)
        tril = ri >= ci
        for h in range(H):
            cs = slice(h * HEAD, (h + 1) * HEAD)
            vs = slice(DA + h * HEAD, DA + (h + 1) * HEAD)
            wb = jnp.where(tril, ws_ref[h], 0.0).astype(BF16)
            for n in range(nch):
                rs = slice(n * HEAD, (n + 1) * HEAD)
                u = _gelu(z_ref[rs, cs])
                vhat, _ = _ln_stats(_gelu(z_ref[rs, vs]))
                vn = vhat * gv_ref[:, cs] + bv_ref[:, cs]
                mixed = jnp.dot(wb, vn.astype(BF16), preferred_element_type=F32) + bsb_ref[h]
                y_ref[rs, cs] = (u * mixed).astype(BF16)
        zb = z_ref[:, o:o + DB]
        hc = z_ref[:, o + DB:o + 2 * DB] * z_ref[:, o + 2 * DB:]
        keep = (i > 0).astype(F32)
        p1 = zp_ref[7:8, o + DB:o + 2 * DB] * zp_ref[7:8, o + 2 * DB:] * keep
        p2 = zp_ref[6:7, o + DB:o + 2 * DB] * zp_ref[6:7, o + 2 * DB:] * keep
        rowi = lax.broadcasted_iota(jnp.int32, (tm, DB), 0)
        r1, r2 = _conv_shifted(hc, p1, p2, rowi)
        cv = cw_ref[0:1, :] * r2 + cw_ref[1:2, :] * r1 + cw_ref[2:3, :] * hc
        y_ref[:, DA:] = (zb * cv).astype(BF16)

    full3 = lambda i: (0, 0, 0)
    full2 = lambda i: (0, 0)
    est = 2 * (tm * DIN * 4 + tm * (DA + DB) * 2) + 10 * tm * DB * 4
    return pl.pallas_call(
        body, name="mix_fwd", grid=(S // tm,),
        in_specs=[pl.BlockSpec((tm, DIN), lambda i: (i, 0)),
                  pl.BlockSpec((8, DIN), lambda i: (jnp.maximum(i * (tm // 8) - 1, 0), 0)),
                  pl.BlockSpec(ws.shape, full3), pl.BlockSpec(bsb.shape, full3),
                  pl.BlockSpec(gv.shape, full2), pl.BlockSpec(bv.shape, full2), pl.BlockSpec(cw8.shape, full2)],
        out_specs=pl.BlockSpec((tm, DA + DB), lambda i: (i, 0)),
        out_shape=jax.ShapeDtypeStruct((S, DA + DB), BF16),
        compiler_params=pltpu.CompilerParams(dimension_semantics=("arbitrary",),
                                             vmem_limit_bytes=_vmem_limit(est)),
    )(z, z, ws, bsb, gv, bv, cw8)


def _mix_bwd(z, dy, ws, bsb, gv, bv, cw8, tm):
    S, DIN = z.shape
    DA = gv.shape[1]
    DB = DA
    H = DA // HEAD
    o = 2 * DA
    nch = tm // HEAD
    nblk = S // tm

    def body(z_ref, zp_ref, zn_ref, dy_ref, dyn_ref, ws_ref, bsb_ref, gv_ref, bv_ref, cw_ref,
             dz_ref, dws_ref, dbs_ref, dgv_ref, dbv_ref, dcw_ref, dbin_ref):
        i = pl.program_id(0)

        @pl.when(i == 0)
        def _():
            for r in (dws_ref, dbs_ref, dgv_ref, dbv_ref, dcw_ref, dbin_ref):
                r[...] = jnp.zeros_like(r)

        ri = lax.broadcasted_iota(jnp.int32, (HEAD, HEAD), 0)
        ci = lax.broadcasted_iota(jnp.int32, (HEAD, HEAD), 1)
        tril = ri >= ci
        for h in range(H):
            cs = slice(h * HEAD, (h + 1) * HEAD)
            vs = slice(DA + h * HEAD, DA + (h + 1) * HEAD)
            wb = jnp.where(tril, ws_ref[h], 0.0).astype(BF16)
            gvh = gv_ref[:, cs]
            dws_acc = jnp.zeros((HEAD, HEAD), F32)
            dbs_acc = jnp.zeros((HEAD, HEAD), F32)
            dgv_acc = jnp.zeros((1, HEAD), F32)
            dbv_acc = jnp.zeros((1, HEAD), F32)
            dbu_acc = jnp.zeros((1, HEAD), F32)
            dbvv_acc = jnp.zeros((1, HEAD), F32)
            for n in range(nch):
                rs = slice(n * HEAD, (n + 1) * HEAD)
                u, du_dz = _gelu_and_grad(z_ref[rs, cs])
                v, dv_dz = _gelu_and_grad(z_ref[rs, vs])
                vhat, rstd = _ln_stats(v)
                vnb = (vhat * gvh + bv_ref[:, cs]).astype(BF16)
                mixed = jnp.dot(wb, vnb, preferred_element_type=F32) + bsb_ref[h]
                dya = dy_ref[rs, cs]
                dzu = dya * mixed * du_dz
                dmx = dya * u
                dmxb = dmx.astype(BF16)
                dbs_acc += jnp.broadcast_to(jnp.sum(dmx, axis=1, keepdims=True), (HEAD, HEAD))
                dws_acc += lax.dot_general(dmxb, vnb, _DIMS['nt'], preferred_element_type=F32)
                dvn = lax.dot_general(wb, dmxb, _DIMS['tn'], preferred_element_type=F32)
                dgv_acc += jnp.sum(dvn * vhat, axis=0, keepdims=True)
                dbv_acc += jnp.sum(dvn, axis=0, keepdims=True)
                dxh = dvn * gvh
                m1 = jnp.mean(dxh, axis=-1, keepdims=True)
                m2 = jnp.mean(dxh * vhat, axis=-1, keepdims=True)
                dzv = rstd * (dxh - m1 - vhat * m2) * dv_dz
                dz_ref[rs, cs] = dzu.astype(BF16)
                dz_ref[rs, vs] = dzv.astype(BF16)
                dbu_acc += jnp.sum(dzu, axis=0, keepdims=True)
                dbvv_acc += jnp.sum(dzv, axis=0, keepdims=True)
            dws_ref[h] += jnp.where(tril, dws_acc, 0.0)
            dbs_ref[h] += dbs_acc
            dgv_ref[:, cs] += dgv_acc
            dbv_ref[:, cs] += dbv_acc
            dbin_ref[:, cs] += dbu_acc
            dbin_ref[:, vs] += dbvv_acc

        zb = z_ref[:, o:o + DB]
        zc = z_ref[:, o + DB:o + 2 * DB]
        zx = z_ref[:, o + 2 * DB:]
        hc = zc * zx
        keep = (i > 0).astype(F32)
        p1 = zp_ref[7:8, o + DB:o + 2 * DB] * zp_ref[7:8, o + 2 * DB:] * keep
        p2 = zp_ref[6:7, o + DB:o + 2 * DB] * zp_ref[6:7, o + 2 * DB:] * keep
        rowi = lax.broadcasted_iota(jnp.int32, (tm, DB), 0)
        r1, r2 = _conv_shifted(hc, p1, p2, rowi)
        w0, w1, w2 = cw_ref[0:1, :], cw_ref[1:2, :], cw_ref[2:3, :]
        cv = w0 * r2 + w1 * r1 + w2 * hc
        dyb = dy_ref[:, DA:]
        dzb = dyb * cv
        dcv = dyb * zb
        more = (i < nblk - 1).astype(F32)
        n0 = dyn_ref[0:1, DA:] * zn_ref[0:1, o:o + DB] * more
        n1 = dyn_ref[1:2, DA:] * zn_ref[1:2, o:o + DB] * more
        f1 = jnp.where(rowi == tm - 1, n0, pltpu.roll(dcv, tm - 1, 0))
        f2 = jnp.where(rowi == tm - 1, n1, jnp.where(rowi == tm - 2, n0, pltpu.roll(dcv, tm - 2, 0)))
        dhc = w2 * dcv + w1 * f1 + w0 * f2
        dzc = dhc * zx
        dzx = dhc * zc
        dcw_ref[0:1, :] += jnp.sum(dcv * r2, axis=0, keepdims=True)
        dcw_ref[1:2, :] += jnp.sum(dcv * r1, axis=0, keepdims=True)
        dcw_ref[2:3, :] += jnp.sum(dcv * hc, axis=0, keepdims=True)
        dz_ref[:, o:o + DB] = dzb.astype(BF16)
        dz_ref[:, o + DB:o + 2 * DB] = dzc.astype(BF16)
        dz_ref[:, o + 2 * DB:] = dzx.astype(BF16)
        dbin_ref[:, o:o + DB] += jnp.sum(dzb, axis=0, keepdims=True)
        dbin_ref[:, o + DB:o + 2 * DB] += jnp.sum(dzc, axis=0, keepdims=True)
        dbin_ref[:, o + 2 * DB:] += jnp.sum(dzx, axis=0, keepdims=True)

    full3 = lambda i: (0, 0, 0)
    full2 = lambda i: (0, 0)
    prev8 = lambda i: (jnp.maximum(i * (tm // 8) - 1, 0), 0)
    next8 = lambda i: (jnp.minimum((i + 1) * (tm // 8), S // 8 - 1), 0)
    est = 2 * (tm * DIN * 4 + tm * (DA + DB) * 4 + tm * DIN * 2) + 16 * tm * DB * 4
    return pl.pallas_call(
        body, name="mix_bwd", grid=(nblk,),
        in_specs=[pl.BlockSpec((tm, DIN), lambda i: (i, 0)),
                  pl.BlockSpec((8, DIN), prev8), pl.BlockSpec((8, DIN), next8),
                  pl.BlockSpec((tm, DA + DB), lambda i: (i, 0)), pl.BlockSpec((8, DA + DB), next8),
                  pl.BlockSpec(ws.shape, full3), pl.BlockSpec(bsb.shape, full3),
                  pl.BlockSpec(gv.shape, full2), pl.BlockSpec(bv.shape, full2), pl.BlockSpec(cw8.shape, full2)],
        out_specs=[pl.BlockSpec((tm, DIN), lambda i: (i, 0)),
                   pl.BlockSpec(ws.shape, full3), pl.BlockSpec(ws.shape, full3),
                   pl.BlockSpec(gv.shape, full2), pl.BlockSpec(gv.shape, full2),
                   pl.BlockSpec(cw8.shape, full2), pl.BlockSpec((1, DIN), full2)],
        out_shape=[jax.ShapeDtypeStruct((S, DIN), BF16),
                   jax.ShapeDtypeStruct(ws.shape, F32), jax.ShapeDtypeStruct(ws.shape, F32),
                   jax.ShapeDtypeStruct(gv.shape, F32), jax.ShapeDtypeStruct(gv.shape, F32),
                   jax.ShapeDtypeStruct(cw8.shape, F32), jax.ShapeDtypeStruct((1, DIN), F32)],
        compiler_params=pltpu.CompilerParams(dimension_semantics=("arbitrary",),
                                             vmem_limit_bytes=_vmem_limit(est)),
    )(z, z, z, dy, dy, ws, bsb, gv, bv, cw8)


def _place():
    x, y, c = lax.axis_index("x"), lax.axis_index("y"), lax.axis_index("c")
    others = [(1 - x, y), (x, 1 - y), (1 - x, 1 - y)]
    return x, y, c, 2 * x + y, others, [2 * ox + oy for ox, oy in others]


def _half(c, rows_half):
    return pl.ds(pl.multiple_of(c * rows_half, 16), rows_half)


def _rcopy(src, dst, ssem, rsem, dev):
    return pltpu.make_async_remote_copy(src_ref=src, dst_ref=dst, send_sem=ssem, recv_sem=rsem,
                                        device_id=dev, device_id_type=MESH)


class _Comm:
    def __init__(self, aliased, ins, fresh, n_sems, stages, sink):
        self.aliased, self.ins, self.fresh = list(aliased), list(ins), list(fresh)
        self.n_sems, self.stages, self.sink = n_sems, stages, sink


def _comm_operands(comms, n_in, n_out):
    arrays, shapes, scratch, aliases = [], [], [], {}
    for cm in comms:
        for a in cm.aliased:
            aliases[n_in + len(arrays)] = n_out + len(shapes)
            arrays.append(a)
            shapes.append(jax.ShapeDtypeStruct(a.shape, a.dtype))
        arrays += cm.ins
        shapes += cm.fresh
        scratch += [pltpu.SemaphoreType.DMA((cm.n_sems,))] * 2
    return arrays, shapes, scratch, aliases


def _comm_refs(comms, in_refs, out_refs, sem_refs):
    per, pi, po = [], 0, 0
    for i, cm in enumerate(comms):
        pi += len(cm.aliased)
        ins = in_refs[pi:pi + len(cm.ins)]
        pi += len(cm.ins)
        al = out_refs[po:po + len(cm.aliased)]
        po += len(cm.aliased)
        fr = out_refs[po:po + len(cm.fresh)]
        po += len(cm.fresh)
        per.append((al, ins, fr, sem_refs[2 * i], sem_refs[2 * i + 1]))
    return per


def _comm_first_step(comms, hosted, grid):
    if not comms:
        return
    at_first = functools.reduce(lambda a, b: a & b, [pl.program_id(d) == 0 for d in range(len(grid))])

    @pl.when(at_first)
    def _():
        for cm, refs in zip(comms, hosted):
            for d in cm.stages[0](*refs, False):
                d.start()


def _comm_last_step(comms, hosted, grid):
    if not comms:
        return
    at_last = functools.reduce(lambda a, b: a & b, [pl.program_id(d) == g - 1 for d, g in enumerate(grid)])

    @pl.when(at_last)
    def _():
        for cm, refs in zip(comms, hosted):
            for d in cm.stages[0](*refs, True):
                d.wait()
        for cm, refs in zip(comms, hosted):
            for stage in cm.stages[1:]:
                for d in stage(*refs, False):
                    d.start()
                for d in stage(*refs, True):
                    d.wait()


def _deliver(comms, outs):
    pos = 0
    for cm in comms:
        al = outs[pos:pos + len(cm.aliased)]
        pos += len(cm.aliased)
        fr = outs[pos:pos + len(cm.fresh)]
        pos += len(cm.fresh)
        cm.sink(list(al), list(fr))


def _comm_call(name, comms):
    in_arrays, out_shapes, scratch, aliases = _comm_operands(comms, 0, 0)
    n_in, n_out = len(in_arrays), len(out_shapes)

    def body(*refs):
        hosted = _comm_refs(comms, refs[:n_in], refs[n_in:n_in + n_out], refs[n_in + n_out:])
        for cm, r in zip(comms, hosted):
            for stage in cm.stages:
                for d in stage(*r, False):
                    d.start()
                for d in stage(*r, True):
                    d.wait()

    outs = pl.pallas_call(body, name=name, in_specs=[ANY] * len(in_arrays), out_specs=[ANY] * len(out_shapes),
                          out_shape=out_shapes, scratch_shapes=scratch, input_output_aliases=aliases)(*in_arrays)
    _deliver(comms, outs)


def _c_gather(fulls, sink, ici, fwd, conv=None):
    n = len(fulls)
    n_ici = 3 * (n + (conv is not None)) if ici else 0

    def stage_ici(al, ins, fr, ssem, rsem, wait):
        x, y, c, chip, others, ochips = _place()
        ds = []
        for m in range(n + (conv is not None)):
            for k in range(3):
                i = 3 * m + k
                if m < n:
                    ah = al[m].shape[1] // 2
                    mine, lands = al[m].at[chip, _half(c, ah)], al[m].at[ochips[k], _half(c, ah)]
                else:
                    mine, lands = al[m].at[chip], al[m].at[ochips[k]]
                ds.append(_rcopy(mine, lands, ssem.at[i], rsem.at[i], (x, y, c)) if wait
                          else _rcopy(mine, mine, ssem.at[i], rsem.at[i], (*others[k], c)))
        return ds

    def stage_fwd(al, ins, fr, ssem, rsem, wait):
        x, y, c, chip, others, ochips = _place()
        ds = []
        for m in range(n):
            ah = al[m].shape[1] // 2
            for k in range(3):
                i = n_ici + 3 * m + k
                got = al[m].at[ochips[k], _half(c, ah)]
                ds.append(_rcopy(got, al[m].at[ochips[k], _half(1 - c, ah)], ssem.at[i], rsem.at[i], (x, y, c)) if wait
                          else _rcopy(got, got, ssem.at[i], rsem.at[i], (x, y, 1 - c)))
        return ds

    stages = ([stage_ici] if ici else []) + ([stage_fwd] if fwd else [])
    return _Comm(list(fulls) + ([conv] if conv is not None else []), [], [], n_ici + (3 * n if fwd else 0),
                 stages, sink)


def _c_rs_pair(parts, sink):
    def stage(al, ins, fr, ssem, rsem, wait):
        x, y, c, _, _, _ = _place()
        return [_rcopy(ins[m].at[:, _half(1 - c, ins[m].shape[1] // 2), :], fr[m], ssem.at[m], rsem.at[m],
                       (x, y, 1 - c)) for m in range(len(ins))]
    fresh = [jax.ShapeDtypeStruct((p.shape[0], p.shape[1] // 2, p.shape[2]), p.dtype) for p in parts]
    return _Comm([], parts, fresh, len(parts), [stage], sink)


def _c_rs_chips(psums, sink):
    def stage(al, ins, fr, ssem, rsem, wait):
        x, y, c, _, others, ochips = _place()
        return [_rcopy(ins[m].at[ochips[k]], fr[m].at[k], ssem.at[3 * m + k], rsem.at[3 * m + k], (*others[k], c))
                for m in range(len(ins)) for k in range(3)]
    fresh = [jax.ShapeDtypeStruct((3,) + p.shape[1:], p.dtype) for p in psums]
    return _Comm([], psums, fresh, 3 * len(psums), [stage], sink)


def _c_exchange(g2s, sink):
    def stage(al, ins, fr, ssem, rsem, wait):
        x, y, c, _, _, _ = _place()
        if wait:
            return [_rcopy(al[m].at[c], al[m].at[1 - c], ssem.at[m], rsem.at[m], (x, y, c)) for m in range(len(al))]
        return [_rcopy(al[m].at[c], al[m].at[c], ssem.at[m], rsem.at[m], (x, y, 1 - c)) for m in range(len(al))]
    return _Comm(g2s, [], [], len(g2s), [stage], sink)


def _c_allgather_small(buf, sink):
    def stage(al, ins, fr, ssem, rsem, wait):
        x, y, c, _, _, _ = _place()
        ds = []
        for r in range(1, 8):
            peer = (1 - x if r & 4 else x, 1 - y if r & 2 else y, 1 - c if r & 1 else c)
            ds.append(_rcopy(al[0].at[0], al[0].at[r], ssem.at[r - 1], rsem.at[r - 1], peer))
        return ds
    return _Comm([buf], [], [], 7, [stage], sink)


def _sum_small(buf, dev_idx):
    _, m, n = buf.shape

    def body(me_ref, b_ref, o_ref):
        me = me_ref[0]
        acc = b_ref[me]
        for d in range(1, 8):
            acc = acc + b_ref[jnp.bitwise_xor(me, d)]
        o_ref[...] = acc

    gs = pltpu.PrefetchScalarGridSpec(
        num_scalar_prefetch=1, grid=(1,),
        in_specs=[pl.BlockSpec(buf.shape, lambda i, me: (0, 0, 0))],
        out_specs=pl.BlockSpec((m, n), lambda i, me: (0, 0)))
    return pl.pallas_call(body, name="sum_small", grid_spec=gs,
                          out_shape=jax.ShapeDtypeStruct((m, n), F32))(dev_idx, buf)


def _sum_pair(name, part, sib, c_idx):
    _, A, B = part.shape
    ah = A // 2
    br = _rows(ah, B * 2, 2 * 2 ** 20)

    def body(c_ref, a_ref, b_ref, o_ref):
        o_ref[...] = (a_ref[...].astype(F32) + b_ref[...].astype(F32)).astype(BF16)

    gs = pltpu.PrefetchScalarGridSpec(
        num_scalar_prefetch=1, grid=(N_CHIPS, ah // br),
        in_specs=[pl.BlockSpec((None, None, br, B), lambda j, i, c: (j, c[0], i, 0)),
                  pl.BlockSpec((None, br, B), lambda j, i, c: (j, i, 0))],
        out_specs=pl.BlockSpec((None, br, B), lambda j, i, c: (j, i, 0)))
    return pl.pallas_call(body, name=name, grid_spec=gs,
                          out_shape=jax.ShapeDtypeStruct((N_CHIPS, ah, B), BF16))(
                              c_idx, part.reshape(N_CHIPS, 2, ah, B), sib)


def _sum_chips(name, psum, recv, place_idx):
    _, ah, B = psum.shape
    br = _rows(ah, B * 4, 2 * 2 ** 20)

    def body(c_ref, p_ref, r0_ref, r1_ref, r2_ref, o_ref):
        o_ref[...] = ((p_ref[...].astype(F32) + r0_ref[...].astype(F32)) + r1_ref[...].astype(F32)) \
            + r2_ref[...].astype(F32)

    gs = pltpu.PrefetchScalarGridSpec(
        num_scalar_prefetch=1, grid=(ah // br,),
        in_specs=[pl.BlockSpec((None, br, B), lambda i, c: (c[0], i, 0))]
        + [pl.BlockSpec((None, br, B), functools.partial(lambda i, c, k: (k, i, 0), k=k)) for k in range(3)],
        out_specs=pl.BlockSpec((None, br, B), lambda i, c: (c[1], i, 0)))
    return pl.pallas_call(body, name=name, grid_spec=gs,
                          out_shape=jax.ShapeDtypeStruct((2, ah, B), F32))(place_idx, psum, recv, recv, recv)


def _cast_into(name, a, place_idx):
    A, B = a.shape
    br = _rows(A, B * 4)

    def body(c_ref, a_ref, o_ref):
        o_ref[...] = a_ref[...].astype(BF16)

    gs = pltpu.PrefetchScalarGridSpec(
        num_scalar_prefetch=1, grid=(A // br,),
        in_specs=[pl.BlockSpec((br, B), lambda i, c: (i, 0))],
        out_specs=pl.BlockSpec((None, br, B), lambda i, c: (c[0], i, 0)))
    return pl.pallas_call(body, name=name, grid_spec=gs,
                          out_shape=jax.ShapeDtypeStruct((N_CHIPS, A, B), BF16))(place_idx, a)


def _adamw_small(gsum, gconv, offsets, ws, ms, vs):
    n = len(ws)

    def body(*refs):
        g_ref, gc_ref = refs[0], refs[1]
        w_refs = refs[2:2 + n]
        m_refs = refs[2 + n:2 + 2 * n]
        v_refs = refs[2 + 2 * n:2 + 3 * n]
        outs = refs[2 + 3 * n:]
        for i in range(n):
            rows = w_refs[i].shape[0]
            g = gc_ref[...] if offsets[i] is None else g_ref[offsets[i]:offsets[i] + rows, :]
            d, m2, v2 = _adamw_math(w_refs[i][...], g, m_refs[i][...], v_refs[i][...])
            outs[4 * i][...] = g
            outs[4 * i + 1][...] = d
            outs[4 * i + 2][...] = m2
            outs[4 * i + 3][...] = v2

    vm = pl.BlockSpec(memory_space=pltpu.VMEM)
    out_shape = []
    for w in ws:
        out_shape += [jax.ShapeDtypeStruct(w.shape, F32)] * 4
    return pl.pallas_call(body, name="adamw_small", in_specs=[vm] * (2 + 3 * n), out_specs=[vm] * (4 * n),
                          out_shape=out_shape)(gsum, gconv, *ws, *ms, *vs)


def _ident(accs, tes, res):
    return [accs[0]]


def _ffn_up(name, hb, wg4, wu4, comms=()):
    S, D = hb.shape

    def epi(accs, tes, res):
        g, u = accs
        return [g, u, g * _sigmoid(g) * u]

    return _mm(name, 'nn', [hb], [wg4, wu4], [(0, 0, 0), (0, 1, 1)], 2, epi, [BF16] * 3,
               bm=_blk(S, 512), bn=wg4.shape[2], bk=D, b_blocked=True, n_outer=True, comms=comms)


def _ffn_gate(name, hb, wg4, comms=()):
    S, D = hb.shape
    return _mm(name, 'nn', [hb], [wg4], [(0, 0, 0)], 1, _ident, [BF16],
               bm=_blk(S, 1024), bn=wg4.shape[2], bk=D, b_blocked=True, n_outer=True, comms=comms)[0]


def _ffn_upmul(name, hb, wu4, gb, comms=()):
    S, D = hb.shape

    def epi(accs, tes, res):
        g = tes[0].astype(F32)
        return [accs[0], g * _sigmoid(g) * accs[0]]

    return _mm(name, 'nn', [hb], [wu4], [(0, 0, 0)], 1, epi, [BF16, BF16],
               bm=_blk(S, 1024), bn=wu4.shape[2], bk=D, tile_extras=[gb], b_blocked=True, n_outer=True, comms=comms)


def _ffn_down(name, ab, wd, hres, comms=()):
    S, F = ab.shape
    D = wd.shape[1]
    return _mm(name, 'nn', [ab], [wd], [(0, 0, 0)], 1, lambda accs, tes, res: [ALPHA * tes[0] + 0.5 * accs[0]],
               [F32], bm=_blk(S, 1024), bn=_blk(D, 512), bk=F, tile_extras=[hres], comms=comms)[0]


def _ffn_da(name, dsub, wd, gb, ub, comms=()):
    S, D = dsub.shape
    F = wd.shape[0]

    def epi(accs, tes, res):
        da = accs[0]
        g = tes[0].astype(F32)
        u = tes[1].astype(F32)
        s = _sigmoid(g)
        silu = g * s
        return [(da * u) * (s + silu * (1.0 - s)), da * silu]

    return _mm(name, 'nt', [dsub], [wd], [(0, 0, 0)], 1, epi, [BF16, BF16],
               bm=_blk(S, 1024), bn=_blk(F, 512), bk=D, tile_extras=[gb, ub], comms=comms)


def _dw_rows(name, ab, dsub, cs, comms=()):
    S, D = dsub.shape
    return _mm(name, 'tn', [ab], [dsub], [(0, 0, 0)], 1, _ident, [BF16],
               bm=cs, bn=_blk(D, 1024), bk=_blk(S, 2048), comms=comms)[0].reshape(N_CHIPS, cs, D)


def _dw_cols(name, hb, dxb, cs, comms=()):
    S, D = hb.shape
    return _mm(name, 'tn', [hb], [dxb], [(0, 0, 0)], 1, _ident, [BF16],
               bm=_blk(D, 1024), bn=cs, bk=_blk(S, 2048), out_blocked=True, comms=comms)[0]


def _ffn_dh(name, dgb, dub, wg4, wu4, dres, comms=()):
    S = dgb.shape[0]
    D = wg4.shape[1]
    return _mm(name, 'nt', [dgb, dub], [wg4, wu4], [(0, 0, 0), (1, 1, 0)], 1,
               lambda accs, tes, res: [accs[0] + tes[0]], [F32],
               bm=_blk(S, 1024), bn=_blk(D, 1024), bk=wg4.shape[2], tile_extras=[dres], b_blocked=True,
               comms=comms)[0]


def _step(p):
    x = p['x'][0]
    target = p['loss_target'][0]
    S, D = x.shape
    cx, cy, cc = lax.axis_index("x"), lax.axis_index("y"), lax.axis_index("c")
    chip = 2 * cx + cy
    c_idx = jnp.reshape(cc, (1,)).astype(jnp.int32)
    place_idx = jnp.stack([chip, cc]).astype(jnp.int32)

    w_s = p['w_s'][0]
    H = w_s.shape[0]
    DA = H * HEAD
    bsb = jnp.broadcast_to(p['b_s'][0][:, :, None], (H, HEAD, HEAD))
    conv_shard = p['conv_w'][0]
    conv8 = jnp.zeros((8, conv_shard.shape[1]), F32).at[:CONV_TAPS].set(conv_shard)

    W = {n: _cast_into(f"cast_{n}", p[n][0], place_idx) for n in BIG}
    conv = {'full': lax.dynamic_update_slice(jnp.zeros((N_CHIPS,) + conv8.shape, F32), conv8[None], (chip, 0, 0))}

    def gathered(names, with_conv=False):
        def sink(al, fr):
            W.update(zip(names, al))
            if with_conv:
                conv['full'] = al[len(names)]
        return sink

    def c_gather(names, ici, fwd, with_conv=False):
        return _c_gather([W[n] for n in names], gathered(names, with_conv), ici, fwd,
                         conv['full'] if with_conv else None)

    xb = _cast_bf16("cast_x", x, comms=[c_gather(['ffa_gate'], True, True)])
    cs_ff = W['ffa_gate'].shape[2]
    F = N_CHIPS * cs_ff

    ga = _ffn_gate("ffa_gate", xb, W['ffa_gate'],
                   comms=[c_gather(['ffa_up'], True, True), c_gather(['ffa_down'], True, False)])
    ua, aa = _ffn_upmul("ffa_up", xb, W['ffa_up'], ga,
                        comms=[c_gather(['ffa_down'], False, True),
                               c_gather(['w_in', 'w_out'], True, False, with_conv=True)])
    cw8 = jnp.transpose(conv['full'], (1, 0, 2)).reshape(8, DA)
    wd_a = W['ffa_down'].reshape(F, D)
    r1 = _ffn_down("ffa_down", aa, wd_a, x,
                   comms=[c_gather(['w_in', 'w_out'], False, True), c_gather(['ffc_gate', 'ffc_down'], True, False)])
    h1, h1b = _ln_fwd("ln_a", r1, p['ln_a_g'], p['ln_a_b'])
    w_in4 = W['w_in']
    cs_in = w_in4.shape[2]
    z = _mm("w_in", 'nn', [h1b], [w_in4], [(0, 0, 0)], 1, lambda accs, tes, res: [accs[0] + res[0]], [F32],
            bm=_blk(S, 1024), bn=cs_in, bk=D, row_extras=[p['b_in']], b_blocked=True, n_outer=True,
            comms=[c_gather(['ffc_gate', 'ffc_down'], False, True), c_gather(['ffc_up'], True, False)])[0]
    w_out = W['w_out'].reshape(DA * 2, D)
    tm = _blk(S, 256)
    yb = _mix_fwd(z, w_s, bsb, p['ln_v_g'], p['ln_v_b'], cw8, tm)
    r2 = _mm("w_out", 'nn', [yb], [w_out], [(0, 0, 0)], 1,
             lambda accs, tes, res: [accs[0] + res[0] + ALPHA * tes[0]], [F32],
             bm=_blk(S, 1024), bn=_blk(D, 1024), bk=2 * DA, tile_extras=[h1], row_extras=[p['b_out']],
             comms=[c_gather(['ffc_up'], False, True)])[0]
    h2, h2b = _ln_fwd("ln_m", r2, p['ln_m_g'], p['ln_m_b'])
    gc, uc, ac = _ffn_up("ffc_up", h2b, W['ffc_gate'], W['ffc_up'])
    wd_c = W['ffc_down'].reshape(F, D)
    r3 = _ffn_down("ffc_down", ac, wd_c, h2)

    pg, sg, sib, psum, recv, g2 = {}, {}, {}, {}, {}, {}

    def c_pair(names):
        return _c_rs_pair([pg[n] for n in names], lambda al, fr: sib.update(zip(names, fr)))

    def c_chips(names):
        return _c_rs_chips([psum[n] for n in names], lambda al, fr: recv.update(zip(names, fr)))

    def c_swap(names):
        return _c_exchange([g2[n] for n in names], lambda al, fr: g2.update(zip(names, al)))

    def sum_pair(*names):
        for n in names:
            psum[n] = _sum_pair(f"sum_pair_{n}", pg[n], sib[n], c_idx)

    def sum_chips(*names):
        for n in names:
            g2[n] = _sum_chips(f"sum_chips_{n}", psum[n], recv[n], place_idx)

    dres3, dsub3, sg['ln_c_g'], sg['ln_c_b'], lsum = _ln_loss_bwd(
        "ln_c_loss_bwd", r3, target, p['ln_c_g'], p['ln_c_b'], 0.5)

    dgc, duc = _ffn_da("ffc_da", dsub3, wd_c, gc, uc)
    pg['ffc_down'] = _dw_rows("ffc_dwd", ac, dsub3, cs_ff)
    pg['ffc_gate'] = _dw_cols("ffc_dwg", h2b, dgc, cs_ff, comms=[c_pair(['ffc_down'])])
    sum_pair('ffc_down')
    pg['ffc_up'] = _dw_cols("ffc_dwu", h2b, duc, cs_ff, comms=[c_pair(['ffc_gate']), c_chips(['ffc_down'])])
    sum_pair('ffc_gate')
    dh2 = _ffn_dh("ffc_dh", dgc, duc, W['ffc_gate'], W['ffc_up'], dres3,
                  comms=[c_pair(['ffc_up']), c_chips(['ffc_gate'])])
    sum_pair('ffc_up')
    sum_chips('ffc_down', 'ffc_gate')

    dres2, dmix, sg['ln_m_g'], sg['ln_m_b'], sg['b_out'] = _ln_bwd("ln_m_bwd", r2, dh2, p['ln_m_g'], 1.0)
    pg['w_out'] = _mm("dw_out", 'tn', [yb], [dmix], [(0, 0, 0)], 1, _ident, [BF16],
                      bm=_blk(2 * DA, 1024), bn=_blk(D, 1024), bk=_blk(S, 2048),
                      comms=[c_swap(['ffc_down', 'ffc_gate'])])[0].reshape(N_CHIPS, 2 * DA // 4, D)
    dy = _mm("dy", 'nt', [dmix], [w_out], [(0, 0, 0)], 1, _ident, [F32],
             bm=_blk(S, 1024), bn=_blk(2 * DA, 1024), bk=D, comms=[c_pair(['w_out'])])[0]
    sum_pair('w_out')
    dz, dws, dbs, sg['ln_v_g'], sg['ln_v_b'], dcw, sg['b_in'] = _mix_bwd(
        z, dy, w_s, bsb, p['ln_v_g'], p['ln_v_b'], cw8, tm)
    sg['w_s'] = dws
    sg['b_s'] = dbs[:, :, 0]
    pg['w_in'] = _dw_cols("dw_in", h1b, dz, cs_in, comms=[c_chips(['ffc_up']), c_chips(['w_out'])])
    sum_chips('ffc_up', 'w_out')
    dh1 = _mm("dh1", 'nt', [dz], [w_in4], [(0, 0, 0)], 1, lambda accs, tes, res: [accs[0] + tes[0]], [F32],
              bm=_blk(S, 1024), bn=_blk(D, 1024), bk=cs_in, tile_extras=[dres2], b_blocked=True,
              comms=[c_pair(['w_in']), c_swap(['ffc_up', 'w_out'])])[0]
    sum_pair('w_in')

    dres1, dsub1, sg['ln_a_g'], sg['ln_a_b'], _ = _ln_bwd("ln_a_bwd", r1, dh1, p['ln_a_g'], 0.5)
    def as_rows(a):
        a = a.reshape(-1, 128)
        pad = (-a.shape[0]) % 8
        return jnp.pad(a, ((0, pad), (0, 0))) if pad else a

    pieces, offsets, off = [], {}, 0
    for n in SMALL + ['conv_w']:
        piece = as_rows(dcw[:CONV_TAPS] if n == 'conv_w' else sg[n])
        offsets[n] = off
        off += piece.shape[0]
        pieces.append(piece)
    packed = jnp.concatenate(pieces, axis=0)
    small = {'buf': jnp.zeros((8,) + packed.shape, F32).at[0].set(packed)}
    dga, dua = _ffn_da("ffa_da", dsub1, wd_a, ga, ua,
                       comms=[c_chips(['w_in']), _c_allgather_small(small['buf'], lambda al, fr: small.update(buf=al[0]))])
    sum_chips('w_in')
    gsum = _sum_small(small['buf'], jnp.reshape(4 * cx + 2 * cy + cc, (1,)).astype(jnp.int32))
    pg['ffa_down'] = _dw_rows("ffa_dwd", aa, dsub1, cs_ff, comms=[c_swap(['w_in'])])
    pg['ffa_gate'] = _dw_cols("ffa_dwg", xb, dga, cs_ff, comms=[c_pair(['ffa_down'])])
    sum_pair('ffa_down')
    pg['ffa_up'] = _dw_cols("ffa_dwu", xb, dua, cs_ff, comms=[c_pair(['ffa_gate']), c_chips(['ffa_down'])])
    sum_pair('ffa_gate')
    _comm_call("pair_ffa_up", [c_pair(['ffa_up'])])
    sum_pair('ffa_up')
    sum_chips('ffa_down')
    dx = _ffn_dh("ffa_dh", dga, dua, W['ffa_gate'], W['ffa_up'], dres1,
                 comms=[c_chips(['ffa_gate']), c_chips(['ffa_up']), c_swap(['ffa_down'])])
    sum_chips('ffa_gate', 'ffa_up')
    _comm_call("swap_ffa", [c_swap(['ffa_gate', 'ffa_up'])])

    out = {}
    for n in BIG:
        g, d, m2, v2 = _adamw(f"adamw_{n}", p[n][0], g2[n].reshape(p[n][0].shape), p['m_' + n][0], p['v_' + n][0])
        shp = p[n].shape
        out[n] = (g.reshape(shp), d.reshape(shp), m2.reshape(shp), v2.reshape(shp))

    loss = lax.psum(0.5 * jnp.sum(lsum) / D, ("x", "y", "c"))
    n_conv = CONV_TAPS * DA // 128
    conv_sum = gsum[offsets['conv_w']:offsets['conv_w'] + n_conv].reshape(CONV_TAPS, DA)
    cw = conv_shard.shape[1]
    gconv = lax.dynamic_slice_in_dim(conv_sum, (2 * cx + cy) * cw, cw, axis=1).reshape(-1, 128)
    names = SMALL + ['conv_w']
    rows128 = lambda a: a.reshape(-1, 128)
    res = _adamw_small(gsum, gconv, [offsets[n] if n != 'conv_w' else None for n in names],
                       [rows128(p[n]) for n in names], [rows128(p['m_' + n]) for n in names],
                       [rows128(p['v_' + n]) for n in names])
    for i, n in enumerate(names):
        out[n] = tuple(r.reshape(p[n].shape) for r in res[4 * i:4 * i + 4])

    return (loss, dx.reshape(p['x'].shape), *[out[n][0] for n in WEIGHTS], *[out[n][1] for n in WEIGHTS],
            *[out[n][2] for n in WEIGHTS], *[out[n][3] for n in WEIGHTS])


def kernel(x, ffa_gate, ffa_up, ffa_down, ln_a_g, ln_a_b, w_in, b_in, w_s, b_s, ln_v_g, ln_v_b, conv_w, w_out, b_out, ln_m_g, ln_m_b, ffc_gate, ffc_up, ffc_down, ln_c_g, ln_c_b, loss_target, m_ffa_gate, m_ffa_up, m_ffa_down, m_ln_a_g, m_ln_a_b, m_w_in, m_b_in, m_w_s, m_b_s, m_ln_v_g, m_ln_v_b, m_conv_w, m_w_out, m_b_out, m_ln_m_g, m_ln_m_b, m_ffc_gate, m_ffc_up, m_ffc_down, m_ln_c_g, m_ln_c_b, v_ffa_gate, v_ffa_up, v_ffa_down, v_ln_a_g, v_ln_a_b, v_w_in, v_b_in, v_w_s, v_b_s, v_ln_v_g, v_ln_v_b, v_conv_w, v_w_out, v_b_out, v_ln_m_g, v_ln_m_b, v_ffc_gate, v_ffc_up, v_ffc_down, v_ln_c_g, v_ln_c_b):
    return _step(dict(locals()))
```

```python
import functools
import math

import jax
import jax.numpy as jnp
from jax import lax
from jax.experimental import pallas as pl
from jax.experimental.pallas import tpu as pltpu

F32 = jnp.float32
BF16 = jnp.bfloat16
MESH = pl.DeviceIdType.MESH
ANY = pl.BlockSpec(memory_space=pl.ANY)

HEAD = 128
CONV_TAPS = 3
LN_EPS = 1e-5
ALPHA = float(2.0 ** 0.25)
ADAM_LR, ADAM_B1, ADAM_B2, ADAM_EPS, ADAM_WD, ADAM_STEP = 0.001, 0.9, 0.999, 1e-08, 0.01, 10

V7X_VMEM_BYTES = 64 * 2 ** 20
N_CHIPS = 4

BIG = ['ffa_gate', 'ffa_up', 'ffa_down', 'w_in', 'w_out', 'ffc_gate', 'ffc_up', 'ffc_down']
WEIGHTS = ['ffa_gate', 'ffa_up', 'ffa_down', 'ln_a_g', 'ln_a_b', 'w_in', 'b_in', 'w_s', 'b_s', 'ln_v_g', 'ln_v_b',
           'conv_w', 'w_out', 'b_out', 'ln_m_g', 'ln_m_b', 'ffc_gate', 'ffc_up', 'ffc_down', 'ln_c_g', 'ln_c_b']
SMALL = [n for n in WEIGHTS if n not in BIG and n != 'conv_w']


def _vmem_limit(estimate_bytes):
    return int(min(max(estimate_bytes * 1.3, 32 * 2 ** 20), V7X_VMEM_BYTES - 6 * 2 ** 20))


def _blk(dim, pref, mult=128):
    if dim <= pref:
        return dim
    best = None
    for d in range(mult, pref + 1, mult):
        if dim % d == 0:
            best = d
    assert best is not None, (dim, pref)
    return best


def _rows(n_rows, row_bytes, target=4 * 2 ** 20):
    return _blk(n_rows, max(16, target // row_bytes), 16)


def _sigmoid(x):
    return 0.5 * jnp.tanh(0.5 * x) + 0.5


def _gelu(x):
    c = math.sqrt(2.0 / math.pi)
    return 0.5 * x * (1.0 + jnp.tanh(c * (x + 0.044715 * (x * x * x))))


def _gelu_and_grad(x):
    c = math.sqrt(2.0 / math.pi)
    t = jnp.tanh(c * (x + 0.044715 * (x * x * x)))
    g = 0.5 * x * (1.0 + t)
    dg = 0.5 * (1.0 + t) + 0.5 * x * (1.0 - t * t) * (c * (1.0 + 3.0 * 0.044715 * (x * x)))
    return g, dg


def _ln_stats(r):
    mu = jnp.mean(r, axis=-1, keepdims=True)
    d = r - mu
    var = jnp.mean(d * d, axis=-1, keepdims=True)
    rstd = lax.rsqrt(var + LN_EPS)
    return d * rstd, rstd


_DIMS = {'nn': (((1,), (0,)), ((), ())), 'nt': (((1,), (1,)), ((), ())), 'tn': (((0,), (0,)), ((), ()))}


def _mm(name, mode, a_list, b_list, pairs, n_acc, epilogue, out_dtypes, bm, bn, bk,
        tile_extras=(), row_extras=(), b_blocked=False, out_blocked=False, n_outer=False, comms=()):
    a0, b0 = a_list[0], b_list[0]
    if mode == 'nn':
        M, K = a0.shape
        N = b0.shape[0] * b0.shape[2] if b_blocked else b0.shape[1]
    elif mode == 'nt':
        M, K = a0.shape
        N = b0.shape[1] if b_blocked else b0.shape[0]
    else:
        K, M = a0.shape
        N = b0.shape[1]
        assert not b_blocked
    assert M % bm == 0 and N % bn == 0 and K % bk == 0, (name, M, N, K, bm, bn, bk)
    gm, gn, gk = M // bm, N // bn, K // bk
    if b_blocked:
        assert (bn if mode == 'nn' else bk) == b0.shape[2], name
    dims = _DIMS[mode]

    def spec(shape, f):
        if n_outer:
            return pl.BlockSpec(shape, lambda g0, g1, g2: f(g1, g0, g2))
        return pl.BlockSpec(shape, f)

    if mode == 'tn':
        a_spec = spec((bk, bm), lambda i, j, k: (k, i))
    else:
        a_spec = spec((bm, bk), lambda i, j, k: (i, k))
    if mode == 'nn':
        b_spec = (spec((None, bk, bn), lambda i, j, k: (j, k, 0)) if b_blocked
                  else spec((bk, bn), lambda i, j, k: (k, j)))
    elif mode == 'nt':
        b_spec = (spec((None, bn, bk), lambda i, j, k: (k, j, 0)) if b_blocked
                  else spec((bn, bk), lambda i, j, k: (j, k)))
    else:
        b_spec = spec((bk, bn), lambda i, j, k: (k, j))
    te_spec = spec((bm, bn), lambda i, j, k: (i, j))
    re_spec = spec((1, bn), lambda i, j, k: (0, j))
    if out_blocked:
        o_spec = spec((None, bm, bn), lambda i, j, k: (j, i, 0))
        o_shape = (gn, M, bn)
    else:
        o_spec = te_spec
        o_shape = (M, N)

    na, nb, nte, nre, no = len(a_list), len(b_list), len(tile_extras), len(row_extras), len(out_dtypes)

    n_scratch_acc = n_acc if gk > 1 else 0
    grid = (gn, gm, gk) if n_outer else (gm, gn, gk)
    cm_in_arrays, cm_out_shapes, cm_scratch, aliases = _comm_operands(comms, na + nb + nte + nre, no)
    n_cm_in, n_cm_out = len(cm_in_arrays), len(cm_out_shapes)

    def body(*refs):
        a_refs = refs[:na]
        b_refs = refs[na:na + nb]
        te_refs = refs[na + nb:na + nb + nte]
        re_refs = refs[na + nb + nte:na + nb + nte + nre]
        n_in = na + nb + nte + nre
        o_refs = refs[n_in + n_cm_in:n_in + n_cm_in + no]
        acc_refs = refs[n_in + n_cm_in + no + n_cm_out:n_in + n_cm_in + no + n_cm_out + n_scratch_acc]
        hosted = _comm_refs(comms, refs[n_in:n_in + n_cm_in], refs[n_in + n_cm_in + no:n_in + n_cm_in + no + n_cm_out],
                            refs[n_in + n_cm_in + no + n_cm_out + n_scratch_acc:])
        _comm_first_step(comms, hosted, grid)

        def finish(accs):
            outs = epilogue(accs, [r[...] for r in te_refs], [r[...] for r in re_refs])
            for o_ref, o in zip(o_refs, outs):
                o_ref[...] = o.astype(o_ref.dtype)

        def products():
            parts = [None] * n_acc
            for ai, bi, ci in pairs:
                d = lax.dot_general(a_refs[ai][...], b_refs[bi][...], dims, preferred_element_type=F32)
                parts[ci] = d if parts[ci] is None else parts[ci] + d
            return parts

        if gk == 1:
            finish(products())
        else:
            kk = pl.program_id(2)

            @pl.when(kk == 0)
            def _():
                for r, part in zip(acc_refs, products()):
                    r[...] = part

            if gk > 2:
                @pl.when((kk > 0) & (kk < gk - 1))
                def _():
                    for r, part in zip(acc_refs, products()):
                        r[...] += part

            @pl.when(kk == gk - 1)
            def _():
                finish([r[...] + part for r, part in zip(acc_refs, products())])

        _comm_last_step(comms, hosted, grid)

    isz = lambda a: jnp.dtype(a.dtype).itemsize
    est = 2 * (sum(bm * bk * isz(a) for a in a_list) + sum(bk * bn * isz(b) for b in b_list)
               + sum(bm * bn * isz(t) for t in tile_extras)
               + sum(bm * bn * jnp.dtype(d).itemsize for d in out_dtypes))
    est += (2 * n_acc + 2) * bm * bn * 4
    outs = pl.pallas_call(
        body, name=name, grid=grid,
        in_specs=[a_spec] * na + [b_spec] * nb + [te_spec] * nte + [re_spec] * nre + [ANY] * len(cm_in_arrays),
        out_specs=[o_spec] * no + [ANY] * len(cm_out_shapes),
        out_shape=[jax.ShapeDtypeStruct(o_shape, d) for d in out_dtypes] + cm_out_shapes,
        scratch_shapes=[pltpu.VMEM((bm, bn), F32)] * n_scratch_acc + cm_scratch,
        input_output_aliases=aliases,
        compiler_params=pltpu.CompilerParams(dimension_semantics=("parallel", "parallel", "arbitrary"),
                                             vmem_limit_bytes=_vmem_limit(est)),
    )(*a_list, *b_list, *tile_extras, *row_extras, *cm_in_arrays)
    _deliver(comms, outs[no:])
    return outs[:no]


def _rowwise(name, fn, br, row_ins, vec_ins, row_outs, acc_outs=(), comms=()):
    S = row_ins[0].shape[0]
    assert S % br == 0, (name, S, br)
    nr, nv, no, nacc = len(row_ins), len(vec_ins), len(row_outs), len(acc_outs)
    grid = (S // br,)
    cm_in_arrays, cm_out_shapes, cm_scratch, aliases = _comm_operands(comms, nr + nv, no + nacc)
    n_cm_in, n_cm_out = len(cm_in_arrays), len(cm_out_shapes)

    def body(*refs):
        rin = refs[:nr]
        vin = refs[nr:nr + nv]
        o0 = nr + nv + n_cm_in
        rout = refs[o0:o0 + no]
        aout = refs[o0 + no:o0 + no + nacc]
        hosted = _comm_refs(comms, refs[nr + nv:o0], refs[o0 + no + nacc:o0 + no + nacc + n_cm_out],
                            refs[o0 + no + nacc + n_cm_out:])
        _comm_first_step(comms, hosted, grid)
        if nacc:
            @pl.when(pl.program_id(0) == 0)
            def _():
                for a in aout:
                    a[...] = jnp.zeros_like(a)
        fn(rin, vin, rout, aout)
        _comm_last_step(comms, hosted, grid)

    def full_spec(shape):
        nd = len(shape)
        return pl.BlockSpec(shape, lambda i: (0,) * nd)

    est = 2 * (sum(br * a.shape[1] * jnp.dtype(a.dtype).itemsize for a in row_ins)
               + sum(br * c * jnp.dtype(d).itemsize for c, d in row_outs))
    est += 6 * br * max(a.shape[1] for a in row_ins) * 4
    outs = pl.pallas_call(
        body, name=name, grid=grid,
        in_specs=[pl.BlockSpec((br, a.shape[1]), lambda i: (i, 0)) for a in row_ins]
        + [full_spec(v.shape) for v in vec_ins] + [ANY] * n_cm_in,
        out_specs=[pl.BlockSpec((br, c), lambda i: (i, 0)) for c, _ in row_outs]
        + [full_spec(s) for s, _ in acc_outs] + [ANY] * n_cm_out,
        out_shape=[jax.ShapeDtypeStruct((S, c), d) for c, d in row_outs]
        + [jax.ShapeDtypeStruct(s, d) for s, d in acc_outs] + cm_out_shapes,
        scratch_shapes=cm_scratch,
        input_output_aliases=aliases,
        compiler_params=pltpu.CompilerParams(dimension_semantics=("arbitrary",),
                                             vmem_limit_bytes=_vmem_limit(est)),
    )(*row_ins, *vec_ins, *cm_in_arrays)
    _deliver(comms, outs[no + nacc:])
    return outs[:no + nacc]


def _cast_bf16(name, a, comms=()):
    def fn(rin, vin, rout, aout):
        rout[0][...] = rin[0][...].astype(BF16)
    return _rowwise(name, fn, _rows(a.shape[0], a.shape[1] * 4), [a], [], [(a.shape[1], BF16)], comms=comms)[0]


def _ln_fwd(name, r, g, b):
    def fn(rin, vin, rout, aout):
        xhat, _ = _ln_stats(rin[0][...])
        h = xhat * vin[0][...] + vin[1][...]
        rout[0][...] = h
        rout[1][...] = h.astype(BF16)
    D = r.shape[1]
    return _rowwise(name, fn, _rows(r.shape[0], D * 4, 2 * 2 ** 20), [r], [g, b], [(D, F32), (D, BF16)])


def _ln_loss_bwd(name, r, target, g, b, sub_scale):
    D = r.shape[1]

    def fn(rin, vin, rout, aout):
        xhat, rstd = _ln_stats(rin[0][...])
        gain = vin[0][...]
        err = xhat * gain + vin[1][...] - rin[1][...]
        dhv = err * (1.0 / D)
        dxh = dhv * gain
        m1 = jnp.mean(dxh, axis=-1, keepdims=True)
        m2 = jnp.mean(dxh * xhat, axis=-1, keepdims=True)
        dr = rstd * (dxh - m1 - xhat * m2)
        rout[0][...] = ALPHA * dr
        rout[1][...] = (sub_scale * dr).astype(BF16)
        aout[0][...] += jnp.sum(dhv * xhat, axis=0, keepdims=True)
        aout[1][...] += jnp.sum(dhv, axis=0, keepdims=True)
        aout[2][...] += jnp.sum(err * err, axis=0, keepdims=True)
    return _rowwise(name, fn, _rows(r.shape[0], D * 4, 2 * 2 ** 20), [r, target], [g, b],
                    [(D, F32), (D, BF16)], [((1, D), F32)] * 3)


def _ln_bwd(name, r, dh, g, sub_scale):
    D = r.shape[1]

    def fn(rin, vin, rout, aout):
        xhat, rstd = _ln_stats(rin[0][...])
        dhv = rin[1][...]
        dxh = dhv * vin[0][...]
        m1 = jnp.mean(dxh, axis=-1, keepdims=True)
        m2 = jnp.mean(dxh * xhat, axis=-1, keepdims=True)
        dr = rstd * (dxh - m1 - xhat * m2)
        rout[0][...] = ALPHA * dr
        rout[1][...] = (sub_scale * dr).astype(BF16)
        aout[0][...] += jnp.sum(dhv * xhat, axis=0, keepdims=True)
        aout[1][...] += jnp.sum(dhv, axis=0, keepdims=True)
        aout[2][...] += jnp.sum(dr, axis=0, keepdims=True)
    return _rowwise(name, fn, _rows(r.shape[0], D * 4, 2 * 2 ** 20), [r, dh], [g],
                    [(D, F32), (D, BF16)], [((1, D), F32)] * 3)


def _adamw_math(w, g, m, v):
    m2 = ADAM_B1 * m + (1.0 - ADAM_B1) * g
    v2 = ADAM_B2 * v + (1.0 - ADAM_B2) * (g * g)
    m_hat = m2 / (1.0 - ADAM_B1 ** ADAM_STEP)
    v_hat = v2 / (1.0 - ADAM_B2 ** ADAM_STEP)
    delta = -ADAM_LR * (m_hat / (jnp.sqrt(v_hat) + ADAM_EPS) + ADAM_WD * w)
    return delta, m2, v2


def _adamw(name, w, g, m, v):
    def fn(rin, vin, rout, aout):
        gv = rin[1][...]
        d, m2, v2 = _adamw_math(rin[0][...], gv, rin[2][...], rin[3][...])
        rout[0][...] = gv
        rout[1][...] = d
        rout[2][...] = m2
        rout[3][...] = v2
    C = w.shape[1]
    return _rowwise(name, fn, _rows(w.shape[0], C * 4, 2 * 2 ** 20), [w, g, m, v], [], [(C, F32)] * 4)


def _conv_shifted(hc, p1, p2, rowi):
    r1 = jnp.where(rowi == 0, p1, pltpu.roll(hc, 1, 0))
    r2 = jnp.where(rowi == 0, p2, jnp.where(rowi == 1, p1, pltpu.roll(hc, 2, 0)))
    return r1, r2


def _mix_fwd(z, ws, bsb, gv, bv, cw8, tm):
    S, DIN = z.shape
    DA = gv.shape[1]
    DB = DA
    H = DA // HEAD
    o = 2 * DA
    nch = tm // HEAD

    def body(z_ref, zp_ref, ws_ref, bsb_ref, gv_ref, bv_ref, cw_ref, y_ref):
        i = pl.program_id(0)
        ri = lax.broadcasted_iota(jnp.int32, (HEAD, HEAD), 0)
        ci = lax.broadcasted_iota(jnp.int32, (HEAD, HEAD), 1)
        tril = ri >= ci
        for h in range(H):
            cs = slice(h * HEAD, (h + 1) * HEAD)
            vs = slice(DA + h * HEAD, DA + (h + 1) * HEAD)
            wb = jnp.where(tril, ws_ref[h], 0.0).astype(BF16)
            for n in range(nch):
                rs = slice(n * HEAD, (n + 1) * HEAD)
                u = _gelu(z_ref[rs, cs])
                vhat, _ = _ln_stats(_gelu(z_ref[rs, vs]))
                vn = vhat * gv_ref[:, cs] + bv_ref[:, cs]
                mixed = jnp.dot(wb, vn.astype(BF16), preferred_element_type=F32) + bsb_ref[h]
                y_ref[rs, cs] = (u * mixed).astype(BF16)
        zb = z_ref[:, o:o + DB]
        hc = z_ref[:, o + DB:o + 2 * DB] * z_ref[:, o + 2 * DB:]
        keep = (i > 0).astype(F32)
        p1 = zp_ref[7:8, o + DB:o + 2 * DB] * zp_ref[7:8, o + 2 * DB:] * keep
        p2 = zp_ref[6:7, o + DB:o + 2 * DB] * zp_ref[6:7, o + 2 * DB:] * keep
        rowi = lax.broadcasted_iota(jnp.int32, (tm, DB), 0)
        r1, r2 = _conv_shifted(hc, p1, p2, rowi)
        cv = cw_ref[0:1, :] * r2 + cw_ref[1:2, :] * r1 + cw_ref[2:3, :] * hc
        y_ref[:, DA:] = (zb * cv).astype(BF16)

    full3 = lambda i: (0, 0, 0)
    full2 = lambda i: (0, 0)
    est = 2 * (tm * DIN * 4 + tm * (DA + DB) * 2) + 10 * tm * DB * 4
    return pl.pallas_call(
        body, name="mix_fwd", grid=(S // tm,),
        in_specs=[pl.BlockSpec((tm, DIN), lambda i: (i, 0)),
                  pl.BlockSpec((8, DIN), lambda i: (jnp.maximum(i * (tm // 8) - 1, 0), 0)),
                  pl.BlockSpec(ws.shape, full3), pl.BlockSpec(bsb.shape, full3),
                  pl.BlockSpec(gv.shape, full2), pl.BlockSpec(bv.shape, full2), pl.BlockSpec(cw8.shape, full2)],
        out_specs=pl.BlockSpec((tm, DA + DB), lambda i: (i, 0)),
        out_shape=jax.ShapeDtypeStruct((S, DA + DB), BF16),
        compiler_params=pltpu.CompilerParams(dimension_semantics=("arbitrary",),
                                             vmem_limit_bytes=_vmem_limit(est)),
    )(z, z, ws, bsb, gv, bv, cw8)


def _mix_bwd(z, dy, ws, bsb, gv, bv, cw8, tm):
    S, DIN = z.shape
    DA = gv.shape[1]
    DB = DA
    H = DA // HEAD
    o = 2 * DA
    nch = tm // HEAD
    nblk = S // tm

    def body(z_ref, zp_ref, zn_ref, dy_ref, dyn_ref, ws_ref, bsb_ref, gv_ref, bv_ref, cw_ref,
             dz_ref, dws_ref, dbs_ref, dgv_ref, dbv_ref, dcw_ref, dbin_ref):
        i = pl.program_id(0)

        @pl.when(i == 0)
        def _():
            for r in (dws_ref, dbs_ref, dgv_ref, dbv_ref, dcw_ref, dbin_ref):
                r[...] = jnp.zeros_like(r)

        ri = lax.broadcasted_iota(jnp.int32, (HEAD, HEAD), 0)
        ci = lax.broadcasted_iota(jnp.int32, (HEAD, HEAD), 1)
        tril = ri >= ci
        for h in range(H):
            cs = slice(h * HEAD, (h + 1) * HEAD)
            vs = slice(DA + h * HEAD, DA + (h + 1) * HEAD)
            wb = jnp.where(tril, ws_ref[h], 0.0).astype(BF16)
            gvh = gv_ref[:, cs]
            dws_acc = jnp.zeros((HEAD, HEAD), F32)
            dbs_acc = jnp.zeros((HEAD, HEAD), F32)
            dgv_acc = jnp.zeros((1, HEAD), F32)
            dbv_acc = jnp.zeros((1, HEAD), F32)
            dbu_acc = jnp.zeros((1, HEAD), F32)
            dbvv_acc = jnp.zeros((1, HEAD), F32)
            for n in range(nch):
                rs = slice(n * HEAD, (n + 1) * HEAD)
                u, du_dz = _gelu_and_grad(z_ref[rs, cs])
                v, dv_dz = _gelu_and_grad(z_ref[rs, vs])
                vhat, rstd = _ln_stats(v)
                vnb = (vhat * gvh + bv_ref[:, cs]).astype(BF16)
                mixed = jnp.dot(wb, vnb, preferred_element_type=F32) + bsb_ref[h]
                dya = dy_ref[rs, cs]
                dzu = dya * mixed * du_dz
                dmx = dya * u
                dmxb = dmx.astype(BF16)
                dbs_acc += jnp.broadcast_to(jnp.sum(dmx, axis=1, keepdims=True), (HEAD, HEAD))
                dws_acc += lax.dot_general(dmxb, vnb, _DIMS['nt'], preferred_element_type=F32)
                dvn = lax.dot_general(wb, dmxb, _DIMS['tn'], preferred_element_type=F32)
                dgv_acc += jnp.sum(dvn * vhat, axis=0, keepdims=True)
                dbv_acc += jnp.sum(dvn, axis=0, keepdims=True)
                dxh = dvn * gvh
                m1 = jnp.mean(dxh, axis=-1, keepdims=True)
                m2 = jnp.mean(dxh * vhat, axis=-1, keepdims=True)
                dzv = rstd * (dxh - m1 - vhat * m2) * dv_dz
                dz_ref[rs, cs] = dzu.astype(BF16)
                dz_ref[rs, vs] = dzv.astype(BF16)
                dbu_acc += jnp.sum(dzu, axis=0, keepdims=True)
                dbvv_acc += jnp.sum(dzv, axis=0, keepdims=True)
            dws_ref[h] += jnp.where(tril, dws_acc, 0.0)
            dbs_ref[h] += dbs_acc
            dgv_ref[:, cs] += dgv_acc
            dbv_ref[:, cs] += dbv_acc
            dbin_ref[:, cs] += dbu_acc
            dbin_ref[:, vs] += dbvv_acc

        zb = z_ref[:, o:o + DB]
        zc = z_ref[:, o + DB:o + 2 * DB]
        zx = z_ref[:, o + 2 * DB:]
        hc = zc * zx
        keep = (i > 0).astype(F32)
        p1 = zp_ref[7:8, o + DB:o + 2 * DB] * zp_ref[7:8, o + 2 * DB:] * keep
        p2 = zp_ref[6:7, o + DB:o + 2 * DB] * zp_ref[6:7, o + 2 * DB:] * keep
        rowi = lax.broadcasted_iota(jnp.int32, (tm, DB), 0)
        r1, r2 = _conv_shifted(hc, p1, p2, rowi)
        w0, w1, w2 = cw_ref[0:1, :], cw_ref[1:2, :], cw_ref[2:3, :]
        cv = w0 * r2 + w1 * r1 + w2 * hc
        dyb = dy_ref[:, DA:]
        dzb = dyb * cv
        dcv = dyb * zb
        more = (i < nblk - 1).astype(F32)
        n0 = dyn_ref[0:1, DA:] * zn_ref[0:1, o:o + DB] * more
        n1 = dyn_ref[1:2, DA:] * zn_ref[1:2, o:o + DB] * more
        f1 = jnp.where(rowi == tm - 1, n0, pltpu.roll(dcv, tm - 1, 0))
        f2 = jnp.where(rowi == tm - 1, n1, jnp.where(rowi == tm - 2, n0, pltpu.roll(dcv, tm - 2, 0)))
        dhc = w2 * dcv + w1 * f1 + w0 * f2
        dzc = dhc * zx
        dzx = dhc * zc
        dcw_ref[0:1, :] += jnp.sum(dcv * r2, axis=0, keepdims=True)
        dcw_ref[1:2, :] += jnp.sum(dcv * r1, axis=0, keepdims=True)
        dcw_ref[2:3, :] += jnp.sum(dcv * hc, axis=0, keepdims=True)
        dz_ref[:, o:o + DB] = dzb.astype(BF16)
        dz_ref[:, o + DB:o + 2 * DB] = dzc.astype(BF16)
        dz_ref[:, o + 2 * DB:] = dzx.astype(BF16)
        dbin_ref[:, o:o + DB] += jnp.sum(dzb, axis=0, keepdims=True)
        dbin_ref[:, o + DB:o + 2 * DB] += jnp.sum(dzc, axis=0, keepdims=True)
        dbin_ref[:, o + 2 * DB:] += jnp.sum(dzx, axis=0, keepdims=True)

    full3 = lambda i: (0, 0, 0)
    full2 = lambda i: (0, 0)
    prev8 = lambda i: (jnp.maximum(i * (tm // 8) - 1, 0), 0)
    next8 = lambda i: (jnp.minimum((i + 1) * (tm // 8), S // 8 - 1), 0)
    est = 2 * (tm * DIN * 4 + tm * (DA + DB) * 4 + tm * DIN * 2) + 16 * tm * DB * 4
    return pl.pallas_call(
        body, name="mix_bwd", grid=(nblk,),
        in_specs=[pl.BlockSpec((tm, DIN), lambda i: (i, 0)),
                  pl.BlockSpec((8, DIN), prev8), pl.BlockSpec((8, DIN), next8),
                  pl.BlockSpec((tm, DA + DB), lambda i: (i, 0)), pl.BlockSpec((8, DA + DB), next8),
                  pl.BlockSpec(ws.shape, full3), pl.BlockSpec(bsb.shape, full3),
                  pl.BlockSpec(gv.shape, full2), pl.BlockSpec(bv.shape, full2), pl.BlockSpec(cw8.shape, full2)],
        out_specs=[pl.BlockSpec((tm, DIN), lambda i: (i, 0)),
                   pl.BlockSpec(ws.shape, full3), pl.BlockSpec(ws.shape, full3),
                   pl.BlockSpec(gv.shape, full2), pl.BlockSpec(gv.shape, full2),
                   pl.BlockSpec(cw8.shape, full2), pl.BlockSpec((1, DIN), full2)],
        out_shape=[jax.ShapeDtypeStruct((S, DIN), BF16),
                   jax.ShapeDtypeStruct(ws.shape, F32), jax.ShapeDtypeStruct(ws.shape, F32),
                   jax.ShapeDtypeStruct(gv.shape, F32), jax.ShapeDtypeStruct(gv.shape, F32),
                   jax.ShapeDtypeStruct(cw8.shape, F32), jax.ShapeDtypeStruct((1, DIN), F32)],
        compiler_params=pltpu.CompilerParams(dimension_semantics=("arbitrary",),
                                             vmem_limit_bytes=_vmem_limit(est)),
    )(z, z, z, dy, dy, ws, bsb, gv, bv, cw8)


def _place():
    x, y, c = lax.axis_index("x"), lax.axis_index("y"), lax.axis_index("c")
    others = [(1 - x, y), (x, 1 - y), (1 - x, 1 - y)]
    return x, y, c, 2 * x + y, others, [2 * ox + oy for ox, oy in others]


def _half(c, rows_half):
    return pl.ds(pl.multiple_of(c * rows_half, 16), rows_half)


def _rcopy(src, dst, ssem, rsem, dev):
    return pltpu.make_async_remote_copy(src_ref=src, dst_ref=dst, send_sem=ssem, recv_sem=rsem,
                                        device_id=dev, device_id_type=MESH)


class _Comm:
    def __init__(self, aliased, ins, fresh, n_sems, stages, sink):
        self.aliased, self.ins, self.fresh = list(aliased), list(ins), list(fresh)
        self.n_sems, self.stages, self.sink = n_sems, stages, sink


def _comm_operands(comms, n_in, n_out):
    arrays, shapes, scratch, aliases = [], [], [], {}
    for cm in comms:
        for a in cm.aliased:
            aliases[n_in + len(arrays)] = n_out + len(shapes)
            arrays.append(a)
            shapes.append(jax.ShapeDtypeStruct(a.shape, a.dtype))
        arrays += cm.ins
        shapes += cm.fresh
        scratch += [pltpu.SemaphoreType.DMA((cm.n_sems,))] * 2
    return arrays, shapes, scratch, aliases


def _comm_refs(comms, in_refs, out_refs, sem_refs):
    per, pi, po = [], 0, 0
    for i, cm in enumerate(comms):
        pi += len(cm.aliased)
        ins = in_refs[pi:pi + len(cm.ins)]
        pi += len(cm.ins)
        al = out_refs[po:po + len(cm.aliased)]
        po += len(cm.aliased)
        fr = out_refs[po:po + len(cm.fresh)]
        po += len(cm.fresh)
        per.append((al, ins, fr, sem_refs[2 * i], sem_refs[2 * i + 1]))
    return per


def _comm_first_step(comms, hosted, grid):
    if not comms:
        return
    at_first = functools.reduce(lambda a, b: a & b, [pl.program_id(d) == 0 for d in range(len(grid))])

    @pl.when(at_first)
    def _():
        for cm, refs in zip(comms, hosted):
            for d in cm.stages[0](*refs, False):
                d.start()


def _comm_last_step(comms, hosted, grid):
    if not comms:
        return
    at_last = functools.reduce(lambda a, b: a & b, [pl.program_id(d) == g - 1 for d, g in enumerate(grid)])

    @pl.when(at_last)
    def _():
        for cm, refs in zip(comms, hosted):
            for d in cm.stages[0](*refs, True):
                d.wait()
        for cm, refs in zip(comms, hosted):
            for stage in cm.stages[1:]:
                for d in stage(*refs, False):
                    d.start()
                for d in stage(*refs, True):
                    d.wait()


def _deliver(comms, outs):
    pos = 0
    for cm in comms:
        al = outs[pos:pos + len(cm.aliased)]
        pos += len(cm.aliased)
        fr = outs[pos:pos + len(cm.fresh)]
        pos += len(cm.fresh)
        cm.sink(list(al), list(fr))


def _comm_call(name, comms):
    in_arrays, out_shapes, scratch, aliases = _comm_operands(comms, 0, 0)
    n_in, n_out = len(in_arrays), len(out_shapes)

    def body(*refs):
        hosted = _comm_refs(comms, refs[:n_in], refs[n_in:n_in + n_out], refs[n_in + n_out:])
        for cm, r in zip(comms, hosted):
            for stage in cm.stages:
                for d in stage(*r, False):
                    d.start()
                for d in stage(*r, True):
                    d.wait()

    outs = pl.pallas_call(body, name=name, in_specs=[ANY] * len(in_arrays), out_specs=[ANY] * len(out_shapes),
                          out_shape=out_shapes, scratch_shapes=scratch, input_output_aliases=aliases)(*in_arrays)
    _deliver(comms, outs)


def _c_gather(fulls, sink, ici, fwd, conv=None):
    n = len(fulls)
    n_ici = 3 * (n + (conv is not None)) if ici else 0

    def stage_ici(al, ins, fr, ssem, rsem, wait):
        x, y, c, chip, others, ochips = _place()
        ds = []
        for m in range(n + (conv is not None)):
            for k in range(3):
                i = 3 * m + k
                if m < n:
                    ah = al[m].shape[1] // 2
                    mine, lands = al[m].at[chip, _half(c, ah)], al[m].at[ochips[k], _half(c, ah)]
                else:
                    mine, lands = al[m].at[chip], al[m].at[ochips[k]]
                ds.append(_rcopy(mine, lands, ssem.at[i], rsem.at[i], (x, y, c)) if wait
                          else _rcopy(mine, mine, ssem.at[i], rsem.at[i], (*others[k], c)))
        return ds

    def stage_fwd(al, ins, fr, ssem, rsem, wait):
        x, y, c, chip, others, ochips = _place()
        ds = []
        for m in range(n):
            ah = al[m].shape[1] // 2
            for k in range(3):
                i = n_ici + 3 * m + k
                got = al[m].at[ochips[k], _half(c, ah)]
                ds.append(_rcopy(got, al[m].at[ochips[k], _half(1 - c, ah)], ssem.at[i], rsem.at[i], (x, y, c)) if wait
                          else _rcopy(got, got, ssem.at[i], rsem.at[i], (x, y, 1 - c)))
        return ds

    stages = ([stage_ici] if ici else []) + ([stage_fwd] if fwd else [])
    return _Comm(list(fulls) + ([conv] if conv is not None else []), [], [], n_ici + (3 * n if fwd else 0),
                 stages, sink)


def _c_rs_pair(parts, sink):
    def stage(al, ins, fr, ssem, rsem, wait):
        x, y, c, _, _, _ = _place()
        return [_rcopy(ins[m].at[:, _half(1 - c, ins[m].shape[1] // 2), :], fr[m], ssem.at[m], rsem.at[m],
                       (x, y, 1 - c)) for m in range(len(ins))]
    fresh = [jax.ShapeDtypeStruct((p.shape[0], p.shape[1] // 2, p.shape[2]), p.dtype) for p in parts]
    return _Comm([], parts, fresh, len(parts), [stage], sink)


def _c_rs_chips(psums, sink):
    def stage(al, ins, fr, ssem, rsem, wait):
        x, y, c, _, others, ochips = _place()
        return [_rcopy(ins[m].at[ochips[k]], fr[m].at[k], ssem.at[3 * m + k], rsem.at[3 * m + k], (*others[k], c))
                for m in range(len(ins)) for k in range(3)]
    fresh = [jax.ShapeDtypeStruct((3,) + p.shape[1:], p.dtype) for p in psums]
    return _Comm([], psums, fresh, 3 * len(psums), [stage], sink)


def _c_exchange(g2s, sink):
    def stage(al, ins, fr, ssem, rsem, wait):
        x, y, c, _, _, _ = _place()
        if wait:
            return [_rcopy(al[m].at[c], al[m].at[1 - c], ssem.at[m], rsem.at[m], (x, y, c)) for m in range(len(al))]
        return [_rcopy(al[m].at[c], al[m].at[c], ssem.at[m], rsem.at[m], (x, y, 1 - c)) for m in range(len(al))]
    return _Comm(g2s, [], [], len(g2s), [stage], sink)


def _c_allgather_small(buf, sink):
    def stage(al, ins, fr, ssem, rsem, wait):
        x, y, c, _, _, _ = _place()
        ds = []
        for r in range(1, 8):
            peer = (1 - x if r & 4 else x, 1 - y if r & 2 else y, 1 - c if r & 1 else c)
            ds.append(_rcopy(al[0].at[0], al[0].at[r], ssem.at[r - 1], rsem.at[r - 1], peer))
        return ds
    return _Comm([buf], [], [], 7, [stage], sink)


def _sum_small(buf, dev_idx):
    _, m, n = buf.shape

    def body(me_ref, b_ref, o_ref):
        me = me_ref[0]
        acc = b_ref[me]
        for d in range(1, 8):
            acc = acc + b_ref[jnp.bitwise_xor(me, d)]
        o_ref[...] = acc

    gs = pltpu.PrefetchScalarGridSpec(
        num_scalar_prefetch=1, grid=(1,),
        in_specs=[pl.BlockSpec(buf.shape, lambda i, me: (0, 0, 0))],
        out_specs=pl.BlockSpec((m, n), lambda i, me: (0, 0)))
    return pl.pallas_call(body, name="sum_small", grid_spec=gs,
                          out_shape=jax.ShapeDtypeStruct((m, n), F32))(dev_idx, buf)


def _sum_pair(name, part, sib, c_idx):
    _, A, B = part.shape
    ah = A // 2
    br = _rows(ah, B * 2, 2 * 2 ** 20)

    def body(c_ref, a_ref, b_ref, o_ref):
        o_ref[...] = (a_ref[...].astype(F32) + b_ref[...].astype(F32)).astype(BF16)

    gs = pltpu.PrefetchScalarGridSpec(
        num_scalar_prefetch=1, grid=(N_CHIPS, ah // br),
        in_specs=[pl.BlockSpec((None, None, br, B), lambda j, i, c: (j, c[0], i, 0)),
                  pl.BlockSpec((None, br, B), lambda j, i, c: (j, i, 0))],
        out_specs=pl.BlockSpec((None, br, B), lambda j, i, c: (j, i, 0)))
    return pl.pallas_call(body, name=name, grid_spec=gs,
                          out_shape=jax.ShapeDtypeStruct((N_CHIPS, ah, B), BF16))(
                              c_idx, part.reshape(N_CHIPS, 2, ah, B), sib)


def _sum_chips(name, psum, recv, place_idx):
    _, ah, B = psum.shape
    br = _rows(ah, B * 4, 2 * 2 ** 20)

    def body(c_ref, p_ref, r0_ref, r1_ref, r2_ref, o_ref):
        o_ref[...] = ((p_ref[...].astype(F32) + r0_ref[...].astype(F32)) + r1_ref[...].astype(F32)) \
            + r2_ref[...].astype(F32)

    gs = pltpu.PrefetchScalarGridSpec(
        num_scalar_prefetch=1, grid=(ah // br,),
        in_specs=[pl.BlockSpec((None, br, B), lambda i, c: (c[0], i, 0))]
        + [pl.BlockSpec((None, br, B), functools.partial(lambda i, c, k: (k, i, 0), k=k)) for k in range(3)],
        out_specs=pl.BlockSpec((None, br, B), lambda i, c: (c[1], i, 0)))
    return pl.pallas_call(body, name=name, grid_spec=gs,
                          out_shape=jax.ShapeDtypeStruct((2, ah, B), F32))(place_idx, psum, recv, recv, recv)


def _cast_into(name, a, place_idx):
    A, B = a.shape
    br = _rows(A, B * 4)

    def body(c_ref, a_ref, o_ref):
        o_ref[...] = a_ref[...].astype(BF16)

    gs = pltpu.PrefetchScalarGridSpec(
        num_scalar_prefetch=1, grid=(A // br,),
        in_specs=[pl.BlockSpec((br, B), lambda i, c: (i, 0))],
        out_specs=pl.BlockSpec((None, br, B), lambda i, c: (c[0], i, 0)))
    return pl.pallas_call(body, name=name, grid_spec=gs,
                          out_shape=jax.ShapeDtypeStruct((N_CHIPS, A, B), BF16))(place_idx, a)


def _adamw_small(gsum, gconv, offsets, ws, ms, vs):
    n = len(ws)

    def body(*refs):
        g_ref, gc_ref = refs[0], refs[1]
        w_refs = refs[2:2 + n]
        m_refs = refs[2 + n:2 + 2 * n]
        v_refs = refs[2 + 2 * n:2 + 3 * n]
        outs = refs[2 + 3 * n:]
        for i in range(n):
            rows = w_refs[i].shape[0]
            g = gc_ref[...] if offsets[i] is None else g_ref[offsets[i]:offsets[i] + rows, :]
            d, m2, v2 = _adamw_math(w_refs[i][...], g, m_refs[i][...], v_refs[i][...])
            outs[4 * i][...] = g
            outs[4 * i + 1][...] = d
            outs[4 * i + 2][...] = m2
            outs[4 * i + 3][...] = v2

    vm = pl.BlockSpec(memory_space=pltpu.VMEM)
    out_shape = []
    for w in ws:
        out_shape += [jax.ShapeDtypeStruct(w.shape, F32)] * 4
    return pl.pallas_call(body, name="adamw_small", in_specs=[vm] * (2 + 3 * n), out_specs=[vm] * (4 * n),
                          out_shape=out_shape)(gsum, gconv, *ws, *ms, *vs)


def _ident(accs, tes, res):
    return [accs[0]]


def _ffn_up(name, hb, wg4, wu4, comms=()):
    S, D = hb.shape

    def epi(accs, tes, res):
        g, u = accs
        return [g, u, g * _sigmoid(g) * u]

    return _mm(name, 'nn', [hb], [wg4, wu4], [(0, 0, 0), (0, 1, 1)], 2, epi, [BF16] * 3,
               bm=_blk(S, 512), bn=wg4.shape[2], bk=D, b_blocked=True, n_outer=True, comms=comms)


def _ffn_gate(name, hb, wg4, comms=()):
    S, D = hb.shape
    return _mm(name, 'nn', [hb], [wg4], [(0, 0, 0)], 1, _ident, [BF16],
               bm=_blk(S, 1024), bn=wg4.shape[2], bk=D, b_blocked=True, n_outer=True, comms=comms)[0]


def _ffn_upmul(name, hb, wu4, gb, comms=()):
    S, D = hb.shape

    def epi(accs, tes, res):
        g = tes[0].astype(F32)
        return [accs[0], g * _sigmoid(g) * accs[0]]

    return _mm(name, 'nn', [hb], [wu4], [(0, 0, 0)], 1, epi, [BF16, BF16],
               bm=_blk(S, 1024), bn=wu4.shape[2], bk=D, tile_extras=[gb], b_blocked=True, n_outer=True, comms=comms)


def _ffn_down(name, ab, wd, hres, comms=()):
    S, F = ab.shape
    D = wd.shape[1]
    return _mm(name, 'nn', [ab], [wd], [(0, 0, 0)], 1, lambda accs, tes, res: [ALPHA * tes[0] + 0.5 * accs[0]],
               [F32], bm=_blk(S, 1024), bn=_blk(D, 512), bk=F, tile_extras=[hres], comms=comms)[0]


def _ffn_da(name, dsub, wd, gb, ub, comms=()):
    S, D = dsub.shape
    F = wd.shape[0]

    def epi(accs, tes, res):
        da = accs[0]
        g = tes[0].astype(F32)
        u = tes[1].astype(F32)
        s = _sigmoid(g)
        silu = g * s
        return [(da * u) * (s + silu * (1.0 - s)), da * silu]

    return _mm(name, 'nt', [dsub], [wd], [(0, 0, 0)], 1, epi, [BF16, BF16],
               bm=_blk(S, 1024), bn=_blk(F, 512), bk=D, tile_extras=[gb, ub], comms=comms)


def _dw_rows(name, ab, dsub, cs, comms=()):
    S, D = dsub.shape
    return _mm(name, 'tn', [ab], [dsub], [(0, 0, 0)], 1, _ident, [BF16],
               bm=cs, bn=_blk(D, 1024), bk=_blk(S, 2048), comms=comms)[0].reshape(N_CHIPS, cs, D)


def _dw_cols(name, hb, dxb, cs, comms=()):
    S, D = hb.shape
    return _mm(name, 'tn', [hb], [dxb], [(0, 0, 0)], 1, _ident, [BF16],
               bm=_blk(D, 1024), bn=cs, bk=_blk(S, 2048), out_blocked=True, comms=comms)[0]


def _ffn_dh(name, dgb, dub, wg4, wu4, dres, comms=()):
    S = dgb.shape[0]
    D = wg4.shape[1]
    return _mm(name, 'nt', [dgb, dub], [wg4, wu4], [(0, 0, 0), (1, 1, 0)], 1,
               lambda accs, tes, res: [accs[0] + tes[0]], [F32],
               bm=_blk(S, 1024), bn=_blk(D, 1024), bk=wg4.shape[2], tile_extras=[dres], b_blocked=True,
               comms=comms)[0]


def _step(p):
    x = p['x'][0]
    target = p['loss_target'][0]
    S, D = x.shape
    cx, cy, cc = lax.axis_index("x"), lax.axis_index("y"), lax.axis_index("c")
    chip = 2 * cx + cy
    c_idx = jnp.reshape(cc, (1,)).astype(jnp.int32)
    place_idx = jnp.stack([chip, cc]).astype(jnp.int32)

    w_s = p['w_s'][0]
    H = w_s.shape[0]
    DA = H * HEAD
    bsb = jnp.broadcast_to(p['b_s'][0][:, :, None], (H, HEAD, HEAD))
    conv_shard = p['conv_w'][0]
    conv8 = jnp.zeros((8, conv_shard.shape[1]), F32).at[:CONV_TAPS].set(conv_shard)

    W = {n: _cast_into(f"cast_{n}", p[n][0], place_idx) for n in BIG}
    conv = {'full': lax.dynamic_update_slice(jnp.zeros((N_CHIPS,) + conv8.shape, F32), conv8[None], (chip, 0, 0))}

    def gathered(names, with_conv=False):
        def sink(al, fr):
            W.update(zip(names, al))
            if with_conv:
                conv['full'] = al[len(names)]
        return sink

    def c_gather(names, ici, fwd, with_conv=False):
        return _c_gather([W[n] for n in names], gathered(names, with_conv), ici, fwd,
                         conv['full'] if with_conv else None)

    xb = _cast_bf16("cast_x", x, comms=[c_gather(['ffa_gate'], True, True)])
    cs_ff = W['ffa_gate'].shape[2]
    F = N_CHIPS * cs_ff

    ga = _ffn_gate("ffa_gate", xb, W['ffa_gate'],
                   comms=[c_gather(['ffa_up'], True, True), c_gather(['ffa_down'], True, False)])
    ua, aa = _ffn_upmul("ffa_up", xb, W['ffa_up'], ga,
                        comms=[c_gather(['ffa_down'], False, True),
                               c_gather(['w_in', 'ffc_gate'], True, False, with_conv=True)])
    cw8 = jnp.transpose(conv['full'], (1, 0, 2)).reshape(8, DA)
    wd_a = W['ffa_down'].reshape(F, D)
    r1 = _ffn_down("ffa_down", aa, wd_a, x,
                   comms=[c_gather(['w_in', 'ffc_gate'], False, True), c_gather(['w_out', 'ffc_down'], True, False)])
    h1, h1b = _ln_fwd("ln_a", r1, p['ln_a_g'], p['ln_a_b'])
    w_in4 = W['w_in']
    cs_in = w_in4.shape[2]
    z = _mm("w_in", 'nn', [h1b], [w_in4], [(0, 0, 0)], 1, lambda accs, tes, res: [accs[0] + res[0]], [F32],
            bm=_blk(S, 1024), bn=cs_in, bk=D, row_extras=[p['b_in']], b_blocked=True, n_outer=True,
            comms=[c_gather(['w_out', 'ffc_down'], False, True), c_gather(['ffc_up'], True, False)])[0]
    w_out = W['w_out'].reshape(DA * 2, D)
    tm = _blk(S, 256)
    yb = _mix_fwd(z, w_s, bsb, p['ln_v_g'], p['ln_v_b'], cw8, tm)
    r2 = _mm("w_out", 'nn', [yb], [w_out], [(0, 0, 0)], 1,
             lambda accs, tes, res: [accs[0] + res[0] + ALPHA * tes[0]], [F32],
             bm=_blk(S, 1024), bn=_blk(D, 1024), bk=2 * DA, tile_extras=[h1], row_extras=[p['b_out']],
             comms=[c_gather(['ffc_up'], False, True)])[0]
    h2, h2b = _ln_fwd("ln_m", r2, p['ln_m_g'], p['ln_m_b'])
    gc, uc, ac = _ffn_up("ffc_up", h2b, W['ffc_gate'], W['ffc_up'])
    wd_c = W['ffc_down'].reshape(F, D)
    r3 = _ffn_down("ffc_down", ac, wd_c, h2)

    pg, sg, sib, psum, recv, g2 = {}, {}, {}, {}, {}, {}

    def c_pair(names):
        return _c_rs_pair([pg[n] for n in names], lambda al, fr: sib.update(zip(names, fr)))

    def c_chips(names):
        return _c_rs_chips([psum[n] for n in names], lambda al, fr: recv.update(zip(names, fr)))

    def c_swap(names):
        return _c_exchange([g2[n] for n in names], lambda al, fr: g2.update(zip(names, al)))

    def sum_pair(*names):
        for n in names:
            psum[n] = _sum_pair(f"sum_pair_{n}", pg[n], sib[n], c_idx)

    def sum_chips(*names):
        for n in names:
            g2[n] = _sum_chips(f"sum_chips_{n}", psum[n], recv[n], place_idx)

    dres3, dsub3, sg['ln_c_g'], sg['ln_c_b'], lsum = _ln_loss_bwd(
        "ln_c_loss_bwd", r3, target, p['ln_c_g'], p['ln_c_b'], 0.5)

    dgc, duc = _ffn_da("ffc_da", dsub3, wd_c, gc, uc)
    pg['ffc_down'] = _dw_rows("ffc_dwd", ac, dsub3, cs_ff)
    pg['ffc_gate'] = _dw_cols("ffc_dwg", h2b, dgc, cs_ff, comms=[c_pair(['ffc_down'])])
    sum_pair('ffc_down')
    pg['ffc_up'] = _dw_cols("ffc_dwu", h2b, duc, cs_ff, comms=[c_pair(['ffc_gate']), c_chips(['ffc_down'])])
    sum_pair('ffc_gate')
    dh2 = _ffn_dh("ffc_dh", dgc, duc, W['ffc_gate'], W['ffc_up'], dres3,
                  comms=[c_pair(['ffc_up']), c_chips(['ffc_gate'])])
    sum_pair('ffc_up')
    sum_chips('ffc_down', 'ffc_gate')

    dres2, dmix, sg['ln_m_g'], sg['ln_m_b'], sg['b_out'] = _ln_bwd("ln_m_bwd", r2, dh2, p['ln_m_g'], 1.0)
    pg['w_out'] = _mm("dw_out", 'tn', [yb], [dmix], [(0, 0, 0)], 1, _ident, [BF16],
                      bm=_blk(2 * DA, 1024), bn=_blk(D, 1024), bk=_blk(S, 2048),
                      comms=[c_swap(['ffc_down', 'ffc_gate'])])[0].reshape(N_CHIPS, 2 * DA // 4, D)
    dy = _mm("dy", 'nt', [dmix], [w_out], [(0, 0, 0)], 1, _ident, [F32],
             bm=_blk(S, 1024), bn=_blk(2 * DA, 1024), bk=D, comms=[c_pair(['w_out'])])[0]
    sum_pair('w_out')
    dz, dws, dbs, sg['ln_v_g'], sg['ln_v_b'], dcw, sg['b_in'] = _mix_bwd(
        z, dy, w_s, bsb, p['ln_v_g'], p['ln_v_b'], cw8, tm)
    sg['w_s'] = dws
    sg['b_s'] = dbs[:, :, 0]
    pg['w_in'] = _dw_cols("dw_in", h1b, dz, cs_in, comms=[c_chips(['ffc_up']), c_chips(['w_out'])])
    sum_chips('ffc_up', 'w_out')
    dh1 = _mm("dh1", 'nt', [dz], [w_in4], [(0, 0, 0)], 1, lambda accs, tes, res: [accs[0] + tes[0]], [F32],
              bm=_blk(S, 1024), bn=_blk(D, 1024), bk=cs_in, tile_extras=[dres2], b_blocked=True,
              comms=[c_pair(['w_in']), c_swap(['ffc_up', 'w_out'])])[0]
    sum_pair('w_in')

    dres1, dsub1, sg['ln_a_g'], sg['ln_a_b'], _ = _ln_bwd("ln_a_bwd", r1, dh1, p['ln_a_g'], 0.5)
    def as_rows(a):
        a = a.reshape(-1, 128)
        pad = (-a.shape[0]) % 8
        return jnp.pad(a, ((0, pad), (0, 0))) if pad else a

    pieces, offsets, off = [], {}, 0
    for n in SMALL + ['conv_w']:
        piece = as_rows(dcw[:CONV_TAPS] if n == 'conv_w' else sg[n])
        offsets[n] = off
        off += piece.shape[0]
        pieces.append(piece)
    packed = jnp.concatenate(pieces, axis=0)
    small = {'buf': jnp.zeros((8,) + packed.shape, F32).at[0].set(packed)}
    dga, dua = _ffn_da("ffa_da", dsub1, wd_a, ga, ua,
                       comms=[c_chips(['w_in']), _c_allgather_small(small['buf'], lambda al, fr: small.update(buf=al[0]))])
    sum_chips('w_in')
    gsum = _sum_small(small['buf'], jnp.reshape(4 * cx + 2 * cy + cc, (1,)).astype(jnp.int32))
    pg['ffa_down'] = _dw_rows("ffa_dwd", aa, dsub1, cs_ff, comms=[c_swap(['w_in'])])
    pg['ffa_gate'] = _dw_cols("ffa_dwg", xb, dga, cs_ff, comms=[c_pair(['ffa_down'])])
    sum_pair('ffa_down')
    pg['ffa_up'] = _dw_cols("ffa_dwu", xb, dua, cs_ff, comms=[c_pair(['ffa_gate']), c_chips(['ffa_down'])])
    sum_pair('ffa_gate')
    _comm_call("pair_ffa_up", [c_pair(['ffa_up'])])
    sum_pair('ffa_up')
    sum_chips('ffa_down')
    dx = _ffn_dh("ffa_dh", dga, dua, W['ffa_gate'], W['ffa_up'], dres1,
                 comms=[c_chips(['ffa_gate']), c_chips(['ffa_up']), c_swap(['ffa_down'])])
    sum_chips('ffa_gate', 'ffa_up')
    _comm_call("swap_ffa", [c_swap(['ffa_gate', 'ffa_up'])])

    out = {}
    for n in BIG:
        g, d, m2, v2 = _adamw(f"adamw_{n}", p[n][0], g2[n].reshape(p[n][0].shape), p['m_' + n][0], p['v_' + n][0])
        shp = p[n].shape
        out[n] = (g.reshape(shp), d.reshape(shp), m2.reshape(shp), v2.reshape(shp))

    loss = lax.psum(0.5 * jnp.sum(lsum) / D, ("x", "y", "c"))
    n_conv = CONV_TAPS * DA // 128
    conv_sum = gsum[offsets['conv_w']:offsets['conv_w'] + n_conv].reshape(CONV_TAPS, DA)
    cw = conv_shard.shape[1]
    gconv = lax.dynamic_slice_in_dim(conv_sum, (2 * cx + cy) * cw, cw, axis=1).reshape(-1, 128)
    names = SMALL + ['conv_w']
    rows128 = lambda a: a.reshape(-1, 128)
    res = _adamw_small(gsum, gconv, [offsets[n] if n != 'conv_w' else None for n in names],
                       [rows128(p[n]) for n in names], [rows128(p['m_' + n]) for n in names],
                       [rows128(p['v_' + n]) for n in names])
    for i, n in enumerate(names):
        out[n] = tuple(r.reshape(p[n].shape) for r in res[4 * i:4 * i + 4])

    return (loss, dx.reshape(p['x'].shape), *[out[n][0] for n in WEIGHTS], *[out[n][1] for n in WEIGHTS],
            *[out[n][2] for n in WEIGHTS], *[out[n][3] for n in WEIGHTS])


def kernel(x, ffa_gate, ffa_up, ffa_down, ln_a_g, ln_a_b, w_in, b_in, w_s, b_s, ln_v_g, ln_v_b, conv_w, w_out, b_out, ln_m_g, ln_m_b, ffc_gate, ffc_up, ffc_down, ln_c_g, ln_c_b, loss_target, m_ffa_gate, m_ffa_up, m_ffa_down, m_ln_a_g, m_ln_a_b, m_w_in, m_b_in, m_w_s, m_b_s, m_ln_v_g, m_ln_v_b, m_conv_w, m_w_out, m_b_out, m_ln_m_g, m_ln_m_b, m_ffc_gate, m_ffc_up, m_ffc_down, m_ln_c_g, m_ln_c_b, v_ffa_gate, v_ffa_up, v_ffa_down, v_ln_a_g, v_ln_a_b, v_w_in, v_b_in, v_w_s, v_b_s, v_ln_v_g, v_ln_v_b, v_conv_w, v_w_out, v_b_out, v_ln_m_g, v_ln_m_b, v_ffc_gate, v_ffc_up, v_ffc_down, v_ln_c_g, v_ln_c_b):
    return _step(dict(locals()))
```

```python
import functools
import math

import jax
import jax.numpy as jnp
from jax import lax
from jax.experimental import pallas as pl
from jax.experimental.pallas import tpu as pltpu

F32 = jnp.float32
BF16 = jnp.bfloat16
MESH = pl.DeviceIdType.MESH
ANY = pl.BlockSpec(memory_space=pl.ANY)

HEAD = 128
CONV_TAPS = 3
LN_EPS = 1e-5
ALPHA = float(2.0 ** 0.25)
ADAM_LR, ADAM_B1, ADAM_B2, ADAM_EPS, ADAM_WD, ADAM_STEP = 0.001, 0.9, 0.999, 1e-08, 0.01, 10

V7X_VMEM_BYTES = 64 * 2 ** 20
N_CHIPS = 4

BIG = ['ffa_gate', 'ffa_up', 'ffa_down', 'w_in', 'w_out', 'ffc_gate', 'ffc_up', 'ffc_down']
WEIGHTS = ['ffa_gate', 'ffa_up', 'ffa_down', 'ln_a_g', 'ln_a_b', 'w_in', 'b_in', 'w_s', 'b_s', 'ln_v_g', 'ln_v_b',
           'conv_w', 'w_out', 'b_out', 'ln_m_g', 'ln_m_b', 'ffc_gate', 'ffc_up', 'ffc_down', 'ln_c_g', 'ln_c_b']
SMALL = [n for n in WEIGHTS if n not in BIG and n != 'conv_w']


def _vmem_limit(estimate_bytes):
    return int(min(max(estimate_bytes * 1.3, 32 * 2 ** 20), V7X_VMEM_BYTES - 6 * 2 ** 20))


def _blk(dim, pref, mult=128):
    if dim <= pref:
        return dim
    best = None
    for d in range(mult, pref + 1, mult):
        if dim % d == 0:
            best = d
    assert best is not None, (dim, pref)
    return best


def _rows(n_rows, row_bytes, target=4 * 2 ** 20):
    return _blk(n_rows, max(16, target // row_bytes), 16)


def _sigmoid(x):
    return 0.5 * jnp.tanh(0.5 * x) + 0.5


def _gelu(x):
    c = math.sqrt(2.0 / math.pi)
    return 0.5 * x * (1.0 + jnp.tanh(c * (x + 0.044715 * (x * x * x))))


def _gelu_and_grad(x):
    c = math.sqrt(2.0 / math.pi)
    t = jnp.tanh(c * (x + 0.044715 * (x * x * x)))
    g = 0.5 * x * (1.0 + t)
    dg = 0.5 * (1.0 + t) + 0.5 * x * (1.0 - t * t) * (c * (1.0 + 3.0 * 0.044715 * (x * x)))
    return g, dg


def _ln_stats(r):
    mu = jnp.mean(r, axis=-1, keepdims=True)
    d = r - mu
    var = jnp.mean(d * d, axis=-1, keepdims=True)
    rstd = lax.rsqrt(var + LN_EPS)
    return d * rstd, rstd


_DIMS = {'nn': (((1,), (0,)), ((), ())), 'nt': (((1,), (1,)), ((), ())), 'tn': (((0,), (0,)), ((), ()))}


def _mm(name, mode, a_list, b_list, pairs, n_acc, epilogue, out_dtypes, bm, bn, bk,
        tile_extras=(), row_extras=(), b_blocked=False, out_blocked=False, n_outer=False, comms=()):
    a0, b0 = a_list[0], b_list[0]
    if mode == 'nn':
        M, K = a0.shape
        N = b0.shape[0] * b0.shape[2] if b_blocked else b0.shape[1]
    elif mode == 'nt':
        M, K = a0.shape
        N = b0.shape[1] if b_blocked else b0.shape[0]
    else:
        K, M = a0.shape
        N = b0.shape[1]
        assert not b_blocked
    assert M % bm == 0 and N % bn == 0 and K % bk == 0, (name, M, N, K, bm, bn, bk)
    gm, gn, gk = M // bm, N // bn, K // bk
    if b_blocked:
        assert (bn if mode == 'nn' else bk) == b0.shape[2], name
    dims = _DIMS[mode]

    def spec(shape, f):
        if n_outer:
            return pl.BlockSpec(shape, lambda g0, g1, g2: f(g1, g0, g2))
        return pl.BlockSpec(shape, f)

    if mode == 'tn':
        a_spec = spec((bk, bm), lambda i, j, k: (k, i))
    else:
        a_spec = spec((bm, bk), lambda i, j, k: (i, k))
    if mode == 'nn':
        b_spec = (spec((None, bk, bn), lambda i, j, k: (j, k, 0)) if b_blocked
                  else spec((bk, bn), lambda i, j, k: (k, j)))
    elif mode == 'nt':
        b_spec = (spec((None, bn, bk), lambda i, j, k: (k, j, 0)) if b_blocked
                  else spec((bn, bk), lambda i, j, k: (j, k)))
    else:
        b_spec = spec((bk, bn), lambda i, j, k: (k, j))
    te_spec = spec((bm, bn), lambda i, j, k: (i, j))
    re_spec = spec((1, bn), lambda i, j, k: (0, j))
    if out_blocked:
        o_spec = spec((None, bm, bn), lambda i, j, k: (j, i, 0))
        o_shape = (gn, M, bn)
    else:
        o_spec = te_spec
        o_shape = (M, N)

    na, nb, nte, nre, no = len(a_list), len(b_list), len(tile_extras), len(row_extras), len(out_dtypes)

    n_scratch_acc = n_acc if gk > 1 else 0
    grid = (gn, gm, gk) if n_outer else (gm, gn, gk)
    cm_in_arrays, cm_out_shapes, cm_scratch, aliases = _comm_operands(comms, na + nb + nte + nre, no)
    n_cm_in, n_cm_out = len(cm_in_arrays), len(cm_out_shapes)

    def body(*refs):
        a_refs = refs[:na]
        b_refs = refs[na:na + nb]
        te_refs = refs[na + nb:na + nb + nte]
        re_refs = refs[na + nb + nte:na + nb + nte + nre]
        n_in = na + nb + nte + nre
        o_refs = refs[n_in + n_cm_in:n_in + n_cm_in + no]
        acc_refs = refs[n_in + n_cm_in + no + n_cm_out:n_in + n_cm_in + no + n_cm_out + n_scratch_acc]
        hosted = _comm_refs(comms, refs[n_in:n_in + n_cm_in], refs[n_in + n_cm_in + no:n_in + n_cm_in + no + n_cm_out],
                            refs[n_in + n_cm_in + no + n_cm_out + n_scratch_acc:])
        _comm_first_step(comms, hosted, grid)

        def finish(accs):
            outs = epilogue(accs, [r[...] for r in te_refs], [r[...] for r in re_refs])
            for o_ref, o in zip(o_refs, outs):
                o_ref[...] = o.astype(o_ref.dtype)

        def products():
            parts = [None] * n_acc
            for ai, bi, ci in pairs:
                d = lax.dot_general(a_refs[ai][...], b_refs[bi][...], dims, preferred_element_type=F32)
                parts[ci] = d if parts[ci] is None else parts[ci] + d
            return parts

        if gk == 1:
            finish(products())
        else:
            kk = pl.program_id(2)

            @pl.when(kk == 0)
            def _():
                for r, part in zip(acc_refs, products()):
                    r[...] = part

            if gk > 2:
                @pl.when((kk > 0) & (kk < gk - 1))
                def _():
                    for r, part in zip(acc_refs, products()):
                        r[...] += part

            @pl.when(kk == gk - 1)
            def _():
                finish([r[...] + part for r, part in zip(acc_refs, products())])

        _comm_last_step(comms, hosted, grid)

    isz = lambda a: jnp.dtype(a.dtype).itemsize
    est = 2 * (sum(bm * bk * isz(a) for a in a_list) + sum(bk * bn * isz(b) for b in b_list)
               + sum(bm * bn * isz(t) for t in tile_extras)
               + sum(bm * bn * jnp.dtype(d).itemsize for d in out_dtypes))
    est += (2 * n_acc + 2) * bm * bn * 4
    outs = pl.pallas_call(
        body, name=name, grid=grid,
        in_specs=[a_spec] * na + [b_spec] * nb + [te_spec] * nte + [re_spec] * nre + [ANY] * len(cm_in_arrays),
        out_specs=[o_spec] * no + [ANY] * len(cm_out_shapes),
        out_shape=[jax.ShapeDtypeStruct(o_shape, d) for d in out_dtypes] + cm_out_shapes,
        scratch_shapes=[pltpu.VMEM((bm, bn), F32)] * n_scratch_acc + cm_scratch,
        input_output_aliases=aliases,
        compiler_params=pltpu.CompilerParams(dimension_semantics=("parallel", "parallel", "arbitrary"),
                                             vmem_limit_bytes=_vmem_limit(est)),
    )(*a_list, *b_list, *tile_extras, *row_extras, *cm_in_arrays)
    _deliver(comms, outs[no:])
    return outs[:no]


def _rowwise(name, fn, br, row_ins, vec_ins, row_outs, acc_outs=(), comms=()):
    S = row_ins[0].shape[0]
    assert S % br == 0, (name, S, br)
    nr, nv, no, nacc = len(row_ins), len(vec_ins), len(row_outs), len(acc_outs)
    grid = (S // br,)
    cm_in_arrays, cm_out_shapes, cm_scratch, aliases = _comm_operands(comms, nr + nv, no + nacc)
    n_cm_in, n_cm_out = len(cm_in_arrays), len(cm_out_shapes)

    def body(*refs):
        rin = refs[:nr]
        vin = refs[nr:nr + nv]
        o0 = nr + nv + n_cm_in
        rout = refs[o0:o0 + no]
        aout = refs[o0 + no:o0 + no + nacc]
        hosted = _comm_refs(comms, refs[nr + nv:o0], refs[o0 + no + nacc:o0 + no + nacc + n_cm_out],
                            refs[o0 + no + nacc + n_cm_out:])
        _comm_first_step(comms, hosted, grid)
        if nacc:
            @pl.when(pl.program_id(0) == 0)
            def _():
                for a in aout:
                    a[...] = jnp.zeros_like(a)
        fn(rin, vin, rout, aout)
        _comm_last_step(comms, hosted, grid)

    def full_spec(shape):
        nd = len(shape)
        return pl.BlockSpec(shape, lambda i: (0,) * nd)

    est = 2 * (sum(br * a.shape[1] * jnp.dtype(a.dtype).itemsize for a in row_ins)
               + sum(br * c * jnp.dtype(d).itemsize for c, d in row_outs))
    est += 6 * br * max(a.shape[1] for a in row_ins) * 4
    outs = pl.pallas_call(
        body, name=name, grid=grid,
        in_specs=[pl.BlockSpec((br, a.shape[1]), lambda i: (i, 0)) for a in row_ins]
        + [full_spec(v.shape) for v in vec_ins] + [ANY] * n_cm_in,
        out_specs=[pl.BlockSpec((br, c), lambda i: (i, 0)) for c, _ in row_outs]
        + [full_spec(s) for s, _ in acc_outs] + [ANY] * n_cm_out,
        out_shape=[jax.ShapeDtypeStruct((S, c), d) for c, d in row_outs]
        + [jax.ShapeDtypeStruct(s, d) for s, d in acc_outs] + cm_out_shapes,
        scratch_shapes=cm_scratch,
        input_output_aliases=aliases,
        compiler_params=pltpu.CompilerParams(dimension_semantics=("arbitrary",),
                                             vmem_limit_bytes=_vmem_limit(est)),
    )(*row_ins, *vec_ins, *cm_in_arrays)
    _deliver(comms, outs[no + nacc:])
    return outs[:no + nacc]


def _ln_fwd(name, r, g, b):
    def fn(rin, vin, rout, aout):
        xhat, _ = _ln_stats(rin[0][...])
        h = xhat * vin[0][...] + vin[1][...]
        rout[0][...] = h
        rout[1][...] = h.astype(BF16)
    D = r.shape[1]
    return _rowwise(name, fn, _rows(r.shape[0], D * 4, 2 * 2 ** 20), [r], [g, b], [(D, F32), (D, BF16)])


def _ln_loss_bwd(name, r, target, g, b, sub_scale):
    D = r.shape[1]

    def fn(rin, vin, rout, aout):
        xhat, rstd = _ln_stats(rin[0][...])
        gain = vin[0][...]
        err = xhat * gain + vin[1][...] - rin[1][...]
        dhv = err * (1.0 / D)
        dxh = dhv * gain
        m1 = jnp.mean(dxh, axis=-1, keepdims=True)
        m2 = jnp.mean(dxh * xhat, axis=-1, keepdims=True)
        dr = rstd * (dxh - m1 - xhat * m2)
        rout[0][...] = ALPHA * dr
        rout[1][...] = (sub_scale * dr).astype(BF16)
        aout[0][...] += jnp.sum(dhv * xhat, axis=0, keepdims=True)
        aout[1][...] += jnp.sum(dhv, axis=0, keepdims=True)
        aout[2][...] += jnp.sum(err * err, axis=0, keepdims=True)
    return _rowwise(name, fn, _rows(r.shape[0], D * 4, 2 * 2 ** 20), [r, target], [g, b],
                    [(D, F32), (D, BF16)], [((1, D), F32)] * 3)


def _ln_bwd(name, r, dh, g, sub_scale):
    D = r.shape[1]

    def fn(rin, vin, rout, aout):
        xhat, rstd = _ln_stats(rin[0][...])
        dhv = rin[1][...]
        dxh = dhv * vin[0][...]
        m1 = jnp.mean(dxh, axis=-1, keepdims=True)
        m2 = jnp.mean(dxh * xhat, axis=-1, keepdims=True)
        dr = rstd * (dxh - m1 - xhat * m2)
        rout[0][...] = ALPHA * dr
        rout[1][...] = (sub_scale * dr).astype(BF16)
        aout[0][...] += jnp.sum(dhv * xhat, axis=0, keepdims=True)
        aout[1][...] += jnp.sum(dhv, axis=0, keepdims=True)
        aout[2][...] += jnp.sum(dr, axis=0, keepdims=True)
    return _rowwise(name, fn, _rows(r.shape[0], D * 4, 2 * 2 ** 20), [r, dh], [g],
                    [(D, F32), (D, BF16)], [((1, D), F32)] * 3)


def _adamw_math(w, g, m, v):
    m2 = ADAM_B1 * m + (1.0 - ADAM_B1) * g
    v2 = ADAM_B2 * v + (1.0 - ADAM_B2) * (g * g)
    m_hat = m2 / (1.0 - ADAM_B1 ** ADAM_STEP)
    v_hat = v2 / (1.0 - ADAM_B2 ** ADAM_STEP)
    delta = -ADAM_LR * (m_hat / (jnp.sqrt(v_hat) + ADAM_EPS) + ADAM_WD * w)
    return delta, m2, v2


def _adamw(name, w, g, m, v, comms=()):
    def fn(rin, vin, rout, aout):
        gv = rin[1][...]
        d, m2, v2 = _adamw_math(rin[0][...], gv, rin[2][...], rin[3][...])
        rout[0][...] = gv
        rout[1][...] = d
        rout[2][...] = m2
        rout[3][...] = v2
    C = w.shape[1]
    return _rowwise(name, fn, _rows(w.shape[0], C * 4, 2 * 2 ** 20), [w, g, m, v], [], [(C, F32)] * 4, comms=comms)


def _conv_shifted(hc, p1, p2, rowi):
    r1 = jnp.where(rowi == 0, p1, pltpu.roll(hc, 1, 0))
    r2 = jnp.where(rowi == 0, p2, jnp.where(rowi == 1, p1, pltpu.roll(hc, 2, 0)))
    return r1, r2


def _mix_fwd(z, ws, bsb, gv, bv, cw8, tm):
    S, DIN = z.shape
    DA = gv.shape[1]
    DB = DA
    H = DA // HEAD
    o = 2 * DA
    nch = tm // HEAD

    def body(z_ref, zp_ref, ws_ref, bsb_ref, gv_ref, bv_ref, cw_ref, y_ref):
        i = pl.program_id(0)
        ri = lax.broadcasted_iota(jnp.int32, (HEAD, HEAD), 0)
        ci = lax.broadcasted_iota(jnp.int32, (HEAD, HEAD), 1)
        tril = ri >= ci
        for h in range(H):
            cs = slice(h * HEAD, (h + 1) * HEAD)
            vs = slice(DA + h * HEAD, DA + (h + 1) * HEAD)
            wb = jnp.where(tril, ws_ref[h], 0.0).astype(BF16)
            for n in range(nch):
                rs = slice(n * HEAD, (n + 1) * HEAD)
                u = _gelu(z_ref[rs, cs])
                vhat, _ = _ln_stats(_gelu(z_ref[rs, vs]))
                vn = vhat * gv_ref[:, cs] + bv_ref[:, cs]
                mixed = jnp.dot(wb, vn.astype(BF16), preferred_element_type=F32) + bsb_ref[h]
                y_ref[rs, cs] = (u * mixed).astype(BF16)
        zb = z_ref[:, o:o + DB]
        hc = z_ref[:, o + DB:o + 2 * DB] * z_ref[:, o + 2 * DB:]
        keep = (i > 0).astype(F32)
        p1 = zp_ref[7:8, o + DB:o + 2 * DB] * zp_ref[7:8, o + 2 * DB:] * keep
        p2 = zp_ref[6:7, o + DB:o + 2 * DB] * zp_ref[6:7, o + 2 * DB:] * keep
        rowi = lax.broadcasted_iota(jnp.int32, (tm, DB), 0)
        r1, r2 = _conv_shifted(hc, p1, p2, rowi)
        cv = cw_ref[0:1, :] * r2 + cw_ref[1:2, :] * r1 + cw_ref[2:3, :] * hc
        y_ref[:, DA:] = (zb * cv).astype(BF16)

    full3 = lambda i: (0, 0, 0)
    full2 = lambda i: (0, 0)
    est = 2 * (tm * DIN * 4 + tm * (DA + DB) * 2) + 10 * tm * DB * 4
    return pl.pallas_call(
        body, name="mix_fwd", grid=(S // tm,),
        in_specs=[pl.BlockSpec((tm, DIN), lambda i: (i, 0)),
                  pl.BlockSpec((8, DIN), lambda i: (jnp.maximum(i * (tm // 8) - 1, 0), 0)),
                  pl.BlockSpec(ws.shape, full3), pl.BlockSpec(bsb.shape, full3),
                  pl.BlockSpec(gv.shape, full2), pl.BlockSpec(bv.shape, full2), pl.BlockSpec(cw8.shape, full2)],
        out_specs=pl.BlockSpec((tm, DA + DB), lambda i: (i, 0)),
        out_shape=jax.ShapeDtypeStruct((S, DA + DB), BF16),
        compiler_params=pltpu.CompilerParams(dimension_semantics=("arbitrary",),
                                             vmem_limit_bytes=_vmem_limit(est)),
    )(z, z, ws, bsb, gv, bv, cw8)


def _mix_bwd(z, dy, ws, bsb, gv, bv, cw8, tm):
    S, DIN = z.shape
    DA = gv.shape[1]
    DB = DA
    H = DA // HEAD
    o = 2 * DA
    nch = tm // HEAD
    nblk = S // tm

    def body(z_ref, zp_ref, zn_ref, dy_ref, dyn_ref, ws_ref, bsb_ref, gv_ref, bv_ref, cw_ref,
             dz_ref, dws_ref, dbs_ref, dgv_ref, dbv_ref, dcw_ref, dbin_ref):
        i = pl.program_id(0)

        @pl.when(i == 0)
        def _():
            for r in (dws_ref, dbs_ref, dgv_ref, dbv_ref, dcw_ref, dbin_ref):
                r[...] = jnp.zeros_like(r)

        ri = lax.broadcasted_iota(jnp.int32, (HEAD, HEAD), 0)
        ci = lax.broadcasted_iota(jnp.int32, (HEAD, HEAD), 1)
        tril = ri >= ci
        for h in range(H):
            cs = slice(h * HEAD, (h + 1) * HEAD)
            vs = slice(DA + h * HEAD, DA + (h + 1) * HEAD)
            wb = jnp.where(tril, ws_ref[h], 0.0).astype(BF16)
            gvh = gv_ref[:, cs]
            dws_acc = jnp.zeros((HEAD, HEAD), F32)
            dbs_acc = jnp.zeros((HEAD, HEAD), F32)
            dgv_acc = jnp.zeros((1, HEAD), F32)
            dbv_acc = jnp.zeros((1, HEAD), F32)
            dbu_acc = jnp.zeros((1, HEAD), F32)
            dbvv_acc = jnp.zeros((1, HEAD), F32)
            for n in range(nch):
                rs = slice(n * HEAD, (n + 1) * HEAD)
                u, du_dz = _gelu_and_grad(z_ref[rs, cs])
                v, dv_dz = _gelu_and_grad(z_ref[rs, vs])
                vhat, rstd = _ln_stats(v)
                vnb = (vhat * gvh + bv_ref[:, cs]).astype(BF16)
                mixed = jnp.dot(wb, vnb, preferred_element_type=F32) + bsb_ref[h]
                dya = dy_ref[rs, cs]
                dzu = dya * mixed * du_dz
                dmx = dya * u
                dmxb = dmx.astype(BF16)
                dbs_acc += jnp.broadcast_to(jnp.sum(dmx, axis=1, keepdims=True), (HEAD, HEAD))
                dws_acc += lax.dot_general(dmxb, vnb, _DIMS['nt'], preferred_element_type=F32)
                dvn = lax.dot_general(wb, dmxb, _DIMS['tn'], preferred_element_type=F32)
                dgv_acc += jnp.sum(dvn * vhat, axis=0, keepdims=True)
                dbv_acc += jnp.sum(dvn, axis=0, keepdims=True)
                dxh = dvn * gvh
                m1 = jnp.mean(dxh, axis=-1, keepdims=True)
                m2 = jnp.mean(dxh * vhat, axis=-1, keepdims=True)
                dzv = rstd * (dxh - m1 - vhat * m2) * dv_dz
                dz_ref[rs, cs] = dzu.astype(BF16)
                dz_ref[rs, vs] = dzv.astype(BF16)
                dbu_acc += jnp.sum(dzu, axis=0, keepdims=True)
                dbvv_acc += jnp.sum(dzv, axis=0, keepdims=True)
            dws_ref[h] += jnp.where(tril, dws_acc, 0.0)
            dbs_ref[h] += dbs_acc
            dgv_ref[:, cs] += dgv_acc
            dbv_ref[:, cs] += dbv_acc
            dbin_ref[:, cs] += dbu_acc
            dbin_ref[:, vs] += dbvv_acc

        zb = z_ref[:, o:o + DB]
        zc = z_ref[:, o + DB:o + 2 * DB]
        zx = z_ref[:, o + 2 * DB:]
        hc = zc * zx
        keep = (i > 0).astype(F32)
        p1 = zp_ref[7:8, o + DB:o + 2 * DB] * zp_ref[7:8, o + 2 * DB:] * keep
        p2 = zp_ref[6:7, o + DB:o + 2 * DB] * zp_ref[6:7, o + 2 * DB:] * keep
        rowi = lax.broadcasted_iota(jnp.int32, (tm, DB), 0)
        r1, r2 = _conv_shifted(hc, p1, p2, rowi)
        w0, w1, w2 = cw_ref[0:1, :], cw_ref[1:2, :], cw_ref[2:3, :]
        cv = w0 * r2 + w1 * r1 + w2 * hc
        dyb = dy_ref[:, DA:]
        dzb = dyb * cv
        dcv = dyb * zb
        more = (i < nblk - 1).astype(F32)
        n0 = dyn_ref[0:1, DA:] * zn_ref[0:1, o:o + DB] * more
        n1 = dyn_ref[1:2, DA:] * zn_ref[1:2, o:o + DB] * more
        f1 = jnp.where(rowi == tm - 1, n0, pltpu.roll(dcv, tm - 1, 0))
        f2 = jnp.where(rowi == tm - 1, n1, jnp.where(rowi == tm - 2, n0, pltpu.roll(dcv, tm - 2, 0)))
        dhc = w2 * dcv + w1 * f1 + w0 * f2
        dzc = dhc * zx
        dzx = dhc * zc
        dcw_ref[0:1, :] += jnp.sum(dcv * r2, axis=0, keepdims=True)
        dcw_ref[1:2, :] += jnp.sum(dcv * r1, axis=0, keepdims=True)
        dcw_ref[2:3, :] += jnp.sum(dcv * hc, axis=0, keepdims=True)
        dz_ref[:, o:o + DB] = dzb.astype(BF16)
        dz_ref[:, o + DB:o + 2 * DB] = dzc.astype(BF16)
        dz_ref[:, o + 2 * DB:] = dzx.astype(BF16)
        dbin_ref[:, o:o + DB] += jnp.sum(dzb, axis=0, keepdims=True)
        dbin_ref[:, o + DB:o + 2 * DB] += jnp.sum(dzc, axis=0, keepdims=True)
        dbin_ref[:, o + 2 * DB:] += jnp.sum(dzx, axis=0, keepdims=True)

    full3 = lambda i: (0, 0, 0)
    full2 = lambda i: (0, 0)
    prev8 = lambda i: (jnp.maximum(i * (tm // 8) - 1, 0), 0)
    next8 = lambda i: (jnp.minimum((i + 1) * (tm // 8), S // 8 - 1), 0)
    est = 2 * (tm * DIN * 4 + tm * (DA + DB) * 4 + tm * DIN * 2) + 16 * tm * DB * 4
    return pl.pallas_call(
        body, name="mix_bwd", grid=(nblk,),
        in_specs=[pl.BlockSpec((tm, DIN), lambda i: (i, 0)),
                  pl.BlockSpec((8, DIN), prev8), pl.BlockSpec((8, DIN), next8),
                  pl.BlockSpec((tm, DA + DB), lambda i: (i, 0)), pl.BlockSpec((8, DA + DB), next8),
                  pl.BlockSpec(ws.shape, full3), pl.BlockSpec(bsb.shape, full3),
                  pl.BlockSpec(gv.shape, full2), pl.BlockSpec(bv.shape, full2), pl.BlockSpec(cw8.shape, full2)],
        out_specs=[pl.BlockSpec((tm, DIN), lambda i: (i, 0)),
                   pl.BlockSpec(ws.shape, full3), pl.BlockSpec(ws.shape, full3),
                   pl.BlockSpec(gv.shape, full2), pl.BlockSpec(gv.shape, full2),
                   pl.BlockSpec(cw8.shape, full2), pl.BlockSpec((1, DIN), full2)],
        out_shape=[jax.ShapeDtypeStruct((S, DIN), BF16),
                   jax.ShapeDtypeStruct(ws.shape, F32), jax.ShapeDtypeStruct(ws.shape, F32),
                   jax.ShapeDtypeStruct(gv.shape, F32), jax.ShapeDtypeStruct(gv.shape, F32),
                   jax.ShapeDtypeStruct(cw8.shape, F32), jax.ShapeDtypeStruct((1, DIN), F32)],
        compiler_params=pltpu.CompilerParams(dimension_semantics=("arbitrary",),
                                             vmem_limit_bytes=_vmem_limit(est)),
    )(z, z, z, dy, dy, ws, bsb, gv, bv, cw8)


def _place():
    x, y, c = lax.axis_index("x"), lax.axis_index("y"), lax.axis_index("c")
    others = [(1 - x, y), (x, 1 - y), (1 - x, 1 - y)]
    return x, y, c, 2 * x + y, others, [2 * ox + oy for ox, oy in others]


def _half(c, rows_half):
    return pl.ds(pl.multiple_of(c * rows_half, 16), rows_half)


def _rcopy(src, dst, ssem, rsem, dev):
    return pltpu.make_async_remote_copy(src_ref=src, dst_ref=dst, send_sem=ssem, recv_sem=rsem,
                                        device_id=dev, device_id_type=MESH)


class _Comm:
    def __init__(self, aliased, ins, fresh, n_sems, stages, sink):
        self.aliased, self.ins, self.fresh = list(aliased), list(ins), list(fresh)
        self.n_sems, self.stages, self.sink = n_sems, stages, sink


def _comm_operands(comms, n_in, n_out):
    arrays, shapes, scratch, aliases = [], [], [], {}
    for cm in comms:
        for a in cm.aliased:
            aliases[n_in + len(arrays)] = n_out + len(shapes)
            arrays.append(a)
            shapes.append(jax.ShapeDtypeStruct(a.shape, a.dtype))
        arrays += cm.ins
        shapes += cm.fresh
        scratch += [pltpu.SemaphoreType.DMA((cm.n_sems,))] * 2
    return arrays, shapes, scratch, aliases


def _comm_refs(comms, in_refs, out_refs, sem_refs):
    per, pi, po = [], 0, 0
    for i, cm in enumerate(comms):
        pi += len(cm.aliased)
        ins = in_refs[pi:pi + len(cm.ins)]
        pi += len(cm.ins)
        al = out_refs[po:po + len(cm.aliased)]
        po += len(cm.aliased)
        fr = out_refs[po:po + len(cm.fresh)]
        po += len(cm.fresh)
        per.append((al, ins, fr, sem_refs[2 * i], sem_refs[2 * i + 1]))
    return per


def _comm_first_step(comms, hosted, grid):
    if not comms:
        return
    at_first = functools.reduce(lambda a, b: a & b, [pl.program_id(d) == 0 for d in range(len(grid))])

    @pl.when(at_first)
    def _():
        for cm, refs in zip(comms, hosted):
            for d in cm.stages[0](*refs, False):
                d.start()


def _comm_last_step(comms, hosted, grid):
    if not comms:
        return
    at_last = functools.reduce(lambda a, b: a & b, [pl.program_id(d) == g - 1 for d, g in enumerate(grid)])

    @pl.when(at_last)
    def _():
        for cm, refs in zip(comms, hosted):
            for d in cm.stages[0](*refs, True):
                d.wait()
        for cm, refs in zip(comms, hosted):
            for stage in cm.stages[1:]:
                for d in stage(*refs, False):
                    d.start()
                for d in stage(*refs, True):
                    d.wait()


def _deliver(comms, outs):
    pos = 0
    for cm in comms:
        al = outs[pos:pos + len(cm.aliased)]
        pos += len(cm.aliased)
        fr = outs[pos:pos + len(cm.fresh)]
        pos += len(cm.fresh)
        cm.sink(list(al), list(fr))


def _c_gather(fulls, sink, ici, fwd, conv=None):
    n = len(fulls)
    n_ici = 3 * (n + (conv is not None)) if ici else 0

    def stage_ici(al, ins, fr, ssem, rsem, wait):
        x, y, c, chip, others, ochips = _place()
        ds = []
        for m in range(n + (conv is not None)):
            for k in range(3):
                i = 3 * m + k
                if m < n:
                    ah = al[m].shape[1] // 2
                    mine, lands = al[m].at[chip, _half(c, ah)], al[m].at[ochips[k], _half(c, ah)]
                else:
                    mine, lands = al[m].at[chip], al[m].at[ochips[k]]
                ds.append(_rcopy(mine, lands, ssem.at[i], rsem.at[i], (x, y, c)) if wait
                          else _rcopy(mine, mine, ssem.at[i], rsem.at[i], (*others[k], c)))
        return ds

    def stage_fwd(al, ins, fr, ssem, rsem, wait):
        x, y, c, chip, others, ochips = _place()
        ds = []
        for m in range(n):
            ah = al[m].shape[1] // 2
            for k in range(3):
                i = n_ici + 3 * m + k
                got = al[m].at[ochips[k], _half(c, ah)]
                ds.append(_rcopy(got, al[m].at[ochips[k], _half(1 - c, ah)], ssem.at[i], rsem.at[i], (x, y, c)) if wait
                          else _rcopy(got, got, ssem.at[i], rsem.at[i], (x, y, 1 - c)))
        return ds

    stages = ([stage_ici] if ici else []) + ([stage_fwd] if fwd else [])
    return _Comm(list(fulls) + ([conv] if conv is not None else []), [], [], n_ici + (3 * n if fwd else 0),
                 stages, sink)


def _c_rs_pair(parts, sink):
    def stage(al, ins, fr, ssem, rsem, wait):
        x, y, c, _, _, _ = _place()
        return [_rcopy(ins[m].at[:, _half(1 - c, ins[m].shape[1] // 2), :], fr[m], ssem.at[m], rsem.at[m],
                       (x, y, 1 - c)) for m in range(len(ins))]
    fresh = [jax.ShapeDtypeStruct((p.shape[0], p.shape[1] // 2, p.shape[2]), p.dtype) for p in parts]
    return _Comm([], parts, fresh, len(parts), [stage], sink)


def _c_rs_chips(psums, sink):
    def stage(al, ins, fr, ssem, rsem, wait):
        x, y, c, _, others, ochips = _place()
        return [_rcopy(ins[m].at[ochips[k]], fr[m].at[k], ssem.at[3 * m + k], rsem.at[3 * m + k], (*others[k], c))
                for m in range(len(ins)) for k in range(3)]
    fresh = [jax.ShapeDtypeStruct((3,) + p.shape[1:], p.dtype) for p in psums]
    return _Comm([], psums, fresh, 3 * len(psums), [stage], sink)


def _c_exchange(g2s, sink):
    def stage(al, ins, fr, ssem, rsem, wait):
        x, y, c, _, _, _ = _place()
        if wait:
            return [_rcopy(al[m].at[c], al[m].at[1 - c], ssem.at[m], rsem.at[m], (x, y, c)) for m in range(len(al))]
        return [_rcopy(al[m].at[c], al[m].at[c], ssem.at[m], rsem.at[m], (x, y, 1 - c)) for m in range(len(al))]
    return _Comm(g2s, [], [], len(g2s), [stage], sink)


def _c_allgather_small(buf, sink):
    def stage(al, ins, fr, ssem, rsem, wait):
        x, y, c, _, _, _ = _place()
        ds = []
        for r in range(1, 8):
            peer = (1 - x if r & 4 else x, 1 - y if r & 2 else y, 1 - c if r & 1 else c)
            ds.append(_rcopy(al[0].at[0], al[0].at[r], ssem.at[r - 1], rsem.at[r - 1], peer))
        return ds
    return _Comm([buf], [], [], 7, [stage], sink)


def _sum_small(buf, dev_idx):
    _, m, n = buf.shape

    def body(me_ref, b_ref, o_ref):
        me = me_ref[0]
        acc = b_ref[me]
        for d in range(1, 8):
            acc = acc + b_ref[jnp.bitwise_xor(me, d)]
        o_ref[...] = acc

    gs = pltpu.PrefetchScalarGridSpec(
        num_scalar_prefetch=1, grid=(1,),
        in_specs=[pl.BlockSpec(buf.shape, lambda i, me: (0, 0, 0))],
        out_specs=pl.BlockSpec((m, n), lambda i, me: (0, 0)))
    return pl.pallas_call(body, name="sum_small", grid_spec=gs,
                          out_shape=jax.ShapeDtypeStruct((m, n), F32))(dev_idx, buf)


def _sum_pair(name, part, sib, c_idx):
    _, A, B = part.shape
    ah = A // 2
    br = _rows(ah, B * 2, 2 * 2 ** 20)

    def body(c_ref, a_ref, b_ref, o_ref):
        o_ref[...] = (a_ref[...].astype(F32) + b_ref[...].astype(F32)).astype(BF16)

    gs = pltpu.PrefetchScalarGridSpec(
        num_scalar_prefetch=1, grid=(N_CHIPS, ah // br),
        in_specs=[pl.BlockSpec((None, None, br, B), lambda j, i, c: (j, c[0], i, 0)),
                  pl.BlockSpec((None, br, B), lambda j, i, c: (j, i, 0))],
        out_specs=pl.BlockSpec((None, br, B), lambda j, i, c: (j, i, 0)))
    return pl.pallas_call(body, name=name, grid_spec=gs,
                          out_shape=jax.ShapeDtypeStruct((N_CHIPS, ah, B), BF16))(
                              c_idx, part.reshape(N_CHIPS, 2, ah, B), sib)


def _sum_chips(name, psum, recv, place_idx):
    _, ah, B = psum.shape
    br = _rows(ah, B * 4, 2 * 2 ** 20)

    def body(c_ref, p_ref, r0_ref, r1_ref, r2_ref, o_ref):
        o_ref[...] = ((p_ref[...].astype(F32) + r0_ref[...].astype(F32)) + r1_ref[...].astype(F32)) \
            + r2_ref[...].astype(F32)

    gs = pltpu.PrefetchScalarGridSpec(
        num_scalar_prefetch=1, grid=(ah // br,),
        in_specs=[pl.BlockSpec((None, br, B), lambda i, c: (c[0], i, 0))]
        + [pl.BlockSpec((None, br, B), functools.partial(lambda i, c, k: (k, i, 0), k=k)) for k in range(3)],
        out_specs=pl.BlockSpec((None, br, B), lambda i, c: (c[1], i, 0)))
    return pl.pallas_call(body, name=name, grid_spec=gs,
                          out_shape=jax.ShapeDtypeStruct((2, ah, B), F32))(place_idx, psum, recv, recv, recv)


def _cast_into(name, a, place_idx):
    A, B = a.shape
    br = _rows(A, B * 4)

    def body(c_ref, a_ref, o_ref):
        o_ref[...] = a_ref[...].astype(BF16)

    gs = pltpu.PrefetchScalarGridSpec(
        num_scalar_prefetch=1, grid=(A // br,),
        in_specs=[pl.BlockSpec((br, B), lambda i, c: (i, 0))],
        out_specs=pl.BlockSpec((None, br, B), lambda i, c: (c[0], i, 0)))
    return pl.pallas_call(body, name=name, grid_spec=gs,
                          out_shape=jax.ShapeDtypeStruct((N_CHIPS, A, B), BF16))(place_idx, a)


def _cast_behind(x, shards, place_idx, comms):
    n_steps = 16
    n = len(shards)

    def view(a):
        total = a.shape[0] * a.shape[1]
        for cols in (a.shape[1], 1024, 512, 256, 128):
            if total % cols == 0 and (total // cols) % (16 * n_steps) == 0:
                return a.reshape(total // cols, cols)
        raise ValueError(a.shape)

    xv = view(x)
    svs = [view(s) for s in shards]
    n_in, n_out = 2 + n, 1 + n
    cm_in_arrays, cm_out_shapes, cm_scratch, aliases = _comm_operands(comms, n_in, n_out)
    n_cm_in, n_cm_out = len(cm_in_arrays), len(cm_out_shapes)
    grid = (n_steps,)

    def body(c_ref, x_ref, *rest):
        s_refs = rest[:n]
        xo_ref = rest[n + n_cm_in]
        so_refs = rest[n + n_cm_in + 1:n + n_cm_in + 1 + n]
        o_end = n + n_cm_in + 1 + n
        hosted = _comm_refs(comms, rest[n:n + n_cm_in], rest[o_end:o_end + n_cm_out], rest[o_end + n_cm_out:])
        _comm_first_step(comms, hosted, grid)
        xo_ref[...] = x_ref[...].astype(BF16)
        for s_ref, o_ref in zip(s_refs, so_refs):
            o_ref[...] = s_ref[...].astype(BF16)
        _comm_last_step(comms, hosted, grid)

    def rows(v):
        return v.shape[0] // n_steps

    gs = pltpu.PrefetchScalarGridSpec(
        num_scalar_prefetch=1, grid=grid,
        in_specs=[pl.BlockSpec((rows(v), v.shape[1]), lambda i, c: (i, 0)) for v in [xv] + svs] + [ANY] * n_cm_in,
        out_specs=[pl.BlockSpec((rows(xv), xv.shape[1]), lambda i, c: (i, 0))]
        + [pl.BlockSpec((None, rows(v), v.shape[1]), lambda i, c: (c[0], i, 0)) for v in svs] + [ANY] * n_cm_out,
        scratch_shapes=cm_scratch)
    outs = pl.pallas_call(
        body, name="cast_behind", grid_spec=gs,
        out_shape=[jax.ShapeDtypeStruct(xv.shape, BF16)]
        + [jax.ShapeDtypeStruct((N_CHIPS,) + v.shape, BF16) for v in svs] + cm_out_shapes,
        input_output_aliases=aliases,
        compiler_params=pltpu.CompilerParams(
            dimension_semantics=("arbitrary",),
            vmem_limit_bytes=_vmem_limit(2 * sum(rows(v) * v.shape[1] * 6 for v in [xv] + svs))),
    )(place_idx, xv, *svs, *cm_in_arrays)
    _deliver(comms, outs[n_out:])
    return outs[0].reshape(x.shape), [o.reshape((N_CHIPS,) + s.shape) for o, s in zip(outs[1:n_out], shards)]


def _adamw_small(gsum, gconv, offsets, ws, ms, vs):
    n = len(ws)

    def body(*refs):
        g_ref, gc_ref = refs[0], refs[1]
        w_refs = refs[2:2 + n]
        m_refs = refs[2 + n:2 + 2 * n]
        v_refs = refs[2 + 2 * n:2 + 3 * n]
        outs = refs[2 + 3 * n:]
        for i in range(n):
            rows = w_refs[i].shape[0]
            g = gc_ref[...] if offsets[i] is None else g_ref[offsets[i]:offsets[i] + rows, :]
            d, m2, v2 = _adamw_math(w_refs[i][...], g, m_refs[i][...], v_refs[i][...])
            outs[4 * i][...] = g
            outs[4 * i + 1][...] = d
            outs[4 * i + 2][...] = m2
            outs[4 * i + 3][...] = v2

    vm = pl.BlockSpec(memory_space=pltpu.VMEM)
    out_shape = []
    for w in ws:
        out_shape += [jax.ShapeDtypeStruct(w.shape, F32)] * 4
    return pl.pallas_call(body, name="adamw_small", in_specs=[vm] * (2 + 3 * n), out_specs=[vm] * (4 * n),
                          out_shape=out_shape)(gsum, gconv, *ws, *ms, *vs)


def _ident(accs, tes, res):
    return [accs[0]]


def _ffn_up(name, hb, wg4, wu4, comms=()):
    S, D = hb.shape

    def epi(accs, tes, res):
        g, u = accs
        return [g, u, g * _sigmoid(g) * u]

    return _mm(name, 'nn', [hb], [wg4, wu4], [(0, 0, 0), (0, 1, 1)], 2, epi, [BF16] * 3,
               bm=_blk(S, 512), bn=wg4.shape[2], bk=D, b_blocked=True, n_outer=True, comms=comms)


def _ffn_gate(name, hb, wg4, comms=()):
    S, D = hb.shape
    return _mm(name, 'nn', [hb], [wg4], [(0, 0, 0)], 1, _ident, [BF16],
               bm=_blk(S, 1024), bn=wg4.shape[2], bk=D, b_blocked=True, n_outer=True, comms=comms)[0]


def _ffn_upmul(name, hb, wu4, gb, comms=()):
    S, D = hb.shape

    def epi(accs, tes, res):
        g = tes[0].astype(F32)
        return [accs[0], g * _sigmoid(g) * accs[0]]

    return _mm(name, 'nn', [hb], [wu4], [(0, 0, 0)], 1, epi, [BF16, BF16],
               bm=_blk(S, 1024), bn=wu4.shape[2], bk=D, tile_extras=[gb], b_blocked=True, n_outer=True, comms=comms)


def _ffn_down(name, ab, wd, hres, comms=()):
    S, F = ab.shape
    D = wd.shape[1]
    return _mm(name, 'nn', [ab], [wd], [(0, 0, 0)], 1, lambda accs, tes, res: [ALPHA * tes[0] + 0.5 * accs[0]],
               [F32], bm=_blk(S, 1024), bn=_blk(D, 512), bk=F, tile_extras=[hres], comms=comms)[0]


def _ffn_da(name, dsub, wd, gb, ub, comms=()):
    S, D = dsub.shape
    F = wd.shape[0]

    def epi(accs, tes, res):
        da = accs[0]
        g = tes[0].astype(F32)
        u = tes[1].astype(F32)
        s = _sigmoid(g)
        silu = g * s
        return [(da * u) * (s + silu * (1.0 - s)), da * silu]

    return _mm(name, 'nt', [dsub], [wd], [(0, 0, 0)], 1, epi, [BF16, BF16],
               bm=_blk(S, 1024), bn=_blk(F, 512), bk=D, tile_extras=[gb, ub], comms=comms)


def _dw_rows(name, ab, dsub, cs, comms=()):
    S, D = dsub.shape
    return _mm(name, 'tn', [ab], [dsub], [(0, 0, 0)], 1, _ident, [BF16],
               bm=cs, bn=_blk(D, 1024), bk=_blk(S, 2048), comms=comms)[0].reshape(N_CHIPS, cs, D)


def _dw_cols(name, hb, dxb, cs, comms=()):
    S, D = hb.shape
    return _mm(name, 'tn', [hb], [dxb], [(0, 0, 0)], 1, _ident, [BF16],
               bm=_blk(D, 1024), bn=cs, bk=_blk(S, 2048), out_blocked=True, comms=comms)[0]


def _ffn_dh(name, dgb, dub, wg4, wu4, dres, comms=()):
    S = dgb.shape[0]
    D = wg4.shape[1]
    return _mm(name, 'nt', [dgb, dub], [wg4, wu4], [(0, 0, 0), (1, 1, 0)], 1,
               lambda accs, tes, res: [accs[0] + tes[0]], [F32],
               bm=_blk(S, 1024), bn=_blk(D, 1024), bk=wg4.shape[2], tile_extras=[dres], b_blocked=True,
               comms=comms)[0]


def _step(p):
    x = p['x'][0]
    target = p['loss_target'][0]
    S, D = x.shape
    cx, cy, cc = lax.axis_index("x"), lax.axis_index("y"), lax.axis_index("c")
    chip = 2 * cx + cy
    c_idx = jnp.reshape(cc, (1,)).astype(jnp.int32)
    place_idx = jnp.stack([chip, cc]).astype(jnp.int32)

    w_s = p['w_s'][0]
    H = w_s.shape[0]
    DA = H * HEAD
    bsb = jnp.broadcast_to(p['b_s'][0][:, :, None], (H, HEAD, HEAD))
    conv_shard = p['conv_w'][0]
    conv8 = jnp.zeros((8, conv_shard.shape[1]), F32).at[:CONV_TAPS].set(conv_shard)

    W = {'ffa_gate': _cast_into("cast_ffa_gate", p['ffa_gate'][0], place_idx)}
    conv = {'full': lax.dynamic_update_slice(jnp.zeros((N_CHIPS,) + conv8.shape, F32), conv8[None], (chip, 0, 0))}

    def gathered(names, with_conv=False):
        def sink(al, fr):
            W.update(zip(names, al))
            if with_conv:
                conv['full'] = al[len(names)]
        return sink

    def c_gather(names, ici, fwd, with_conv=False):
        return _c_gather([W[n] for n in names], gathered(names, with_conv), ici, fwd,
                         conv['full'] if with_conv else None)

    rest = [n for n in BIG if n != 'ffa_gate']
    xb, casted = _cast_behind(x, [p[n][0] for n in rest], place_idx, comms=[c_gather(['ffa_gate'], True, True)])
    W.update(zip(rest, casted))
    cs_ff = W['ffa_gate'].shape[2]
    F = N_CHIPS * cs_ff

    ga = _ffn_gate("ffa_gate", xb, W['ffa_gate'],
                   comms=[c_gather(['ffa_up'], True, True), c_gather(['ffa_down'], True, False)])
    ua, aa = _ffn_upmul("ffa_up", xb, W['ffa_up'], ga,
                        comms=[c_gather(['ffa_down'], False, True),
                               c_gather(['w_in', 'ffc_gate'], True, False, with_conv=True)])
    cw8 = jnp.transpose(conv['full'], (1, 0, 2)).reshape(8, DA)
    wd_a = W['ffa_down'].reshape(F, D)
    r1 = _ffn_down("ffa_down", aa, wd_a, x,
                   comms=[c_gather(['w_in', 'ffc_gate'], False, True), c_gather(['w_out', 'ffc_down'], True, False)])
    h1, h1b = _ln_fwd("ln_a", r1, p['ln_a_g'], p['ln_a_b'])
    w_in4 = W['w_in']
    cs_in = w_in4.shape[2]
    z = _mm("w_in", 'nn', [h1b], [w_in4], [(0, 0, 0)], 1, lambda accs, tes, res: [accs[0] + res[0]], [F32],
            bm=_blk(S, 1024), bn=cs_in, bk=D, row_extras=[p['b_in']], b_blocked=True, n_outer=True,
            comms=[c_gather(['w_out', 'ffc_down'], False, True), c_gather(['ffc_up'], True, False)])[0]
    w_out = W['w_out'].reshape(DA * 2, D)
    tm = _blk(S, 256)
    yb = _mix_fwd(z, w_s, bsb, p['ln_v_g'], p['ln_v_b'], cw8, tm)
    r2 = _mm("w_out", 'nn', [yb], [w_out], [(0, 0, 0)], 1,
             lambda accs, tes, res: [accs[0] + res[0] + ALPHA * tes[0]], [F32],
             bm=_blk(S, 1024), bn=_blk(D, 1024), bk=2 * DA, tile_extras=[h1], row_extras=[p['b_out']],
             comms=[c_gather(['ffc_up'], False, True)])[0]
    h2, h2b = _ln_fwd("ln_m", r2, p['ln_m_g'], p['ln_m_b'])
    gc, uc, ac = _ffn_up("ffc_up", h2b, W['ffc_gate'], W['ffc_up'])
    wd_c = W['ffc_down'].reshape(F, D)
    r3 = _ffn_down("ffc_down", ac, wd_c, h2)

    pg, sg, sib, psum, recv, g2 = {}, {}, {}, {}, {}, {}

    def c_pair(names):
        return _c_rs_pair([pg[n] for n in names], lambda al, fr: sib.update(zip(names, fr)))

    def c_chips(names):
        return _c_rs_chips([psum[n] for n in names], lambda al, fr: recv.update(zip(names, fr)))

    def c_swap(names):
        return _c_exchange([g2[n] for n in names], lambda al, fr: g2.update(zip(names, al)))

    def sum_pair(*names):
        for n in names:
            psum[n] = _sum_pair(f"sum_pair_{n}", pg[n], sib[n], c_idx)

    def sum_chips(*names):
        for n in names:
            g2[n] = _sum_chips(f"sum_chips_{n}", psum[n], recv[n], place_idx)

    dres3, dsub3, sg['ln_c_g'], sg['ln_c_b'], lsum = _ln_loss_bwd(
        "ln_c_loss_bwd", r3, target, p['ln_c_g'], p['ln_c_b'], 0.5)

    dgc, duc = _ffn_da("ffc_da", dsub3, wd_c, gc, uc)
    pg['ffc_down'] = _dw_rows("ffc_dwd", ac, dsub3, cs_ff)
    pg['ffc_gate'] = _dw_cols("ffc_dwg", h2b, dgc, cs_ff, comms=[c_pair(['ffc_down'])])
    sum_pair('ffc_down')
    pg['ffc_up'] = _dw_cols("ffc_dwu", h2b, duc, cs_ff, comms=[c_pair(['ffc_gate']), c_chips(['ffc_down'])])
    sum_pair('ffc_gate')
    dh2 = _ffn_dh("ffc_dh", dgc, duc, W['ffc_gate'], W['ffc_up'], dres3,
                  comms=[c_pair(['ffc_up']), c_chips(['ffc_gate'])])
    sum_pair('ffc_up')
    sum_chips('ffc_down', 'ffc_gate')

    dres2, dmix, sg['ln_m_g'], sg['ln_m_b'], sg['b_out'] = _ln_bwd("ln_m_bwd", r2, dh2, p['ln_m_g'], 1.0)
    pg['w_out'] = _mm("dw_out", 'tn', [yb], [dmix], [(0, 0, 0)], 1, _ident, [BF16],
                      bm=_blk(2 * DA, 1024), bn=_blk(D, 1024), bk=_blk(S, 2048),
                      comms=[c_swap(['ffc_down', 'ffc_gate'])])[0].reshape(N_CHIPS, 2 * DA // 4, D)
    dy = _mm("dy", 'nt', [dmix], [w_out], [(0, 0, 0)], 1, _ident, [F32],
             bm=_blk(S, 1024), bn=_blk(2 * DA, 1024), bk=D, comms=[c_pair(['w_out'])])[0]
    sum_pair('w_out')
    dz, dws, dbs, sg['ln_v_g'], sg['ln_v_b'], dcw, sg['b_in'] = _mix_bwd(
        z, dy, w_s, bsb, p['ln_v_g'], p['ln_v_b'], cw8, tm)
    sg['w_s'] = dws
    sg['b_s'] = dbs[:, :, 0]
    pg['w_in'] = _dw_cols("dw_in", h1b, dz, cs_in, comms=[c_chips(['ffc_up']), c_chips(['w_out'])])
    sum_chips('ffc_up', 'w_out')
    dh1 = _mm("dh1", 'nt', [dz], [w_in4], [(0, 0, 0)], 1, lambda accs, tes, res: [accs[0] + tes[0]], [F32],
              bm=_blk(S, 1024), bn=_blk(D, 1024), bk=cs_in, tile_extras=[dres2], b_blocked=True,
              comms=[c_pair(['w_in']), c_swap(['ffc_up', 'w_out'])])[0]
    sum_pair('w_in')

    dres1, dsub1, sg['ln_a_g'], sg['ln_a_b'], _ = _ln_bwd("ln_a_bwd", r1, dh1, p['ln_a_g'], 0.5)
    def as_rows(a):
        a = a.reshape(-1, 128)
        pad = (-a.shape[0]) % 8
        return jnp.pad(a, ((0, pad), (0, 0))) if pad else a

    pieces, offsets, off = [], {}, 0
    for n in SMALL + ['conv_w']:
        piece = as_rows(dcw[:CONV_TAPS] if n == 'conv_w' else sg[n])
        offsets[n] = off
        off += piece.shape[0]
        pieces.append(piece)
    packed = jnp.concatenate(pieces, axis=0)
    small = {'buf': jnp.zeros((8,) + packed.shape, F32).at[0].set(packed)}
    dga, dua = _ffn_da("ffa_da", dsub1, wd_a, ga, ua,
                       comms=[c_chips(['w_in']), _c_allgather_small(small['buf'], lambda al, fr: small.update(buf=al[0]))])
    sum_chips('w_in')
    gsum = _sum_small(small['buf'], jnp.reshape(4 * cx + 2 * cy + cc, (1,)).astype(jnp.int32))
    pg['ffa_down'] = _dw_rows("ffa_dwd", aa, dsub1, cs_ff, comms=[c_swap(['w_in'])])
    pg['ffa_gate'] = _dw_cols("ffa_dwg", xb, dga, cs_ff, comms=[c_pair(['ffa_down'])])
    sum_pair('ffa_down')
    pg['ffa_up'] = _dw_cols("ffa_dwu", xb, dua, cs_ff, comms=[c_pair(['ffa_gate']), c_chips(['ffa_down'])])
    sum_pair('ffa_gate')
    out = {}

    def adam(n, comms=()):
        g, d, m2, v2 = _adamw(f"adamw_{n}", p[n][0], g2[n].reshape(p[n][0].shape), p['m_' + n][0], p['v_' + n][0],
                              comms=comms)
        shp = p[n].shape
        out[n] = (g.reshape(shp), d.reshape(shp), m2.reshape(shp), v2.reshape(shp))

    adam('w_in', comms=[c_pair(['ffa_up'])])
    sum_pair('ffa_up')
    sum_chips('ffa_down')
    dx = _ffn_dh("ffa_dh", dga, dua, W['ffa_gate'], W['ffa_up'], dres1,
                 comms=[c_chips(['ffa_gate']), c_chips(['ffa_up']), c_swap(['ffa_down'])])
    sum_chips('ffa_gate', 'ffa_up')
    adam('ffa_down', comms=[c_swap(['ffa_gate', 'ffa_up'])])
    for n in BIG:
        if n not in out:
            adam(n)

    loss = lax.psum(0.5 * jnp.sum(lsum) / D, ("x", "y", "c"))
    n_conv = CONV_TAPS * DA // 128
    conv_sum = gsum[offsets['conv_w']:offsets['conv_w'] + n_conv].reshape(CONV_TAPS, DA)
    cw = conv_shard.shape[1]
    gconv = lax.dynamic_slice_in_dim(conv_sum, (2 * cx + cy) * cw, cw, axis=1).reshape(-1, 128)
    names = SMALL + ['conv_w']
    rows128 = lambda a: a.reshape(-1, 128)
    res = _adamw_small(gsum, gconv, [offsets[n] if n != 'conv_w' else None for n in names],
                       [rows128(p[n]) for n in names], [rows128(p['m_' + n]) for n in names],
                       [rows128(p['v_' + n]) for n in names])
    for i, n in enumerate(names):
        out[n] = tuple(r.reshape(p[n].shape) for r in res[4 * i:4 * i + 4])

    return (loss, dx.reshape(p['x'].shape), *[out[n][0] for n in WEIGHTS], *[out[n][1] for n in WEIGHTS],
            *[out[n][2] for n in WEIGHTS], *[out[n][3] for n in WEIGHTS])


def kernel(x, ffa_gate, ffa_up, ffa_down, ln_a_g, ln_a_b, w_in, b_in, w_s, b_s, ln_v_g, ln_v_b, conv_w, w_out, b_out, ln_m_g, ln_m_b, ffc_gate, ffc_up, ffc_down, ln_c_g, ln_c_b, loss_target, m_ffa_gate, m_ffa_up, m_ffa_down, m_ln_a_g, m_ln_a_b, m_w_in, m_b_in, m_w_s, m_b_s, m_ln_v_g, m_ln_v_b, m_conv_w, m_w_out, m_b_out, m_ln_m_g, m_ln_m_b, m_ffc_gate, m_ffc_up, m_ffc_down, m_ln_c_g, m_ln_c_b, v_ffa_gate, v_ffa_up, v_ffa_down, v_ln_a_g, v_ln_a_b, v_w_in, v_b_in, v_w_s, v_b_s, v_ln_v_g, v_ln_v_b, v_conv_w, v_w_out, v_b_out, v_ln_m_g, v_ln_m_b, v_ffc_gate, v_ffc_up, v_ffc_down, v_ln_c_g, v_ln_c_b):
    return _step(dict(locals()))
```

```python
import functools
import math

import jax
import jax.numpy as jnp
from jax import lax
from jax.experimental import pallas as pl
from jax.experimental.pallas import tpu as pltpu

F32 = jnp.float32
BF16 = jnp.bfloat16
MESH = pl.DeviceIdType.MESH
ANY = pl.BlockSpec(memory_space=pl.ANY)

HEAD = 128
CONV_TAPS = 3
LN_EPS = 1e-5
ALPHA = float(2.0 ** 0.25)
ADAM_LR, ADAM_B1, ADAM_B2, ADAM_EPS, ADAM_WD, ADAM_STEP = 0.001, 0.9, 0.999, 1e-08, 0.01, 10

V7X_VMEM_BYTES = 64 * 2 ** 20
N_CHIPS = 4

BIG = ['ffa_gate', 'ffa_up', 'ffa_down', 'w_in', 'w_out', 'ffc_gate', 'ffc_up', 'ffc_down']
WEIGHTS = ['ffa_gate', 'ffa_up', 'ffa_down', 'ln_a_g', 'ln_a_b', 'w_in', 'b_in', 'w_s', 'b_s', 'ln_v_g', 'ln_v_b',
           'conv_w', 'w_out', 'b_out', 'ln_m_g', 'ln_m_b', 'ffc_gate', 'ffc_up', 'ffc_down', 'ln_c_g', 'ln_c_b']
SMALL = [n for n in WEIGHTS if n not in BIG and n != 'conv_w']


def _vmem_limit(estimate_bytes):
    return int(min(max(estimate_bytes * 1.3, 32 * 2 ** 20), V7X_VMEM_BYTES - 6 * 2 ** 20))


def _blk(dim, pref, mult=128):
    if dim <= pref:
        return dim
    best = None
    for d in range(mult, pref + 1, mult):
        if dim % d == 0:
            best = d
    assert best is not None, (dim, pref)
    return best


def _rows(n_rows, row_bytes, target=4 * 2 ** 20):
    return _blk(n_rows, max(16, target // row_bytes), 16)


def _sigmoid(x):
    return 0.5 * jnp.tanh(0.5 * x) + 0.5


def _gelu(x):
    c = math.sqrt(2.0 / math.pi)
    return 0.5 * x * (1.0 + jnp.tanh(c * (x + 0.044715 * (x * x * x))))


def _gelu_and_grad(x):
    c = math.sqrt(2.0 / math.pi)
    t = jnp.tanh(c * (x + 0.044715 * (x * x * x)))
    g = 0.5 * x * (1.0 + t)
    dg = 0.5 * (1.0 + t) + 0.5 * x * (1.0 - t * t) * (c * (1.0 + 3.0 * 0.044715 * (x * x)))
    return g, dg


def _ln_stats(r):
    mu = jnp.mean(r, axis=-1, keepdims=True)
    d = r - mu
    var = jnp.mean(d * d, axis=-1, keepdims=True)
    rstd = lax.rsqrt(var + LN_EPS)
    return d * rstd, rstd


_DIMS = {'nn': (((1,), (0,)), ((), ())), 'nt': (((1,), (1,)), ((), ())), 'tn': (((0,), (0,)), ((), ()))}


def _mm(name, mode, a_list, b_list, pairs, n_acc, epilogue, out_dtypes, bm, bn, bk,
        tile_extras=(), row_extras=(), b_blocked=False, out_blocked=False, n_outer=False, comms=()):
    a0, b0 = a_list[0], b_list[0]
    if mode == 'nn':
        M, K = a0.shape
        N = b0.shape[0] * b0.shape[2] if b_blocked else b0.shape[1]
    elif mode == 'nt':
        M, K = a0.shape
        N = b0.shape[1] if b_blocked else b0.shape[0]
    else:
        K, M = a0.shape
        N = b0.shape[1]
        assert not b_blocked
    assert M % bm == 0 and N % bn == 0 and K % bk == 0, (name, M, N, K, bm, bn, bk)
    gm, gn, gk = M // bm, N // bn, K // bk
    if b_blocked:
        assert (bn if mode == 'nn' else bk) == b0.shape[2], name
    dims = _DIMS[mode]

    def spec(shape, f):
        if n_outer:
            return pl.BlockSpec(shape, lambda g0, g1, g2: f(g1, g0, g2))
        return pl.BlockSpec(shape, f)

    if mode == 'tn':
        a_spec = spec((bk, bm), lambda i, j, k: (k, i))
    else:
        a_spec = spec((bm, bk), lambda i, j, k: (i, k))
    if mode == 'nn':
        b_spec = (spec((None, bk, bn), lambda i, j, k: (j, k, 0)) if b_blocked
                  else spec((bk, bn), lambda i, j, k: (k, j)))
    elif mode == 'nt':
        b_spec = (spec((None, bn, bk), lambda i, j, k: (k, j, 0)) if b_blocked
                  else spec((bn, bk), lambda i, j, k: (j, k)))
    else:
        b_spec = spec((bk, bn), lambda i, j, k: (k, j))
    te_spec = spec((bm, bn), lambda i, j, k: (i, j))
    re_spec = spec((1, bn), lambda i, j, k: (0, j))
    if out_blocked:
        o_spec = spec((None, bm, bn), lambda i, j, k: (j, i, 0))
        o_shape = (gn, M, bn)
    else:
        o_spec = te_spec
        o_shape = (M, N)

    na, nb, nte, nre, no = len(a_list), len(b_list), len(tile_extras), len(row_extras), len(out_dtypes)

    n_scratch_acc = n_acc if gk > 1 else 0
    grid = (gn, gm, gk) if n_outer else (gm, gn, gk)
    cm_in_arrays, cm_out_shapes, cm_scratch, aliases = _comm_operands(comms, na + nb + nte + nre, no)
    n_cm_in, n_cm_out = len(cm_in_arrays), len(cm_out_shapes)

    def body(*refs):
        a_refs = refs[:na]
        b_refs = refs[na:na + nb]
        te_refs = refs[na + nb:na + nb + nte]
        re_refs = refs[na + nb + nte:na + nb + nte + nre]
        n_in = na + nb + nte + nre
        o_refs = refs[n_in + n_cm_in:n_in + n_cm_in + no]
        acc_refs = refs[n_in + n_cm_in + no + n_cm_out:n_in + n_cm_in + no + n_cm_out + n_scratch_acc]
        hosted = _comm_refs(comms, refs[n_in:n_in + n_cm_in], refs[n_in + n_cm_in + no:n_in + n_cm_in + no + n_cm_out],
                            refs[n_in + n_cm_in + no + n_cm_out + n_scratch_acc:])
        _comm_first_step(comms, hosted, grid)

        def finish(accs):
            outs = epilogue(accs, [r[...] for r in te_refs], [r[...] for r in re_refs])
            for o_ref, o in zip(o_refs, outs):
                o_ref[...] = o.astype(o_ref.dtype)

        def products():
            parts = [None] * n_acc
            for ai, bi, ci in pairs:
                d = lax.dot_general(a_refs[ai][...], b_refs[bi][...], dims, preferred_element_type=F32)
                parts[ci] = d if parts[ci] is None else parts[ci] + d
            return parts

        if gk == 1:
            finish(products())
        else:
            kk = pl.program_id(2)

            @pl.when(kk == 0)
            def _():
                for r, part in zip(acc_refs, products()):
                    r[...] = part

            if gk > 2:
                @pl.when((kk > 0) & (kk < gk - 1))
                def _():
                    for r, part in zip(acc_refs, products()):
                        r[...] += part

            @pl.when(kk == gk - 1)
            def _():
                finish([r[...] + part for r, part in zip(acc_refs, products())])

        _comm_last_step(comms, hosted, grid)

    isz = lambda a: jnp.dtype(a.dtype).itemsize
    est = 2 * (sum(bm * bk * isz(a) for a in a_list) + sum(bk * bn * isz(b) for b in b_list)
               + sum(bm * bn * isz(t) for t in tile_extras)
               + sum(bm * bn * jnp.dtype(d).itemsize for d in out_dtypes))
    est += (2 * n_acc + 2) * bm * bn * 4
    outs = pl.pallas_call(
        body, name=name, grid=grid,
        in_specs=[a_spec] * na + [b_spec] * nb + [te_spec] * nte + [re_spec] * nre + [ANY] * len(cm_in_arrays),
        out_specs=[o_spec] * no + [ANY] * len(cm_out_shapes),
        out_shape=[jax.ShapeDtypeStruct(o_shape, d) for d in out_dtypes] + cm_out_shapes,
        scratch_shapes=[pltpu.VMEM((bm, bn), F32)] * n_scratch_acc + cm_scratch,
        input_output_aliases=aliases,
        compiler_params=pltpu.CompilerParams(dimension_semantics=("parallel", "parallel", "arbitrary"),
                                             vmem_limit_bytes=_vmem_limit(est)),
    )(*a_list, *b_list, *tile_extras, *row_extras, *cm_in_arrays)
    _deliver(comms, outs[no:])
    return outs[:no]


def _rowwise(name, fn, br, row_ins, vec_ins, row_outs, acc_outs=()):
    S = row_ins[0].shape[0]
    assert S % br == 0, (name, S, br)
    nr, nv, no, nacc = len(row_ins), len(vec_ins), len(row_outs), len(acc_outs)

    def body(*refs):
        rin = refs[:nr]
        vin = refs[nr:nr + nv]
        rout = refs[nr + nv:nr + nv + no]
        aout = refs[nr + nv + no:]
        if nacc:
            @pl.when(pl.program_id(0) == 0)
            def _():
                for a in aout:
                    a[...] = jnp.zeros_like(a)
        fn(rin, vin, rout, aout)

    def full_spec(shape):
        nd = len(shape)
        return pl.BlockSpec(shape, lambda i: (0,) * nd)

    est = 2 * (sum(br * a.shape[1] * jnp.dtype(a.dtype).itemsize for a in row_ins)
               + sum(br * c * jnp.dtype(d).itemsize for c, d in row_outs))
    est += 6 * br * max(a.shape[1] for a in row_ins) * 4
    return pl.pallas_call(
        body, name=name, grid=(S // br,),
        in_specs=[pl.BlockSpec((br, a.shape[1]), lambda i: (i, 0)) for a in row_ins]
        + [full_spec(v.shape) for v in vec_ins],
        out_specs=[pl.BlockSpec((br, c), lambda i: (i, 0)) for c, _ in row_outs]
        + [full_spec(s) for s, _ in acc_outs],
        out_shape=[jax.ShapeDtypeStruct((S, c), d) for c, d in row_outs]
        + [jax.ShapeDtypeStruct(s, d) for s, d in acc_outs],
        compiler_params=pltpu.CompilerParams(dimension_semantics=("arbitrary",),
                                             vmem_limit_bytes=_vmem_limit(est)),
    )(*row_ins, *vec_ins)


def _ln_fwd(name, r, g, b):
    def fn(rin, vin, rout, aout):
        xhat, _ = _ln_stats(rin[0][...])
        h = xhat * vin[0][...] + vin[1][...]
        rout[0][...] = h
        rout[1][...] = h.astype(BF16)
    D = r.shape[1]
    return _rowwise(name, fn, _rows(r.shape[0], D * 4, 2 * 2 ** 20), [r], [g, b], [(D, F32), (D, BF16)])


def _ln_loss_bwd(name, r, target, g, b, sub_scale):
    D = r.shape[1]

    def fn(rin, vin, rout, aout):
        xhat, rstd = _ln_stats(rin[0][...])
        gain = vin[0][...]
        err = xhat * gain + vin[1][...] - rin[1][...]
        dhv = err * (1.0 / D)
        dxh = dhv * gain
        m1 = jnp.mean(dxh, axis=-1, keepdims=True)
        m2 = jnp.mean(dxh * xhat, axis=-1, keepdims=True)
        dr = rstd * (dxh - m1 - xhat * m2)
        rout[0][...] = ALPHA * dr
        rout[1][...] = (sub_scale * dr).astype(BF16)
        aout[0][...] += jnp.sum(dhv * xhat, axis=0, keepdims=True)
        aout[1][...] += jnp.sum(dhv, axis=0, keepdims=True)
        aout[2][...] += jnp.sum(err * err, axis=0, keepdims=True)
    return _rowwise(name, fn, _rows(r.shape[0], D * 4, 2 * 2 ** 20), [r, target], [g, b],
                    [(D, F32), (D, BF16)], [((1, D), F32)] * 3)


def _ln_bwd(name, r, dh, g, sub_scale):
    D = r.shape[1]

    def fn(rin, vin, rout, aout):
        xhat, rstd = _ln_stats(rin[0][...])
        dhv = rin[1][...]
        dxh = dhv * vin[0][...]
        m1 = jnp.mean(dxh, axis=-1, keepdims=True)
        m2 = jnp.mean(dxh * xhat, axis=-1, keepdims=True)
        dr = rstd * (dxh - m1 - xhat * m2)
        rout[0][...] = ALPHA * dr
        rout[1][...] = (sub_scale * dr).astype(BF16)
        aout[0][...] += jnp.sum(dhv * xhat, axis=0, keepdims=True)
        aout[1][...] += jnp.sum(dhv, axis=0, keepdims=True)
        aout[2][...] += jnp.sum(dr, axis=0, keepdims=True)
    return _rowwise(name, fn, _rows(r.shape[0], D * 4, 2 * 2 ** 20), [r, dh], [g],
                    [(D, F32), (D, BF16)], [((1, D), F32)] * 3)


def _adamw_math(w, g, m, v):
    m2 = ADAM_B1 * m + (1.0 - ADAM_B1) * g
    v2 = ADAM_B2 * v + (1.0 - ADAM_B2) * (g * g)
    m_hat = m2 / (1.0 - ADAM_B1 ** ADAM_STEP)
    v_hat = v2 / (1.0 - ADAM_B2 ** ADAM_STEP)
    delta = -ADAM_LR * (m_hat / (jnp.sqrt(v_hat) + ADAM_EPS) + ADAM_WD * w)
    return delta, m2, v2


def _adamw(name, w, g, m, v):
    def fn(rin, vin, rout, aout):
        gv = rin[1][...]
        d, m2, v2 = _adamw_math(rin[0][...], gv, rin[2][...], rin[3][...])
        rout[0][...] = gv
        rout[1][...] = d
        rout[2][...] = m2
        rout[3][...] = v2
    C = w.shape[1]
    return _rowwise(name, fn, _rows(w.shape[0], C * 4, 2 * 2 ** 20), [w, g, m, v], [], [(C, F32)] * 4)


def _conv_shifted(hc, p1, p2, rowi):
    r1 = jnp.where(rowi == 0, p1, pltpu.roll(hc, 1, 0))
    r2 = jnp.where(rowi == 0, p2, jnp.where(rowi == 1, p1, pltpu.roll(hc, 2, 0)))
    return r1, r2


def _mix_fwd(z, ws, bsb, gv, bv, cw8, tm):
    S, DIN = z.shape
    DA = gv.shape[1]
    DB = DA
    H = DA // HEAD
    o = 2 * DA
    nch = tm // HEAD

    def body(z_ref, zp_ref, ws_ref, bsb_ref, gv_ref, bv_ref, cw_ref, y_ref):
        i = pl.program_id(0)
        ri = lax.broadcasted_iota(jnp.int32, (HEAD, HEAD), 0)
        ci = lax.broadcasted_iota(jnp.int32, (HEAD, HEAD), 1)
        tril = ri >= ci
        for h in range(H):
            cs = slice(h * HEAD, (h + 1) * HEAD)
            vs = slice(DA + h * HEAD, DA + (h + 1) * HEAD)
            wb = jnp.where(tril, ws_ref[h], 0.0).astype(BF16)
            for n in range(nch):
                rs = slice(n * HEAD, (n + 1) * HEAD)
                u = _gelu(z_ref[rs, cs])
                vhat, _ = _ln_stats(_gelu(z_ref[rs, vs]))
                vn = vhat * gv_ref[:, cs] + bv_ref[:, cs]
                mixed = jnp.dot(wb, vn.astype(BF16), preferred_element_type=F32) + bsb_ref[h]
                y_ref[rs, cs] = (u * mixed).astype(BF16)
        zb = z_ref[:, o:o + DB]
        hc = z_ref[:, o + DB:o + 2 * DB] * z_ref[:, o + 2 * DB:]
        keep = (i > 0).astype(F32)
        p1 = zp_ref[7:8, o + DB:o + 2 * DB] * zp_ref[7:8, o + 2 * DB:] * keep
        p2 = zp_ref[6:7, o + DB:o + 2 * DB] * zp_ref[6:7, o + 2 * DB:] * keep
        rowi = lax.broadcasted_iota(jnp.int32, (tm, DB), 0)
        r1, r2 = _conv_shifted(hc, p1, p2, rowi)
        cv = cw_ref[0:1, :] * r2 + cw_ref[1:2, :] * r1 + cw_ref[2:3, :] * hc
        y_ref[:, DA:] = (zb * cv).astype(BF16)

    full3 = lambda i: (0, 0, 0)
    full2 = lambda i: (0, 0)
    est = 2 * (tm * DIN * 4 + tm * (DA + DB) * 2) + 10 * tm * DB * 4
    return pl.pallas_call(
        body, name="mix_fwd", grid=(S // tm,),
        in_specs=[pl.BlockSpec((tm, DIN), lambda i: (i, 0)),
                  pl.BlockSpec((8, DIN), lambda i: (jnp.maximum(i * (tm // 8) - 1, 0), 0)),
                  pl.BlockSpec(ws.shape, full3), pl.BlockSpec(bsb.shape, full3),
                  pl.BlockSpec(gv.shape, full2), pl.BlockSpec(bv.shape, full2), pl.BlockSpec(cw8.shape, full2)],
        out_specs=pl.BlockSpec((tm, DA + DB), lambda i: (i, 0)),
        out_shape=jax.ShapeDtypeStruct((S, DA + DB), BF16),
        compiler_params=pltpu.CompilerParams(dimension_semantics=("arbitrary",),
                                             vmem_limit_bytes=_vmem_limit(est)),
    )(z, z, ws, bsb, gv, bv, cw8)


def _mix_bwd(z, dy, ws, bsb, gv, bv, cw8, tm):
    S, DIN = z.shape
    DA = gv.shape[1]
    DB = DA
    H = DA // HEAD
    o = 2 * DA
    nch = tm // HEAD
    nblk = S // tm

    def body(z_ref, zp_ref, zn_ref, dy_ref, dyn_ref, ws_ref, bsb_ref, gv_ref, bv_ref, cw_ref,
             dz_ref, dws_ref, dbs_ref, dgv_ref, dbv_ref, dcw_ref, dbin_ref):
        i = pl.program_id(0)

        @pl.when(i == 0)
        def _():
            for r in (dws_ref, dbs_ref, dgv_ref, dbv_ref, dcw_ref, dbin_ref):
                r[...] = jnp.zeros_like(r)

        ri = lax.broadcasted_iota(jnp.int32, (HEAD, HEAD), 0)
        ci = lax.broadcasted_iota(jnp.int32, (HEAD, HEAD), 1)
        tril = ri >= ci
        for h in range(H):
            cs = slice(h * HEAD, (h + 1) * HEAD)
            vs = slice(DA + h * HEAD, DA + (h + 1) * HEAD)
            wb = jnp.where(tril, ws_ref[h], 0.0).astype(BF16)
            gvh = gv_ref[:, cs]
            dws_acc = jnp.zeros((HEAD, HEAD), F32)
            dbs_acc = jnp.zeros((HEAD, HEAD), F32)
            dgv_acc = jnp.zeros((1, HEAD), F32)
            dbv_acc = jnp.zeros((1, HEAD), F32)
            dbu_acc = jnp.zeros((1, HEAD), F32)
            dbvv_acc = jnp.zeros((1, HEAD), F32)
            for n in range(nch):
                rs = slice(n * HEAD, (n + 1) * HEAD)
                u, du_dz = _gelu_and_grad(z_ref[rs, cs])
                v, dv_dz = _gelu_and_grad(z_ref[rs, vs])
                vhat, rstd = _ln_stats(v)
                vnb = (vhat * gvh + bv_ref[:, cs]).astype(BF16)
                mixed = jnp.dot(wb, vnb, preferred_element_type=F32) + bsb_ref[h]
                dya = dy_ref[rs, cs]
                dzu = dya * mixed * du_dz
                dmx = dya * u
                dmxb = dmx.astype(BF16)
                dbs_acc += jnp.broadcast_to(jnp.sum(dmx, axis=1, keepdims=True), (HEAD, HEAD))
                dws_acc += lax.dot_general(dmxb, vnb, _DIMS['nt'], preferred_element_type=F32)
                dvn = lax.dot_general(wb, dmxb, _DIMS['tn'], preferred_element_type=F32)
                dgv_acc += jnp.sum(dvn * vhat, axis=0, keepdims=True)
                dbv_acc += jnp.sum(dvn, axis=0, keepdims=True)
                dxh = dvn * gvh
                m1 = jnp.mean(dxh, axis=-1, keepdims=True)
                m2 = jnp.mean(dxh * vhat, axis=-1, keepdims=True)
                dzv = rstd * (dxh - m1 - vhat * m2) * dv_dz
                dz_ref[rs, cs] = dzu.astype(BF16)
                dz_ref[rs, vs] = dzv.astype(BF16)
                dbu_acc += jnp.sum(dzu, axis=0, keepdims=True)
                dbvv_acc += jnp.sum(dzv, axis=0, keepdims=True)
            dws_ref[h] += jnp.where(tril, dws_acc, 0.0)
            dbs_ref[h] += dbs_acc
            dgv_ref[:, cs] += dgv_acc
            dbv_ref[:, cs] += dbv_acc
            dbin_ref[:, cs] += dbu_acc
            dbin_ref[:, vs] += dbvv_acc

        zb = z_ref[:, o:o + DB]
        zc = z_ref[:, o + DB:o + 2 * DB]
        zx = z_ref[:, o + 2 * DB:]
        hc = zc * zx
        keep = (i > 0).astype(F32)
        p1 = zp_ref[7:8, o + DB:o + 2 * DB] * zp_ref[7:8, o + 2 * DB:] * keep
        p2 = zp_ref[6:7, o + DB:o + 2 * DB] * zp_ref[6:7, o + 2 * DB:] * keep
        rowi = lax.broadcasted_iota(jnp.int32, (tm, DB), 0)
        r1, r2 = _conv_shifted(hc, p1, p2, rowi)
        w0, w1, w2 = cw_ref[0:1, :], cw_ref[1:2, :], cw_ref[2:3, :]
        cv = w0 * r2 + w1 * r1 + w2 * hc
        dyb = dy_ref[:, DA:]
        dzb = dyb * cv
        dcv = dyb * zb
        more = (i < nblk - 1).astype(F32)
        n0 = dyn_ref[0:1, DA:] * zn_ref[0:1, o:o + DB] * more
        n1 = dyn_ref[1:2, DA:] * zn_ref[1:2, o:o + DB] * more
        f1 = jnp.where(rowi == tm - 1, n0, pltpu.roll(dcv, tm - 1, 0))
        f2 = jnp.where(rowi == tm - 1, n1, jnp.where(rowi == tm - 2, n0, pltpu.roll(dcv, tm - 2, 0)))
        dhc = w2 * dcv + w1 * f1 + w0 * f2
        dzc = dhc * zx
        dzx = dhc * zc
        dcw_ref[0:1, :] += jnp.sum(dcv * r2, axis=0, keepdims=True)
        dcw_ref[1:2, :] += jnp.sum(dcv * r1, axis=0, keepdims=True)
        dcw_ref[2:3, :] += jnp.sum(dcv * hc, axis=0, keepdims=True)
        dz_ref[:, o:o + DB] = dzb.astype(BF16)
        dz_ref[:, o + DB:o + 2 * DB] = dzc.astype(BF16)
        dz_ref[:, o + 2 * DB:] = dzx.astype(BF16)
        dbin_ref[:, o:o + DB] += jnp.sum(dzb, axis=0, keepdims=True)
        dbin_ref[:, o + DB:o + 2 * DB] += jnp.sum(dzc, axis=0, keepdims=True)
        dbin_ref[:, o + 2 * DB:] += jnp.sum(dzx, axis=0, keepdims=True)

    full3 = lambda i: (0, 0, 0)
    full2 = lambda i: (0, 0)
    prev8 = lambda i: (jnp.maximum(i * (tm // 8) - 1, 0), 0)
    next8 = lambda i: (jnp.minimum((i + 1) * (tm // 8), S // 8 - 1), 0)
    est = 2 * (tm * DIN * 4 + tm * (DA + DB) * 4 + tm * DIN * 2) + 16 * tm * DB * 4
    return pl.pallas_call(
        body, name="mix_bwd", grid=(nblk,),
        in_specs=[pl.BlockSpec((tm, DIN), lambda i: (i, 0)),
                  pl.BlockSpec((8, DIN), prev8), pl.BlockSpec((8, DIN), next8),
                  pl.BlockSpec((tm, DA + DB), lambda i: (i, 0)), pl.BlockSpec((8, DA + DB), next8),
                  pl.BlockSpec(ws.shape, full3), pl.BlockSpec(bsb.shape, full3),
                  pl.BlockSpec(gv.shape, full2), pl.BlockSpec(bv.shape, full2), pl.BlockSpec(cw8.shape, full2)],
        out_specs=[pl.BlockSpec((tm, DIN), lambda i: (i, 0)),
                   pl.BlockSpec(ws.shape, full3), pl.BlockSpec(ws.shape, full3),
                   pl.BlockSpec(gv.shape, full2), pl.BlockSpec(gv.shape, full2),
                   pl.BlockSpec(cw8.shape, full2), pl.BlockSpec((1, DIN), full2)],
        out_shape=[jax.ShapeDtypeStruct((S, DIN), BF16),
                   jax.ShapeDtypeStruct(ws.shape, F32), jax.ShapeDtypeStruct(ws.shape, F32),
                   jax.ShapeDtypeStruct(gv.shape, F32), jax.ShapeDtypeStruct(gv.shape, F32),
                   jax.ShapeDtypeStruct(cw8.shape, F32), jax.ShapeDtypeStruct((1, DIN), F32)],
        compiler_params=pltpu.CompilerParams(dimension_semantics=("arbitrary",),
                                             vmem_limit_bytes=_vmem_limit(est)),
    )(z, z, z, dy, dy, ws, bsb, gv, bv, cw8)


def _place():
    x, y, c = lax.axis_index("x"), lax.axis_index("y"), lax.axis_index("c")
    others = [(1 - x, y), (x, 1 - y), (1 - x, 1 - y)]
    return x, y, c, 2 * x + y, others, [2 * ox + oy for ox, oy in others]


def _half(c, rows_half):
    return pl.ds(pl.multiple_of(c * rows_half, 16), rows_half)


def _rcopy(src, dst, ssem, rsem, dev):
    return pltpu.make_async_remote_copy(src_ref=src, dst_ref=dst, send_sem=ssem, recv_sem=rsem,
                                        device_id=dev, device_id_type=MESH)


class _Comm:
    def __init__(self, aliased, ins, fresh, n_sems, stages, sink):
        self.aliased, self.ins, self.fresh = list(aliased), list(ins), list(fresh)
        self.n_sems, self.stages, self.sink = n_sems, stages, sink


def _comm_operands(comms, n_in, n_out):
    arrays, shapes, scratch, aliases = [], [], [], {}
    for cm in comms:
        for a in cm.aliased:
            aliases[n_in + len(arrays)] = n_out + len(shapes)
            arrays.append(a)
            shapes.append(jax.ShapeDtypeStruct(a.shape, a.dtype))
        arrays += cm.ins
        shapes += cm.fresh
        scratch += [pltpu.SemaphoreType.DMA((cm.n_sems,))] * 2
    return arrays, shapes, scratch, aliases


def _comm_refs(comms, in_refs, out_refs, sem_refs):
    per, pi, po = [], 0, 0
    for i, cm in enumerate(comms):
        pi += len(cm.aliased)
        ins = in_refs[pi:pi + len(cm.ins)]
        pi += len(cm.ins)
        al = out_refs[po:po + len(cm.aliased)]
        po += len(cm.aliased)
        fr = out_refs[po:po + len(cm.fresh)]
        po += len(cm.fresh)
        per.append((al, ins, fr, sem_refs[2 * i], sem_refs[2 * i + 1]))
    return per


def _comm_first_step(comms, hosted, grid):
    if not comms:
        return
    at_first = functools.reduce(lambda a, b: a & b, [pl.program_id(d) == 0 for d in range(len(grid))])

    @pl.when(at_first)
    def _():
        for cm, refs in zip(comms, hosted):
            for d in cm.stages[0](*refs, False):
                d.start()


def _comm_last_step(comms, hosted, grid):
    if not comms:
        return
    at_last = functools.reduce(lambda a, b: a & b, [pl.program_id(d) == g - 1 for d, g in enumerate(grid)])

    @pl.when(at_last)
    def _():
        for cm, refs in zip(comms, hosted):
            for d in cm.stages[0](*refs, True):
                d.wait()
        for cm, refs in zip(comms, hosted):
            for stage in cm.stages[1:]:
                for d in stage(*refs, False):
                    d.start()
                for d in stage(*refs, True):
                    d.wait()


def _deliver(comms, outs):
    pos = 0
    for cm in comms:
        al = outs[pos:pos + len(cm.aliased)]
        pos += len(cm.aliased)
        fr = outs[pos:pos + len(cm.fresh)]
        pos += len(cm.fresh)
        cm.sink(list(al), list(fr))


def _comm_call(name, comms):
    in_arrays, out_shapes, scratch, aliases = _comm_operands(comms, 0, 0)
    n_in, n_out = len(in_arrays), len(out_shapes)

    def body(*refs):
        hosted = _comm_refs(comms, refs[:n_in], refs[n_in:n_in + n_out], refs[n_in + n_out:])
        for cm, r in zip(comms, hosted):
            for stage in cm.stages:
                for d in stage(*r, False):
                    d.start()
                for d in stage(*r, True):
                    d.wait()

    outs = pl.pallas_call(body, name=name, in_specs=[ANY] * n_in, out_specs=[ANY] * n_out,
                          out_shape=out_shapes, scratch_shapes=scratch, input_output_aliases=aliases)(*in_arrays)
    _deliver(comms, outs)


def _c_gather(fulls, sink, ici, fwd, conv=None):
    n = len(fulls)
    n_ici = 3 * (n + (conv is not None)) if ici else 0

    def stage_ici(al, ins, fr, ssem, rsem, wait):
        x, y, c, chip, others, ochips = _place()
        ds = []
        for m in range(n + (conv is not None)):
            for k in range(3):
                i = 3 * m + k
                if m < n:
                    ah = al[m].shape[1] // 2
                    mine, lands = al[m].at[chip, _half(c, ah)], al[m].at[ochips[k], _half(c, ah)]
                else:
                    mine, lands = al[m].at[chip], al[m].at[ochips[k]]
                ds.append(_rcopy(mine, lands, ssem.at[i], rsem.at[i], (x, y, c)) if wait
                          else _rcopy(mine, mine, ssem.at[i], rsem.at[i], (*others[k], c)))
        return ds

    def stage_fwd(al, ins, fr, ssem, rsem, wait):
        x, y, c, chip, others, ochips = _place()
        ds = []
        for m in range(n):
            ah = al[m].shape[1] // 2
            for k in range(3):
                i = n_ici + 3 * m + k
                got = al[m].at[ochips[k], _half(c, ah)]
                ds.append(_rcopy(got, al[m].at[ochips[k], _half(1 - c, ah)], ssem.at[i], rsem.at[i], (x, y, c)) if wait
                          else _rcopy(got, got, ssem.at[i], rsem.at[i], (x, y, 1 - c)))
        return ds

    stages = ([stage_ici] if ici else []) + ([stage_fwd] if fwd else [])
    return _Comm(list(fulls) + ([conv] if conv is not None else []), [], [], n_ici + (3 * n if fwd else 0),
                 stages, sink)


def _c_rs_pair(parts, sink):
    def stage(al, ins, fr, ssem, rsem, wait):
        x, y, c, _, _, _ = _place()
        return [_rcopy(ins[m].at[:, _half(1 - c, ins[m].shape[1] // 2), :], fr[m], ssem.at[m], rsem.at[m],
                       (x, y, 1 - c)) for m in range(len(ins))]
    fresh = [jax.ShapeDtypeStruct((p.shape[0], p.shape[1] // 2, p.shape[2]), p.dtype) for p in parts]
    return _Comm([], parts, fresh, len(parts), [stage], sink)


def _c_rs_chips(psums, sink):
    def stage(al, ins, fr, ssem, rsem, wait):
        x, y, c, _, others, ochips = _place()
        return [_rcopy(ins[m].at[ochips[k]], fr[m].at[k], ssem.at[3 * m + k], rsem.at[3 * m + k], (*others[k], c))
                for m in range(len(ins)) for k in range(3)]
    fresh = [jax.ShapeDtypeStruct((3,) + p.shape[1:], p.dtype) for p in psums]
    return _Comm([], psums, fresh, 3 * len(psums), [stage], sink)


def _c_exchange(g2s, sink):
    def stage(al, ins, fr, ssem, rsem, wait):
        x, y, c, _, _, _ = _place()
        if wait:
            return [_rcopy(al[m].at[c], al[m].at[1 - c], ssem.at[m], rsem.at[m], (x, y, c)) for m in range(len(al))]
        return [_rcopy(al[m].at[c], al[m].at[c], ssem.at[m], rsem.at[m], (x, y, 1 - c)) for m in range(len(al))]
    return _Comm(g2s, [], [], len(g2s), [stage], sink)


def _c_allgather_small(buf, sink):
    def stage(al, ins, fr, ssem, rsem, wait):
        x, y, c, _, _, _ = _place()
        ds = []
        for r in range(1, 8):
            peer = (1 - x if r & 4 else x, 1 - y if r & 2 else y, 1 - c if r & 1 else c)
            ds.append(_rcopy(al[0].at[0], al[0].at[r], ssem.at[r - 1], rsem.at[r - 1], peer))
        return ds
    return _Comm([buf], [], [], 7, [stage], sink)


def _sum_small(buf, dev_idx):
    _, m, n = buf.shape

    def body(me_ref, b_ref, o_ref):
        me = me_ref[0]
        acc = b_ref[me]
        for d in range(1, 8):
            acc = acc + b_ref[jnp.bitwise_xor(me, d)]
        o_ref[...] = acc

    gs = pltpu.PrefetchScalarGridSpec(
        num_scalar_prefetch=1, grid=(1,),
        in_specs=[pl.BlockSpec(buf.shape, lambda i, me: (0, 0, 0))],
        out_specs=pl.BlockSpec((m, n), lambda i, me: (0, 0)))
    return pl.pallas_call(body, name="sum_small", grid_spec=gs,
                          out_shape=jax.ShapeDtypeStruct((m, n), F32))(dev_idx, buf)


def _sum_pair(name, part, sib, c_idx):
    _, A, B = part.shape
    ah = A // 2
    br = _rows(ah, B * 2, 2 * 2 ** 20)

    def body(c_ref, a_ref, b_ref, o_ref):
        o_ref[...] = (a_ref[...].astype(F32) + b_ref[...].astype(F32)).astype(BF16)

    gs = pltpu.PrefetchScalarGridSpec(
        num_scalar_prefetch=1, grid=(N_CHIPS, ah // br),
        in_specs=[pl.BlockSpec((None, None, br, B), lambda j, i, c: (j, c[0], i, 0)),
                  pl.BlockSpec((None, br, B), lambda j, i, c: (j, i, 0))],
        out_specs=pl.BlockSpec((None, br, B), lambda j, i, c: (j, i, 0)))
    return pl.pallas_call(body, name=name, grid_spec=gs,
                          out_shape=jax.ShapeDtypeStruct((N_CHIPS, ah, B), BF16))(
                              c_idx, part.reshape(N_CHIPS, 2, ah, B), sib)


def _sum_chips(name, psum, recv, place_idx):
    _, ah, B = psum.shape
    br = _rows(ah, B * 4, 2 * 2 ** 20)

    def body(c_ref, p_ref, r0_ref, r1_ref, r2_ref, o_ref):
        o_ref[...] = ((p_ref[...].astype(F32) + r0_ref[...].astype(F32)) + r1_ref[...].astype(F32)) \
            + r2_ref[...].astype(F32)

    gs = pltpu.PrefetchScalarGridSpec(
        num_scalar_prefetch=1, grid=(ah // br,),
        in_specs=[pl.BlockSpec((None, br, B), lambda i, c: (c[0], i, 0))]
        + [pl.BlockSpec((None, br, B), functools.partial(lambda i, c, k: (k, i, 0), k=k)) for k in range(3)],
        out_specs=pl.BlockSpec((None, br, B), lambda i, c: (c[1], i, 0)))
    return pl.pallas_call(body, name=name, grid_spec=gs,
                          out_shape=jax.ShapeDtypeStruct((2, ah, B), F32))(place_idx, psum, recv, recv, recv)


def _cast_into(name, a, place_idx):
    A, B = a.shape
    br = _rows(A, B * 4)

    def body(c_ref, a_ref, o_ref):
        o_ref[...] = a_ref[...].astype(BF16)

    gs = pltpu.PrefetchScalarGridSpec(
        num_scalar_prefetch=1, grid=(A // br,),
        in_specs=[pl.BlockSpec((br, B), lambda i, c: (i, 0))],
        out_specs=pl.BlockSpec((None, br, B), lambda i, c: (c[0], i, 0)))
    return pl.pallas_call(body, name=name, grid_spec=gs,
                          out_shape=jax.ShapeDtypeStruct((N_CHIPS, A, B), BF16))(place_idx, a)


def _cast_behind(x, shards, place_idx, comms):
    n_steps = 16
    n = len(shards)
    xv = x
    svs = list(shards)

    def pace(v):
        for steps_per_block in (1, 2, 4, 8, 16):
            blocks = n_steps // steps_per_block
            if v.shape[0] % blocks == 0 and (v.shape[0] // blocks) % 16 == 0:
                return steps_per_block
        raise ValueError(v.shape)

    n_in, n_out = 2 + n, 1 + n
    cm_in_arrays, cm_out_shapes, cm_scratch, aliases = _comm_operands(comms, n_in, n_out)
    n_cm_in, n_cm_out = len(cm_in_arrays), len(cm_out_shapes)
    grid = (n_steps,)

    def body(c_ref, x_ref, *rest):
        s_refs = rest[:n]
        xo_ref = rest[n + n_cm_in]
        so_refs = rest[n + n_cm_in + 1:n + n_cm_in + 1 + n]
        o_end = n + n_cm_in + 1 + n
        hosted = _comm_refs(comms, rest[n:n + n_cm_in], rest[o_end:o_end + n_cm_out], rest[o_end + n_cm_out:])
        _comm_first_step(comms, hosted, grid)
        xo_ref[...] = x_ref[...].astype(BF16)
        for s_ref, o_ref in zip(s_refs, so_refs):
            o_ref[...] = s_ref[...].astype(BF16)
        _comm_last_step(comms, hosted, grid)

    def rows(v):
        return v.shape[0] * pace(v) // n_steps

    def in_spec(v):
        return pl.BlockSpec((rows(v), v.shape[1]), functools.partial(lambda i, c, q: (i // q, 0), q=pace(v)))

    def shard_out_spec(v):
        return pl.BlockSpec((None, rows(v), v.shape[1]),
                            functools.partial(lambda i, c, q: (c[0], i // q, 0), q=pace(v)))

    gs = pltpu.PrefetchScalarGridSpec(
        num_scalar_prefetch=1, grid=grid,
        in_specs=[in_spec(v) for v in [xv] + svs] + [ANY] * n_cm_in,
        out_specs=[in_spec(xv)] + [shard_out_spec(v) for v in svs] + [ANY] * n_cm_out,
        scratch_shapes=cm_scratch)
    outs = pl.pallas_call(
        body, name="cast_behind", grid_spec=gs,
        out_shape=[jax.ShapeDtypeStruct(xv.shape, BF16)]
        + [jax.ShapeDtypeStruct((N_CHIPS,) + v.shape, BF16) for v in svs] + cm_out_shapes,
        input_output_aliases=aliases,
        compiler_params=pltpu.CompilerParams(
            dimension_semantics=("arbitrary",),
            vmem_limit_bytes=_vmem_limit(2 * sum(rows(v) * v.shape[1] * 6 for v in [xv] + svs))),
    )(place_idx, xv, *svs, *cm_in_arrays)
    _deliver(comms, outs[n_out:])
    return outs[0].reshape(x.shape), [o.reshape((N_CHIPS,) + s.shape) for o, s in zip(outs[1:n_out], shards)]


def _adamw_small(gsum, gconv, offsets, ws, ms, vs):
    n = len(ws)

    def body(*refs):
        g_ref, gc_ref = refs[0], refs[1]
        w_refs = refs[2:2 + n]
        m_refs = refs[2 + n:2 + 2 * n]
        v_refs = refs[2 + 2 * n:2 + 3 * n]
        outs = refs[2 + 3 * n:]
        for i in range(n):
            rows = w_refs[i].shape[0]
            g = gc_ref[...] if offsets[i] is None else g_ref[offsets[i]:offsets[i] + rows, :]
            d, m2, v2 = _adamw_math(w_refs[i][...], g, m_refs[i][...], v_refs[i][...])
            outs[4 * i][...] = g
            outs[4 * i + 1][...] = d
            outs[4 * i + 2][...] = m2
            outs[4 * i + 3][...] = v2

    vm = pl.BlockSpec(memory_space=pltpu.VMEM)
    out_shape = []
    for w in ws:
        out_shape += [jax.ShapeDtypeStruct(w.shape, F32)] * 4
    return pl.pallas_call(body, name="adamw_small", in_specs=[vm] * (2 + 3 * n), out_specs=[vm] * (4 * n),
                          out_shape=out_shape)(gsum, gconv, *ws, *ms, *vs)


def _ident(accs, tes, res):
    return [accs[0]]


def _ffn_up(name, hb, wg4, wu4, comms=()):
    S, D = hb.shape

    def epi(accs, tes, res):
        g, u = accs
        return [g, u, g * _sigmoid(g) * u]

    return _mm(name, 'nn', [hb], [wg4, wu4], [(0, 0, 0), (0, 1, 1)], 2, epi, [BF16] * 3,
               bm=_blk(S, 512), bn=wg4.shape[2], bk=D, b_blocked=True, n_outer=True, comms=comms)


def _ffn_gate(name, hb, wg4, comms=()):
    S, D = hb.shape
    return _mm(name, 'nn', [hb], [wg4], [(0, 0, 0)], 1, _ident, [BF16],
               bm=_blk(S, 1024), bn=wg4.shape[2], bk=D, b_blocked=True, n_outer=True, comms=comms)[0]


def _ffn_upmul(name, hb, wu4, gb, comms=()):
    S, D = hb.shape

    def epi(accs, tes, res):
        g = tes[0].astype(F32)
        return [accs[0], g * _sigmoid(g) * accs[0]]

    return _mm(name, 'nn', [hb], [wu4], [(0, 0, 0)], 1, epi, [BF16, BF16],
               bm=_blk(S, 1024), bn=wu4.shape[2], bk=D, tile_extras=[gb], b_blocked=True, n_outer=True, comms=comms)


def _ffn_down(name, ab, wd, hres, comms=()):
    S, F = ab.shape
    D = wd.shape[1]
    return _mm(name, 'nn', [ab], [wd], [(0, 0, 0)], 1, lambda accs, tes, res: [ALPHA * tes[0] + 0.5 * accs[0]],
               [F32], bm=_blk(S, 1024), bn=_blk(D, 512), bk=F, tile_extras=[hres], comms=comms)[0]


def _ffn_da(name, dsub, wd, gb, ub, comms=()):
    S, D = dsub.shape
    F = wd.shape[0]

    def epi(accs, tes, res):
        da = accs[0]
        g = tes[0].astype(F32)
        u = tes[1].astype(F32)
        s = _sigmoid(g)
        silu = g * s
        return [(da * u) * (s + silu * (1.0 - s)), da * silu]

    return _mm(name, 'nt', [dsub], [wd], [(0, 0, 0)], 1, epi, [BF16, BF16],
               bm=_blk(S, 1024), bn=_blk(F, 512), bk=D, tile_extras=[gb, ub], comms=comms)


def _dw_rows(name, ab, dsub, cs, comms=()):
    S, D = dsub.shape
    return _mm(name, 'tn', [ab], [dsub], [(0, 0, 0)], 1, _ident, [BF16],
               bm=cs, bn=_blk(D, 1024), bk=_blk(S, 2048), comms=comms)[0].reshape(N_CHIPS, cs, D)


def _dw_cols(name, hb, dxb, cs, comms=()):
    S, D = hb.shape
    return _mm(name, 'tn', [hb], [dxb], [(0, 0, 0)], 1, _ident, [BF16],
               bm=_blk(D, 1024), bn=cs, bk=_blk(S, 2048), out_blocked=True, comms=comms)[0]


def _ffn_dh(name, dgb, dub, wg4, wu4, dres, comms=()):
    S = dgb.shape[0]
    D = wg4.shape[1]
    return _mm(name, 'nt', [dgb, dub], [wg4, wu4], [(0, 0, 0), (1, 1, 0)], 1,
               lambda accs, tes, res: [accs[0] + tes[0]], [F32],
               bm=_blk(S, 1024), bn=_blk(D, 1024), bk=wg4.shape[2], tile_extras=[dres], b_blocked=True,
               comms=comms)[0]


def _step(p):
    x = p['x'][0]
    target = p['loss_target'][0]
    S, D = x.shape
    cx, cy, cc = lax.axis_index("x"), lax.axis_index("y"), lax.axis_index("c")
    chip = 2 * cx + cy
    c_idx = jnp.reshape(cc, (1,)).astype(jnp.int32)
    place_idx = jnp.stack([chip, cc]).astype(jnp.int32)

    w_s = p['w_s'][0]
    H = w_s.shape[0]
    DA = H * HEAD
    bsb = jnp.broadcast_to(p['b_s'][0][:, :, None], (H, HEAD, HEAD))
    conv_shard = p['conv_w'][0]
    conv8 = jnp.zeros((8, conv_shard.shape[1]), F32).at[:CONV_TAPS].set(conv_shard)

    W = {'ffa_gate': _cast_into("cast_ffa_gate", p['ffa_gate'][0], place_idx)}
    conv = {'full': lax.dynamic_update_slice(jnp.zeros((N_CHIPS,) + conv8.shape, F32), conv8[None], (chip, 0, 0))}

    def gathered(names, with_conv=False):
        def sink(al, fr):
            W.update(zip(names, al))
            if with_conv:
                conv['full'] = al[len(names)]
        return sink

    def c_gather(names, ici, fwd, with_conv=False):
        return _c_gather([W[n] for n in names], gathered(names, with_conv), ici, fwd,
                         conv['full'] if with_conv else None)

    rest = [n for n in BIG if n != 'ffa_gate']
    xb, casted = _cast_behind(x, [p[n][0] for n in rest], place_idx, comms=[c_gather(['ffa_gate'], True, True)])
    W.update(zip(rest, casted))
    cs_ff = W['ffa_gate'].shape[2]
    F = N_CHIPS * cs_ff

    ga = _ffn_gate("ffa_gate", xb, W['ffa_gate'],
                   comms=[c_gather(['ffa_up'], True, True), c_gather(['ffa_down'], True, False)])
    ua, aa = _ffn_upmul("ffa_up", xb, W['ffa_up'], ga,
                        comms=[c_gather(['ffa_down'], False, True),
                               c_gather(['w_in', 'ffc_gate'], True, False, with_conv=True)])
    cw8 = jnp.transpose(conv['full'], (1, 0, 2)).reshape(8, DA)
    wd_a = W['ffa_down'].reshape(F, D)
    r1 = _ffn_down("ffa_down", aa, wd_a, x,
                   comms=[c_gather(['w_in', 'ffc_gate'], False, True), c_gather(['w_out', 'ffc_down'], True, False)])
    h1, h1b = _ln_fwd("ln_a", r1, p['ln_a_g'], p['ln_a_b'])
    w_in4 = W['w_in']
    cs_in = w_in4.shape[2]
    z = _mm("w_in", 'nn', [h1b], [w_in4], [(0, 0, 0)], 1, lambda accs, tes, res: [accs[0] + res[0]], [F32],
            bm=_blk(S, 1024), bn=cs_in, bk=D, row_extras=[p['b_in']], b_blocked=True, n_outer=True,
            comms=[c_gather(['w_out', 'ffc_down'], False, True), c_gather(['ffc_up'], True, False)])[0]
    w_out = W['w_out'].reshape(DA * 2, D)
    tm = _blk(S, 256)
    yb = _mix_fwd(z, w_s, bsb, p['ln_v_g'], p['ln_v_b'], cw8, tm)
    r2 = _mm("w_out", 'nn', [yb], [w_out], [(0, 0, 0)], 1,
             lambda accs, tes, res: [accs[0] + res[0] + ALPHA * tes[0]], [F32],
             bm=_blk(S, 1024), bn=_blk(D, 1024), bk=2 * DA, tile_extras=[h1], row_extras=[p['b_out']],
             comms=[c_gather(['ffc_up'], False, True)])[0]
    h2, h2b = _ln_fwd("ln_m", r2, p['ln_m_g'], p['ln_m_b'])
    gc, uc, ac = _ffn_up("ffc_up", h2b, W['ffc_gate'], W['ffc_up'])
    wd_c = W['ffc_down'].reshape(F, D)
    r3 = _ffn_down("ffc_down", ac, wd_c, h2)

    pg, sg, sib, psum, recv, g2 = {}, {}, {}, {}, {}, {}

    def c_pair(names):
        return _c_rs_pair([pg[n] for n in names], lambda al, fr: sib.update(zip(names, fr)))

    def c_chips(names):
        return _c_rs_chips([psum[n] for n in names], lambda al, fr: recv.update(zip(names, fr)))

    def c_swap(names):
        return _c_exchange([g2[n] for n in names], lambda al, fr: g2.update(zip(names, al)))

    def sum_pair(*names):
        for n in names:
            psum[n] = _sum_pair(f"sum_pair_{n}", pg[n], sib[n], c_idx)

    def sum_chips(*names):
        for n in names:
            g2[n] = _sum_chips(f"sum_chips_{n}", psum[n], recv[n], place_idx)

    dres3, dsub3, sg['ln_c_g'], sg['ln_c_b'], lsum = _ln_loss_bwd(
        "ln_c_loss_bwd", r3, target, p['ln_c_g'], p['ln_c_b'], 0.5)

    dgc, duc = _ffn_da("ffc_da", dsub3, wd_c, gc, uc)
    pg['ffc_down'] = _dw_rows("ffc_dwd", ac, dsub3, cs_ff)
    pg['ffc_gate'] = _dw_cols("ffc_dwg", h2b, dgc, cs_ff, comms=[c_pair(['ffc_down'])])
    sum_pair('ffc_down')
    pg['ffc_up'] = _dw_cols("ffc_dwu", h2b, duc, cs_ff, comms=[c_pair(['ffc_gate']), c_chips(['ffc_down'])])
    sum_pair('ffc_gate')
    dh2 = _ffn_dh("ffc_dh", dgc, duc, W['ffc_gate'], W['ffc_up'], dres3,
                  comms=[c_pair(['ffc_up']), c_chips(['ffc_gate'])])
    sum_pair('ffc_up')
    sum_chips('ffc_down', 'ffc_gate')

    dres2, dmix, sg['ln_m_g'], sg['ln_m_b'], sg['b_out'] = _ln_bwd("ln_m_bwd", r2, dh2, p['ln_m_g'], 1.0)
    pg['w_out'] = _mm("dw_out", 'tn', [yb], [dmix], [(0, 0, 0)], 1, _ident, [BF16],
                      bm=_blk(2 * DA, 1024), bn=_blk(D, 1024), bk=_blk(S, 2048),
                      comms=[c_swap(['ffc_down', 'ffc_gate'])])[0].reshape(N_CHIPS, 2 * DA // 4, D)
    dy = _mm("dy", 'nt', [dmix], [w_out], [(0, 0, 0)], 1, _ident, [F32],
             bm=_blk(S, 1024), bn=_blk(2 * DA, 1024), bk=D, comms=[c_pair(['w_out'])])[0]
    sum_pair('w_out')
    dz, dws, dbs, sg['ln_v_g'], sg['ln_v_b'], dcw, sg['b_in'] = _mix_bwd(
        z, dy, w_s, bsb, p['ln_v_g'], p['ln_v_b'], cw8, tm)
    sg['w_s'] = dws
    sg['b_s'] = dbs[:, :, 0]
    pg['w_in'] = _dw_cols("dw_in", h1b, dz, cs_in, comms=[c_chips(['ffc_up']), c_chips(['w_out'])])
    sum_chips('ffc_up', 'w_out')
    dh1 = _mm("dh1", 'nt', [dz], [w_in4], [(0, 0, 0)], 1, lambda accs, tes, res: [accs[0] + tes[0]], [F32],
              bm=_blk(S, 1024), bn=_blk(D, 1024), bk=cs_in, tile_extras=[dres2], b_blocked=True,
              comms=[c_pair(['w_in']), c_swap(['ffc_up', 'w_out'])])[0]
    sum_pair('w_in')

    dres1, dsub1, sg['ln_a_g'], sg['ln_a_b'], _ = _ln_bwd("ln_a_bwd", r1, dh1, p['ln_a_g'], 0.5)
    def as_rows(a):
        a = a.reshape(-1, 128)
        pad = (-a.shape[0]) % 8
        return jnp.pad(a, ((0, pad), (0, 0))) if pad else a

    pieces, offsets, off = [], {}, 0
    for n in SMALL + ['conv_w']:
        piece = as_rows(dcw[:CONV_TAPS] if n == 'conv_w' else sg[n])
        offsets[n] = off
        off += piece.shape[0]
        pieces.append(piece)
    packed = jnp.concatenate(pieces, axis=0)
    small = {'buf': jnp.zeros((8,) + packed.shape, F32).at[0].set(packed)}
    dga, dua = _ffn_da("ffa_da", dsub1, wd_a, ga, ua,
                       comms=[c_chips(['w_in']), _c_allgather_small(small['buf'], lambda al, fr: small.update(buf=al[0]))])
    sum_chips('w_in')
    gsum = _sum_small(small['buf'], jnp.reshape(4 * cx + 2 * cy + cc, (1,)).astype(jnp.int32))
    pg['ffa_down'] = _dw_rows("ffa_dwd", aa, dsub1, cs_ff, comms=[c_swap(['w_in'])])
    pg['ffa_gate'] = _dw_cols("ffa_dwg", xb, dga, cs_ff, comms=[c_pair(['ffa_down'])])
    sum_pair('ffa_down')
    pg['ffa_up'] = _dw_cols("ffa_dwu", xb, dua, cs_ff, comms=[c_pair(['ffa_gate']), c_chips(['ffa_down'])])
    sum_pair('ffa_gate')
    _comm_call("pair_ffa_up", [c_pair(['ffa_up'])])
    sum_pair('ffa_up')
    sum_chips('ffa_down')
    dx = _ffn_dh("ffa_dh", dga, dua, W['ffa_gate'], W['ffa_up'], dres1,
                 comms=[c_chips(['ffa_gate']), c_chips(['ffa_up']), c_swap(['ffa_down'])])
    sum_chips('ffa_gate', 'ffa_up')
    _comm_call("swap_ffa", [c_swap(['ffa_gate', 'ffa_up'])])

    out = {}
    for n in BIG:
        g, d, m2, v2 = _adamw(f"adamw_{n}", p[n][0], g2[n].reshape(p[n][0].shape), p['m_' + n][0], p['v_' + n][0])
        shp = p[n].shape
        out[n] = (g.reshape(shp), d.reshape(shp), m2.reshape(shp), v2.reshape(shp))

    loss = lax.psum(0.5 * jnp.sum(lsum) / D, ("x", "y", "c"))
    n_conv = CONV_TAPS * DA // 128
    conv_sum = gsum[offsets['conv_w']:offsets['conv_w'] + n_conv].reshape(CONV_TAPS, DA)
    cw = conv_shard.shape[1]
    gconv = lax.dynamic_slice_in_dim(conv_sum, (2 * cx + cy) * cw, cw, axis=1).reshape(-1, 128)
    names = SMALL + ['conv_w']
    rows128 = lambda a: a.reshape(-1, 128)
    res = _adamw_small(gsum, gconv, [offsets[n] if n != 'conv_w' else None for n in names],
                       [rows128(p[n]) for n in names], [rows128(p['m_' + n]) for n in names],
                       [rows128(p['v_' + n]) for n in names])
    for i, n in enumerate(names):
        out[n] = tuple(r.reshape(p[n].shape) for r in res[4 * i:4 * i + 4])

    return (loss, dx.reshape(p['x'].shape), *[out[n][0] for n in WEIGHTS], *[out[n][1] for n in WEIGHTS],
            *[out[n][2] for n in WEIGHTS], *[out[n][3] for n in WEIGHTS])


def kernel(x, ffa_gate, ffa_up, ffa_down, ln_a_g, ln_a_b, w_in, b_in, w_s, b_s, ln_v_g, ln_v_b, conv_w, w_out, b_out, ln_m_g, ln_m_b, ffc_gate, ffc_up, ffc_down, ln_c_g, ln_c_b, loss_target, m_ffa_gate, m_ffa_up, m_ffa_down, m_ln_a_g, m_ln_a_b, m_w_in, m_b_in, m_w_s, m_b_s, m_ln_v_g, m_ln_v_b, m_conv_w, m_w_out, m_b_out, m_ln_m_g, m_ln_m_b, m_ffc_gate, m_ffc_up, m_ffc_down, m_ln_c_g, m_ln_c_b, v_ffa_gate, v_ffa_up, v_ffa_down, v_ln_a_g, v_ln_a_b, v_w_in, v_b_in, v_w_s, v_b_s, v_ln_v_g, v_ln_v_b, v_conv_w, v_w_out, v_b_out, v_ln_m_g, v_ln_m_b, v_ffc_gate, v_ffc_up, v_ffc_down, v_ln_c_g, v_ln_c_b):
    return _step(dict(locals()))
```

```python
import functools
import math

import jax
import jax.numpy as jnp
from jax import lax
from jax.experimental import pallas as pl
from jax.experimental.pallas import tpu as pltpu

F32 = jnp.float32
BF16 = jnp.bfloat16
MESH = pl.DeviceIdType.MESH
ANY = pl.BlockSpec(memory_space=pl.ANY)

HEAD = 128
CONV_TAPS = 3
LN_EPS = 1e-5
ALPHA = float(2.0 ** 0.25)
ADAM_LR, ADAM_B1, ADAM_B2, ADAM_EPS, ADAM_WD, ADAM_STEP = 0.001, 0.9, 0.999, 1e-08, 0.01, 10

V7X_VMEM_BYTES = 64 * 2 ** 20
N_CHIPS = 4

BIG = ['ffa_gate', 'ffa_up', 'ffa_down', 'w_in', 'w_out', 'ffc_gate', 'ffc_up', 'ffc_down']
WEIGHTS = ['ffa_gate', 'ffa_up', 'ffa_down', 'ln_a_g', 'ln_a_b', 'w_in', 'b_in', 'w_s', 'b_s', 'ln_v_g', 'ln_v_b',
           'conv_w', 'w_out', 'b_out', 'ln_m_g', 'ln_m_b', 'ffc_gate', 'ffc_up', 'ffc_down', 'ln_c_g', 'ln_c_b']
SMALL = [n for n in WEIGHTS if n not in BIG and n != 'conv_w']


def _vmem_limit(estimate_bytes):
    return int(min(max(estimate_bytes * 1.3, 32 * 2 ** 20), V7X_VMEM_BYTES - 6 * 2 ** 20))


def _blk(dim, pref, mult=128):
    if dim <= pref:
        return dim
    best = None
    for d in range(mult, pref + 1, mult):
        if dim % d == 0:
            best = d
    assert best is not None, (dim, pref)
    return best


def _rows(n_rows, row_bytes, target=4 * 2 ** 20):
    return _blk(n_rows, max(16, target // row_bytes), 16)


def _sigmoid(x):
    return 0.5 * jnp.tanh(0.5 * x) + 0.5


def _gelu(x):
    c = math.sqrt(2.0 / math.pi)
    return 0.5 * x * (1.0 + jnp.tanh(c * (x + 0.044715 * (x * x * x))))


def _gelu_and_grad(x):
    c = math.sqrt(2.0 / math.pi)
    t = jnp.tanh(c * (x + 0.044715 * (x * x * x)))
    g = 0.5 * x * (1.0 + t)
    dg = 0.5 * (1.0 + t) + 0.5 * x * (1.0 - t * t) * (c * (1.0 + 3.0 * 0.044715 * (x * x)))
    return g, dg


def _ln_stats(r):
    mu = jnp.mean(r, axis=-1, keepdims=True)
    d = r - mu
    var = jnp.mean(d * d, axis=-1, keepdims=True)
    rstd = lax.rsqrt(var + LN_EPS)
    return d * rstd, rstd


_DIMS = {'nn': (((1,), (0,)), ((), ())), 'nt': (((1,), (1,)), ((), ())), 'tn': (((0,), (0,)), ((), ()))}


def _mm(name, mode, a_list, b_list, pairs, n_acc, epilogue, out_dtypes, bm, bn, bk,
        tile_extras=(), row_extras=(), b_blocked=False, out_blocked=False, n_outer=False, comms=()):
    a0, b0 = a_list[0], b_list[0]
    if mode == 'nn':
        M, K = a0.shape
        N = b0.shape[0] * b0.shape[2] if b_blocked else b0.shape[1]
    elif mode == 'nt':
        M, K = a0.shape
        N = b0.shape[1] if b_blocked else b0.shape[0]
    else:
        K, M = a0.shape
        N = b0.shape[1]
        assert not b_blocked
    assert M % bm == 0 and N % bn == 0 and K % bk == 0, (name, M, N, K, bm, bn, bk)
    gm, gn, gk = M // bm, N // bn, K // bk
    if b_blocked:
        assert (bn if mode == 'nn' else bk) == b0.shape[2], name
    dims = _DIMS[mode]

    def spec(shape, f):
        if n_outer:
            return pl.BlockSpec(shape, lambda g0, g1, g2: f(g1, g0, g2))
        return pl.BlockSpec(shape, f)

    if mode == 'tn':
        a_spec = spec((bk, bm), lambda i, j, k: (k, i))
    else:
        a_spec = spec((bm, bk), lambda i, j, k: (i, k))
    if mode == 'nn':
        b_spec = (spec((None, bk, bn), lambda i, j, k: (j, k, 0)) if b_blocked
                  else spec((bk, bn), lambda i, j, k: (k, j)))
    elif mode == 'nt':
        b_spec = (spec((None, bn, bk), lambda i, j, k: (k, j, 0)) if b_blocked
                  else spec((bn, bk), lambda i, j, k: (j, k)))
    else:
        b_spec = spec((bk, bn), lambda i, j, k: (k, j))
    te_spec = spec((bm, bn), lambda i, j, k: (i, j))
    re_spec = spec((1, bn), lambda i, j, k: (0, j))
    if out_blocked:
        o_spec = spec((None, bm, bn), lambda i, j, k: (j, i, 0))
        o_shape = (gn, M, bn)
    else:
        o_spec = te_spec
        o_shape = (M, N)

    na, nb, nte, nre, no = len(a_list), len(b_list), len(tile_extras), len(row_extras), len(out_dtypes)

    n_scratch_acc = n_acc if gk > 1 else 0
    grid = (gn, gm, gk) if n_outer else (gm, gn, gk)
    cm_in_arrays, cm_out_shapes, cm_scratch, aliases = _comm_operands(comms, na + nb + nte + nre, no)
    n_cm_in, n_cm_out = len(cm_in_arrays), len(cm_out_shapes)

    def body(*refs):
        a_refs = refs[:na]
        b_refs = refs[na:na + nb]
        te_refs = refs[na + nb:na + nb + nte]
        re_refs = refs[na + nb + nte:na + nb + nte + nre]
        n_in = na + nb + nte + nre
        o_refs = refs[n_in + n_cm_in:n_in + n_cm_in + no]
        acc_refs = refs[n_in + n_cm_in + no + n_cm_out:n_in + n_cm_in + no + n_cm_out + n_scratch_acc]
        hosted = _comm_refs(comms, refs[n_in:n_in + n_cm_in], refs[n_in + n_cm_in + no:n_in + n_cm_in + no + n_cm_out],
                            refs[n_in + n_cm_in + no + n_cm_out + n_scratch_acc:])
        _comm_first_step(comms, hosted, grid)

        def finish(accs):
            outs = epilogue(accs, [r[...] for r in te_refs], [r[...] for r in re_refs])
            for o_ref, o in zip(o_refs, outs):
                o_ref[...] = o.astype(o_ref.dtype)

        def products():
            parts = [None] * n_acc
            for ai, bi, ci in pairs:
                d = lax.dot_general(a_refs[ai][...], b_refs[bi][...], dims, preferred_element_type=F32)
                parts[ci] = d if parts[ci] is None else parts[ci] + d
            return parts

        if gk == 1:
            finish(products())
        else:
            kk = pl.program_id(2)

            @pl.when(kk == 0)
            def _():
                for r, part in zip(acc_refs, products()):
                    r[...] = part

            if gk > 2:
                @pl.when((kk > 0) & (kk < gk - 1))
                def _():
                    for r, part in zip(acc_refs, products()):
                        r[...] += part

            @pl.when(kk == gk - 1)
            def _():
                finish([r[...] + part for r, part in zip(acc_refs, products())])

        _comm_last_step(comms, hosted, grid)

    isz = lambda a: jnp.dtype(a.dtype).itemsize
    est = 2 * (sum(bm * bk * isz(a) for a in a_list) + sum(bk * bn * isz(b) for b in b_list)
               + sum(bm * bn * isz(t) for t in tile_extras)
               + sum(bm * bn * jnp.dtype(d).itemsize for d in out_dtypes))
    est += (2 * n_acc + 2) * bm * bn * 4
    outs = pl.pallas_call(
        body, name=name, grid=grid,
        in_specs=[a_spec] * na + [b_spec] * nb + [te_spec] * nte + [re_spec] * nre + [ANY] * len(cm_in_arrays),
        out_specs=[o_spec] * no + [ANY] * len(cm_out_shapes),
        out_shape=[jax.ShapeDtypeStruct(o_shape, d) for d in out_dtypes] + cm_out_shapes,
        scratch_shapes=[pltpu.VMEM((bm, bn), F32)] * n_scratch_acc + cm_scratch,
        input_output_aliases=aliases,
        compiler_params=pltpu.CompilerParams(dimension_semantics=("parallel", "parallel", "arbitrary"),
                                             vmem_limit_bytes=_vmem_limit(est)),
    )(*a_list, *b_list, *tile_extras, *row_extras, *cm_in_arrays)
    _deliver(comms, outs[no:])
    return outs[:no]


def _rowwise(name, fn, br, row_ins, vec_ins, row_outs, acc_outs=()):
    S = row_ins[0].shape[0]
    assert S % br == 0, (name, S, br)
    nr, nv, no, nacc = len(row_ins), len(vec_ins), len(row_outs), len(acc_outs)

    def body(*refs):
        rin = refs[:nr]
        vin = refs[nr:nr + nv]
        rout = refs[nr + nv:nr + nv + no]
        aout = refs[nr + nv + no:]
        if nacc:
            @pl.when(pl.program_id(0) == 0)
            def _():
                for a in aout:
                    a[...] = jnp.zeros_like(a)
        fn(rin, vin, rout, aout)

    def full_spec(shape):
        nd = len(shape)
        return pl.BlockSpec(shape, lambda i: (0,) * nd)

    est = 2 * (sum(br * a.shape[1] * jnp.dtype(a.dtype).itemsize for a in row_ins)
               + sum(br * c * jnp.dtype(d).itemsize for c, d in row_outs))
    est += 6 * br * max(a.shape[1] for a in row_ins) * 4
    return pl.pallas_call(
        body, name=name, grid=(S // br,),
        in_specs=[pl.BlockSpec((br, a.shape[1]), lambda i: (i, 0)) for a in row_ins]
        + [full_spec(v.shape) for v in vec_ins],
        out_specs=[pl.BlockSpec((br, c), lambda i: (i, 0)) for c, _ in row_outs]
        + [full_spec(s) for s, _ in acc_outs],
        out_shape=[jax.ShapeDtypeStruct((S, c), d) for c, d in row_outs]
        + [jax.ShapeDtypeStruct(s, d) for s, d in acc_outs],
        compiler_params=pltpu.CompilerParams(dimension_semantics=("arbitrary",),
                                             vmem_limit_bytes=_vmem_limit(est)),
    )(*row_ins, *vec_ins)


def _ln_fwd(name, r, g, b):
    def fn(rin, vin, rout, aout):
        xhat, _ = _ln_stats(rin[0][...])
        h = xhat * vin[0][...] + vin[1][...]
        rout[0][...] = h
        rout[1][...] = h.astype(BF16)
    D = r.shape[1]
    return _rowwise(name, fn, _rows(r.shape[0], D * 4, 2 * 2 ** 20), [r], [g, b], [(D, F32), (D, BF16)])


def _ln_loss_bwd(name, r, target, g, b, sub_scale):
    D = r.shape[1]

    def fn(rin, vin, rout, aout):
        xhat, rstd = _ln_stats(rin[0][...])
        gain = vin[0][...]
        err = xhat * gain + vin[1][...] - rin[1][...]
        dhv = err * (1.0 / D)
        dxh = dhv * gain
        m1 = jnp.mean(dxh, axis=-1, keepdims=True)
        m2 = jnp.mean(dxh * xhat, axis=-1, keepdims=True)
        dr = rstd * (dxh - m1 - xhat * m2)
        rout[0][...] = ALPHA * dr
        rout[1][...] = (sub_scale * dr).astype(BF16)
        aout[0][...] += jnp.sum(dhv * xhat, axis=0, keepdims=True)
        aout[1][...] += jnp.sum(dhv, axis=0, keepdims=True)
        aout[2][...] += jnp.sum(err * err, axis=0, keepdims=True)
    return _rowwise(name, fn, _rows(r.shape[0], D * 4, 2 * 2 ** 20), [r, target], [g, b],
                    [(D, F32), (D, BF16)], [((1, D), F32)] * 3)


def _ln_bwd(name, r, dh, g, sub_scale):
    D = r.shape[1]

    def fn(rin, vin, rout, aout):
        xhat, rstd = _ln_stats(rin[0][...])
        dhv = rin[1][...]
        dxh = dhv * vin[0][...]
        m1 = jnp.mean(dxh, axis=-1, keepdims=True)
        m2 = jnp.mean(dxh * xhat, axis=-1, keepdims=True)
        dr = rstd * (dxh - m1 - xhat * m2)
        rout[0][...] = ALPHA * dr
        rout[1][...] = (sub_scale * dr).astype(BF16)
        aout[0][...] += jnp.sum(dhv * xhat, axis=0, keepdims=True)
        aout[1][...] += jnp.sum(dhv, axis=0, keepdims=True)
        aout[2][...] += jnp.sum(dr, axis=0, keepdims=True)
    return _rowwise(name, fn, _rows(r.shape[0], D * 4, 2 * 2 ** 20), [r, dh], [g],
                    [(D, F32), (D, BF16)], [((1, D), F32)] * 3)


def _adamw_math(w, g, m, v):
    m2 = ADAM_B1 * m + (1.0 - ADAM_B1) * g
    v2 = ADAM_B2 * v + (1.0 - ADAM_B2) * (g * g)
    m_hat = m2 / (1.0 - ADAM_B1 ** ADAM_STEP)
    v_hat = v2 / (1.0 - ADAM_B2 ** ADAM_STEP)
    delta = -ADAM_LR * (m_hat / (jnp.sqrt(v_hat) + ADAM_EPS) + ADAM_WD * w)
    return delta, m2, v2


def _adamw(name, w, g, m, v):
    def fn(rin, vin, rout, aout):
        gv = rin[1][...]
        d, m2, v2 = _adamw_math(rin[0][...], gv, rin[2][...], rin[3][...])
        rout[0][...] = gv
        rout[1][...] = d
        rout[2][...] = m2
        rout[3][...] = v2
    C = w.shape[1]
    return _rowwise(name, fn, _rows(w.shape[0], C * 4, 2 * 2 ** 20), [w, g, m, v], [], [(C, F32)] * 4)


def _conv_shifted(hc, p1, p2, rowi):
    r1 = jnp.where(rowi == 0, p1, pltpu.roll(hc, 1, 0))
    r2 = jnp.where(rowi == 0, p2, jnp.where(rowi == 1, p1, pltpu.roll(hc, 2, 0)))
    return r1, r2


def _mix_fwd(z, ws, bsb, gv, bv, cw8, tm):
    S, DIN = z.shape
    DA = gv.shape[1]
    DB = DA
    H = DA // HEAD
    o = 2 * DA
    nch = tm // HEAD

    def body(z_ref, zp_ref, ws_ref, bsb_ref, gv_ref, bv_ref, cw_ref, y_ref):
        i = pl.program_id(0)
        ri = lax.broadcasted_iota(jnp.int32, (HEAD, HEAD), 0)
        ci = lax.broadcasted_iota(jnp.int32, (HEAD, HEAD), 1)
        tril = ri >= ci
        for h in range(H):
            cs = slice(h * HEAD, (h + 1) * HEAD)
            vs = slice(DA + h * HEAD, DA + (h + 1) * HEAD)
            wb = jnp.where(tril, ws_ref[h], 0.0).astype(BF16)
            for n in range(nch):
                rs = slice(n * HEAD, (n + 1) * HEAD)
                u = _gelu(z_ref[rs, cs])
                vhat, _ = _ln_stats(_gelu(z_ref[rs, vs]))
                vn = vhat * gv_ref[:, cs] + bv_ref[:, cs]
                mixed = jnp.dot(wb, vn.astype(BF16), preferred_element_type=F32) + bsb_ref[h]
                y_ref[rs, cs] = (u * mixed).astype(BF16)
        zb = z_ref[:, o:o + DB]
        hc = z_ref[:, o + DB:o + 2 * DB] * z_ref[:, o + 2 * DB:]
        keep = (i > 0).astype(F32)
        p1 = zp_ref[7:8, o + DB:o + 2 * DB] * zp_ref[7:8, o + 2 * DB:] * keep
        p2 = zp_ref[6:7, o + DB:o + 2 * DB] * zp_ref[6:7, o + 2 * DB:] * keep
        rowi = lax.broadcasted_iota(jnp.int32, (tm, DB), 0)
        r1, r2 = _conv_shifted(hc, p1, p2, rowi)
        cv = cw_ref[0:1, :] * r2 + cw_ref[1:2, :] * r1 + cw_ref[2:3, :] * hc
        y_ref[:, DA:] = (zb * cv).astype(BF16)

    full3 = lambda i: (0, 0, 0)
    full2 = lambda i: (0, 0)
    est = 2 * (tm * DIN * 4 + tm * (DA + DB) * 2) + 10 * tm * DB * 4
    return pl.pallas_call(
        body, name="mix_fwd", grid=(S // tm,),
        in_specs=[pl.BlockSpec((tm, DIN), lambda i: (i, 0)),
                  pl.BlockSpec((8, DIN), lambda i: (jnp.maximum(i * (tm // 8) - 1, 0), 0)),
                  pl.BlockSpec(ws.shape, full3), pl.BlockSpec(bsb.shape, full3),
                  pl.BlockSpec(gv.shape, full2), pl.BlockSpec(bv.shape, full2), pl.BlockSpec(cw8.shape, full2)],
        out_specs=pl.BlockSpec((tm, DA + DB), lambda i: (i, 0)),
        out_shape=jax.ShapeDtypeStruct((S, DA + DB), BF16),
        compiler_params=pltpu.CompilerParams(dimension_semantics=("arbitrary",),
                                             vmem_limit_bytes=_vmem_limit(est)),
    )(z, z, ws, bsb, gv, bv, cw8)


def _mix_bwd(z, dy, ws, bsb, gv, bv, cw8, tm):
    S, DIN = z.shape
    DA = gv.shape[1]
    DB = DA
    H = DA // HEAD
    o = 2 * DA
    nch = tm // HEAD
    nblk = S // tm

    def body(z_ref, zp_ref, zn_ref, dy_ref, dyn_ref, ws_ref, bsb_ref, gv_ref, bv_ref, cw_ref,
             dz_ref, dws_ref, dbs_ref, dgv_ref, dbv_ref, dcw_ref, dbin_ref):
        i = pl.program_id(0)

        @pl.when(i == 0)
        def _():
            for r in (dws_ref, dbs_ref, dgv_ref, dbv_ref, dcw_ref, dbin_ref):
                r[...] = jnp.zeros_like(r)

        ri = lax.broadcasted_iota(jnp.int32, (HEAD, HEAD), 0)
        ci = lax.broadcasted_iota(jnp.int32, (HEAD, HEAD), 1)
        tril = ri >= ci
        for h in range(H):
            cs = slice(h * HEAD, (h + 1) * HEAD)
            vs = slice(DA + h * HEAD, DA + (h + 1) * HEAD)
            wb = jnp.where(tril, ws_ref[h], 0.0).astype(BF16)
            gvh = gv_ref[:, cs]
            dws_acc = jnp.zeros((HEAD, HEAD), F32)
            dbs_acc = jnp.zeros((HEAD, HEAD), F32)
            dgv_acc = jnp.zeros((1, HEAD), F32)
            dbv_acc = jnp.zeros((1, HEAD), F32)
            dbu_acc = jnp.zeros((1, HEAD), F32)
            dbvv_acc = jnp.zeros((1, HEAD), F32)
            for n in range(nch):
                rs = slice(n * HEAD, (n + 1) * HEAD)
                u, du_dz = _gelu_and_grad(z_ref[rs, cs])
                v, dv_dz = _gelu_and_grad(z_ref[rs, vs])
                vhat, rstd = _ln_stats(v)
                vnb = (vhat * gvh + bv_ref[:, cs]).astype(BF16)
                mixed = jnp.dot(wb, vnb, preferred_element_type=F32) + bsb_ref[h]
                dya = dy_ref[rs, cs]
                dzu = dya * mixed * du_dz
                dmx = dya * u
                dmxb = dmx.astype(BF16)
                dbs_acc += jnp.broadcast_to(jnp.sum(dmx, axis=1, keepdims=True), (HEAD, HEAD))
                dws_acc += lax.dot_general(dmxb, vnb, _DIMS['nt'], preferred_element_type=F32)
                dvn = lax.dot_general(wb, dmxb, _DIMS['tn'], preferred_element_type=F32)
                dgv_acc += jnp.sum(dvn * vhat, axis=0, keepdims=True)
                dbv_acc += jnp.sum(dvn, axis=0, keepdims=True)
                dxh = dvn * gvh
                m1 = jnp.mean(dxh, axis=-1, keepdims=True)
                m2 = jnp.mean(dxh * vhat, axis=-1, keepdims=True)
                dzv = rstd * (dxh - m1 - vhat * m2) * dv_dz
                dz_ref[rs, cs] = dzu.astype(BF16)
                dz_ref[rs, vs] = dzv.astype(BF16)
                dbu_acc += jnp.sum(dzu, axis=0, keepdims=True)
                dbvv_acc += jnp.sum(dzv, axis=0, keepdims=True)
            dws_ref[h] += jnp.where(tril, dws_acc, 0.0)
            dbs_ref[h] += dbs_acc
            dgv_ref[:, cs] += dgv_acc
            dbv_ref[:, cs] += dbv_acc
            dbin_ref[:, cs] += dbu_acc
            dbin_ref[:, vs] += dbvv_acc

        zb = z_ref[:, o:o + DB]
        zc = z_ref[:, o + DB:o + 2 * DB]
        zx = z_ref[:, o + 2 * DB:]
        hc = zc * zx
        keep = (i > 0).astype(F32)
        p1 = zp_ref[7:8, o + DB:o + 2 * DB] * zp_ref[7:8, o + 2 * DB:] * keep
        p2 = zp_ref[6:7, o + DB:o + 2 * DB] * zp_ref[6:7, o + 2 * DB:] * keep
        rowi = lax.broadcasted_iota(jnp.int32, (tm, DB), 0)
        r1, r2 = _conv_shifted(hc, p1, p2, rowi)
        w0, w1, w2 = cw_ref[0:1, :], cw_ref[1:2, :], cw_ref[2:3, :]
        cv = w0 * r2 + w1 * r1 + w2 * hc
        dyb = dy_ref[:, DA:]
        dzb = dyb * cv
        dcv = dyb * zb
        more = (i < nblk - 1).astype(F32)
        n0 = dyn_ref[0:1, DA:] * zn_ref[0:1, o:o + DB] * more
        n1 = dyn_ref[1:2, DA:] * zn_ref[1:2, o:o + DB] * more
        f1 = jnp.where(rowi == tm - 1, n0, pltpu.roll(dcv, tm - 1, 0))
        f2 = jnp.where(rowi == tm - 1, n1, jnp.where(rowi == tm - 2, n0, pltpu.roll(dcv, tm - 2, 0)))
        dhc = w2 * dcv + w1 * f1 + w0 * f2
        dzc = dhc * zx
        dzx = dhc * zc
        dcw_ref[0:1, :] += jnp.sum(dcv * r2, axis=0, keepdims=True)
        dcw_ref[1:2, :] += jnp.sum(dcv * r1, axis=0, keepdims=True)
        dcw_ref[2:3, :] += jnp.sum(dcv * hc, axis=0, keepdims=True)
        dz_ref[:, o:o + DB] = dzb.astype(BF16)
        dz_ref[:, o + DB:o + 2 * DB] = dzc.astype(BF16)
        dz_ref[:, o + 2 * DB:] = dzx.astype(BF16)
        dbin_ref[:, o:o + DB] += jnp.sum(dzb, axis=0, keepdims=True)
        dbin_ref[:, o + DB:o + 2 * DB] += jnp.sum(dzc, axis=0, keepdims=True)
        dbin_ref[:, o + 2 * DB:] += jnp.sum(dzx, axis=0, keepdims=True)

    full3 = lambda i: (0, 0, 0)
    full2 = lambda i: (0, 0)
    prev8 = lambda i: (jnp.maximum(i * (tm // 8) - 1, 0), 0)
    next8 = lambda i: (jnp.minimum((i + 1) * (tm // 8), S // 8 - 1), 0)
    est = 2 * (tm * DIN * 4 + tm * (DA + DB) * 4 + tm * DIN * 2) + 16 * tm * DB * 4
    return pl.pallas_call(
        body, name="mix_bwd", grid=(nblk,),
        in_specs=[pl.BlockSpec((tm, DIN), lambda i: (i, 0)),
                  pl.BlockSpec((8, DIN), prev8), pl.BlockSpec((8, DIN), next8),
                  pl.BlockSpec((tm, DA + DB), lambda i: (i, 0)), pl.BlockSpec((8, DA + DB), next8),
                  pl.BlockSpec(ws.shape, full3), pl.BlockSpec(bsb.shape, full3),
                  pl.BlockSpec(gv.shape, full2), pl.BlockSpec(bv.shape, full2), pl.BlockSpec(cw8.shape, full2)],
        out_specs=[pl.BlockSpec((tm, DIN), lambda i: (i, 0)),
                   pl.BlockSpec(ws.shape, full3), pl.BlockSpec(ws.shape, full3),
                   pl.BlockSpec(gv.shape, full2), pl.BlockSpec(gv.shape, full2),
                   pl.BlockSpec(cw8.shape, full2), pl.BlockSpec((1, DIN), full2)],
        out_shape=[jax.ShapeDtypeStruct((S, DIN), BF16),
                   jax.ShapeDtypeStruct(ws.shape, F32), jax.ShapeDtypeStruct(ws.shape, F32),
                   jax.ShapeDtypeStruct(gv.shape, F32), jax.ShapeDtypeStruct(gv.shape, F32),
                   jax.ShapeDtypeStruct(cw8.shape, F32), jax.ShapeDtypeStruct((1, DIN), F32)],
        compiler_params=pltpu.CompilerParams(dimension_semantics=("arbitrary",),
                                             vmem_limit_bytes=_vmem_limit(est)),
    )(z, z, z, dy, dy, ws, bsb, gv, bv, cw8)


def _place():
    x, y, c = lax.axis_index("x"), lax.axis_index("y"), lax.axis_index("c")
    others = [(1 - x, y), (x, 1 - y), (1 - x, 1 - y)]
    return x, y, c, 2 * x + y, others, [2 * ox + oy for ox, oy in others]


def _half(c, rows_half):
    return pl.ds(pl.multiple_of(c * rows_half, 16), rows_half)


def _rcopy(src, dst, ssem, rsem, dev):
    return pltpu.make_async_remote_copy(src_ref=src, dst_ref=dst, send_sem=ssem, recv_sem=rsem,
                                        device_id=dev, device_id_type=MESH)


class _Comm:
    def __init__(self, aliased, ins, fresh, n_sems, stages, sink):
        self.aliased, self.ins, self.fresh = list(aliased), list(ins), list(fresh)
        self.n_sems, self.stages, self.sink = n_sems, stages, sink


def _comm_operands(comms, n_in, n_out):
    arrays, shapes, scratch, aliases = [], [], [], {}
    for cm in comms:
        for a in cm.aliased:
            aliases[n_in + len(arrays)] = n_out + len(shapes)
            arrays.append(a)
            shapes.append(jax.ShapeDtypeStruct(a.shape, a.dtype))
        arrays += cm.ins
        shapes += cm.fresh
        scratch += [pltpu.SemaphoreType.DMA((cm.n_sems,))] * 2
    return arrays, shapes, scratch, aliases


def _comm_refs(comms, in_refs, out_refs, sem_refs):
    per, pi, po = [], 0, 0
    for i, cm in enumerate(comms):
        pi += len(cm.aliased)
        ins = in_refs[pi:pi + len(cm.ins)]
        pi += len(cm.ins)
        al = out_refs[po:po + len(cm.aliased)]
        po += len(cm.aliased)
        fr = out_refs[po:po + len(cm.fresh)]
        po += len(cm.fresh)
        per.append((al, ins, fr, sem_refs[2 * i], sem_refs[2 * i + 1]))
    return per


def _comm_first_step(comms, hosted, grid):
    if not comms:
        return
    at_first = functools.reduce(lambda a, b: a & b, [pl.program_id(d) == 0 for d in range(len(grid))])

    @pl.when(at_first)
    def _():
        for cm, refs in zip(comms, hosted):
            for d in cm.stages[0](*refs, False):
                d.start()


def _comm_last_step(comms, hosted, grid):
    if not comms:
        return
    at_last = functools.reduce(lambda a, b: a & b, [pl.program_id(d) == g - 1 for d, g in enumerate(grid)])

    @pl.when(at_last)
    def _():
        for cm, refs in zip(comms, hosted):
            for d in cm.stages[0](*refs, True):
                d.wait()
        for cm, refs in zip(comms, hosted):
            for stage in cm.stages[1:]:
                for d in stage(*refs, False):
                    d.start()
                for d in stage(*refs, True):
                    d.wait()


def _deliver(comms, outs):
    pos = 0
    for cm in comms:
        al = outs[pos:pos + len(cm.aliased)]
        pos += len(cm.aliased)
        fr = outs[pos:pos + len(cm.fresh)]
        pos += len(cm.fresh)
        cm.sink(list(al), list(fr))


def _comm_call(name, comms):
    in_arrays, out_shapes, scratch, aliases = _comm_operands(comms, 0, 0)
    n_in, n_out = len(in_arrays), len(out_shapes)

    def body(*refs):
        hosted = _comm_refs(comms, refs[:n_in], refs[n_in:n_in + n_out], refs[n_in + n_out:])
        for cm, r in zip(comms, hosted):
            for stage in cm.stages:
                for d in stage(*r, False):
                    d.start()
                for d in stage(*r, True):
                    d.wait()

    outs = pl.pallas_call(body, name=name, in_specs=[ANY] * n_in, out_specs=[ANY] * n_out,
                          out_shape=out_shapes, scratch_shapes=scratch, input_output_aliases=aliases)(*in_arrays)
    _deliver(comms, outs)


def _c_gather(fulls, sink, ici, fwd, conv=None):
    n = len(fulls)
    n_ici = 3 * (n + (conv is not None)) if ici else 0

    def stage_ici(al, ins, fr, ssem, rsem, wait):
        x, y, c, chip, others, ochips = _place()
        ds = []
        for m in range(n + (conv is not None)):
            for k in range(3):
                i = 3 * m + k
                if m < n:
                    ah = al[m].shape[1] // 2
                    mine, lands = al[m].at[chip, _half(c, ah)], al[m].at[ochips[k], _half(c, ah)]
                else:
                    mine, lands = al[m].at[chip], al[m].at[ochips[k]]
                ds.append(_rcopy(mine, lands, ssem.at[i], rsem.at[i], (x, y, c)) if wait
                          else _rcopy(mine, mine, ssem.at[i], rsem.at[i], (*others[k], c)))
        return ds

    def stage_fwd(al, ins, fr, ssem, rsem, wait):
        x, y, c, chip, others, ochips = _place()
        ds = []
        for m in range(n):
            ah = al[m].shape[1] // 2
            for k in range(3):
                i = n_ici + 3 * m + k
                got = al[m].at[ochips[k], _half(c, ah)]
                ds.append(_rcopy(got, al[m].at[ochips[k], _half(1 - c, ah)], ssem.at[i], rsem.at[i], (x, y, c)) if wait
                          else _rcopy(got, got, ssem.at[i], rsem.at[i], (x, y, 1 - c)))
        return ds

    stages = ([stage_ici] if ici else []) + ([stage_fwd] if fwd else [])
    return _Comm(list(fulls) + ([conv] if conv is not None else []), [], [], n_ici + (3 * n if fwd else 0),
                 stages, sink)


def _c_rs_pair(parts, sink):
    def stage(al, ins, fr, ssem, rsem, wait):
        x, y, c, _, _, _ = _place()
        return [_rcopy(ins[m].at[:, _half(1 - c, ins[m].shape[1] // 2), :], fr[m], ssem.at[m], rsem.at[m],
                       (x, y, 1 - c)) for m in range(len(ins))]
    fresh = [jax.ShapeDtypeStruct((p.shape[0], p.shape[1] // 2, p.shape[2]), p.dtype) for p in parts]
    return _Comm([], parts, fresh, len(parts), [stage], sink)


def _c_rs_chips(psums, sink):
    def stage(al, ins, fr, ssem, rsem, wait):
        x, y, c, _, others, ochips = _place()
        return [_rcopy(ins[m].at[ochips[k]], fr[m].at[k], ssem.at[3 * m + k], rsem.at[3 * m + k], (*others[k], c))
                for m in range(len(ins)) for k in range(3)]
    fresh = [jax.ShapeDtypeStruct((3,) + p.shape[1:], p.dtype) for p in psums]
    return _Comm([], psums, fresh, 3 * len(psums), [stage], sink)


def _c_exchange(g2s, sink):
    def stage(al, ins, fr, ssem, rsem, wait):
        x, y, c, _, _, _ = _place()
        if wait:
            return [_rcopy(al[m].at[c], al[m].at[1 - c], ssem.at[m], rsem.at[m], (x, y, c)) for m in range(len(al))]
        return [_rcopy(al[m].at[c], al[m].at[c], ssem.at[m], rsem.at[m], (x, y, 1 - c)) for m in range(len(al))]
    return _Comm(g2s, [], [], len(g2s), [stage], sink)


def _c_allgather_small(buf, sink):
    def stage(al, ins, fr, ssem, rsem, wait):
        x, y, c, _, _, _ = _place()
        ds = []
        for r in range(1, 8):
            peer = (1 - x if r & 4 else x, 1 - y if r & 2 else y, 1 - c if r & 1 else c)
            ds.append(_rcopy(al[0].at[0], al[0].at[r], ssem.at[r - 1], rsem.at[r - 1], peer))
        return ds
    return _Comm([buf], [], [], 7, [stage], sink)


def _sum_small(buf, dev_idx):
    _, m, n = buf.shape

    def body(me_ref, b_ref, o_ref):
        me = me_ref[0]
        acc = b_ref[me]
        for d in range(1, 8):
            acc = acc + b_ref[jnp.bitwise_xor(me, d)]
        o_ref[...] = acc

    gs = pltpu.PrefetchScalarGridSpec(
        num_scalar_prefetch=1, grid=(1,),
        in_specs=[pl.BlockSpec(buf.shape, lambda i, me: (0, 0, 0))],
        out_specs=pl.BlockSpec((m, n), lambda i, me: (0, 0)))
    return pl.pallas_call(body, name="sum_small", grid_spec=gs,
                          out_shape=jax.ShapeDtypeStruct((m, n), F32))(dev_idx, buf)


def _sum_pair(name, part, sib, c_idx):
    _, A, B = part.shape
    ah = A // 2
    br = _rows(ah, B * 2, 2 * 2 ** 20)

    def body(c_ref, a_ref, b_ref, o_ref):
        o_ref[...] = (a_ref[...].astype(F32) + b_ref[...].astype(F32)).astype(BF16)

    gs = pltpu.PrefetchScalarGridSpec(
        num_scalar_prefetch=1, grid=(N_CHIPS, ah // br),
        in_specs=[pl.BlockSpec((None, None, br, B), lambda j, i, c: (j, c[0], i, 0)),
                  pl.BlockSpec((None, br, B), lambda j, i, c: (j, i, 0))],
        out_specs=pl.BlockSpec((None, br, B), lambda j, i, c: (j, i, 0)))
    return pl.pallas_call(body, name=name, grid_spec=gs,
                          out_shape=jax.ShapeDtypeStruct((N_CHIPS, ah, B), BF16))(
                              c_idx, part.reshape(N_CHIPS, 2, ah, B), sib)


def _sum_chips(name, psum, recv, place_idx):
    _, ah, B = psum.shape
    br = _rows(ah, B * 4, 2 * 2 ** 20)

    def body(c_ref, p_ref, r0_ref, r1_ref, r2_ref, o_ref):
        o_ref[...] = ((p_ref[...].astype(F32) + r0_ref[...].astype(F32)) + r1_ref[...].astype(F32)) \
            + r2_ref[...].astype(F32)

    gs = pltpu.PrefetchScalarGridSpec(
        num_scalar_prefetch=1, grid=(ah // br,),
        in_specs=[pl.BlockSpec((None, br, B), lambda i, c: (c[0], i, 0))]
        + [pl.BlockSpec((None, br, B), functools.partial(lambda i, c, k: (k, i, 0), k=k)) for k in range(3)],
        out_specs=pl.BlockSpec((None, br, B), lambda i, c: (c[1], i, 0)))
    return pl.pallas_call(body, name=name, grid_spec=gs,
                          out_shape=jax.ShapeDtypeStruct((2, ah, B), F32))(place_idx, psum, recv, recv, recv)


def _cast_into(name, a, place_idx):
    A, B = a.shape
    br = _rows(A, B * 4)

    def body(c_ref, a_ref, o_ref):
        o_ref[...] = a_ref[...].astype(BF16)

    gs = pltpu.PrefetchScalarGridSpec(
        num_scalar_prefetch=1, grid=(A // br,),
        in_specs=[pl.BlockSpec((br, B), lambda i, c: (i, 0))],
        out_specs=pl.BlockSpec((None, br, B), lambda i, c: (c[0], i, 0)))
    return pl.pallas_call(body, name=name, grid_spec=gs,
                          out_shape=jax.ShapeDtypeStruct((N_CHIPS, A, B), BF16))(place_idx, a)


def _cast_behind(x, shards, place_idx, comms):
    n_steps = 16
    n = len(shards)
    xv = x
    svs = list(shards)

    def pace(v):
        for steps_per_block in (1, 2, 4, 8, 16):
            blocks = n_steps // steps_per_block
            if v.shape[0] % blocks == 0 and (v.shape[0] // blocks) % 16 == 0:
                return steps_per_block
        raise ValueError(v.shape)

    n_in, n_out = 2 + n, 1 + n
    cm_in_arrays, cm_out_shapes, cm_scratch, aliases = _comm_operands(comms, n_in, n_out)
    n_cm_in, n_cm_out = len(cm_in_arrays), len(cm_out_shapes)
    grid = (n_steps,)

    def body(c_ref, x_ref, *rest):
        s_refs = rest[:n]
        xo_ref = rest[n + n_cm_in]
        so_refs = rest[n + n_cm_in + 1:n + n_cm_in + 1 + n]
        o_end = n + n_cm_in + 1 + n
        hosted = _comm_refs(comms, rest[n:n + n_cm_in], rest[o_end:o_end + n_cm_out], rest[o_end + n_cm_out:])
        _comm_first_step(comms, hosted, grid)
        xo_ref[...] = x_ref[...].astype(BF16)
        for s_ref, o_ref in zip(s_refs, so_refs):
            o_ref[...] = s_ref[...].astype(BF16)
        _comm_last_step(comms, hosted, grid)

    def rows(v):
        return v.shape[0] * pace(v) // n_steps

    def in_spec(v):
        return pl.BlockSpec((rows(v), v.shape[1]), functools.partial(lambda i, c, q: (i // q, 0), q=pace(v)))

    def shard_out_spec(v):
        return pl.BlockSpec((None, rows(v), v.shape[1]),
                            functools.partial(lambda i, c, q: (c[0], i // q, 0), q=pace(v)))

    gs = pltpu.PrefetchScalarGridSpec(
        num_scalar_prefetch=1, grid=grid,
        in_specs=[in_spec(v) for v in [xv] + svs] + [ANY] * n_cm_in,
        out_specs=[in_spec(xv)] + [shard_out_spec(v) for v in svs] + [ANY] * n_cm_out,
        scratch_shapes=cm_scratch)
    outs = pl.pallas_call(
        body, name="cast_behind", grid_spec=gs,
        out_shape=[jax.ShapeDtypeStruct(xv.shape, BF16)]
        + [jax.ShapeDtypeStruct((N_CHIPS,) + v.shape, BF16) for v in svs] + cm_out_shapes,
        input_output_aliases=aliases,
        compiler_params=pltpu.CompilerParams(
            dimension_semantics=("arbitrary",),
            vmem_limit_bytes=_vmem_limit(2 * sum(rows(v) * v.shape[1] * 6 for v in [xv] + svs))),
    )(place_idx, xv, *svs, *cm_in_arrays)
    _deliver(comms, outs[n_out:])
    return outs[0].reshape(x.shape), [o.reshape((N_CHIPS,) + s.shape) for o, s in zip(outs[1:n_out], shards)]


def _adamw_small(gsum, gconv, offsets, ws, ms, vs):
    n = len(ws)

    def body(*refs):
        g_ref, gc_ref = refs[0], refs[1]
        w_refs = refs[2:2 + n]
        m_refs = refs[2 + n:2 + 2 * n]
        v_refs = refs[2 + 2 * n:2 + 3 * n]
        outs = refs[2 + 3 * n:]
        for i in range(n):
            rows = w_refs[i].shape[0]
            g = gc_ref[...] if offsets[i] is None else g_ref[offsets[i]:offsets[i] + rows, :]
            d, m2, v2 = _adamw_math(w_refs[i][...], g, m_refs[i][...], v_refs[i][...])
            outs[4 * i][...] = g
            outs[4 * i + 1][...] = d
            outs[4 * i + 2][...] = m2
            outs[4 * i + 3][...] = v2

    vm = pl.BlockSpec(memory_space=pltpu.VMEM)
    out_shape = []
    for w in ws:
        out_shape += [jax.ShapeDtypeStruct(w.shape, F32)] * 4
    return pl.pallas_call(body, name="adamw_small", in_specs=[vm] * (2 + 3 * n), out_specs=[vm] * (4 * n),
                          out_shape=out_shape)(gsum, gconv, *ws, *ms, *vs)


def _ident(accs, tes, res):
    return [accs[0]]


def _ffn_up(name, hb, wg4, wu4, comms=()):
    S, D = hb.shape

    def epi(accs, tes, res):
        g, u = accs
        return [g, u, g * _sigmoid(g) * u]

    return _mm(name, 'nn', [hb], [wg4, wu4], [(0, 0, 0), (0, 1, 1)], 2, epi, [BF16] * 3,
               bm=_blk(S, 512), bn=wg4.shape[2], bk=D, b_blocked=True, n_outer=True, comms=comms)


def _ffn_gate(name, hb, wg4, comms=()):
    S, D = hb.shape
    return _mm(name, 'nn', [hb], [wg4], [(0, 0, 0)], 1, _ident, [BF16],
               bm=_blk(S, 1024), bn=wg4.shape[2], bk=D, b_blocked=True, n_outer=True, comms=comms)[0]


def _ffn_upmul(name, hb, wu4, gb, comms=()):
    S, D = hb.shape

    def epi(accs, tes, res):
        g = tes[0].astype(F32)
        return [accs[0], g * _sigmoid(g) * accs[0]]

    return _mm(name, 'nn', [hb], [wu4], [(0, 0, 0)], 1, epi, [BF16, BF16],
               bm=_blk(S, 1024), bn=wu4.shape[2], bk=D, tile_extras=[gb], b_blocked=True, n_outer=True, comms=comms)


def _ffn_down(name, ab, wd, hres, comms=()):
    S, F = ab.shape
    D = wd.shape[1]
    return _mm(name, 'nn', [ab], [wd], [(0, 0, 0)], 1, lambda accs, tes, res: [ALPHA * tes[0] + 0.5 * accs[0]],
               [F32], bm=_blk(S, 1024), bn=_blk(D, 512), bk=F, tile_extras=[hres], comms=comms)[0]


def _ffn_da(name, dsub, wd, gb, ub, comms=()):
    S, D = dsub.shape
    F = wd.shape[0]

    def epi(accs, tes, res):
        da = accs[0]
        g = tes[0].astype(F32)
        u = tes[1].astype(F32)
        s = _sigmoid(g)
        silu = g * s
        return [(da * u) * (s + silu * (1.0 - s)), da * silu]

    return _mm(name, 'nt', [dsub], [wd], [(0, 0, 0)], 1, epi, [BF16, BF16],
               bm=_blk(S, 1024), bn=_blk(F, 512), bk=D, tile_extras=[gb, ub], comms=comms)


def _dw_rows(name, ab, dsub, cs, comms=()):
    S, D = dsub.shape
    return _mm(name, 'tn', [ab], [dsub], [(0, 0, 0)], 1, _ident, [BF16],
               bm=cs, bn=_blk(D, 1024), bk=_blk(S, 2048), comms=comms)[0].reshape(N_CHIPS, cs, D)


def _dw_cols(name, hb, dxb, cs, comms=()):
    S, D = hb.shape
    return _mm(name, 'tn', [hb], [dxb], [(0, 0, 0)], 1, _ident, [BF16],
               bm=_blk(D, 1024), bn=cs, bk=_blk(S, 2048), out_blocked=True, comms=comms)[0]


def _ffn_dh(name, dgb, dub, wg4, wu4, dres, comms=()):
    S = dgb.shape[0]
    D = wg4.shape[1]
    return _mm(name, 'nt', [dgb, dub], [wg4, wu4], [(0, 0, 0), (1, 1, 0)], 1,
               lambda accs, tes, res: [accs[0] + tes[0]], [F32],
               bm=_blk(S, 1024), bn=_blk(D, 1024), bk=wg4.shape[2], tile_extras=[dres], b_blocked=True,
               comms=comms)[0]


def _step(p):
    x = p['x'][0]
    target = p['loss_target'][0]
    S, D = x.shape
    cx, cy, cc = lax.axis_index("x"), lax.axis_index("y"), lax.axis_index("c")
    chip = 2 * cx + cy
    c_idx = jnp.reshape(cc, (1,)).astype(jnp.int32)
    place_idx = jnp.stack([chip, cc]).astype(jnp.int32)

    w_s = p['w_s'][0]
    H = w_s.shape[0]
    DA = H * HEAD
    bsb = jnp.broadcast_to(p['b_s'][0][:, :, None], (H, HEAD, HEAD))
    conv_shard = p['conv_w'][0]
    conv8 = jnp.zeros((8, conv_shard.shape[1]), F32).at[:CONV_TAPS].set(conv_shard)

    W = {'ffa_gate': _cast_into("cast_ffa_gate", p['ffa_gate'][0], place_idx)}
    conv = {'full': lax.dynamic_update_slice(jnp.zeros((N_CHIPS,) + conv8.shape, F32), conv8[None], (chip, 0, 0))}

    def gathered(names, with_conv=False):
        def sink(al, fr):
            W.update(zip(names, al))
            if with_conv:
                conv['full'] = al[len(names)]
        return sink

    def c_gather(names, ici, fwd, with_conv=False):
        return _c_gather([W[n] for n in names], gathered(names, with_conv), ici, fwd,
                         conv['full'] if with_conv else None)

    rest = [n for n in BIG if n != 'ffa_gate']
    xb, casted = _cast_behind(x, [p[n][0] for n in rest], place_idx, comms=[c_gather(['ffa_gate'], True, True)])
    W.update(zip(rest, casted))
    cs_ff = W['ffa_gate'].shape[2]
    F = N_CHIPS * cs_ff

    ga = _ffn_gate("ffa_gate", xb, W['ffa_gate'],
                   comms=[c_gather(['ffa_up'], True, True), c_gather(['w_out'], True, False)])
    ua, aa = _ffn_upmul("ffa_up", xb, W['ffa_up'], ga,
                        comms=[c_gather(['ffa_down'], True, True), c_gather(['w_out'], False, True)])
    wd_a = W['ffa_down'].reshape(F, D)
    r1 = _ffn_down("ffa_down", aa, wd_a, x,
                   comms=[c_gather(['w_in'], True, True, with_conv=True), c_gather(['ffc_up'], True, False)])
    cw8 = jnp.transpose(conv['full'], (1, 0, 2)).reshape(8, DA)
    h1, h1b = _ln_fwd("ln_a", r1, p['ln_a_g'], p['ln_a_b'])
    w_in4 = W['w_in']
    cs_in = w_in4.shape[2]
    z = _mm("w_in", 'nn', [h1b], [w_in4], [(0, 0, 0)], 1, lambda accs, tes, res: [accs[0] + res[0]], [F32],
            bm=_blk(S, 1024), bn=cs_in, bk=D, row_extras=[p['b_in']], b_blocked=True, n_outer=True,
            comms=[c_gather(['ffc_up'], False, True), c_gather(['ffc_gate'], True, False)])[0]
    w_out = W['w_out'].reshape(DA * 2, D)
    tm = _blk(S, 256)
    yb = _mix_fwd(z, w_s, bsb, p['ln_v_g'], p['ln_v_b'], cw8, tm)
    r2 = _mm("w_out", 'nn', [yb], [w_out], [(0, 0, 0)], 1,
             lambda accs, tes, res: [accs[0] + res[0] + ALPHA * tes[0]], [F32],
             bm=_blk(S, 1024), bn=_blk(D, 1024), bk=2 * DA, tile_extras=[h1], row_extras=[p['b_out']],
             comms=[c_gather(['ffc_gate'], False, True)])[0]
    h2, h2b = _ln_fwd("ln_m", r2, p['ln_m_g'], p['ln_m_b'])
    gc, uc, ac = _ffn_up("ffc_up", h2b, W['ffc_gate'], W['ffc_up'], comms=[c_gather(['ffc_down'], True, True)])
    wd_c = W['ffc_down'].reshape(F, D)
    r3 = _ffn_down("ffc_down", ac, wd_c, h2)

    pg, sg, sib, psum, recv, g2 = {}, {}, {}, {}, {}, {}

    def c_pair(names):
        return _c_rs_pair([pg[n] for n in names], lambda al, fr: sib.update(zip(names, fr)))

    def c_chips(names):
        return _c_rs_chips([psum[n] for n in names], lambda al, fr: recv.update(zip(names, fr)))

    def c_swap(names):
        return _c_exchange([g2[n] for n in names], lambda al, fr: g2.update(zip(names, al)))

    def sum_pair(*names):
        for n in names:
            psum[n] = _sum_pair(f"sum_pair_{n}", pg[n], sib[n], c_idx)

    def sum_chips(*names):
        for n in names:
            g2[n] = _sum_chips(f"sum_chips_{n}", psum[n], recv[n], place_idx)

    dres3, dsub3, sg['ln_c_g'], sg['ln_c_b'], lsum = _ln_loss_bwd(
        "ln_c_loss_bwd", r3, target, p['ln_c_g'], p['ln_c_b'], 0.5)

    dgc, duc = _ffn_da("ffc_da", dsub3, wd_c, gc, uc)
    pg['ffc_down'] = _dw_rows("ffc_dwd", ac, dsub3, cs_ff)
    pg['ffc_gate'] = _dw_cols("ffc_dwg", h2b, dgc, cs_ff, comms=[c_pair(['ffc_down'])])
    sum_pair('ffc_down')
    pg['ffc_up'] = _dw_cols("ffc_dwu", h2b, duc, cs_ff, comms=[c_pair(['ffc_gate']), c_chips(['ffc_down'])])
    sum_pair('ffc_gate')
    dh2 = _ffn_dh("ffc_dh", dgc, duc, W['ffc_gate'], W['ffc_up'], dres3,
                  comms=[c_pair(['ffc_up']), c_chips(['ffc_gate'])])
    sum_pair('ffc_up')
    sum_chips('ffc_down', 'ffc_gate')

    dres2, dmix, sg['ln_m_g'], sg['ln_m_b'], sg['b_out'] = _ln_bwd("ln_m_bwd", r2, dh2, p['ln_m_g'], 1.0)
    pg['w_out'] = _mm("dw_out", 'tn', [yb], [dmix], [(0, 0, 0)], 1, _ident, [BF16],
                      bm=_blk(2 * DA, 1024), bn=_blk(D, 1024), bk=_blk(S, 2048),
                      comms=[c_swap(['ffc_down', 'ffc_gate'])])[0].reshape(N_CHIPS, 2 * DA // 4, D)
    dy = _mm("dy", 'nt', [dmix], [w_out], [(0, 0, 0)], 1, _ident, [F32],
             bm=_blk(S, 1024), bn=_blk(2 * DA, 1024), bk=D, comms=[c_pair(['w_out'])])[0]
    sum_pair('w_out')
    dz, dws, dbs, sg['ln_v_g'], sg['ln_v_b'], dcw, sg['b_in'] = _mix_bwd(
        z, dy, w_s, bsb, p['ln_v_g'], p['ln_v_b'], cw8, tm)
    sg['w_s'] = dws
    sg['b_s'] = dbs[:, :, 0]
    pg['w_in'] = _dw_cols("dw_in", h1b, dz, cs_in, comms=[c_chips(['ffc_up']), c_chips(['w_out'])])
    sum_chips('ffc_up', 'w_out')
    dh1 = _mm("dh1", 'nt', [dz], [w_in4], [(0, 0, 0)], 1, lambda accs, tes, res: [accs[0] + tes[0]], [F32],
              bm=_blk(S, 1024), bn=_blk(D, 1024), bk=cs_in, tile_extras=[dres2], b_blocked=True,
              comms=[c_pair(['w_in']), c_swap(['ffc_up', 'w_out'])])[0]
    sum_pair('w_in')

    dres1, dsub1, sg['ln_a_g'], sg['ln_a_b'], _ = _ln_bwd("ln_a_bwd", r1, dh1, p['ln_a_g'], 0.5)
    def as_rows(a):
        a = a.reshape(-1, 128)
        pad = (-a.shape[0]) % 8
        return jnp.pad(a, ((0, pad), (0, 0))) if pad else a

    pieces, offsets, off = [], {}, 0
    for n in SMALL + ['conv_w']:
        piece = as_rows(dcw[:CONV_TAPS] if n == 'conv_w' else sg[n])
        offsets[n] = off
        off += piece.shape[0]
        pieces.append(piece)
    packed = jnp.concatenate(pieces, axis=0)
    small = {'buf': jnp.zeros((8,) + packed.shape, F32).at[0].set(packed)}
    dga, dua = _ffn_da("ffa_da", dsub1, wd_a, ga, ua,
                       comms=[c_chips(['w_in']), _c_allgather_small(small['buf'], lambda al, fr: small.update(buf=al[0]))])
    sum_chips('w_in')
    gsum = _sum_small(small['buf'], jnp.reshape(4 * cx + 2 * cy + cc, (1,)).astype(jnp.int32))
    pg['ffa_down'] = _dw_rows("ffa_dwd", aa, dsub1, cs_ff, comms=[c_swap(['w_in'])])
    pg['ffa_gate'] = _dw_cols("ffa_dwg", xb, dga, cs_ff, comms=[c_pair(['ffa_down'])])
    sum_pair('ffa_down')
    pg['ffa_up'] = _dw_cols("ffa_dwu", xb, dua, cs_ff, comms=[c_pair(['ffa_gate']), c_chips(['ffa_down'])])
    sum_pair('ffa_gate')
    _comm_call("pair_ffa_up", [c_pair(['ffa_up'])])
    sum_pair('ffa_up')
    sum_chips('ffa_down')
    dx = _ffn_dh("ffa_dh", dga, dua, W['ffa_gate'], W['ffa_up'], dres1,
                 comms=[c_chips(['ffa_gate']), c_chips(['ffa_up']), c_swap(['ffa_down'])])
    sum_chips('ffa_gate', 'ffa_up')
    _comm_call("swap_ffa", [c_swap(['ffa_gate', 'ffa_up'])])

    out = {}
    for n in BIG:
        g, d, m2, v2 = _adamw(f"adamw_{n}", p[n][0], g2[n].reshape(p[n][0].shape), p['m_' + n][0], p['v_' + n][0])
        shp = p[n].shape
        out[n] = (g.reshape(shp), d.reshape(shp), m2.reshape(shp), v2.reshape(shp))

    loss = lax.psum(0.5 * jnp.sum(lsum) / D, ("x", "y", "c"))
    n_conv = CONV_TAPS * DA // 128
    conv_sum = gsum[offsets['conv_w']:offsets['conv_w'] + n_conv].reshape(CONV_TAPS, DA)
    cw = conv_shard.shape[1]
    gconv = lax.dynamic_slice_in_dim(conv_sum, (2 * cx + cy) * cw, cw, axis=1).reshape(-1, 128)
    names = SMALL + ['conv_w']
    rows128 = lambda a: a.reshape(-1, 128)
    res = _adamw_small(gsum, gconv, [offsets[n] if n != 'conv_w' else None for n in names],
                       [rows128(p[n]) for n in names], [rows128(p['m_' + n]) for n in names],
                       [rows128(p['v_' + n]) for n in names])
    for i, n in enumerate(names):
        out[n] = tuple(r.reshape(p[n].shape) for r in res[4 * i:4 * i + 4])

    return (loss, dx.reshape(p['x'].shape), *[out[n][0] for n in WEIGHTS], *[out[n][1] for n in WEIGHTS],
            *[out[n][2] for n in WEIGHTS], *[out[n][3] for n in WEIGHTS])


def kernel(x, ffa_gate, ffa_up, ffa_down, ln_a_g, ln_a_b, w_in, b_in, w_s, b_s, ln_v_g, ln_v_b, conv_w, w_out, b_out, ln_m_g, ln_m_b, ffc_gate, ffc_up, ffc_down, ln_c_g, ln_c_b, loss_target, m_ffa_gate, m_ffa_up, m_ffa_down, m_ln_a_g, m_ln_a_b, m_w_in, m_b_in, m_w_s, m_b_s, m_ln_v_g, m_ln_v_b, m_conv_w, m_w_out, m_b_out, m_ln_m_g, m_ln_m_b, m_ffc_gate, m_ffc_up, m_ffc_down, m_ln_c_g, m_ln_c_b, v_ffa_gate, v_ffa_up, v_ffa_down, v_ln_a_g, v_ln_a_b, v_w_in, v_b_in, v_w_s, v_b_s, v_ln_v_g, v_ln_v_b, v_conv_w, v_w_out, v_b_out, v_ln_m_g, v_ln_m_b, v_ffc_gate, v_ffc_up, v_ffc_down, v_ln_c_g, v_ln_c_b):
    return _step(dict(locals()))
```

```python
import functools
import math

import jax
import jax.numpy as jnp
from jax import lax
from jax.experimental import pallas as pl
from jax.experimental.pallas import tpu as pltpu

F32 = jnp.float32
BF16 = jnp.bfloat16
MESH = pl.DeviceIdType.MESH
ANY = pl.BlockSpec(memory_space=pl.ANY)

HEAD = 128
CONV_TAPS = 3
LN_EPS = 1e-5
ALPHA = float(2.0 ** 0.25)
ADAM_LR, ADAM_B1, ADAM_B2, ADAM_EPS, ADAM_WD, ADAM_STEP = 0.001, 0.9, 0.999, 1e-08, 0.01, 10

V7X_VMEM_BYTES = 64 * 2 ** 20
N_CHIPS = 4

BIG = ['ffa_gate', 'ffa_up', 'ffa_down', 'w_in', 'w_out', 'ffc_gate', 'ffc_up', 'ffc_down']
WEIGHTS = ['ffa_gate', 'ffa_up', 'ffa_down', 'ln_a_g', 'ln_a_b', 'w_in', 'b_in', 'w_s', 'b_s', 'ln_v_g', 'ln_v_b',
           'conv_w', 'w_out', 'b_out', 'ln_m_g', 'ln_m_b', 'ffc_gate', 'ffc_up', 'ffc_down', 'ln_c_g', 'ln_c_b']
SMALL = [n for n in WEIGHTS if n not in BIG and n != 'conv_w']


def _vmem_limit(estimate_bytes):
    return int(min(max(estimate_bytes * 1.3, 32 * 2 ** 20), V7X_VMEM_BYTES - 6 * 2 ** 20))


def _blk(dim, pref, mult=128):
    if dim <= pref:
        return dim
    best = None
    for d in range(mult, pref + 1, mult):
        if dim % d == 0:
            best = d
    assert best is not None, (dim, pref)
    return best


def _rows(n_rows, row_bytes, target=4 * 2 ** 20):
    return _blk(n_rows, max(16, target // row_bytes), 16)


def _sigmoid(x):
    return 0.5 * jnp.tanh(0.5 * x) + 0.5


_GELU_C = math.sqrt(2.0 / math.pi)
_GELU_A = 0.044715


def _gelu(x):
    return (0.5 * x) * (1.0 + jnp.tanh(x * (_GELU_C + (_GELU_C * _GELU_A) * (x * x))))


def _gelu_and_grad(x):
    x2 = x * x
    t = jnp.tanh(x * (_GELU_C + (_GELU_C * _GELU_A) * x2))
    half_x = 0.5 * x
    one_plus_t = 1.0 + t
    dg = 0.5 * one_plus_t + half_x * (1.0 - t * t) * (_GELU_C + (3.0 * _GELU_C * _GELU_A) * x2)
    return half_x * one_plus_t, dg


def _ln_stats(r):
    mu = jnp.mean(r, axis=-1, keepdims=True)
    d = r - mu
    var = jnp.mean(d * d, axis=-1, keepdims=True)
    rstd = lax.rsqrt(var + LN_EPS)
    return d * rstd, rstd


_DIMS = {'nn': (((1,), (0,)), ((), ())), 'nt': (((1,), (1,)), ((), ())), 'tn': (((0,), (0,)), ((), ()))}


def _mm(name, mode, a_list, b_list, pairs, n_acc, epilogue, out_dtypes, bm, bn, bk,
        tile_extras=(), row_extras=(), b_blocked=False, out_blocked=False, n_outer=False, comms=()):
    a0, b0 = a_list[0], b_list[0]
    if mode == 'nn':
        M, K = a0.shape
        N = b0.shape[0] * b0.shape[2] if b_blocked else b0.shape[1]
    elif mode == 'nt':
        M, K = a0.shape
        N = b0.shape[1] if b_blocked else b0.shape[0]
    else:
        K, M = a0.shape
        N = b0.shape[1]
        assert not b_blocked
    assert M % bm == 0 and N % bn == 0 and K % bk == 0, (name, M, N, K, bm, bn, bk)
    gm, gn, gk = M // bm, N // bn, K // bk
    if b_blocked:
        assert (bn if mode == 'nn' else bk) == b0.shape[2], name
    dims = _DIMS[mode]

    def spec(shape, f):
        if n_outer:
            return pl.BlockSpec(shape, lambda g0, g1, g2: f(g1, g0, g2))
        return pl.BlockSpec(shape, f)

    if mode == 'tn':
        a_spec = spec((bk, bm), lambda i, j, k: (k, i))
    else:
        a_spec = spec((bm, bk), lambda i, j, k: (i, k))
    if mode == 'nn':
        b_spec = (spec((None, bk, bn), lambda i, j, k: (j, k, 0)) if b_blocked
                  else spec((bk, bn), lambda i, j, k: (k, j)))
    elif mode == 'nt':
        b_spec = (spec((None, bn, bk), lambda i, j, k: (k, j, 0)) if b_blocked
                  else spec((bn, bk), lambda i, j, k: (j, k)))
    else:
        b_spec = spec((bk, bn), lambda i, j, k: (k, j))
    te_spec = spec((bm, bn), lambda i, j, k: (i, j))
    re_spec = spec((1, bn), lambda i, j, k: (0, j))
    if out_blocked:
        o_spec = spec((None, bm, bn), lambda i, j, k: (j, i, 0))
        o_shape = (gn, M, bn)
    else:
        o_spec = te_spec
        o_shape = (M, N)

    na, nb, nte, nre, no = len(a_list), len(b_list), len(tile_extras), len(row_extras), len(out_dtypes)

    n_scratch_acc = n_acc if gk > 1 else 0
    grid = (gn, gm, gk) if n_outer else (gm, gn, gk)
    cm_in_arrays, cm_out_shapes, cm_scratch, aliases = _comm_operands(comms, na + nb + nte + nre, no)
    n_cm_in, n_cm_out = len(cm_in_arrays), len(cm_out_shapes)

    def body(*refs):
        a_refs = refs[:na]
        b_refs = refs[na:na + nb]
        te_refs = refs[na + nb:na + nb + nte]
        re_refs = refs[na + nb + nte:na + nb + nte + nre]
        n_in = na + nb + nte + nre
        o_refs = refs[n_in + n_cm_in:n_in + n_cm_in + no]
        acc_refs = refs[n_in + n_cm_in + no + n_cm_out:n_in + n_cm_in + no + n_cm_out + n_scratch_acc]
        hosted = _comm_refs(comms, refs[n_in:n_in + n_cm_in], refs[n_in + n_cm_in + no:n_in + n_cm_in + no + n_cm_out],
                            refs[n_in + n_cm_in + no + n_cm_out + n_scratch_acc:])
        _comm_first_step(comms, hosted, grid)

        def finish(accs):
            outs = epilogue(accs, [r[...] for r in te_refs], [r[...] for r in re_refs])
            for o_ref, o in zip(o_refs, outs):
                o_ref[...] = o.astype(o_ref.dtype)

        def products():
            parts = [None] * n_acc
            for ai, bi, ci in pairs:
                d = lax.dot_general(a_refs[ai][...], b_refs[bi][...], dims, preferred_element_type=F32)
                parts[ci] = d if parts[ci] is None else parts[ci] + d
            return parts

        if gk == 1:
            finish(products())
        else:
            kk = pl.program_id(2)

            @pl.when(kk == 0)
            def _():
                for r, part in zip(acc_refs, products()):
                    r[...] = part

            if gk > 2:
                @pl.when((kk > 0) & (kk < gk - 1))
                def _():
                    for r, part in zip(acc_refs, products()):
                        r[...] += part

            @pl.when(kk == gk - 1)
            def _():
                finish([r[...] + part for r, part in zip(acc_refs, products())])

        _comm_last_step(comms, hosted, grid)

    isz = lambda a: jnp.dtype(a.dtype).itemsize
    est = 2 * (sum(bm * bk * isz(a) for a in a_list) + sum(bk * bn * isz(b) for b in b_list)
               + sum(bm * bn * isz(t) for t in tile_extras)
               + sum(bm * bn * jnp.dtype(d).itemsize for d in out_dtypes))
    est += (2 * n_acc + 2) * bm * bn * 4
    outs = pl.pallas_call(
        body, name=name, grid=grid,
        in_specs=[a_spec] * na + [b_spec] * nb + [te_spec] * nte + [re_spec] * nre + [ANY] * len(cm_in_arrays),
        out_specs=[o_spec] * no + [ANY] * len(cm_out_shapes),
        out_shape=[jax.ShapeDtypeStruct(o_shape, d) for d in out_dtypes] + cm_out_shapes,
        scratch_shapes=[pltpu.VMEM((bm, bn), F32)] * n_scratch_acc + cm_scratch,
        input_output_aliases=aliases,
        compiler_params=pltpu.CompilerParams(dimension_semantics=("parallel", "parallel", "arbitrary"),
                                             vmem_limit_bytes=_vmem_limit(est)),
    )(*a_list, *b_list, *tile_extras, *row_extras, *cm_in_arrays)
    _deliver(comms, outs[no:])
    return outs[:no]


def _rowwise(name, fn, br, row_ins, vec_ins, row_outs, acc_outs=()):
    S = row_ins[0].shape[0]
    assert S % br == 0, (name, S, br)
    nr, nv, no, nacc = len(row_ins), len(vec_ins), len(row_outs), len(acc_outs)

    def body(*refs):
        rin = refs[:nr]
        vin = refs[nr:nr + nv]
        rout = refs[nr + nv:nr + nv + no]
        aout = refs[nr + nv + no:]
        if nacc:
            @pl.when(pl.program_id(0) == 0)
            def _():
                for a in aout:
                    a[...] = jnp.zeros_like(a)
        fn(rin, vin, rout, aout)

    def full_spec(shape):
        nd = len(shape)
        return pl.BlockSpec(shape, lambda i: (0,) * nd)

    est = 2 * (sum(br * a.shape[1] * jnp.dtype(a.dtype).itemsize for a in row_ins)
               + sum(br * c * jnp.dtype(d).itemsize for c, d in row_outs))
    est += 6 * br * max(a.shape[1] for a in row_ins) * 4
    return pl.pallas_call(
        body, name=name, grid=(S // br,),
        in_specs=[pl.BlockSpec((br, a.shape[1]), lambda i: (i, 0)) for a in row_ins]
        + [full_spec(v.shape) for v in vec_ins],
        out_specs=[pl.BlockSpec((br, c), lambda i: (i, 0)) for c, _ in row_outs]
        + [full_spec(s) for s, _ in acc_outs],
        out_shape=[jax.ShapeDtypeStruct((S, c), d) for c, d in row_outs]
        + [jax.ShapeDtypeStruct(s, d) for s, d in acc_outs],
        compiler_params=pltpu.CompilerParams(dimension_semantics=("arbitrary",),
                                             vmem_limit_bytes=_vmem_limit(est)),
    )(*row_ins, *vec_ins)


def _ln_fwd(name, r, g, b):
    def fn(rin, vin, rout, aout):
        xhat, _ = _ln_stats(rin[0][...])
        h = xhat * vin[0][...] + vin[1][...]
        rout[0][...] = h
        rout[1][...] = h.astype(BF16)
    D = r.shape[1]
    return _rowwise(name, fn, _rows(r.shape[0], D * 4), [r], [g, b], [(D, F32), (D, BF16)])


def _ln_loss_bwd(name, r, target, g, b, sub_scale):
    D = r.shape[1]

    def fn(rin, vin, rout, aout):
        xhat, rstd = _ln_stats(rin[0][...])
        gain = vin[0][...]
        err = xhat * gain + vin[1][...] - rin[1][...]
        dhv = err * (1.0 / D)
        dxh = dhv * gain
        m1 = jnp.mean(dxh, axis=-1, keepdims=True)
        m2 = jnp.mean(dxh * xhat, axis=-1, keepdims=True)
        dr = rstd * (dxh - m1 - xhat * m2)
        rout[0][...] = ALPHA * dr
        rout[1][...] = (sub_scale * dr).astype(BF16)
        aout[0][...] += jnp.sum(dhv * xhat, axis=0, keepdims=True)
        aout[1][...] += jnp.sum(dhv, axis=0, keepdims=True)
        aout[2][...] += jnp.sum(err * err, axis=0, keepdims=True)
    return _rowwise(name, fn, _rows(r.shape[0], D * 4, 2 * 2 ** 20), [r, target], [g, b],
                    [(D, F32), (D, BF16)], [((1, D), F32)] * 3)


def _ln_bwd(name, r, dh, g, sub_scale):
    D = r.shape[1]

    def fn(rin, vin, rout, aout):
        xhat, rstd = _ln_stats(rin[0][...])
        dhv = rin[1][...]
        dxh = dhv * vin[0][...]
        m1 = jnp.mean(dxh, axis=-1, keepdims=True)
        m2 = jnp.mean(dxh * xhat, axis=-1, keepdims=True)
        dr = rstd * (dxh - m1 - xhat * m2)
        rout[0][...] = ALPHA * dr
        rout[1][...] = (sub_scale * dr).astype(BF16)
        aout[0][...] += jnp.sum(dhv * xhat, axis=0, keepdims=True)
        aout[1][...] += jnp.sum(dhv, axis=0, keepdims=True)
        aout[2][...] += jnp.sum(dr, axis=0, keepdims=True)
    return _rowwise(name, fn, _rows(r.shape[0], D * 4, 2 * 2 ** 20), [r, dh], [g],
                    [(D, F32), (D, BF16)], [((1, D), F32)] * 3)


def _adamw_math(w, g, m, v):
    m2 = ADAM_B1 * m + (1.0 - ADAM_B1) * g
    v2 = ADAM_B2 * v + (1.0 - ADAM_B2) * (g * g)
    m_hat = m2 / (1.0 - ADAM_B1 ** ADAM_STEP)
    v_hat = v2 / (1.0 - ADAM_B2 ** ADAM_STEP)
    delta = -ADAM_LR * (m_hat / (jnp.sqrt(v_hat) + ADAM_EPS) + ADAM_WD * w)
    return delta, m2, v2


def _adamw(name, w, g, m, v):
    def fn(rin, vin, rout, aout):
        gv = rin[1][...]
        d, m2, v2 = _adamw_math(rin[0][...], gv, rin[2][...], rin[3][...])
        rout[0][...] = gv
        rout[1][...] = d
        rout[2][...] = m2
        rout[3][...] = v2
    C = w.shape[1]
    return _rowwise(name, fn, _rows(w.shape[0], C * 4, 2 * 2 ** 20), [w, g, m, v], [], [(C, F32)] * 4)


def _conv_shifted(hc, p1, p2, rowi):
    r1 = jnp.where(rowi == 0, p1, pltpu.roll(hc, 1, 0))
    r2 = jnp.where(rowi == 0, p2, jnp.where(rowi == 1, p1, pltpu.roll(hc, 2, 0)))
    return r1, r2


def _mix_fwd(z, ws, bsb, gv, bv, cw8, tm):
    S, DIN = z.shape
    DA = gv.shape[1]
    DB = DA
    H = DA // HEAD
    o = 2 * DA
    nch = tm // HEAD

    def body(z_ref, zp_ref, ws_ref, bsb_ref, gv_ref, bv_ref, cw_ref, y_ref):
        i = pl.program_id(0)
        ri = lax.broadcasted_iota(jnp.int32, (HEAD, HEAD), 0)
        ci = lax.broadcasted_iota(jnp.int32, (HEAD, HEAD), 1)
        tril = ri >= ci
        for h in range(H):
            cs = slice(h * HEAD, (h + 1) * HEAD)
            vs = slice(DA + h * HEAD, DA + (h + 1) * HEAD)
            wb = jnp.where(tril, ws_ref[h], 0.0).astype(BF16)
            for n in range(nch):
                rs = slice(n * HEAD, (n + 1) * HEAD)
                u = _gelu(z_ref[rs, cs])
                vhat, _ = _ln_stats(_gelu(z_ref[rs, vs]))
                vn = vhat * gv_ref[:, cs] + bv_ref[:, cs]
                mixed = jnp.dot(wb, vn.astype(BF16), preferred_element_type=F32) + bsb_ref[h]
                y_ref[rs, cs] = (u * mixed).astype(BF16)
        zb = z_ref[:, o:o + DB]
        hc = z_ref[:, o + DB:o + 2 * DB] * z_ref[:, o + 2 * DB:]
        keep = (i > 0).astype(F32)
        p1 = zp_ref[7:8, o + DB:o + 2 * DB] * zp_ref[7:8, o + 2 * DB:] * keep
        p2 = zp_ref[6:7, o + DB:o + 2 * DB] * zp_ref[6:7, o + 2 * DB:] * keep
        rowi = lax.broadcasted_iota(jnp.int32, (tm, DB), 0)
        r1, r2 = _conv_shifted(hc, p1, p2, rowi)
        cv = cw_ref[0:1, :] * r2 + cw_ref[1:2, :] * r1 + cw_ref[2:3, :] * hc
        y_ref[:, DA:] = (zb * cv).astype(BF16)

    full3 = lambda i: (0, 0, 0)
    full2 = lambda i: (0, 0)
    est = 2 * (tm * DIN * 4 + tm * (DA + DB) * 2) + 10 * tm * DB * 4
    return pl.pallas_call(
        body, name="mix_fwd", grid=(S // tm,),
        in_specs=[pl.BlockSpec((tm, DIN), lambda i: (i, 0)),
                  pl.BlockSpec((8, DIN), lambda i: (jnp.maximum(i * (tm // 8) - 1, 0), 0)),
                  pl.BlockSpec(ws.shape, full3), pl.BlockSpec(bsb.shape, full3),
                  pl.BlockSpec(gv.shape, full2), pl.BlockSpec(bv.shape, full2), pl.BlockSpec(cw8.shape, full2)],
        out_specs=pl.BlockSpec((tm, DA + DB), lambda i: (i, 0)),
        out_shape=jax.ShapeDtypeStruct((S, DA + DB), BF16),
        compiler_params=pltpu.CompilerParams(dimension_semantics=("arbitrary",),
                                             vmem_limit_bytes=_vmem_limit(est)),
    )(z, z, ws, bsb, gv, bv, cw8)


def _mix_bwd(z, dy, ws, bsb, gv, bv, cw8, tm):
    S, DIN = z.shape
    DA = gv.shape[1]
    DB = DA
    H = DA // HEAD
    o = 2 * DA
    nch = tm // HEAD
    nblk = S // tm

    def body(z_ref, zp_ref, zn_ref, dy_ref, dyn_ref, ws_ref, bsb_ref, gv_ref, bv_ref, cw_ref,
             dz_ref, dws_ref, dbs_ref, dgv_ref, dbv_ref, dcw_ref, dbin_ref):
        i = pl.program_id(0)

        @pl.when(i == 0)
        def _():
            for r in (dws_ref, dbs_ref, dgv_ref, dbv_ref, dcw_ref, dbin_ref):
                r[...] = jnp.zeros_like(r)

        ri = lax.broadcasted_iota(jnp.int32, (HEAD, HEAD), 0)
        ci = lax.broadcasted_iota(jnp.int32, (HEAD, HEAD), 1)
        tril = ri >= ci
        for h in range(H):
            cs = slice(h * HEAD, (h + 1) * HEAD)
            vs = slice(DA + h * HEAD, DA + (h + 1) * HEAD)
            wb = jnp.where(tril, ws_ref[h], 0.0).astype(BF16)
            gvh = gv_ref[:, cs]
            dws_acc = jnp.zeros((HEAD, HEAD), F32)
            dbs_acc = jnp.zeros((HEAD, HEAD), F32)
            dgv_acc = jnp.zeros((1, HEAD), F32)
            dbv_acc = jnp.zeros((1, HEAD), F32)
            dbu_acc = jnp.zeros((1, HEAD), F32)
            dbvv_acc = jnp.zeros((1, HEAD), F32)
            for n in range(nch):
                rs = slice(n * HEAD, (n + 1) * HEAD)
                u, du_dz = _gelu_and_grad(z_ref[rs, cs])
                v, dv_dz = _gelu_and_grad(z_ref[rs, vs])
                vhat, rstd = _ln_stats(v)
                vnb = (vhat * gvh + bv_ref[:, cs]).astype(BF16)
                mixed = jnp.dot(wb, vnb, preferred_element_type=F32) + bsb_ref[h]
                dya = dy_ref[rs, cs]
                dzu = dya * mixed * du_dz
                dmx = dya * u
                dmxb = dmx.astype(BF16)
                dbs_acc += jnp.broadcast_to(jnp.sum(dmx, axis=1, keepdims=True), (HEAD, HEAD))
                dws_acc += lax.dot_general(dmxb, vnb, _DIMS['nt'], preferred_element_type=F32)
                dvn = lax.dot_general(wb, dmxb, _DIMS['tn'], preferred_element_type=F32)
                dgv_acc += jnp.sum(dvn * vhat, axis=0, keepdims=True)
                dbv_acc += jnp.sum(dvn, axis=0, keepdims=True)
                dxh = dvn * gvh
                m1 = jnp.mean(dxh, axis=-1, keepdims=True)
                m2 = jnp.mean(dxh * vhat, axis=-1, keepdims=True)
                dzv = rstd * (dxh - m1 - vhat * m2) * dv_dz
                dz_ref[rs, cs] = dzu.astype(BF16)
                dz_ref[rs, vs] = dzv.astype(BF16)
                dbu_acc += jnp.sum(dzu, axis=0, keepdims=True)
                dbvv_acc += jnp.sum(dzv, axis=0, keepdims=True)
            dws_ref[h] += jnp.where(tril, dws_acc, 0.0)
            dbs_ref[h] += dbs_acc
            dgv_ref[:, cs] += dgv_acc
            dbv_ref[:, cs] += dbv_acc
            dbin_ref[:, cs] += dbu_acc
            dbin_ref[:, vs] += dbvv_acc

        zb = z_ref[:, o:o + DB]
        zc = z_ref[:, o + DB:o + 2 * DB]
        zx = z_ref[:, o + 2 * DB:]
        hc = zc * zx
        keep = (i > 0).astype(F32)
        p1 = zp_ref[7:8, o + DB:o + 2 * DB] * zp_ref[7:8, o + 2 * DB:] * keep
        p2 = zp_ref[6:7, o + DB:o + 2 * DB] * zp_ref[6:7, o + 2 * DB:] * keep
        rowi = lax.broadcasted_iota(jnp.int32, (tm, DB), 0)
        r1, r2 = _conv_shifted(hc, p1, p2, rowi)
        w0, w1, w2 = cw_ref[0:1, :], cw_ref[1:2, :], cw_ref[2:3, :]
        cv = w0 * r2 + w1 * r1 + w2 * hc
        dyb = dy_ref[:, DA:]
        dzb = dyb * cv
        dcv = dyb * zb
        more = (i < nblk - 1).astype(F32)
        n0 = dyn_ref[0:1, DA:] * zn_ref[0:1, o:o + DB] * more
        n1 = dyn_ref[1:2, DA:] * zn_ref[1:2, o:o + DB] * more
        f1 = jnp.where(rowi == tm - 1, n0, pltpu.roll(dcv, tm - 1, 0))
        f2 = jnp.where(rowi == tm - 1, n1, jnp.where(rowi == tm - 2, n0, pltpu.roll(dcv, tm - 2, 0)))
        dhc = w2 * dcv + w1 * f1 + w0 * f2
        dzc = dhc * zx
        dzx = dhc * zc
        dcw_ref[0:1, :] += jnp.sum(dcv * r2, axis=0, keepdims=True)
        dcw_ref[1:2, :] += jnp.sum(dcv * r1, axis=0, keepdims=True)
        dcw_ref[2:3, :] += jnp.sum(dcv * hc, axis=0, keepdims=True)
        dz_ref[:, o:o + DB] = dzb.astype(BF16)
        dz_ref[:, o + DB:o + 2 * DB] = dzc.astype(BF16)
        dz_ref[:, o + 2 * DB:] = dzx.astype(BF16)
        dbin_ref[:, o:o + DB] += jnp.sum(dzb, axis=0, keepdims=True)
        dbin_ref[:, o + DB:o + 2 * DB] += jnp.sum(dzc, axis=0, keepdims=True)
        dbin_ref[:, o + 2 * DB:] += jnp.sum(dzx, axis=0, keepdims=True)

    full3 = lambda i: (0, 0, 0)
    full2 = lambda i: (0, 0)
    prev8 = lambda i: (jnp.maximum(i * (tm // 8) - 1, 0), 0)
    next8 = lambda i: (jnp.minimum((i + 1) * (tm // 8), S // 8 - 1), 0)
    est = 2 * (tm * DIN * 4 + tm * (DA + DB) * 4 + tm * DIN * 2) + 16 * tm * DB * 4
    return pl.pallas_call(
        body, name="mix_bwd", grid=(nblk,),
        in_specs=[pl.BlockSpec((tm, DIN), lambda i: (i, 0)),
                  pl.BlockSpec((8, DIN), prev8), pl.BlockSpec((8, DIN), next8),
                  pl.BlockSpec((tm, DA + DB), lambda i: (i, 0)), pl.BlockSpec((8, DA + DB), next8),
                  pl.BlockSpec(ws.shape, full3), pl.BlockSpec(bsb.shape, full3),
                  pl.BlockSpec(gv.shape, full2), pl.BlockSpec(bv.shape, full2), pl.BlockSpec(cw8.shape, full2)],
        out_specs=[pl.BlockSpec((tm, DIN), lambda i: (i, 0)),
                   pl.BlockSpec(ws.shape, full3), pl.BlockSpec(ws.shape, full3),
                   pl.BlockSpec(gv.shape, full2), pl.BlockSpec(gv.shape, full2),
                   pl.BlockSpec(cw8.shape, full2), pl.BlockSpec((1, DIN), full2)],
        out_shape=[jax.ShapeDtypeStruct((S, DIN), BF16),
                   jax.ShapeDtypeStruct(ws.shape, F32), jax.ShapeDtypeStruct(ws.shape, F32),
                   jax.ShapeDtypeStruct(gv.shape, F32), jax.ShapeDtypeStruct(gv.shape, F32),
                   jax.ShapeDtypeStruct(cw8.shape, F32), jax.ShapeDtypeStruct((1, DIN), F32)],
        compiler_params=pltpu.CompilerParams(dimension_semantics=("arbitrary",),
                                             vmem_limit_bytes=_vmem_limit(est)),
    )(z, z, z, dy, dy, ws, bsb, gv, bv, cw8)


def _place():
    x, y, c = lax.axis_index("x"), lax.axis_index("y"), lax.axis_index("c")
    others = [(1 - x, y), (x, 1 - y), (1 - x, 1 - y)]
    return x, y, c, 2 * x + y, others, [2 * ox + oy for ox, oy in others]


def _half(c, rows_half):
    return pl.ds(pl.multiple_of(c * rows_half, 16), rows_half)


def _rcopy(src, dst, ssem, rsem, dev):
    return pltpu.make_async_remote_copy(src_ref=src, dst_ref=dst, send_sem=ssem, recv_sem=rsem,
                                        device_id=dev, device_id_type=MESH)


class _Comm:
    def __init__(self, aliased, ins, fresh, n_sems, stages, sink):
        self.aliased, self.ins, self.fresh = list(aliased), list(ins), list(fresh)
        self.n_sems, self.stages, self.sink = n_sems, stages, sink


def _comm_operands(comms, n_in, n_out):
    arrays, shapes, scratch, aliases = [], [], [], {}
    for cm in comms:
        for a in cm.aliased:
            aliases[n_in + len(arrays)] = n_out + len(shapes)
            arrays.append(a)
            shapes.append(jax.ShapeDtypeStruct(a.shape, a.dtype))
        arrays += cm.ins
        shapes += cm.fresh
        scratch += [pltpu.SemaphoreType.DMA((cm.n_sems,))] * 2
    return arrays, shapes, scratch, aliases


def _comm_refs(comms, in_refs, out_refs, sem_refs):
    per, pi, po = [], 0, 0
    for i, cm in enumerate(comms):
        pi += len(cm.aliased)
        ins = in_refs[pi:pi + len(cm.ins)]
        pi += len(cm.ins)
        al = out_refs[po:po + len(cm.aliased)]
        po += len(cm.aliased)
        fr = out_refs[po:po + len(cm.fresh)]
        po += len(cm.fresh)
        per.append((al, ins, fr, sem_refs[2 * i], sem_refs[2 * i + 1]))
    return per


def _comm_first_step(comms, hosted, grid):
    if not comms:
        return
    at_first = functools.reduce(lambda a, b: a & b, [pl.program_id(d) == 0 for d in range(len(grid))])

    @pl.when(at_first)
    def _():
        for cm, refs in zip(comms, hosted):
            for d in cm.stages[0](*refs, False):
                d.start()


def _comm_last_step(comms, hosted, grid):
    if not comms:
        return
    at_last = functools.reduce(lambda a, b: a & b, [pl.program_id(d) == g - 1 for d, g in enumerate(grid)])

    @pl.when(at_last)
    def _():
        for cm, refs in zip(comms, hosted):
            for d in cm.stages[0](*refs, True):
                d.wait()
        for cm, refs in zip(comms, hosted):
            for stage in cm.stages[1:]:
                for d in stage(*refs, False):
                    d.start()
                for d in stage(*refs, True):
                    d.wait()


def _deliver(comms, outs):
    pos = 0
    for cm in comms:
        al = outs[pos:pos + len(cm.aliased)]
        pos += len(cm.aliased)
        fr = outs[pos:pos + len(cm.fresh)]
        pos += len(cm.fresh)
        cm.sink(list(al), list(fr))


def _comm_call(name, comms):
    in_arrays, out_shapes, scratch, aliases = _comm_operands(comms, 0, 0)
    n_in, n_out = len(in_arrays), len(out_shapes)

    def body(*refs):
        hosted = _comm_refs(comms, refs[:n_in], refs[n_in:n_in + n_out], refs[n_in + n_out:])
        for cm, r in zip(comms, hosted):
            for stage in cm.stages:
                for d in stage(*r, False):
                    d.start()
                for d in stage(*r, True):
                    d.wait()

    outs = pl.pallas_call(body, name=name, in_specs=[ANY] * n_in, out_specs=[ANY] * n_out,
                          out_shape=out_shapes, scratch_shapes=scratch, input_output_aliases=aliases)(*in_arrays)
    _deliver(comms, outs)


def _c_gather(fulls, sink, ici, fwd, conv=None):
    n = len(fulls)
    n_ici = 3 * (n + (conv is not None)) if ici else 0

    def stage_ici(al, ins, fr, ssem, rsem, wait):
        x, y, c, chip, others, ochips = _place()
        ds = []
        for m in range(n + (conv is not None)):
            for k in range(3):
                i = 3 * m + k
                if m < n:
                    ah = al[m].shape[1] // 2
                    mine, lands = al[m].at[chip, _half(c, ah)], al[m].at[ochips[k], _half(c, ah)]
                else:
                    mine, lands = al[m].at[chip], al[m].at[ochips[k]]
                ds.append(_rcopy(mine, lands, ssem.at[i], rsem.at[i], (x, y, c)) if wait
                          else _rcopy(mine, mine, ssem.at[i], rsem.at[i], (*others[k], c)))
        return ds

    def stage_fwd(al, ins, fr, ssem, rsem, wait):
        x, y, c, chip, others, ochips = _place()
        ds = []
        for m in range(n):
            ah = al[m].shape[1] // 2
            for k in range(3):
                i = n_ici + 3 * m + k
                got = al[m].at[ochips[k], _half(c, ah)]
                ds.append(_rcopy(got, al[m].at[ochips[k], _half(1 - c, ah)], ssem.at[i], rsem.at[i], (x, y, c)) if wait
                          else _rcopy(got, got, ssem.at[i], rsem.at[i], (x, y, 1 - c)))
        return ds

    stages = ([stage_ici] if ici else []) + ([stage_fwd] if fwd else [])
    return _Comm(list(fulls) + ([conv] if conv is not None else []), [], [], n_ici + (3 * n if fwd else 0),
                 stages, sink)


def _c_rs_pair(parts, sink):
    def stage(al, ins, fr, ssem, rsem, wait):
        x, y, c, _, _, _ = _place()
        return [_rcopy(ins[m].at[:, _half(1 - c, ins[m].shape[1] // 2), :], fr[m], ssem.at[m], rsem.at[m],
                       (x, y, 1 - c)) for m in range(len(ins))]
    fresh = [jax.ShapeDtypeStruct((p.shape[0], p.shape[1] // 2, p.shape[2]), p.dtype) for p in parts]
    return _Comm([], parts, fresh, len(parts), [stage], sink)


def _c_rs_chips(psums, sink):
    def stage(al, ins, fr, ssem, rsem, wait):
        x, y, c, _, others, ochips = _place()
        return [_rcopy(ins[m].at[ochips[k]], fr[m].at[k], ssem.at[3 * m + k], rsem.at[3 * m + k], (*others[k], c))
                for m in range(len(ins)) for k in range(3)]
    fresh = [jax.ShapeDtypeStruct((3,) + p.shape[1:], p.dtype) for p in psums]
    return _Comm([], psums, fresh, 3 * len(psums), [stage], sink)


def _c_exchange(g2s, sink):
    def stage(al, ins, fr, ssem, rsem, wait):
        x, y, c, _, _, _ = _place()
        if wait:
            return [_rcopy(al[m].at[c], al[m].at[1 - c], ssem.at[m], rsem.at[m], (x, y, c)) for m in range(len(al))]
        return [_rcopy(al[m].at[c], al[m].at[c], ssem.at[m], rsem.at[m], (x, y, 1 - c)) for m in range(len(al))]
    return _Comm(g2s, [], [], len(g2s), [stage], sink)


def _c_allgather_small(buf, sink):
    def stage(al, ins, fr, ssem, rsem, wait):
        x, y, c, _, _, _ = _place()
        ds = []
        for r in range(1, 8):
            peer = (1 - x if r & 4 else x, 1 - y if r & 2 else y, 1 - c if r & 1 else c)
            ds.append(_rcopy(al[0].at[0], al[0].at[r], ssem.at[r - 1], rsem.at[r - 1], peer))
        return ds
    return _Comm([buf], [], [], 7, [stage], sink)


def _sum_small(buf, dev_idx):
    _, m, n = buf.shape

    def body(me_ref, b_ref, o_ref):
        me = me_ref[0]
        acc = b_ref[me]
        for d in range(1, 8):
            acc = acc + b_ref[jnp.bitwise_xor(me, d)]
        o_ref[...] = acc

    gs = pltpu.PrefetchScalarGridSpec(
        num_scalar_prefetch=1, grid=(1,),
        in_specs=[pl.BlockSpec(buf.shape, lambda i, me: (0, 0, 0))],
        out_specs=pl.BlockSpec((m, n), lambda i, me: (0, 0)))
    return pl.pallas_call(body, name="sum_small", grid_spec=gs,
                          out_shape=jax.ShapeDtypeStruct((m, n), F32))(dev_idx, buf)


def _sum_pair(name, part, sib, c_idx):
    _, A, B = part.shape
    ah = A // 2
    br = _rows(ah, B * 2)

    def body(c_ref, a_ref, b_ref, o_ref):
        o_ref[...] = (a_ref[...].astype(F32) + b_ref[...].astype(F32)).astype(BF16)

    gs = pltpu.PrefetchScalarGridSpec(
        num_scalar_prefetch=1, grid=(N_CHIPS, ah // br),
        in_specs=[pl.BlockSpec((None, None, br, B), lambda j, i, c: (j, c[0], i, 0)),
                  pl.BlockSpec((None, br, B), lambda j, i, c: (j, i, 0))],
        out_specs=pl.BlockSpec((None, br, B), lambda j, i, c: (j, i, 0)))
    return pl.pallas_call(body, name=name, grid_spec=gs,
                          out_shape=jax.ShapeDtypeStruct((N_CHIPS, ah, B), BF16),
                          compiler_params=pltpu.CompilerParams(vmem_limit_bytes=_vmem_limit(2 * 3 * br * B * 2)),
                          )(c_idx, part.reshape(N_CHIPS, 2, ah, B), sib)


def _sum_chips(name, psum, recv, place_idx):
    _, ah, B = psum.shape
    br = _rows(ah, B * 4)

    def body(c_ref, p_ref, r0_ref, r1_ref, r2_ref, o_ref):
        o_ref[...] = ((p_ref[...].astype(F32) + r0_ref[...].astype(F32)) + r1_ref[...].astype(F32)) \
            + r2_ref[...].astype(F32)

    gs = pltpu.PrefetchScalarGridSpec(
        num_scalar_prefetch=1, grid=(ah // br,),
        in_specs=[pl.BlockSpec((None, br, B), lambda i, c: (c[0], i, 0))]
        + [pl.BlockSpec((None, br, B), functools.partial(lambda i, c, k: (k, i, 0), k=k)) for k in range(3)],
        out_specs=pl.BlockSpec((None, br, B), lambda i, c: (c[1], i, 0)))
    return pl.pallas_call(body, name=name, grid_spec=gs,
                          out_shape=jax.ShapeDtypeStruct((2, ah, B), F32),
                          compiler_params=pltpu.CompilerParams(vmem_limit_bytes=_vmem_limit(2 * br * B * (4 * 2 + 4))),
                          )(place_idx, psum, recv, recv, recv)


def _cast_into(name, a, place_idx):
    A, B = a.shape
    br = _rows(A, B * 4)

    def body(c_ref, a_ref, o_ref):
        o_ref[...] = a_ref[...].astype(BF16)

    gs = pltpu.PrefetchScalarGridSpec(
        num_scalar_prefetch=1, grid=(A // br,),
        in_specs=[pl.BlockSpec((br, B), lambda i, c: (i, 0))],
        out_specs=pl.BlockSpec((None, br, B), lambda i, c: (c[0], i, 0)))
    return pl.pallas_call(body, name=name, grid_spec=gs,
                          out_shape=jax.ShapeDtypeStruct((N_CHIPS, A, B), BF16))(place_idx, a)


def _cast_behind(x, shards, place_idx, comms):
    n_steps = 16
    n = len(shards)
    xv = x
    svs = list(shards)

    def pace(v):
        for steps_per_block in (1, 2, 4, 8, 16):
            blocks = n_steps // steps_per_block
            if v.shape[0] % blocks == 0 and (v.shape[0] // blocks) % 16 == 0:
                return steps_per_block
        raise ValueError(v.shape)

    n_in, n_out = 2 + n, 1 + n
    cm_in_arrays, cm_out_shapes, cm_scratch, aliases = _comm_operands(comms, n_in, n_out)
    n_cm_in, n_cm_out = len(cm_in_arrays), len(cm_out_shapes)
    grid = (n_steps,)

    def body(c_ref, x_ref, *rest):
        s_refs = rest[:n]
        xo_ref = rest[n + n_cm_in]
        so_refs = rest[n + n_cm_in + 1:n + n_cm_in + 1 + n]
        o_end = n + n_cm_in + 1 + n
        hosted = _comm_refs(comms, rest[n:n + n_cm_in], rest[o_end:o_end + n_cm_out], rest[o_end + n_cm_out:])
        _comm_first_step(comms, hosted, grid)
        xo_ref[...] = x_ref[...].astype(BF16)
        for s_ref, o_ref in zip(s_refs, so_refs):
            o_ref[...] = s_ref[...].astype(BF16)
        _comm_last_step(comms, hosted, grid)

    def rows(v):
        return v.shape[0] * pace(v) // n_steps

    def in_spec(v):
        return pl.BlockSpec((rows(v), v.shape[1]), functools.partial(lambda i, c, q: (i // q, 0), q=pace(v)))

    def shard_out_spec(v):
        return pl.BlockSpec((None, rows(v), v.shape[1]),
                            functools.partial(lambda i, c, q: (c[0], i // q, 0), q=pace(v)))

    gs = pltpu.PrefetchScalarGridSpec(
        num_scalar_prefetch=1, grid=grid,
        in_specs=[in_spec(v) for v in [xv] + svs] + [ANY] * n_cm_in,
        out_specs=[in_spec(xv)] + [shard_out_spec(v) for v in svs] + [ANY] * n_cm_out,
        scratch_shapes=cm_scratch)
    outs = pl.pallas_call(
        body, name="cast_behind", grid_spec=gs,
        out_shape=[jax.ShapeDtypeStruct(xv.shape, BF16)]
        + [jax.ShapeDtypeStruct((N_CHIPS,) + v.shape, BF16) for v in svs] + cm_out_shapes,
        input_output_aliases=aliases,
        compiler_params=pltpu.CompilerParams(
            dimension_semantics=("arbitrary",),
            vmem_limit_bytes=_vmem_limit(2 * sum(rows(v) * v.shape[1] * 6 for v in [xv] + svs))),
    )(place_idx, xv, *svs, *cm_in_arrays)
    _deliver(comms, outs[n_out:])
    return outs[0].reshape(x.shape), [o.reshape((N_CHIPS,) + s.shape) for o, s in zip(outs[1:n_out], shards)]


def _adamw_small(gsum, gconv, offsets, ws, ms, vs):
    n = len(ws)

    def body(*refs):
        g_ref, gc_ref = refs[0], refs[1]
        w_refs = refs[2:2 + n]
        m_refs = refs[2 + n:2 + 2 * n]
        v_refs = refs[2 + 2 * n:2 + 3 * n]
        outs = refs[2 + 3 * n:]
        for i in range(n):
            rows = w_refs[i].shape[0]
            g = gc_ref[...] if offsets[i] is None else g_ref[offsets[i]:offsets[i] + rows, :]
            d, m2, v2 = _adamw_math(w_refs[i][...], g, m_refs[i][...], v_refs[i][...])
            outs[4 * i][...] = g
            outs[4 * i + 1][...] = d
            outs[4 * i + 2][...] = m2
            outs[4 * i + 3][...] = v2

    vm = pl.BlockSpec(memory_space=pltpu.VMEM)
    out_shape = []
    for w in ws:
        out_shape += [jax.ShapeDtypeStruct(w.shape, F32)] * 4
    return pl.pallas_call(body, name="adamw_small", in_specs=[vm] * (2 + 3 * n), out_specs=[vm] * (4 * n),
                          out_shape=out_shape)(gsum, gconv, *ws, *ms, *vs)


def _ident(accs, tes, res):
    return [accs[0]]


def _ffn_up(name, hb, wg4, wu4, comms=()):
    S, D = hb.shape

    def epi(accs, tes, res):
        g, u = accs
        return [g, u, g * _sigmoid(g) * u]

    return _mm(name, 'nn', [hb], [wg4, wu4], [(0, 0, 0), (0, 1, 1)], 2, epi, [BF16] * 3,
               bm=_blk(S, 512), bn=wg4.shape[2], bk=D, b_blocked=True, n_outer=True, comms=comms)


def _ffn_gate(name, hb, wg4, comms=()):
    S, D = hb.shape
    return _mm(name, 'nn', [hb], [wg4], [(0, 0, 0)], 1, _ident, [BF16],
               bm=_blk(S, 1024), bn=wg4.shape[2], bk=D, b_blocked=True, n_outer=True, comms=comms)[0]


def _ffn_upmul(name, hb, wu4, gb, comms=()):
    S, D = hb.shape

    def epi(accs, tes, res):
        g = tes[0].astype(F32)
        return [accs[0], g * _sigmoid(g) * accs[0]]

    return _mm(name, 'nn', [hb], [wu4], [(0, 0, 0)], 1, epi, [BF16, BF16],
               bm=_blk(S, 1024), bn=wu4.shape[2], bk=D, tile_extras=[gb], b_blocked=True, n_outer=True, comms=comms)


def _ffn_down(name, ab, wd, hres, comms=()):
    S, F = ab.shape
    D = wd.shape[1]
    return _mm(name, 'nn', [ab], [wd], [(0, 0, 0)], 1, lambda accs, tes, res: [ALPHA * tes[0] + 0.5 * accs[0]],
               [F32], bm=_blk(S, 1024), bn=_blk(D, 512), bk=F, tile_extras=[hres], comms=comms)[0]


def _ffn_da(name, dsub, wd, gb, ub, comms=()):
    S, D = dsub.shape
    F = wd.shape[0]

    def epi(accs, tes, res):
        da = accs[0]
        g = tes[0].astype(F32)
        u = tes[1].astype(F32)
        s = _sigmoid(g)
        silu = g * s
        return [(da * u) * (s + silu * (1.0 - s)), da * silu]

    return _mm(name, 'nt', [dsub], [wd], [(0, 0, 0)], 1, epi, [BF16, BF16],
               bm=_blk(S, 1024), bn=_blk(F, 512), bk=D, tile_extras=[gb, ub], comms=comms)


def _dw_rows(name, ab, dsub, cs, comms=()):
    S, D = dsub.shape
    return _mm(name, 'tn', [ab], [dsub], [(0, 0, 0)], 1, _ident, [BF16],
               bm=cs, bn=_blk(D, 1024), bk=_blk(S, 2048), comms=comms)[0].reshape(N_CHIPS, cs, D)


def _dw_cols(name, hb, dxb, cs, comms=()):
    S, D = hb.shape
    return _mm(name, 'tn', [hb], [dxb], [(0, 0, 0)], 1, _ident, [BF16],
               bm=_blk(D, 1024), bn=cs, bk=_blk(S, 2048), out_blocked=True, comms=comms)[0]


def _ffn_dh(name, dgb, dub, wg4, wu4, dres, comms=()):
    S = dgb.shape[0]
    D = wg4.shape[1]
    return _mm(name, 'nt', [dgb, dub], [wg4, wu4], [(0, 0, 0), (1, 1, 0)], 1,
               lambda accs, tes, res: [accs[0] + tes[0]], [F32],
               bm=_blk(S, 1024), bn=_blk(D, 1024), bk=wg4.shape[2], tile_extras=[dres], b_blocked=True,
               comms=comms)[0]


def _step(p):
    x = p['x'][0]
    target = p['loss_target'][0]
    S, D = x.shape
    cx, cy, cc = lax.axis_index("x"), lax.axis_index("y"), lax.axis_index("c")
    chip = 2 * cx + cy
    c_idx = jnp.reshape(cc, (1,)).astype(jnp.int32)
    place_idx = jnp.stack([chip, cc]).astype(jnp.int32)

    w_s = p['w_s'][0]
    H = w_s.shape[0]
    DA = H * HEAD
    bsb = jnp.broadcast_to(p['b_s'][0][:, :, None], (H, HEAD, HEAD))
    conv_shard = p['conv_w'][0]
    conv8 = jnp.zeros((8, conv_shard.shape[1]), F32).at[:CONV_TAPS].set(conv_shard)

    W = {'ffa_gate': _cast_into("cast_ffa_gate", p['ffa_gate'][0], place_idx)}
    conv = {'full': lax.dynamic_update_slice(jnp.zeros((N_CHIPS,) + conv8.shape, F32), conv8[None], (chip, 0, 0))}

    def gathered(names, with_conv=False):
        def sink(al, fr):
            W.update(zip(names, al))
            if with_conv:
                conv['full'] = al[len(names)]
        return sink

    def c_gather(names, ici, fwd, with_conv=False):
        return _c_gather([W[n] for n in names], gathered(names, with_conv), ici, fwd,
                         conv['full'] if with_conv else None)

    rest = [n for n in BIG if n != 'ffa_gate']
    xb, casted = _cast_behind(x, [p[n][0] for n in rest], place_idx, comms=[c_gather(['ffa_gate'], True, True)])
    W.update(zip(rest, casted))
    cs_ff = W['ffa_gate'].shape[2]
    F = N_CHIPS * cs_ff

    ga = _ffn_gate("ffa_gate", xb, W['ffa_gate'],
                   comms=[c_gather(['ffa_up'], True, True), c_gather(['ffa_down'], True, False)])
    ua, aa = _ffn_upmul("ffa_up", xb, W['ffa_up'], ga,
                        comms=[c_gather(['ffa_down'], False, True),
                               c_gather(['w_in', 'ffc_gate'], True, False, with_conv=True)])
    cw8 = jnp.transpose(conv['full'], (1, 0, 2)).reshape(8, DA)
    wd_a = W['ffa_down'].reshape(F, D)
    r1 = _ffn_down("ffa_down", aa, wd_a, x,
                   comms=[c_gather(['w_in', 'ffc_gate'], False, True), c_gather(['w_out', 'ffc_down'], True, False)])
    h1, h1b = _ln_fwd("ln_a", r1, p['ln_a_g'], p['ln_a_b'])
    w_in4 = W['w_in']
    cs_in = w_in4.shape[2]
    z = _mm("w_in", 'nn', [h1b], [w_in4], [(0, 0, 0)], 1, lambda accs, tes, res: [accs[0] + res[0]], [F32],
            bm=_blk(S, 1024), bn=cs_in, bk=D, row_extras=[p['b_in']], b_blocked=True, n_outer=True,
            comms=[c_gather(['w_out', 'ffc_down'], False, True), c_gather(['ffc_up'], True, False)])[0]
    w_out = W['w_out'].reshape(DA * 2, D)
    tm = _blk(S, 256)
    yb = _mix_fwd(z, w_s, bsb, p['ln_v_g'], p['ln_v_b'], cw8, tm)
    r2 = _mm("w_out", 'nn', [yb], [w_out], [(0, 0, 0)], 1,
             lambda accs, tes, res: [accs[0] + res[0] + ALPHA * tes[0]], [F32],
             bm=_blk(S, 1024), bn=_blk(D, 1024), bk=2 * DA, tile_extras=[h1], row_extras=[p['b_out']],
             comms=[c_gather(['ffc_up'], False, True)])[0]
    h2, h2b = _ln_fwd("ln_m", r2, p['ln_m_g'], p['ln_m_b'])
    gc, uc, ac = _ffn_up("ffc_up", h2b, W['ffc_gate'], W['ffc_up'])
    wd_c = W['ffc_down'].reshape(F, D)
    r3 = _ffn_down("ffc_down", ac, wd_c, h2)

    pg, sg, sib, psum, recv, g2 = {}, {}, {}, {}, {}, {}

    def c_pair(names):
        return _c_rs_pair([pg[n] for n in names], lambda al, fr: sib.update(zip(names, fr)))

    def c_chips(names):
        return _c_rs_chips([psum[n] for n in names], lambda al, fr: recv.update(zip(names, fr)))

    def c_swap(names):
        return _c_exchange([g2[n] for n in names], lambda al, fr: g2.update(zip(names, al)))

    def sum_pair(*names):
        for n in names:
            psum[n] = _sum_pair(f"sum_pair_{n}", pg[n], sib[n], c_idx)

    def sum_chips(*names):
        for n in names:
            g2[n] = _sum_chips(f"sum_chips_{n}", psum[n], recv[n], place_idx)

    dres3, dsub3, sg['ln_c_g'], sg['ln_c_b'], lsum = _ln_loss_bwd(
        "ln_c_loss_bwd", r3, target, p['ln_c_g'], p['ln_c_b'], 0.5)

    dgc, duc = _ffn_da("ffc_da", dsub3, wd_c, gc, uc)
    pg['ffc_down'] = _dw_rows("ffc_dwd", ac, dsub3, cs_ff)
    pg['ffc_gate'] = _dw_cols("ffc_dwg", h2b, dgc, cs_ff, comms=[c_pair(['ffc_down'])])
    sum_pair('ffc_down')
    pg['ffc_up'] = _dw_cols("ffc_dwu", h2b, duc, cs_ff, comms=[c_pair(['ffc_gate']), c_chips(['ffc_down'])])
    sum_pair('ffc_gate')
    dh2 = _ffn_dh("ffc_dh", dgc, duc, W['ffc_gate'], W['ffc_up'], dres3,
                  comms=[c_pair(['ffc_up']), c_chips(['ffc_gate'])])
    sum_pair('ffc_up')
    sum_chips('ffc_down', 'ffc_gate')

    dres2, dmix, sg['ln_m_g'], sg['ln_m_b'], sg['b_out'] = _ln_bwd("ln_m_bwd", r2, dh2, p['ln_m_g'], 1.0)
    pg['w_out'] = _mm("dw_out", 'tn', [yb], [dmix], [(0, 0, 0)], 1, _ident, [BF16],
                      bm=_blk(2 * DA, 1024), bn=_blk(D, 1024), bk=_blk(S, 2048),
                      comms=[c_swap(['ffc_down', 'ffc_gate'])])[0].reshape(N_CHIPS, 2 * DA // 4, D)
    dy = _mm("dy", 'nt', [dmix], [w_out], [(0, 0, 0)], 1, _ident, [F32],
             bm=_blk(S, 1024), bn=_blk(2 * DA, 1024), bk=D, comms=[c_pair(['w_out'])])[0]
    sum_pair('w_out')
    dz, dws, dbs, sg['ln_v_g'], sg['ln_v_b'], dcw, sg['b_in'] = _mix_bwd(
        z, dy, w_s, bsb, p['ln_v_g'], p['ln_v_b'], cw8, tm)
    sg['w_s'] = dws
    sg['b_s'] = dbs[:, :, 0]
    pg['w_in'] = _dw_cols("dw_in", h1b, dz, cs_in, comms=[c_chips(['ffc_up']), c_chips(['w_out'])])
    sum_chips('ffc_up', 'w_out')
    dh1 = _mm("dh1", 'nt', [dz], [w_in4], [(0, 0, 0)], 1, lambda accs, tes, res: [accs[0] + tes[0]], [F32],
              bm=_blk(S, 1024), bn=_blk(D, 1024), bk=cs_in, tile_extras=[dres2], b_blocked=True,
              comms=[c_pair(['w_in']), c_swap(['ffc_up', 'w_out'])])[0]
    sum_pair('w_in')

    dres1, dsub1, sg['ln_a_g'], sg['ln_a_b'], _ = _ln_bwd("ln_a_bwd", r1, dh1, p['ln_a_g'], 0.5)
    def as_rows(a):
        a = a.reshape(-1, 128)
        pad = (-a.shape[0]) % 8
        return jnp.pad(a, ((0, pad), (0, 0))) if pad else a

    pieces, offsets, off = [], {}, 0
    for n in SMALL + ['conv_w']:
        piece = as_rows(dcw[:CONV_TAPS] if n == 'conv_w' else sg[n])
        offsets[n] = off
        off += piece.shape[0]
        pieces.append(piece)
    packed = jnp.concatenate(pieces, axis=0)
    small = {'buf': jnp.zeros((8,) + packed.shape, F32).at[0].set(packed)}
    dga, dua = _ffn_da("ffa_da", dsub1, wd_a, ga, ua,
                       comms=[c_chips(['w_in']), _c_allgather_small(small['buf'], lambda al, fr: small.update(buf=al[0]))])
    sum_chips('w_in')
    gsum = _sum_small(small['buf'], jnp.reshape(4 * cx + 2 * cy + cc, (1,)).astype(jnp.int32))
    pg['ffa_down'] = _dw_rows("ffa_dwd", aa, dsub1, cs_ff, comms=[c_swap(['w_in'])])
    pg['ffa_gate'] = _dw_cols("ffa_dwg", xb, dga, cs_ff, comms=[c_pair(['ffa_down'])])
    sum_pair('ffa_down')
    pg['ffa_up'] = _dw_cols("ffa_dwu", xb, dua, cs_ff, comms=[c_pair(['ffa_gate']), c_chips(['ffa_down'])])
    sum_pair('ffa_gate')
    _comm_call("pair_ffa_up", [c_pair(['ffa_up'])])
    sum_pair('ffa_up')
    sum_chips('ffa_down')
    dx = _ffn_dh("ffa_dh", dga, dua, W['ffa_gate'], W['ffa_up'], dres1,
                 comms=[c_chips(['ffa_gate']), c_chips(['ffa_up']), c_swap(['ffa_down'])])
    sum_chips('ffa_gate', 'ffa_up')
    _comm_call("swap_ffa", [c_swap(['ffa_gate', 'ffa_up'])])

    out = {}
    for n in BIG:
        g, d, m2, v2 = _adamw(f"adamw_{n}", p[n][0], g2[n].reshape(p[n][0].shape), p['m_' + n][0], p['v_' + n][0])
        shp = p[n].shape
        out[n] = (g.reshape(shp), d.reshape(shp), m2.reshape(shp), v2.reshape(shp))

    loss = lax.psum(0.5 * jnp.sum(lsum) / D, ("x", "y", "c"))
    n_conv = CONV_TAPS * DA // 128
    conv_sum = gsum[offsets['conv_w']:offsets['conv_w'] + n_conv].reshape(CONV_TAPS, DA)
    cw = conv_shard.shape[1]
    gconv = lax.dynamic_slice_in_dim(conv_sum, (2 * cx + cy) * cw, cw, axis=1).reshape(-1, 128)
    names = SMALL + ['conv_w']
    rows128 = lambda a: a.reshape(-1, 128)
    res = _adamw_small(gsum, gconv, [offsets[n] if n != 'conv_w' else None for n in names],
                       [rows128(p[n]) for n in names], [rows128(p['m_' + n]) for n in names],
                       [rows128(p['v_' + n]) for n in names])
    for i, n in enumerate(names):
        out[n] = tuple(r.reshape(p[n].shape) for r in res[4 * i:4 * i + 4])

    return (loss, dx.reshape(p['x'].shape), *[out[n][0] for n in WEIGHTS], *[out[n][1] for n in WEIGHTS],
            *[out[n][2] for n in WEIGHTS], *[out[n][3] for n in WEIGHTS])


def kernel(x, ffa_gate, ffa_up, ffa_down, ln_a_g, ln_a_b, w_in, b_in, w_s, b_s, ln_v_g, ln_v_b, conv_w, w_out, b_out, ln_m_g, ln_m_b, ffc_gate, ffc_up, ffc_down, ln_c_g, ln_c_b, loss_target, m_ffa_gate, m_ffa_up, m_ffa_down, m_ln_a_g, m_ln_a_b, m_w_in, m_b_in, m_w_s, m_b_s, m_ln_v_g, m_ln_v_b, m_conv_w, m_w_out, m_b_out, m_ln_m_g, m_ln_m_b, m_ffc_gate, m_ffc_up, m_ffc_down, m_ln_c_g, m_ln_c_b, v_ffa_gate, v_ffa_up, v_ffa_down, v_ln_a_g, v_ln_a_b, v_w_in, v_b_in, v_w_s, v_b_s, v_ln_v_g, v_ln_v_b, v_conv_w, v_w_out, v_b_out, v_ln_m_g, v_ln_m_b, v_ffc_gate, v_ffc_up, v_ffc_down, v_ln_c_g, v_ln_c_b):
    return _step(dict(locals()))
```

```python
import functools
import math

import jax
import jax.numpy as jnp
from jax import lax
from jax.experimental import pallas as pl
from jax.experimental.pallas import tpu as pltpu

F32 = jnp.float32
BF16 = jnp.bfloat16
MESH = pl.DeviceIdType.MESH
ANY = pl.BlockSpec(memory_space=pl.ANY)

HEAD = 128
CONV_TAPS = 3
LN_EPS = 1e-5
ALPHA = float(2.0 ** 0.25)
ADAM_LR, ADAM_B1, ADAM_B2, ADAM_EPS, ADAM_WD, ADAM_STEP = 0.001, 0.9, 0.999, 1e-08, 0.01, 10

V7X_VMEM_BYTES = 64 * 2 ** 20
N_CHIPS = 4

BIG = ['ffa_gate', 'ffa_up', 'ffa_down', 'w_in', 'w_out', 'ffc_gate', 'ffc_up', 'ffc_down']
WEIGHTS = ['ffa_gate', 'ffa_up', 'ffa_down', 'ln_a_g', 'ln_a_b', 'w_in', 'b_in', 'w_s', 'b_s', 'ln_v_g', 'ln_v_b',
           'conv_w', 'w_out', 'b_out', 'ln_m_g', 'ln_m_b', 'ffc_gate', 'ffc_up', 'ffc_down', 'ln_c_g', 'ln_c_b']
SMALL = [n for n in WEIGHTS if n not in BIG and n != 'conv_w']


def _vmem_limit(estimate_bytes):
    return int(min(max(estimate_bytes * 1.3, 32 * 2 ** 20), V7X_VMEM_BYTES - 6 * 2 ** 20))


def _blk(dim, pref, mult=128):
    if dim <= pref:
        return dim
    best = None
    for d in range(mult, pref + 1, mult):
        if dim % d == 0:
            best = d
    assert best is not None, (dim, pref)
    return best


def _rows(n_rows, row_bytes, target=4 * 2 ** 20):
    return _blk(n_rows, max(16, target // row_bytes), 16)


def _sigmoid(x):
    return 0.5 * jnp.tanh(0.5 * x) + 0.5


_GELU_C = math.sqrt(2.0 / math.pi)
_GELU_A = 0.044715


def _gelu(x):
    return (0.5 * x) * (1.0 + jnp.tanh(x * (_GELU_C + (_GELU_C * _GELU_A) * (x * x))))


def _gelu_and_grad(x):
    x2 = x * x
    t = jnp.tanh(x * (_GELU_C + (_GELU_C * _GELU_A) * x2))
    half_x = 0.5 * x
    one_plus_t = 1.0 + t
    dg = 0.5 * one_plus_t + half_x * (1.0 - t * t) * (_GELU_C + (3.0 * _GELU_C * _GELU_A) * x2)
    return half_x * one_plus_t, dg


def _ln_stats(r):
    mu = jnp.mean(r, axis=-1, keepdims=True)
    d = r - mu
    var = jnp.mean(d * d, axis=-1, keepdims=True)
    rstd = lax.rsqrt(var + LN_EPS)
    return d * rstd, rstd


_DIMS = {'nn': (((1,), (0,)), ((), ())), 'nt': (((1,), (1,)), ((), ())), 'tn': (((0,), (0,)), ((), ()))}


def _mm(name, mode, a_list, b_list, pairs, n_acc, epilogue, out_dtypes, bm, bn, bk,
        tile_extras=(), row_extras=(), b_blocked=False, out_blocked=False, n_outer=False, comms=()):
    a0, b0 = a_list[0], b_list[0]
    if mode == 'nn':
        M, K = a0.shape
        N = b0.shape[0] * b0.shape[2] if b_blocked else b0.shape[1]
    elif mode == 'nt':
        M, K = a0.shape
        N = b0.shape[1] if b_blocked else b0.shape[0]
    else:
        K, M = a0.shape
        N = b0.shape[1]
        assert not b_blocked
    assert M % bm == 0 and N % bn == 0 and K % bk == 0, (name, M, N, K, bm, bn, bk)
    gm, gn, gk = M // bm, N // bn, K // bk
    if b_blocked:
        assert (bn if mode == 'nn' else bk) == b0.shape[2], name
    dims = _DIMS[mode]

    def spec(shape, f):
        if n_outer:
            return pl.BlockSpec(shape, lambda g0, g1, g2: f(g1, g0, g2))
        return pl.BlockSpec(shape, f)

    if mode == 'tn':
        a_spec = spec((bk, bm), lambda i, j, k: (k, i))
    else:
        a_spec = spec((bm, bk), lambda i, j, k: (i, k))
    if mode == 'nn':
        b_spec = (spec((None, bk, bn), lambda i, j, k: (j, k, 0)) if b_blocked
                  else spec((bk, bn), lambda i, j, k: (k, j)))
    elif mode == 'nt':
        b_spec = (spec((None, bn, bk), lambda i, j, k: (k, j, 0)) if b_blocked
                  else spec((bn, bk), lambda i, j, k: (j, k)))
    else:
        b_spec = spec((bk, bn), lambda i, j, k: (k, j))
    te_spec = spec((bm, bn), lambda i, j, k: (i, j))
    re_spec = spec((1, bn), lambda i, j, k: (0, j))
    if out_blocked:
        o_spec = spec((None, bm, bn), lambda i, j, k: (j, i, 0))
        o_shape = (gn, M, bn)
    else:
        o_spec = te_spec
        o_shape = (M, N)

    na, nb, nte, nre, no = len(a_list), len(b_list), len(tile_extras), len(row_extras), len(out_dtypes)

    n_scratch_acc = n_acc if gk > 1 else 0
    grid = (gn, gm, gk) if n_outer else (gm, gn, gk)
    cm_in_arrays, cm_out_shapes, cm_scratch, aliases = _comm_operands(comms, na + nb + nte + nre, no)
    n_cm_in, n_cm_out = len(cm_in_arrays), len(cm_out_shapes)

    def body(*refs):
        a_refs = refs[:na]
        b_refs = refs[na:na + nb]
        te_refs = refs[na + nb:na + nb + nte]
        re_refs = refs[na + nb + nte:na + nb + nte + nre]
        n_in = na + nb + nte + nre
        o_refs = refs[n_in + n_cm_in:n_in + n_cm_in + no]
        acc_refs = refs[n_in + n_cm_in + no + n_cm_out:n_in + n_cm_in + no + n_cm_out + n_scratch_acc]
        hosted = _comm_refs(comms, refs[n_in:n_in + n_cm_in], refs[n_in + n_cm_in + no:n_in + n_cm_in + no + n_cm_out],
                            refs[n_in + n_cm_in + no + n_cm_out + n_scratch_acc:])
        _comm_first_step(comms, hosted, grid)

        def finish(accs):
            outs = epilogue(accs, [r[...] for r in te_refs], [r[...] for r in re_refs])
            for o_ref, o in zip(o_refs, outs):
                o_ref[...] = o.astype(o_ref.dtype)

        def products():
            parts = [None] * n_acc
            for ai, bi, ci in pairs:
                d = lax.dot_general(a_refs[ai][...], b_refs[bi][...], dims, preferred_element_type=F32)
                parts[ci] = d if parts[ci] is None else parts[ci] + d
            return parts

        if gk == 1:
            finish(products())
        else:
            kk = pl.program_id(2)

            @pl.when(kk == 0)
            def _():
                for r, part in zip(acc_refs, products()):
                    r[...] = part

            if gk > 2:
                @pl.when((kk > 0) & (kk < gk - 1))
                def _():
                    for r, part in zip(acc_refs, products()):
                        r[...] += part

            @pl.when(kk == gk - 1)
            def _():
                finish([r[...] + part for r, part in zip(acc_refs, products())])

        _comm_last_step(comms, hosted, grid)

    isz = lambda a: jnp.dtype(a.dtype).itemsize
    est = 2 * (sum(bm * bk * isz(a) for a in a_list) + sum(bk * bn * isz(b) for b in b_list)
               + sum(bm * bn * isz(t) for t in tile_extras)
               + sum(bm * bn * jnp.dtype(d).itemsize for d in out_dtypes))
    est += (2 * n_acc + 2) * bm * bn * 4
    outs = pl.pallas_call(
        body, name=name, grid=grid,
        in_specs=[a_spec] * na + [b_spec] * nb + [te_spec] * nte + [re_spec] * nre + [ANY] * len(cm_in_arrays),
        out_specs=[o_spec] * no + [ANY] * len(cm_out_shapes),
        out_shape=[jax.ShapeDtypeStruct(o_shape, d) for d in out_dtypes] + cm_out_shapes,
        scratch_shapes=[pltpu.VMEM((bm, bn), F32)] * n_scratch_acc + cm_scratch,
        input_output_aliases=aliases,
        compiler_params=pltpu.CompilerParams(dimension_semantics=("parallel", "parallel", "arbitrary"),
                                             vmem_limit_bytes=_vmem_limit(est)),
    )(*a_list, *b_list, *tile_extras, *row_extras, *cm_in_arrays)
    _deliver(comms, outs[no:])
    return outs[:no]


def _rowwise(name, fn, br, row_ins, vec_ins, row_outs, acc_outs=()):
    S = row_ins[0].shape[0]
    assert S % br == 0, (name, S, br)
    nr, nv, no, nacc = len(row_ins), len(vec_ins), len(row_outs), len(acc_outs)

    def body(*refs):
        rin = refs[:nr]
        vin = refs[nr:nr + nv]
        rout = refs[nr + nv:nr + nv + no]
        aout = refs[nr + nv + no:]
        if nacc:
            @pl.when(pl.program_id(0) == 0)
            def _():
                for a in aout:
                    a[...] = jnp.zeros_like(a)
        fn(rin, vin, rout, aout)

    def full_spec(shape):
        nd = len(shape)
        return pl.BlockSpec(shape, lambda i: (0,) * nd)

    est = 2 * (sum(br * a.shape[1] * jnp.dtype(a.dtype).itemsize for a in row_ins)
               + sum(br * c * jnp.dtype(d).itemsize for c, d in row_outs))
    est += 6 * br * max(a.shape[1] for a in row_ins) * 4
    return pl.pallas_call(
        body, name=name, grid=(S // br,),
        in_specs=[pl.BlockSpec((br, a.shape[1]), lambda i: (i, 0)) for a in row_ins]
        + [full_spec(v.shape) for v in vec_ins],
        out_specs=[pl.BlockSpec((br, c), lambda i: (i, 0)) for c, _ in row_outs]
        + [full_spec(s) for s, _ in acc_outs],
        out_shape=[jax.ShapeDtypeStruct((S, c), d) for c, d in row_outs]
        + [jax.ShapeDtypeStruct(s, d) for s, d in acc_outs],
        compiler_params=pltpu.CompilerParams(dimension_semantics=("arbitrary",),
                                             vmem_limit_bytes=_vmem_limit(est)),
    )(*row_ins, *vec_ins)


def _ln_fwd(name, r, g, b):
    def fn(rin, vin, rout, aout):
        xhat, _ = _ln_stats(rin[0][...])
        h = xhat * vin[0][...] + vin[1][...]
        rout[0][...] = h
        rout[1][...] = h.astype(BF16)
    D = r.shape[1]
    return _rowwise(name, fn, _rows(r.shape[0], D * 4), [r], [g, b], [(D, F32), (D, BF16)])


def _ln_loss_bwd(name, r, target, g, b, sub_scale):
    D = r.shape[1]

    def fn(rin, vin, rout, aout):
        xhat, rstd = _ln_stats(rin[0][...])
        gain = vin[0][...]
        err = xhat * gain + vin[1][...] - rin[1][...]
        dhv = err * (1.0 / D)
        dxh = dhv * gain
        m1 = jnp.mean(dxh, axis=-1, keepdims=True)
        m2 = jnp.mean(dxh * xhat, axis=-1, keepdims=True)
        dr = rstd * (dxh - m1 - xhat * m2)
        rout[0][...] = ALPHA * dr
        rout[1][...] = (sub_scale * dr).astype(BF16)
        aout[0][...] += jnp.sum(dhv * xhat, axis=0, keepdims=True)
        aout[1][...] += jnp.sum(dhv, axis=0, keepdims=True)
        aout[2][...] += jnp.sum(err * err, axis=0, keepdims=True)
    return _rowwise(name, fn, _rows(r.shape[0], D * 4, 2 * 2 ** 20), [r, target], [g, b],
                    [(D, F32), (D, BF16)], [((1, D), F32)] * 3)


def _ln_bwd(name, r, dh, g, sub_scale):
    D = r.shape[1]

    def fn(rin, vin, rout, aout):
        xhat, rstd = _ln_stats(rin[0][...])
        dhv = rin[1][...]
        dxh = dhv * vin[0][...]
        m1 = jnp.mean(dxh, axis=-1, keepdims=True)
        m2 = jnp.mean(dxh * xhat, axis=-1, keepdims=True)
        dr = rstd * (dxh - m1 - xhat * m2)
        rout[0][...] = ALPHA * dr
        rout[1][...] = (sub_scale * dr).astype(BF16)
        aout[0][...] += jnp.sum(dhv * xhat, axis=0, keepdims=True)
        aout[1][...] += jnp.sum(dhv, axis=0, keepdims=True)
        aout[2][...] += jnp.sum(dr, axis=0, keepdims=True)
    return _rowwise(name, fn, _rows(r.shape[0], D * 4, 2 * 2 ** 20), [r, dh], [g],
                    [(D, F32), (D, BF16)], [((1, D), F32)] * 3)


def _adamw_math(w, g, m, v):
    m2 = ADAM_B1 * m + (1.0 - ADAM_B1) * g
    v2 = ADAM_B2 * v + (1.0 - ADAM_B2) * (g * g)
    m_hat = m2 / (1.0 - ADAM_B1 ** ADAM_STEP)
    v_hat = v2 / (1.0 - ADAM_B2 ** ADAM_STEP)
    delta = -ADAM_LR * (m_hat / (jnp.sqrt(v_hat) + ADAM_EPS) + ADAM_WD * w)
    return delta, m2, v2


def _adamw(name, w, g, m, v):
    def fn(rin, vin, rout, aout):
        gv = rin[1][...]
        d, m2, v2 = _adamw_math(rin[0][...], gv, rin[2][...], rin[3][...])
        rout[0][...] = gv
        rout[1][...] = d
        rout[2][...] = m2
        rout[3][...] = v2
    C = w.shape[1]
    return _rowwise(name, fn, _rows(w.shape[0], C * 4, 2 * 2 ** 20), [w, g, m, v], [], [(C, F32)] * 4)


def _conv_shifted(hc, p1, p2, rowi):
    r1 = jnp.where(rowi == 0, p1, pltpu.roll(hc, 1, 0))
    r2 = jnp.where(rowi == 0, p2, jnp.where(rowi == 1, p1, pltpu.roll(hc, 2, 0)))
    return r1, r2


def _mix_fwd(z, ws, bsb, gv, bv, cw8, tm):
    S, DIN = z.shape
    DA = gv.shape[1]
    DB = DA
    H = DA // HEAD
    o = 2 * DA
    nch = tm // HEAD

    def body(z_ref, zp_ref, ws_ref, bsb_ref, gv_ref, bv_ref, cw_ref, y_ref):
        i = pl.program_id(0)
        ri = lax.broadcasted_iota(jnp.int32, (HEAD, HEAD), 0)
        ci = lax.broadcasted_iota(jnp.int32, (HEAD, HEAD), 1)
        tril = ri >= ci
        for h in range(H):
            cs = slice(h * HEAD, (h + 1) * HEAD)
            vs = slice(DA + h * HEAD, DA + (h + 1) * HEAD)
            wb = jnp.where(tril, ws_ref[h], 0.0).astype(BF16)
            for n in range(nch):
                rs = slice(n * HEAD, (n + 1) * HEAD)
                u = _gelu(z_ref[rs, cs])
                vhat, _ = _ln_stats(_gelu(z_ref[rs, vs]))
                vn = vhat * gv_ref[:, cs] + bv_ref[:, cs]
                mixed = jnp.dot(wb, vn.astype(BF16), preferred_element_type=F32) + bsb_ref[h]
                y_ref[rs, cs] = (u * mixed).astype(BF16)
        zb = z_ref[:, o:o + DB]
        hc = z_ref[:, o + DB:o + 2 * DB] * z_ref[:, o + 2 * DB:]
        keep = (i > 0).astype(F32)
        p1 = zp_ref[7:8, o + DB:o + 2 * DB] * zp_ref[7:8, o + 2 * DB:] * keep
        p2 = zp_ref[6:7, o + DB:o + 2 * DB] * zp_ref[6:7, o + 2 * DB:] * keep
        rowi = lax.broadcasted_iota(jnp.int32, (tm, DB), 0)
        r1, r2 = _conv_shifted(hc, p1, p2, rowi)
        cv = cw_ref[0:1, :] * r2 + cw_ref[1:2, :] * r1 + cw_ref[2:3, :] * hc
        y_ref[:, DA:] = (zb * cv).astype(BF16)

    full3 = lambda i: (0, 0, 0)
    full2 = lambda i: (0, 0)
    est = 2 * (tm * DIN * 4 + tm * (DA + DB) * 2) + 10 * tm * DB * 4
    return pl.pallas_call(
        body, name="mix_fwd", grid=(S // tm,),
        in_specs=[pl.BlockSpec((tm, DIN), lambda i: (i, 0)),
                  pl.BlockSpec((8, DIN), lambda i: (jnp.maximum(i * (tm // 8) - 1, 0), 0)),
                  pl.BlockSpec(ws.shape, full3), pl.BlockSpec(bsb.shape, full3),
                  pl.BlockSpec(gv.shape, full2), pl.BlockSpec(bv.shape, full2), pl.BlockSpec(cw8.shape, full2)],
        out_specs=pl.BlockSpec((tm, DA + DB), lambda i: (i, 0)),
        out_shape=jax.ShapeDtypeStruct((S, DA + DB), BF16),
        compiler_params=pltpu.CompilerParams(dimension_semantics=("arbitrary",),
                                             vmem_limit_bytes=_vmem_limit(est)),
    )(z, z, ws, bsb, gv, bv, cw8)


def _mix_bwd(z, dy, ws, bsb, gv, bv, cw8, tm):
    S, DIN = z.shape
    DA = gv.shape[1]
    DB = DA
    H = DA // HEAD
    o = 2 * DA
    nch = tm // HEAD
    nblk = S // tm

    def body(z_ref, zp_ref, zn_ref, dy_ref, dyn_ref, ws_ref, bsb_ref, gv_ref, bv_ref, cw_ref,
             dz_ref, dws_ref, dbs_ref, dgv_ref, dbv_ref, dcw_ref, dbin_ref):
        i = pl.program_id(0)

        @pl.when(i == 0)
        def _():
            for r in (dws_ref, dbs_ref, dgv_ref, dbv_ref, dcw_ref, dbin_ref):
                r[...] = jnp.zeros_like(r)

        ri = lax.broadcasted_iota(jnp.int32, (HEAD, HEAD), 0)
        ci = lax.broadcasted_iota(jnp.int32, (HEAD, HEAD), 1)
        tril = ri >= ci
        for h in range(H):
            cs = slice(h * HEAD, (h + 1) * HEAD)
            vs = slice(DA + h * HEAD, DA + (h + 1) * HEAD)
            wb = jnp.where(tril, ws_ref[h], 0.0).astype(BF16)
            gvh = gv_ref[:, cs]
            dws_acc = jnp.zeros((HEAD, HEAD), F32)
            dbs_acc = jnp.zeros((HEAD, HEAD), F32)
            dgv_acc = jnp.zeros((1, HEAD), F32)
            dbv_acc = jnp.zeros((1, HEAD), F32)
            dbu_acc = jnp.zeros((1, HEAD), F32)
            dbvv_acc = jnp.zeros((1, HEAD), F32)
            for n in range(nch):
                rs = slice(n * HEAD, (n + 1) * HEAD)
                u, du_dz = _gelu_and_grad(z_ref[rs, cs])
                v, dv_dz = _gelu_and_grad(z_ref[rs, vs])
                vhat, rstd = _ln_stats(v)
                vnb = (vhat * gvh + bv_ref[:, cs]).astype(BF16)
                mixed = jnp.dot(wb, vnb, preferred_element_type=F32) + bsb_ref[h]
                dya = dy_ref[rs, cs]
                dzu = dya * mixed * du_dz
                dmx = dya * u
                dmxb = dmx.astype(BF16)
                dbs_acc += jnp.broadcast_to(jnp.sum(dmx, axis=1, keepdims=True), (HEAD, HEAD))
                dws_acc += lax.dot_general(dmxb, vnb, _DIMS['nt'], preferred_element_type=F32)
                dvn = lax.dot_general(wb, dmxb, _DIMS['tn'], preferred_element_type=F32)
                dgv_acc += jnp.sum(dvn * vhat, axis=0, keepdims=True)
                dbv_acc += jnp.sum(dvn, axis=0, keepdims=True)
                dxh = dvn * gvh
                m1 = jnp.mean(dxh, axis=-1, keepdims=True)
                m2 = jnp.mean(dxh * vhat, axis=-1, keepdims=True)
                dzv = rstd * (dxh - m1 - vhat * m2) * dv_dz
                dz_ref[rs, cs] = dzu.astype(BF16)
                dz_ref[rs, vs] = dzv.astype(BF16)
                dbu_acc += jnp.sum(dzu, axis=0, keepdims=True)
                dbvv_acc += jnp.sum(dzv, axis=0, keepdims=True)
            dws_ref[h] += jnp.where(tril, dws_acc, 0.0)
            dbs_ref[h] += dbs_acc
            dgv_ref[:, cs] += dgv_acc
            dbv_ref[:, cs] += dbv_acc
            dbin_ref[:, cs] += dbu_acc
            dbin_ref[:, vs] += dbvv_acc

        zb = z_ref[:, o:o + DB]
        zc = z_ref[:, o + DB:o + 2 * DB]
        zx = z_ref[:, o + 2 * DB:]
        hc = zc * zx
        keep = (i > 0).astype(F32)
        p1 = zp_ref[7:8, o + DB:o + 2 * DB] * zp_ref[7:8, o + 2 * DB:] * keep
        p2 = zp_ref[6:7, o + DB:o + 2 * DB] * zp_ref[6:7, o + 2 * DB:] * keep
        rowi = lax.broadcasted_iota(jnp.int32, (tm, DB), 0)
        r1, r2 = _conv_shifted(hc, p1, p2, rowi)
        w0, w1, w2 = cw_ref[0:1, :], cw_ref[1:2, :], cw_ref[2:3, :]
        cv = w0 * r2 + w1 * r1 + w2 * hc
        dyb = dy_ref[:, DA:]
        dzb = dyb * cv
        dcv = dyb * zb
        more = (i < nblk - 1).astype(F32)
        n0 = dyn_ref[0:1, DA:] * zn_ref[0:1, o:o + DB] * more
        n1 = dyn_ref[1:2, DA:] * zn_ref[1:2, o:o + DB] * more
        f1 = jnp.where(rowi == tm - 1, n0, pltpu.roll(dcv, tm - 1, 0))
        f2 = jnp.where(rowi == tm - 1, n1, jnp.where(rowi == tm - 2, n0, pltpu.roll(dcv, tm - 2, 0)))
        dhc = w2 * dcv + w1 * f1 + w0 * f2
        dzc = dhc * zx
        dzx = dhc * zc
        dcw_ref[0:1, :] += jnp.sum(dcv * r2, axis=0, keepdims=True)
        dcw_ref[1:2, :] += jnp.sum(dcv * r1, axis=0, keepdims=True)
        dcw_ref[2:3, :] += jnp.sum(dcv * hc, axis=0, keepdims=True)
        dz_ref[:, o:o + DB] = dzb.astype(BF16)
        dz_ref[:, o + DB:o + 2 * DB] = dzc.astype(BF16)
        dz_ref[:, o + 2 * DB:] = dzx.astype(BF16)
        dbin_ref[:, o:o + DB] += jnp.sum(dzb, axis=0, keepdims=True)
        dbin_ref[:, o + DB:o + 2 * DB] += jnp.sum(dzc, axis=0, keepdims=True)
        dbin_ref[:, o + 2 * DB:] += jnp.sum(dzx, axis=0, keepdims=True)

    full3 = lambda i: (0, 0, 0)
    full2 = lambda i: (0, 0)
    prev8 = lambda i: (jnp.maximum(i * (tm // 8) - 1, 0), 0)
    next8 = lambda i: (jnp.minimum((i + 1) * (tm // 8), S // 8 - 1), 0)
    est = 2 * (tm * DIN * 4 + tm * (DA + DB) * 4 + tm * DIN * 2) + 16 * tm * DB * 4
    return pl.pallas_call(
        body, name="mix_bwd", grid=(nblk,),
        in_specs=[pl.BlockSpec((tm, DIN), lambda i: (i, 0)),
                  pl.BlockSpec((8, DIN), prev8), pl.BlockSpec((8, DIN), next8),
                  pl.BlockSpec((tm, DA + DB), lambda i: (i, 0)), pl.BlockSpec((8, DA + DB), next8),
                  pl.BlockSpec(ws.shape, full3), pl.BlockSpec(bsb.shape, full3),
                  pl.BlockSpec(gv.shape, full2), pl.BlockSpec(bv.shape, full2), pl.BlockSpec(cw8.shape, full2)],
        out_specs=[pl.BlockSpec((tm, DIN), lambda i: (i, 0)),
                   pl.BlockSpec(ws.shape, full3), pl.BlockSpec(ws.shape, full3),
                   pl.BlockSpec(gv.shape, full2), pl.BlockSpec(gv.shape, full2),
                   pl.BlockSpec(cw8.shape, full2), pl.BlockSpec((1, DIN), full2)],
        out_shape=[jax.ShapeDtypeStruct((S, DIN), BF16),
                   jax.ShapeDtypeStruct(ws.shape, F32), jax.ShapeDtypeStruct(ws.shape, F32),
                   jax.ShapeDtypeStruct(gv.shape, F32), jax.ShapeDtypeStruct(gv.shape, F32),
                   jax.ShapeDtypeStruct(cw8.shape, F32), jax.ShapeDtypeStruct((1, DIN), F32)],
        compiler_params=pltpu.CompilerParams(dimension_semantics=("arbitrary",),
                                             vmem_limit_bytes=_vmem_limit(est)),
    )(z, z, z, dy, dy, ws, bsb, gv, bv, cw8)


def _place():
    x, y, c = lax.axis_index("x"), lax.axis_index("y"), lax.axis_index("c")
    others = [(1 - x, y), (x, 1 - y), (1 - x, 1 - y)]
    return x, y, c, 2 * x + y, others, [2 * ox + oy for ox, oy in others]


def _half(c, rows_half):
    return pl.ds(pl.multiple_of(c * rows_half, 16), rows_half)


def _rcopy(src, dst, ssem, rsem, dev):
    return pltpu.make_async_remote_copy(src_ref=src, dst_ref=dst, send_sem=ssem, recv_sem=rsem,
                                        device_id=dev, device_id_type=MESH)


class _Comm:
    def __init__(self, aliased, ins, fresh, n_sems, stages, sink):
        self.aliased, self.ins, self.fresh = list(aliased), list(ins), list(fresh)
        self.n_sems, self.stages, self.sink = n_sems, stages, sink


def _in_hbm(a):
    return pltpu.with_memory_space_constraint(a, pltpu.HBM)


def _comm_operands(comms, n_in, n_out):
    arrays, shapes, scratch, aliases = [], [], [], {}
    for cm in comms:
        for a in cm.aliased:
            aliases[n_in + len(arrays)] = n_out + len(shapes)
            arrays.append(_in_hbm(a))
            shapes.append(pltpu.HBM(a.shape, a.dtype))
        arrays += [_in_hbm(a) for a in cm.ins]
        shapes += [pltpu.HBM(s.shape, s.dtype) for s in cm.fresh]
        scratch += [pltpu.SemaphoreType.DMA((cm.n_sems,))] * 2
    return arrays, shapes, scratch, aliases


def _comm_refs(comms, in_refs, out_refs, sem_refs):
    per, pi, po = [], 0, 0
    for i, cm in enumerate(comms):
        pi += len(cm.aliased)
        ins = in_refs[pi:pi + len(cm.ins)]
        pi += len(cm.ins)
        al = out_refs[po:po + len(cm.aliased)]
        po += len(cm.aliased)
        fr = out_refs[po:po + len(cm.fresh)]
        po += len(cm.fresh)
        per.append((al, ins, fr, sem_refs[2 * i], sem_refs[2 * i + 1]))
    return per


def _comm_first_step(comms, hosted, grid):
    if not comms:
        return
    at_first = functools.reduce(lambda a, b: a & b, [pl.program_id(d) == 0 for d in range(len(grid))])

    @pl.when(at_first)
    def _():
        for cm, refs in zip(comms, hosted):
            for d in cm.stages[0](*refs, False):
                d.start()


def _comm_last_step(comms, hosted, grid):
    if not comms:
        return
    at_last = functools.reduce(lambda a, b: a & b, [pl.program_id(d) == g - 1 for d, g in enumerate(grid)])

    @pl.when(at_last)
    def _():
        for cm, refs in zip(comms, hosted):
            for d in cm.stages[0](*refs, True):
                d.wait()
        for cm, refs in zip(comms, hosted):
            for stage in cm.stages[1:]:
                for d in stage(*refs, False):
                    d.start()
                for d in stage(*refs, True):
                    d.wait()


def _deliver(comms, outs):
    pos = 0
    for cm in comms:
        al = outs[pos:pos + len(cm.aliased)]
        pos += len(cm.aliased)
        fr = outs[pos:pos + len(cm.fresh)]
        pos += len(cm.fresh)
        cm.sink(list(al), list(fr))


def _comm_call(name, comms):
    in_arrays, out_shapes, scratch, aliases = _comm_operands(comms, 0, 0)
    n_in, n_out = len(in_arrays), len(out_shapes)

    def body(*refs):
        hosted = _comm_refs(comms, refs[:n_in], refs[n_in:n_in + n_out], refs[n_in + n_out:])
        for cm, r in zip(comms, hosted):
            for stage in cm.stages:
                for d in stage(*r, False):
                    d.start()
                for d in stage(*r, True):
                    d.wait()

    outs = pl.pallas_call(body, name=name, in_specs=[ANY] * n_in, out_specs=[ANY] * n_out,
                          out_shape=out_shapes, scratch_shapes=scratch, input_output_aliases=aliases)(*in_arrays)
    _deliver(comms, outs)


def _c_gather(fulls, sink, ici, fwd, conv=None):
    n = len(fulls)
    n_ici = 3 * (n + (conv is not None)) if ici else 0

    def stage_ici(al, ins, fr, ssem, rsem, wait):
        x, y, c, chip, others, ochips = _place()
        ds = []
        for m in range(n + (conv is not None)):
            for k in range(3):
                i = 3 * m + k
                if m < n:
                    ah = al[m].shape[1] // 2
                    mine, lands = al[m].at[chip, _half(c, ah)], al[m].at[ochips[k], _half(c, ah)]
                else:
                    mine, lands = al[m].at[chip], al[m].at[ochips[k]]
                ds.append(_rcopy(mine, lands, ssem.at[i], rsem.at[i], (x, y, c)) if wait
                          else _rcopy(mine, mine, ssem.at[i], rsem.at[i], (*others[k], c)))
        return ds

    def stage_fwd(al, ins, fr, ssem, rsem, wait):
        x, y, c, chip, others, ochips = _place()
        ds = []
        for m in range(n):
            ah = al[m].shape[1] // 2
            for k in range(3):
                i = n_ici + 3 * m + k
                got = al[m].at[ochips[k], _half(c, ah)]
                ds.append(_rcopy(got, al[m].at[ochips[k], _half(1 - c, ah)], ssem.at[i], rsem.at[i], (x, y, c)) if wait
                          else _rcopy(got, got, ssem.at[i], rsem.at[i], (x, y, 1 - c)))
        return ds

    stages = ([stage_ici] if ici else []) + ([stage_fwd] if fwd else [])
    return _Comm(list(fulls) + ([conv] if conv is not None else []), [], [], n_ici + (3 * n if fwd else 0),
                 stages, sink)


def _c_rs_pair(parts, sink):
    def stage(al, ins, fr, ssem, rsem, wait):
        x, y, c, _, _, _ = _place()
        return [_rcopy(ins[m].at[:, _half(1 - c, ins[m].shape[1] // 2), :], fr[m], ssem.at[m], rsem.at[m],
                       (x, y, 1 - c)) for m in range(len(ins))]
    fresh = [jax.ShapeDtypeStruct((p.shape[0], p.shape[1] // 2, p.shape[2]), p.dtype) for p in parts]
    return _Comm([], parts, fresh, len(parts), [stage], sink)


def _c_rs_chips(psums, sink):
    def stage(al, ins, fr, ssem, rsem, wait):
        x, y, c, _, others, ochips = _place()
        return [_rcopy(ins[m].at[ochips[k]], fr[m].at[k], ssem.at[3 * m + k], rsem.at[3 * m + k], (*others[k], c))
                for m in range(len(ins)) for k in range(3)]
    fresh = [jax.ShapeDtypeStruct((3,) + p.shape[1:], p.dtype) for p in psums]
    return _Comm([], psums, fresh, 3 * len(psums), [stage], sink)


def _c_exchange(g2s, sink):
    def stage(al, ins, fr, ssem, rsem, wait):
        x, y, c, _, _, _ = _place()
        if wait:
            return [_rcopy(al[m].at[c], al[m].at[1 - c], ssem.at[m], rsem.at[m], (x, y, c)) for m in range(len(al))]
        return [_rcopy(al[m].at[c], al[m].at[c], ssem.at[m], rsem.at[m], (x, y, 1 - c)) for m in range(len(al))]
    return _Comm(g2s, [], [], len(g2s), [stage], sink)


def _c_allgather_small(buf, sink):
    def stage(al, ins, fr, ssem, rsem, wait):
        x, y, c, _, _, _ = _place()
        ds = []
        for r in range(1, 8):
            peer = (1 - x if r & 4 else x, 1 - y if r & 2 else y, 1 - c if r & 1 else c)
            ds.append(_rcopy(al[0].at[0], al[0].at[r], ssem.at[r - 1], rsem.at[r - 1], peer))
        return ds
    return _Comm([buf], [], [], 7, [stage], sink)


def _sum_small(buf, dev_idx):
    _, m, n = buf.shape

    def body(me_ref, b_ref, o_ref):
        me = me_ref[0]
        acc = b_ref[me]
        for d in range(1, 8):
            acc = acc + b_ref[jnp.bitwise_xor(me, d)]
        o_ref[...] = acc

    gs = pltpu.PrefetchScalarGridSpec(
        num_scalar_prefetch=1, grid=(1,),
        in_specs=[pl.BlockSpec(buf.shape, lambda i, me: (0, 0, 0))],
        out_specs=pl.BlockSpec((m, n), lambda i, me: (0, 0)))
    return pl.pallas_call(body, name="sum_small", grid_spec=gs,
                          out_shape=jax.ShapeDtypeStruct((m, n), F32))(dev_idx, buf)


def _sum_pair(name, part, sib, c_idx):
    _, A, B = part.shape
    ah = A // 2
    br = _rows(ah, B * 2)

    def body(c_ref, a_ref, b_ref, o_ref):
        o_ref[...] = (a_ref[...].astype(F32) + b_ref[...].astype(F32)).astype(BF16)

    gs = pltpu.PrefetchScalarGridSpec(
        num_scalar_prefetch=1, grid=(N_CHIPS, ah // br),
        in_specs=[pl.BlockSpec((None, None, br, B), lambda j, i, c: (j, c[0], i, 0)),
                  pl.BlockSpec((None, br, B), lambda j, i, c: (j, i, 0))],
        out_specs=pl.BlockSpec((None, br, B), lambda j, i, c: (j, i, 0)))
    return pl.pallas_call(body, name=name, grid_spec=gs,
                          out_shape=pltpu.HBM((N_CHIPS, ah, B), BF16),
                          compiler_params=pltpu.CompilerParams(vmem_limit_bytes=_vmem_limit(2 * 3 * br * B * 2)),
                          )(c_idx, _in_hbm(part.reshape(N_CHIPS, 2, ah, B)), _in_hbm(sib))


def _sum_chips(name, psum, recv, place_idx):
    _, ah, B = psum.shape
    br = _rows(ah, B * 4)

    def body(c_ref, p_ref, r0_ref, r1_ref, r2_ref, o_ref):
        o_ref[...] = ((p_ref[...].astype(F32) + r0_ref[...].astype(F32)) + r1_ref[...].astype(F32)) \
            + r2_ref[...].astype(F32)

    gs = pltpu.PrefetchScalarGridSpec(
        num_scalar_prefetch=1, grid=(ah // br,),
        in_specs=[pl.BlockSpec((None, br, B), lambda i, c: (c[0], i, 0))]
        + [pl.BlockSpec((None, br, B), functools.partial(lambda i, c, k: (k, i, 0), k=k)) for k in range(3)],
        out_specs=pl.BlockSpec((None, br, B), lambda i, c: (c[1], i, 0)))
    return pl.pallas_call(body, name=name, grid_spec=gs,
                          out_shape=pltpu.HBM((2, ah, B), F32),
                          compiler_params=pltpu.CompilerParams(vmem_limit_bytes=_vmem_limit(2 * br * B * (4 * 2 + 4))),
                          )(place_idx, _in_hbm(psum), *[_in_hbm(recv)] * 3)


def _cast_into(name, a, place_idx):
    A, B = a.shape
    br = _rows(A, B * 4)

    def body(c_ref, a_ref, o_ref):
        o_ref[...] = a_ref[...].astype(BF16)

    gs = pltpu.PrefetchScalarGridSpec(
        num_scalar_prefetch=1, grid=(A // br,),
        in_specs=[pl.BlockSpec((br, B), lambda i, c: (i, 0))],
        out_specs=pl.BlockSpec((None, br, B), lambda i, c: (c[0], i, 0)))
    return pl.pallas_call(body, name=name, grid_spec=gs,
                          out_shape=pltpu.HBM((N_CHIPS, A, B), BF16))(place_idx, a)


def _cast_behind(x, shards, place_idx, comms):
    n_steps = 16
    n = len(shards)
    xv = x
    svs = list(shards)

    def pace(v):
        for steps_per_block in (1, 2, 4, 8, 16):
            blocks = n_steps // steps_per_block
            if v.shape[0] % blocks == 0 and (v.shape[0] // blocks) % 16 == 0:
                return steps_per_block
        raise ValueError(v.shape)

    n_in, n_out = 2 + n, 1 + n
    cm_in_arrays, cm_out_shapes, cm_scratch, aliases = _comm_operands(comms, n_in, n_out)
    n_cm_in, n_cm_out = len(cm_in_arrays), len(cm_out_shapes)
    grid = (n_steps,)

    def body(c_ref, x_ref, *rest):
        s_refs = rest[:n]
        xo_ref = rest[n + n_cm_in]
        so_refs = rest[n + n_cm_in + 1:n + n_cm_in + 1 + n]
        o_end = n + n_cm_in + 1 + n
        hosted = _comm_refs(comms, rest[n:n + n_cm_in], rest[o_end:o_end + n_cm_out], rest[o_end + n_cm_out:])
        _comm_first_step(comms, hosted, grid)
        xo_ref[...] = x_ref[...].astype(BF16)
        for s_ref, o_ref in zip(s_refs, so_refs):
            o_ref[...] = s_ref[...].astype(BF16)
        _comm_last_step(comms, hosted, grid)

    def rows(v):
        return v.shape[0] * pace(v) // n_steps

    def in_spec(v):
        return pl.BlockSpec((rows(v), v.shape[1]), functools.partial(lambda i, c, q: (i // q, 0), q=pace(v)))

    def shard_out_spec(v):
        return pl.BlockSpec((None, rows(v), v.shape[1]),
                            functools.partial(lambda i, c, q: (c[0], i // q, 0), q=pace(v)))

    gs = pltpu.PrefetchScalarGridSpec(
        num_scalar_prefetch=1, grid=grid,
        in_specs=[in_spec(v) for v in [xv] + svs] + [ANY] * n_cm_in,
        out_specs=[in_spec(xv)] + [shard_out_spec(v) for v in svs] + [ANY] * n_cm_out,
        scratch_shapes=cm_scratch)
    outs = pl.pallas_call(
        body, name="cast_behind", grid_spec=gs,
        out_shape=[jax.ShapeDtypeStruct(xv.shape, BF16)]
        + [pltpu.HBM((N_CHIPS,) + v.shape, BF16) for v in svs] + cm_out_shapes,
        input_output_aliases=aliases,
        compiler_params=pltpu.CompilerParams(
            dimension_semantics=("arbitrary",),
            vmem_limit_bytes=_vmem_limit(2 * sum(rows(v) * v.shape[1] * 6 for v in [xv] + svs))),
    )(place_idx, xv, *svs, *cm_in_arrays)
    _deliver(comms, outs[n_out:])
    return outs[0].reshape(x.shape), [o.reshape((N_CHIPS,) + s.shape) for o, s in zip(outs[1:n_out], shards)]


def _adamw_small(gsum, gconv, offsets, ws, ms, vs):
    n = len(ws)

    def body(*refs):
        g_ref, gc_ref = refs[0], refs[1]
        w_refs = refs[2:2 + n]
        m_refs = refs[2 + n:2 + 2 * n]
        v_refs = refs[2 + 2 * n:2 + 3 * n]
        outs = refs[2 + 3 * n:]
        for i in range(n):
            rows = w_refs[i].shape[0]
            g = gc_ref[...] if offsets[i] is None else g_ref[offsets[i]:offsets[i] + rows, :]
            d, m2, v2 = _adamw_math(w_refs[i][...], g, m_refs[i][...], v_refs[i][...])
            outs[4 * i][...] = g
            outs[4 * i + 1][...] = d
            outs[4 * i + 2][...] = m2
            outs[4 * i + 3][...] = v2

    vm = pl.BlockSpec(memory_space=pltpu.VMEM)
    out_shape = []
    for w in ws:
        out_shape += [jax.ShapeDtypeStruct(w.shape, F32)] * 4
    return pl.pallas_call(body, name="adamw_small", in_specs=[vm] * (2 + 3 * n), out_specs=[vm] * (4 * n),
                          out_shape=out_shape)(gsum, gconv, *ws, *ms, *vs)


def _ident(accs, tes, res):
    return [accs[0]]


def _ffn_up(name, hb, wg4, wu4, comms=()):
    S, D = hb.shape

    def epi(accs, tes, res):
        g, u = accs
        return [g, u, g * _sigmoid(g) * u]

    return _mm(name, 'nn', [hb], [wg4, wu4], [(0, 0, 0), (0, 1, 1)], 2, epi, [BF16] * 3,
               bm=_blk(S, 512), bn=wg4.shape[2], bk=D, b_blocked=True, n_outer=True, comms=comms)


def _ffn_gate(name, hb, wg4, comms=()):
    S, D = hb.shape
    return _mm(name, 'nn', [hb], [wg4], [(0, 0, 0)], 1, _ident, [BF16],
               bm=_blk(S, 1024), bn=wg4.shape[2], bk=D, b_blocked=True, n_outer=True, comms=comms)[0]


def _ffn_upmul(name, hb, wu4, gb, comms=()):
    S, D = hb.shape

    def epi(accs, tes, res):
        g = tes[0].astype(F32)
        return [accs[0], g * _sigmoid(g) * accs[0]]

    return _mm(name, 'nn', [hb], [wu4], [(0, 0, 0)], 1, epi, [BF16, BF16],
               bm=_blk(S, 1024), bn=wu4.shape[2], bk=D, tile_extras=[gb], b_blocked=True, n_outer=True, comms=comms)


def _ffn_down(name, ab, wd, hres, comms=()):
    S, F = ab.shape
    D = wd.shape[1]
    return _mm(name, 'nn', [ab], [wd], [(0, 0, 0)], 1, lambda accs, tes, res: [ALPHA * tes[0] + 0.5 * accs[0]],
               [F32], bm=_blk(S, 1024), bn=_blk(D, 512), bk=F, tile_extras=[hres], comms=comms)[0]


def _ffn_da(name, dsub, wd, gb, ub, comms=()):
    S, D = dsub.shape
    F = wd.shape[0]

    def epi(accs, tes, res):
        da = accs[0]
        g = tes[0].astype(F32)
        u = tes[1].astype(F32)
        s = _sigmoid(g)
        silu = g * s
        return [(da * u) * (s + silu * (1.0 - s)), da * silu]

    return _mm(name, 'nt', [dsub], [wd], [(0, 0, 0)], 1, epi, [BF16, BF16],
               bm=_blk(S, 1024), bn=_blk(F, 512), bk=D, tile_extras=[gb, ub], comms=comms)


def _dw_rows(name, ab, dsub, cs, comms=()):
    S, D = dsub.shape
    return _mm(name, 'tn', [ab], [dsub], [(0, 0, 0)], 1, _ident, [BF16],
               bm=cs, bn=_blk(D, 1024), bk=_blk(S, 2048), comms=comms)[0].reshape(N_CHIPS, cs, D)


def _dw_cols(name, hb, dxb, cs, comms=()):
    S, D = hb.shape
    return _mm(name, 'tn', [hb], [dxb], [(0, 0, 0)], 1, _ident, [BF16],
               bm=_blk(D, 1024), bn=cs, bk=_blk(S, 2048), out_blocked=True, comms=comms)[0]


def _ffn_dh(name, dgb, dub, wg4, wu4, dres, comms=()):
    S = dgb.shape[0]
    D = wg4.shape[1]
    return _mm(name, 'nt', [dgb, dub], [wg4, wu4], [(0, 0, 0), (1, 1, 0)], 1,
               lambda accs, tes, res: [accs[0] + tes[0]], [F32],
               bm=_blk(S, 1024), bn=_blk(D, 1024), bk=wg4.shape[2], tile_extras=[dres], b_blocked=True,
               comms=comms)[0]


def _step(p):
    x = p['x'][0]
    target = p['loss_target'][0]
    S, D = x.shape
    cx, cy, cc = lax.axis_index("x"), lax.axis_index("y"), lax.axis_index("c")
    chip = 2 * cx + cy
    c_idx = jnp.reshape(cc, (1,)).astype(jnp.int32)
    place_idx = jnp.stack([chip, cc]).astype(jnp.int32)

    w_s = p['w_s'][0]
    H = w_s.shape[0]
    DA = H * HEAD
    bsb = jnp.broadcast_to(p['b_s'][0][:, :, None], (H, HEAD, HEAD))
    conv_shard = p['conv_w'][0]
    conv8 = jnp.zeros((8, conv_shard.shape[1]), F32).at[:CONV_TAPS].set(conv_shard)

    W = {'ffa_gate': _cast_into("cast_ffa_gate", p['ffa_gate'][0], place_idx)}
    conv = {'full': lax.dynamic_update_slice(jnp.zeros((N_CHIPS,) + conv8.shape, F32), conv8[None], (chip, 0, 0))}

    def gathered(names, with_conv=False):
        def sink(al, fr):
            W.update(zip(names, al))
            if with_conv:
                conv['full'] = al[len(names)]
        return sink

    def c_gather(names, ici, fwd, with_conv=False):
        return _c_gather([W[n] for n in names], gathered(names, with_conv), ici, fwd,
                         conv['full'] if with_conv else None)

    rest = [n for n in BIG if n != 'ffa_gate']
    xb, casted = _cast_behind(x, [p[n][0] for n in rest], place_idx, comms=[c_gather(['ffa_gate'], True, True)])
    W.update(zip(rest, casted))
    cs_ff = W['ffa_gate'].shape[2]
    F = N_CHIPS * cs_ff

    ga = _ffn_gate("ffa_gate", xb, W['ffa_gate'],
                   comms=[c_gather(['ffa_up'], True, True), c_gather(['ffa_down'], True, False)])
    ua, aa = _ffn_upmul("ffa_up", xb, W['ffa_up'], ga,
                        comms=[c_gather(['ffa_down'], False, True),
                               c_gather(['w_in', 'ffc_gate'], True, False, with_conv=True)])
    cw8 = jnp.transpose(conv['full'], (1, 0, 2)).reshape(8, DA)
    wd_a = W['ffa_down'].reshape(F, D)
    r1 = _ffn_down("ffa_down", aa, wd_a, x,
                   comms=[c_gather(['w_in', 'ffc_gate'], False, True), c_gather(['w_out', 'ffc_down'], True, False)])
    h1, h1b = _ln_fwd("ln_a", r1, p['ln_a_g'], p['ln_a_b'])
    w_in4 = W['w_in']
    cs_in = w_in4.shape[2]
    z = _mm("w_in", 'nn', [h1b], [w_in4], [(0, 0, 0)], 1, lambda accs, tes, res: [accs[0] + res[0]], [F32],
            bm=_blk(S, 1024), bn=cs_in, bk=D, row_extras=[p['b_in']], b_blocked=True, n_outer=True,
            comms=[c_gather(['w_out', 'ffc_down'], False, True), c_gather(['ffc_up'], True, False)])[0]
    w_out = W['w_out'].reshape(DA * 2, D)
    tm = _blk(S, 256)
    yb = _mix_fwd(z, w_s, bsb, p['ln_v_g'], p['ln_v_b'], cw8, tm)
    r2 = _mm("w_out", 'nn', [yb], [w_out], [(0, 0, 0)], 1,
             lambda accs, tes, res: [accs[0] + res[0] + ALPHA * tes[0]], [F32],
             bm=_blk(S, 1024), bn=_blk(D, 1024), bk=2 * DA, tile_extras=[h1], row_extras=[p['b_out']],
             comms=[c_gather(['ffc_up'], False, True)])[0]
    h2, h2b = _ln_fwd("ln_m", r2, p['ln_m_g'], p['ln_m_b'])
    gc, uc, ac = _ffn_up("ffc_up", h2b, W['ffc_gate'], W['ffc_up'])
    wd_c = W['ffc_down'].reshape(F, D)
    r3 = _ffn_down("ffc_down", ac, wd_c, h2)

    pg, sg, sib, psum, recv, g2 = {}, {}, {}, {}, {}, {}

    def c_pair(names):
        return _c_rs_pair([pg[n] for n in names], lambda al, fr: sib.update(zip(names, fr)))

    def c_chips(names):
        return _c_rs_chips([psum[n] for n in names], lambda al, fr: recv.update(zip(names, fr)))

    def c_swap(names):
        return _c_exchange([g2[n] for n in names], lambda al, fr: g2.update(zip(names, al)))

    def sum_pair(*names):
        for n in names:
            psum[n] = _sum_pair(f"sum_pair_{n}", pg[n], sib[n], c_idx)

    def sum_chips(*names):
        for n in names:
            g2[n] = _sum_chips(f"sum_chips_{n}", psum[n], recv[n], place_idx)

    dres3, dsub3, sg['ln_c_g'], sg['ln_c_b'], lsum = _ln_loss_bwd(
        "ln_c_loss_bwd", r3, target, p['ln_c_g'], p['ln_c_b'], 0.5)

    dgc, duc = _ffn_da("ffc_da", dsub3, wd_c, gc, uc)
    pg['ffc_down'] = _dw_rows("ffc_dwd", ac, dsub3, cs_ff)
    pg['ffc_gate'] = _dw_cols("ffc_dwg", h2b, dgc, cs_ff, comms=[c_pair(['ffc_down'])])
    sum_pair('ffc_down')
    pg['ffc_up'] = _dw_cols("ffc_dwu", h2b, duc, cs_ff, comms=[c_pair(['ffc_gate']), c_chips(['ffc_down'])])
    sum_pair('ffc_gate')
    dh2 = _ffn_dh("ffc_dh", dgc, duc, W['ffc_gate'], W['ffc_up'], dres3,
                  comms=[c_pair(['ffc_up']), c_chips(['ffc_gate'])])
    sum_pair('ffc_up')
    sum_chips('ffc_down', 'ffc_gate')

    dres2, dmix, sg['ln_m_g'], sg['ln_m_b'], sg['b_out'] = _ln_bwd("ln_m_bwd", r2, dh2, p['ln_m_g'], 1.0)
    pg['w_out'] = _mm("dw_out", 'tn', [yb], [dmix], [(0, 0, 0)], 1, _ident, [BF16],
                      bm=_blk(2 * DA, 1024), bn=_blk(D, 1024), bk=_blk(S, 2048),
                      comms=[c_swap(['ffc_down', 'ffc_gate'])])[0].reshape(N_CHIPS, 2 * DA // 4, D)
    dy = _mm("dy", 'nt', [dmix], [w_out], [(0, 0, 0)], 1, _ident, [F32],
             bm=_blk(S, 1024), bn=_blk(2 * DA, 1024), bk=D, comms=[c_pair(['w_out'])])[0]
    sum_pair('w_out')
    dz, dws, dbs, sg['ln_v_g'], sg['ln_v_b'], dcw, sg['b_in'] = _mix_bwd(
        z, dy, w_s, bsb, p['ln_v_g'], p['ln_v_b'], cw8, tm)
    sg['w_s'] = dws
    sg['b_s'] = dbs[:, :, 0]
    pg['w_in'] = _dw_cols("dw_in", h1b, dz, cs_in, comms=[c_chips(['ffc_up']), c_chips(['w_out'])])
    sum_chips('ffc_up', 'w_out')
    dh1 = _mm("dh1", 'nt', [dz], [w_in4], [(0, 0, 0)], 1, lambda accs, tes, res: [accs[0] + tes[0]], [F32],
              bm=_blk(S, 1024), bn=_blk(D, 1024), bk=cs_in, tile_extras=[dres2], b_blocked=True,
              comms=[c_pair(['w_in']), c_swap(['ffc_up', 'w_out'])])[0]
    sum_pair('w_in')

    dres1, dsub1, sg['ln_a_g'], sg['ln_a_b'], _ = _ln_bwd("ln_a_bwd", r1, dh1, p['ln_a_g'], 0.5)
    def as_rows(a):
        a = a.reshape(-1, 128)
        pad = (-a.shape[0]) % 8
        return jnp.pad(a, ((0, pad), (0, 0))) if pad else a

    pieces, offsets, off = [], {}, 0
    for n in SMALL + ['conv_w']:
        piece = as_rows(dcw[:CONV_TAPS] if n == 'conv_w' else sg[n])
        offsets[n] = off
        off += piece.shape[0]
        pieces.append(piece)
    packed = jnp.concatenate(pieces, axis=0)
    small = {'buf': jnp.zeros((8,) + packed.shape, F32).at[0].set(packed)}
    dga, dua = _ffn_da("ffa_da", dsub1, wd_a, ga, ua,
                       comms=[c_chips(['w_in']), _c_allgather_small(small['buf'], lambda al, fr: small.update(buf=al[0]))])
    sum_chips('w_in')
    gsum = _sum_small(small['buf'], jnp.reshape(4 * cx + 2 * cy + cc, (1,)).astype(jnp.int32))
    pg['ffa_down'] = _dw_rows("ffa_dwd", aa, dsub1, cs_ff, comms=[c_swap(['w_in'])])
    pg['ffa_gate'] = _dw_cols("ffa_dwg", xb, dga, cs_ff, comms=[c_pair(['ffa_down'])])
    sum_pair('ffa_down')
    pg['ffa_up'] = _dw_cols("ffa_dwu", xb, dua, cs_ff, comms=[c_pair(['ffa_gate']), c_chips(['ffa_down'])])
    sum_pair('ffa_gate')
    _comm_call("pair_ffa_up", [c_pair(['ffa_up'])])
    sum_pair('ffa_up')
    sum_chips('ffa_down')
    dx = _ffn_dh("ffa_dh", dga, dua, W['ffa_gate'], W['ffa_up'], dres1,
                 comms=[c_chips(['ffa_gate']), c_chips(['ffa_up']), c_swap(['ffa_down'])])
    sum_chips('ffa_gate', 'ffa_up')
    _comm_call("swap_ffa", [c_swap(['ffa_gate', 'ffa_up'])])

    out = {}
    for n in BIG:
        g, d, m2, v2 = _adamw(f"adamw_{n}", p[n][0], g2[n].reshape(p[n][0].shape), p['m_' + n][0], p['v_' + n][0])
        shp = p[n].shape
        out[n] = (g.reshape(shp), d.reshape(shp), m2.reshape(shp), v2.reshape(shp))

    loss = lax.psum(0.5 * jnp.sum(lsum) / D, ("x", "y", "c"))
    n_conv = CONV_TAPS * DA // 128
    conv_sum = gsum[offsets['conv_w']:offsets['conv_w'] + n_conv].reshape(CONV_TAPS, DA)
    cw = conv_shard.shape[1]
    gconv = lax.dynamic_slice_in_dim(conv_sum, (2 * cx + cy) * cw, cw, axis=1).reshape(-1, 128)
    names = SMALL + ['conv_w']
    rows128 = lambda a: a.reshape(-1, 128)
    res = _adamw_small(gsum, gconv, [offsets[n] if n != 'conv_w' else None for n in names],
                       [rows128(p[n]) for n in names], [rows128(p['m_' + n]) for n in names],
                       [rows128(p['v_' + n]) for n in names])
    for i, n in enumerate(names):
        out[n] = tuple(r.reshape(p[n].shape) for r in res[4 * i:4 * i + 4])

    return (loss, dx.reshape(p['x'].shape), *[out[n][0] for n in WEIGHTS], *[out[n][1] for n in WEIGHTS],
            *[out[n][2] for n in WEIGHTS], *[out[n][3] for n in WEIGHTS])


def kernel(x, ffa_gate, ffa_up, ffa_down, ln_a_g, ln_a_b, w_in, b_in, w_s, b_s, ln_v_g, ln_v_b, conv_w, w_out, b_out, ln_m_g, ln_m_b, ffc_gate, ffc_up, ffc_down, ln_c_g, ln_c_b, loss_target, m_ffa_gate, m_ffa_up, m_ffa_down, m_ln_a_g, m_ln_a_b, m_w_in, m_b_in, m_w_s, m_b_s, m_ln_v_g, m_ln_v_b, m_conv_w, m_w_out, m_b_out, m_ln_m_g, m_ln_m_b, m_ffc_gate, m_ffc_up, m_ffc_down, m_ln_c_g, m_ln_c_b, v_ffa_gate, v_ffa_up, v_ffa_down, v_ln_a_g, v_ln_a_b, v_w_in, v_b_in, v_w_s, v_b_s, v_ln_v_g, v_ln_v_b, v_conv_w, v_w_out, v_b_out, v_ln_m_g, v_ln_m_b, v_ffc_gate, v_ffc_up, v_ffc_down, v_ln_c_g, v_ln_c_b):
    return _step(dict(locals()))
```

```python
import functools
import math

import jax
import jax.numpy as jnp
from jax import lax
from jax.experimental import pallas as pl
from jax.experimental.pallas import tpu as pltpu

F32 = jnp.float32
BF16 = jnp.bfloat16
MESH = pl.DeviceIdType.MESH
ANY = pl.BlockSpec(memory_space=pl.ANY)

HEAD = 128
CONV_TAPS = 3
LN_EPS = 1e-5
ALPHA = float(2.0 ** 0.25)
ADAM_LR, ADAM_B1, ADAM_B2, ADAM_EPS, ADAM_WD, ADAM_STEP = 0.001, 0.9, 0.999, 1e-08, 0.01, 10

V7X_VMEM_BYTES = 64 * 2 ** 20
VMEM_COMPILER_RESERVE = 6 * 2 ** 20
VMEM_MIN_REQUEST = 32 * 2 ** 20
VMEM_ESTIMATE_SLACK = 1.3
N_CHIPS = 4

BIG = ['ffa_gate', 'ffa_up', 'ffa_down', 'w_in', 'w_out', 'ffc_gate', 'ffc_up', 'ffc_down']
WEIGHTS = ['ffa_gate', 'ffa_up', 'ffa_down', 'ln_a_g', 'ln_a_b', 'w_in', 'b_in', 'w_s', 'b_s', 'ln_v_g', 'ln_v_b',
           'conv_w', 'w_out', 'b_out', 'ln_m_g', 'ln_m_b', 'ffc_gate', 'ffc_up', 'ffc_down', 'ln_c_g', 'ln_c_b']
SMALL = [n for n in WEIGHTS if n not in BIG and n != 'conv_w']


def _vmem_limit(estimate_bytes):
    return int(min(max(estimate_bytes * VMEM_ESTIMATE_SLACK, VMEM_MIN_REQUEST),
                   V7X_VMEM_BYTES - VMEM_COMPILER_RESERVE))


def _blk(dim, pref, mult=128):
    if dim <= pref:
        return dim
    best = None
    for d in range(mult, pref + 1, mult):
        if dim % d == 0:
            best = d
    assert best is not None, (dim, pref)
    return best


def _rows(n_rows, row_bytes, target=4 * 2 ** 20):
    return _blk(n_rows, max(16, target // row_bytes), 16)


def _sigmoid(x):
    return 0.5 * jnp.tanh(0.5 * x) + 0.5


_GELU_C = math.sqrt(2.0 / math.pi)
_GELU_A = 0.044715


def _gelu(x):
    return (0.5 * x) * (1.0 + jnp.tanh(x * (_GELU_C + (_GELU_C * _GELU_A) * (x * x))))


def _gelu_and_grad(x):
    x2 = x * x
    t = jnp.tanh(x * (_GELU_C + (_GELU_C * _GELU_A) * x2))
    half_x = 0.5 * x
    one_plus_t = 1.0 + t
    dg = 0.5 * one_plus_t + half_x * (1.0 - t * t) * (_GELU_C + (3.0 * _GELU_C * _GELU_A) * x2)
    return half_x * one_plus_t, dg


def _ln_stats(r):
    mu = jnp.mean(r, axis=-1, keepdims=True)
    d = r - mu
    var = jnp.mean(d * d, axis=-1, keepdims=True)
    rstd = lax.rsqrt(var + LN_EPS)
    return d * rstd, rstd


_DIMS = {'nn': (((1,), (0,)), ((), ())), 'nt': (((1,), (1,)), ((), ())), 'tn': (((0,), (0,)), ((), ()))}


def _mm(name, mode, a_list, b_list, pairs, n_acc, epilogue, out_dtypes, bm, bn, bk,
        tile_extras=(), row_extras=(), b_blocked=False, out_blocked=False, n_outer=False, comms=()):
    a0, b0 = a_list[0], b_list[0]
    if mode == 'nn':
        M, K = a0.shape
        N = b0.shape[0] * b0.shape[2] if b_blocked else b0.shape[1]
    elif mode == 'nt':
        M, K = a0.shape
        N = b0.shape[1] if b_blocked else b0.shape[0]
    else:
        K, M = a0.shape
        N = b0.shape[1]
        assert not b_blocked
    assert M % bm == 0 and N % bn == 0 and K % bk == 0, (name, M, N, K, bm, bn, bk)
    gm, gn, gk = M // bm, N // bn, K // bk
    if b_blocked:
        assert (bn if mode == 'nn' else bk) == b0.shape[2], name
    dims = _DIMS[mode]

    def spec(shape, f):
        if n_outer:
            return pl.BlockSpec(shape, lambda g0, g1, g2: f(g1, g0, g2))
        return pl.BlockSpec(shape, f)

    if mode == 'tn':
        a_spec = spec((bk, bm), lambda i, j, k: (k, i))
    else:
        a_spec = spec((bm, bk), lambda i, j, k: (i, k))
    if mode == 'nn':
        b_spec = (spec((None, bk, bn), lambda i, j, k: (j, k, 0)) if b_blocked
                  else spec((bk, bn), lambda i, j, k: (k, j)))
    elif mode == 'nt':
        b_spec = (spec((None, bn, bk), lambda i, j, k: (k, j, 0)) if b_blocked
                  else spec((bn, bk), lambda i, j, k: (j, k)))
    else:
        b_spec = spec((bk, bn), lambda i, j, k: (k, j))
    te_spec = spec((bm, bn), lambda i, j, k: (i, j))
    re_spec = spec((1, bn), lambda i, j, k: (0, j))
    if out_blocked:
        o_spec = spec((None, bm, bn), lambda i, j, k: (j, i, 0))
        o_shape = (gn, M, bn)
    else:
        o_spec = te_spec
        o_shape = (M, N)

    na, nb, nte, nre, no = len(a_list), len(b_list), len(tile_extras), len(row_extras), len(out_dtypes)

    n_scratch_acc = n_acc if gk > 1 else 0
    grid = (gn, gm, gk) if n_outer else (gm, gn, gk)
    cm_in_arrays, cm_out_shapes, cm_scratch, aliases = _comm_operands(comms, na + nb + nte + nre, no)
    n_cm_in, n_cm_out = len(cm_in_arrays), len(cm_out_shapes)

    def body(*refs):
        a_refs = refs[:na]
        b_refs = refs[na:na + nb]
        te_refs = refs[na + nb:na + nb + nte]
        re_refs = refs[na + nb + nte:na + nb + nte + nre]
        n_in = na + nb + nte + nre
        o_refs = refs[n_in + n_cm_in:n_in + n_cm_in + no]
        acc_refs = refs[n_in + n_cm_in + no + n_cm_out:n_in + n_cm_in + no + n_cm_out + n_scratch_acc]
        hosted = _comm_refs(comms, refs[n_in:n_in + n_cm_in], refs[n_in + n_cm_in + no:n_in + n_cm_in + no + n_cm_out],
                            refs[n_in + n_cm_in + no + n_cm_out + n_scratch_acc:])
        _comm_first_step(comms, hosted, grid)
        _comm_mid_step(comms, hosted, grid)

        def finish(accs):
            outs = epilogue(accs, [r[...] for r in te_refs], [r[...] for r in re_refs])
            for o_ref, o in zip(o_refs, outs):
                o_ref[...] = o.astype(o_ref.dtype)

        def products():
            parts = [None] * n_acc
            for ai, bi, ci in pairs:
                d = lax.dot_general(a_refs[ai][...], b_refs[bi][...], dims, preferred_element_type=F32)
                parts[ci] = d if parts[ci] is None else parts[ci] + d
            return parts

        if gk == 1:
            finish(products())
        else:
            kk = pl.program_id(2)

            @pl.when(kk == 0)
            def _():
                for r, part in zip(acc_refs, products()):
                    r[...] = part

            if gk > 2:
                @pl.when((kk > 0) & (kk < gk - 1))
                def _():
                    for r, part in zip(acc_refs, products()):
                        r[...] += part

            @pl.when(kk == gk - 1)
            def _():
                finish([r[...] + part for r, part in zip(acc_refs, products())])

        _comm_last_step(comms, hosted, grid)

    isz = lambda a: jnp.dtype(a.dtype).itemsize
    est = 2 * (sum(bm * bk * isz(a) for a in a_list) + sum(bk * bn * isz(b) for b in b_list)
               + sum(bm * bn * isz(t) for t in tile_extras)
               + sum(bm * bn * jnp.dtype(d).itemsize for d in out_dtypes))
    est += (2 * n_acc + 2) * bm * bn * 4
    outs = pl.pallas_call(
        body, name=name, grid=grid,
        in_specs=[a_spec] * na + [b_spec] * nb + [te_spec] * nte + [re_spec] * nre + [ANY] * len(cm_in_arrays),
        out_specs=[o_spec] * no + [ANY] * len(cm_out_shapes),
        out_shape=[jax.ShapeDtypeStruct(o_shape, d) for d in out_dtypes] + cm_out_shapes,
        scratch_shapes=[pltpu.VMEM((bm, bn), F32)] * n_scratch_acc + cm_scratch,
        input_output_aliases=aliases,
        compiler_params=pltpu.CompilerParams(dimension_semantics=("parallel", "parallel", "arbitrary"),
                                             vmem_limit_bytes=_vmem_limit(est)),
    )(*a_list, *b_list, *tile_extras, *row_extras, *cm_in_arrays)
    _deliver(comms, outs[no:])
    return outs[:no]


def _rowwise(name, fn, br, row_ins, vec_ins, row_outs, acc_outs=()):
    S = row_ins[0].shape[0]
    assert S % br == 0, (name, S, br)
    nr, nv, no, nacc = len(row_ins), len(vec_ins), len(row_outs), len(acc_outs)

    def body(*refs):
        rin = refs[:nr]
        vin = refs[nr:nr + nv]
        rout = refs[nr + nv:nr + nv + no]
        aout = refs[nr + nv + no:]
        if nacc:
            @pl.when(pl.program_id(0) == 0)
            def _():
                for a in aout:
                    a[...] = jnp.zeros_like(a)
        fn(rin, vin, rout, aout)

    def full_spec(shape):
        nd = len(shape)
        return pl.BlockSpec(shape, lambda i: (0,) * nd)

    est = 2 * (sum(br * a.shape[1] * jnp.dtype(a.dtype).itemsize for a in row_ins)
               + sum(br * c * jnp.dtype(d).itemsize for c, d in row_outs))
    est += 6 * br * max(a.shape[1] for a in row_ins) * 4
    return pl.pallas_call(
        body, name=name, grid=(S // br,),
        in_specs=[pl.BlockSpec((br, a.shape[1]), lambda i: (i, 0)) for a in row_ins]
        + [full_spec(v.shape) for v in vec_ins],
        out_specs=[pl.BlockSpec((br, c), lambda i: (i, 0)) for c, _ in row_outs]
        + [full_spec(s) for s, _ in acc_outs],
        out_shape=[jax.ShapeDtypeStruct((S, c), d) for c, d in row_outs]
        + [jax.ShapeDtypeStruct(s, d) for s, d in acc_outs],
        compiler_params=pltpu.CompilerParams(dimension_semantics=("arbitrary",),
                                             vmem_limit_bytes=_vmem_limit(est)),
    )(*row_ins, *vec_ins)


def _ln_fwd(name, r, g, b):
    def fn(rin, vin, rout, aout):
        xhat, _ = _ln_stats(rin[0][...])
        h = xhat * vin[0][...] + vin[1][...]
        rout[0][...] = h
        rout[1][...] = h.astype(BF16)
    D = r.shape[1]
    return _rowwise(name, fn, _rows(r.shape[0], D * 4), [r], [g, b], [(D, F32), (D, BF16)])


def _ln_loss_bwd(name, r, target, g, b, sub_scale):
    D = r.shape[1]

    def fn(rin, vin, rout, aout):
        xhat, rstd = _ln_stats(rin[0][...])
        gain = vin[0][...]
        err = xhat * gain + vin[1][...] - rin[1][...]
        dhv = err * (1.0 / D)
        dxh = dhv * gain
        m1 = jnp.mean(dxh, axis=-1, keepdims=True)
        m2 = jnp.mean(dxh * xhat, axis=-1, keepdims=True)
        dr = rstd * (dxh - m1 - xhat * m2)
        rout[0][...] = ALPHA * dr
        rout[1][...] = (sub_scale * dr).astype(BF16)
        aout[0][...] += jnp.sum(dhv * xhat, axis=0, keepdims=True)
        aout[1][...] += jnp.sum(dhv, axis=0, keepdims=True)
        aout[2][...] += jnp.sum(err * err, axis=0, keepdims=True)
    return _rowwise(name, fn, _rows(r.shape[0], D * 4, 2 * 2 ** 20), [r, target], [g, b],
                    [(D, F32), (D, BF16)], [((1, D), F32)] * 3)


def _ln_bwd(name, r, dh, g, sub_scale):
    D = r.shape[1]

    def fn(rin, vin, rout, aout):
        xhat, rstd = _ln_stats(rin[0][...])
        dhv = rin[1][...]
        dxh = dhv * vin[0][...]
        m1 = jnp.mean(dxh, axis=-1, keepdims=True)
        m2 = jnp.mean(dxh * xhat, axis=-1, keepdims=True)
        dr = rstd * (dxh - m1 - xhat * m2)
        rout[0][...] = ALPHA * dr
        rout[1][...] = (sub_scale * dr).astype(BF16)
        aout[0][...] += jnp.sum(dhv * xhat, axis=0, keepdims=True)
        aout[1][...] += jnp.sum(dhv, axis=0, keepdims=True)
        aout[2][...] += jnp.sum(dr, axis=0, keepdims=True)
    return _rowwise(name, fn, _rows(r.shape[0], D * 4, 2 * 2 ** 20), [r, dh], [g],
                    [(D, F32), (D, BF16)], [((1, D), F32)] * 3)


def _adamw_math(w, g, m, v):
    m2 = ADAM_B1 * m + (1.0 - ADAM_B1) * g
    v2 = ADAM_B2 * v + (1.0 - ADAM_B2) * (g * g)
    m_hat = m2 / (1.0 - ADAM_B1 ** ADAM_STEP)
    v_hat = v2 / (1.0 - ADAM_B2 ** ADAM_STEP)
    delta = -ADAM_LR * (m_hat / (jnp.sqrt(v_hat) + ADAM_EPS) + ADAM_WD * w)
    return delta, m2, v2


def _adamw(name, w, g, m, v):
    def fn(rin, vin, rout, aout):
        gv = rin[1][...]
        d, m2, v2 = _adamw_math(rin[0][...], gv, rin[2][...], rin[3][...])
        rout[0][...] = gv
        rout[1][...] = d
        rout[2][...] = m2
        rout[3][...] = v2
    C = w.shape[1]
    return _rowwise(name, fn, _rows(w.shape[0], C * 4, 2 * 2 ** 20), [w, g, m, v], [], [(C, F32)] * 4)


def _conv_shifted(hc, p1, p2, rowi):
    r1 = jnp.where(rowi == 0, p1, pltpu.roll(hc, 1, 0))
    r2 = jnp.where(rowi == 0, p2, jnp.where(rowi == 1, p1, pltpu.roll(hc, 2, 0)))
    return r1, r2


def _mix_fwd(z, ws, bsb, gv, bv, cw8, tm):
    S, DIN = z.shape
    DA = gv.shape[1]
    DB = DA
    H = DA // HEAD
    o = 2 * DA
    nch = tm // HEAD

    def body(z_ref, zp_ref, ws_ref, bsb_ref, gv_ref, bv_ref, cw_ref, y_ref):
        i = pl.program_id(0)
        ri = lax.broadcasted_iota(jnp.int32, (HEAD, HEAD), 0)
        ci = lax.broadcasted_iota(jnp.int32, (HEAD, HEAD), 1)
        tril = ri >= ci
        for h in range(H):
            cs = slice(h * HEAD, (h + 1) * HEAD)
            vs = slice(DA + h * HEAD, DA + (h + 1) * HEAD)
            wb = jnp.where(tril, ws_ref[h], 0.0).astype(BF16)
            for n in range(nch):
                rs = slice(n * HEAD, (n + 1) * HEAD)
                u = _gelu(z_ref[rs, cs])
                vhat, _ = _ln_stats(_gelu(z_ref[rs, vs]))
                vn = vhat * gv_ref[:, cs] + bv_ref[:, cs]
                mixed = jnp.dot(wb, vn.astype(BF16), preferred_element_type=F32) + bsb_ref[h]
                y_ref[rs, cs] = (u * mixed).astype(BF16)
        zb = z_ref[:, o:o + DB]
        hc = z_ref[:, o + DB:o + 2 * DB] * z_ref[:, o + 2 * DB:]
        keep = (i > 0).astype(F32)
        p1 = zp_ref[7:8, o + DB:o + 2 * DB] * zp_ref[7:8, o + 2 * DB:] * keep
        p2 = zp_ref[6:7, o + DB:o + 2 * DB] * zp_ref[6:7, o + 2 * DB:] * keep
        rowi = lax.broadcasted_iota(jnp.int32, (tm, DB), 0)
        r1, r2 = _conv_shifted(hc, p1, p2, rowi)
        cv = cw_ref[0:1, :] * r2 + cw_ref[1:2, :] * r1 + cw_ref[2:3, :] * hc
        y_ref[:, DA:] = (zb * cv).astype(BF16)

    full3 = lambda i: (0, 0, 0)
    full2 = lambda i: (0, 0)
    est = 2 * (tm * DIN * 4 + tm * (DA + DB) * 2) + 10 * tm * DB * 4
    return pl.pallas_call(
        body, name="mix_fwd", grid=(S // tm,),
        in_specs=[pl.BlockSpec((tm, DIN), lambda i: (i, 0)),
                  pl.BlockSpec((8, DIN), lambda i: (jnp.maximum(i * (tm // 8) - 1, 0), 0)),
                  pl.BlockSpec(ws.shape, full3), pl.BlockSpec(bsb.shape, full3),
                  pl.BlockSpec(gv.shape, full2), pl.BlockSpec(bv.shape, full2), pl.BlockSpec(cw8.shape, full2)],
        out_specs=pl.BlockSpec((tm, DA + DB), lambda i: (i, 0)),
        out_shape=jax.ShapeDtypeStruct((S, DA + DB), BF16),
        compiler_params=pltpu.CompilerParams(dimension_semantics=("arbitrary",),
                                             vmem_limit_bytes=_vmem_limit(est)),
    )(z, z, ws, bsb, gv, bv, cw8)


def _mix_bwd(z, dy, ws, bsb, gv, bv, cw8, tm):
    S, DIN = z.shape
    DA = gv.shape[1]
    DB = DA
    H = DA // HEAD
    o = 2 * DA
    nch = tm // HEAD
    nblk = S // tm

    def body(z_ref, zp_ref, zn_ref, dy_ref, dyn_ref, ws_ref, bsb_ref, gv_ref, bv_ref, cw_ref,
             dz_ref, dws_ref, dbs_ref, dgv_ref, dbv_ref, dcw_ref, dbin_ref):
        i = pl.program_id(0)

        @pl.when(i == 0)
        def _():
            for r in (dws_ref, dbs_ref, dgv_ref, dbv_ref, dcw_ref, dbin_ref):
                r[...] = jnp.zeros_like(r)

        ri = lax.broadcasted_iota(jnp.int32, (HEAD, HEAD), 0)
        ci = lax.broadcasted_iota(jnp.int32, (HEAD, HEAD), 1)
        tril = ri >= ci
        for h in range(H):
            cs = slice(h * HEAD, (h + 1) * HEAD)
            vs = slice(DA + h * HEAD, DA + (h + 1) * HEAD)
            wb = jnp.where(tril, ws_ref[h], 0.0).astype(BF16)
            gvh = gv_ref[:, cs]
            dws_acc = jnp.zeros((HEAD, HEAD), F32)
            dbs_acc = jnp.zeros((HEAD, HEAD), F32)
            dgv_acc = jnp.zeros((1, HEAD), F32)
            dbv_acc = jnp.zeros((1, HEAD), F32)
            dbu_acc = jnp.zeros((1, HEAD), F32)
            dbvv_acc = jnp.zeros((1, HEAD), F32)
            for n in range(nch):
                rs = slice(n * HEAD, (n + 1) * HEAD)
                u, du_dz = _gelu_and_grad(z_ref[rs, cs])
                v, dv_dz = _gelu_and_grad(z_ref[rs, vs])
                vhat, rstd = _ln_stats(v)
                vnb = (vhat * gvh + bv_ref[:, cs]).astype(BF16)
                mixed = jnp.dot(wb, vnb, preferred_element_type=F32) + bsb_ref[h]
                dya = dy_ref[rs, cs]
                dzu = dya * mixed * du_dz
                dmx = dya * u
                dmxb = dmx.astype(BF16)
                dbs_acc += jnp.broadcast_to(jnp.sum(dmx, axis=1, keepdims=True), (HEAD, HEAD))
                dws_acc += lax.dot_general(dmxb, vnb, _DIMS['nt'], preferred_element_type=F32)
                dvn = lax.dot_general(wb, dmxb, _DIMS['tn'], preferred_element_type=F32)
                dgv_acc += jnp.sum(dvn * vhat, axis=0, keepdims=True)
                dbv_acc += jnp.sum(dvn, axis=0, keepdims=True)
                dxh = dvn * gvh
                m1 = jnp.mean(dxh, axis=-1, keepdims=True)
                m2 = jnp.mean(dxh * vhat, axis=-1, keepdims=True)
                dzv = rstd * (dxh - m1 - vhat * m2) * dv_dz
                dz_ref[rs, cs] = dzu.astype(BF16)
                dz_ref[rs, vs] = dzv.astype(BF16)
                dbu_acc += jnp.sum(dzu, axis=0, keepdims=True)
                dbvv_acc += jnp.sum(dzv, axis=0, keepdims=True)
            dws_ref[h] += jnp.where(tril, dws_acc, 0.0)
            dbs_ref[h] += dbs_acc
            dgv_ref[:, cs] += dgv_acc
            dbv_ref[:, cs] += dbv_acc
            dbin_ref[:, cs] += dbu_acc
            dbin_ref[:, vs] += dbvv_acc

        zb = z_ref[:, o:o + DB]
        zc = z_ref[:, o + DB:o + 2 * DB]
        zx = z_ref[:, o + 2 * DB:]
        hc = zc * zx
        keep = (i > 0).astype(F32)
        p1 = zp_ref[7:8, o + DB:o + 2 * DB] * zp_ref[7:8, o + 2 * DB:] * keep
        p2 = zp_ref[6:7, o + DB:o + 2 * DB] * zp_ref[6:7, o + 2 * DB:] * keep
        rowi = lax.broadcasted_iota(jnp.int32, (tm, DB), 0)
        r1, r2 = _conv_shifted(hc, p1, p2, rowi)
        w0, w1, w2 = cw_ref[0:1, :], cw_ref[1:2, :], cw_ref[2:3, :]
        cv = w0 * r2 + w1 * r1 + w2 * hc
        dyb = dy_ref[:, DA:]
        dzb = dyb * cv
        dcv = dyb * zb
        more = (i < nblk - 1).astype(F32)
        n0 = dyn_ref[0:1, DA:] * zn_ref[0:1, o:o + DB] * more
        n1 = dyn_ref[1:2, DA:] * zn_ref[1:2, o:o + DB] * more
        f1 = jnp.where(rowi == tm - 1, n0, pltpu.roll(dcv, tm - 1, 0))
        f2 = jnp.where(rowi == tm - 1, n1, jnp.where(rowi == tm - 2, n0, pltpu.roll(dcv, tm - 2, 0)))
        dhc = w2 * dcv + w1 * f1 + w0 * f2
        dzc = dhc * zx
        dzx = dhc * zc
        dcw_ref[0:1, :] += jnp.sum(dcv * r2, axis=0, keepdims=True)
        dcw_ref[1:2, :] += jnp.sum(dcv * r1, axis=0, keepdims=True)
        dcw_ref[2:3, :] += jnp.sum(dcv * hc, axis=0, keepdims=True)
        dz_ref[:, o:o + DB] = dzb.astype(BF16)
        dz_ref[:, o + DB:o + 2 * DB] = dzc.astype(BF16)
        dz_ref[:, o + 2 * DB:] = dzx.astype(BF16)
        dbin_ref[:, o:o + DB] += jnp.sum(dzb, axis=0, keepdims=True)
        dbin_ref[:, o + DB:o + 2 * DB] += jnp.sum(dzc, axis=0, keepdims=True)
        dbin_ref[:, o + 2 * DB:] += jnp.sum(dzx, axis=0, keepdims=True)

    full3 = lambda i: (0, 0, 0)
    full2 = lambda i: (0, 0)
    prev8 = lambda i: (jnp.maximum(i * (tm // 8) - 1, 0), 0)
    next8 = lambda i: (jnp.minimum((i + 1) * (tm // 8), S // 8 - 1), 0)
    est = 2 * (tm * DIN * 4 + tm * (DA + DB) * 4 + tm * DIN * 2) + 16 * tm * DB * 4
    return pl.pallas_call(
        body, name="mix_bwd", grid=(nblk,),
        in_specs=[pl.BlockSpec((tm, DIN), lambda i: (i, 0)),
                  pl.BlockSpec((8, DIN), prev8), pl.BlockSpec((8, DIN), next8),
                  pl.BlockSpec((tm, DA + DB), lambda i: (i, 0)), pl.BlockSpec((8, DA + DB), next8),
                  pl.BlockSpec(ws.shape, full3), pl.BlockSpec(bsb.shape, full3),
                  pl.BlockSpec(gv.shape, full2), pl.BlockSpec(bv.shape, full2), pl.BlockSpec(cw8.shape, full2)],
        out_specs=[pl.BlockSpec((tm, DIN), lambda i: (i, 0)),
                   pl.BlockSpec(ws.shape, full3), pl.BlockSpec(ws.shape, full3),
                   pl.BlockSpec(gv.shape, full2), pl.BlockSpec(gv.shape, full2),
                   pl.BlockSpec(cw8.shape, full2), pl.BlockSpec((1, DIN), full2)],
        out_shape=[jax.ShapeDtypeStruct((S, DIN), BF16),
                   jax.ShapeDtypeStruct(ws.shape, F32), jax.ShapeDtypeStruct(ws.shape, F32),
                   jax.ShapeDtypeStruct(gv.shape, F32), jax.ShapeDtypeStruct(gv.shape, F32),
                   jax.ShapeDtypeStruct(cw8.shape, F32), jax.ShapeDtypeStruct((1, DIN), F32)],
        compiler_params=pltpu.CompilerParams(dimension_semantics=("arbitrary",),
                                             vmem_limit_bytes=_vmem_limit(est)),
    )(z, z, z, dy, dy, ws, bsb, gv, bv, cw8)


def _place():
    x, y, c = lax.axis_index("x"), lax.axis_index("y"), lax.axis_index("c")
    others = [(1 - x, y), (x, 1 - y), (1 - x, 1 - y)]
    return x, y, c, 2 * x + y, others, [2 * ox + oy for ox, oy in others]


def _half(c, rows_half):
    return pl.ds(pl.multiple_of(c * rows_half, 16), rows_half)


def _rcopy(src, dst, ssem, rsem, dev):
    return pltpu.make_async_remote_copy(src_ref=src, dst_ref=dst, send_sem=ssem, recv_sem=rsem,
                                        device_id=dev, device_id_type=MESH)


class _Comm:
    def __init__(self, aliased, ins, fresh, n_sems, stages, sink, relayed=False):
        self.aliased, self.ins, self.fresh = list(aliased), list(ins), list(fresh)
        self.n_sems, self.stages, self.sink, self.relayed = n_sems, stages, sink, relayed


def _in_hbm(a):
    return pltpu.with_memory_space_constraint(a, pltpu.HBM)


def _comm_operands(comms, n_in, n_out):
    arrays, shapes, scratch, aliases = [], [], [], {}
    for cm in comms:
        for a in cm.aliased:
            aliases[n_in + len(arrays)] = n_out + len(shapes)
            arrays.append(_in_hbm(a))
            shapes.append(pltpu.HBM(a.shape, a.dtype))
        arrays += [_in_hbm(a) for a in cm.ins]
        shapes += [pltpu.HBM(s.shape, s.dtype) for s in cm.fresh]
        scratch += [pltpu.SemaphoreType.DMA((cm.n_sems,))] * 2
    return arrays, shapes, scratch, aliases


def _comm_refs(comms, in_refs, out_refs, sem_refs):
    per, pi, po = [], 0, 0
    for i, cm in enumerate(comms):
        pi += len(cm.aliased)
        ins = in_refs[pi:pi + len(cm.ins)]
        pi += len(cm.ins)
        al = out_refs[po:po + len(cm.aliased)]
        po += len(cm.aliased)
        fr = out_refs[po:po + len(cm.fresh)]
        po += len(cm.fresh)
        per.append((al, ins, fr, sem_refs[2 * i], sem_refs[2 * i + 1]))
    return per


def _comm_first_step(comms, hosted, grid):
    if not comms:
        return
    at_first = functools.reduce(lambda a, b: a & b, [pl.program_id(d) == 0 for d in range(len(grid))])

    @pl.when(at_first)
    def _():
        for cm, refs in zip(comms, hosted):
            for d in cm.stages[0](*refs, False):
                d.start()


def _comm_mid_step(comms, hosted, grid):
    relayed = [(cm, refs) for cm, refs in zip(comms, hosted) if cm.relayed]
    if not relayed:
        return
    n_steps = math.prod(grid)
    assert n_steps >= 3, grid
    mid, at_mid = (2 * n_steps) // 3, None
    for d in reversed(range(len(grid))):
        here = pl.program_id(d) == mid % grid[d]
        at_mid = here if at_mid is None else at_mid & here
        mid //= grid[d]

    @pl.when(at_mid)
    def _():
        for cm, refs in relayed:
            for d in cm.stages[0](*refs, True):
                d.wait()
            for d in cm.stages[1](*refs, False):
                d.start()


def _comm_last_step(comms, hosted, grid):
    if not comms:
        return
    at_last = functools.reduce(lambda a, b: a & b, [pl.program_id(d) == g - 1 for d, g in enumerate(grid)])

    @pl.when(at_last)
    def _():
        for cm, refs in zip(comms, hosted):
            for d in cm.stages[1 if cm.relayed else 0](*refs, True):
                d.wait()
        for cm, refs in zip(comms, hosted):
            for stage in cm.stages[2 if cm.relayed else 1:]:
                for d in stage(*refs, False):
                    d.start()
                for d in stage(*refs, True):
                    d.wait()


def _deliver(comms, outs):
    pos = 0
    for cm in comms:
        al = outs[pos:pos + len(cm.aliased)]
        pos += len(cm.aliased)
        fr = outs[pos:pos + len(cm.fresh)]
        pos += len(cm.fresh)
        cm.sink(list(al), list(fr))


def _comm_call(name, comms):
    in_arrays, out_shapes, scratch, aliases = _comm_operands(comms, 0, 0)
    n_in, n_out = len(in_arrays), len(out_shapes)

    def body(*refs):
        hosted = _comm_refs(comms, refs[:n_in], refs[n_in:n_in + n_out], refs[n_in + n_out:])
        for cm, r in zip(comms, hosted):
            for stage in cm.stages:
                for d in stage(*r, False):
                    d.start()
                for d in stage(*r, True):
                    d.wait()

    outs = pl.pallas_call(body, name=name, in_specs=[ANY] * n_in, out_specs=[ANY] * n_out,
                          out_shape=out_shapes, scratch_shapes=scratch, input_output_aliases=aliases)(*in_arrays)
    _deliver(comms, outs)


def _c_gather(fulls, sink, ici, fwd, conv=None):
    n = len(fulls)
    n_direct = 2 * n + (3 if conv is not None else 0)
    n_ici = n_direct + n if ici else 0

    def stage_direct(al, ins, fr, ssem, rsem, wait):
        x, y, c, chip, others, ochips = _place()
        ds = []
        for m in range(n):
            ah = al[m].shape[1] // 2
            mine = al[m].at[chip, _half(c, ah)]
            for k in range(2):
                i = 2 * m + k
                ds.append(_rcopy(mine, al[m].at[ochips[k], _half(c, ah)], ssem.at[i], rsem.at[i], (x, y, c)) if wait
                          else _rcopy(mine, mine, ssem.at[i], rsem.at[i], (*others[k], c)))
        if conv is not None:
            mine = al[n].at[chip]
            for k in range(3):
                i = 2 * n + k
                ds.append(_rcopy(mine, al[n].at[ochips[k]], ssem.at[i], rsem.at[i], (x, y, c)) if wait
                          else _rcopy(mine, mine, ssem.at[i], rsem.at[i], (*others[k], c)))
        return ds

    def stage_relay(al, ins, fr, ssem, rsem, wait):
        x, y, c, chip, others, ochips = _place()
        got_from = ochips[1] + (ochips[0] - ochips[1]) * c
        send_to = (x + (1 - 2 * x) * (1 - c), y + (1 - 2 * y) * c, c)
        ds = []
        for m in range(n):
            ah = al[m].shape[1] // 2
            i = n_direct + m
            got = al[m].at[got_from, _half(c, ah)]
            ds.append(_rcopy(got, al[m].at[ochips[2], _half(c, ah)], ssem.at[i], rsem.at[i], (x, y, c)) if wait
                      else _rcopy(got, got, ssem.at[i], rsem.at[i], send_to))
        return ds

    def stage_fwd(al, ins, fr, ssem, rsem, wait):
        x, y, c, chip, others, ochips = _place()
        ds = []
        for m in range(n):
            ah = al[m].shape[1] // 2
            for k in range(3):
                i = n_ici + 3 * m + k
                got = al[m].at[ochips[k], _half(c, ah)]
                ds.append(_rcopy(got, al[m].at[ochips[k], _half(1 - c, ah)], ssem.at[i], rsem.at[i], (x, y, c)) if wait
                          else _rcopy(got, got, ssem.at[i], rsem.at[i], (x, y, 1 - c)))
        return ds

    stages = ([stage_direct, stage_relay] if ici else []) + ([stage_fwd] if fwd else [])
    return _Comm(list(fulls) + ([conv] if conv is not None else []), [], [], n_ici + (3 * n if fwd else 0),
                 stages, sink, relayed=ici)


def _c_rs_pair(parts, sink):
    def stage(al, ins, fr, ssem, rsem, wait):
        x, y, c, _, _, _ = _place()
        return [_rcopy(ins[m].at[:, _half(1 - c, ins[m].shape[1] // 2), :], fr[m], ssem.at[m], rsem.at[m],
                       (x, y, 1 - c)) for m in range(len(ins))]
    fresh = [jax.ShapeDtypeStruct((p.shape[0], p.shape[1] // 2, p.shape[2]), p.dtype) for p in parts]
    return _Comm([], parts, fresh, len(parts), [stage], sink)


def _c_rs_chips(psums, sink):
    def stage(al, ins, fr, ssem, rsem, wait):
        x, y, c, _, others, ochips = _place()
        return [_rcopy(ins[m].at[ochips[k]], fr[m].at[k], ssem.at[3 * m + k], rsem.at[3 * m + k], (*others[k], c))
                for m in range(len(ins)) for k in range(3)]
    fresh = [jax.ShapeDtypeStruct((3,) + p.shape[1:], p.dtype) for p in psums]
    return _Comm([], psums, fresh, 3 * len(psums), [stage], sink)


def _c_exchange(g2s, sink):
    def stage(al, ins, fr, ssem, rsem, wait):
        x, y, c, _, _, _ = _place()
        if wait:
            return [_rcopy(al[m].at[c], al[m].at[1 - c], ssem.at[m], rsem.at[m], (x, y, c)) for m in range(len(al))]
        return [_rcopy(al[m].at[c], al[m].at[c], ssem.at[m], rsem.at[m], (x, y, 1 - c)) for m in range(len(al))]
    return _Comm(g2s, [], [], len(g2s), [stage], sink)


def _c_allgather_small(buf, sink):
    def stage(al, ins, fr, ssem, rsem, wait):
        x, y, c, _, _, _ = _place()
        ds = []
        for r in range(1, 8):
            peer = (1 - x if r & 4 else x, 1 - y if r & 2 else y, 1 - c if r & 1 else c)
            ds.append(_rcopy(al[0].at[0], al[0].at[r], ssem.at[r - 1], rsem.at[r - 1], peer))
        return ds
    return _Comm([buf], [], [], 7, [stage], sink)


def _sum_small(buf, dev_idx):
    _, m, n = buf.shape

    def body(me_ref, b_ref, o_ref):
        me = me_ref[0]
        acc = b_ref[me]
        for d in range(1, 8):
            acc = acc + b_ref[jnp.bitwise_xor(me, d)]
        o_ref[...] = acc

    gs = pltpu.PrefetchScalarGridSpec(
        num_scalar_prefetch=1, grid=(1,),
        in_specs=[pl.BlockSpec(buf.shape, lambda i, me: (0, 0, 0))],
        out_specs=pl.BlockSpec((m, n), lambda i, me: (0, 0)))
    return pl.pallas_call(body, name="sum_small", grid_spec=gs,
                          out_shape=jax.ShapeDtypeStruct((m, n), F32))(dev_idx, buf)


def _sum_pair(name, part, sib, c_idx):
    _, A, B = part.shape
    ah = A // 2
    br = _rows(ah, B * 2)

    def body(c_ref, a_ref, b_ref, o_ref):
        o_ref[...] = (a_ref[...].astype(F32) + b_ref[...].astype(F32)).astype(BF16)

    gs = pltpu.PrefetchScalarGridSpec(
        num_scalar_prefetch=1, grid=(N_CHIPS, ah // br),
        in_specs=[pl.BlockSpec((None, None, br, B), lambda j, i, c: (j, c[0], i, 0)),
                  pl.BlockSpec((None, br, B), lambda j, i, c: (j, i, 0))],
        out_specs=pl.BlockSpec((None, br, B), lambda j, i, c: (j, i, 0)))
    return pl.pallas_call(body, name=name, grid_spec=gs,
                          out_shape=pltpu.HBM((N_CHIPS, ah, B), BF16),
                          compiler_params=pltpu.CompilerParams(vmem_limit_bytes=_vmem_limit(2 * 3 * br * B * 2)),
                          )(c_idx, _in_hbm(part.reshape(N_CHIPS, 2, ah, B)), _in_hbm(sib))


def _sum_chips(name, psum, recv, place_idx):
    _, ah, B = psum.shape
    br = _rows(ah, B * 4)

    def body(c_ref, p_ref, r0_ref, r1_ref, r2_ref, o_ref):
        o_ref[...] = ((p_ref[...].astype(F32) + r0_ref[...].astype(F32)) + r1_ref[...].astype(F32)) \
            + r2_ref[...].astype(F32)

    gs = pltpu.PrefetchScalarGridSpec(
        num_scalar_prefetch=1, grid=(ah // br,),
        in_specs=[pl.BlockSpec((None, br, B), lambda i, c: (c[0], i, 0))]
        + [pl.BlockSpec((None, br, B), functools.partial(lambda i, c, k: (k, i, 0), k=k)) for k in range(3)],
        out_specs=pl.BlockSpec((None, br, B), lambda i, c: (c[1], i, 0)))
    return pl.pallas_call(body, name=name, grid_spec=gs,
                          out_shape=pltpu.HBM((2, ah, B), F32),
                          compiler_params=pltpu.CompilerParams(vmem_limit_bytes=_vmem_limit(2 * br * B * (4 * 2 + 4))),
                          )(place_idx, _in_hbm(psum), *[_in_hbm(recv)] * 3)


def _cast_into(name, a, place_idx):
    A, B = a.shape
    br = _rows(A, B * 4)

    def body(c_ref, a_ref, o_ref):
        o_ref[...] = a_ref[...].astype(BF16)

    gs = pltpu.PrefetchScalarGridSpec(
        num_scalar_prefetch=1, grid=(A // br,),
        in_specs=[pl.BlockSpec((br, B), lambda i, c: (i, 0))],
        out_specs=pl.BlockSpec((None, br, B), lambda i, c: (c[0], i, 0)))
    return pl.pallas_call(body, name=name, grid_spec=gs,
                          out_shape=pltpu.HBM((N_CHIPS, A, B), BF16))(place_idx, a)


def _cast_behind(x, shards, place_idx, comms):
    n_steps = 16
    n = len(shards)
    xv = x
    svs = list(shards)

    def pace(v):
        for steps_per_block in (1, 2, 4, 8, 16):
            blocks = n_steps // steps_per_block
            if v.shape[0] % blocks == 0 and (v.shape[0] // blocks) % 16 == 0:
                return steps_per_block
        raise ValueError(v.shape)

    n_in, n_out = 2 + n, 1 + n
    cm_in_arrays, cm_out_shapes, cm_scratch, aliases = _comm_operands(comms, n_in, n_out)
    n_cm_in, n_cm_out = len(cm_in_arrays), len(cm_out_shapes)
    grid = (n_steps,)

    def body(c_ref, x_ref, *rest):
        s_refs = rest[:n]
        xo_ref = rest[n + n_cm_in]
        so_refs = rest[n + n_cm_in + 1:n + n_cm_in + 1 + n]
        o_end = n + n_cm_in + 1 + n
        hosted = _comm_refs(comms, rest[n:n + n_cm_in], rest[o_end:o_end + n_cm_out], rest[o_end + n_cm_out:])
        _comm_first_step(comms, hosted, grid)
        _comm_mid_step(comms, hosted, grid)
        xo_ref[...] = x_ref[...].astype(BF16)
        for s_ref, o_ref in zip(s_refs, so_refs):
            o_ref[...] = s_ref[...].astype(BF16)
        _comm_last_step(comms, hosted, grid)

    def rows(v):
        return v.shape[0] * pace(v) // n_steps

    def in_spec(v):
        return pl.BlockSpec((rows(v), v.shape[1]), functools.partial(lambda i, c, q: (i // q, 0), q=pace(v)))

    def shard_out_spec(v):
        return pl.BlockSpec((None, rows(v), v.shape[1]),
                            functools.partial(lambda i, c, q: (c[0], i // q, 0), q=pace(v)))

    gs = pltpu.PrefetchScalarGridSpec(
        num_scalar_prefetch=1, grid=grid,
        in_specs=[in_spec(v) for v in [xv] + svs] + [ANY] * n_cm_in,
        out_specs=[in_spec(xv)] + [shard_out_spec(v) for v in svs] + [ANY] * n_cm_out,
        scratch_shapes=cm_scratch)
    outs = pl.pallas_call(
        body, name="cast_behind", grid_spec=gs,
        out_shape=[jax.ShapeDtypeStruct(xv.shape, BF16)]
        + [pltpu.HBM((N_CHIPS,) + v.shape, BF16) for v in svs] + cm_out_shapes,
        input_output_aliases=aliases,
        compiler_params=pltpu.CompilerParams(
            dimension_semantics=("arbitrary",),
            vmem_limit_bytes=_vmem_limit(2 * sum(rows(v) * v.shape[1] * 6 for v in [xv] + svs))),
    )(place_idx, xv, *svs, *cm_in_arrays)
    _deliver(comms, outs[n_out:])
    return outs[0].reshape(x.shape), [o.reshape((N_CHIPS,) + s.shape) for o, s in zip(outs[1:n_out], shards)]


def _adamw_small(gsum, gconv, offsets, ws, ms, vs):
    n = len(ws)

    def body(*refs):
        g_ref, gc_ref = refs[0], refs[1]
        w_refs = refs[2:2 + n]
        m_refs = refs[2 + n:2 + 2 * n]
        v_refs = refs[2 + 2 * n:2 + 3 * n]
        outs = refs[2 + 3 * n:]
        for i in range(n):
            rows = w_refs[i].shape[0]
            g = gc_ref[...] if offsets[i] is None else g_ref[offsets[i]:offsets[i] + rows, :]
            d, m2, v2 = _adamw_math(w_refs[i][...], g, m_refs[i][...], v_refs[i][...])
            outs[4 * i][...] = g
            outs[4 * i + 1][...] = d
            outs[4 * i + 2][...] = m2
            outs[4 * i + 3][...] = v2

    vm = pl.BlockSpec(memory_space=pltpu.VMEM)
    out_shape = []
    for w in ws:
        out_shape += [jax.ShapeDtypeStruct(w.shape, F32)] * 4
    return pl.pallas_call(body, name="adamw_small", in_specs=[vm] * (2 + 3 * n), out_specs=[vm] * (4 * n),
                          out_shape=out_shape)(gsum, gconv, *ws, *ms, *vs)


def _ident(accs, tes, res):
    return [accs[0]]


def _ffn_up(name, hb, wg4, wu4, comms=()):
    S, D = hb.shape

    def epi(accs, tes, res):
        g, u = accs
        return [g, u, g * _sigmoid(g) * u]

    return _mm(name, 'nn', [hb], [wg4, wu4], [(0, 0, 0), (0, 1, 1)], 2, epi, [BF16] * 3,
               bm=_blk(S, 512), bn=wg4.shape[2], bk=D, b_blocked=True, n_outer=True, comms=comms)


def _ffn_gate(name, hb, wg4, comms=()):
    S, D = hb.shape
    return _mm(name, 'nn', [hb], [wg4], [(0, 0, 0)], 1, _ident, [BF16],
               bm=_blk(S, 1024), bn=wg4.shape[2], bk=D, b_blocked=True, n_outer=True, comms=comms)[0]


def _ffn_upmul(name, hb, wu4, gb, comms=()):
    S, D = hb.shape

    def epi(accs, tes, res):
        g = tes[0].astype(F32)
        return [accs[0], g * _sigmoid(g) * accs[0]]

    return _mm(name, 'nn', [hb], [wu4], [(0, 0, 0)], 1, epi, [BF16, BF16],
               bm=_blk(S, 1024), bn=wu4.shape[2], bk=D, tile_extras=[gb], b_blocked=True, n_outer=True, comms=comms)


def _ffn_down(name, ab, wd, hres, comms=()):
    S, F = ab.shape
    D = wd.shape[1]
    return _mm(name, 'nn', [ab], [wd], [(0, 0, 0)], 1, lambda accs, tes, res: [ALPHA * tes[0] + 0.5 * accs[0]],
               [F32], bm=_blk(S, 1024), bn=_blk(D, 512), bk=F, tile_extras=[hres], comms=comms)[0]


def _ffn_da(name, dsub, wd, gb, ub, comms=()):
    S, D = dsub.shape
    F = wd.shape[0]

    def epi(accs, tes, res):
        da = accs[0]
        g = tes[0].astype(F32)
        u = tes[1].astype(F32)
        s = _sigmoid(g)
        silu = g * s
        return [(da * u) * (s + silu * (1.0 - s)), da * silu]

    return _mm(name, 'nt', [dsub], [wd], [(0, 0, 0)], 1, epi, [BF16, BF16],
               bm=_blk(S, 1024), bn=_blk(F, 512), bk=D, tile_extras=[gb, ub], comms=comms)


def _dw_rows(name, ab, dsub, cs, comms=()):
    S, D = dsub.shape
    return _mm(name, 'tn', [ab], [dsub], [(0, 0, 0)], 1, _ident, [BF16],
               bm=cs, bn=_blk(D, 1024), bk=_blk(S, 2048), comms=comms)[0].reshape(N_CHIPS, cs, D)


def _dw_cols(name, hb, dxb, cs, comms=()):
    S, D = hb.shape
    return _mm(name, 'tn', [hb], [dxb], [(0, 0, 0)], 1, _ident, [BF16],
               bm=_blk(D, 1024), bn=cs, bk=_blk(S, 2048), out_blocked=True, comms=comms)[0]


def _ffn_dh(name, dgb, dub, wg4, wu4, dres, comms=()):
    S = dgb.shape[0]
    D = wg4.shape[1]
    return _mm(name, 'nt', [dgb, dub], [wg4, wu4], [(0, 0, 0), (1, 1, 0)], 1,
               lambda accs, tes, res: [accs[0] + tes[0]], [F32],
               bm=_blk(S, 1024), bn=_blk(D, 1024), bk=wg4.shape[2], tile_extras=[dres], b_blocked=True,
               comms=comms)[0]


def _step(p):
    x = p['x'][0]
    target = p['loss_target'][0]
    S, D = x.shape
    cx, cy, cc = lax.axis_index("x"), lax.axis_index("y"), lax.axis_index("c")
    chip = 2 * cx + cy
    c_idx = jnp.reshape(cc, (1,)).astype(jnp.int32)
    place_idx = jnp.stack([chip, cc]).astype(jnp.int32)

    w_s = p['w_s'][0]
    H = w_s.shape[0]
    DA = H * HEAD
    bsb = jnp.broadcast_to(p['b_s'][0][:, :, None], (H, HEAD, HEAD))
    conv_shard = p['conv_w'][0]
    conv8 = jnp.zeros((8, conv_shard.shape[1]), F32).at[:CONV_TAPS].set(conv_shard)

    W = {'ffa_gate': _cast_into("cast_ffa_gate", p['ffa_gate'][0], place_idx)}
    conv = {'full': lax.dynamic_update_slice(jnp.zeros((N_CHIPS,) + conv8.shape, F32), conv8[None], (chip, 0, 0))}

    def gathered(names, with_conv=False):
        def sink(al, fr):
            W.update(zip(names, al))
            if with_conv:
                conv['full'] = al[len(names)]
        return sink

    def c_gather(names, ici, fwd, with_conv=False):
        return _c_gather([W[n] for n in names], gathered(names, with_conv), ici, fwd,
                         conv['full'] if with_conv else None)

    rest = [n for n in BIG if n != 'ffa_gate']
    xb, casted = _cast_behind(x, [p[n][0] for n in rest], place_idx, comms=[c_gather(['ffa_gate'], True, True)])
    W.update(zip(rest, casted))
    cs_ff = W['ffa_gate'].shape[2]
    F = N_CHIPS * cs_ff

    ga = _ffn_gate("ffa_gate", xb, W['ffa_gate'],
                   comms=[c_gather(['ffa_up'], True, True), c_gather(['ffa_down'], True, False)])
    ua, aa = _ffn_upmul("ffa_up", xb, W['ffa_up'], ga,
                        comms=[c_gather(['ffa_down'], False, True),
                               c_gather(['w_in', 'ffc_gate'], True, False, with_conv=True)])
    cw8 = jnp.transpose(conv['full'], (1, 0, 2)).reshape(8, DA)
    wd_a = W['ffa_down'].reshape(F, D)
    r1 = _ffn_down("ffa_down", aa, wd_a, x,
                   comms=[c_gather(['w_in', 'ffc_gate'], False, True), c_gather(['w_out', 'ffc_down'], True, False)])
    h1, h1b = _ln_fwd("ln_a", r1, p['ln_a_g'], p['ln_a_b'])
    w_in4 = W['w_in']
    cs_in = w_in4.shape[2]
    z = _mm("w_in", 'nn', [h1b], [w_in4], [(0, 0, 0)], 1, lambda accs, tes, res: [accs[0] + res[0]], [F32],
            bm=_blk(S, 1024), bn=cs_in, bk=D, row_extras=[p['b_in']], b_blocked=True, n_outer=True,
            comms=[c_gather(['w_out', 'ffc_down'], False, True), c_gather(['ffc_up'], True, False)])[0]
    w_out = W['w_out'].reshape(DA * 2, D)
    tm = _blk(S, 256)
    yb = _mix_fwd(z, w_s, bsb, p['ln_v_g'], p['ln_v_b'], cw8, tm)
    r2 = _mm("w_out", 'nn', [yb], [w_out], [(0, 0, 0)], 1,
             lambda accs, tes, res: [accs[0] + res[0] + ALPHA * tes[0]], [F32],
             bm=_blk(S, 1024), bn=_blk(D, 1024), bk=2 * DA, tile_extras=[h1], row_extras=[p['b_out']],
             comms=[c_gather(['ffc_up'], False, True)])[0]
    h2, h2b = _ln_fwd("ln_m", r2, p['ln_m_g'], p['ln_m_b'])
    gc, uc, ac = _ffn_up("ffc_up", h2b, W['ffc_gate'], W['ffc_up'])
    wd_c = W['ffc_down'].reshape(F, D)
    r3 = _ffn_down("ffc_down", ac, wd_c, h2)

    pg, sg, sib, psum, recv, g2 = {}, {}, {}, {}, {}, {}

    def c_pair(names):
        return _c_rs_pair([pg[n] for n in names], lambda al, fr: sib.update(zip(names, fr)))

    def c_chips(names):
        return _c_rs_chips([psum[n] for n in names], lambda al, fr: recv.update(zip(names, fr)))

    def c_swap(names):
        return _c_exchange([g2[n] for n in names], lambda al, fr: g2.update(zip(names, al)))

    def sum_pair(*names):
        for n in names:
            psum[n] = _sum_pair(f"sum_pair_{n}", pg[n], sib[n], c_idx)

    def sum_chips(*names):
        for n in names:
            g2[n] = _sum_chips(f"sum_chips_{n}", psum[n], recv[n], place_idx)

    dres3, dsub3, sg['ln_c_g'], sg['ln_c_b'], lsum = _ln_loss_bwd(
        "ln_c_loss_bwd", r3, target, p['ln_c_g'], p['ln_c_b'], 0.5)

    dgc, duc = _ffn_da("ffc_da", dsub3, wd_c, gc, uc)
    pg['ffc_down'] = _dw_rows("ffc_dwd", ac, dsub3, cs_ff)
    pg['ffc_gate'] = _dw_cols("ffc_dwg", h2b, dgc, cs_ff, comms=[c_pair(['ffc_down'])])
    sum_pair('ffc_down')
    pg['ffc_up'] = _dw_cols("ffc_dwu", h2b, duc, cs_ff, comms=[c_pair(['ffc_gate']), c_chips(['ffc_down'])])
    sum_pair('ffc_gate')
    dh2 = _ffn_dh("ffc_dh", dgc, duc, W['ffc_gate'], W['ffc_up'], dres3,
                  comms=[c_pair(['ffc_up']), c_chips(['ffc_gate'])])
    sum_pair('ffc_up')
    sum_chips('ffc_down', 'ffc_gate')

    dres2, dmix, sg['ln_m_g'], sg['ln_m_b'], sg['b_out'] = _ln_bwd("ln_m_bwd", r2, dh2, p['ln_m_g'], 1.0)
    pg['w_out'] = _mm("dw_out", 'tn', [yb], [dmix], [(0, 0, 0)], 1, _ident, [BF16],
                      bm=_blk(2 * DA, 1024), bn=_blk(D, 1024), bk=_blk(S, 2048),
                      comms=[c_swap(['ffc_down', 'ffc_gate'])])[0].reshape(N_CHIPS, 2 * DA // 4, D)
    dy = _mm("dy", 'nt', [dmix], [w_out], [(0, 0, 0)], 1, _ident, [F32],
             bm=_blk(S, 1024), bn=_blk(2 * DA, 1024), bk=D, comms=[c_pair(['w_out'])])[0]
    sum_pair('w_out')
    dz, dws, dbs, sg['ln_v_g'], sg['ln_v_b'], dcw, sg['b_in'] = _mix_bwd(
        z, dy, w_s, bsb, p['ln_v_g'], p['ln_v_b'], cw8, tm)
    sg['w_s'] = dws
    sg['b_s'] = dbs[:, :, 0]
    pg['w_in'] = _dw_cols("dw_in", h1b, dz, cs_in, comms=[c_chips(['ffc_up']), c_chips(['w_out'])])
    sum_chips('ffc_up', 'w_out')
    dh1 = _mm("dh1", 'nt', [dz], [w_in4], [(0, 0, 0)], 1, lambda accs, tes, res: [accs[0] + tes[0]], [F32],
              bm=_blk(S, 1024), bn=_blk(D, 1024), bk=cs_in, tile_extras=[dres2], b_blocked=True,
              comms=[c_pair(['w_in']), c_swap(['ffc_up', 'w_out'])])[0]
    sum_pair('w_in')

    dres1, dsub1, sg['ln_a_g'], sg['ln_a_b'], _ = _ln_bwd("ln_a_bwd", r1, dh1, p['ln_a_g'], 0.5)
    def as_rows(a):
        a = a.reshape(-1, 128)
        pad = (-a.shape[0]) % 8
        return jnp.pad(a, ((0, pad), (0, 0))) if pad else a

    pieces, offsets, off = [], {}, 0
    for n in SMALL + ['conv_w']:
        piece = as_rows(dcw[:CONV_TAPS] if n == 'conv_w' else sg[n])
        offsets[n] = off
        off += piece.shape[0]
        pieces.append(piece)
    packed = jnp.concatenate(pieces, axis=0)
    small = {'buf': jnp.zeros((8,) + packed.shape, F32).at[0].set(packed)}
    dga, dua = _ffn_da("ffa_da", dsub1, wd_a, ga, ua,
                       comms=[c_chips(['w_in']), _c_allgather_small(small['buf'], lambda al, fr: small.update(buf=al[0]))])
    sum_chips('w_in')
    gsum = _sum_small(small['buf'], jnp.reshape(4 * cx + 2 * cy + cc, (1,)).astype(jnp.int32))
    pg['ffa_down'] = _dw_rows("ffa_dwd", aa, dsub1, cs_ff, comms=[c_swap(['w_in'])])
    pg['ffa_gate'] = _dw_cols("ffa_dwg", xb, dga, cs_ff, comms=[c_pair(['ffa_down'])])
    sum_pair('ffa_down')
    pg['ffa_up'] = _dw_cols("ffa_dwu", xb, dua, cs_ff, comms=[c_pair(['ffa_gate']), c_chips(['ffa_down'])])
    sum_pair('ffa_gate')
    _comm_call("pair_ffa_up", [c_pair(['ffa_up'])])
    sum_pair('ffa_up')
    sum_chips('ffa_down')
    dx = _ffn_dh("ffa_dh", dga, dua, W['ffa_gate'], W['ffa_up'], dres1,
                 comms=[c_chips(['ffa_gate']), c_chips(['ffa_up']), c_swap(['ffa_down'])])
    sum_chips('ffa_gate', 'ffa_up')
    _comm_call("swap_ffa", [c_swap(['ffa_gate', 'ffa_up'])])

    out = {}
    for n in BIG:
        g, d, m2, v2 = _adamw(f"adamw_{n}", p[n][0], g2[n].reshape(p[n][0].shape), p['m_' + n][0], p['v_' + n][0])
        shp = p[n].shape
        out[n] = (g.reshape(shp), d.reshape(shp), m2.reshape(shp), v2.reshape(shp))

    loss = lax.psum(0.5 * jnp.sum(lsum) / D, ("x", "y", "c"))
    n_conv = CONV_TAPS * DA // 128
    conv_sum = gsum[offsets['conv_w']:offsets['conv_w'] + n_conv].reshape(CONV_TAPS, DA)
    cw = conv_shard.shape[1]
    gconv = lax.dynamic_slice_in_dim(conv_sum, (2 * cx + cy) * cw, cw, axis=1).reshape(-1, 128)
    names = SMALL + ['conv_w']
    rows128 = lambda a: a.reshape(-1, 128)
    res = _adamw_small(gsum, gconv, [offsets[n] if n != 'conv_w' else None for n in names],
                       [rows128(p[n]) for n in names], [rows128(p['m_' + n]) for n in names],
                       [rows128(p['v_' + n]) for n in names])
    for i, n in enumerate(names):
        out[n] = tuple(r.reshape(p[n].shape) for r in res[4 * i:4 * i + 4])

    return (loss, dx.reshape(p['x'].shape), *[out[n][0] for n in WEIGHTS], *[out[n][1] for n in WEIGHTS],
            *[out[n][2] for n in WEIGHTS], *[out[n][3] for n in WEIGHTS])


def kernel(x, ffa_gate, ffa_up, ffa_down, ln_a_g, ln_a_b, w_in, b_in, w_s, b_s, ln_v_g, ln_v_b, conv_w, w_out, b_out, ln_m_g, ln_m_b, ffc_gate, ffc_up, ffc_down, ln_c_g, ln_c_b, loss_target, m_ffa_gate, m_ffa_up, m_ffa_down, m_ln_a_g, m_ln_a_b, m_w_in, m_b_in, m_w_s, m_b_s, m_ln_v_g, m_ln_v_b, m_conv_w, m_w_out, m_b_out, m_ln_m_g, m_ln_m_b, m_ffc_gate, m_ffc_up, m_ffc_down, m_ln_c_g, m_ln_c_b, v_ffa_gate, v_ffa_up, v_ffa_down, v_ln_a_g, v_ln_a_b, v_w_in, v_b_in, v_w_s, v_b_s, v_ln_v_g, v_ln_v_b, v_conv_w, v_w_out, v_b_out, v_ln_m_g, v_ln_m_b, v_ffc_gate, v_ffc_up, v_ffc_down, v_ln_c_g, v_ln_c_b):
    return _step(dict(locals()))
```

```python
import functools
import math

import jax
import jax.numpy as jnp
from jax import lax
from jax.experimental import pallas as pl
from jax.experimental.pallas import tpu as pltpu

F32 = jnp.float32
BF16 = jnp.bfloat16
MESH = pl.DeviceIdType.MESH
ANY = pl.BlockSpec(memory_space=pl.ANY)

HEAD = 128
CONV_TAPS = 3
LN_EPS = 1e-5
ALPHA = float(2.0 ** 0.25)
ADAM_LR, ADAM_B1, ADAM_B2, ADAM_EPS, ADAM_WD, ADAM_STEP = 0.001, 0.9, 0.999, 1e-08, 0.01, 10

V7X_VMEM_BYTES = 64 * 2 ** 20
VMEM_COMPILER_RESERVE = 6 * 2 ** 20
VMEM_MIN_REQUEST = 32 * 2 ** 20
VMEM_ESTIMATE_SLACK = 1.3
N_CHIPS = 4

BIG = ['ffa_gate', 'ffa_up', 'ffa_down', 'w_in', 'w_out', 'ffc_gate', 'ffc_up', 'ffc_down']
WEIGHTS = ['ffa_gate', 'ffa_up', 'ffa_down', 'ln_a_g', 'ln_a_b', 'w_in', 'b_in', 'w_s', 'b_s', 'ln_v_g', 'ln_v_b',
           'conv_w', 'w_out', 'b_out', 'ln_m_g', 'ln_m_b', 'ffc_gate', 'ffc_up', 'ffc_down', 'ln_c_g', 'ln_c_b']
SMALL = [n for n in WEIGHTS if n not in BIG and n != 'conv_w']


def _vmem_limit(estimate_bytes):
    return int(min(max(estimate_bytes * VMEM_ESTIMATE_SLACK, VMEM_MIN_REQUEST),
                   V7X_VMEM_BYTES - VMEM_COMPILER_RESERVE))


def _blk(dim, pref, mult=128):
    if dim <= pref:
        return dim
    best = None
    for d in range(mult, pref + 1, mult):
        if dim % d == 0:
            best = d
    assert best is not None, (dim, pref)
    return best


def _rows(n_rows, row_bytes, target=4 * 2 ** 20):
    return _blk(n_rows, max(16, target // row_bytes), 16)


def _sigmoid(x):
    return 0.5 * jnp.tanh(0.5 * x) + 0.5


_GELU_C = math.sqrt(2.0 / math.pi)
_GELU_A = 0.044715


def _gelu(x):
    return (0.5 * x) * (1.0 + jnp.tanh(x * (_GELU_C + (_GELU_C * _GELU_A) * (x * x))))


def _gelu_and_grad(x):
    x2 = x * x
    t = jnp.tanh(x * (_GELU_C + (_GELU_C * _GELU_A) * x2))
    half_x = 0.5 * x
    one_plus_t = 1.0 + t
    dg = 0.5 * one_plus_t + half_x * (1.0 - t * t) * (_GELU_C + (3.0 * _GELU_C * _GELU_A) * x2)
    return half_x * one_plus_t, dg


def _ln_stats(r):
    mu = jnp.mean(r, axis=-1, keepdims=True)
    d = r - mu
    var = jnp.mean(d * d, axis=-1, keepdims=True)
    rstd = lax.rsqrt(var + LN_EPS)
    return d * rstd, rstd


_DIMS = {'nn': (((1,), (0,)), ((), ())), 'nt': (((1,), (1,)), ((), ())), 'tn': (((0,), (0,)), ((), ()))}


def _mm(name, mode, a_list, b_list, pairs, n_acc, epilogue, out_dtypes, bm, bn, bk,
        tile_extras=(), row_extras=(), b_blocked=False, out_blocked=False, n_outer=False, comms=()):
    a0, b0 = a_list[0], b_list[0]
    if mode == 'nn':
        M, K = a0.shape
        N = b0.shape[0] * b0.shape[2] if b_blocked else b0.shape[1]
    elif mode == 'nt':
        M, K = a0.shape
        N = b0.shape[1] if b_blocked else b0.shape[0]
    else:
        K, M = a0.shape
        N = b0.shape[1]
        assert not b_blocked
    assert M % bm == 0 and N % bn == 0 and K % bk == 0, (name, M, N, K, bm, bn, bk)
    gm, gn, gk = M // bm, N // bn, K // bk
    if b_blocked:
        assert (bn if mode == 'nn' else bk) == b0.shape[2], name
    dims = _DIMS[mode]

    def spec(shape, f):
        if n_outer:
            return pl.BlockSpec(shape, lambda g0, g1, g2: f(g1, g0, g2))
        return pl.BlockSpec(shape, f)

    if mode == 'tn':
        a_spec = spec((bk, bm), lambda i, j, k: (k, i))
    else:
        a_spec = spec((bm, bk), lambda i, j, k: (i, k))
    if mode == 'nn':
        b_spec = (spec((None, bk, bn), lambda i, j, k: (j, k, 0)) if b_blocked
                  else spec((bk, bn), lambda i, j, k: (k, j)))
    elif mode == 'nt':
        b_spec = (spec((None, bn, bk), lambda i, j, k: (k, j, 0)) if b_blocked
                  else spec((bn, bk), lambda i, j, k: (j, k)))
    else:
        b_spec = spec((bk, bn), lambda i, j, k: (k, j))
    te_spec = spec((bm, bn), lambda i, j, k: (i, j))
    re_spec = spec((1, bn), lambda i, j, k: (0, j))
    if out_blocked:
        o_spec = spec((None, bm, bn), lambda i, j, k: (j, i, 0))
        o_shape = (gn, M, bn)
    else:
        o_spec = te_spec
        o_shape = (M, N)

    na, nb, nte, nre, no = len(a_list), len(b_list), len(tile_extras), len(row_extras), len(out_dtypes)

    n_scratch_acc = n_acc if gk > 1 else 0
    grid = (gn, gm, gk) if n_outer else (gm, gn, gk)
    cm_in_arrays, cm_out_shapes, cm_scratch, aliases = _comm_operands(comms, na + nb + nte + nre, no)
    n_cm_in, n_cm_out = len(cm_in_arrays), len(cm_out_shapes)

    def body(*refs):
        a_refs = refs[:na]
        b_refs = refs[na:na + nb]
        te_refs = refs[na + nb:na + nb + nte]
        re_refs = refs[na + nb + nte:na + nb + nte + nre]
        n_in = na + nb + nte + nre
        o_refs = refs[n_in + n_cm_in:n_in + n_cm_in + no]
        acc_refs = refs[n_in + n_cm_in + no + n_cm_out:n_in + n_cm_in + no + n_cm_out + n_scratch_acc]
        hosted = _comm_refs(comms, refs[n_in:n_in + n_cm_in], refs[n_in + n_cm_in + no:n_in + n_cm_in + no + n_cm_out],
                            refs[n_in + n_cm_in + no + n_cm_out + n_scratch_acc:])
        _comm_first_step(comms, hosted, grid)
        _comm_mid_step(comms, hosted, grid)

        def finish(accs):
            outs = epilogue(accs, [r[...] for r in te_refs], [r[...] for r in re_refs])
            for o_ref, o in zip(o_refs, outs):
                o_ref[...] = o.astype(o_ref.dtype)

        def products():
            parts = [None] * n_acc
            for ai, bi, ci in pairs:
                d = lax.dot_general(a_refs[ai][...], b_refs[bi][...], dims, preferred_element_type=F32)
                parts[ci] = d if parts[ci] is None else parts[ci] + d
            return parts

        if gk == 1:
            finish(products())
        else:
            kk = pl.program_id(2)

            @pl.when(kk == 0)
            def _():
                for r, part in zip(acc_refs, products()):
                    r[...] = part

            if gk > 2:
                @pl.when((kk > 0) & (kk < gk - 1))
                def _():
                    for r, part in zip(acc_refs, products()):
                        r[...] += part

            @pl.when(kk == gk - 1)
            def _():
                finish([r[...] + part for r, part in zip(acc_refs, products())])

        _comm_last_step(comms, hosted, grid)

    isz = lambda a: jnp.dtype(a.dtype).itemsize
    est = 2 * (sum(bm * bk * isz(a) for a in a_list) + sum(bk * bn * isz(b) for b in b_list)
               + sum(bm * bn * isz(t) for t in tile_extras)
               + sum(bm * bn * jnp.dtype(d).itemsize for d in out_dtypes))
    est += (2 * n_acc + 2) * bm * bn * 4
    outs = pl.pallas_call(
        body, name=name, grid=grid,
        in_specs=[a_spec] * na + [b_spec] * nb + [te_spec] * nte + [re_spec] * nre + [ANY] * len(cm_in_arrays),
        out_specs=[o_spec] * no + [ANY] * len(cm_out_shapes),
        out_shape=[jax.ShapeDtypeStruct(o_shape, d) for d in out_dtypes] + cm_out_shapes,
        scratch_shapes=[pltpu.VMEM((bm, bn), F32)] * n_scratch_acc + cm_scratch,
        input_output_aliases=aliases,
        compiler_params=pltpu.CompilerParams(dimension_semantics=("parallel", "parallel", "arbitrary"),
                                             vmem_limit_bytes=_vmem_limit(est)),
    )(*a_list, *b_list, *tile_extras, *row_extras, *cm_in_arrays)
    _deliver(comms, outs[no:])
    return outs[:no]


def _rowwise(name, fn, br, row_ins, vec_ins, row_outs, acc_outs=()):
    S = row_ins[0].shape[0]
    assert S % br == 0, (name, S, br)
    nr, nv, no, nacc = len(row_ins), len(vec_ins), len(row_outs), len(acc_outs)

    def body(*refs):
        rin = refs[:nr]
        vin = refs[nr:nr + nv]
        rout = refs[nr + nv:nr + nv + no]
        aout = refs[nr + nv + no:]
        if nacc:
            @pl.when(pl.program_id(0) == 0)
            def _():
                for a in aout:
                    a[...] = jnp.zeros_like(a)
        fn(rin, vin, rout, aout)

    def full_spec(shape):
        nd = len(shape)
        return pl.BlockSpec(shape, lambda i: (0,) * nd)

    est = 2 * (sum(br * a.shape[1] * jnp.dtype(a.dtype).itemsize for a in row_ins)
               + sum(br * c * jnp.dtype(d).itemsize for c, d in row_outs))
    est += 6 * br * max(a.shape[1] for a in row_ins) * 4
    return pl.pallas_call(
        body, name=name, grid=(S // br,),
        in_specs=[pl.BlockSpec((br, a.shape[1]), lambda i: (i, 0)) for a in row_ins]
        + [full_spec(v.shape) for v in vec_ins],
        out_specs=[pl.BlockSpec((br, c), lambda i: (i, 0)) for c, _ in row_outs]
        + [full_spec(s) for s, _ in acc_outs],
        out_shape=[jax.ShapeDtypeStruct((S, c), d) for c, d in row_outs]
        + [jax.ShapeDtypeStruct(s, d) for s, d in acc_outs],
        compiler_params=pltpu.CompilerParams(dimension_semantics=("arbitrary",),
                                             vmem_limit_bytes=_vmem_limit(est)),
    )(*row_ins, *vec_ins)


def _ln_fwd(name, r, g, b):
    def fn(rin, vin, rout, aout):
        xhat, _ = _ln_stats(rin[0][...])
        h = xhat * vin[0][...] + vin[1][...]
        rout[0][...] = h
        rout[1][...] = h.astype(BF16)
    D = r.shape[1]
    return _rowwise(name, fn, _rows(r.shape[0], D * 4), [r], [g, b], [(D, F32), (D, BF16)])


def _ln_loss_bwd(name, r, target, g, b, sub_scale):
    D = r.shape[1]

    def fn(rin, vin, rout, aout):
        xhat, rstd = _ln_stats(rin[0][...])
        gain = vin[0][...]
        err = xhat * gain + vin[1][...] - rin[1][...]
        dhv = err * (1.0 / D)
        dxh = dhv * gain
        m1 = jnp.mean(dxh, axis=-1, keepdims=True)
        m2 = jnp.mean(dxh * xhat, axis=-1, keepdims=True)
        dr = rstd * (dxh - m1 - xhat * m2)
        rout[0][...] = ALPHA * dr
        rout[1][...] = (sub_scale * dr).astype(BF16)
        aout[0][...] += jnp.sum(dhv * xhat, axis=0, keepdims=True)
        aout[1][...] += jnp.sum(dhv, axis=0, keepdims=True)
        aout[2][...] += jnp.sum(err * err, axis=0, keepdims=True)
    return _rowwise(name, fn, _rows(r.shape[0], D * 4, 2 * 2 ** 20), [r, target], [g, b],
                    [(D, F32), (D, BF16)], [((1, D), F32)] * 3)


def _ln_bwd(name, r, dh, g, sub_scale):
    D = r.shape[1]

    def fn(rin, vin, rout, aout):
        xhat, rstd = _ln_stats(rin[0][...])
        dhv = rin[1][...]
        dxh = dhv * vin[0][...]
        m1 = jnp.mean(dxh, axis=-1, keepdims=True)
        m2 = jnp.mean(dxh * xhat, axis=-1, keepdims=True)
        dr = rstd * (dxh - m1 - xhat * m2)
        rout[0][...] = ALPHA * dr
        rout[1][...] = (sub_scale * dr).astype(BF16)
        aout[0][...] += jnp.sum(dhv * xhat, axis=0, keepdims=True)
        aout[1][...] += jnp.sum(dhv, axis=0, keepdims=True)
        aout[2][...] += jnp.sum(dr, axis=0, keepdims=True)
    return _rowwise(name, fn, _rows(r.shape[0], D * 4, 2 * 2 ** 20), [r, dh], [g],
                    [(D, F32), (D, BF16)], [((1, D), F32)] * 3)


def _adamw_math(w, g, m, v):
    m2 = ADAM_B1 * m + (1.0 - ADAM_B1) * g
    v2 = ADAM_B2 * v + (1.0 - ADAM_B2) * (g * g)
    m_hat = m2 / (1.0 - ADAM_B1 ** ADAM_STEP)
    v_hat = v2 / (1.0 - ADAM_B2 ** ADAM_STEP)
    delta = -ADAM_LR * (m_hat / (jnp.sqrt(v_hat) + ADAM_EPS) + ADAM_WD * w)
    return delta, m2, v2


def _adamw(name, w, g, m, v):
    def fn(rin, vin, rout, aout):
        gv = rin[1][...]
        d, m2, v2 = _adamw_math(rin[0][...], gv, rin[2][...], rin[3][...])
        rout[0][...] = gv
        rout[1][...] = d
        rout[2][...] = m2
        rout[3][...] = v2
    C = w.shape[1]
    return _rowwise(name, fn, _rows(w.shape[0], C * 4, 2 * 2 ** 20), [w, g, m, v], [], [(C, F32)] * 4)


def _conv_shifted(hc, p1, p2, rowi):
    r1 = jnp.where(rowi == 0, p1, pltpu.roll(hc, 1, 0))
    r2 = jnp.where(rowi == 0, p2, jnp.where(rowi == 1, p1, pltpu.roll(hc, 2, 0)))
    return r1, r2


def _mix_fwd(z, ws, bsb, gv, bv, cw8, tm):
    S, DIN = z.shape
    DA = gv.shape[1]
    DB = DA
    H = DA // HEAD
    o = 2 * DA
    nch = tm // HEAD

    def body(z_ref, zp_ref, ws_ref, bsb_ref, gv_ref, bv_ref, cw_ref, y_ref):
        i = pl.program_id(0)
        ri = lax.broadcasted_iota(jnp.int32, (HEAD, HEAD), 0)
        ci = lax.broadcasted_iota(jnp.int32, (HEAD, HEAD), 1)
        tril = ri >= ci
        for h in range(H):
            cs = slice(h * HEAD, (h + 1) * HEAD)
            vs = slice(DA + h * HEAD, DA + (h + 1) * HEAD)
            wb = jnp.where(tril, ws_ref[h], 0.0).astype(BF16)
            for n in range(nch):
                rs = slice(n * HEAD, (n + 1) * HEAD)
                u = _gelu(z_ref[rs, cs])
                vhat, _ = _ln_stats(_gelu(z_ref[rs, vs]))
                vn = vhat * gv_ref[:, cs] + bv_ref[:, cs]
                mixed = jnp.dot(wb, vn.astype(BF16), preferred_element_type=F32) + bsb_ref[h]
                y_ref[rs, cs] = (u * mixed).astype(BF16)
        zb = z_ref[:, o:o + DB]
        hc = z_ref[:, o + DB:o + 2 * DB] * z_ref[:, o + 2 * DB:]
        keep = (i > 0).astype(F32)
        p1 = zp_ref[7:8, o + DB:o + 2 * DB] * zp_ref[7:8, o + 2 * DB:] * keep
        p2 = zp_ref[6:7, o + DB:o + 2 * DB] * zp_ref[6:7, o + 2 * DB:] * keep
        rowi = lax.broadcasted_iota(jnp.int32, (tm, DB), 0)
        r1, r2 = _conv_shifted(hc, p1, p2, rowi)
        cv = cw_ref[0:1, :] * r2 + cw_ref[1:2, :] * r1 + cw_ref[2:3, :] * hc
        y_ref[:, DA:] = (zb * cv).astype(BF16)

    full3 = lambda i: (0, 0, 0)
    full2 = lambda i: (0, 0)
    est = 2 * (tm * DIN * 4 + tm * (DA + DB) * 2) + 10 * tm * DB * 4
    return pl.pallas_call(
        body, name="mix_fwd", grid=(S // tm,),
        in_specs=[pl.BlockSpec((tm, DIN), lambda i: (i, 0)),
                  pl.BlockSpec((8, DIN), lambda i: (jnp.maximum(i * (tm // 8) - 1, 0), 0)),
                  pl.BlockSpec(ws.shape, full3), pl.BlockSpec(bsb.shape, full3),
                  pl.BlockSpec(gv.shape, full2), pl.BlockSpec(bv.shape, full2), pl.BlockSpec(cw8.shape, full2)],
        out_specs=pl.BlockSpec((tm, DA + DB), lambda i: (i, 0)),
        out_shape=jax.ShapeDtypeStruct((S, DA + DB), BF16),
        compiler_params=pltpu.CompilerParams(dimension_semantics=("arbitrary",),
                                             vmem_limit_bytes=_vmem_limit(est)),
    )(z, z, ws, bsb, gv, bv, cw8)


def _mix_bwd(z, dy, ws, bsb, gv, bv, cw8, tm):
    S, DIN = z.shape
    DA = gv.shape[1]
    DB = DA
    H = DA // HEAD
    o = 2 * DA
    nch = tm // HEAD
    nblk = S // tm

    def body(z_ref, zp_ref, zn_ref, dy_ref, dyn_ref, ws_ref, bsb_ref, gv_ref, bv_ref, cw_ref,
             dz_ref, dws_ref, dbs_ref, dgv_ref, dbv_ref, dcw_ref, dbin_ref):
        i = pl.program_id(0)

        @pl.when(i == 0)
        def _():
            for r in (dws_ref, dbs_ref, dgv_ref, dbv_ref, dcw_ref, dbin_ref):
                r[...] = jnp.zeros_like(r)

        ri = lax.broadcasted_iota(jnp.int32, (HEAD, HEAD), 0)
        ci = lax.broadcasted_iota(jnp.int32, (HEAD, HEAD), 1)
        tril = ri >= ci
        for h in range(H):
            cs = slice(h * HEAD, (h + 1) * HEAD)
            vs = slice(DA + h * HEAD, DA + (h + 1) * HEAD)
            wb = jnp.where(tril, ws_ref[h], 0.0).astype(BF16)
            gvh = gv_ref[:, cs]
            dws_acc = jnp.zeros((HEAD, HEAD), F32)
            dbs_acc = jnp.zeros((HEAD, HEAD), F32)
            dgv_acc = jnp.zeros((1, HEAD), F32)
            dbv_acc = jnp.zeros((1, HEAD), F32)
            dbu_acc = jnp.zeros((1, HEAD), F32)
            dbvv_acc = jnp.zeros((1, HEAD), F32)
            for n in range(nch):
                rs = slice(n * HEAD, (n + 1) * HEAD)
                u, du_dz = _gelu_and_grad(z_ref[rs, cs])
                v, dv_dz = _gelu_and_grad(z_ref[rs, vs])
                vhat, rstd = _ln_stats(v)
                vnb = (vhat * gvh + bv_ref[:, cs]).astype(BF16)
                mixed = jnp.dot(wb, vnb, preferred_element_type=F32) + bsb_ref[h]
                dya = dy_ref[rs, cs]
                dzu = dya * mixed * du_dz
                dmx = dya * u
                dmxb = dmx.astype(BF16)
                dbs_acc += jnp.broadcast_to(jnp.sum(dmx, axis=1, keepdims=True), (HEAD, HEAD))
                dws_acc += lax.dot_general(dmxb, vnb, _DIMS['nt'], preferred_element_type=F32)
                dvn = lax.dot_general(wb, dmxb, _DIMS['tn'], preferred_element_type=F32)
                dgv_acc += jnp.sum(dvn * vhat, axis=0, keepdims=True)
                dbv_acc += jnp.sum(dvn, axis=0, keepdims=True)
                dxh = dvn * gvh
                m1 = jnp.mean(dxh, axis=-1, keepdims=True)
                m2 = jnp.mean(dxh * vhat, axis=-1, keepdims=True)
                dzv = rstd * (dxh - m1 - vhat * m2) * dv_dz
                dz_ref[rs, cs] = dzu.astype(BF16)
                dz_ref[rs, vs] = dzv.astype(BF16)
                dbu_acc += jnp.sum(dzu, axis=0, keepdims=True)
                dbvv_acc += jnp.sum(dzv, axis=0, keepdims=True)
            dws_ref[h] += jnp.where(tril, dws_acc, 0.0)
            dbs_ref[h] += dbs_acc
            dgv_ref[:, cs] += dgv_acc
            dbv_ref[:, cs] += dbv_acc
            dbin_ref[:, cs] += dbu_acc
            dbin_ref[:, vs] += dbvv_acc

        zb = z_ref[:, o:o + DB]
        zc = z_ref[:, o + DB:o + 2 * DB]
        zx = z_ref[:, o + 2 * DB:]
        hc = zc * zx
        keep = (i > 0).astype(F32)
        p1 = zp_ref[7:8, o + DB:o + 2 * DB] * zp_ref[7:8, o + 2 * DB:] * keep
        p2 = zp_ref[6:7, o + DB:o + 2 * DB] * zp_ref[6:7, o + 2 * DB:] * keep
        rowi = lax.broadcasted_iota(jnp.int32, (tm, DB), 0)
        r1, r2 = _conv_shifted(hc, p1, p2, rowi)
        w0, w1, w2 = cw_ref[0:1, :], cw_ref[1:2, :], cw_ref[2:3, :]
        cv = w0 * r2 + w1 * r1 + w2 * hc
        dyb = dy_ref[:, DA:]
        dzb = dyb * cv
        dcv = dyb * zb
        more = (i < nblk - 1).astype(F32)
        n0 = dyn_ref[0:1, DA:] * zn_ref[0:1, o:o + DB] * more
        n1 = dyn_ref[1:2, DA:] * zn_ref[1:2, o:o + DB] * more
        f1 = jnp.where(rowi == tm - 1, n0, pltpu.roll(dcv, tm - 1, 0))
        f2 = jnp.where(rowi == tm - 1, n1, jnp.where(rowi == tm - 2, n0, pltpu.roll(dcv, tm - 2, 0)))
        dhc = w2 * dcv + w1 * f1 + w0 * f2
        dzc = dhc * zx
        dzx = dhc * zc
        dcw_ref[0:1, :] += jnp.sum(dcv * r2, axis=0, keepdims=True)
        dcw_ref[1:2, :] += jnp.sum(dcv * r1, axis=0, keepdims=True)
        dcw_ref[2:3, :] += jnp.sum(dcv * hc, axis=0, keepdims=True)
        dz_ref[:, o:o + DB] = dzb.astype(BF16)
        dz_ref[:, o + DB:o + 2 * DB] = dzc.astype(BF16)
        dz_ref[:, o + 2 * DB:] = dzx.astype(BF16)
        dbin_ref[:, o:o + DB] += jnp.sum(dzb, axis=0, keepdims=True)
        dbin_ref[:, o + DB:o + 2 * DB] += jnp.sum(dzc, axis=0, keepdims=True)
        dbin_ref[:, o + 2 * DB:] += jnp.sum(dzx, axis=0, keepdims=True)

    full3 = lambda i: (0, 0, 0)
    full2 = lambda i: (0, 0)
    prev8 = lambda i: (jnp.maximum(i * (tm // 8) - 1, 0), 0)
    next8 = lambda i: (jnp.minimum((i + 1) * (tm // 8), S // 8 - 1), 0)
    est = 2 * (tm * DIN * 4 + tm * (DA + DB) * 4 + tm * DIN * 2) + 16 * tm * DB * 4
    return pl.pallas_call(
        body, name="mix_bwd", grid=(nblk,),
        in_specs=[pl.BlockSpec((tm, DIN), lambda i: (i, 0)),
                  pl.BlockSpec((8, DIN), prev8), pl.BlockSpec((8, DIN), next8),
                  pl.BlockSpec((tm, DA + DB), lambda i: (i, 0)), pl.BlockSpec((8, DA + DB), next8),
                  pl.BlockSpec(ws.shape, full3), pl.BlockSpec(bsb.shape, full3),
                  pl.BlockSpec(gv.shape, full2), pl.BlockSpec(bv.shape, full2), pl.BlockSpec(cw8.shape, full2)],
        out_specs=[pl.BlockSpec((tm, DIN), lambda i: (i, 0)),
                   pl.BlockSpec(ws.shape, full3), pl.BlockSpec(ws.shape, full3),
                   pl.BlockSpec(gv.shape, full2), pl.BlockSpec(gv.shape, full2),
                   pl.BlockSpec(cw8.shape, full2), pl.BlockSpec((1, DIN), full2)],
        out_shape=[jax.ShapeDtypeStruct((S, DIN), BF16),
                   jax.ShapeDtypeStruct(ws.shape, F32), jax.ShapeDtypeStruct(ws.shape, F32),
                   jax.ShapeDtypeStruct(gv.shape, F32), jax.ShapeDtypeStruct(gv.shape, F32),
                   jax.ShapeDtypeStruct(cw8.shape, F32), jax.ShapeDtypeStruct((1, DIN), F32)],
        compiler_params=pltpu.CompilerParams(dimension_semantics=("arbitrary",),
                                             vmem_limit_bytes=_vmem_limit(est)),
    )(z, z, z, dy, dy, ws, bsb, gv, bv, cw8)


def _place():
    x, y, c = lax.axis_index("x"), lax.axis_index("y"), lax.axis_index("c")
    others = [(1 - x, y), (x, 1 - y), (1 - x, 1 - y)]
    return x, y, c, 2 * x + y, others, [2 * ox + oy for ox, oy in others]


def _half(c, rows_half):
    return pl.ds(pl.multiple_of(c * rows_half, 16), rows_half)


def _rcopy(src, dst, ssem, rsem, dev):
    return pltpu.make_async_remote_copy(src_ref=src, dst_ref=dst, send_sem=ssem, recv_sem=rsem,
                                        device_id=dev, device_id_type=MESH)


class _Comm:
    def __init__(self, aliased, ins, fresh, n_sems, stages, sink, relayed=False):
        self.aliased, self.ins, self.fresh = list(aliased), list(ins), list(fresh)
        self.n_sems, self.stages, self.sink, self.relayed = n_sems, stages, sink, relayed


def _in_hbm(a):
    return pltpu.with_memory_space_constraint(a, pltpu.HBM)


def _comm_operands(comms, n_in, n_out):
    arrays, shapes, scratch, aliases = [], [], [], {}
    for cm in comms:
        for a in cm.aliased:
            aliases[n_in + len(arrays)] = n_out + len(shapes)
            arrays.append(_in_hbm(a))
            shapes.append(pltpu.HBM(a.shape, a.dtype))
        arrays += [_in_hbm(a) for a in cm.ins]
        shapes += [pltpu.HBM(s.shape, s.dtype) for s in cm.fresh]
        scratch += [pltpu.SemaphoreType.DMA((cm.n_sems,))] * 2
    return arrays, shapes, scratch, aliases


def _comm_refs(comms, in_refs, out_refs, sem_refs):
    per, pi, po = [], 0, 0
    for i, cm in enumerate(comms):
        pi += len(cm.aliased)
        ins = in_refs[pi:pi + len(cm.ins)]
        pi += len(cm.ins)
        al = out_refs[po:po + len(cm.aliased)]
        po += len(cm.aliased)
        fr = out_refs[po:po + len(cm.fresh)]
        po += len(cm.fresh)
        per.append((al, ins, fr, sem_refs[2 * i], sem_refs[2 * i + 1]))
    return per


def _comm_first_step(comms, hosted, grid):
    if not comms:
        return
    at_first = functools.reduce(lambda a, b: a & b, [pl.program_id(d) == 0 for d in range(len(grid))])

    @pl.when(at_first)
    def _():
        for cm, refs in zip(comms, hosted):
            for d in cm.stages[0](*refs, False):
                d.start()


def _comm_mid_step(comms, hosted, grid):
    relayed = [(cm, refs) for cm, refs in zip(comms, hosted) if cm.relayed]
    if not relayed:
        return
    n_steps = math.prod(grid)
    assert n_steps >= 3, grid
    mid, at_mid = (2 * n_steps) // 3, None
    for d in reversed(range(len(grid))):
        here = pl.program_id(d) == mid % grid[d]
        at_mid = here if at_mid is None else at_mid & here
        mid //= grid[d]

    @pl.when(at_mid)
    def _():
        for cm, refs in relayed:
            for d in cm.stages[0](*refs, True):
                d.wait()
            for d in cm.stages[1](*refs, False):
                d.start()


def _comm_last_step(comms, hosted, grid):
    if not comms:
        return
    at_last = functools.reduce(lambda a, b: a & b, [pl.program_id(d) == g - 1 for d, g in enumerate(grid)])

    @pl.when(at_last)
    def _():
        for cm, refs in zip(comms, hosted):
            for d in cm.stages[1 if cm.relayed else 0](*refs, True):
                d.wait()
        for cm, refs in zip(comms, hosted):
            for stage in cm.stages[2 if cm.relayed else 1:]:
                for d in stage(*refs, False):
                    d.start()
                for d in stage(*refs, True):
                    d.wait()


def _deliver(comms, outs):
    pos = 0
    for cm in comms:
        al = outs[pos:pos + len(cm.aliased)]
        pos += len(cm.aliased)
        fr = outs[pos:pos + len(cm.fresh)]
        pos += len(cm.fresh)
        cm.sink(list(al), list(fr))


def _comm_call(name, comms):
    in_arrays, out_shapes, scratch, aliases = _comm_operands(comms, 0, 0)
    n_in, n_out = len(in_arrays), len(out_shapes)

    def body(*refs):
        hosted = _comm_refs(comms, refs[:n_in], refs[n_in:n_in + n_out], refs[n_in + n_out:])
        for cm, r in zip(comms, hosted):
            for stage in cm.stages:
                for d in stage(*r, False):
                    d.start()
                for d in stage(*r, True):
                    d.wait()

    outs = pl.pallas_call(body, name=name, in_specs=[ANY] * n_in, out_specs=[ANY] * n_out,
                          out_shape=out_shapes, scratch_shapes=scratch, input_output_aliases=aliases)(*in_arrays)
    _deliver(comms, outs)


def _c_gather(fulls, sink, ici, fwd, conv=None):
    n = len(fulls)
    n_direct = 2 * n + (3 if conv is not None else 0)
    n_ici = n_direct + n if ici else 0

    def stage_direct(al, ins, fr, ssem, rsem, wait):
        x, y, c, chip, others, ochips = _place()
        ds = []
        for m in range(n):
            ah = al[m].shape[1] // 2
            mine = al[m].at[chip, _half(c, ah)]
            for k in range(2):
                i = 2 * m + k
                ds.append(_rcopy(mine, al[m].at[ochips[k], _half(c, ah)], ssem.at[i], rsem.at[i], (x, y, c)) if wait
                          else _rcopy(mine, mine, ssem.at[i], rsem.at[i], (*others[k], c)))
        if conv is not None:
            mine = al[n].at[chip]
            for k in range(3):
                i = 2 * n + k
                ds.append(_rcopy(mine, al[n].at[ochips[k]], ssem.at[i], rsem.at[i], (x, y, c)) if wait
                          else _rcopy(mine, mine, ssem.at[i], rsem.at[i], (*others[k], c)))
        return ds

    def stage_relay(al, ins, fr, ssem, rsem, wait):
        x, y, c, chip, others, ochips = _place()
        got_from = ochips[1] + (ochips[0] - ochips[1]) * c
        send_to = (x + (1 - 2 * x) * (1 - c), y + (1 - 2 * y) * c, c)
        ds = []
        for m in range(n):
            ah = al[m].shape[1] // 2
            i = n_direct + m
            got = al[m].at[got_from, _half(c, ah)]
            ds.append(_rcopy(got, al[m].at[ochips[2], _half(c, ah)], ssem.at[i], rsem.at[i], (x, y, c)) if wait
                      else _rcopy(got, got, ssem.at[i], rsem.at[i], send_to))
        return ds

    def stage_fwd(al, ins, fr, ssem, rsem, wait):
        x, y, c, chip, others, ochips = _place()
        ds = []
        for m in range(n):
            ah = al[m].shape[1] // 2
            for k in range(3):
                i = n_ici + 3 * m + k
                got = al[m].at[ochips[k], _half(c, ah)]
                ds.append(_rcopy(got, al[m].at[ochips[k], _half(1 - c, ah)], ssem.at[i], rsem.at[i], (x, y, c)) if wait
                          else _rcopy(got, got, ssem.at[i], rsem.at[i], (x, y, 1 - c)))
        return ds

    stages = ([stage_direct, stage_relay] if ici else []) + ([stage_fwd] if fwd else [])
    return _Comm(list(fulls) + ([conv] if conv is not None else []), [], [], n_ici + (3 * n if fwd else 0),
                 stages, sink, relayed=ici)


def _c_rs_pair(parts, sink):
    def stage(al, ins, fr, ssem, rsem, wait):
        x, y, c, _, _, _ = _place()
        return [_rcopy(ins[m].at[:, _half(1 - c, ins[m].shape[1] // 2), :], fr[m], ssem.at[m], rsem.at[m],
                       (x, y, 1 - c)) for m in range(len(ins))]
    fresh = [jax.ShapeDtypeStruct((p.shape[0], p.shape[1] // 2, p.shape[2]), p.dtype) for p in parts]
    return _Comm([], parts, fresh, len(parts), [stage], sink)


def _c_rs_chips(psums, sink):
    def stage(al, ins, fr, ssem, rsem, wait):
        x, y, c, _, others, ochips = _place()
        return [_rcopy(ins[m].at[ochips[k]], fr[m].at[k], ssem.at[3 * m + k], rsem.at[3 * m + k], (*others[k], c))
                for m in range(len(ins)) for k in range(3)]
    fresh = [jax.ShapeDtypeStruct((3,) + p.shape[1:], p.dtype) for p in psums]
    return _Comm([], psums, fresh, 3 * len(psums), [stage], sink)


def _c_exchange(g2s, sink):
    def stage(al, ins, fr, ssem, rsem, wait):
        x, y, c, _, _, _ = _place()
        if wait:
            return [_rcopy(al[m].at[c], al[m].at[1 - c], ssem.at[m], rsem.at[m], (x, y, c)) for m in range(len(al))]
        return [_rcopy(al[m].at[c], al[m].at[c], ssem.at[m], rsem.at[m], (x, y, 1 - c)) for m in range(len(al))]
    return _Comm(g2s, [], [], len(g2s), [stage], sink)


def _c_allgather_small(buf, sink):
    def stage(al, ins, fr, ssem, rsem, wait):
        x, y, c, _, _, _ = _place()
        ds = []
        for r in range(1, 8):
            peer = (1 - x if r & 4 else x, 1 - y if r & 2 else y, 1 - c if r & 1 else c)
            ds.append(_rcopy(al[0].at[0], al[0].at[r], ssem.at[r - 1], rsem.at[r - 1], peer))
        return ds
    return _Comm([buf], [], [], 7, [stage], sink)


def _sum_small(buf, dev_idx):
    _, m, n = buf.shape

    def body(me_ref, b_ref, o_ref):
        me = me_ref[0]
        acc = b_ref[me]
        for d in range(1, 8):
            acc = acc + b_ref[jnp.bitwise_xor(me, d)]
        o_ref[...] = acc

    gs = pltpu.PrefetchScalarGridSpec(
        num_scalar_prefetch=1, grid=(1,),
        in_specs=[pl.BlockSpec(buf.shape, lambda i, me: (0, 0, 0))],
        out_specs=pl.BlockSpec((m, n), lambda i, me: (0, 0)))
    return pl.pallas_call(body, name="sum_small", grid_spec=gs,
                          out_shape=jax.ShapeDtypeStruct((m, n), F32))(dev_idx, buf)


def _sum_pair(name, part, sib, c_idx):
    _, A, B = part.shape
    ah = A // 2
    br = _rows(ah, B * 2)

    def body(c_ref, a_ref, b_ref, o_ref):
        o_ref[...] = (a_ref[...].astype(F32) + b_ref[...].astype(F32)).astype(BF16)

    gs = pltpu.PrefetchScalarGridSpec(
        num_scalar_prefetch=1, grid=(N_CHIPS, ah // br),
        in_specs=[pl.BlockSpec((None, None, br, B), lambda j, i, c: (j, c[0], i, 0)),
                  pl.BlockSpec((None, br, B), lambda j, i, c: (j, i, 0))],
        out_specs=pl.BlockSpec((None, br, B), lambda j, i, c: (j, i, 0)))
    return pl.pallas_call(body, name=name, grid_spec=gs,
                          out_shape=pltpu.HBM((N_CHIPS, ah, B), BF16),
                          compiler_params=pltpu.CompilerParams(vmem_limit_bytes=_vmem_limit(2 * 3 * br * B * 2)),
                          )(c_idx, _in_hbm(part.reshape(N_CHIPS, 2, ah, B)), _in_hbm(sib))


def _sum_chips(name, psum, recv, place_idx):
    _, ah, B = psum.shape
    br = _rows(ah, B * 4)

    def body(c_ref, p_ref, r0_ref, r1_ref, r2_ref, o_ref):
        o_ref[...] = ((p_ref[...].astype(F32) + r0_ref[...].astype(F32)) + r1_ref[...].astype(F32)) \
            + r2_ref[...].astype(F32)

    gs = pltpu.PrefetchScalarGridSpec(
        num_scalar_prefetch=1, grid=(ah // br,),
        in_specs=[pl.BlockSpec((None, br, B), lambda i, c: (c[0], i, 0))]
        + [pl.BlockSpec((None, br, B), functools.partial(lambda i, c, k: (k, i, 0), k=k)) for k in range(3)],
        out_specs=pl.BlockSpec((None, br, B), lambda i, c: (c[1], i, 0)))
    return pl.pallas_call(body, name=name, grid_spec=gs,
                          out_shape=pltpu.HBM((2, ah, B), F32),
                          compiler_params=pltpu.CompilerParams(vmem_limit_bytes=_vmem_limit(2 * br * B * (4 * 2 + 4))),
                          )(place_idx, _in_hbm(psum), *[_in_hbm(recv)] * 3)


def _cast_into(name, a, place_idx):
    A, B = a.shape
    br = _rows(A, B * 4)

    def body(c_ref, a_ref, o_ref):
        o_ref[...] = a_ref[...].astype(BF16)

    gs = pltpu.PrefetchScalarGridSpec(
        num_scalar_prefetch=1, grid=(A // br,),
        in_specs=[pl.BlockSpec((br, B), lambda i, c: (i, 0))],
        out_specs=pl.BlockSpec((None, br, B), lambda i, c: (c[0], i, 0)))
    return pl.pallas_call(body, name=name, grid_spec=gs,
                          out_shape=pltpu.HBM((N_CHIPS, A, B), BF16))(place_idx, a)


def _cast_behind(x, shards, place_idx, comms):
    n_steps = 16
    n = len(shards)
    xv = x
    svs = list(shards)

    def pace(v):
        for steps_per_block in (1, 2, 4, 8, 16):
            blocks = n_steps // steps_per_block
            if v.shape[0] % blocks == 0 and (v.shape[0] // blocks) % 16 == 0:
                return steps_per_block
        raise ValueError(v.shape)

    n_in, n_out = 2 + n, 1 + n
    cm_in_arrays, cm_out_shapes, cm_scratch, aliases = _comm_operands(comms, n_in, n_out)
    n_cm_in, n_cm_out = len(cm_in_arrays), len(cm_out_shapes)
    grid = (n_steps,)

    def body(c_ref, x_ref, *rest):
        s_refs = rest[:n]
        xo_ref = rest[n + n_cm_in]
        so_refs = rest[n + n_cm_in + 1:n + n_cm_in + 1 + n]
        o_end = n + n_cm_in + 1 + n
        hosted = _comm_refs(comms, rest[n:n + n_cm_in], rest[o_end:o_end + n_cm_out], rest[o_end + n_cm_out:])
        _comm_first_step(comms, hosted, grid)
        _comm_mid_step(comms, hosted, grid)
        xo_ref[...] = x_ref[...].astype(BF16)
        for s_ref, o_ref in zip(s_refs, so_refs):
            o_ref[...] = s_ref[...].astype(BF16)
        _comm_last_step(comms, hosted, grid)

    def rows(v):
        return v.shape[0] * pace(v) // n_steps

    def in_spec(v):
        return pl.BlockSpec((rows(v), v.shape[1]), functools.partial(lambda i, c, q: (i // q, 0), q=pace(v)))

    def shard_out_spec(v):
        return pl.BlockSpec((None, rows(v), v.shape[1]),
                            functools.partial(lambda i, c, q: (c[0], i // q, 0), q=pace(v)))

    gs = pltpu.PrefetchScalarGridSpec(
        num_scalar_prefetch=1, grid=grid,
        in_specs=[in_spec(v) for v in [xv] + svs] + [ANY] * n_cm_in,
        out_specs=[in_spec(xv)] + [shard_out_spec(v) for v in svs] + [ANY] * n_cm_out,
        scratch_shapes=cm_scratch)
    outs = pl.pallas_call(
        body, name="cast_behind", grid_spec=gs,
        out_shape=[jax.ShapeDtypeStruct(xv.shape, BF16)]
        + [pltpu.HBM((N_CHIPS,) + v.shape, BF16) for v in svs] + cm_out_shapes,
        input_output_aliases=aliases,
        compiler_params=pltpu.CompilerParams(
            dimension_semantics=("arbitrary",),
            vmem_limit_bytes=_vmem_limit(2 * sum(rows(v) * v.shape[1] * 6 for v in [xv] + svs))),
    )(place_idx, xv, *svs, *cm_in_arrays)
    _deliver(comms, outs[n_out:])
    return outs[0].reshape(x.shape), [o.reshape((N_CHIPS,) + s.shape) for o, s in zip(outs[1:n_out], shards)]


def _adamw_small(gsum, gconv, offsets, ws, ms, vs):
    n = len(ws)

    def body(*refs):
        g_ref, gc_ref = refs[0], refs[1]
        w_refs = refs[2:2 + n]
        m_refs = refs[2 + n:2 + 2 * n]
        v_refs = refs[2 + 2 * n:2 + 3 * n]
        outs = refs[2 + 3 * n:]
        for i in range(n):
            rows = w_refs[i].shape[0]
            g = gc_ref[...] if offsets[i] is None else g_ref[offsets[i]:offsets[i] + rows, :]
            d, m2, v2 = _adamw_math(w_refs[i][...], g, m_refs[i][...], v_refs[i][...])
            outs[4 * i][...] = g
            outs[4 * i + 1][...] = d
            outs[4 * i + 2][...] = m2
            outs[4 * i + 3][...] = v2

    vm = pl.BlockSpec(memory_space=pltpu.VMEM)
    out_shape = []
    for w in ws:
        out_shape += [jax.ShapeDtypeStruct(w.shape, F32)] * 4
    return pl.pallas_call(body, name="adamw_small", in_specs=[vm] * (2 + 3 * n), out_specs=[vm] * (4 * n),
                          out_shape=out_shape)(gsum, gconv, *ws, *ms, *vs)


def _ident(accs, tes, res):
    return [accs[0]]


def _ffn_up(name, hb, wg4, wu4, comms=()):
    S, D = hb.shape

    def epi(accs, tes, res):
        g, u = accs
        return [g, u, g * _sigmoid(g) * u]

    return _mm(name, 'nn', [hb], [wg4, wu4], [(0, 0, 0), (0, 1, 1)], 2, epi, [BF16] * 3,
               bm=_blk(S, 512), bn=wg4.shape[2], bk=D, b_blocked=True, n_outer=True, comms=comms)


def _ffn_gate(name, hb, wg4, comms=()):
    S, D = hb.shape
    return _mm(name, 'nn', [hb], [wg4], [(0, 0, 0)], 1, _ident, [BF16],
               bm=_blk(S, 1024), bn=wg4.shape[2], bk=D, b_blocked=True, n_outer=True, comms=comms)[0]


def _ffn_upmul(name, hb, wu4, gb, comms=()):
    S, D = hb.shape

    def epi(accs, tes, res):
        g = tes[0].astype(F32)
        return [accs[0], g * _sigmoid(g) * accs[0]]

    return _mm(name, 'nn', [hb], [wu4], [(0, 0, 0)], 1, epi, [BF16, BF16],
               bm=_blk(S, 1024), bn=wu4.shape[2], bk=D, tile_extras=[gb], b_blocked=True, n_outer=True, comms=comms)


def _ffn_down(name, ab, wd, hres, comms=()):
    S, F = ab.shape
    D = wd.shape[1]
    return _mm(name, 'nn', [ab], [wd], [(0, 0, 0)], 1, lambda accs, tes, res: [ALPHA * tes[0] + 0.5 * accs[0]],
               [F32], bm=_blk(S, 1024), bn=_blk(D, 512), bk=F, tile_extras=[hres], comms=comms)[0]


def _ffn_da(name, dsub, wd, gb, ub, comms=()):
    S, D = dsub.shape
    F = wd.shape[0]

    def epi(accs, tes, res):
        da = accs[0]
        g = tes[0].astype(F32)
        u = tes[1].astype(F32)
        s = _sigmoid(g)
        silu = g * s
        return [(da * u) * (s + silu * (1.0 - s)), da * silu]

    return _mm(name, 'nt', [dsub], [wd], [(0, 0, 0)], 1, epi, [BF16, BF16],
               bm=_blk(S, 1024), bn=_blk(F, 512), bk=D, tile_extras=[gb, ub], comms=comms)


def _dw_rows(name, ab, dsub, cs, comms=()):
    S, D = dsub.shape
    return _mm(name, 'tn', [ab], [dsub], [(0, 0, 0)], 1, _ident, [BF16],
               bm=cs, bn=_blk(D, 1024), bk=_blk(S, 2048), comms=comms)[0].reshape(N_CHIPS, cs, D)


def _dw_cols(name, hb, dxb, cs, comms=()):
    S, D = hb.shape
    return _mm(name, 'tn', [hb], [dxb], [(0, 0, 0)], 1, _ident, [BF16],
               bm=_blk(D, 1024), bn=cs, bk=_blk(S, 2048), out_blocked=True, comms=comms)[0]


def _ffn_dh(name, dgb, dub, wg4, wu4, dres, comms=()):
    S = dgb.shape[0]
    D = wg4.shape[1]
    return _mm(name, 'nt', [dgb, dub], [wg4, wu4], [(0, 0, 0), (1, 1, 0)], 1,
               lambda accs, tes, res: [accs[0] + tes[0]], [F32],
               bm=_blk(S, 1024), bn=_blk(D, 1024), bk=wg4.shape[2], tile_extras=[dres], b_blocked=True,
               comms=comms)[0]


def _step(p):
    x = p['x'][0]
    target = p['loss_target'][0]
    S, D = x.shape
    cx, cy, cc = lax.axis_index("x"), lax.axis_index("y"), lax.axis_index("c")
    chip = 2 * cx + cy
    c_idx = jnp.reshape(cc, (1,)).astype(jnp.int32)
    place_idx = jnp.stack([chip, cc]).astype(jnp.int32)

    w_s = p['w_s'][0]
    H = w_s.shape[0]
    DA = H * HEAD
    bsb = jnp.broadcast_to(p['b_s'][0][:, :, None], (H, HEAD, HEAD))
    conv_shard = p['conv_w'][0]
    conv8 = jnp.zeros((8, conv_shard.shape[1]), F32).at[:CONV_TAPS].set(conv_shard)

    W = {'ffa_gate': _cast_into("cast_ffa_gate", p['ffa_gate'][0], place_idx)}
    conv = {'full': lax.dynamic_update_slice(jnp.zeros((N_CHIPS,) + conv8.shape, F32), conv8[None], (chip, 0, 0))}

    def gathered(names, with_conv=False):
        def sink(al, fr):
            W.update(zip(names, al))
            if with_conv:
                conv['full'] = al[len(names)]
        return sink

    def c_gather(names, ici, fwd, with_conv=False):
        return _c_gather([W[n] for n in names], gathered(names, with_conv), ici, fwd,
                         conv['full'] if with_conv else None)

    rest = [n for n in BIG if n != 'ffa_gate']
    xb, casted = _cast_behind(x, [p[n][0] for n in rest], place_idx, comms=[c_gather(['ffa_gate'], True, True)])
    W.update(zip(rest, casted))
    cs_ff = W['ffa_gate'].shape[2]
    F = N_CHIPS * cs_ff

    ga = _ffn_gate("ffa_gate", xb, W['ffa_gate'],
                   comms=[c_gather(['ffa_up'], True, True), c_gather(['ffa_down'], True, False)])
    ua, aa = _ffn_upmul("ffa_up", xb, W['ffa_up'], ga,
                        comms=[c_gather(['ffa_down'], False, True),
                               c_gather(['w_in', 'ffc_gate'], True, False, with_conv=True)])
    cw8 = jnp.transpose(conv['full'], (1, 0, 2)).reshape(8, DA)
    wd_a = W['ffa_down'].reshape(F, D)
    r1 = _ffn_down("ffa_down", aa, wd_a, x,
                   comms=[c_gather(['w_in', 'ffc_gate'], False, True), c_gather(['w_out', 'ffc_down'], True, False)])
    h1, h1b = _ln_fwd("ln_a", r1, p['ln_a_g'], p['ln_a_b'])
    w_in4 = W['w_in']
    cs_in = w_in4.shape[2]
    z = _mm("w_in", 'nn', [h1b], [w_in4], [(0, 0, 0)], 1, lambda accs, tes, res: [accs[0] + res[0]], [F32],
            bm=_blk(S, 1024), bn=cs_in, bk=D, row_extras=[p['b_in']], b_blocked=True, n_outer=True,
            comms=[c_gather(['w_out', 'ffc_down'], False, True), c_gather(['ffc_up'], True, False)])[0]
    w_out = W['w_out'].reshape(DA * 2, D)
    tm = _blk(S, 256)
    yb = _mix_fwd(z, w_s, bsb, p['ln_v_g'], p['ln_v_b'], cw8, tm)
    r2 = _mm("w_out", 'nn', [yb], [w_out], [(0, 0, 0)], 1,
             lambda accs, tes, res: [accs[0] + res[0] + ALPHA * tes[0]], [F32],
             bm=_blk(S, 1024), bn=_blk(D, 1024), bk=2 * DA, tile_extras=[h1], row_extras=[p['b_out']],
             comms=[c_gather(['ffc_up'], False, True)])[0]
    h2, h2b = _ln_fwd("ln_m", r2, p['ln_m_g'], p['ln_m_b'])
    gc, uc, ac = _ffn_up("ffc_up", h2b, W['ffc_gate'], W['ffc_up'])
    wd_c = W['ffc_down'].reshape(F, D)
    r3 = _ffn_down("ffc_down", ac, wd_c, h2)

    pg, sg, sib, psum, recv, g2 = {}, {}, {}, {}, {}, {}

    def c_pair(names):
        return _c_rs_pair([pg[n] for n in names], lambda al, fr: sib.update(zip(names, fr)))

    def c_chips(names):
        return _c_rs_chips([psum[n] for n in names], lambda al, fr: recv.update(zip(names, fr)))

    def c_swap(names):
        return _c_exchange([g2[n] for n in names], lambda al, fr: g2.update(zip(names, al)))

    def sum_pair(*names):
        for n in names:
            psum[n] = _sum_pair(f"sum_pair_{n}", pg[n], sib[n], c_idx)

    def sum_chips(*names):
        for n in names:
            g2[n] = _sum_chips(f"sum_chips_{n}", psum[n], recv[n], place_idx)

    dres3, dsub3, sg['ln_c_g'], sg['ln_c_b'], lsum = _ln_loss_bwd(
        "ln_c_loss_bwd", r3, target, p['ln_c_g'], p['ln_c_b'], 0.5)

    dgc, duc = _ffn_da("ffc_da", dsub3, wd_c, gc, uc)
    pg['ffc_down'] = _dw_rows("ffc_dwd", ac, dsub3, cs_ff)
    pg['ffc_gate'] = _dw_cols("ffc_dwg", h2b, dgc, cs_ff)
    pg['ffc_up'] = _dw_cols("ffc_dwu", h2b, duc, cs_ff, comms=[c_pair(['ffc_down', 'ffc_gate'])])
    sum_pair('ffc_down', 'ffc_gate')
    dh2 = _ffn_dh("ffc_dh", dgc, duc, W['ffc_gate'], W['ffc_up'], dres3,
                  comms=[c_pair(['ffc_up']), c_chips(['ffc_down']), c_chips(['ffc_gate'])])
    sum_pair('ffc_up')
    sum_chips('ffc_down', 'ffc_gate')

    dres2, dmix, sg['ln_m_g'], sg['ln_m_b'], sg['b_out'] = _ln_bwd("ln_m_bwd", r2, dh2, p['ln_m_g'], 1.0)
    pg['w_out'] = _mm("dw_out", 'tn', [yb], [dmix], [(0, 0, 0)], 1, _ident, [BF16],
                      bm=_blk(2 * DA, 1024), bn=_blk(D, 1024),
                      bk=_blk(S, 2048))[0].reshape(N_CHIPS, 2 * DA // 4, D)
    dy = _mm("dy", 'nt', [dmix], [w_out], [(0, 0, 0)], 1, _ident, [F32],
             bm=_blk(S, 1024), bn=_blk(2 * DA, 1024), bk=D)[0]
    dz, dws, dbs, sg['ln_v_g'], sg['ln_v_b'], dcw, sg['b_in'] = _mix_bwd(
        z, dy, w_s, bsb, p['ln_v_g'], p['ln_v_b'], cw8, tm)
    sg['w_s'] = dws
    sg['b_s'] = dbs[:, :, 0]
    pg['w_in'] = _dw_cols("dw_in", h1b, dz, cs_in,
                          comms=[c_chips(['ffc_up']), c_pair(['w_out']), c_swap(['ffc_down', 'ffc_gate'])])
    sum_chips('ffc_up')
    sum_pair('w_out')
    dh1 = _mm("dh1", 'nt', [dz], [w_in4], [(0, 0, 0)], 1, lambda accs, tes, res: [accs[0] + tes[0]], [F32],
              bm=_blk(S, 1024), bn=_blk(D, 1024), bk=cs_in, tile_extras=[dres2], b_blocked=True,
              comms=[c_pair(['w_in']), c_chips(['w_out']), c_swap(['ffc_up'])])[0]
    sum_pair('w_in')
    sum_chips('w_out')

    dres1, dsub1, sg['ln_a_g'], sg['ln_a_b'], _ = _ln_bwd("ln_a_bwd", r1, dh1, p['ln_a_g'], 0.5)
    def as_rows(a):
        a = a.reshape(-1, 128)
        pad = (-a.shape[0]) % 8
        return jnp.pad(a, ((0, pad), (0, 0))) if pad else a

    pieces, offsets, off = [], {}, 0
    for n in SMALL + ['conv_w']:
        piece = as_rows(dcw[:CONV_TAPS] if n == 'conv_w' else sg[n])
        offsets[n] = off
        off += piece.shape[0]
        pieces.append(piece)
    packed = jnp.concatenate(pieces, axis=0)
    small = {'buf': jnp.zeros((8,) + packed.shape, F32).at[0].set(packed)}
    dga, dua = _ffn_da("ffa_da", dsub1, wd_a, ga, ua,
                       comms=[c_chips(['w_in']), c_swap(['w_out']),
                              _c_allgather_small(small['buf'], lambda al, fr: small.update(buf=al[0]))])
    sum_chips('w_in')
    gsum = _sum_small(small['buf'], jnp.reshape(4 * cx + 2 * cy + cc, (1,)).astype(jnp.int32))
    pg['ffa_down'] = _dw_rows("ffa_dwd", aa, dsub1, cs_ff)
    pg['ffa_gate'] = _dw_cols("ffa_dwg", xb, dga, cs_ff)
    pg['ffa_up'] = _dw_cols("ffa_dwu", xb, dua, cs_ff, comms=[c_pair(['ffa_down', 'ffa_gate']), c_swap(['w_in'])])
    sum_pair('ffa_down', 'ffa_gate')
    _comm_call("pair_ffa_up", [c_pair(['ffa_up'])])
    sum_pair('ffa_up')
    dx = _ffn_dh("ffa_dh", dga, dua, W['ffa_gate'], W['ffa_up'], dres1,
                 comms=[c_chips(['ffa_down']), c_chips(['ffa_gate']), c_chips(['ffa_up'])])
    sum_chips('ffa_down', 'ffa_gate', 'ffa_up')
    _comm_call("swap_ffa", [c_swap(['ffa_down', 'ffa_gate', 'ffa_up'])])

    out = {}
    for n in BIG:
        g, d, m2, v2 = _adamw(f"adamw_{n}", p[n][0], g2[n].reshape(p[n][0].shape), p['m_' + n][0], p['v_' + n][0])
        shp = p[n].shape
        out[n] = (g.reshape(shp), d.reshape(shp), m2.reshape(shp), v2.reshape(shp))

    loss = lax.psum(0.5 * jnp.sum(lsum) / D, ("x", "y", "c"))
    n_conv = CONV_TAPS * DA // 128
    conv_sum = gsum[offsets['conv_w']:offsets['conv_w'] + n_conv].reshape(CONV_TAPS, DA)
    cw = conv_shard.shape[1]
    gconv = lax.dynamic_slice_in_dim(conv_sum, (2 * cx + cy) * cw, cw, axis=1).reshape(-1, 128)
    names = SMALL + ['conv_w']
    rows128 = lambda a: a.reshape(-1, 128)
    res = _adamw_small(gsum, gconv, [offsets[n] if n != 'conv_w' else None for n in names],
                       [rows128(p[n]) for n in names], [rows128(p['m_' + n]) for n in names],
                       [rows128(p['v_' + n]) for n in names])
    for i, n in enumerate(names):
        out[n] = tuple(r.reshape(p[n].shape) for r in res[4 * i:4 * i + 4])

    return (loss, dx.reshape(p['x'].shape), *[out[n][0] for n in WEIGHTS], *[out[n][1] for n in WEIGHTS],
            *[out[n][2] for n in WEIGHTS], *[out[n][3] for n in WEIGHTS])


def kernel(x, ffa_gate, ffa_up, ffa_down, ln_a_g, ln_a_b, w_in, b_in, w_s, b_s, ln_v_g, ln_v_b, conv_w, w_out, b_out, ln_m_g, ln_m_b, ffc_gate, ffc_up, ffc_down, ln_c_g, ln_c_b, loss_target, m_ffa_gate, m_ffa_up, m_ffa_down, m_ln_a_g, m_ln_a_b, m_w_in, m_b_in, m_w_s, m_b_s, m_ln_v_g, m_ln_v_b, m_conv_w, m_w_out, m_b_out, m_ln_m_g, m_ln_m_b, m_ffc_gate, m_ffc_up, m_ffc_down, m_ln_c_g, m_ln_c_b, v_ffa_gate, v_ffa_up, v_ffa_down, v_ln_a_g, v_ln_a_b, v_w_in, v_b_in, v_w_s, v_b_s, v_ln_v_g, v_ln_v_b, v_conv_w, v_w_out, v_b_out, v_ln_m_g, v_ln_m_b, v_ffc_gate, v_ffc_up, v_ffc_down, v_ln_c_g, v_ln_c_b):
    return _step(dict(locals()))
```

```python
import functools
import math

import jax
import jax.numpy as jnp
from jax import lax
from jax.experimental import pallas as pl
from jax.experimental.pallas import tpu as pltpu

F32 = jnp.float32
BF16 = jnp.bfloat16
MESH = pl.DeviceIdType.MESH
ANY = pl.BlockSpec(memory_space=pl.ANY)

HEAD = 128
CONV_TAPS = 3
LN_EPS = 1e-5
ALPHA = float(2.0 ** 0.25)
ADAM_LR, ADAM_B1, ADAM_B2, ADAM_EPS, ADAM_WD, ADAM_STEP = 0.001, 0.9, 0.999, 1e-08, 0.01, 10

V7X_VMEM_BYTES = 64 * 2 ** 20
VMEM_COMPILER_RESERVE = 6 * 2 ** 20
VMEM_MIN_REQUEST = 32 * 2 ** 20
VMEM_ESTIMATE_SLACK = 1.3
N_CHIPS = 4

BIG = ['ffa_gate', 'ffa_up', 'ffa_down', 'w_in', 'w_out', 'ffc_gate', 'ffc_up', 'ffc_down']
WEIGHTS = ['ffa_gate', 'ffa_up', 'ffa_down', 'ln_a_g', 'ln_a_b', 'w_in', 'b_in', 'w_s', 'b_s', 'ln_v_g', 'ln_v_b',
           'conv_w', 'w_out', 'b_out', 'ln_m_g', 'ln_m_b', 'ffc_gate', 'ffc_up', 'ffc_down', 'ln_c_g', 'ln_c_b']
SMALL = [n for n in WEIGHTS if n not in BIG and n != 'conv_w']


def _vmem_limit(estimate_bytes):
    return int(min(max(estimate_bytes * VMEM_ESTIMATE_SLACK, VMEM_MIN_REQUEST),
                   V7X_VMEM_BYTES - VMEM_COMPILER_RESERVE))


def _blk(dim, pref, mult=128):
    if dim <= pref:
        return dim
    best = None
    for d in range(mult, pref + 1, mult):
        if dim % d == 0:
            best = d
    assert best is not None, (dim, pref)
    return best


def _rows(n_rows, row_bytes, target=4 * 2 ** 20):
    return _blk(n_rows, max(16, target // row_bytes), 16)


def _sigmoid(x):
    return 0.5 * jnp.tanh(0.5 * x) + 0.5


_GELU_C = math.sqrt(2.0 / math.pi)
_GELU_A = 0.044715


def _gelu(x):
    return (0.5 * x) * (1.0 + jnp.tanh(x * (_GELU_C + (_GELU_C * _GELU_A) * (x * x))))


def _gelu_and_grad(x):
    x2 = x * x
    t = jnp.tanh(x * (_GELU_C + (_GELU_C * _GELU_A) * x2))
    half_x = 0.5 * x
    one_plus_t = 1.0 + t
    dg = 0.5 * one_plus_t + half_x * (1.0 - t * t) * (_GELU_C + (3.0 * _GELU_C * _GELU_A) * x2)
    return half_x * one_plus_t, dg


def _ln_stats(r):
    mu = jnp.mean(r, axis=-1, keepdims=True)
    d = r - mu
    var = jnp.mean(d * d, axis=-1, keepdims=True)
    rstd = lax.rsqrt(var + LN_EPS)
    return d * rstd, rstd


_DIMS = {'nn': (((1,), (0,)), ((), ())), 'nt': (((1,), (1,)), ((), ())), 'tn': (((0,), (0,)), ((), ()))}


def _mm(name, mode, a_list, b_list, pairs, n_acc, epilogue, out_dtypes, bm, bn, bk,
        tile_extras=(), row_extras=(), b_blocked=False, out_blocked=False, n_outer=False, comms=()):
    a0, b0 = a_list[0], b_list[0]
    if mode == 'nn':
        M, K = a0.shape
        N = b0.shape[0] * b0.shape[2] if b_blocked else b0.shape[1]
    elif mode == 'nt':
        M, K = a0.shape
        N = b0.shape[1] if b_blocked else b0.shape[0]
    else:
        K, M = a0.shape
        N = b0.shape[1]
        assert not b_blocked
    assert M % bm == 0 and N % bn == 0 and K % bk == 0, (name, M, N, K, bm, bn, bk)
    gm, gn, gk = M // bm, N // bn, K // bk
    if b_blocked:
        assert (bn if mode == 'nn' else bk) == b0.shape[2], name
    dims = _DIMS[mode]

    def spec(shape, f):
        if n_outer:
            return pl.BlockSpec(shape, lambda g0, g1, g2: f(g1, g0, g2))
        return pl.BlockSpec(shape, f)

    if mode == 'tn':
        a_spec = spec((bk, bm), lambda i, j, k: (k, i))
    else:
        a_spec = spec((bm, bk), lambda i, j, k: (i, k))
    if mode == 'nn':
        b_spec = (spec((None, bk, bn), lambda i, j, k: (j, k, 0)) if b_blocked
                  else spec((bk, bn), lambda i, j, k: (k, j)))
    elif mode == 'nt':
        b_spec = (spec((None, bn, bk), lambda i, j, k: (k, j, 0)) if b_blocked
                  else spec((bn, bk), lambda i, j, k: (j, k)))
    else:
        b_spec = spec((bk, bn), lambda i, j, k: (k, j))
    te_spec = spec((bm, bn), lambda i, j, k: (i, j))
    re_spec = spec((1, bn), lambda i, j, k: (0, j))
    if out_blocked:
        o_spec = spec((None, bm, bn), lambda i, j, k: (j, i, 0))
        o_shape = (gn, M, bn)
    else:
        o_spec = te_spec
        o_shape = (M, N)

    na, nb, nte, nre, no = len(a_list), len(b_list), len(tile_extras), len(row_extras), len(out_dtypes)

    n_scratch_acc = n_acc if gk > 1 else 0
    grid = (gn, gm, gk) if n_outer else (gm, gn, gk)
    cm_in_arrays, cm_out_shapes, cm_scratch, aliases = _comm_operands(comms, na + nb + nte + nre, no)
    n_cm_in, n_cm_out = len(cm_in_arrays), len(cm_out_shapes)

    def body(*refs):
        a_refs = refs[:na]
        b_refs = refs[na:na + nb]
        te_refs = refs[na + nb:na + nb + nte]
        re_refs = refs[na + nb + nte:na + nb + nte + nre]
        n_in = na + nb + nte + nre
        o_refs = refs[n_in + n_cm_in:n_in + n_cm_in + no]
        acc_refs = refs[n_in + n_cm_in + no + n_cm_out:n_in + n_cm_in + no + n_cm_out + n_scratch_acc]
        hosted = _comm_refs(comms, refs[n_in:n_in + n_cm_in], refs[n_in + n_cm_in + no:n_in + n_cm_in + no + n_cm_out],
                            refs[n_in + n_cm_in + no + n_cm_out + n_scratch_acc:])
        _comm_first_step(comms, hosted, grid)
        _comm_mid_step(comms, hosted, grid)

        def finish(accs):
            outs = epilogue(accs, [r[...] for r in te_refs], [r[...] for r in re_refs])
            for o_ref, o in zip(o_refs, outs):
                o_ref[...] = o.astype(o_ref.dtype)

        def products():
            parts = [None] * n_acc
            for ai, bi, ci in pairs:
                d = lax.dot_general(a_refs[ai][...], b_refs[bi][...], dims, preferred_element_type=F32)
                parts[ci] = d if parts[ci] is None else parts[ci] + d
            return parts

        if gk == 1:
            finish(products())
        else:
            kk = pl.program_id(2)

            @pl.when(kk == 0)
            def _():
                for r, part in zip(acc_refs, products()):
                    r[...] = part

            if gk > 2:
                @pl.when((kk > 0) & (kk < gk - 1))
                def _():
                    for r, part in zip(acc_refs, products()):
                        r[...] += part

            @pl.when(kk == gk - 1)
            def _():
                finish([r[...] + part for r, part in zip(acc_refs, products())])

        _comm_last_step(comms, hosted, grid)

    isz = lambda a: jnp.dtype(a.dtype).itemsize
    est = 2 * (sum(bm * bk * isz(a) for a in a_list) + sum(bk * bn * isz(b) for b in b_list)
               + sum(bm * bn * isz(t) for t in tile_extras)
               + sum(bm * bn * jnp.dtype(d).itemsize for d in out_dtypes))
    est += (2 * n_acc + 2) * bm * bn * 4
    outs = pl.pallas_call(
        body, name=name, grid=grid,
        in_specs=[a_spec] * na + [b_spec] * nb + [te_spec] * nte + [re_spec] * nre + [ANY] * len(cm_in_arrays),
        out_specs=[o_spec] * no + [ANY] * len(cm_out_shapes),
        out_shape=[jax.ShapeDtypeStruct(o_shape, d) for d in out_dtypes] + cm_out_shapes,
        scratch_shapes=[pltpu.VMEM((bm, bn), F32)] * n_scratch_acc + cm_scratch,
        input_output_aliases=aliases,
        compiler_params=pltpu.CompilerParams(dimension_semantics=("parallel", "parallel", "arbitrary"),
                                             vmem_limit_bytes=_vmem_limit(est)),
    )(*a_list, *b_list, *tile_extras, *row_extras, *cm_in_arrays)
    _deliver(comms, outs[no:])
    return outs[:no]


def _rowwise(name, fn, br, row_ins, vec_ins, row_outs, acc_outs=()):
    S = row_ins[0].shape[0]
    assert S % br == 0, (name, S, br)
    nr, nv, no, nacc = len(row_ins), len(vec_ins), len(row_outs), len(acc_outs)

    def body(*refs):
        rin = refs[:nr]
        vin = refs[nr:nr + nv]
        rout = refs[nr + nv:nr + nv + no]
        aout = refs[nr + nv + no:]
        if nacc:
            @pl.when(pl.program_id(0) == 0)
            def _():
                for a in aout:
                    a[...] = jnp.zeros_like(a)
        fn(rin, vin, rout, aout)

    def full_spec(shape):
        nd = len(shape)
        return pl.BlockSpec(shape, lambda i: (0,) * nd)

    est = 2 * (sum(br * a.shape[1] * jnp.dtype(a.dtype).itemsize for a in row_ins)
               + sum(br * c * jnp.dtype(d).itemsize for c, d in row_outs))
    est += 6 * br * max(a.shape[1] for a in row_ins) * 4
    return pl.pallas_call(
        body, name=name, grid=(S // br,),
        in_specs=[pl.BlockSpec((br, a.shape[1]), lambda i: (i, 0)) for a in row_ins]
        + [full_spec(v.shape) for v in vec_ins],
        out_specs=[pl.BlockSpec((br, c), lambda i: (i, 0)) for c, _ in row_outs]
        + [full_spec(s) for s, _ in acc_outs],
        out_shape=[jax.ShapeDtypeStruct((S, c), d) for c, d in row_outs]
        + [jax.ShapeDtypeStruct(s, d) for s, d in acc_outs],
        compiler_params=pltpu.CompilerParams(dimension_semantics=("arbitrary",),
                                             vmem_limit_bytes=_vmem_limit(est)),
    )(*row_ins, *vec_ins)


def _ln_fwd(name, r, g, b):
    def fn(rin, vin, rout, aout):
        xhat, _ = _ln_stats(rin[0][...])
        h = xhat * vin[0][...] + vin[1][...]
        rout[0][...] = h
        rout[1][...] = h.astype(BF16)
    D = r.shape[1]
    return _rowwise(name, fn, _rows(r.shape[0], D * 4), [r], [g, b], [(D, F32), (D, BF16)])


def _ln_loss_bwd(name, r, target, g, b, sub_scale):
    D = r.shape[1]

    def fn(rin, vin, rout, aout):
        xhat, rstd = _ln_stats(rin[0][...])
        gain = vin[0][...]
        err = xhat * gain + vin[1][...] - rin[1][...]
        dhv = err * (1.0 / D)
        dxh = dhv * gain
        m1 = jnp.mean(dxh, axis=-1, keepdims=True)
        m2 = jnp.mean(dxh * xhat, axis=-1, keepdims=True)
        dr = rstd * (dxh - m1 - xhat * m2)
        rout[0][...] = ALPHA * dr
        rout[1][...] = (sub_scale * dr).astype(BF16)
        aout[0][...] += jnp.sum(dhv * xhat, axis=0, keepdims=True)
        aout[1][...] += jnp.sum(dhv, axis=0, keepdims=True)
        aout[2][...] += jnp.sum(err * err, axis=0, keepdims=True)
    return _rowwise(name, fn, _rows(r.shape[0], D * 4, 2 * 2 ** 20), [r, target], [g, b],
                    [(D, F32), (D, BF16)], [((1, D), F32)] * 3)


def _ln_bwd(name, r, dh, g, sub_scale):
    D = r.shape[1]

    def fn(rin, vin, rout, aout):
        xhat, rstd = _ln_stats(rin[0][...])
        dhv = rin[1][...]
        dxh = dhv * vin[0][...]
        m1 = jnp.mean(dxh, axis=-1, keepdims=True)
        m2 = jnp.mean(dxh * xhat, axis=-1, keepdims=True)
        dr = rstd * (dxh - m1 - xhat * m2)
        rout[0][...] = ALPHA * dr
        rout[1][...] = (sub_scale * dr).astype(BF16)
        aout[0][...] += jnp.sum(dhv * xhat, axis=0, keepdims=True)
        aout[1][...] += jnp.sum(dhv, axis=0, keepdims=True)
        aout[2][...] += jnp.sum(dr, axis=0, keepdims=True)
    return _rowwise(name, fn, _rows(r.shape[0], D * 4, 2 * 2 ** 20), [r, dh], [g],
                    [(D, F32), (D, BF16)], [((1, D), F32)] * 3)


def _adamw_math(w, g, m, v):
    m2 = ADAM_B1 * m + (1.0 - ADAM_B1) * g
    v2 = ADAM_B2 * v + (1.0 - ADAM_B2) * (g * g)
    m_hat = m2 / (1.0 - ADAM_B1 ** ADAM_STEP)
    v_hat = v2 / (1.0 - ADAM_B2 ** ADAM_STEP)
    delta = -ADAM_LR * (m_hat / (jnp.sqrt(v_hat) + ADAM_EPS) + ADAM_WD * w)
    return delta, m2, v2


def _adamw(name, w, g, m, v):
    def fn(rin, vin, rout, aout):
        gv = rin[1][...]
        d, m2, v2 = _adamw_math(rin[0][...], gv, rin[2][...], rin[3][...])
        rout[0][...] = gv
        rout[1][...] = d
        rout[2][...] = m2
        rout[3][...] = v2
    C = w.shape[1]
    return _rowwise(name, fn, _rows(w.shape[0], C * 4, 2 * 2 ** 20), [w, g, m, v], [], [(C, F32)] * 4)


def _conv_shifted(hc, p1, p2, rowi):
    r1 = jnp.where(rowi == 0, p1, pltpu.roll(hc, 1, 0))
    r2 = jnp.where(rowi == 0, p2, jnp.where(rowi == 1, p1, pltpu.roll(hc, 2, 0)))
    return r1, r2


def _mix_fwd(z, ws, bsb, gv, bv, cw8, tm):
    S, DIN = z.shape
    DA = gv.shape[1]
    DB = DA
    H = DA // HEAD
    o = 2 * DA
    nch = tm // HEAD

    def body(z_ref, zp_ref, ws_ref, bsb_ref, gv_ref, bv_ref, cw_ref, y_ref):
        i = pl.program_id(0)
        ri = lax.broadcasted_iota(jnp.int32, (HEAD, HEAD), 0)
        ci = lax.broadcasted_iota(jnp.int32, (HEAD, HEAD), 1)
        tril = ri >= ci
        for h in range(H):
            cs = slice(h * HEAD, (h + 1) * HEAD)
            vs = slice(DA + h * HEAD, DA + (h + 1) * HEAD)
            wb = jnp.where(tril, ws_ref[h], 0.0).astype(BF16)
            for n in range(nch):
                rs = slice(n * HEAD, (n + 1) * HEAD)
                u = _gelu(z_ref[rs, cs])
                vhat, _ = _ln_stats(_gelu(z_ref[rs, vs]))
                vn = vhat * gv_ref[:, cs] + bv_ref[:, cs]
                mixed = jnp.dot(wb, vn.astype(BF16), preferred_element_type=F32) + bsb_ref[h]
                y_ref[rs, cs] = (u * mixed).astype(BF16)
        zb = z_ref[:, o:o + DB]
        hc = z_ref[:, o + DB:o + 2 * DB] * z_ref[:, o + 2 * DB:]
        keep = (i > 0).astype(F32)
        p1 = zp_ref[7:8, o + DB:o + 2 * DB] * zp_ref[7:8, o + 2 * DB:] * keep
        p2 = zp_ref[6:7, o + DB:o + 2 * DB] * zp_ref[6:7, o + 2 * DB:] * keep
        rowi = lax.broadcasted_iota(jnp.int32, (tm, DB), 0)
        r1, r2 = _conv_shifted(hc, p1, p2, rowi)
        cv = cw_ref[0:1, :] * r2 + cw_ref[1:2, :] * r1 + cw_ref[2:3, :] * hc
        y_ref[:, DA:] = (zb * cv).astype(BF16)

    full3 = lambda i: (0, 0, 0)
    full2 = lambda i: (0, 0)
    est = 2 * (tm * DIN * 4 + tm * (DA + DB) * 2) + 10 * tm * DB * 4
    return pl.pallas_call(
        body, name="mix_fwd", grid=(S // tm,),
        in_specs=[pl.BlockSpec((tm, DIN), lambda i: (i, 0)),
                  pl.BlockSpec((8, DIN), lambda i: (jnp.maximum(i * (tm // 8) - 1, 0), 0)),
                  pl.BlockSpec(ws.shape, full3), pl.BlockSpec(bsb.shape, full3),
                  pl.BlockSpec(gv.shape, full2), pl.BlockSpec(bv.shape, full2), pl.BlockSpec(cw8.shape, full2)],
        out_specs=pl.BlockSpec((tm, DA + DB), lambda i: (i, 0)),
        out_shape=jax.ShapeDtypeStruct((S, DA + DB), BF16),
        compiler_params=pltpu.CompilerParams(dimension_semantics=("arbitrary",),
                                             vmem_limit_bytes=_vmem_limit(est)),
    )(z, z, ws, bsb, gv, bv, cw8)


def _mix_bwd(z, dy, ws, bsb, gv, bv, cw8, tm):
    S, DIN = z.shape
    DA = gv.shape[1]
    DB = DA
    H = DA // HEAD
    o = 2 * DA
    nch = tm // HEAD
    nblk = S // tm

    def body(z_ref, zp_ref, zn_ref, dy_ref, dyn_ref, ws_ref, bsb_ref, gv_ref, bv_ref, cw_ref,
             dz_ref, dws_ref, dbs_ref, dgv_ref, dbv_ref, dcw_ref, dbin_ref):
        i = pl.program_id(0)

        @pl.when(i == 0)
        def _():
            for r in (dws_ref, dbs_ref, dgv_ref, dbv_ref, dcw_ref, dbin_ref):
                r[...] = jnp.zeros_like(r)

        ri = lax.broadcasted_iota(jnp.int32, (HEAD, HEAD), 0)
        ci = lax.broadcasted_iota(jnp.int32, (HEAD, HEAD), 1)
        tril = ri >= ci
        for h in range(H):
            cs = slice(h * HEAD, (h + 1) * HEAD)
            vs = slice(DA + h * HEAD, DA + (h + 1) * HEAD)
            wb = jnp.where(tril, ws_ref[h], 0.0).astype(BF16)
            gvh = gv_ref[:, cs]
            dws_acc = jnp.zeros((HEAD, HEAD), F32)
            dbs_acc = jnp.zeros((HEAD, HEAD), F32)
            dgv_acc = jnp.zeros((1, HEAD), F32)
            dbv_acc = jnp.zeros((1, HEAD), F32)
            dbu_acc = jnp.zeros((1, HEAD), F32)
            dbvv_acc = jnp.zeros((1, HEAD), F32)
            for n in range(nch):
                rs = slice(n * HEAD, (n + 1) * HEAD)
                u, du_dz = _gelu_and_grad(z_ref[rs, cs])
                v, dv_dz = _gelu_and_grad(z_ref[rs, vs])
                vhat, rstd = _ln_stats(v)
                vnb = (vhat * gvh + bv_ref[:, cs]).astype(BF16)
                mixed = jnp.dot(wb, vnb, preferred_element_type=F32) + bsb_ref[h]
                dya = dy_ref[rs, cs]
                dzu = dya * mixed * du_dz
                dmx = dya * u
                dmxb = dmx.astype(BF16)
                dbs_acc += jnp.broadcast_to(jnp.sum(dmx, axis=1, keepdims=True), (HEAD, HEAD))
                dws_acc += lax.dot_general(dmxb, vnb, _DIMS['nt'], preferred_element_type=F32)
                dvn = lax.dot_general(wb, dmxb, _DIMS['tn'], preferred_element_type=F32)
                dgv_acc += jnp.sum(dvn * vhat, axis=0, keepdims=True)
                dbv_acc += jnp.sum(dvn, axis=0, keepdims=True)
                dxh = dvn * gvh
                m1 = jnp.mean(dxh, axis=-1, keepdims=True)
                m2 = jnp.mean(dxh * vhat, axis=-1, keepdims=True)
                dzv = rstd * (dxh - m1 - vhat * m2) * dv_dz
                dz_ref[rs, cs] = dzu.astype(BF16)
                dz_ref[rs, vs] = dzv.astype(BF16)
                dbu_acc += jnp.sum(dzu, axis=0, keepdims=True)
                dbvv_acc += jnp.sum(dzv, axis=0, keepdims=True)
            dws_ref[h] += jnp.where(tril, dws_acc, 0.0)
            dbs_ref[h] += dbs_acc
            dgv_ref[:, cs] += dgv_acc
            dbv_ref[:, cs] += dbv_acc
            dbin_ref[:, cs] += dbu_acc
            dbin_ref[:, vs] += dbvv_acc

        zb = z_ref[:, o:o + DB]
        zc = z_ref[:, o + DB:o + 2 * DB]
        zx = z_ref[:, o + 2 * DB:]
        hc = zc * zx
        keep = (i > 0).astype(F32)
        p1 = zp_ref[7:8, o + DB:o + 2 * DB] * zp_ref[7:8, o + 2 * DB:] * keep
        p2 = zp_ref[6:7, o + DB:o + 2 * DB] * zp_ref[6:7, o + 2 * DB:] * keep
        rowi = lax.broadcasted_iota(jnp.int32, (tm, DB), 0)
        r1, r2 = _conv_shifted(hc, p1, p2, rowi)
        w0, w1, w2 = cw_ref[0:1, :], cw_ref[1:2, :], cw_ref[2:3, :]
        cv = w0 * r2 + w1 * r1 + w2 * hc
        dyb = dy_ref[:, DA:]
        dzb = dyb * cv
        dcv = dyb * zb
        more = (i < nblk - 1).astype(F32)
        n0 = dyn_ref[0:1, DA:] * zn_ref[0:1, o:o + DB] * more
        n1 = dyn_ref[1:2, DA:] * zn_ref[1:2, o:o + DB] * more
        f1 = jnp.where(rowi == tm - 1, n0, pltpu.roll(dcv, tm - 1, 0))
        f2 = jnp.where(rowi == tm - 1, n1, jnp.where(rowi == tm - 2, n0, pltpu.roll(dcv, tm - 2, 0)))
        dhc = w2 * dcv + w1 * f1 + w0 * f2
        dzc = dhc * zx
        dzx = dhc * zc
        dcw_ref[0:1, :] += jnp.sum(dcv * r2, axis=0, keepdims=True)
        dcw_ref[1:2, :] += jnp.sum(dcv * r1, axis=0, keepdims=True)
        dcw_ref[2:3, :] += jnp.sum(dcv * hc, axis=0, keepdims=True)
        dz_ref[:, o:o + DB] = dzb.astype(BF16)
        dz_ref[:, o + DB:o + 2 * DB] = dzc.astype(BF16)
        dz_ref[:, o + 2 * DB:] = dzx.astype(BF16)
        dbin_ref[:, o:o + DB] += jnp.sum(dzb, axis=0, keepdims=True)
        dbin_ref[:, o + DB:o + 2 * DB] += jnp.sum(dzc, axis=0, keepdims=True)
        dbin_ref[:, o + 2 * DB:] += jnp.sum(dzx, axis=0, keepdims=True)

    full3 = lambda i: (0, 0, 0)
    full2 = lambda i: (0, 0)
    prev8 = lambda i: (jnp.maximum(i * (tm // 8) - 1, 0), 0)
    next8 = lambda i: (jnp.minimum((i + 1) * (tm // 8), S // 8 - 1), 0)
    est = 2 * (tm * DIN * 4 + tm * (DA + DB) * 4 + tm * DIN * 2) + 16 * tm * DB * 4
    return pl.pallas_call(
        body, name="mix_bwd", grid=(nblk,),
        in_specs=[pl.BlockSpec((tm, DIN), lambda i: (i, 0)),
                  pl.BlockSpec((8, DIN), prev8), pl.BlockSpec((8, DIN), next8),
                  pl.BlockSpec((tm, DA + DB), lambda i: (i, 0)), pl.BlockSpec((8, DA + DB), next8),
                  pl.BlockSpec(ws.shape, full3), pl.BlockSpec(bsb.shape, full3),
                  pl.BlockSpec(gv.shape, full2), pl.BlockSpec(bv.shape, full2), pl.BlockSpec(cw8.shape, full2)],
        out_specs=[pl.BlockSpec((tm, DIN), lambda i: (i, 0)),
                   pl.BlockSpec(ws.shape, full3), pl.BlockSpec(ws.shape, full3),
                   pl.BlockSpec(gv.shape, full2), pl.BlockSpec(gv.shape, full2),
                   pl.BlockSpec(cw8.shape, full2), pl.BlockSpec((1, DIN), full2)],
        out_shape=[jax.ShapeDtypeStruct((S, DIN), BF16),
                   jax.ShapeDtypeStruct(ws.shape, F32), jax.ShapeDtypeStruct(ws.shape, F32),
                   jax.ShapeDtypeStruct(gv.shape, F32), jax.ShapeDtypeStruct(gv.shape, F32),
                   jax.ShapeDtypeStruct(cw8.shape, F32), jax.ShapeDtypeStruct((1, DIN), F32)],
        compiler_params=pltpu.CompilerParams(dimension_semantics=("arbitrary",),
                                             vmem_limit_bytes=_vmem_limit(est)),
    )(z, z, z, dy, dy, ws, bsb, gv, bv, cw8)


def _place():
    x, y, c = lax.axis_index("x"), lax.axis_index("y"), lax.axis_index("c")
    others = [(1 - x, y), (x, 1 - y), (1 - x, 1 - y)]
    return x, y, c, 2 * x + y, others, [2 * ox + oy for ox, oy in others]


def _half(c, rows_half):
    return pl.ds(pl.multiple_of(c * rows_half, 16), rows_half)


def _rcopy(src, dst, ssem, rsem, dev):
    return pltpu.make_async_remote_copy(src_ref=src, dst_ref=dst, send_sem=ssem, recv_sem=rsem,
                                        device_id=dev, device_id_type=MESH)


class _Comm:
    def __init__(self, aliased, ins, fresh, n_sems, stages, sink, relayed=False):
        self.aliased, self.ins, self.fresh = list(aliased), list(ins), list(fresh)
        self.n_sems, self.stages, self.sink, self.relayed = n_sems, stages, sink, relayed


def _in_hbm(a):
    return pltpu.with_memory_space_constraint(a, pltpu.HBM)


def _comm_operands(comms, n_in, n_out):
    arrays, shapes, scratch, aliases = [], [], [], {}
    for cm in comms:
        for a in cm.aliased:
            aliases[n_in + len(arrays)] = n_out + len(shapes)
            arrays.append(_in_hbm(a))
            shapes.append(pltpu.HBM(a.shape, a.dtype))
        arrays += [_in_hbm(a) for a in cm.ins]
        shapes += [pltpu.HBM(s.shape, s.dtype) for s in cm.fresh]
        scratch += [pltpu.SemaphoreType.DMA((cm.n_sems,))] * 2
    return arrays, shapes, scratch, aliases


def _comm_refs(comms, in_refs, out_refs, sem_refs):
    per, pi, po = [], 0, 0
    for i, cm in enumerate(comms):
        pi += len(cm.aliased)
        ins = in_refs[pi:pi + len(cm.ins)]
        pi += len(cm.ins)
        al = out_refs[po:po + len(cm.aliased)]
        po += len(cm.aliased)
        fr = out_refs[po:po + len(cm.fresh)]
        po += len(cm.fresh)
        per.append((al, ins, fr, sem_refs[2 * i], sem_refs[2 * i + 1]))
    return per


def _comm_first_step(comms, hosted, grid):
    if not comms:
        return
    at_first = functools.reduce(lambda a, b: a & b, [pl.program_id(d) == 0 for d in range(len(grid))])

    @pl.when(at_first)
    def _():
        for cm, refs in zip(comms, hosted):
            for d in cm.stages[0](*refs, False):
                d.start()


def _comm_mid_step(comms, hosted, grid):
    relayed = [(cm, refs) for cm, refs in zip(comms, hosted) if cm.relayed]
    if not relayed:
        return
    n_steps = math.prod(grid)
    assert n_steps >= 3, grid
    mid, at_mid = (2 * n_steps) // 3, None
    for d in reversed(range(len(grid))):
        here = pl.program_id(d) == mid % grid[d]
        at_mid = here if at_mid is None else at_mid & here
        mid //= grid[d]

    @pl.when(at_mid)
    def _():
        for cm, refs in relayed:
            for d in cm.stages[0](*refs, True):
                d.wait()
            for d in cm.stages[1](*refs, False):
                d.start()


def _comm_last_step(comms, hosted, grid):
    if not comms:
        return
    at_last = functools.reduce(lambda a, b: a & b, [pl.program_id(d) == g - 1 for d, g in enumerate(grid)])

    @pl.when(at_last)
    def _():
        for cm, refs in zip(comms, hosted):
            for d in cm.stages[1 if cm.relayed else 0](*refs, True):
                d.wait()
        for cm, refs in zip(comms, hosted):
            for stage in cm.stages[2 if cm.relayed else 1:]:
                for d in stage(*refs, False):
                    d.start()
                for d in stage(*refs, True):
                    d.wait()


def _deliver(comms, outs):
    pos = 0
    for cm in comms:
        al = outs[pos:pos + len(cm.aliased)]
        pos += len(cm.aliased)
        fr = outs[pos:pos + len(cm.fresh)]
        pos += len(cm.fresh)
        cm.sink(list(al), list(fr))


def _comm_call(name, comms):
    in_arrays, out_shapes, scratch, aliases = _comm_operands(comms, 0, 0)
    n_in, n_out = len(in_arrays), len(out_shapes)

    def body(*refs):
        hosted = _comm_refs(comms, refs[:n_in], refs[n_in:n_in + n_out], refs[n_in + n_out:])
        for cm, r in zip(comms, hosted):
            for stage in cm.stages:
                for d in stage(*r, False):
                    d.start()
                for d in stage(*r, True):
                    d.wait()

    outs = pl.pallas_call(body, name=name, in_specs=[ANY] * n_in, out_specs=[ANY] * n_out,
                          out_shape=out_shapes, scratch_shapes=scratch, input_output_aliases=aliases)(*in_arrays)
    _deliver(comms, outs)


def _c_gather(fulls, sink, ici, fwd, conv=None):
    n = len(fulls)
    n_direct = 2 * n + (3 if conv is not None else 0)
    n_ici = n_direct + n if ici else 0

    def stage_direct(al, ins, fr, ssem, rsem, wait):
        x, y, c, chip, others, ochips = _place()
        ds = []
        for m in range(n):
            ah = al[m].shape[1] // 2
            mine = al[m].at[chip, _half(c, ah)]
            for k in range(2):
                i = 2 * m + k
                ds.append(_rcopy(mine, al[m].at[ochips[k], _half(c, ah)], ssem.at[i], rsem.at[i], (x, y, c)) if wait
                          else _rcopy(mine, mine, ssem.at[i], rsem.at[i], (*others[k], c)))
        if conv is not None:
            mine = al[n].at[chip]
            for k in range(3):
                i = 2 * n + k
                ds.append(_rcopy(mine, al[n].at[ochips[k]], ssem.at[i], rsem.at[i], (x, y, c)) if wait
                          else _rcopy(mine, mine, ssem.at[i], rsem.at[i], (*others[k], c)))
        return ds

    def stage_relay(al, ins, fr, ssem, rsem, wait):
        x, y, c, chip, others, ochips = _place()
        got_from = ochips[1] + (ochips[0] - ochips[1]) * c
        send_to = (x + (1 - 2 * x) * (1 - c), y + (1 - 2 * y) * c, c)
        ds = []
        for m in range(n):
            ah = al[m].shape[1] // 2
            i = n_direct + m
            got = al[m].at[got_from, _half(c, ah)]
            ds.append(_rcopy(got, al[m].at[ochips[2], _half(c, ah)], ssem.at[i], rsem.at[i], (x, y, c)) if wait
                      else _rcopy(got, got, ssem.at[i], rsem.at[i], send_to))
        return ds

    def stage_fwd(al, ins, fr, ssem, rsem, wait):
        x, y, c, chip, others, ochips = _place()
        ds = []
        for m in range(n):
            ah = al[m].shape[1] // 2
            for k in range(3):
                i = n_ici + 3 * m + k
                got = al[m].at[ochips[k], _half(c, ah)]
                ds.append(_rcopy(got, al[m].at[ochips[k], _half(1 - c, ah)], ssem.at[i], rsem.at[i], (x, y, c)) if wait
                          else _rcopy(got, got, ssem.at[i], rsem.at[i], (x, y, 1 - c)))
        return ds

    stages = ([stage_direct, stage_relay] if ici else []) + ([stage_fwd] if fwd else [])
    return _Comm(list(fulls) + ([conv] if conv is not None else []), [], [], n_ici + (3 * n if fwd else 0),
                 stages, sink, relayed=ici)


def _c_rs_pair(parts, sink):
    def stage(al, ins, fr, ssem, rsem, wait):
        x, y, c, _, _, _ = _place()
        return [_rcopy(ins[m].at[:, _half(1 - c, ins[m].shape[1] // 2), :], fr[m], ssem.at[m], rsem.at[m],
                       (x, y, 1 - c)) for m in range(len(ins))]
    fresh = [jax.ShapeDtypeStruct((p.shape[0], p.shape[1] // 2, p.shape[2]), p.dtype) for p in parts]
    return _Comm([], parts, fresh, len(parts), [stage], sink)


def _c_rs_chips(psums, sink):
    def stage(al, ins, fr, ssem, rsem, wait):
        x, y, c, _, others, ochips = _place()
        return [_rcopy(ins[m].at[ochips[k]], fr[m].at[k], ssem.at[3 * m + k], rsem.at[3 * m + k], (*others[k], c))
                for m in range(len(ins)) for k in range(3)]
    fresh = [jax.ShapeDtypeStruct((3,) + p.shape[1:], p.dtype) for p in psums]
    return _Comm([], psums, fresh, 3 * len(psums), [stage], sink)


def _c_exchange(g2s, sink):
    def stage(al, ins, fr, ssem, rsem, wait):
        x, y, c, _, _, _ = _place()
        if wait:
            return [_rcopy(al[m].at[c], al[m].at[1 - c], ssem.at[m], rsem.at[m], (x, y, c)) for m in range(len(al))]
        return [_rcopy(al[m].at[c], al[m].at[c], ssem.at[m], rsem.at[m], (x, y, 1 - c)) for m in range(len(al))]
    return _Comm(g2s, [], [], len(g2s), [stage], sink)


def _c_allgather_small(buf, sink):
    def stage(al, ins, fr, ssem, rsem, wait):
        x, y, c, _, _, _ = _place()
        ds = []
        for r in range(1, 8):
            peer = (1 - x if r & 4 else x, 1 - y if r & 2 else y, 1 - c if r & 1 else c)
            ds.append(_rcopy(al[0].at[0], al[0].at[r], ssem.at[r - 1], rsem.at[r - 1], peer))
        return ds
    return _Comm([buf], [], [], 7, [stage], sink)


def _sum_small(buf, dev_idx):
    _, m, n = buf.shape

    def body(me_ref, b_ref, o_ref):
        me = me_ref[0]
        acc = b_ref[me]
        for d in range(1, 8):
            acc = acc + b_ref[jnp.bitwise_xor(me, d)]
        o_ref[...] = acc

    gs = pltpu.PrefetchScalarGridSpec(
        num_scalar_prefetch=1, grid=(1,),
        in_specs=[pl.BlockSpec(buf.shape, lambda i, me: (0, 0, 0))],
        out_specs=pl.BlockSpec((m, n), lambda i, me: (0, 0)))
    return pl.pallas_call(body, name="sum_small", grid_spec=gs,
                          out_shape=jax.ShapeDtypeStruct((m, n), F32))(dev_idx, buf)


def _sum_pair(name, part, sib, c_idx):
    _, A, B = part.shape
    ah = A // 2
    br = _rows(ah, B * 2)

    def body(c_ref, a_ref, b_ref, o_ref):
        o_ref[...] = (a_ref[...].astype(F32) + b_ref[...].astype(F32)).astype(BF16)

    gs = pltpu.PrefetchScalarGridSpec(
        num_scalar_prefetch=1, grid=(N_CHIPS, ah // br),
        in_specs=[pl.BlockSpec((None, None, br, B), lambda j, i, c: (j, c[0], i, 0)),
                  pl.BlockSpec((None, br, B), lambda j, i, c: (j, i, 0))],
        out_specs=pl.BlockSpec((None, br, B), lambda j, i, c: (j, i, 0)))
    return pl.pallas_call(body, name=name, grid_spec=gs,
                          out_shape=pltpu.HBM((N_CHIPS, ah, B), BF16),
                          compiler_params=pltpu.CompilerParams(vmem_limit_bytes=_vmem_limit(2 * 3 * br * B * 2)),
                          )(c_idx, _in_hbm(part.reshape(N_CHIPS, 2, ah, B)), _in_hbm(sib))


def _sum_chips(name, psum, recv, place_idx):
    _, ah, B = psum.shape
    br = _rows(ah, B * 4)

    def body(c_ref, p_ref, r0_ref, r1_ref, r2_ref, o_ref):
        o_ref[...] = ((p_ref[...].astype(F32) + r0_ref[...].astype(F32)) + r1_ref[...].astype(F32)) \
            + r2_ref[...].astype(F32)

    gs = pltpu.PrefetchScalarGridSpec(
        num_scalar_prefetch=1, grid=(ah // br,),
        in_specs=[pl.BlockSpec((None, br, B), lambda i, c: (c[0], i, 0))]
        + [pl.BlockSpec((None, br, B), functools.partial(lambda i, c, k: (k, i, 0), k=k)) for k in range(3)],
        out_specs=pl.BlockSpec((None, br, B), lambda i, c: (c[1], i, 0)))
    return pl.pallas_call(body, name=name, grid_spec=gs,
                          out_shape=pltpu.HBM((2, ah, B), F32),
                          compiler_params=pltpu.CompilerParams(vmem_limit_bytes=_vmem_limit(2 * br * B * (4 * 2 + 4))),
                          )(place_idx, _in_hbm(psum), *[_in_hbm(recv)] * 3)


def _cast_into(name, a, place_idx):
    A, B = a.shape
    br = _rows(A, B * 4)

    def body(c_ref, a_ref, o_ref):
        o_ref[...] = a_ref[...].astype(BF16)

    gs = pltpu.PrefetchScalarGridSpec(
        num_scalar_prefetch=1, grid=(A // br,),
        in_specs=[pl.BlockSpec((br, B), lambda i, c: (i, 0))],
        out_specs=pl.BlockSpec((None, br, B), lambda i, c: (c[0], i, 0)))
    return pl.pallas_call(body, name=name, grid_spec=gs,
                          out_shape=pltpu.HBM((N_CHIPS, A, B), BF16))(place_idx, a)


def _cast_behind(x, shards, place_idx, comms):
    n_steps = 16
    n = len(shards)
    xv = x
    svs = list(shards)

    def pace(v):
        for steps_per_block in (1, 2, 4, 8, 16):
            blocks = n_steps // steps_per_block
            if v.shape[0] % blocks == 0 and (v.shape[0] // blocks) % 16 == 0:
                return steps_per_block
        raise ValueError(v.shape)

    n_in, n_out = 2 + n, 1 + n
    cm_in_arrays, cm_out_shapes, cm_scratch, aliases = _comm_operands(comms, n_in, n_out)
    n_cm_in, n_cm_out = len(cm_in_arrays), len(cm_out_shapes)
    grid = (n_steps,)

    def body(c_ref, x_ref, *rest):
        s_refs = rest[:n]
        xo_ref = rest[n + n_cm_in]
        so_refs = rest[n + n_cm_in + 1:n + n_cm_in + 1 + n]
        o_end = n + n_cm_in + 1 + n
        hosted = _comm_refs(comms, rest[n:n + n_cm_in], rest[o_end:o_end + n_cm_out], rest[o_end + n_cm_out:])
        _comm_first_step(comms, hosted, grid)
        _comm_mid_step(comms, hosted, grid)
        xo_ref[...] = x_ref[...].astype(BF16)
        for s_ref, o_ref in zip(s_refs, so_refs):
            o_ref[...] = s_ref[...].astype(BF16)
        _comm_last_step(comms, hosted, grid)

    def rows(v):
        return v.shape[0] * pace(v) // n_steps

    def in_spec(v):
        return pl.BlockSpec((rows(v), v.shape[1]), functools.partial(lambda i, c, q: (i // q, 0), q=pace(v)))

    def shard_out_spec(v):
        return pl.BlockSpec((None, rows(v), v.shape[1]),
                            functools.partial(lambda i, c, q: (c[0], i // q, 0), q=pace(v)))

    gs = pltpu.PrefetchScalarGridSpec(
        num_scalar_prefetch=1, grid=grid,
        in_specs=[in_spec(v) for v in [xv] + svs] + [ANY] * n_cm_in,
        out_specs=[in_spec(xv)] + [shard_out_spec(v) for v in svs] + [ANY] * n_cm_out,
        scratch_shapes=cm_scratch)
    outs = pl.pallas_call(
        body, name="cast_behind", grid_spec=gs,
        out_shape=[jax.ShapeDtypeStruct(xv.shape, BF16)]
        + [pltpu.HBM((N_CHIPS,) + v.shape, BF16) for v in svs] + cm_out_shapes,
        input_output_aliases=aliases,
        compiler_params=pltpu.CompilerParams(
            dimension_semantics=("arbitrary",),
            vmem_limit_bytes=_vmem_limit(2 * sum(rows(v) * v.shape[1] * 6 for v in [xv] + svs))),
    )(place_idx, xv, *svs, *cm_in_arrays)
    _deliver(comms, outs[n_out:])
    return outs[0].reshape(x.shape), [o.reshape((N_CHIPS,) + s.shape) for o, s in zip(outs[1:n_out], shards)]


def _adamw_small(gsum, gconv, offsets, ws, ms, vs):
    n = len(ws)

    def body(*refs):
        g_ref, gc_ref = refs[0], refs[1]
        w_refs = refs[2:2 + n]
        m_refs = refs[2 + n:2 + 2 * n]
        v_refs = refs[2 + 2 * n:2 + 3 * n]
        outs = refs[2 + 3 * n:]
        for i in range(n):
            rows = w_refs[i].shape[0]
            g = gc_ref[...] if offsets[i] is None else g_ref[offsets[i]:offsets[i] + rows, :]
            d, m2, v2 = _adamw_math(w_refs[i][...], g, m_refs[i][...], v_refs[i][...])
            outs[4 * i][...] = g
            outs[4 * i + 1][...] = d
            outs[4 * i + 2][...] = m2
            outs[4 * i + 3][...] = v2

    vm = pl.BlockSpec(memory_space=pltpu.VMEM)
    out_shape = []
    for w in ws:
        out_shape += [jax.ShapeDtypeStruct(w.shape, F32)] * 4
    return pl.pallas_call(body, name="adamw_small", in_specs=[vm] * (2 + 3 * n), out_specs=[vm] * (4 * n),
                          out_shape=out_shape)(gsum, gconv, *ws, *ms, *vs)


def _ident(accs, tes, res):
    return [accs[0]]


def _ffn_up(name, hb, wg4, wu4, comms=()):
    S, D = hb.shape

    def epi(accs, tes, res):
        g, u = accs
        return [g, u, g * _sigmoid(g) * u]

    return _mm(name, 'nn', [hb], [wg4, wu4], [(0, 0, 0), (0, 1, 1)], 2, epi, [BF16] * 3,
               bm=_blk(S, 512), bn=wg4.shape[2], bk=D, b_blocked=True, n_outer=True, comms=comms)


def _ffn_gate(name, hb, wg4, comms=()):
    S, D = hb.shape
    return _mm(name, 'nn', [hb], [wg4], [(0, 0, 0)], 1, _ident, [BF16],
               bm=_blk(S, 1024), bn=wg4.shape[2], bk=D, b_blocked=True, n_outer=True, comms=comms)[0]


def _ffn_upmul(name, hb, wu4, gb, comms=()):
    S, D = hb.shape

    def epi(accs, tes, res):
        g = tes[0].astype(F32)
        return [accs[0], g * _sigmoid(g) * accs[0]]

    return _mm(name, 'nn', [hb], [wu4], [(0, 0, 0)], 1, epi, [BF16, BF16],
               bm=_blk(S, 1024), bn=wu4.shape[2], bk=D, tile_extras=[gb], b_blocked=True, n_outer=True, comms=comms)


def _ffn_down(name, ab, wd, hres, comms=()):
    S, F = ab.shape
    D = wd.shape[1]
    return _mm(name, 'nn', [ab], [wd], [(0, 0, 0)], 1, lambda accs, tes, res: [ALPHA * tes[0] + 0.5 * accs[0]],
               [F32], bm=_blk(S, 1024), bn=_blk(D, 512), bk=F, tile_extras=[hres], comms=comms)[0]


def _ffn_da(name, dsub, wd, gb, ub, comms=()):
    S, D = dsub.shape
    F = wd.shape[0]

    def epi(accs, tes, res):
        da = accs[0]
        g = tes[0].astype(F32)
        u = tes[1].astype(F32)
        s = _sigmoid(g)
        silu = g * s
        return [(da * u) * (s + silu * (1.0 - s)), da * silu]

    return _mm(name, 'nt', [dsub], [wd], [(0, 0, 0)], 1, epi, [BF16, BF16],
               bm=_blk(S, 1024), bn=_blk(F, 512), bk=D, tile_extras=[gb, ub], comms=comms)


def _dw_rows(name, ab, dsub, cs, comms=()):
    S, D = dsub.shape
    return _mm(name, 'tn', [ab], [dsub], [(0, 0, 0)], 1, _ident, [BF16],
               bm=cs, bn=_blk(D, 1024), bk=_blk(S, 2048), comms=comms)[0].reshape(N_CHIPS, cs, D)


def _dw_cols(name, hb, dxb, cs, comms=()):
    S, D = hb.shape
    return _mm(name, 'tn', [hb], [dxb], [(0, 0, 0)], 1, _ident, [BF16],
               bm=_blk(D, 1024), bn=cs, bk=_blk(S, 2048), out_blocked=True, comms=comms)[0]


def _ffn_dh(name, dgb, dub, wg4, wu4, dres, comms=()):
    S = dgb.shape[0]
    D = wg4.shape[1]
    return _mm(name, 'nt', [dgb, dub], [wg4, wu4], [(0, 0, 0), (1, 1, 0)], 1,
               lambda accs, tes, res: [accs[0] + tes[0]], [F32],
               bm=_blk(S, 1024), bn=_blk(D, 1024), bk=wg4.shape[2], tile_extras=[dres], b_blocked=True,
               comms=comms)[0]


def _step(p):
    x = p['x'][0]
    target = p['loss_target'][0]
    S, D = x.shape
    cx, cy, cc = lax.axis_index("x"), lax.axis_index("y"), lax.axis_index("c")
    chip = 2 * cx + cy
    c_idx = jnp.reshape(cc, (1,)).astype(jnp.int32)
    place_idx = jnp.stack([chip, cc]).astype(jnp.int32)

    w_s = p['w_s'][0]
    H = w_s.shape[0]
    DA = H * HEAD
    bsb = jnp.broadcast_to(p['b_s'][0][:, :, None], (H, HEAD, HEAD))
    conv_shard = p['conv_w'][0]
    conv8 = jnp.zeros((8, conv_shard.shape[1]), F32).at[:CONV_TAPS].set(conv_shard)

    W = {'ffa_gate': _cast_into("cast_ffa_gate", p['ffa_gate'][0], place_idx)}
    conv = {'full': lax.dynamic_update_slice(jnp.zeros((N_CHIPS,) + conv8.shape, F32), conv8[None], (chip, 0, 0))}

    def gathered(names, with_conv=False):
        def sink(al, fr):
            W.update(zip(names, al))
            if with_conv:
                conv['full'] = al[len(names)]
        return sink

    def c_gather(names, ici, fwd, with_conv=False):
        return _c_gather([W[n] for n in names], gathered(names, with_conv), ici, fwd,
                         conv['full'] if with_conv else None)

    rest = [n for n in BIG if n != 'ffa_gate']
    xb, casted = _cast_behind(x, [p[n][0] for n in rest], place_idx, comms=[c_gather(['ffa_gate'], True, True)])
    W.update(zip(rest, casted))
    cs_ff = W['ffa_gate'].shape[2]
    F = N_CHIPS * cs_ff

    ga = _ffn_gate("ffa_gate", xb, W['ffa_gate'],
                   comms=[c_gather(['ffa_up'], True, True), c_gather(['ffa_down'], True, False)])
    ua, aa = _ffn_upmul("ffa_up", xb, W['ffa_up'], ga,
                        comms=[c_gather(['ffa_down'], False, True),
                               c_gather(['w_in', 'ffc_gate'], True, False, with_conv=True)])
    cw8 = jnp.transpose(conv['full'], (1, 0, 2)).reshape(8, DA)
    wd_a = W['ffa_down'].reshape(F, D)
    r1 = _ffn_down("ffa_down", aa, wd_a, x,
                   comms=[c_gather(['w_in', 'ffc_gate'], False, True), c_gather(['w_out', 'ffc_down'], True, False)])
    h1, h1b = _ln_fwd("ln_a", r1, p['ln_a_g'], p['ln_a_b'])
    w_in4 = W['w_in']
    cs_in = w_in4.shape[2]
    z = _mm("w_in", 'nn', [h1b], [w_in4], [(0, 0, 0)], 1, lambda accs, tes, res: [accs[0] + res[0]], [F32],
            bm=_blk(S, 1024), bn=cs_in, bk=D, row_extras=[p['b_in']], b_blocked=True, n_outer=True,
            comms=[c_gather(['w_out', 'ffc_down'], False, True), c_gather(['ffc_up'], True, False)])[0]
    w_out = W['w_out'].reshape(DA * 2, D)
    tm = _blk(S, 128)
    yb = _mix_fwd(z, w_s, bsb, p['ln_v_g'], p['ln_v_b'], cw8, tm)
    r2 = _mm("w_out", 'nn', [yb], [w_out], [(0, 0, 0)], 1,
             lambda accs, tes, res: [accs[0] + res[0] + ALPHA * tes[0]], [F32],
             bm=_blk(S, 1024), bn=_blk(D, 1024), bk=2 * DA, tile_extras=[h1], row_extras=[p['b_out']],
             comms=[c_gather(['ffc_up'], False, True)])[0]
    h2, h2b = _ln_fwd("ln_m", r2, p['ln_m_g'], p['ln_m_b'])
    gc, uc, ac = _ffn_up("ffc_up", h2b, W['ffc_gate'], W['ffc_up'])
    wd_c = W['ffc_down'].reshape(F, D)
    r3 = _ffn_down("ffc_down", ac, wd_c, h2)

    pg, sg, sib, psum, recv, g2 = {}, {}, {}, {}, {}, {}

    def c_pair(names):
        return _c_rs_pair([pg[n] for n in names], lambda al, fr: sib.update(zip(names, fr)))

    def c_chips(names):
        return _c_rs_chips([psum[n] for n in names], lambda al, fr: recv.update(zip(names, fr)))

    def c_swap(names):
        return _c_exchange([g2[n] for n in names], lambda al, fr: g2.update(zip(names, al)))

    def sum_pair(*names):
        for n in names:
            psum[n] = _sum_pair(f"sum_pair_{n}", pg[n], sib[n], c_idx)

    def sum_chips(*names):
        for n in names:
            g2[n] = _sum_chips(f"sum_chips_{n}", psum[n], recv[n], place_idx)

    dres3, dsub3, sg['ln_c_g'], sg['ln_c_b'], lsum = _ln_loss_bwd(
        "ln_c_loss_bwd", r3, target, p['ln_c_g'], p['ln_c_b'], 0.5)

    dgc, duc = _ffn_da("ffc_da", dsub3, wd_c, gc, uc)
    pg['ffc_down'] = _dw_rows("ffc_dwd", ac, dsub3, cs_ff)
    pg['ffc_gate'] = _dw_cols("ffc_dwg", h2b, dgc, cs_ff)
    pg['ffc_up'] = _dw_cols("ffc_dwu", h2b, duc, cs_ff, comms=[c_pair(['ffc_down', 'ffc_gate'])])
    sum_pair('ffc_down', 'ffc_gate')
    dh2 = _ffn_dh("ffc_dh", dgc, duc, W['ffc_gate'], W['ffc_up'], dres3,
                  comms=[c_pair(['ffc_up']), c_chips(['ffc_down']), c_chips(['ffc_gate'])])
    sum_pair('ffc_up')
    sum_chips('ffc_down', 'ffc_gate')

    dres2, dmix, sg['ln_m_g'], sg['ln_m_b'], sg['b_out'] = _ln_bwd("ln_m_bwd", r2, dh2, p['ln_m_g'], 1.0)
    pg['w_out'] = _mm("dw_out", 'tn', [yb], [dmix], [(0, 0, 0)], 1, _ident, [BF16],
                      bm=_blk(2 * DA, 1024), bn=_blk(D, 1024),
                      bk=_blk(S, 2048))[0].reshape(N_CHIPS, 2 * DA // 4, D)
    dy = _mm("dy", 'nt', [dmix], [w_out], [(0, 0, 0)], 1, _ident, [F32],
             bm=_blk(S, 1024), bn=_blk(2 * DA, 1024), bk=D)[0]
    dz, dws, dbs, sg['ln_v_g'], sg['ln_v_b'], dcw, sg['b_in'] = _mix_bwd(
        z, dy, w_s, bsb, p['ln_v_g'], p['ln_v_b'], cw8, tm)
    sg['w_s'] = dws
    sg['b_s'] = dbs[:, :, 0]
    pg['w_in'] = _dw_cols("dw_in", h1b, dz, cs_in,
                          comms=[c_chips(['ffc_up']), c_pair(['w_out']), c_swap(['ffc_down', 'ffc_gate'])])
    sum_chips('ffc_up')
    sum_pair('w_out')
    dh1 = _mm("dh1", 'nt', [dz], [w_in4], [(0, 0, 0)], 1, lambda accs, tes, res: [accs[0] + tes[0]], [F32],
              bm=_blk(S, 1024), bn=_blk(D, 1024), bk=cs_in, tile_extras=[dres2], b_blocked=True,
              comms=[c_pair(['w_in']), c_chips(['w_out']), c_swap(['ffc_up'])])[0]
    sum_pair('w_in')
    sum_chips('w_out')

    dres1, dsub1, sg['ln_a_g'], sg['ln_a_b'], _ = _ln_bwd("ln_a_bwd", r1, dh1, p['ln_a_g'], 0.5)
    def as_rows(a):
        a = a.reshape(-1, 128)
        pad = (-a.shape[0]) % 8
        return jnp.pad(a, ((0, pad), (0, 0))) if pad else a

    pieces, offsets, off = [], {}, 0
    for n in SMALL + ['conv_w']:
        piece = as_rows(dcw[:CONV_TAPS] if n == 'conv_w' else sg[n])
        offsets[n] = off
        off += piece.shape[0]
        pieces.append(piece)
    packed = jnp.concatenate(pieces, axis=0)
    small = {'buf': jnp.zeros((8,) + packed.shape, F32).at[0].set(packed)}
    dga, dua = _ffn_da("ffa_da", dsub1, wd_a, ga, ua,
                       comms=[c_chips(['w_in']), c_swap(['w_out']),
                              _c_allgather_small(small['buf'], lambda al, fr: small.update(buf=al[0]))])
    sum_chips('w_in')
    gsum = _sum_small(small['buf'], jnp.reshape(4 * cx + 2 * cy + cc, (1,)).astype(jnp.int32))
    pg['ffa_down'] = _dw_rows("ffa_dwd", aa, dsub1, cs_ff)
    pg['ffa_gate'] = _dw_cols("ffa_dwg", xb, dga, cs_ff)
    pg['ffa_up'] = _dw_cols("ffa_dwu", xb, dua, cs_ff, comms=[c_pair(['ffa_down', 'ffa_gate']), c_swap(['w_in'])])
    sum_pair('ffa_down', 'ffa_gate')
    _comm_call("pair_ffa_up", [c_pair(['ffa_up'])])
    sum_pair('ffa_up')
    dx = _ffn_dh("ffa_dh", dga, dua, W['ffa_gate'], W['ffa_up'], dres1,
                 comms=[c_chips(['ffa_down']), c_chips(['ffa_gate']), c_chips(['ffa_up'])])
    sum_chips('ffa_down', 'ffa_gate', 'ffa_up')
    _comm_call("swap_ffa", [c_swap(['ffa_down', 'ffa_gate', 'ffa_up'])])

    out = {}
    for n in BIG:
        g, d, m2, v2 = _adamw(f"adamw_{n}", p[n][0], g2[n].reshape(p[n][0].shape), p['m_' + n][0], p['v_' + n][0])
        shp = p[n].shape
        out[n] = (g.reshape(shp), d.reshape(shp), m2.reshape(shp), v2.reshape(shp))

    loss = lax.psum(0.5 * jnp.sum(lsum) / D, ("x", "y", "c"))
    n_conv = CONV_TAPS * DA // 128
    conv_sum = gsum[offsets['conv_w']:offsets['conv_w'] + n_conv].reshape(CONV_TAPS, DA)
    cw = conv_shard.shape[1]
    gconv = lax.dynamic_slice_in_dim(conv_sum, (2 * cx + cy) * cw, cw, axis=1).reshape(-1, 128)
    names = SMALL + ['conv_w']
    rows128 = lambda a: a.reshape(-1, 128)
    res = _adamw_small(gsum, gconv, [offsets[n] if n != 'conv_w' else None for n in names],
                       [rows128(p[n]) for n in names], [rows128(p['m_' + n]) for n in names],
                       [rows128(p['v_' + n]) for n in names])
    for i, n in enumerate(names):
        out[n] = tuple(r.reshape(p[n].shape) for r in res[4 * i:4 * i + 4])

    return (loss, dx.reshape(p['x'].shape), *[out[n][0] for n in WEIGHTS], *[out[n][1] for n in WEIGHTS],
            *[out[n][2] for n in WEIGHTS], *[out[n][3] for n in WEIGHTS])


def kernel(x, ffa_gate, ffa_up, ffa_down, ln_a_g, ln_a_b, w_in, b_in, w_s, b_s, ln_v_g, ln_v_b, conv_w, w_out, b_out, ln_m_g, ln_m_b, ffc_gate, ffc_up, ffc_down, ln_c_g, ln_c_b, loss_target, m_ffa_gate, m_ffa_up, m_ffa_down, m_ln_a_g, m_ln_a_b, m_w_in, m_b_in, m_w_s, m_b_s, m_ln_v_g, m_ln_v_b, m_conv_w, m_w_out, m_b_out, m_ln_m_g, m_ln_m_b, m_ffc_gate, m_ffc_up, m_ffc_down, m_ln_c_g, m_ln_c_b, v_ffa_gate, v_ffa_up, v_ffa_down, v_ln_a_g, v_ln_a_b, v_w_in, v_b_in, v_w_s, v_b_s, v_ln_v_g, v_ln_v_b, v_conv_w, v_w_out, v_b_out, v_ln_m_g, v_ln_m_b, v_ffc_gate, v_ffc_up, v_ffc_down, v_ln_c_g, v_ln_c_b):
    return _step(dict(locals()))
```

```python
import functools
import math

import jax
import jax.numpy as jnp
from jax import lax
from jax.experimental import pallas as pl
from jax.experimental.pallas import tpu as pltpu

F32 = jnp.float32
BF16 = jnp.bfloat16
MESH = pl.DeviceIdType.MESH
ANY = pl.BlockSpec(memory_space=pl.ANY)

HEAD = 128
CONV_TAPS = 3
LN_EPS = 1e-5
ALPHA = float(2.0 ** 0.25)
ADAM_LR, ADAM_B1, ADAM_B2, ADAM_EPS, ADAM_WD, ADAM_STEP = 0.001, 0.9, 0.999, 1e-08, 0.01, 10

V7X_VMEM_BYTES = 64 * 2 ** 20
VMEM_COMPILER_RESERVE = 6 * 2 ** 20
VMEM_MIN_REQUEST = 32 * 2 ** 20
VMEM_ESTIMATE_SLACK = 1.3
N_CHIPS = 4

BIG = ['ffa_gate', 'ffa_up', 'ffa_down', 'w_in', 'w_out', 'ffc_gate', 'ffc_up', 'ffc_down']
WEIGHTS = ['ffa_gate', 'ffa_up', 'ffa_down', 'ln_a_g', 'ln_a_b', 'w_in', 'b_in', 'w_s', 'b_s', 'ln_v_g', 'ln_v_b',
           'conv_w', 'w_out', 'b_out', 'ln_m_g', 'ln_m_b', 'ffc_gate', 'ffc_up', 'ffc_down', 'ln_c_g', 'ln_c_b']
SMALL = [n for n in WEIGHTS if n not in BIG and n != 'conv_w']


def _vmem_limit(estimate_bytes):
    return int(min(max(estimate_bytes * VMEM_ESTIMATE_SLACK, VMEM_MIN_REQUEST),
                   V7X_VMEM_BYTES - VMEM_COMPILER_RESERVE))


def _blk(dim, pref, mult=128):
    if dim <= pref:
        return dim
    best = None
    for d in range(mult, pref + 1, mult):
        if dim % d == 0:
            best = d
    assert best is not None, (dim, pref)
    return best


def _rows(n_rows, row_bytes, target=4 * 2 ** 20):
    return _blk(n_rows, max(16, target // row_bytes), 16)


def _sigmoid(x):
    return 0.5 * jnp.tanh(0.5 * x) + 0.5


_GELU_C = math.sqrt(2.0 / math.pi)
_GELU_A = 0.044715


def _gelu(x):
    return (0.5 * x) * (1.0 + jnp.tanh(x * (_GELU_C + (_GELU_C * _GELU_A) * (x * x))))


def _gelu_and_grad(x):
    x2 = x * x
    t = jnp.tanh(x * (_GELU_C + (_GELU_C * _GELU_A) * x2))
    half_x = 0.5 * x
    one_plus_t = 1.0 + t
    dg = 0.5 * one_plus_t + half_x * (1.0 - t * t) * (_GELU_C + (3.0 * _GELU_C * _GELU_A) * x2)
    return half_x * one_plus_t, dg


def _ln_stats(r):
    mu = jnp.mean(r, axis=-1, keepdims=True)
    d = r - mu
    var = jnp.mean(d * d, axis=-1, keepdims=True)
    rstd = lax.rsqrt(var + LN_EPS)
    return d * rstd, rstd


_DIMS = {'nn': (((1,), (0,)), ((), ())), 'nt': (((1,), (1,)), ((), ())), 'tn': (((0,), (0,)), ((), ()))}


def _mm(name, mode, a_list, b_list, pairs, n_acc, epilogue, out_dtypes, bm, bn, bk,
        tile_extras=(), row_extras=(), b_blocked=False, out_blocked=False, n_outer=False, comms=()):
    a0, b0 = a_list[0], b_list[0]
    if mode == 'nn':
        M, K = a0.shape
        N = b0.shape[0] * b0.shape[2] if b_blocked else b0.shape[1]
    elif mode == 'nt':
        M, K = a0.shape
        N = b0.shape[1] if b_blocked else b0.shape[0]
    else:
        K, M = a0.shape
        N = b0.shape[1]
        assert not b_blocked
    assert M % bm == 0 and N % bn == 0 and K % bk == 0, (name, M, N, K, bm, bn, bk)
    gm, gn, gk = M // bm, N // bn, K // bk
    if b_blocked:
        assert (bn if mode == 'nn' else bk) == b0.shape[2], name
    dims = _DIMS[mode]

    def spec(shape, f):
        if n_outer:
            return pl.BlockSpec(shape, lambda g0, g1, g2: f(g1, g0, g2))
        return pl.BlockSpec(shape, f)

    if mode == 'tn':
        a_spec = spec((bk, bm), lambda i, j, k: (k, i))
    else:
        a_spec = spec((bm, bk), lambda i, j, k: (i, k))
    if mode == 'nn':
        b_spec = (spec((None, bk, bn), lambda i, j, k: (j, k, 0)) if b_blocked
                  else spec((bk, bn), lambda i, j, k: (k, j)))
    elif mode == 'nt':
        b_spec = (spec((None, bn, bk), lambda i, j, k: (k, j, 0)) if b_blocked
                  else spec((bn, bk), lambda i, j, k: (j, k)))
    else:
        b_spec = spec((bk, bn), lambda i, j, k: (k, j))
    te_spec = spec((bm, bn), lambda i, j, k: (i, j))
    re_spec = spec((1, bn), lambda i, j, k: (0, j))
    if out_blocked:
        o_spec = spec((None, bm, bn), lambda i, j, k: (j, i, 0))
        o_shape = (gn, M, bn)
    else:
        o_spec = te_spec
        o_shape = (M, N)

    na, nb, nte, nre, no = len(a_list), len(b_list), len(tile_extras), len(row_extras), len(out_dtypes)

    n_scratch_acc = n_acc if gk > 1 else 0
    grid = (gn, gm, gk) if n_outer else (gm, gn, gk)
    cm_in_arrays, cm_out_shapes, cm_scratch, aliases = _comm_operands(comms, na + nb + nte + nre, no)
    n_cm_in, n_cm_out = len(cm_in_arrays), len(cm_out_shapes)

    def body(*refs):
        a_refs = refs[:na]
        b_refs = refs[na:na + nb]
        te_refs = refs[na + nb:na + nb + nte]
        re_refs = refs[na + nb + nte:na + nb + nte + nre]
        n_in = na + nb + nte + nre
        o_refs = refs[n_in + n_cm_in:n_in + n_cm_in + no]
        acc_refs = refs[n_in + n_cm_in + no + n_cm_out:n_in + n_cm_in + no + n_cm_out + n_scratch_acc]
        hosted = _comm_refs(comms, refs[n_in:n_in + n_cm_in], refs[n_in + n_cm_in + no:n_in + n_cm_in + no + n_cm_out],
                            refs[n_in + n_cm_in + no + n_cm_out + n_scratch_acc:])
        _comm_first_step(comms, hosted, grid)
        _comm_mid_step(comms, hosted, grid)

        def finish(accs):
            outs = epilogue(accs, [r[...] for r in te_refs], [r[...] for r in re_refs])
            for o_ref, o in zip(o_refs, outs):
                o_ref[...] = o.astype(o_ref.dtype)

        def products():
            parts = [None] * n_acc
            for ai, bi, ci in pairs:
                d = lax.dot_general(a_refs[ai][...], b_refs[bi][...], dims, preferred_element_type=F32)
                parts[ci] = d if parts[ci] is None else parts[ci] + d
            return parts

        if gk == 1:
            finish(products())
        else:
            kk = pl.program_id(2)

            @pl.when(kk == 0)
            def _():
                for r, part in zip(acc_refs, products()):
                    r[...] = part

            if gk > 2:
                @pl.when((kk > 0) & (kk < gk - 1))
                def _():
                    for r, part in zip(acc_refs, products()):
                        r[...] += part

            @pl.when(kk == gk - 1)
            def _():
                finish([r[...] + part for r, part in zip(acc_refs, products())])

        _comm_last_step(comms, hosted, grid)

    isz = lambda a: jnp.dtype(a.dtype).itemsize
    est = 2 * (sum(bm * bk * isz(a) for a in a_list) + sum(bk * bn * isz(b) for b in b_list)
               + sum(bm * bn * isz(t) for t in tile_extras)
               + sum(bm * bn * jnp.dtype(d).itemsize for d in out_dtypes))
    est += (2 * n_acc + 2) * bm * bn * 4
    outs = pl.pallas_call(
        body, name=name, grid=grid,
        in_specs=[a_spec] * na + [b_spec] * nb + [te_spec] * nte + [re_spec] * nre + [ANY] * len(cm_in_arrays),
        out_specs=[o_spec] * no + [ANY] * len(cm_out_shapes),
        out_shape=[jax.ShapeDtypeStruct(o_shape, d) for d in out_dtypes] + cm_out_shapes,
        scratch_shapes=[pltpu.VMEM((bm, bn), F32)] * n_scratch_acc + cm_scratch,
        input_output_aliases=aliases,
        compiler_params=pltpu.CompilerParams(dimension_semantics=("parallel", "parallel", "arbitrary"),
                                             vmem_limit_bytes=_vmem_limit(est)),
    )(*a_list, *b_list, *tile_extras, *row_extras, *cm_in_arrays)
    _deliver(comms, outs[no:])
    return outs[:no]


def _rowwise(name, fn, br, row_ins, vec_ins, row_outs, acc_outs=()):
    S = row_ins[0].shape[0]
    assert S % br == 0, (name, S, br)
    nr, nv, no, nacc = len(row_ins), len(vec_ins), len(row_outs), len(acc_outs)

    def body(*refs):
        rin = refs[:nr]
        vin = refs[nr:nr + nv]
        rout = refs[nr + nv:nr + nv + no]
        aout = refs[nr + nv + no:]
        if nacc:
            @pl.when(pl.program_id(0) == 0)
            def _():
                for a in aout:
                    a[...] = jnp.zeros_like(a)
        fn(rin, vin, rout, aout)

    def full_spec(shape):
        nd = len(shape)
        return pl.BlockSpec(shape, lambda i: (0,) * nd)

    est = 2 * (sum(br * a.shape[1] * jnp.dtype(a.dtype).itemsize for a in row_ins)
               + sum(br * c * jnp.dtype(d).itemsize for c, d in row_outs))
    est += 6 * br * max(a.shape[1] for a in row_ins) * 4
    return pl.pallas_call(
        body, name=name, grid=(S // br,),
        in_specs=[pl.BlockSpec((br, a.shape[1]), lambda i: (i, 0)) for a in row_ins]
        + [full_spec(v.shape) for v in vec_ins],
        out_specs=[pl.BlockSpec((br, c), lambda i: (i, 0)) for c, _ in row_outs]
        + [full_spec(s) for s, _ in acc_outs],
        out_shape=[jax.ShapeDtypeStruct((S, c), d) for c, d in row_outs]
        + [jax.ShapeDtypeStruct(s, d) for s, d in acc_outs],
        compiler_params=pltpu.CompilerParams(dimension_semantics=("arbitrary",),
                                             vmem_limit_bytes=_vmem_limit(est)),
    )(*row_ins, *vec_ins)


def _ln_fwd(name, r, g, b):
    def fn(rin, vin, rout, aout):
        xhat, _ = _ln_stats(rin[0][...])
        h = xhat * vin[0][...] + vin[1][...]
        rout[0][...] = h
        rout[1][...] = h.astype(BF16)
    D = r.shape[1]
    return _rowwise(name, fn, _rows(r.shape[0], D * 4), [r], [g, b], [(D, F32), (D, BF16)])


def _ln_loss_bwd(name, r, target, g, b, sub_scale):
    D = r.shape[1]

    def fn(rin, vin, rout, aout):
        xhat, rstd = _ln_stats(rin[0][...])
        gain = vin[0][...]
        err = xhat * gain + vin[1][...] - rin[1][...]
        dhv = err * (1.0 / D)
        dxh = dhv * gain
        m1 = jnp.mean(dxh, axis=-1, keepdims=True)
        m2 = jnp.mean(dxh * xhat, axis=-1, keepdims=True)
        dr = rstd * (dxh - m1 - xhat * m2)
        rout[0][...] = ALPHA * dr
        rout[1][...] = (sub_scale * dr).astype(BF16)
        aout[0][...] += jnp.sum(dhv * xhat, axis=0, keepdims=True)
        aout[1][...] += jnp.sum(dhv, axis=0, keepdims=True)
        aout[2][...] += jnp.sum(err * err, axis=0, keepdims=True)
    return _rowwise(name, fn, _rows(r.shape[0], D * 4, 2 * 2 ** 20), [r, target], [g, b],
                    [(D, F32), (D, BF16)], [((1, D), F32)] * 3)


def _ln_bwd(name, r, dh, g, sub_scale):
    D = r.shape[1]

    def fn(rin, vin, rout, aout):
        xhat, rstd = _ln_stats(rin[0][...])
        dhv = rin[1][...]
        dxh = dhv * vin[0][...]
        m1 = jnp.mean(dxh, axis=-1, keepdims=True)
        m2 = jnp.mean(dxh * xhat, axis=-1, keepdims=True)
        dr = rstd * (dxh - m1 - xhat * m2)
        rout[0][...] = ALPHA * dr
        rout[1][...] = (sub_scale * dr).astype(BF16)
        aout[0][...] += jnp.sum(dhv * xhat, axis=0, keepdims=True)
        aout[1][...] += jnp.sum(dhv, axis=0, keepdims=True)
        aout[2][...] += jnp.sum(dr, axis=0, keepdims=True)
    return _rowwise(name, fn, _rows(r.shape[0], D * 4, 2 * 2 ** 20), [r, dh], [g],
                    [(D, F32), (D, BF16)], [((1, D), F32)] * 3)


def _adamw_math(w, g, m, v):
    m2 = ADAM_B1 * m + (1.0 - ADAM_B1) * g
    v2 = ADAM_B2 * v + (1.0 - ADAM_B2) * (g * g)
    m_hat = m2 / (1.0 - ADAM_B1 ** ADAM_STEP)
    v_hat = v2 / (1.0 - ADAM_B2 ** ADAM_STEP)
    delta = -ADAM_LR * (m_hat / (jnp.sqrt(v_hat) + ADAM_EPS) + ADAM_WD * w)
    return delta, m2, v2


def _adamw(name, w, g, m, v):
    def fn(rin, vin, rout, aout):
        gv = rin[1][...]
        d, m2, v2 = _adamw_math(rin[0][...], gv, rin[2][...], rin[3][...])
        rout[0][...] = gv
        rout[1][...] = d
        rout[2][...] = m2
        rout[3][...] = v2
    C = w.shape[1]
    return _rowwise(name, fn, _rows(w.shape[0], C * 4, 2 * 2 ** 20), [w, g, m, v], [], [(C, F32)] * 4)


def _conv_shifted(hc, p1, p2, rowi):
    r1 = jnp.where(rowi == 0, p1, pltpu.roll(hc, 1, 0))
    r2 = jnp.where(rowi == 0, p2, jnp.where(rowi == 1, p1, pltpu.roll(hc, 2, 0)))
    return r1, r2


def _mix_fwd(z, ws, bsb, gv, bv, cw8, tm):
    S, DIN = z.shape
    DA = gv.shape[1]
    DB = DA
    H = DA // HEAD
    o = 2 * DA
    nch = tm // HEAD

    def body(z_ref, zp_ref, ws_ref, bsb_ref, gv_ref, bv_ref, cw_ref, y_ref):
        i = pl.program_id(0)
        ri = lax.broadcasted_iota(jnp.int32, (HEAD, HEAD), 0)
        ci = lax.broadcasted_iota(jnp.int32, (HEAD, HEAD), 1)
        tril = ri >= ci
        for h in range(H):
            cs = slice(h * HEAD, (h + 1) * HEAD)
            vs = slice(DA + h * HEAD, DA + (h + 1) * HEAD)
            wb = jnp.where(tril, ws_ref[h], 0.0).astype(BF16)
            for n in range(nch):
                rs = slice(n * HEAD, (n + 1) * HEAD)
                u = _gelu(z_ref[rs, cs])
                vhat, _ = _ln_stats(_gelu(z_ref[rs, vs]))
                vn = vhat * gv_ref[:, cs] + bv_ref[:, cs]
                mixed = jnp.dot(wb, vn.astype(BF16), preferred_element_type=F32) + bsb_ref[h]
                y_ref[rs, cs] = (u * mixed).astype(BF16)
        zb = z_ref[:, o:o + DB]
        hc = z_ref[:, o + DB:o + 2 * DB] * z_ref[:, o + 2 * DB:]
        keep = (i > 0).astype(F32)
        p1 = zp_ref[7:8, o + DB:o + 2 * DB] * zp_ref[7:8, o + 2 * DB:] * keep
        p2 = zp_ref[6:7, o + DB:o + 2 * DB] * zp_ref[6:7, o + 2 * DB:] * keep
        rowi = lax.broadcasted_iota(jnp.int32, (tm, DB), 0)
        r1, r2 = _conv_shifted(hc, p1, p2, rowi)
        cv = cw_ref[0:1, :] * r2 + cw_ref[1:2, :] * r1 + cw_ref[2:3, :] * hc
        y_ref[:, DA:] = (zb * cv).astype(BF16)

    full3 = lambda i: (0, 0, 0)
    full2 = lambda i: (0, 0)
    est = 2 * (tm * DIN * 4 + tm * (DA + DB) * 2) + 10 * tm * DB * 4
    return pl.pallas_call(
        body, name="mix_fwd", grid=(S // tm,),
        in_specs=[pl.BlockSpec((tm, DIN), lambda i: (i, 0)),
                  pl.BlockSpec((8, DIN), lambda i: (jnp.maximum(i * (tm // 8) - 1, 0), 0)),
                  pl.BlockSpec(ws.shape, full3), pl.BlockSpec(bsb.shape, full3),
                  pl.BlockSpec(gv.shape, full2), pl.BlockSpec(bv.shape, full2), pl.BlockSpec(cw8.shape, full2)],
        out_specs=pl.BlockSpec((tm, DA + DB), lambda i: (i, 0)),
        out_shape=jax.ShapeDtypeStruct((S, DA + DB), BF16),
        compiler_params=pltpu.CompilerParams(dimension_semantics=("arbitrary",),
                                             vmem_limit_bytes=_vmem_limit(est)),
    )(z, z, ws, bsb, gv, bv, cw8)


def _mix_bwd(z, dy, ws, bsb, gv, bv, cw8, tm):
    S, DIN = z.shape
    DA = gv.shape[1]
    DB = DA
    H = DA // HEAD
    o = 2 * DA
    nch = tm // HEAD
    nblk = S // tm

    def body(z_ref, zp_ref, zn_ref, dy_ref, dyn_ref, ws_ref, bsb_ref, gv_ref, bv_ref, cw_ref,
             dz_ref, dws_ref, dbs_ref, dgv_ref, dbv_ref, dcw_ref, dbin_ref):
        i = pl.program_id(0)

        @pl.when(i == 0)
        def _():
            for r in (dws_ref, dbs_ref, dgv_ref, dbv_ref, dcw_ref, dbin_ref):
                r[...] = jnp.zeros_like(r)

        ri = lax.broadcasted_iota(jnp.int32, (HEAD, HEAD), 0)
        ci = lax.broadcasted_iota(jnp.int32, (HEAD, HEAD), 1)
        tril = ri >= ci
        for h in range(H):
            cs = slice(h * HEAD, (h + 1) * HEAD)
            vs = slice(DA + h * HEAD, DA + (h + 1) * HEAD)
            wb = jnp.where(tril, ws_ref[h], 0.0).astype(BF16)
            gvh = gv_ref[:, cs]
            dws_acc = jnp.zeros((HEAD, HEAD), F32)
            dbs_acc = jnp.zeros((HEAD, HEAD), F32)
            dgv_acc = jnp.zeros((1, HEAD), F32)
            dbv_acc = jnp.zeros((1, HEAD), F32)
            dbu_acc = jnp.zeros((1, HEAD), F32)
            dbvv_acc = jnp.zeros((1, HEAD), F32)
            for n in range(nch):
                rs = slice(n * HEAD, (n + 1) * HEAD)
                u, du_dz = _gelu_and_grad(z_ref[rs, cs])
                v, dv_dz = _gelu_and_grad(z_ref[rs, vs])
                vhat, rstd = _ln_stats(v)
                vnb = (vhat * gvh + bv_ref[:, cs]).astype(BF16)
                mixed = jnp.dot(wb, vnb, preferred_element_type=F32) + bsb_ref[h]
                dya = dy_ref[rs, cs]
                dzu = dya * mixed * du_dz
                dmx = dya * u
                dmxb = dmx.astype(BF16)
                dbs_acc += jnp.broadcast_to(jnp.sum(dmx, axis=1, keepdims=True), (HEAD, HEAD))
                dws_acc += lax.dot_general(dmxb, vnb, _DIMS['nt'], preferred_element_type=F32)
                dvn = lax.dot_general(wb, dmxb, _DIMS['tn'], preferred_element_type=F32)
                dgv_acc += jnp.sum(dvn * vhat, axis=0, keepdims=True)
                dbv_acc += jnp.sum(dvn, axis=0, keepdims=True)
                dxh = dvn * gvh
                m1 = jnp.mean(dxh, axis=-1, keepdims=True)
                m2 = jnp.mean(dxh * vhat, axis=-1, keepdims=True)
                dzv = rstd * (dxh - m1 - vhat * m2) * dv_dz
                dz_ref[rs, cs] = dzu.astype(BF16)
                dz_ref[rs, vs] = dzv.astype(BF16)
                dbu_acc += jnp.sum(dzu, axis=0, keepdims=True)
                dbvv_acc += jnp.sum(dzv, axis=0, keepdims=True)
            dws_ref[h] += jnp.where(tril, dws_acc, 0.0)
            dbs_ref[h] += dbs_acc
            dgv_ref[:, cs] += dgv_acc
            dbv_ref[:, cs] += dbv_acc
            dbin_ref[:, cs] += dbu_acc
            dbin_ref[:, vs] += dbvv_acc

        zb = z_ref[:, o:o + DB]
        zc = z_ref[:, o + DB:o + 2 * DB]
        zx = z_ref[:, o + 2 * DB:]
        hc = zc * zx
        keep = (i > 0).astype(F32)
        p1 = zp_ref[7:8, o + DB:o + 2 * DB] * zp_ref[7:8, o + 2 * DB:] * keep
        p2 = zp_ref[6:7, o + DB:o + 2 * DB] * zp_ref[6:7, o + 2 * DB:] * keep
        rowi = lax.broadcasted_iota(jnp.int32, (tm, DB), 0)
        r1, r2 = _conv_shifted(hc, p1, p2, rowi)
        w0, w1, w2 = cw_ref[0:1, :], cw_ref[1:2, :], cw_ref[2:3, :]
        cv = w0 * r2 + w1 * r1 + w2 * hc
        dyb = dy_ref[:, DA:]
        dzb = dyb * cv
        dcv = dyb * zb
        more = (i < nblk - 1).astype(F32)
        n0 = dyn_ref[0:1, DA:] * zn_ref[0:1, o:o + DB] * more
        n1 = dyn_ref[1:2, DA:] * zn_ref[1:2, o:o + DB] * more
        f1 = jnp.where(rowi == tm - 1, n0, pltpu.roll(dcv, tm - 1, 0))
        f2 = jnp.where(rowi == tm - 1, n1, jnp.where(rowi == tm - 2, n0, pltpu.roll(dcv, tm - 2, 0)))
        dhc = w2 * dcv + w1 * f1 + w0 * f2
        dzc = dhc * zx
        dzx = dhc * zc
        dcw_ref[0:1, :] += jnp.sum(dcv * r2, axis=0, keepdims=True)
        dcw_ref[1:2, :] += jnp.sum(dcv * r1, axis=0, keepdims=True)
        dcw_ref[2:3, :] += jnp.sum(dcv * hc, axis=0, keepdims=True)
        dz_ref[:, o:o + DB] = dzb.astype(BF16)
        dz_ref[:, o + DB:o + 2 * DB] = dzc.astype(BF16)
        dz_ref[:, o + 2 * DB:] = dzx.astype(BF16)
        dbin_ref[:, o:o + DB] += jnp.sum(dzb, axis=0, keepdims=True)
        dbin_ref[:, o + DB:o + 2 * DB] += jnp.sum(dzc, axis=0, keepdims=True)
        dbin_ref[:, o + 2 * DB:] += jnp.sum(dzx, axis=0, keepdims=True)

    full3 = lambda i: (0, 0, 0)
    full2 = lambda i: (0, 0)
    prev8 = lambda i: (jnp.maximum(i * (tm // 8) - 1, 0), 0)
    next8 = lambda i: (jnp.minimum((i + 1) * (tm // 8), S // 8 - 1), 0)
    est = 2 * (tm * DIN * 4 + tm * (DA + DB) * 4 + tm * DIN * 2) + 16 * tm * DB * 4
    return pl.pallas_call(
        body, name="mix_bwd", grid=(nblk,),
        in_specs=[pl.BlockSpec((tm, DIN), lambda i: (i, 0)),
                  pl.BlockSpec((8, DIN), prev8), pl.BlockSpec((8, DIN), next8),
                  pl.BlockSpec((tm, DA + DB), lambda i: (i, 0)), pl.BlockSpec((8, DA + DB), next8),
                  pl.BlockSpec(ws.shape, full3), pl.BlockSpec(bsb.shape, full3),
                  pl.BlockSpec(gv.shape, full2), pl.BlockSpec(bv.shape, full2), pl.BlockSpec(cw8.shape, full2)],
        out_specs=[pl.BlockSpec((tm, DIN), lambda i: (i, 0)),
                   pl.BlockSpec(ws.shape, full3), pl.BlockSpec(ws.shape, full3),
                   pl.BlockSpec(gv.shape, full2), pl.BlockSpec(gv.shape, full2),
                   pl.BlockSpec(cw8.shape, full2), pl.BlockSpec((1, DIN), full2)],
        out_shape=[jax.ShapeDtypeStruct((S, DIN), BF16),
                   jax.ShapeDtypeStruct(ws.shape, F32), jax.ShapeDtypeStruct(ws.shape, F32),
                   jax.ShapeDtypeStruct(gv.shape, F32), jax.ShapeDtypeStruct(gv.shape, F32),
                   jax.ShapeDtypeStruct(cw8.shape, F32), jax.ShapeDtypeStruct((1, DIN), F32)],
        compiler_params=pltpu.CompilerParams(dimension_semantics=("arbitrary",),
                                             vmem_limit_bytes=_vmem_limit(est)),
    )(z, z, z, dy, dy, ws, bsb, gv, bv, cw8)


def _place():
    x, y, c = lax.axis_index("x"), lax.axis_index("y"), lax.axis_index("c")
    others = [(1 - x, y), (x, 1 - y), (1 - x, 1 - y)]
    return x, y, c, 2 * x + y, others, [2 * ox + oy for ox, oy in others]


def _half(c, rows_half):
    return pl.ds(pl.multiple_of(c * rows_half, 16), rows_half)


def _rcopy(src, dst, ssem, rsem, dev):
    return pltpu.make_async_remote_copy(src_ref=src, dst_ref=dst, send_sem=ssem, recv_sem=rsem,
                                        device_id=dev, device_id_type=MESH)


class _Comm:
    def __init__(self, aliased, ins, fresh, n_sems, stages, sink, relayed=False):
        self.aliased, self.ins, self.fresh = list(aliased), list(ins), list(fresh)
        self.n_sems, self.stages, self.sink, self.relayed = n_sems, stages, sink, relayed


def _in_hbm(a):
    return pltpu.with_memory_space_constraint(a, pltpu.HBM)


def _comm_operands(comms, n_in, n_out):
    arrays, shapes, scratch, aliases = [], [], [], {}
    for cm in comms:
        for a in cm.aliased:
            aliases[n_in + len(arrays)] = n_out + len(shapes)
            arrays.append(_in_hbm(a))
            shapes.append(pltpu.HBM(a.shape, a.dtype))
        arrays += [_in_hbm(a) for a in cm.ins]
        shapes += [pltpu.HBM(s.shape, s.dtype) for s in cm.fresh]
        scratch += [pltpu.SemaphoreType.DMA((cm.n_sems,))] * 2
    return arrays, shapes, scratch, aliases


def _comm_refs(comms, in_refs, out_refs, sem_refs):
    per, pi, po = [], 0, 0
    for i, cm in enumerate(comms):
        pi += len(cm.aliased)
        ins = in_refs[pi:pi + len(cm.ins)]
        pi += len(cm.ins)
        al = out_refs[po:po + len(cm.aliased)]
        po += len(cm.aliased)
        fr = out_refs[po:po + len(cm.fresh)]
        po += len(cm.fresh)
        per.append((al, ins, fr, sem_refs[2 * i], sem_refs[2 * i + 1]))
    return per


def _comm_first_step(comms, hosted, grid):
    if not comms:
        return
    at_first = functools.reduce(lambda a, b: a & b, [pl.program_id(d) == 0 for d in range(len(grid))])

    @pl.when(at_first)
    def _():
        for cm, refs in zip(comms, hosted):
            for d in cm.stages[0](*refs, False):
                d.start()


def _comm_mid_step(comms, hosted, grid):
    relayed = [(cm, refs) for cm, refs in zip(comms, hosted) if cm.relayed]
    if not relayed:
        return
    n_steps = math.prod(grid)
    assert n_steps >= 3, grid
    mid, at_mid = (2 * n_steps) // 3, None
    for d in reversed(range(len(grid))):
        here = pl.program_id(d) == mid % grid[d]
        at_mid = here if at_mid is None else at_mid & here
        mid //= grid[d]

    @pl.when(at_mid)
    def _():
        for cm, refs in relayed:
            for d in cm.stages[0](*refs, True):
                d.wait()
            for d in cm.stages[1](*refs, False):
                d.start()


def _comm_last_step(comms, hosted, grid):
    if not comms:
        return
    at_last = functools.reduce(lambda a, b: a & b, [pl.program_id(d) == g - 1 for d, g in enumerate(grid)])

    @pl.when(at_last)
    def _():
        for cm, refs in zip(comms, hosted):
            for d in cm.stages[1 if cm.relayed else 0](*refs, True):
                d.wait()
        for cm, refs in zip(comms, hosted):
            for stage in cm.stages[2 if cm.relayed else 1:]:
                for d in stage(*refs, False):
                    d.start()
                for d in stage(*refs, True):
                    d.wait()


def _deliver(comms, outs):
    pos = 0
    for cm in comms:
        al = outs[pos:pos + len(cm.aliased)]
        pos += len(cm.aliased)
        fr = outs[pos:pos + len(cm.fresh)]
        pos += len(cm.fresh)
        cm.sink(list(al), list(fr))


def _comm_call(name, comms):
    in_arrays, out_shapes, scratch, aliases = _comm_operands(comms, 0, 0)
    n_in, n_out = len(in_arrays), len(out_shapes)

    def body(*refs):
        hosted = _comm_refs(comms, refs[:n_in], refs[n_in:n_in + n_out], refs[n_in + n_out:])
        for cm, r in zip(comms, hosted):
            for stage in cm.stages:
                for d in stage(*r, False):
                    d.start()
                for d in stage(*r, True):
                    d.wait()

    outs = pl.pallas_call(body, name=name, in_specs=[ANY] * n_in, out_specs=[ANY] * n_out,
                          out_shape=out_shapes, scratch_shapes=scratch, input_output_aliases=aliases)(*in_arrays)
    _deliver(comms, outs)


def _c_gather(fulls, sink, ici, fwd, conv=None):
    n = len(fulls)
    n_direct = 2 * n + (3 if conv is not None else 0)
    n_ici = n_direct + n if ici else 0

    def stage_direct(al, ins, fr, ssem, rsem, wait):
        x, y, c, chip, others, ochips = _place()
        ds = []
        for m in range(n):
            ah = al[m].shape[1] // 2
            mine = al[m].at[chip, _half(c, ah)]
            for k in range(2):
                i = 2 * m + k
                ds.append(_rcopy(mine, al[m].at[ochips[k], _half(c, ah)], ssem.at[i], rsem.at[i], (x, y, c)) if wait
                          else _rcopy(mine, mine, ssem.at[i], rsem.at[i], (*others[k], c)))
        if conv is not None:
            mine = al[n].at[chip]
            for k in range(3):
                i = 2 * n + k
                ds.append(_rcopy(mine, al[n].at[ochips[k]], ssem.at[i], rsem.at[i], (x, y, c)) if wait
                          else _rcopy(mine, mine, ssem.at[i], rsem.at[i], (*others[k], c)))
        return ds

    def stage_relay(al, ins, fr, ssem, rsem, wait):
        x, y, c, chip, others, ochips = _place()
        got_from = ochips[1] + (ochips[0] - ochips[1]) * c
        send_to = (x + (1 - 2 * x) * (1 - c), y + (1 - 2 * y) * c, c)
        ds = []
        for m in range(n):
            ah = al[m].shape[1] // 2
            i = n_direct + m
            got = al[m].at[got_from, _half(c, ah)]
            ds.append(_rcopy(got, al[m].at[ochips[2], _half(c, ah)], ssem.at[i], rsem.at[i], (x, y, c)) if wait
                      else _rcopy(got, got, ssem.at[i], rsem.at[i], send_to))
        return ds

    def stage_fwd(al, ins, fr, ssem, rsem, wait):
        x, y, c, chip, others, ochips = _place()
        ds = []
        for m in range(n):
            ah = al[m].shape[1] // 2
            for k in range(3):
                i = n_ici + 3 * m + k
                got = al[m].at[ochips[k], _half(c, ah)]
                ds.append(_rcopy(got, al[m].at[ochips[k], _half(1 - c, ah)], ssem.at[i], rsem.at[i], (x, y, c)) if wait
                          else _rcopy(got, got, ssem.at[i], rsem.at[i], (x, y, 1 - c)))
        return ds

    stages = ([stage_direct, stage_relay] if ici else []) + ([stage_fwd] if fwd else [])
    return _Comm(list(fulls) + ([conv] if conv is not None else []), [], [], n_ici + (3 * n if fwd else 0),
                 stages, sink, relayed=ici)


def _c_rs_pair(parts, sink):
    def stage(al, ins, fr, ssem, rsem, wait):
        x, y, c, _, _, _ = _place()
        return [_rcopy(ins[m].at[:, _half(1 - c, ins[m].shape[1] // 2), :], fr[m], ssem.at[m], rsem.at[m],
                       (x, y, 1 - c)) for m in range(len(ins))]
    fresh = [jax.ShapeDtypeStruct((p.shape[0], p.shape[1] // 2, p.shape[2]), p.dtype) for p in parts]
    return _Comm([], parts, fresh, len(parts), [stage], sink)


def _c_rs_chips(psums, sink):
    def stage(al, ins, fr, ssem, rsem, wait):
        x, y, c, _, others, ochips = _place()
        return [_rcopy(ins[m].at[ochips[k]], fr[m].at[k], ssem.at[3 * m + k], rsem.at[3 * m + k], (*others[k], c))
                for m in range(len(ins)) for k in range(3)]
    fresh = [jax.ShapeDtypeStruct((3,) + p.shape[1:], p.dtype) for p in psums]
    return _Comm([], psums, fresh, 3 * len(psums), [stage], sink)


def _c_exchange(g2s, sink):
    def stage(al, ins, fr, ssem, rsem, wait):
        x, y, c, _, _, _ = _place()
        if wait:
            return [_rcopy(al[m].at[c], al[m].at[1 - c], ssem.at[m], rsem.at[m], (x, y, c)) for m in range(len(al))]
        return [_rcopy(al[m].at[c], al[m].at[c], ssem.at[m], rsem.at[m], (x, y, 1 - c)) for m in range(len(al))]
    return _Comm(g2s, [], [], len(g2s), [stage], sink)


def _c_allgather_small(buf, sink):
    def stage(al, ins, fr, ssem, rsem, wait):
        x, y, c, _, _, _ = _place()
        ds = []
        for r in range(1, 8):
            peer = (1 - x if r & 4 else x, 1 - y if r & 2 else y, 1 - c if r & 1 else c)
            ds.append(_rcopy(al[0].at[0], al[0].at[r], ssem.at[r - 1], rsem.at[r - 1], peer))
        return ds
    return _Comm([buf], [], [], 7, [stage], sink)


def _sum_small(buf, dev_idx):
    _, m, n = buf.shape

    def body(me_ref, b_ref, o_ref):
        me = me_ref[0]
        acc = b_ref[me]
        for d in range(1, 8):
            acc = acc + b_ref[jnp.bitwise_xor(me, d)]
        o_ref[...] = acc

    gs = pltpu.PrefetchScalarGridSpec(
        num_scalar_prefetch=1, grid=(1,),
        in_specs=[pl.BlockSpec(buf.shape, lambda i, me: (0, 0, 0))],
        out_specs=pl.BlockSpec((m, n), lambda i, me: (0, 0)))
    return pl.pallas_call(body, name="sum_small", grid_spec=gs,
                          out_shape=jax.ShapeDtypeStruct((m, n), F32))(dev_idx, buf)


def _sum_pair(name, part, sib, c_idx):
    _, A, B = part.shape
    ah = A // 2
    br = _rows(ah, B * 2)

    def body(c_ref, a_ref, b_ref, o_ref):
        o_ref[...] = (a_ref[...].astype(F32) + b_ref[...].astype(F32)).astype(BF16)

    gs = pltpu.PrefetchScalarGridSpec(
        num_scalar_prefetch=1, grid=(N_CHIPS, ah // br),
        in_specs=[pl.BlockSpec((None, None, br, B), lambda j, i, c: (j, c[0], i, 0)),
                  pl.BlockSpec((None, br, B), lambda j, i, c: (j, i, 0))],
        out_specs=pl.BlockSpec((None, br, B), lambda j, i, c: (j, i, 0)))
    return pl.pallas_call(body, name=name, grid_spec=gs,
                          out_shape=pltpu.HBM((N_CHIPS, ah, B), BF16),
                          compiler_params=pltpu.CompilerParams(vmem_limit_bytes=_vmem_limit(2 * 3 * br * B * 2)),
                          )(c_idx, _in_hbm(part.reshape(N_CHIPS, 2, ah, B)), _in_hbm(sib))


def _sum_chips(name, psum, recv, place_idx):
    _, ah, B = psum.shape
    br = _rows(ah, B * 4)

    def body(c_ref, p_ref, r0_ref, r1_ref, r2_ref, o_ref):
        o_ref[...] = ((p_ref[...].astype(F32) + r0_ref[...].astype(F32)) + r1_ref[...].astype(F32)) \
            + r2_ref[...].astype(F32)

    gs = pltpu.PrefetchScalarGridSpec(
        num_scalar_prefetch=1, grid=(ah // br,),
        in_specs=[pl.BlockSpec((None, br, B), lambda i, c: (c[0], i, 0))]
        + [pl.BlockSpec((None, br, B), functools.partial(lambda i, c, k: (k, i, 0), k=k)) for k in range(3)],
        out_specs=pl.BlockSpec((None, br, B), lambda i, c: (c[1], i, 0)))
    return pl.pallas_call(body, name=name, grid_spec=gs,
                          out_shape=pltpu.HBM((2, ah, B), F32),
                          compiler_params=pltpu.CompilerParams(vmem_limit_bytes=_vmem_limit(2 * br * B * (4 * 2 + 4))),
                          )(place_idx, _in_hbm(psum), *[_in_hbm(recv)] * 3)


def _cast_into(name, a, place_idx):
    A, B = a.shape
    br = _rows(A, B * 4)

    def body(c_ref, a_ref, o_ref):
        o_ref[...] = a_ref[...].astype(BF16)

    gs = pltpu.PrefetchScalarGridSpec(
        num_scalar_prefetch=1, grid=(A // br,),
        in_specs=[pl.BlockSpec((br, B), lambda i, c: (i, 0))],
        out_specs=pl.BlockSpec((None, br, B), lambda i, c: (c[0], i, 0)))
    return pl.pallas_call(body, name=name, grid_spec=gs,
                          out_shape=pltpu.HBM((N_CHIPS, A, B), BF16))(place_idx, a)


def _cast_behind(x, shards, place_idx, comms):
    n_steps = 16
    n = len(shards)
    xv = x
    svs = list(shards)

    def pace(v):
        for steps_per_block in (1, 2, 4, 8, 16):
            blocks = n_steps // steps_per_block
            if v.shape[0] % blocks == 0 and (v.shape[0] // blocks) % 16 == 0:
                return steps_per_block
        raise ValueError(v.shape)

    n_in, n_out = 2 + n, 1 + n
    cm_in_arrays, cm_out_shapes, cm_scratch, aliases = _comm_operands(comms, n_in, n_out)
    n_cm_in, n_cm_out = len(cm_in_arrays), len(cm_out_shapes)
    grid = (n_steps,)

    def body(c_ref, x_ref, *rest):
        s_refs = rest[:n]
        xo_ref = rest[n + n_cm_in]
        so_refs = rest[n + n_cm_in + 1:n + n_cm_in + 1 + n]
        o_end = n + n_cm_in + 1 + n
        hosted = _comm_refs(comms, rest[n:n + n_cm_in], rest[o_end:o_end + n_cm_out], rest[o_end + n_cm_out:])
        _comm_first_step(comms, hosted, grid)
        _comm_mid_step(comms, hosted, grid)
        xo_ref[...] = x_ref[...].astype(BF16)
        for s_ref, o_ref in zip(s_refs, so_refs):
            o_ref[...] = s_ref[...].astype(BF16)
        _comm_last_step(comms, hosted, grid)

    def rows(v):
        return v.shape[0] * pace(v) // n_steps

    def in_spec(v):
        return pl.BlockSpec((rows(v), v.shape[1]), functools.partial(lambda i, c, q: (i // q, 0), q=pace(v)))

    def shard_out_spec(v):
        return pl.BlockSpec((None, rows(v), v.shape[1]),
                            functools.partial(lambda i, c, q: (c[0], i // q, 0), q=pace(v)))

    gs = pltpu.PrefetchScalarGridSpec(
        num_scalar_prefetch=1, grid=grid,
        in_specs=[in_spec(v) for v in [xv] + svs] + [ANY] * n_cm_in,
        out_specs=[in_spec(xv)] + [shard_out_spec(v) for v in svs] + [ANY] * n_cm_out,
        scratch_shapes=cm_scratch)
    outs = pl.pallas_call(
        body, name="cast_behind", grid_spec=gs,
        out_shape=[jax.ShapeDtypeStruct(xv.shape, BF16)]
        + [pltpu.HBM((N_CHIPS,) + v.shape, BF16) for v in svs] + cm_out_shapes,
        input_output_aliases=aliases,
        compiler_params=pltpu.CompilerParams(
            dimension_semantics=("arbitrary",),
            vmem_limit_bytes=_vmem_limit(2 * sum(rows(v) * v.shape[1] * 6 for v in [xv] + svs))),
    )(place_idx, xv, *svs, *cm_in_arrays)
    _deliver(comms, outs[n_out:])
    return outs[0].reshape(x.shape), [o.reshape((N_CHIPS,) + s.shape) for o, s in zip(outs[1:n_out], shards)]


def _adamw_small(gsum, gconv, offsets, ws, ms, vs):
    n = len(ws)

    def body(*refs):
        g_ref, gc_ref = refs[0], refs[1]
        w_refs = refs[2:2 + n]
        m_refs = refs[2 + n:2 + 2 * n]
        v_refs = refs[2 + 2 * n:2 + 3 * n]
        outs = refs[2 + 3 * n:]
        for i in range(n):
            rows = w_refs[i].shape[0]
            g = gc_ref[...] if offsets[i] is None else g_ref[offsets[i]:offsets[i] + rows, :]
            d, m2, v2 = _adamw_math(w_refs[i][...], g, m_refs[i][...], v_refs[i][...])
            outs[4 * i][...] = g
            outs[4 * i + 1][...] = d
            outs[4 * i + 2][...] = m2
            outs[4 * i + 3][...] = v2

    vm = pl.BlockSpec(memory_space=pltpu.VMEM)
    out_shape = []
    for w in ws:
        out_shape += [jax.ShapeDtypeStruct(w.shape, F32)] * 4
    return pl.pallas_call(body, name="adamw_small", in_specs=[vm] * (2 + 3 * n), out_specs=[vm] * (4 * n),
                          out_shape=out_shape)(gsum, gconv, *ws, *ms, *vs)


def _ident(accs, tes, res):
    return [accs[0]]


def _ffn_up(name, hb, wg4, wu4, comms=()):
    S, D = hb.shape

    def epi(accs, tes, res):
        g, u = accs
        return [g, u, g * _sigmoid(g) * u]

    return _mm(name, 'nn', [hb], [wg4, wu4], [(0, 0, 0), (0, 1, 1)], 2, epi, [BF16] * 3,
               bm=_blk(S, 512), bn=wg4.shape[2], bk=D, b_blocked=True, n_outer=True, comms=comms)


def _ffn_gate(name, hb, wg4, comms=()):
    S, D = hb.shape
    return _mm(name, 'nn', [hb], [wg4], [(0, 0, 0)], 1, _ident, [BF16],
               bm=_blk(S, 1024), bn=wg4.shape[2], bk=D, b_blocked=True, n_outer=True, comms=comms)[0]


def _ffn_upmul(name, hb, wu4, gb, comms=()):
    S, D = hb.shape

    def epi(accs, tes, res):
        g = tes[0].astype(F32)
        return [accs[0], g * _sigmoid(g) * accs[0]]

    return _mm(name, 'nn', [hb], [wu4], [(0, 0, 0)], 1, epi, [BF16, BF16],
               bm=_blk(S, 1024), bn=wu4.shape[2], bk=D, tile_extras=[gb], b_blocked=True, n_outer=True, comms=comms)


def _ffn_down(name, ab, wd, hres, comms=()):
    S, F = ab.shape
    D = wd.shape[1]
    return _mm(name, 'nn', [ab], [wd], [(0, 0, 0)], 1, lambda accs, tes, res: [ALPHA * tes[0] + 0.5 * accs[0]],
               [F32], bm=_blk(S, 1024), bn=_blk(D, 512), bk=F, tile_extras=[hres], comms=comms)[0]


def _ffn_da(name, dsub, wd, gb, ub, comms=()):
    S, D = dsub.shape
    F = wd.shape[0]

    def epi(accs, tes, res):
        da = accs[0]
        g = tes[0].astype(F32)
        u = tes[1].astype(F32)
        s = _sigmoid(g)
        silu = g * s
        return [(da * u) * (s + silu * (1.0 - s)), da * silu]

    return _mm(name, 'nt', [dsub], [wd], [(0, 0, 0)], 1, epi, [BF16, BF16],
               bm=_blk(S, 1024), bn=_blk(F, 512), bk=D, tile_extras=[gb, ub], comms=comms)


def _dw_rows(name, ab, dsub, cs, comms=()):
    S, D = dsub.shape
    return _mm(name, 'tn', [ab], [dsub], [(0, 0, 0)], 1, _ident, [BF16],
               bm=cs, bn=_blk(D, 1024), bk=_blk(S, 2048), comms=comms)[0].reshape(N_CHIPS, cs, D)


def _dw_cols(name, hb, dxb, cs, comms=()):
    S, D = hb.shape
    return _mm(name, 'tn', [hb], [dxb], [(0, 0, 0)], 1, _ident, [BF16],
               bm=_blk(D, 1024), bn=cs, bk=_blk(S, 2048), out_blocked=True, comms=comms)[0]


def _ffn_dh(name, dgb, dub, wg4, wu4, dres, comms=()):
    S = dgb.shape[0]
    D = wg4.shape[1]
    return _mm(name, 'nt', [dgb, dub], [wg4, wu4], [(0, 0, 0), (1, 1, 0)], 1,
               lambda accs, tes, res: [accs[0] + tes[0]], [F32],
               bm=_blk(S, 1024), bn=_blk(D, 1024), bk=wg4.shape[2], tile_extras=[dres], b_blocked=True,
               comms=comms)[0]


def _step(p):
    x = p['x'][0]
    target = p['loss_target'][0]
    S, D = x.shape
    cx, cy, cc = lax.axis_index("x"), lax.axis_index("y"), lax.axis_index("c")
    chip = 2 * cx + cy
    c_idx = jnp.reshape(cc, (1,)).astype(jnp.int32)
    place_idx = jnp.stack([chip, cc]).astype(jnp.int32)

    w_s = p['w_s'][0]
    H = w_s.shape[0]
    DA = H * HEAD
    bsb = jnp.broadcast_to(p['b_s'][0][:, :, None], (H, HEAD, HEAD))
    conv_shard = p['conv_w'][0]
    conv8 = jnp.zeros((8, conv_shard.shape[1]), F32).at[:CONV_TAPS].set(conv_shard)

    W = {'ffa_gate': _cast_into("cast_ffa_gate", p['ffa_gate'][0], place_idx)}
    conv = {'full': lax.dynamic_update_slice(jnp.zeros((N_CHIPS,) + conv8.shape, F32), conv8[None], (chip, 0, 0))}

    def gathered(names, with_conv=False):
        def sink(al, fr):
            W.update(zip(names, al))
            if with_conv:
                conv['full'] = al[len(names)]
        return sink

    def c_gather(names, ici, fwd, with_conv=False):
        return _c_gather([W[n] for n in names], gathered(names, with_conv), ici, fwd,
                         conv['full'] if with_conv else None)

    rest = [n for n in BIG if n != 'ffa_gate']
    xb, casted = _cast_behind(x, [p[n][0] for n in rest], place_idx, comms=[c_gather(['ffa_gate'], True, True)])
    W.update(zip(rest, casted))
    cs_ff = W['ffa_gate'].shape[2]
    F = N_CHIPS * cs_ff

    ga = _ffn_gate("ffa_gate", xb, W['ffa_gate'],
                   comms=[c_gather(['ffa_up'], True, True), c_gather(['ffa_down'], True, False)])
    ua, aa = _ffn_upmul("ffa_up", xb, W['ffa_up'], ga,
                        comms=[c_gather(['ffa_down'], False, True),
                               c_gather(['w_in', 'ffc_gate'], True, False, with_conv=True)])
    cw8 = jnp.transpose(conv['full'], (1, 0, 2)).reshape(8, DA)
    wd_a = W['ffa_down'].reshape(F, D)
    r1 = _ffn_down("ffa_down", aa, wd_a, x,
                   comms=[c_gather(['w_in', 'ffc_gate'], False, True), c_gather(['w_out', 'ffc_down'], True, False)])
    h1, h1b = _ln_fwd("ln_a", r1, p['ln_a_g'], p['ln_a_b'])
    w_in4 = W['w_in']
    cs_in = w_in4.shape[2]
    z = _mm("w_in", 'nn', [h1b], [w_in4], [(0, 0, 0)], 1, lambda accs, tes, res: [accs[0] + res[0]], [F32],
            bm=_blk(S, 1024), bn=cs_in, bk=D, row_extras=[p['b_in']], b_blocked=True, n_outer=True,
            comms=[c_gather(['w_out', 'ffc_down'], False, True), c_gather(['ffc_up'], True, False)])[0]
    w_out = W['w_out'].reshape(DA * 2, D)
    yb = _mix_fwd(z, w_s, bsb, p['ln_v_g'], p['ln_v_b'], cw8, _blk(S, 2 * HEAD))
    r2 = _mm("w_out", 'nn', [yb], [w_out], [(0, 0, 0)], 1,
             lambda accs, tes, res: [accs[0] + res[0] + ALPHA * tes[0]], [F32],
             bm=_blk(S, 1024), bn=_blk(D, 1024), bk=2 * DA, tile_extras=[h1], row_extras=[p['b_out']],
             comms=[c_gather(['ffc_up'], False, True)])[0]
    h2, h2b = _ln_fwd("ln_m", r2, p['ln_m_g'], p['ln_m_b'])
    gc, uc, ac = _ffn_up("ffc_up", h2b, W['ffc_gate'], W['ffc_up'])
    wd_c = W['ffc_down'].reshape(F, D)
    r3 = _ffn_down("ffc_down", ac, wd_c, h2)

    pg, sg, sib, psum, recv, g2 = {}, {}, {}, {}, {}, {}

    def c_pair(names):
        return _c_rs_pair([pg[n] for n in names], lambda al, fr: sib.update(zip(names, fr)))

    def c_chips(names):
        return _c_rs_chips([psum[n] for n in names], lambda al, fr: recv.update(zip(names, fr)))

    def c_swap(names):
        return _c_exchange([g2[n] for n in names], lambda al, fr: g2.update(zip(names, al)))

    def sum_pair(*names):
        for n in names:
            psum[n] = _sum_pair(f"sum_pair_{n}", pg[n], sib[n], c_idx)

    def sum_chips(*names):
        for n in names:
            g2[n] = _sum_chips(f"sum_chips_{n}", psum[n], recv[n], place_idx)

    dres3, dsub3, sg['ln_c_g'], sg['ln_c_b'], lsum = _ln_loss_bwd(
        "ln_c_loss_bwd", r3, target, p['ln_c_g'], p['ln_c_b'], 0.5)

    dgc, duc = _ffn_da("ffc_da", dsub3, wd_c, gc, uc)
    pg['ffc_down'] = _dw_rows("ffc_dwd", ac, dsub3, cs_ff)
    pg['ffc_gate'] = _dw_cols("ffc_dwg", h2b, dgc, cs_ff)
    pg['ffc_up'] = _dw_cols("ffc_dwu", h2b, duc, cs_ff, comms=[c_pair(['ffc_down', 'ffc_gate'])])
    sum_pair('ffc_down', 'ffc_gate')
    dh2 = _ffn_dh("ffc_dh", dgc, duc, W['ffc_gate'], W['ffc_up'], dres3,
                  comms=[c_pair(['ffc_up']), c_chips(['ffc_down']), c_chips(['ffc_gate'])])
    sum_pair('ffc_up')
    sum_chips('ffc_down', 'ffc_gate')

    dres2, dmix, sg['ln_m_g'], sg['ln_m_b'], sg['b_out'] = _ln_bwd("ln_m_bwd", r2, dh2, p['ln_m_g'], 1.0)
    pg['w_out'] = _mm("dw_out", 'tn', [yb], [dmix], [(0, 0, 0)], 1, _ident, [BF16],
                      bm=_blk(2 * DA, 1024), bn=_blk(D, 1024),
                      bk=_blk(S, 2048))[0].reshape(N_CHIPS, 2 * DA // 4, D)
    dy = _mm("dy", 'nt', [dmix], [w_out], [(0, 0, 0)], 1, _ident, [F32],
             bm=_blk(S, 1024), bn=_blk(2 * DA, 1024), bk=D)[0]
    dz, dws, dbs, sg['ln_v_g'], sg['ln_v_b'], dcw, sg['b_in'] = _mix_bwd(
        z, dy, w_s, bsb, p['ln_v_g'], p['ln_v_b'], cw8, HEAD)
    sg['w_s'] = dws
    sg['b_s'] = dbs[:, :, 0]
    pg['w_in'] = _dw_cols("dw_in", h1b, dz, cs_in,
                          comms=[c_chips(['ffc_up']), c_pair(['w_out']), c_swap(['ffc_down', 'ffc_gate'])])
    sum_chips('ffc_up')
    sum_pair('w_out')
    dh1 = _mm("dh1", 'nt', [dz], [w_in4], [(0, 0, 0)], 1, lambda accs, tes, res: [accs[0] + tes[0]], [F32],
              bm=_blk(S, 1024), bn=_blk(D, 1024), bk=cs_in, tile_extras=[dres2], b_blocked=True,
              comms=[c_pair(['w_in']), c_chips(['w_out']), c_swap(['ffc_up'])])[0]
    sum_pair('w_in')
    sum_chips('w_out')

    dres1, dsub1, sg['ln_a_g'], sg['ln_a_b'], _ = _ln_bwd("ln_a_bwd", r1, dh1, p['ln_a_g'], 0.5)
    def as_rows(a):
        a = a.reshape(-1, 128)
        pad = (-a.shape[0]) % 8
        return jnp.pad(a, ((0, pad), (0, 0))) if pad else a

    pieces, offsets, off = [], {}, 0
    for n in SMALL + ['conv_w']:
        piece = as_rows(dcw[:CONV_TAPS] if n == 'conv_w' else sg[n])
        offsets[n] = off
        off += piece.shape[0]
        pieces.append(piece)
    packed = jnp.concatenate(pieces, axis=0)
    small = {'buf': jnp.zeros((8,) + packed.shape, F32).at[0].set(packed)}
    dga, dua = _ffn_da("ffa_da", dsub1, wd_a, ga, ua,
                       comms=[c_chips(['w_in']), c_swap(['w_out']),
                              _c_allgather_small(small['buf'], lambda al, fr: small.update(buf=al[0]))])
    sum_chips('w_in')
    gsum = _sum_small(small['buf'], jnp.reshape(4 * cx + 2 * cy + cc, (1,)).astype(jnp.int32))
    pg['ffa_down'] = _dw_rows("ffa_dwd", aa, dsub1, cs_ff)
    pg['ffa_gate'] = _dw_cols("ffa_dwg", xb, dga, cs_ff)
    pg['ffa_up'] = _dw_cols("ffa_dwu", xb, dua, cs_ff, comms=[c_pair(['ffa_down', 'ffa_gate']), c_swap(['w_in'])])
    sum_pair('ffa_down', 'ffa_gate')
    _comm_call("pair_ffa_up", [c_pair(['ffa_up'])])
    sum_pair('ffa_up')
    dx = _ffn_dh("ffa_dh", dga, dua, W['ffa_gate'], W['ffa_up'], dres1,
                 comms=[c_chips(['ffa_down']), c_chips(['ffa_gate']), c_chips(['ffa_up'])])
    sum_chips('ffa_down', 'ffa_gate', 'ffa_up')
    _comm_call("swap_ffa", [c_swap(['ffa_down', 'ffa_gate', 'ffa_up'])])

    out = {}
    for n in BIG:
        g, d, m2, v2 = _adamw(f"adamw_{n}", p[n][0], g2[n].reshape(p[n][0].shape), p['m_' + n][0], p['v_' + n][0])
        shp = p[n].shape
        out[n] = (g.reshape(shp), d.reshape(shp), m2.reshape(shp), v2.reshape(shp))

    loss = lax.psum(0.5 * jnp.sum(lsum) / D, ("x", "y", "c"))
    n_conv = CONV_TAPS * DA // 128
    conv_sum = gsum[offsets['conv_w']:offsets['conv_w'] + n_conv].reshape(CONV_TAPS, DA)
    cw = conv_shard.shape[1]
    gconv = lax.dynamic_slice_in_dim(conv_sum, (2 * cx + cy) * cw, cw, axis=1).reshape(-1, 128)
    names = SMALL + ['conv_w']
    rows128 = lambda a: a.reshape(-1, 128)
    res = _adamw_small(gsum, gconv, [offsets[n] if n != 'conv_w' else None for n in names],
                       [rows128(p[n]) for n in names], [rows128(p['m_' + n]) for n in names],
                       [rows128(p['v_' + n]) for n in names])
    for i, n in enumerate(names):
        out[n] = tuple(r.reshape(p[n].shape) for r in res[4 * i:4 * i + 4])

    return (loss, dx.reshape(p['x'].shape), *[out[n][0] for n in WEIGHTS], *[out[n][1] for n in WEIGHTS],
            *[out[n][2] for n in WEIGHTS], *[out[n][3] for n in WEIGHTS])


def kernel(x, ffa_gate, ffa_up, ffa_down, ln_a_g, ln_a_b, w_in, b_in, w_s, b_s, ln_v_g, ln_v_b, conv_w, w_out, b_out, ln_m_g, ln_m_b, ffc_gate, ffc_up, ffc_down, ln_c_g, ln_c_b, loss_target, m_ffa_gate, m_ffa_up, m_ffa_down, m_ln_a_g, m_ln_a_b, m_w_in, m_b_in, m_w_s, m_b_s, m_ln_v_g, m_ln_v_b, m_conv_w, m_w_out, m_b_out, m_ln_m_g, m_ln_m_b, m_ffc_gate, m_ffc_up, m_ffc_down, m_ln_c_g, m_ln_c_b, v_ffa_gate, v_ffa_up, v_ffa_down, v_ln_a_g, v_ln_a_b, v_w_in, v_b_in, v_w_s, v_b_s, v_ln_v_g, v_ln_v_b, v_conv_w, v_w_out, v_b_out, v_ln_m_g, v_ln_m_b, v_ffc_gate, v_ffc_up, v_ffc_down, v_ln_c_g, v_ln_c_b):
    return _step(dict(locals()))
```

```python
import functools
import math

import jax
import jax.numpy as jnp
from jax import lax
from jax.experimental import pallas as pl
from jax.experimental.pallas import tpu as pltpu

F32 = jnp.float32
BF16 = jnp.bfloat16
MESH = pl.DeviceIdType.MESH
ANY = pl.BlockSpec(memory_space=pl.ANY)

HEAD = 128
CONV_TAPS = 3
LN_EPS = 1e-5
ALPHA = float(2.0 ** 0.25)
ADAM_LR, ADAM_B1, ADAM_B2, ADAM_EPS, ADAM_WD, ADAM_STEP = 0.001, 0.9, 0.999, 1e-08, 0.01, 10

V7X_VMEM_BYTES = 64 * 2 ** 20
VMEM_COMPILER_RESERVE = 6 * 2 ** 20
VMEM_MIN_REQUEST = 32 * 2 ** 20
VMEM_ESTIMATE_SLACK = 1.3
N_CHIPS = 4

BIG = ['ffa_gate', 'ffa_up', 'ffa_down', 'w_in', 'w_out', 'ffc_gate', 'ffc_up', 'ffc_down']
WEIGHTS = ['ffa_gate', 'ffa_up', 'ffa_down', 'ln_a_g', 'ln_a_b', 'w_in', 'b_in', 'w_s', 'b_s', 'ln_v_g', 'ln_v_b',
           'conv_w', 'w_out', 'b_out', 'ln_m_g', 'ln_m_b', 'ffc_gate', 'ffc_up', 'ffc_down', 'ln_c_g', 'ln_c_b']
SMALL = [n for n in WEIGHTS if n not in BIG and n != 'conv_w']


def _vmem_limit(estimate_bytes):
    return int(min(max(estimate_bytes * VMEM_ESTIMATE_SLACK, VMEM_MIN_REQUEST),
                   V7X_VMEM_BYTES - VMEM_COMPILER_RESERVE))


def _blk(dim, pref, mult=128):
    if dim <= pref:
        return dim
    best = None
    for d in range(mult, pref + 1, mult):
        if dim % d == 0:
            best = d
    assert best is not None, (dim, pref)
    return best


def _rows(n_rows, row_bytes, target=4 * 2 ** 20):
    return _blk(n_rows, max(16, target // row_bytes), 16)


def _sigmoid(x):
    return 0.5 * jnp.tanh(0.5 * x) + 0.5


_GELU_C = math.sqrt(2.0 / math.pi)
_GELU_A = 0.044715


def _gelu(x):
    return (0.5 * x) * (1.0 + jnp.tanh(x * (_GELU_C + (_GELU_C * _GELU_A) * (x * x))))


def _gelu_and_grad(x):
    x2 = x * x
    t = jnp.tanh(x * (_GELU_C + (_GELU_C * _GELU_A) * x2))
    half_x = 0.5 * x
    one_plus_t = 1.0 + t
    dg = 0.5 * one_plus_t + half_x * (1.0 - t * t) * (_GELU_C + (3.0 * _GELU_C * _GELU_A) * x2)
    return half_x * one_plus_t, dg


def _ln_stats(r):
    mu = jnp.mean(r, axis=-1, keepdims=True)
    d = r - mu
    var = jnp.mean(d * d, axis=-1, keepdims=True)
    rstd = lax.rsqrt(var + LN_EPS)
    return d * rstd, rstd


_DIMS = {'nn': (((1,), (0,)), ((), ())), 'nt': (((1,), (1,)), ((), ())), 'tn': (((0,), (0,)), ((), ()))}


def _mm(name, mode, a_list, b_list, pairs, n_acc, epilogue, out_dtypes, bm, bn, bk,
        tile_extras=(), row_extras=(), b_blocked=False, out_blocked=False, n_outer=False, comms=()):
    a0, b0 = a_list[0], b_list[0]
    if mode == 'nn':
        M, K = a0.shape
        N = b0.shape[0] * b0.shape[2] if b_blocked else b0.shape[1]
    elif mode == 'nt':
        M, K = a0.shape
        N = b0.shape[1] if b_blocked else b0.shape[0]
    else:
        K, M = a0.shape
        N = b0.shape[1]
        assert not b_blocked
    assert M % bm == 0 and N % bn == 0 and K % bk == 0, (name, M, N, K, bm, bn, bk)
    gm, gn, gk = M // bm, N // bn, K // bk
    if b_blocked:
        assert (bn if mode == 'nn' else bk) == b0.shape[2], name
    dims = _DIMS[mode]

    def spec(shape, f):
        if n_outer:
            return pl.BlockSpec(shape, lambda g0, g1, g2: f(g1, g0, g2))
        return pl.BlockSpec(shape, f)

    if mode == 'tn':
        a_spec = spec((bk, bm), lambda i, j, k: (k, i))
    else:
        a_spec = spec((bm, bk), lambda i, j, k: (i, k))
    if mode == 'nn':
        b_spec = (spec((None, bk, bn), lambda i, j, k: (j, k, 0)) if b_blocked
                  else spec((bk, bn), lambda i, j, k: (k, j)))
    elif mode == 'nt':
        b_spec = (spec((None, bn, bk), lambda i, j, k: (k, j, 0)) if b_blocked
                  else spec((bn, bk), lambda i, j, k: (j, k)))
    else:
        b_spec = spec((bk, bn), lambda i, j, k: (k, j))
    te_spec = spec((bm, bn), lambda i, j, k: (i, j))
    re_spec = spec((1, bn), lambda i, j, k: (0, j))
    if out_blocked:
        o_spec = spec((None, bm, bn), lambda i, j, k: (j, i, 0))
        o_shape = (gn, M, bn)
    else:
        o_spec = te_spec
        o_shape = (M, N)

    na, nb, nte, nre, no = len(a_list), len(b_list), len(tile_extras), len(row_extras), len(out_dtypes)

    n_scratch_acc = n_acc if gk > 1 else 0
    grid = (gn, gm, gk) if n_outer else (gm, gn, gk)
    cm_in_arrays, cm_out_shapes, cm_scratch, aliases = _comm_operands(comms, na + nb + nte + nre, no)
    n_cm_in, n_cm_out = len(cm_in_arrays), len(cm_out_shapes)

    def body(*refs):
        a_refs = refs[:na]
        b_refs = refs[na:na + nb]
        te_refs = refs[na + nb:na + nb + nte]
        re_refs = refs[na + nb + nte:na + nb + nte + nre]
        n_in = na + nb + nte + nre
        o_refs = refs[n_in + n_cm_in:n_in + n_cm_in + no]
        acc_refs = refs[n_in + n_cm_in + no + n_cm_out:n_in + n_cm_in + no + n_cm_out + n_scratch_acc]
        hosted = _comm_refs(comms, refs[n_in:n_in + n_cm_in], refs[n_in + n_cm_in + no:n_in + n_cm_in + no + n_cm_out],
                            refs[n_in + n_cm_in + no + n_cm_out + n_scratch_acc:])
        _comm_first_step(comms, hosted, grid)
        _comm_mid_step(comms, hosted, grid)

        def finish(accs):
            outs = epilogue(accs, [r[...] for r in te_refs], [r[...] for r in re_refs])
            for o_ref, o in zip(o_refs, outs):
                o_ref[...] = o.astype(o_ref.dtype)

        def products():
            parts = [None] * n_acc
            for ai, bi, ci in pairs:
                d = lax.dot_general(a_refs[ai][...], b_refs[bi][...], dims, preferred_element_type=F32)
                parts[ci] = d if parts[ci] is None else parts[ci] + d
            return parts

        if gk == 1:
            finish(products())
        else:
            kk = pl.program_id(2)

            @pl.when(kk == 0)
            def _():
                for r, part in zip(acc_refs, products()):
                    r[...] = part

            if gk > 2:
                @pl.when((kk > 0) & (kk < gk - 1))
                def _():
                    for r, part in zip(acc_refs, products()):
                        r[...] += part

            @pl.when(kk == gk - 1)
            def _():
                finish([r[...] + part for r, part in zip(acc_refs, products())])

        _comm_last_step(comms, hosted, grid)

    isz = lambda a: jnp.dtype(a.dtype).itemsize
    est = 2 * (sum(bm * bk * isz(a) for a in a_list) + sum(bk * bn * isz(b) for b in b_list)
               + sum(bm * bn * isz(t) for t in tile_extras)
               + sum(bm * bn * jnp.dtype(d).itemsize for d in out_dtypes))
    est += (2 * n_acc + 2) * bm * bn * 4
    outs = pl.pallas_call(
        body, name=name, grid=grid,
        in_specs=[a_spec] * na + [b_spec] * nb + [te_spec] * nte + [re_spec] * nre + [ANY] * len(cm_in_arrays),
        out_specs=[o_spec] * no + [ANY] * len(cm_out_shapes),
        out_shape=[jax.ShapeDtypeStruct(o_shape, d) for d in out_dtypes] + cm_out_shapes,
        scratch_shapes=[pltpu.VMEM((bm, bn), F32)] * n_scratch_acc + cm_scratch,
        input_output_aliases=aliases,
        compiler_params=pltpu.CompilerParams(dimension_semantics=("parallel", "parallel", "arbitrary"),
                                             vmem_limit_bytes=_vmem_limit(est)),
    )(*a_list, *b_list, *tile_extras, *row_extras, *cm_in_arrays)
    _deliver(comms, outs[no:])
    return outs[:no]


def _rowwise(name, fn, br, row_ins, vec_ins, row_outs, acc_outs=()):
    S = row_ins[0].shape[0]
    assert S % br == 0, (name, S, br)
    nr, nv, no, nacc = len(row_ins), len(vec_ins), len(row_outs), len(acc_outs)

    def body(*refs):
        rin = refs[:nr]
        vin = refs[nr:nr + nv]
        rout = refs[nr + nv:nr + nv + no]
        aout = refs[nr + nv + no:]
        if nacc:
            @pl.when(pl.program_id(0) == 0)
            def _():
                for a in aout:
                    a[...] = jnp.zeros_like(a)
        fn(rin, vin, rout, aout)

    def full_spec(shape):
        nd = len(shape)
        return pl.BlockSpec(shape, lambda i: (0,) * nd)

    est = 2 * (sum(br * a.shape[1] * jnp.dtype(a.dtype).itemsize for a in row_ins)
               + sum(br * c * jnp.dtype(d).itemsize for c, d in row_outs))
    est += 6 * br * max(a.shape[1] for a in row_ins) * 4
    return pl.pallas_call(
        body, name=name, grid=(S // br,),
        in_specs=[pl.BlockSpec((br, a.shape[1]), lambda i: (i, 0)) for a in row_ins]
        + [full_spec(v.shape) for v in vec_ins],
        out_specs=[pl.BlockSpec((br, c), lambda i: (i, 0)) for c, _ in row_outs]
        + [full_spec(s) for s, _ in acc_outs],
        out_shape=[jax.ShapeDtypeStruct((S, c), d) for c, d in row_outs]
        + [jax.ShapeDtypeStruct(s, d) for s, d in acc_outs],
        compiler_params=pltpu.CompilerParams(dimension_semantics=("arbitrary",),
                                             vmem_limit_bytes=_vmem_limit(est)),
    )(*row_ins, *vec_ins)


def _ln_fwd(name, r, g, b):
    def fn(rin, vin, rout, aout):
        xhat, _ = _ln_stats(rin[0][...])
        h = xhat * vin[0][...] + vin[1][...]
        rout[0][...] = h
        rout[1][...] = h.astype(BF16)
    D = r.shape[1]
    return _rowwise(name, fn, _rows(r.shape[0], D * 4), [r], [g, b], [(D, F32), (D, BF16)])


def _ln_loss_bwd(name, r, target, g, b, sub_scale):
    D = r.shape[1]

    def fn(rin, vin, rout, aout):
        xhat, rstd = _ln_stats(rin[0][...])
        gain = vin[0][...]
        err = xhat * gain + vin[1][...] - rin[1][...]
        dhv = err * (1.0 / D)
        dxh = dhv * gain
        m1 = jnp.mean(dxh, axis=-1, keepdims=True)
        m2 = jnp.mean(dxh * xhat, axis=-1, keepdims=True)
        dr = rstd * (dxh - m1 - xhat * m2)
        rout[0][...] = ALPHA * dr
        rout[1][...] = (sub_scale * dr).astype(BF16)
        aout[0][...] += jnp.sum(dhv * xhat, axis=0, keepdims=True)
        aout[1][...] += jnp.sum(dhv, axis=0, keepdims=True)
        aout[2][...] += jnp.sum(err * err, axis=0, keepdims=True)
    return _rowwise(name, fn, _rows(r.shape[0], D * 4, 2 * 2 ** 20), [r, target], [g, b],
                    [(D, F32), (D, BF16)], [((1, D), F32)] * 3)


def _ln_bwd(name, r, dh, g, sub_scale):
    D = r.shape[1]

    def fn(rin, vin, rout, aout):
        xhat, rstd = _ln_stats(rin[0][...])
        dhv = rin[1][...]
        dxh = dhv * vin[0][...]
        m1 = jnp.mean(dxh, axis=-1, keepdims=True)
        m2 = jnp.mean(dxh * xhat, axis=-1, keepdims=True)
        dr = rstd * (dxh - m1 - xhat * m2)
        rout[0][...] = ALPHA * dr
        rout[1][...] = (sub_scale * dr).astype(BF16)
        aout[0][...] += jnp.sum(dhv * xhat, axis=0, keepdims=True)
        aout[1][...] += jnp.sum(dhv, axis=0, keepdims=True)
        aout[2][...] += jnp.sum(dr, axis=0, keepdims=True)
    return _rowwise(name, fn, _rows(r.shape[0], D * 4, 2 * 2 ** 20), [r, dh], [g],
                    [(D, F32), (D, BF16)], [((1, D), F32)] * 3)


def _adamw_math(w, g, m, v):
    m2 = ADAM_B1 * m + (1.0 - ADAM_B1) * g
    v2 = ADAM_B2 * v + (1.0 - ADAM_B2) * (g * g)
    m_hat = m2 / (1.0 - ADAM_B1 ** ADAM_STEP)
    v_hat = v2 / (1.0 - ADAM_B2 ** ADAM_STEP)
    delta = -ADAM_LR * (m_hat / (jnp.sqrt(v_hat) + ADAM_EPS) + ADAM_WD * w)
    return delta, m2, v2


def _adamw(name, w, g, m, v):
    def fn(rin, vin, rout, aout):
        gv = rin[1][...]
        d, m2, v2 = _adamw_math(rin[0][...], gv, rin[2][...], rin[3][...])
        rout[0][...] = gv
        rout[1][...] = d
        rout[2][...] = m2
        rout[3][...] = v2
    C = w.shape[1]
    return _rowwise(name, fn, _rows(w.shape[0], C * 4, 2 * 2 ** 20), [w, g, m, v], [], [(C, F32)] * 4)


def _conv_shifted(hc, p1, p2, rowi):
    r1 = jnp.where(rowi == 0, p1, pltpu.roll(hc, 1, 0))
    r2 = jnp.where(rowi == 0, p2, jnp.where(rowi == 1, p1, pltpu.roll(hc, 2, 0)))
    return r1, r2


def _mix_fwd(z, ws, bsb, gv, bv, cw8, tm):
    S, DIN = z.shape
    DA = gv.shape[1]
    DB = DA
    H = DA // HEAD
    o = 2 * DA
    nch = tm // HEAD

    def body(z_ref, zp_ref, ws_ref, bsb_ref, gv_ref, bv_ref, cw_ref, y_ref):
        i = pl.program_id(0)
        ri = lax.broadcasted_iota(jnp.int32, (HEAD, HEAD), 0)
        ci = lax.broadcasted_iota(jnp.int32, (HEAD, HEAD), 1)
        tril = ri >= ci
        for h in range(H):
            cs = slice(h * HEAD, (h + 1) * HEAD)
            vs = slice(DA + h * HEAD, DA + (h + 1) * HEAD)
            wb = jnp.where(tril, ws_ref[h], 0.0).astype(BF16)
            for n in range(nch):
                rs = slice(n * HEAD, (n + 1) * HEAD)
                u = _gelu(z_ref[rs, cs])
                vhat, _ = _ln_stats(_gelu(z_ref[rs, vs]))
                vn = vhat * gv_ref[:, cs] + bv_ref[:, cs]
                mixed = jnp.dot(wb, vn.astype(BF16), preferred_element_type=F32) + bsb_ref[h]
                y_ref[rs, cs] = (u * mixed).astype(BF16)
        keep = (i > 0).astype(F32)
        rowi = lax.broadcasted_iota(jnp.int32, (tm, HEAD), 0)
        for s in range(DB // HEAD):
            ch = slice(s * HEAD, (s + 1) * HEAD)
            cs = slice(o + DB + s * HEAD, o + DB + (s + 1) * HEAD)
            xs = slice(o + 2 * DB + s * HEAD, o + 2 * DB + (s + 1) * HEAD)
            hc = z_ref[:, cs] * z_ref[:, xs]
            p1 = zp_ref[7:8, cs] * zp_ref[7:8, xs] * keep
            p2 = zp_ref[6:7, cs] * zp_ref[6:7, xs] * keep
            r1, r2 = _conv_shifted(hc, p1, p2, rowi)
            cv = cw_ref[0:1, ch] * r2 + cw_ref[1:2, ch] * r1 + cw_ref[2:3, ch] * hc
            y_ref[:, DA + s * HEAD:DA + (s + 1) * HEAD] = (z_ref[:, o + s * HEAD:o + (s + 1) * HEAD] * cv).astype(BF16)

    full3 = lambda i: (0, 0, 0)
    full2 = lambda i: (0, 0)
    est = 2 * (tm * DIN * 4 + tm * (DA + DB) * 2) + 10 * tm * DB * 4
    return pl.pallas_call(
        body, name="mix_fwd", grid=(S // tm,),
        in_specs=[pl.BlockSpec((tm, DIN), lambda i: (i, 0)),
                  pl.BlockSpec((8, DIN), lambda i: (jnp.maximum(i * (tm // 8) - 1, 0), 0)),
                  pl.BlockSpec(ws.shape, full3), pl.BlockSpec(bsb.shape, full3),
                  pl.BlockSpec(gv.shape, full2), pl.BlockSpec(bv.shape, full2), pl.BlockSpec(cw8.shape, full2)],
        out_specs=pl.BlockSpec((tm, DA + DB), lambda i: (i, 0)),
        out_shape=jax.ShapeDtypeStruct((S, DA + DB), BF16),
        compiler_params=pltpu.CompilerParams(dimension_semantics=("arbitrary",),
                                             vmem_limit_bytes=_vmem_limit(est)),
    )(z, z, ws, bsb, gv, bv, cw8)


def _mix_bwd(z, dy, ws, bsb, gv, bv, cw8, tm):
    S, DIN = z.shape
    DA = gv.shape[1]
    DB = DA
    H = DA // HEAD
    o = 2 * DA
    nch = tm // HEAD
    nblk = S // tm

    def body(z_ref, zp_ref, zn_ref, dy_ref, dyn_ref, ws_ref, bsb_ref, gv_ref, bv_ref, cw_ref,
             dz_ref, dws_ref, dbs_ref, dgv_ref, dbv_ref, dcw_ref, dbin_ref):
        i = pl.program_id(0)

        @pl.when(i == 0)
        def _():
            for r in (dws_ref, dbs_ref, dgv_ref, dbv_ref, dcw_ref, dbin_ref):
                r[...] = jnp.zeros_like(r)

        ri = lax.broadcasted_iota(jnp.int32, (HEAD, HEAD), 0)
        ci = lax.broadcasted_iota(jnp.int32, (HEAD, HEAD), 1)
        tril = ri >= ci
        for h in range(H):
            cs = slice(h * HEAD, (h + 1) * HEAD)
            vs = slice(DA + h * HEAD, DA + (h + 1) * HEAD)
            wb = jnp.where(tril, ws_ref[h], 0.0).astype(BF16)
            gvh = gv_ref[:, cs]
            dws_acc = jnp.zeros((HEAD, HEAD), F32)
            dbs_acc = jnp.zeros((HEAD, HEAD), F32)
            dgv_acc = jnp.zeros((1, HEAD), F32)
            dbv_acc = jnp.zeros((1, HEAD), F32)
            dbu_acc = jnp.zeros((1, HEAD), F32)
            dbvv_acc = jnp.zeros((1, HEAD), F32)
            for n in range(nch):
                rs = slice(n * HEAD, (n + 1) * HEAD)
                u, du_dz = _gelu_and_grad(z_ref[rs, cs])
                v, dv_dz = _gelu_and_grad(z_ref[rs, vs])
                vhat, rstd = _ln_stats(v)
                vnb = (vhat * gvh + bv_ref[:, cs]).astype(BF16)
                mixed = jnp.dot(wb, vnb, preferred_element_type=F32) + bsb_ref[h]
                dya = dy_ref[rs, cs]
                dzu = dya * mixed * du_dz
                dmx = dya * u
                dmxb = dmx.astype(BF16)
                dbs_acc += jnp.broadcast_to(jnp.sum(dmx, axis=1, keepdims=True), (HEAD, HEAD))
                dws_acc += lax.dot_general(dmxb, vnb, _DIMS['nt'], preferred_element_type=F32)
                dvn = lax.dot_general(wb, dmxb, _DIMS['tn'], preferred_element_type=F32)
                dgv_acc += jnp.sum(dvn * vhat, axis=0, keepdims=True)
                dbv_acc += jnp.sum(dvn, axis=0, keepdims=True)
                dxh = dvn * gvh
                m1 = jnp.mean(dxh, axis=-1, keepdims=True)
                m2 = jnp.mean(dxh * vhat, axis=-1, keepdims=True)
                dzv = rstd * (dxh - m1 - vhat * m2) * dv_dz
                dz_ref[rs, cs] = dzu.astype(BF16)
                dz_ref[rs, vs] = dzv.astype(BF16)
                dbu_acc += jnp.sum(dzu, axis=0, keepdims=True)
                dbvv_acc += jnp.sum(dzv, axis=0, keepdims=True)
            dws_ref[h] += jnp.where(tril, dws_acc, 0.0)
            dbs_ref[h] += dbs_acc
            dgv_ref[:, cs] += dgv_acc
            dbv_ref[:, cs] += dbv_acc
            dbin_ref[:, cs] += dbu_acc
            dbin_ref[:, vs] += dbvv_acc

        keep = (i > 0).astype(F32)
        more = (i < nblk - 1).astype(F32)
        rowi = lax.broadcasted_iota(jnp.int32, (tm, HEAD), 0)
        for s in range(DB // HEAD):
            ch = slice(s * HEAD, (s + 1) * HEAD)
            bs = slice(o + s * HEAD, o + (s + 1) * HEAD)
            cs = slice(o + DB + s * HEAD, o + DB + (s + 1) * HEAD)
            xs = slice(o + 2 * DB + s * HEAD, o + 2 * DB + (s + 1) * HEAD)
            ys = slice(DA + s * HEAD, DA + (s + 1) * HEAD)
            zb, zc, zx = z_ref[:, bs], z_ref[:, cs], z_ref[:, xs]
            hc = zc * zx
            p1 = zp_ref[7:8, cs] * zp_ref[7:8, xs] * keep
            p2 = zp_ref[6:7, cs] * zp_ref[6:7, xs] * keep
            r1, r2 = _conv_shifted(hc, p1, p2, rowi)
            w0, w1, w2 = cw_ref[0:1, ch], cw_ref[1:2, ch], cw_ref[2:3, ch]
            cv = w0 * r2 + w1 * r1 + w2 * hc
            dyb = dy_ref[:, ys]
            dzb = dyb * cv
            dcv = dyb * zb
            n0 = dyn_ref[0:1, ys] * zn_ref[0:1, bs] * more
            n1 = dyn_ref[1:2, ys] * zn_ref[1:2, bs] * more
            f1 = jnp.where(rowi == tm - 1, n0, pltpu.roll(dcv, tm - 1, 0))
            f2 = jnp.where(rowi == tm - 1, n1, jnp.where(rowi == tm - 2, n0, pltpu.roll(dcv, tm - 2, 0)))
            dhc = w2 * dcv + w1 * f1 + w0 * f2
            dzc = dhc * zx
            dzx = dhc * zc
            dcw_ref[0:1, ch] += jnp.sum(dcv * r2, axis=0, keepdims=True)
            dcw_ref[1:2, ch] += jnp.sum(dcv * r1, axis=0, keepdims=True)
            dcw_ref[2:3, ch] += jnp.sum(dcv * hc, axis=0, keepdims=True)
            dz_ref[:, bs] = dzb.astype(BF16)
            dz_ref[:, cs] = dzc.astype(BF16)
            dz_ref[:, xs] = dzx.astype(BF16)
            dbin_ref[:, bs] += jnp.sum(dzb, axis=0, keepdims=True)
            dbin_ref[:, cs] += jnp.sum(dzc, axis=0, keepdims=True)
            dbin_ref[:, xs] += jnp.sum(dzx, axis=0, keepdims=True)

    full3 = lambda i: (0, 0, 0)
    full2 = lambda i: (0, 0)
    prev8 = lambda i: (jnp.maximum(i * (tm // 8) - 1, 0), 0)
    next8 = lambda i: (jnp.minimum((i + 1) * (tm // 8), S // 8 - 1), 0)
    est = 2 * (tm * DIN * 4 + tm * (DA + DB) * 4 + tm * DIN * 2) + 16 * tm * DB * 4
    return pl.pallas_call(
        body, name="mix_bwd", grid=(nblk,),
        in_specs=[pl.BlockSpec((tm, DIN), lambda i: (i, 0)),
                  pl.BlockSpec((8, DIN), prev8), pl.BlockSpec((8, DIN), next8),
                  pl.BlockSpec((tm, DA + DB), lambda i: (i, 0)), pl.BlockSpec((8, DA + DB), next8),
                  pl.BlockSpec(ws.shape, full3), pl.BlockSpec(bsb.shape, full3),
                  pl.BlockSpec(gv.shape, full2), pl.BlockSpec(bv.shape, full2), pl.BlockSpec(cw8.shape, full2)],
        out_specs=[pl.BlockSpec((tm, DIN), lambda i: (i, 0)),
                   pl.BlockSpec(ws.shape, full3), pl.BlockSpec(ws.shape, full3),
                   pl.BlockSpec(gv.shape, full2), pl.BlockSpec(gv.shape, full2),
                   pl.BlockSpec(cw8.shape, full2), pl.BlockSpec((1, DIN), full2)],
        out_shape=[jax.ShapeDtypeStruct((S, DIN), BF16),
                   jax.ShapeDtypeStruct(ws.shape, F32), jax.ShapeDtypeStruct(ws.shape, F32),
                   jax.ShapeDtypeStruct(gv.shape, F32), jax.ShapeDtypeStruct(gv.shape, F32),
                   jax.ShapeDtypeStruct(cw8.shape, F32), jax.ShapeDtypeStruct((1, DIN), F32)],
        compiler_params=pltpu.CompilerParams(dimension_semantics=("arbitrary",),
                                             vmem_limit_bytes=_vmem_limit(est)),
    )(z, z, z, dy, dy, ws, bsb, gv, bv, cw8)


def _place():
    x, y, c = lax.axis_index("x"), lax.axis_index("y"), lax.axis_index("c")
    others = [(1 - x, y), (x, 1 - y), (1 - x, 1 - y)]
    return x, y, c, 2 * x + y, others, [2 * ox + oy for ox, oy in others]


def _half(c, rows_half):
    return pl.ds(pl.multiple_of(c * rows_half, 16), rows_half)


def _rcopy(src, dst, ssem, rsem, dev):
    return pltpu.make_async_remote_copy(src_ref=src, dst_ref=dst, send_sem=ssem, recv_sem=rsem,
                                        device_id=dev, device_id_type=MESH)


class _Comm:
    def __init__(self, aliased, ins, fresh, n_sems, stages, sink, relayed=False):
        self.aliased, self.ins, self.fresh = list(aliased), list(ins), list(fresh)
        self.n_sems, self.stages, self.sink, self.relayed = n_sems, stages, sink, relayed


def _in_hbm(a):
    return pltpu.with_memory_space_constraint(a, pltpu.HBM)


def _comm_operands(comms, n_in, n_out):
    arrays, shapes, scratch, aliases = [], [], [], {}
    for cm in comms:
        for a in cm.aliased:
            aliases[n_in + len(arrays)] = n_out + len(shapes)
            arrays.append(_in_hbm(a))
            shapes.append(pltpu.HBM(a.shape, a.dtype))
        arrays += [_in_hbm(a) for a in cm.ins]
        shapes += [pltpu.HBM(s.shape, s.dtype) for s in cm.fresh]
        scratch += [pltpu.SemaphoreType.DMA((cm.n_sems,))] * 2
    return arrays, shapes, scratch, aliases


def _comm_refs(comms, in_refs, out_refs, sem_refs):
    per, pi, po = [], 0, 0
    for i, cm in enumerate(comms):
        pi += len(cm.aliased)
        ins = in_refs[pi:pi + len(cm.ins)]
        pi += len(cm.ins)
        al = out_refs[po:po + len(cm.aliased)]
        po += len(cm.aliased)
        fr = out_refs[po:po + len(cm.fresh)]
        po += len(cm.fresh)
        per.append((al, ins, fr, sem_refs[2 * i], sem_refs[2 * i + 1]))
    return per


def _comm_first_step(comms, hosted, grid):
    if not comms:
        return
    at_first = functools.reduce(lambda a, b: a & b, [pl.program_id(d) == 0 for d in range(len(grid))])

    @pl.when(at_first)
    def _():
        for cm, refs in zip(comms, hosted):
            for d in cm.stages[0](*refs, False):
                d.start()


def _comm_mid_step(comms, hosted, grid):
    relayed = [(cm, refs) for cm, refs in zip(comms, hosted) if cm.relayed]
    if not relayed:
        return
    n_steps = math.prod(grid)
    assert n_steps >= 3, grid
    mid, at_mid = (2 * n_steps) // 3, None
    for d in reversed(range(len(grid))):
        here = pl.program_id(d) == mid % grid[d]
        at_mid = here if at_mid is None else at_mid & here
        mid //= grid[d]

    @pl.when(at_mid)
    def _():
        for cm, refs in relayed:
            for d in cm.stages[0](*refs, True):
                d.wait()
            for d in cm.stages[1](*refs, False):
                d.start()


def _comm_last_step(comms, hosted, grid):
    if not comms:
        return
    at_last = functools.reduce(lambda a, b: a & b, [pl.program_id(d) == g - 1 for d, g in enumerate(grid)])

    @pl.when(at_last)
    def _():
        for cm, refs in zip(comms, hosted):
            for d in cm.stages[1 if cm.relayed else 0](*refs, True):
                d.wait()
        for cm, refs in zip(comms, hosted):
            for stage in cm.stages[2 if cm.relayed else 1:]:
                for d in stage(*refs, False):
                    d.start()
                for d in stage(*refs, True):
                    d.wait()


def _deliver(comms, outs):
    pos = 0
    for cm in comms:
        al = outs[pos:pos + len(cm.aliased)]
        pos += len(cm.aliased)
        fr = outs[pos:pos + len(cm.fresh)]
        pos += len(cm.fresh)
        cm.sink(list(al), list(fr))


def _comm_call(name, comms):
    in_arrays, out_shapes, scratch, aliases = _comm_operands(comms, 0, 0)
    n_in, n_out = len(in_arrays), len(out_shapes)

    def body(*refs):
        hosted = _comm_refs(comms, refs[:n_in], refs[n_in:n_in + n_out], refs[n_in + n_out:])
        for cm, r in zip(comms, hosted):
            for stage in cm.stages:
                for d in stage(*r, False):
                    d.start()
                for d in stage(*r, True):
                    d.wait()

    outs = pl.pallas_call(body, name=name, in_specs=[ANY] * n_in, out_specs=[ANY] * n_out,
                          out_shape=out_shapes, scratch_shapes=scratch, input_output_aliases=aliases)(*in_arrays)
    _deliver(comms, outs)


def _c_gather(fulls, sink, ici, fwd, conv=None):
    n = len(fulls)
    n_direct = 2 * n + (3 if conv is not None else 0)
    n_ici = n_direct + n if ici else 0

    def stage_direct(al, ins, fr, ssem, rsem, wait):
        x, y, c, chip, others, ochips = _place()
        ds = []
        for m in range(n):
            ah = al[m].shape[1] // 2
            mine = al[m].at[chip, _half(c, ah)]
            for k in range(2):
                i = 2 * m + k
                ds.append(_rcopy(mine, al[m].at[ochips[k], _half(c, ah)], ssem.at[i], rsem.at[i], (x, y, c)) if wait
                          else _rcopy(mine, mine, ssem.at[i], rsem.at[i], (*others[k], c)))
        if conv is not None:
            mine = al[n].at[chip]
            for k in range(3):
                i = 2 * n + k
                ds.append(_rcopy(mine, al[n].at[ochips[k]], ssem.at[i], rsem.at[i], (x, y, c)) if wait
                          else _rcopy(mine, mine, ssem.at[i], rsem.at[i], (*others[k], c)))
        return ds

    def stage_relay(al, ins, fr, ssem, rsem, wait):
        x, y, c, chip, others, ochips = _place()
        got_from = ochips[1] + (ochips[0] - ochips[1]) * c
        send_to = (x + (1 - 2 * x) * (1 - c), y + (1 - 2 * y) * c, c)
        ds = []
        for m in range(n):
            ah = al[m].shape[1] // 2
            i = n_direct + m
            got = al[m].at[got_from, _half(c, ah)]
            ds.append(_rcopy(got, al[m].at[ochips[2], _half(c, ah)], ssem.at[i], rsem.at[i], (x, y, c)) if wait
                      else _rcopy(got, got, ssem.at[i], rsem.at[i], send_to))
        return ds

    def stage_fwd(al, ins, fr, ssem, rsem, wait):
        x, y, c, chip, others, ochips = _place()
        ds = []
        for m in range(n):
            ah = al[m].shape[1] // 2
            for k in range(3):
                i = n_ici + 3 * m + k
                got = al[m].at[ochips[k], _half(c, ah)]
                ds.append(_rcopy(got, al[m].at[ochips[k], _half(1 - c, ah)], ssem.at[i], rsem.at[i], (x, y, c)) if wait
                          else _rcopy(got, got, ssem.at[i], rsem.at[i], (x, y, 1 - c)))
        return ds

    stages = ([stage_direct, stage_relay] if ici else []) + ([stage_fwd] if fwd else [])
    return _Comm(list(fulls) + ([conv] if conv is not None else []), [], [], n_ici + (3 * n if fwd else 0),
                 stages, sink, relayed=ici)


def _c_rs_pair(parts, sink):
    def stage(al, ins, fr, ssem, rsem, wait):
        x, y, c, _, _, _ = _place()
        return [_rcopy(ins[m].at[:, _half(1 - c, ins[m].shape[1] // 2), :], fr[m], ssem.at[m], rsem.at[m],
                       (x, y, 1 - c)) for m in range(len(ins))]
    fresh = [jax.ShapeDtypeStruct((p.shape[0], p.shape[1] // 2, p.shape[2]), p.dtype) for p in parts]
    return _Comm([], parts, fresh, len(parts), [stage], sink)


def _c_rs_chips(psums, sink):
    def stage(al, ins, fr, ssem, rsem, wait):
        x, y, c, _, others, ochips = _place()
        return [_rcopy(ins[m].at[ochips[k]], fr[m].at[k], ssem.at[3 * m + k], rsem.at[3 * m + k], (*others[k], c))
                for m in range(len(ins)) for k in range(3)]
    fresh = [jax.ShapeDtypeStruct((3,) + p.shape[1:], p.dtype) for p in psums]
    return _Comm([], psums, fresh, 3 * len(psums), [stage], sink)


def _c_exchange(g2s, sink):
    def stage(al, ins, fr, ssem, rsem, wait):
        x, y, c, _, _, _ = _place()
        if wait:
            return [_rcopy(al[m].at[c], al[m].at[1 - c], ssem.at[m], rsem.at[m], (x, y, c)) for m in range(len(al))]
        return [_rcopy(al[m].at[c], al[m].at[c], ssem.at[m], rsem.at[m], (x, y, 1 - c)) for m in range(len(al))]
    return _Comm(g2s, [], [], len(g2s), [stage], sink)


def _c_allgather_small(buf, sink):
    def stage(al, ins, fr, ssem, rsem, wait):
        x, y, c, _, _, _ = _place()
        ds = []
        for r in range(1, 8):
            peer = (1 - x if r & 4 else x, 1 - y if r & 2 else y, 1 - c if r & 1 else c)
            ds.append(_rcopy(al[0].at[0], al[0].at[r], ssem.at[r - 1], rsem.at[r - 1], peer))
        return ds
    return _Comm([buf], [], [], 7, [stage], sink)


def _sum_small(buf, dev_idx):
    _, m, n = buf.shape

    def body(me_ref, b_ref, o_ref):
        me = me_ref[0]
        acc = b_ref[me]
        for d in range(1, 8):
            acc = acc + b_ref[jnp.bitwise_xor(me, d)]
        o_ref[...] = acc

    gs = pltpu.PrefetchScalarGridSpec(
        num_scalar_prefetch=1, grid=(1,),
        in_specs=[pl.BlockSpec(buf.shape, lambda i, me: (0, 0, 0))],
        out_specs=pl.BlockSpec((m, n), lambda i, me: (0, 0)))
    return pl.pallas_call(body, name="sum_small", grid_spec=gs,
                          out_shape=jax.ShapeDtypeStruct((m, n), F32))(dev_idx, buf)


def _sum_pair(name, part, sib, c_idx):
    _, A, B = part.shape
    ah = A // 2
    br = _rows(ah, B * 2)

    def body(c_ref, a_ref, b_ref, o_ref):
        o_ref[...] = (a_ref[...].astype(F32) + b_ref[...].astype(F32)).astype(BF16)

    gs = pltpu.PrefetchScalarGridSpec(
        num_scalar_prefetch=1, grid=(N_CHIPS, ah // br),
        in_specs=[pl.BlockSpec((None, None, br, B), lambda j, i, c: (j, c[0], i, 0)),
                  pl.BlockSpec((None, br, B), lambda j, i, c: (j, i, 0))],
        out_specs=pl.BlockSpec((None, br, B), lambda j, i, c: (j, i, 0)))
    return pl.pallas_call(body, name=name, grid_spec=gs,
                          out_shape=pltpu.HBM((N_CHIPS, ah, B), BF16),
                          compiler_params=pltpu.CompilerParams(vmem_limit_bytes=_vmem_limit(2 * 3 * br * B * 2)),
                          )(c_idx, _in_hbm(part.reshape(N_CHIPS, 2, ah, B)), _in_hbm(sib))


def _sum_chips(name, psum, recv, place_idx):
    _, ah, B = psum.shape
    br = _rows(ah, B * 4)

    def body(c_ref, p_ref, r0_ref, r1_ref, r2_ref, o_ref):
        o_ref[...] = ((p_ref[...].astype(F32) + r0_ref[...].astype(F32)) + r1_ref[...].astype(F32)) \
            + r2_ref[...].astype(F32)

    gs = pltpu.PrefetchScalarGridSpec(
        num_scalar_prefetch=1, grid=(ah // br,),
        in_specs=[pl.BlockSpec((None, br, B), lambda i, c: (c[0], i, 0))]
        + [pl.BlockSpec((None, br, B), functools.partial(lambda i, c, k: (k, i, 0), k=k)) for k in range(3)],
        out_specs=pl.BlockSpec((None, br, B), lambda i, c: (c[1], i, 0)))
    return pl.pallas_call(body, name=name, grid_spec=gs,
                          out_shape=pltpu.HBM((2, ah, B), F32),
                          compiler_params=pltpu.CompilerParams(vmem_limit_bytes=_vmem_limit(2 * br * B * (4 * 2 + 4))),
                          )(place_idx, _in_hbm(psum), *[_in_hbm(recv)] * 3)


def _cast_into(name, a, place_idx):
    A, B = a.shape
    br = _rows(A, B * 4)

    def body(c_ref, a_ref, o_ref):
        o_ref[...] = a_ref[...].astype(BF16)

    gs = pltpu.PrefetchScalarGridSpec(
        num_scalar_prefetch=1, grid=(A // br,),
        in_specs=[pl.BlockSpec((br, B), lambda i, c: (i, 0))],
        out_specs=pl.BlockSpec((None, br, B), lambda i, c: (c[0], i, 0)))
    return pl.pallas_call(body, name=name, grid_spec=gs,
                          out_shape=pltpu.HBM((N_CHIPS, A, B), BF16))(place_idx, a)


def _cast_behind(x, shards, place_idx, comms):
    n_steps = 16
    n = len(shards)
    xv = x
    svs = list(shards)

    def pace(v):
        for steps_per_block in (1, 2, 4, 8, 16):
            blocks = n_steps // steps_per_block
            if v.shape[0] % blocks == 0 and (v.shape[0] // blocks) % 16 == 0:
                return steps_per_block
        raise ValueError(v.shape)

    n_in, n_out = 2 + n, 1 + n
    cm_in_arrays, cm_out_shapes, cm_scratch, aliases = _comm_operands(comms, n_in, n_out)
    n_cm_in, n_cm_out = len(cm_in_arrays), len(cm_out_shapes)
    grid = (n_steps,)

    def body(c_ref, x_ref, *rest):
        s_refs = rest[:n]
        xo_ref = rest[n + n_cm_in]
        so_refs = rest[n + n_cm_in + 1:n + n_cm_in + 1 + n]
        o_end = n + n_cm_in + 1 + n
        hosted = _comm_refs(comms, rest[n:n + n_cm_in], rest[o_end:o_end + n_cm_out], rest[o_end + n_cm_out:])
        _comm_first_step(comms, hosted, grid)
        _comm_mid_step(comms, hosted, grid)
        xo_ref[...] = x_ref[...].astype(BF16)
        for s_ref, o_ref in zip(s_refs, so_refs):
            o_ref[...] = s_ref[...].astype(BF16)
        _comm_last_step(comms, hosted, grid)

    def rows(v):
        return v.shape[0] * pace(v) // n_steps

    def in_spec(v):
        return pl.BlockSpec((rows(v), v.shape[1]), functools.partial(lambda i, c, q: (i // q, 0), q=pace(v)))

    def shard_out_spec(v):
        return pl.BlockSpec((None, rows(v), v.shape[1]),
                            functools.partial(lambda i, c, q: (c[0], i // q, 0), q=pace(v)))

    gs = pltpu.PrefetchScalarGridSpec(
        num_scalar_prefetch=1, grid=grid,
        in_specs=[in_spec(v) for v in [xv] + svs] + [ANY] * n_cm_in,
        out_specs=[in_spec(xv)] + [shard_out_spec(v) for v in svs] + [ANY] * n_cm_out,
        scratch_shapes=cm_scratch)
    outs = pl.pallas_call(
        body, name="cast_behind", grid_spec=gs,
        out_shape=[jax.ShapeDtypeStruct(xv.shape, BF16)]
        + [pltpu.HBM((N_CHIPS,) + v.shape, BF16) for v in svs] + cm_out_shapes,
        input_output_aliases=aliases,
        compiler_params=pltpu.CompilerParams(
            dimension_semantics=("arbitrary",),
            vmem_limit_bytes=_vmem_limit(2 * sum(rows(v) * v.shape[1] * 6 for v in [xv] + svs))),
    )(place_idx, xv, *svs, *cm_in_arrays)
    _deliver(comms, outs[n_out:])
    return outs[0].reshape(x.shape), [o.reshape((N_CHIPS,) + s.shape) for o, s in zip(outs[1:n_out], shards)]


def _adamw_small(gsum, gconv, offsets, ws, ms, vs):
    n = len(ws)

    def body(*refs):
        g_ref, gc_ref = refs[0], refs[1]
        w_refs = refs[2:2 + n]
        m_refs = refs[2 + n:2 + 2 * n]
        v_refs = refs[2 + 2 * n:2 + 3 * n]
        outs = refs[2 + 3 * n:]
        for i in range(n):
            rows = w_refs[i].shape[0]
            g = gc_ref[...] if offsets[i] is None else g_ref[offsets[i]:offsets[i] + rows, :]
            d, m2, v2 = _adamw_math(w_refs[i][...], g, m_refs[i][...], v_refs[i][...])
            outs[4 * i][...] = g
            outs[4 * i + 1][...] = d
            outs[4 * i + 2][...] = m2
            outs[4 * i + 3][...] = v2

    vm = pl.BlockSpec(memory_space=pltpu.VMEM)
    out_shape = []
    for w in ws:
        out_shape += [jax.ShapeDtypeStruct(w.shape, F32)] * 4
    return pl.pallas_call(body, name="adamw_small", in_specs=[vm] * (2 + 3 * n), out_specs=[vm] * (4 * n),
                          out_shape=out_shape)(gsum, gconv, *ws, *ms, *vs)


def _ident(accs, tes, res):
    return [accs[0]]


def _ffn_up(name, hb, wg4, wu4, comms=()):
    S, D = hb.shape

    def epi(accs, tes, res):
        g, u = accs
        return [g, u, g * _sigmoid(g) * u]

    return _mm(name, 'nn', [hb], [wg4, wu4], [(0, 0, 0), (0, 1, 1)], 2, epi, [BF16] * 3,
               bm=_blk(S, 512), bn=wg4.shape[2], bk=D, b_blocked=True, n_outer=True, comms=comms)


def _ffn_gate(name, hb, wg4, comms=()):
    S, D = hb.shape
    return _mm(name, 'nn', [hb], [wg4], [(0, 0, 0)], 1, _ident, [BF16],
               bm=_blk(S, 1024), bn=wg4.shape[2], bk=D, b_blocked=True, n_outer=True, comms=comms)[0]


def _ffn_upmul(name, hb, wu4, gb, comms=()):
    S, D = hb.shape

    def epi(accs, tes, res):
        g = tes[0].astype(F32)
        return [accs[0], g * _sigmoid(g) * accs[0]]

    return _mm(name, 'nn', [hb], [wu4], [(0, 0, 0)], 1, epi, [BF16, BF16],
               bm=_blk(S, 1024), bn=wu4.shape[2], bk=D, tile_extras=[gb], b_blocked=True, n_outer=True, comms=comms)


def _ffn_down(name, ab, wd, hres, comms=()):
    S, F = ab.shape
    D = wd.shape[1]
    return _mm(name, 'nn', [ab], [wd], [(0, 0, 0)], 1, lambda accs, tes, res: [ALPHA * tes[0] + 0.5 * accs[0]],
               [F32], bm=_blk(S, 1024), bn=_blk(D, 512), bk=F, tile_extras=[hres], comms=comms)[0]


def _ffn_da(name, dsub, wd, gb, ub, comms=()):
    S, D = dsub.shape
    F = wd.shape[0]

    def epi(accs, tes, res):
        da = accs[0]
        g = tes[0].astype(F32)
        u = tes[1].astype(F32)
        s = _sigmoid(g)
        silu = g * s
        return [(da * u) * (s + silu * (1.0 - s)), da * silu]

    return _mm(name, 'nt', [dsub], [wd], [(0, 0, 0)], 1, epi, [BF16, BF16],
               bm=_blk(S, 1024), bn=_blk(F, 512), bk=D, tile_extras=[gb, ub], comms=comms)


def _dw_rows(name, ab, dsub, cs, comms=()):
    S, D = dsub.shape
    return _mm(name, 'tn', [ab], [dsub], [(0, 0, 0)], 1, _ident, [BF16],
               bm=cs, bn=_blk(D, 1024), bk=_blk(S, 2048), comms=comms)[0].reshape(N_CHIPS, cs, D)


def _dw_cols(name, hb, dxb, cs, comms=()):
    S, D = hb.shape
    return _mm(name, 'tn', [hb], [dxb], [(0, 0, 0)], 1, _ident, [BF16],
               bm=_blk(D, 1024), bn=cs, bk=_blk(S, 2048), out_blocked=True, comms=comms)[0]


def _ffn_dh(name, dgb, dub, wg4, wu4, dres, comms=()):
    S = dgb.shape[0]
    D = wg4.shape[1]
    return _mm(name, 'nt', [dgb, dub], [wg4, wu4], [(0, 0, 0), (1, 1, 0)], 1,
               lambda accs, tes, res: [accs[0] + tes[0]], [F32],
               bm=_blk(S, 1024), bn=_blk(D, 1024), bk=wg4.shape[2], tile_extras=[dres], b_blocked=True,
               comms=comms)[0]


def _step(p):
    x = p['x'][0]
    target = p['loss_target'][0]
    S, D = x.shape
    cx, cy, cc = lax.axis_index("x"), lax.axis_index("y"), lax.axis_index("c")
    chip = 2 * cx + cy
    c_idx = jnp.reshape(cc, (1,)).astype(jnp.int32)
    place_idx = jnp.stack([chip, cc]).astype(jnp.int32)

    w_s = p['w_s'][0]
    H = w_s.shape[0]
    DA = H * HEAD
    bsb = jnp.broadcast_to(p['b_s'][0][:, :, None], (H, HEAD, HEAD))
    conv_shard = p['conv_w'][0]
    conv8 = jnp.zeros((8, conv_shard.shape[1]), F32).at[:CONV_TAPS].set(conv_shard)

    W = {'ffa_gate': _cast_into("cast_ffa_gate", p['ffa_gate'][0], place_idx)}
    conv = {'full': lax.dynamic_update_slice(jnp.zeros((N_CHIPS,) + conv8.shape, F32), conv8[None], (chip, 0, 0))}

    def gathered(names, with_conv=False):
        def sink(al, fr):
            W.update(zip(names, al))
            if with_conv:
                conv['full'] = al[len(names)]
        return sink

    def c_gather(names, ici, fwd, with_conv=False):
        return _c_gather([W[n] for n in names], gathered(names, with_conv), ici, fwd,
                         conv['full'] if with_conv else None)

    rest = [n for n in BIG if n != 'ffa_gate']
    xb, casted = _cast_behind(x, [p[n][0] for n in rest], place_idx, comms=[c_gather(['ffa_gate'], True, True)])
    W.update(zip(rest, casted))
    cs_ff = W['ffa_gate'].shape[2]
    F = N_CHIPS * cs_ff

    ga = _ffn_gate("ffa_gate", xb, W['ffa_gate'],
                   comms=[c_gather(['ffa_up'], True, True), c_gather(['ffa_down'], True, False)])
    ua, aa = _ffn_upmul("ffa_up", xb, W['ffa_up'], ga,
                        comms=[c_gather(['ffa_down'], False, True),
                               c_gather(['w_in', 'ffc_gate'], True, False, with_conv=True)])
    cw8 = jnp.transpose(conv['full'], (1, 0, 2)).reshape(8, DA)
    wd_a = W['ffa_down'].reshape(F, D)
    r1 = _ffn_down("ffa_down", aa, wd_a, x,
                   comms=[c_gather(['w_in', 'ffc_gate'], False, True), c_gather(['w_out', 'ffc_down'], True, False)])
    h1, h1b = _ln_fwd("ln_a", r1, p['ln_a_g'], p['ln_a_b'])
    w_in4 = W['w_in']
    cs_in = w_in4.shape[2]
    z = _mm("w_in", 'nn', [h1b], [w_in4], [(0, 0, 0)], 1, lambda accs, tes, res: [accs[0] + res[0]], [F32],
            bm=_blk(S, 1024), bn=cs_in, bk=D, row_extras=[p['b_in']], b_blocked=True, n_outer=True,
            comms=[c_gather(['w_out', 'ffc_down'], False, True), c_gather(['ffc_up'], True, False)])[0]
    w_out = W['w_out'].reshape(DA * 2, D)
    yb = _mix_fwd(z, w_s, bsb, p['ln_v_g'], p['ln_v_b'], cw8, _blk(S, 2 * HEAD))
    r2 = _mm("w_out", 'nn', [yb], [w_out], [(0, 0, 0)], 1,
             lambda accs, tes, res: [accs[0] + res[0] + ALPHA * tes[0]], [F32],
             bm=_blk(S, 1024), bn=_blk(D, 1024), bk=2 * DA, tile_extras=[h1], row_extras=[p['b_out']],
             comms=[c_gather(['ffc_up'], False, True)])[0]
    h2, h2b = _ln_fwd("ln_m", r2, p['ln_m_g'], p['ln_m_b'])
    gc, uc, ac = _ffn_up("ffc_up", h2b, W['ffc_gate'], W['ffc_up'])
    wd_c = W['ffc_down'].reshape(F, D)
    r3 = _ffn_down("ffc_down", ac, wd_c, h2)

    pg, sg, sib, psum, recv, g2 = {}, {}, {}, {}, {}, {}

    def c_pair(names):
        return _c_rs_pair([pg[n] for n in names], lambda al, fr: sib.update(zip(names, fr)))

    def c_chips(names):
        return _c_rs_chips([psum[n] for n in names], lambda al, fr: recv.update(zip(names, fr)))

    def c_swap(names):
        return _c_exchange([g2[n] for n in names], lambda al, fr: g2.update(zip(names, al)))

    def sum_pair(*names):
        for n in names:
            psum[n] = _sum_pair(f"sum_pair_{n}", pg[n], sib[n], c_idx)

    def sum_chips(*names):
        for n in names:
            g2[n] = _sum_chips(f"sum_chips_{n}", psum[n], recv[n], place_idx)

    dres3, dsub3, sg['ln_c_g'], sg['ln_c_b'], lsum = _ln_loss_bwd(
        "ln_c_loss_bwd", r3, target, p['ln_c_g'], p['ln_c_b'], 0.5)

    dgc, duc = _ffn_da("ffc_da", dsub3, wd_c, gc, uc)
    pg['ffc_down'] = _dw_rows("ffc_dwd", ac, dsub3, cs_ff)
    pg['ffc_gate'] = _dw_cols("ffc_dwg", h2b, dgc, cs_ff)
    pg['ffc_up'] = _dw_cols("ffc_dwu", h2b, duc, cs_ff, comms=[c_pair(['ffc_down', 'ffc_gate'])])
    sum_pair('ffc_down', 'ffc_gate')
    dh2 = _ffn_dh("ffc_dh", dgc, duc, W['ffc_gate'], W['ffc_up'], dres3,
                  comms=[c_pair(['ffc_up']), c_chips(['ffc_down']), c_chips(['ffc_gate'])])
    sum_pair('ffc_up')
    sum_chips('ffc_down', 'ffc_gate')

    dres2, dmix, sg['ln_m_g'], sg['ln_m_b'], sg['b_out'] = _ln_bwd("ln_m_bwd", r2, dh2, p['ln_m_g'], 1.0)
    pg['w_out'] = _mm("dw_out", 'tn', [yb], [dmix], [(0, 0, 0)], 1, _ident, [BF16],
                      bm=_blk(2 * DA, 1024), bn=_blk(D, 1024),
                      bk=_blk(S, 2048))[0].reshape(N_CHIPS, 2 * DA // 4, D)
    dy = _mm("dy", 'nt', [dmix], [w_out], [(0, 0, 0)], 1, _ident, [F32],
             bm=_blk(S, 1024), bn=_blk(2 * DA, 1024), bk=D)[0]
    dz, dws, dbs, sg['ln_v_g'], sg['ln_v_b'], dcw, sg['b_in'] = _mix_bwd(
        z, dy, w_s, bsb, p['ln_v_g'], p['ln_v_b'], cw8, HEAD)
    sg['w_s'] = dws
    sg['b_s'] = dbs[:, :, 0]
    pg['w_in'] = _dw_cols("dw_in", h1b, dz, cs_in,
                          comms=[c_chips(['ffc_up']), c_pair(['w_out']), c_swap(['ffc_down', 'ffc_gate'])])
    sum_chips('ffc_up')
    sum_pair('w_out')
    dh1 = _mm("dh1", 'nt', [dz], [w_in4], [(0, 0, 0)], 1, lambda accs, tes, res: [accs[0] + tes[0]], [F32],
              bm=_blk(S, 1024), bn=_blk(D, 1024), bk=cs_in, tile_extras=[dres2], b_blocked=True,
              comms=[c_pair(['w_in']), c_chips(['w_out']), c_swap(['ffc_up'])])[0]
    sum_pair('w_in')
    sum_chips('w_out')

    dres1, dsub1, sg['ln_a_g'], sg['ln_a_b'], _ = _ln_bwd("ln_a_bwd", r1, dh1, p['ln_a_g'], 0.5)
    def as_rows(a):
        a = a.reshape(-1, 128)
        pad = (-a.shape[0]) % 8
        return jnp.pad(a, ((0, pad), (0, 0))) if pad else a

    pieces, offsets, off = [], {}, 0
    for n in SMALL + ['conv_w']:
        piece = as_rows(dcw[:CONV_TAPS] if n == 'conv_w' else sg[n])
        offsets[n] = off
        off += piece.shape[0]
        pieces.append(piece)
    packed = jnp.concatenate(pieces, axis=0)
    small = {'buf': jnp.zeros((8,) + packed.shape, F32).at[0].set(packed)}
    dga, dua = _ffn_da("ffa_da", dsub1, wd_a, ga, ua,
                       comms=[c_chips(['w_in']), c_swap(['w_out']),
                              _c_allgather_small(small['buf'], lambda al, fr: small.update(buf=al[0]))])
    sum_chips('w_in')
    gsum = _sum_small(small['buf'], jnp.reshape(4 * cx + 2 * cy + cc, (1,)).astype(jnp.int32))
    pg['ffa_down'] = _dw_rows("ffa_dwd", aa, dsub1, cs_ff)
    pg['ffa_gate'] = _dw_cols("ffa_dwg", xb, dga, cs_ff)
    pg['ffa_up'] = _dw_cols("ffa_dwu", xb, dua, cs_ff, comms=[c_pair(['ffa_down', 'ffa_gate']), c_swap(['w_in'])])
    sum_pair('ffa_down', 'ffa_gate')
    _comm_call("pair_ffa_up", [c_pair(['ffa_up'])])
    sum_pair('ffa_up')
    dx = _ffn_dh("ffa_dh", dga, dua, W['ffa_gate'], W['ffa_up'], dres1,
                 comms=[c_chips(['ffa_down']), c_chips(['ffa_gate']), c_chips(['ffa_up'])])
    sum_chips('ffa_down', 'ffa_gate', 'ffa_up')
    _comm_call("swap_ffa", [c_swap(['ffa_down', 'ffa_gate', 'ffa_up'])])

    out = {}
    for n in BIG:
        g, d, m2, v2 = _adamw(f"adamw_{n}", p[n][0], g2[n].reshape(p[n][0].shape), p['m_' + n][0], p['v_' + n][0])
        shp = p[n].shape
        out[n] = (g.reshape(shp), d.reshape(shp), m2.reshape(shp), v2.reshape(shp))

    loss = lax.psum(0.5 * jnp.sum(lsum) / D, ("x", "y", "c"))
    n_conv = CONV_TAPS * DA // 128
    conv_sum = gsum[offsets['conv_w']:offsets['conv_w'] + n_conv].reshape(CONV_TAPS, DA)
    cw = conv_shard.shape[1]
    gconv = lax.dynamic_slice_in_dim(conv_sum, (2 * cx + cy) * cw, cw, axis=1).reshape(-1, 128)
    names = SMALL + ['conv_w']
    rows128 = lambda a: a.reshape(-1, 128)
    res = _adamw_small(gsum, gconv, [offsets[n] if n != 'conv_w' else None for n in names],
                       [rows128(p[n]) for n in names], [rows128(p['m_' + n]) for n in names],
                       [rows128(p['v_' + n]) for n in names])
    for i, n in enumerate(names):
        out[n] = tuple(r.reshape(p[n].shape) for r in res[4 * i:4 * i + 4])

    return (loss, dx.reshape(p['x'].shape), *[out[n][0] for n in WEIGHTS], *[out[n][1] for n in WEIGHTS],
            *[out[n][2] for n in WEIGHTS], *[out[n][3] for n in WEIGHTS])


def kernel(x, ffa_gate, ffa_up, ffa_down, ln_a_g, ln_a_b, w_in, b_in, w_s, b_s, ln_v_g, ln_v_b, conv_w, w_out, b_out, ln_m_g, ln_m_b, ffc_gate, ffc_up, ffc_down, ln_c_g, ln_c_b, loss_target, m_ffa_gate, m_ffa_up, m_ffa_down, m_ln_a_g, m_ln_a_b, m_w_in, m_b_in, m_w_s, m_b_s, m_ln_v_g, m_ln_v_b, m_conv_w, m_w_out, m_b_out, m_ln_m_g, m_ln_m_b, m_ffc_gate, m_ffc_up, m_ffc_down, m_ln_c_g, m_ln_c_b, v_ffa_gate, v_ffa_up, v_ffa_down, v_ln_a_g, v_ln_a_b, v_w_in, v_b_in, v_w_s, v_b_s, v_ln_v_g, v_ln_v_b, v_conv_w, v_w_out, v_b_out, v_ln_m_g, v_ln_m_b, v_ffc_gate, v_ffc_up, v_ffc_down, v_ln_c_g, v_ln_c_b):
    return _step(dict(locals()))
```

```python
import functools
import math

import jax
import jax.numpy as jnp
from jax import lax
from jax.experimental import pallas as pl
from jax.experimental.pallas import tpu as pltpu

F32 = jnp.float32
BF16 = jnp.bfloat16
MESH = pl.DeviceIdType.MESH
ANY = pl.BlockSpec(memory_space=pl.ANY)

HEAD = 128
CONV_TAPS = 3
LN_EPS = 1e-5
ALPHA = float(2.0 ** 0.25)
ADAM_LR, ADAM_B1, ADAM_B2, ADAM_EPS, ADAM_WD, ADAM_STEP = 0.001, 0.9, 0.999, 1e-08, 0.01, 10

V7X_VMEM_BYTES = 64 * 2 ** 20
VMEM_COMPILER_RESERVE = 6 * 2 ** 20
VMEM_MIN_REQUEST = 32 * 2 ** 20
VMEM_ESTIMATE_SLACK = 1.3
N_CHIPS = 4
EPILOGUE_ROWS = 128

BIG = ['ffa_gate', 'ffa_up', 'ffa_down', 'w_in', 'w_out', 'ffc_gate', 'ffc_up', 'ffc_down']
WEIGHTS = ['ffa_gate', 'ffa_up', 'ffa_down', 'ln_a_g', 'ln_a_b', 'w_in', 'b_in', 'w_s', 'b_s', 'ln_v_g', 'ln_v_b',
           'conv_w', 'w_out', 'b_out', 'ln_m_g', 'ln_m_b', 'ffc_gate', 'ffc_up', 'ffc_down', 'ln_c_g', 'ln_c_b']
SMALL = [n for n in WEIGHTS if n not in BIG and n != 'conv_w']


def _vmem_limit(estimate_bytes):
    return int(min(max(estimate_bytes * VMEM_ESTIMATE_SLACK, VMEM_MIN_REQUEST),
                   V7X_VMEM_BYTES - VMEM_COMPILER_RESERVE))


def _blk(dim, pref, mult=128):
    if dim <= pref:
        return dim
    best = None
    for d in range(mult, pref + 1, mult):
        if dim % d == 0:
            best = d
    assert best is not None, (dim, pref)
    return best


def _rows(n_rows, row_bytes, target=4 * 2 ** 20):
    return _blk(n_rows, max(16, target // row_bytes), 16)


def _sigmoid(x):
    return 0.5 * jnp.tanh(0.5 * x) + 0.5


_GELU_C = math.sqrt(2.0 / math.pi)
_GELU_A = 0.044715


def _gelu(x):
    return (0.5 * x) * (1.0 + jnp.tanh(x * (_GELU_C + (_GELU_C * _GELU_A) * (x * x))))


def _gelu_and_grad(x):
    x2 = x * x
    t = jnp.tanh(x * (_GELU_C + (_GELU_C * _GELU_A) * x2))
    half_x = 0.5 * x
    one_plus_t = 1.0 + t
    dg = 0.5 * one_plus_t + half_x * (1.0 - t * t) * (_GELU_C + (3.0 * _GELU_C * _GELU_A) * x2)
    return half_x * one_plus_t, dg


def _ln_stats(r):
    mu = jnp.mean(r, axis=-1, keepdims=True)
    d = r - mu
    var = jnp.mean(d * d, axis=-1, keepdims=True)
    rstd = lax.rsqrt(var + LN_EPS)
    return d * rstd, rstd


_DIMS = {'nn': (((1,), (0,)), ((), ())), 'nt': (((1,), (1,)), ((), ())), 'tn': (((0,), (0,)), ((), ()))}


def _mm(name, mode, a_list, b_list, pairs, n_acc, epilogue, out_dtypes, bm, bn, bk,
        tile_extras=(), row_extras=(), b_blocked=False, out_blocked=False, n_outer=False, comms=()):
    a0, b0 = a_list[0], b_list[0]
    if mode == 'nn':
        M, K = a0.shape
        N = b0.shape[0] * b0.shape[2] if b_blocked else b0.shape[1]
    elif mode == 'nt':
        M, K = a0.shape
        N = b0.shape[1] if b_blocked else b0.shape[0]
    else:
        K, M = a0.shape
        N = b0.shape[1]
        assert not b_blocked
    assert M % bm == 0 and N % bn == 0 and K % bk == 0, (name, M, N, K, bm, bn, bk)
    gm, gn, gk = M // bm, N // bn, K // bk
    if b_blocked:
        assert (bn if mode == 'nn' else bk) == b0.shape[2], name
    dims = _DIMS[mode]

    def spec(shape, f):
        if n_outer:
            return pl.BlockSpec(shape, lambda g0, g1, g2: f(g1, g0, g2))
        return pl.BlockSpec(shape, f)

    if mode == 'tn':
        a_spec = spec((bk, bm), lambda i, j, k: (k, i))
    else:
        a_spec = spec((bm, bk), lambda i, j, k: (i, k))
    if mode == 'nn':
        b_spec = (spec((None, bk, bn), lambda i, j, k: (j, k, 0)) if b_blocked
                  else spec((bk, bn), lambda i, j, k: (k, j)))
    elif mode == 'nt':
        b_spec = (spec((None, bn, bk), lambda i, j, k: (k, j, 0)) if b_blocked
                  else spec((bn, bk), lambda i, j, k: (j, k)))
    else:
        b_spec = spec((bk, bn), lambda i, j, k: (k, j))
    te_spec = spec((bm, bn), lambda i, j, k: (i, j))
    re_spec = spec((1, bn), lambda i, j, k: (0, j))
    if out_blocked:
        o_spec = spec((None, bm, bn), lambda i, j, k: (j, i, 0))
        o_shape = (gn, M, bn)
    else:
        o_spec = te_spec
        o_shape = (M, N)

    na, nb, nte, nre, no = len(a_list), len(b_list), len(tile_extras), len(row_extras), len(out_dtypes)

    n_scratch_acc = n_acc if gk > 1 else 0
    grid = (gn, gm, gk) if n_outer else (gm, gn, gk)
    cm_in_arrays, cm_out_shapes, cm_scratch, aliases = _comm_operands(comms, na + nb + nte + nre, no)
    n_cm_in, n_cm_out = len(cm_in_arrays), len(cm_out_shapes)

    def body(*refs):
        a_refs = refs[:na]
        b_refs = refs[na:na + nb]
        te_refs = refs[na + nb:na + nb + nte]
        re_refs = refs[na + nb + nte:na + nb + nte + nre]
        n_in = na + nb + nte + nre
        o_refs = refs[n_in + n_cm_in:n_in + n_cm_in + no]
        acc_refs = refs[n_in + n_cm_in + no + n_cm_out:n_in + n_cm_in + no + n_cm_out + n_scratch_acc]
        hosted = _comm_refs(comms, refs[n_in:n_in + n_cm_in], refs[n_in + n_cm_in + no:n_in + n_cm_in + no + n_cm_out],
                            refs[n_in + n_cm_in + no + n_cm_out + n_scratch_acc:])
        _comm_first_step(comms, hosted, grid)
        _comm_mid_step(comms, hosted, grid)

        def finish(accs):
            slab = min(bm, EPILOGUE_ROWS)
            res = [r[...] for r in re_refs]
            for r0 in range(0, bm, slab):
                rows = slice(r0, r0 + slab)
                outs = epilogue([a[rows] for a in accs], [r[rows, :] for r in te_refs], res)
                for o_ref, o in zip(o_refs, outs):
                    o_ref[rows, :] = o.astype(o_ref.dtype)

        def products():
            parts = [None] * n_acc
            for ai, bi, ci in pairs:
                d = lax.dot_general(a_refs[ai][...], b_refs[bi][...], dims, preferred_element_type=F32)
                parts[ci] = d if parts[ci] is None else parts[ci] + d
            return parts

        if gk == 1:
            finish(products())
        else:
            kk = pl.program_id(2)

            @pl.when(kk == 0)
            def _():
                for r, part in zip(acc_refs, products()):
                    r[...] = part

            if gk > 2:
                @pl.when((kk > 0) & (kk < gk - 1))
                def _():
                    for r, part in zip(acc_refs, products()):
                        r[...] += part

            @pl.when(kk == gk - 1)
            def _():
                finish([r[...] + part for r, part in zip(acc_refs, products())])

        _comm_last_step(comms, hosted, grid)

    isz = lambda a: jnp.dtype(a.dtype).itemsize
    est = 2 * (sum(bm * bk * isz(a) for a in a_list) + sum(bk * bn * isz(b) for b in b_list)
               + sum(bm * bn * isz(t) for t in tile_extras)
               + sum(bm * bn * jnp.dtype(d).itemsize for d in out_dtypes))
    est += (2 * n_acc + 2) * bm * bn * 4
    outs = pl.pallas_call(
        body, name=name, grid=grid,
        in_specs=[a_spec] * na + [b_spec] * nb + [te_spec] * nte + [re_spec] * nre + [ANY] * len(cm_in_arrays),
        out_specs=[o_spec] * no + [ANY] * len(cm_out_shapes),
        out_shape=[jax.ShapeDtypeStruct(o_shape, d) for d in out_dtypes] + cm_out_shapes,
        scratch_shapes=[pltpu.VMEM((bm, bn), F32)] * n_scratch_acc + cm_scratch,
        input_output_aliases=aliases,
        compiler_params=pltpu.CompilerParams(dimension_semantics=("parallel", "parallel", "arbitrary"),
                                             vmem_limit_bytes=_vmem_limit(est)),
    )(*a_list, *b_list, *tile_extras, *row_extras, *cm_in_arrays)
    _deliver(comms, outs[no:])
    return outs[:no]


def _rowwise(name, fn, br, row_ins, vec_ins, row_outs, acc_outs=()):
    S = row_ins[0].shape[0]
    assert S % br == 0, (name, S, br)
    nr, nv, no, nacc = len(row_ins), len(vec_ins), len(row_outs), len(acc_outs)

    def body(*refs):
        rin = refs[:nr]
        vin = refs[nr:nr + nv]
        rout = refs[nr + nv:nr + nv + no]
        aout = refs[nr + nv + no:]
        if nacc:
            @pl.when(pl.program_id(0) == 0)
            def _():
                for a in aout:
                    a[...] = jnp.zeros_like(a)
        fn(rin, vin, rout, aout)

    def full_spec(shape):
        nd = len(shape)
        return pl.BlockSpec(shape, lambda i: (0,) * nd)

    est = 2 * (sum(br * a.shape[1] * jnp.dtype(a.dtype).itemsize for a in row_ins)
               + sum(br * c * jnp.dtype(d).itemsize for c, d in row_outs))
    est += 6 * br * max(a.shape[1] for a in row_ins) * 4
    return pl.pallas_call(
        body, name=name, grid=(S // br,),
        in_specs=[pl.BlockSpec((br, a.shape[1]), lambda i: (i, 0)) for a in row_ins]
        + [full_spec(v.shape) for v in vec_ins],
        out_specs=[pl.BlockSpec((br, c), lambda i: (i, 0)) for c, _ in row_outs]
        + [full_spec(s) for s, _ in acc_outs],
        out_shape=[jax.ShapeDtypeStruct((S, c), d) for c, d in row_outs]
        + [jax.ShapeDtypeStruct(s, d) for s, d in acc_outs],
        compiler_params=pltpu.CompilerParams(dimension_semantics=("arbitrary",),
                                             vmem_limit_bytes=_vmem_limit(est)),
    )(*row_ins, *vec_ins)


def _ln_fwd(name, r, g, b):
    def fn(rin, vin, rout, aout):
        xhat, _ = _ln_stats(rin[0][...])
        h = xhat * vin[0][...] + vin[1][...]
        rout[0][...] = h
        rout[1][...] = h.astype(BF16)
    D = r.shape[1]
    return _rowwise(name, fn, _rows(r.shape[0], D * 4), [r], [g, b], [(D, F32), (D, BF16)])


def _ln_loss_bwd(name, r, target, g, b, sub_scale):
    D = r.shape[1]

    def fn(rin, vin, rout, aout):
        xhat, rstd = _ln_stats(rin[0][...])
        gain = vin[0][...]
        err = xhat * gain + vin[1][...] - rin[1][...]
        dhv = err * (1.0 / D)
        dxh = dhv * gain
        m1 = jnp.mean(dxh, axis=-1, keepdims=True)
        m2 = jnp.mean(dxh * xhat, axis=-1, keepdims=True)
        dr = rstd * (dxh - m1 - xhat * m2)
        rout[0][...] = ALPHA * dr
        rout[1][...] = (sub_scale * dr).astype(BF16)
        aout[0][...] += jnp.sum(dhv * xhat, axis=0, keepdims=True)
        aout[1][...] += jnp.sum(dhv, axis=0, keepdims=True)
        aout[2][...] += jnp.sum(err * err, axis=0, keepdims=True)
    return _rowwise(name, fn, _rows(r.shape[0], D * 4, 2 * 2 ** 20), [r, target], [g, b],
                    [(D, F32), (D, BF16)], [((1, D), F32)] * 3)


def _ln_bwd(name, r, dh, g, sub_scale):
    D = r.shape[1]

    def fn(rin, vin, rout, aout):
        xhat, rstd = _ln_stats(rin[0][...])
        dhv = rin[1][...]
        dxh = dhv * vin[0][...]
        m1 = jnp.mean(dxh, axis=-1, keepdims=True)
        m2 = jnp.mean(dxh * xhat, axis=-1, keepdims=True)
        dr = rstd * (dxh - m1 - xhat * m2)
        rout[0][...] = ALPHA * dr
        rout[1][...] = (sub_scale * dr).astype(BF16)
        aout[0][...] += jnp.sum(dhv * xhat, axis=0, keepdims=True)
        aout[1][...] += jnp.sum(dhv, axis=0, keepdims=True)
        aout[2][...] += jnp.sum(dr, axis=0, keepdims=True)
    return _rowwise(name, fn, _rows(r.shape[0], D * 4, 2 * 2 ** 20), [r, dh], [g],
                    [(D, F32), (D, BF16)], [((1, D), F32)] * 3)


def _adamw_math(w, g, m, v):
    m2 = ADAM_B1 * m + (1.0 - ADAM_B1) * g
    v2 = ADAM_B2 * v + (1.0 - ADAM_B2) * (g * g)
    m_hat = m2 / (1.0 - ADAM_B1 ** ADAM_STEP)
    v_hat = v2 / (1.0 - ADAM_B2 ** ADAM_STEP)
    delta = -ADAM_LR * (m_hat / (jnp.sqrt(v_hat) + ADAM_EPS) + ADAM_WD * w)
    return delta, m2, v2


def _adamw(name, w, g, m, v):
    def fn(rin, vin, rout, aout):
        gv = rin[1][...]
        d, m2, v2 = _adamw_math(rin[0][...], gv, rin[2][...], rin[3][...])
        rout[0][...] = gv
        rout[1][...] = d
        rout[2][...] = m2
        rout[3][...] = v2
    C = w.shape[1]
    return _rowwise(name, fn, _rows(w.shape[0], C * 4, 2 * 2 ** 20), [w, g, m, v], [], [(C, F32)] * 4)


def _conv_shifted(hc, p1, p2, rowi):
    r1 = jnp.where(rowi == 0, p1, pltpu.roll(hc, 1, 0))
    r2 = jnp.where(rowi == 0, p2, jnp.where(rowi == 1, p1, pltpu.roll(hc, 2, 0)))
    return r1, r2


def _mix_fwd(z, ws, bsb, gv, bv, cw8, tm):
    S, DIN = z.shape
    DA = gv.shape[1]
    DB = DA
    H = DA // HEAD
    o = 2 * DA
    nch = tm // HEAD

    def body(z_ref, zp_ref, ws_ref, bsb_ref, gv_ref, bv_ref, cw_ref, y_ref):
        i = pl.program_id(0)
        ri = lax.broadcasted_iota(jnp.int32, (HEAD, HEAD), 0)
        ci = lax.broadcasted_iota(jnp.int32, (HEAD, HEAD), 1)
        tril = ri >= ci
        for h in range(H):
            cs = slice(h * HEAD, (h + 1) * HEAD)
            vs = slice(DA + h * HEAD, DA + (h + 1) * HEAD)
            wb = jnp.where(tril, ws_ref[h], 0.0).astype(BF16)
            for n in range(nch):
                rs = slice(n * HEAD, (n + 1) * HEAD)
                u = _gelu(z_ref[rs, cs])
                vhat, _ = _ln_stats(_gelu(z_ref[rs, vs]))
                vn = vhat * gv_ref[:, cs] + bv_ref[:, cs]
                mixed = jnp.dot(wb, vn.astype(BF16), preferred_element_type=F32) + bsb_ref[h]
                y_ref[rs, cs] = (u * mixed).astype(BF16)
        keep = (i > 0).astype(F32)
        rowi = lax.broadcasted_iota(jnp.int32, (tm, HEAD), 0)
        for s in range(DB // HEAD):
            ch = slice(s * HEAD, (s + 1) * HEAD)
            cs = slice(o + DB + s * HEAD, o + DB + (s + 1) * HEAD)
            xs = slice(o + 2 * DB + s * HEAD, o + 2 * DB + (s + 1) * HEAD)
            hc = z_ref[:, cs] * z_ref[:, xs]
            p1 = zp_ref[7:8, cs] * zp_ref[7:8, xs] * keep
            p2 = zp_ref[6:7, cs] * zp_ref[6:7, xs] * keep
            r1, r2 = _conv_shifted(hc, p1, p2, rowi)
            cv = cw_ref[0:1, ch] * r2 + cw_ref[1:2, ch] * r1 + cw_ref[2:3, ch] * hc
            y_ref[:, DA + s * HEAD:DA + (s + 1) * HEAD] = (z_ref[:, o + s * HEAD:o + (s + 1) * HEAD] * cv).astype(BF16)

    full3 = lambda i: (0, 0, 0)
    full2 = lambda i: (0, 0)
    est = 2 * (tm * DIN * 4 + tm * (DA + DB) * 2) + 10 * tm * DB * 4
    return pl.pallas_call(
        body, name="mix_fwd", grid=(S // tm,),
        in_specs=[pl.BlockSpec((tm, DIN), lambda i: (i, 0)),
                  pl.BlockSpec((8, DIN), lambda i: (jnp.maximum(i * (tm // 8) - 1, 0), 0)),
                  pl.BlockSpec(ws.shape, full3), pl.BlockSpec(bsb.shape, full3),
                  pl.BlockSpec(gv.shape, full2), pl.BlockSpec(bv.shape, full2), pl.BlockSpec(cw8.shape, full2)],
        out_specs=pl.BlockSpec((tm, DA + DB), lambda i: (i, 0)),
        out_shape=jax.ShapeDtypeStruct((S, DA + DB), BF16),
        compiler_params=pltpu.CompilerParams(dimension_semantics=("arbitrary",),
                                             vmem_limit_bytes=_vmem_limit(est)),
    )(z, z, ws, bsb, gv, bv, cw8)


def _mix_bwd(z, dy, ws, bsb, gv, bv, cw8, tm):
    S, DIN = z.shape
    DA = gv.shape[1]
    DB = DA
    H = DA // HEAD
    o = 2 * DA
    nch = tm // HEAD
    nblk = S // tm

    def body(z_ref, zp_ref, zn_ref, dy_ref, dyn_ref, ws_ref, bsb_ref, gv_ref, bv_ref, cw_ref,
             dz_ref, dws_ref, dbs_ref, dgv_ref, dbv_ref, dcw_ref, dbin_ref):
        i = pl.program_id(0)

        @pl.when(i == 0)
        def _():
            for r in (dws_ref, dbs_ref, dgv_ref, dbv_ref, dcw_ref, dbin_ref):
                r[...] = jnp.zeros_like(r)

        ri = lax.broadcasted_iota(jnp.int32, (HEAD, HEAD), 0)
        ci = lax.broadcasted_iota(jnp.int32, (HEAD, HEAD), 1)
        tril = ri >= ci
        for h in range(H):
            cs = slice(h * HEAD, (h + 1) * HEAD)
            vs = slice(DA + h * HEAD, DA + (h + 1) * HEAD)
            wb = jnp.where(tril, ws_ref[h], 0.0).astype(BF16)
            gvh = gv_ref[:, cs]
            dws_acc = jnp.zeros((HEAD, HEAD), F32)
            dbs_acc = jnp.zeros((HEAD, HEAD), F32)
            dgv_acc = jnp.zeros((1, HEAD), F32)
            dbv_acc = jnp.zeros((1, HEAD), F32)
            dbu_acc = jnp.zeros((1, HEAD), F32)
            dbvv_acc = jnp.zeros((1, HEAD), F32)
            for n in range(nch):
                rs = slice(n * HEAD, (n + 1) * HEAD)
                u, du_dz = _gelu_and_grad(z_ref[rs, cs])
                v, dv_dz = _gelu_and_grad(z_ref[rs, vs])
                vhat, rstd = _ln_stats(v)
                vnb = (vhat * gvh + bv_ref[:, cs]).astype(BF16)
                mixed = jnp.dot(wb, vnb, preferred_element_type=F32) + bsb_ref[h]
                dya = dy_ref[rs, cs]
                dzu = dya * mixed * du_dz
                dmx = dya * u
                dmxb = dmx.astype(BF16)
                dbs_acc += jnp.broadcast_to(jnp.sum(dmx, axis=1, keepdims=True), (HEAD, HEAD))
                dws_acc += lax.dot_general(dmxb, vnb, _DIMS['nt'], preferred_element_type=F32)
                dvn = lax.dot_general(wb, dmxb, _DIMS['tn'], preferred_element_type=F32)
                dgv_acc += jnp.sum(dvn * vhat, axis=0, keepdims=True)
                dbv_acc += jnp.sum(dvn, axis=0, keepdims=True)
                dxh = dvn * gvh
                m1 = jnp.mean(dxh, axis=-1, keepdims=True)
                m2 = jnp.mean(dxh * vhat, axis=-1, keepdims=True)
                dzv = rstd * (dxh - m1 - vhat * m2) * dv_dz
                dz_ref[rs, cs] = dzu.astype(BF16)
                dz_ref[rs, vs] = dzv.astype(BF16)
                dbu_acc += jnp.sum(dzu, axis=0, keepdims=True)
                dbvv_acc += jnp.sum(dzv, axis=0, keepdims=True)
            dws_ref[h] += jnp.where(tril, dws_acc, 0.0)
            dbs_ref[h] += dbs_acc
            dgv_ref[:, cs] += dgv_acc
            dbv_ref[:, cs] += dbv_acc
            dbin_ref[:, cs] += dbu_acc
            dbin_ref[:, vs] += dbvv_acc

        keep = (i > 0).astype(F32)
        more = (i < nblk - 1).astype(F32)
        rowi = lax.broadcasted_iota(jnp.int32, (tm, HEAD), 0)
        for s in range(DB // HEAD):
            ch = slice(s * HEAD, (s + 1) * HEAD)
            bs = slice(o + s * HEAD, o + (s + 1) * HEAD)
            cs = slice(o + DB + s * HEAD, o + DB + (s + 1) * HEAD)
            xs = slice(o + 2 * DB + s * HEAD, o + 2 * DB + (s + 1) * HEAD)
            ys = slice(DA + s * HEAD, DA + (s + 1) * HEAD)
            zb, zc, zx = z_ref[:, bs], z_ref[:, cs], z_ref[:, xs]
            hc = zc * zx
            p1 = zp_ref[7:8, cs] * zp_ref[7:8, xs] * keep
            p2 = zp_ref[6:7, cs] * zp_ref[6:7, xs] * keep
            r1, r2 = _conv_shifted(hc, p1, p2, rowi)
            w0, w1, w2 = cw_ref[0:1, ch], cw_ref[1:2, ch], cw_ref[2:3, ch]
            cv = w0 * r2 + w1 * r1 + w2 * hc
            dyb = dy_ref[:, ys]
            dzb = dyb * cv
            dcv = dyb * zb
            n0 = dyn_ref[0:1, ys] * zn_ref[0:1, bs] * more
            n1 = dyn_ref[1:2, ys] * zn_ref[1:2, bs] * more
            f1 = jnp.where(rowi == tm - 1, n0, pltpu.roll(dcv, tm - 1, 0))
            f2 = jnp.where(rowi == tm - 1, n1, jnp.where(rowi == tm - 2, n0, pltpu.roll(dcv, tm - 2, 0)))
            dhc = w2 * dcv + w1 * f1 + w0 * f2
            dzc = dhc * zx
            dzx = dhc * zc
            dcw_ref[0:1, ch] += jnp.sum(dcv * r2, axis=0, keepdims=True)
            dcw_ref[1:2, ch] += jnp.sum(dcv * r1, axis=0, keepdims=True)
            dcw_ref[2:3, ch] += jnp.sum(dcv * hc, axis=0, keepdims=True)
            dz_ref[:, bs] = dzb.astype(BF16)
            dz_ref[:, cs] = dzc.astype(BF16)
            dz_ref[:, xs] = dzx.astype(BF16)
            dbin_ref[:, bs] += jnp.sum(dzb, axis=0, keepdims=True)
            dbin_ref[:, cs] += jnp.sum(dzc, axis=0, keepdims=True)
            dbin_ref[:, xs] += jnp.sum(dzx, axis=0, keepdims=True)

    full3 = lambda i: (0, 0, 0)
    full2 = lambda i: (0, 0)
    prev8 = lambda i: (jnp.maximum(i * (tm // 8) - 1, 0), 0)
    next8 = lambda i: (jnp.minimum((i + 1) * (tm // 8), S // 8 - 1), 0)
    est = 2 * (tm * DIN * 4 + tm * (DA + DB) * 4 + tm * DIN * 2) + 16 * tm * DB * 4
    return pl.pallas_call(
        body, name="mix_bwd", grid=(nblk,),
        in_specs=[pl.BlockSpec((tm, DIN), lambda i: (i, 0)),
                  pl.BlockSpec((8, DIN), prev8), pl.BlockSpec((8, DIN), next8),
                  pl.BlockSpec((tm, DA + DB), lambda i: (i, 0)), pl.BlockSpec((8, DA + DB), next8),
                  pl.BlockSpec(ws.shape, full3), pl.BlockSpec(bsb.shape, full3),
                  pl.BlockSpec(gv.shape, full2), pl.BlockSpec(bv.shape, full2), pl.BlockSpec(cw8.shape, full2)],
        out_specs=[pl.BlockSpec((tm, DIN), lambda i: (i, 0)),
                   pl.BlockSpec(ws.shape, full3), pl.BlockSpec(ws.shape, full3),
                   pl.BlockSpec(gv.shape, full2), pl.BlockSpec(gv.shape, full2),
                   pl.BlockSpec(cw8.shape, full2), pl.BlockSpec((1, DIN), full2)],
        out_shape=[jax.ShapeDtypeStruct((S, DIN), BF16),
                   jax.ShapeDtypeStruct(ws.shape, F32), jax.ShapeDtypeStruct(ws.shape, F32),
                   jax.ShapeDtypeStruct(gv.shape, F32), jax.ShapeDtypeStruct(gv.shape, F32),
                   jax.ShapeDtypeStruct(cw8.shape, F32), jax.ShapeDtypeStruct((1, DIN), F32)],
        compiler_params=pltpu.CompilerParams(dimension_semantics=("arbitrary",),
                                             vmem_limit_bytes=_vmem_limit(est)),
    )(z, z, z, dy, dy, ws, bsb, gv, bv, cw8)


def _place():
    x, y, c = lax.axis_index("x"), lax.axis_index("y"), lax.axis_index("c")
    others = [(1 - x, y), (x, 1 - y), (1 - x, 1 - y)]
    return x, y, c, 2 * x + y, others, [2 * ox + oy for ox, oy in others]


def _half(c, rows_half):
    return pl.ds(pl.multiple_of(c * rows_half, 16), rows_half)


def _rcopy(src, dst, ssem, rsem, dev):
    return pltpu.make_async_remote_copy(src_ref=src, dst_ref=dst, send_sem=ssem, recv_sem=rsem,
                                        device_id=dev, device_id_type=MESH)


class _Comm:
    def __init__(self, aliased, ins, fresh, n_sems, stages, sink, relayed=False):
        self.aliased, self.ins, self.fresh = list(aliased), list(ins), list(fresh)
        self.n_sems, self.stages, self.sink, self.relayed = n_sems, stages, sink, relayed


def _in_hbm(a):
    return pltpu.with_memory_space_constraint(a, pltpu.HBM)


def _comm_operands(comms, n_in, n_out):
    arrays, shapes, scratch, aliases = [], [], [], {}
    for cm in comms:
        for a in cm.aliased:
            aliases[n_in + len(arrays)] = n_out + len(shapes)
            arrays.append(_in_hbm(a))
            shapes.append(pltpu.HBM(a.shape, a.dtype))
        arrays += [_in_hbm(a) for a in cm.ins]
        shapes += [pltpu.HBM(s.shape, s.dtype) for s in cm.fresh]
        scratch += [pltpu.SemaphoreType.DMA((cm.n_sems,))] * 2
    return arrays, shapes, scratch, aliases


def _comm_refs(comms, in_refs, out_refs, sem_refs):
    per, pi, po = [], 0, 0
    for i, cm in enumerate(comms):
        pi += len(cm.aliased)
        ins = in_refs[pi:pi + len(cm.ins)]
        pi += len(cm.ins)
        al = out_refs[po:po + len(cm.aliased)]
        po += len(cm.aliased)
        fr = out_refs[po:po + len(cm.fresh)]
        po += len(cm.fresh)
        per.append((al, ins, fr, sem_refs[2 * i], sem_refs[2 * i + 1]))
    return per


def _comm_first_step(comms, hosted, grid):
    if not comms:
        return
    at_first = functools.reduce(lambda a, b: a & b, [pl.program_id(d) == 0 for d in range(len(grid))])

    @pl.when(at_first)
    def _():
        for cm, refs in zip(comms, hosted):
            for d in cm.stages[0](*refs, False):
                d.start()


def _comm_mid_step(comms, hosted, grid):
    relayed = [(cm, refs) for cm, refs in zip(comms, hosted) if cm.relayed]
    if not relayed:
        return
    n_steps = math.prod(grid)
    assert n_steps >= 3, grid
    mid, at_mid = (2 * n_steps) // 3, None
    for d in reversed(range(len(grid))):
        here = pl.program_id(d) == mid % grid[d]
        at_mid = here if at_mid is None else at_mid & here
        mid //= grid[d]

    @pl.when(at_mid)
    def _():
        for cm, refs in relayed:
            for d in cm.stages[0](*refs, True):
                d.wait()
            for d in cm.stages[1](*refs, False):
                d.start()


def _comm_last_step(comms, hosted, grid):
    if not comms:
        return
    at_last = functools.reduce(lambda a, b: a & b, [pl.program_id(d) == g - 1 for d, g in enumerate(grid)])

    @pl.when(at_last)
    def _():
        for cm, refs in zip(comms, hosted):
            for d in cm.stages[1 if cm.relayed else 0](*refs, True):
                d.wait()
        for cm, refs in zip(comms, hosted):
            for stage in cm.stages[2 if cm.relayed else 1:]:
                for d in stage(*refs, False):
                    d.start()
                for d in stage(*refs, True):
                    d.wait()


def _deliver(comms, outs):
    pos = 0
    for cm in comms:
        al = outs[pos:pos + len(cm.aliased)]
        pos += len(cm.aliased)
        fr = outs[pos:pos + len(cm.fresh)]
        pos += len(cm.fresh)
        cm.sink(list(al), list(fr))


def _comm_call(name, comms):
    in_arrays, out_shapes, scratch, aliases = _comm_operands(comms, 0, 0)
    n_in, n_out = len(in_arrays), len(out_shapes)

    def body(*refs):
        hosted = _comm_refs(comms, refs[:n_in], refs[n_in:n_in + n_out], refs[n_in + n_out:])
        for cm, r in zip(comms, hosted):
            for stage in cm.stages:
                for d in stage(*r, False):
                    d.start()
                for d in stage(*r, True):
                    d.wait()

    outs = pl.pallas_call(body, name=name, in_specs=[ANY] * n_in, out_specs=[ANY] * n_out,
                          out_shape=out_shapes, scratch_shapes=scratch, input_output_aliases=aliases)(*in_arrays)
    _deliver(comms, outs)


def _c_gather(fulls, sink, ici, fwd, conv=None):
    n = len(fulls)
    n_direct = 2 * n + (3 if conv is not None else 0)
    n_ici = n_direct + n if ici else 0

    def stage_direct(al, ins, fr, ssem, rsem, wait):
        x, y, c, chip, others, ochips = _place()
        ds = []
        for m in range(n):
            ah = al[m].shape[1] // 2
            mine = al[m].at[chip, _half(c, ah)]
            for k in range(2):
                i = 2 * m + k
                ds.append(_rcopy(mine, al[m].at[ochips[k], _half(c, ah)], ssem.at[i], rsem.at[i], (x, y, c)) if wait
                          else _rcopy(mine, mine, ssem.at[i], rsem.at[i], (*others[k], c)))
        if conv is not None:
            mine = al[n].at[chip]
            for k in range(3):
                i = 2 * n + k
                ds.append(_rcopy(mine, al[n].at[ochips[k]], ssem.at[i], rsem.at[i], (x, y, c)) if wait
                          else _rcopy(mine, mine, ssem.at[i], rsem.at[i], (*others[k], c)))
        return ds

    def stage_relay(al, ins, fr, ssem, rsem, wait):
        x, y, c, chip, others, ochips = _place()
        got_from = ochips[1] + (ochips[0] - ochips[1]) * c
        send_to = (x + (1 - 2 * x) * (1 - c), y + (1 - 2 * y) * c, c)
        ds = []
        for m in range(n):
            ah = al[m].shape[1] // 2
            i = n_direct + m
            got = al[m].at[got_from, _half(c, ah)]
            ds.append(_rcopy(got, al[m].at[ochips[2], _half(c, ah)], ssem.at[i], rsem.at[i], (x, y, c)) if wait
                      else _rcopy(got, got, ssem.at[i], rsem.at[i], send_to))
        return ds

    def stage_fwd(al, ins, fr, ssem, rsem, wait):
        x, y, c, chip, others, ochips = _place()
        ds = []
        for m in range(n):
            ah = al[m].shape[1] // 2
            for k in range(3):
                i = n_ici + 3 * m + k
                got = al[m].at[ochips[k], _half(c, ah)]
                ds.append(_rcopy(got, al[m].at[ochips[k], _half(1 - c, ah)], ssem.at[i], rsem.at[i], (x, y, c)) if wait
                          else _rcopy(got, got, ssem.at[i], rsem.at[i], (x, y, 1 - c)))
        return ds

    stages = ([stage_direct, stage_relay] if ici else []) + ([stage_fwd] if fwd else [])
    return _Comm(list(fulls) + ([conv] if conv is not None else []), [], [], n_ici + (3 * n if fwd else 0),
                 stages, sink, relayed=ici)


def _c_rs_pair(parts, sink):
    def stage(al, ins, fr, ssem, rsem, wait):
        x, y, c, _, _, _ = _place()
        return [_rcopy(ins[m].at[:, _half(1 - c, ins[m].shape[1] // 2), :], fr[m], ssem.at[m], rsem.at[m],
                       (x, y, 1 - c)) for m in range(len(ins))]
    fresh = [jax.ShapeDtypeStruct((p.shape[0], p.shape[1] // 2, p.shape[2]), p.dtype) for p in parts]
    return _Comm([], parts, fresh, len(parts), [stage], sink)


def _c_rs_chips(psums, sink):
    def stage(al, ins, fr, ssem, rsem, wait):
        x, y, c, _, others, ochips = _place()
        return [_rcopy(ins[m].at[ochips[k]], fr[m].at[k], ssem.at[3 * m + k], rsem.at[3 * m + k], (*others[k], c))
                for m in range(len(ins)) for k in range(3)]
    fresh = [jax.ShapeDtypeStruct((3,) + p.shape[1:], p.dtype) for p in psums]
    return _Comm([], psums, fresh, 3 * len(psums), [stage], sink)


def _c_exchange(g2s, sink):
    def stage(al, ins, fr, ssem, rsem, wait):
        x, y, c, _, _, _ = _place()
        if wait:
            return [_rcopy(al[m].at[c], al[m].at[1 - c], ssem.at[m], rsem.at[m], (x, y, c)) for m in range(len(al))]
        return [_rcopy(al[m].at[c], al[m].at[c], ssem.at[m], rsem.at[m], (x, y, 1 - c)) for m in range(len(al))]
    return _Comm(g2s, [], [], len(g2s), [stage], sink)


def _c_allgather_small(buf, sink):
    def stage(al, ins, fr, ssem, rsem, wait):
        x, y, c, _, _, _ = _place()
        ds = []
        for r in range(1, 8):
            peer = (1 - x if r & 4 else x, 1 - y if r & 2 else y, 1 - c if r & 1 else c)
            ds.append(_rcopy(al[0].at[0], al[0].at[r], ssem.at[r - 1], rsem.at[r - 1], peer))
        return ds
    return _Comm([buf], [], [], 7, [stage], sink)


def _sum_small(buf, dev_idx):
    _, m, n = buf.shape

    def body(me_ref, b_ref, o_ref):
        me = me_ref[0]
        acc = b_ref[me]
        for d in range(1, 8):
            acc = acc + b_ref[jnp.bitwise_xor(me, d)]
        o_ref[...] = acc

    gs = pltpu.PrefetchScalarGridSpec(
        num_scalar_prefetch=1, grid=(1,),
        in_specs=[pl.BlockSpec(buf.shape, lambda i, me: (0, 0, 0))],
        out_specs=pl.BlockSpec((m, n), lambda i, me: (0, 0)))
    return pl.pallas_call(body, name="sum_small", grid_spec=gs,
                          out_shape=jax.ShapeDtypeStruct((m, n), F32))(dev_idx, buf)


def _sum_pair(name, part, sib, c_idx):
    _, A, B = part.shape
    ah = A // 2
    br = _rows(ah, B * 2)

    def body(c_ref, a_ref, b_ref, o_ref):
        o_ref[...] = (a_ref[...].astype(F32) + b_ref[...].astype(F32)).astype(BF16)

    gs = pltpu.PrefetchScalarGridSpec(
        num_scalar_prefetch=1, grid=(N_CHIPS, ah // br),
        in_specs=[pl.BlockSpec((None, None, br, B), lambda j, i, c: (j, c[0], i, 0)),
                  pl.BlockSpec((None, br, B), lambda j, i, c: (j, i, 0))],
        out_specs=pl.BlockSpec((None, br, B), lambda j, i, c: (j, i, 0)))
    return pl.pallas_call(body, name=name, grid_spec=gs,
                          out_shape=pltpu.HBM((N_CHIPS, ah, B), BF16),
                          compiler_params=pltpu.CompilerParams(vmem_limit_bytes=_vmem_limit(2 * 3 * br * B * 2)),
                          )(c_idx, _in_hbm(part.reshape(N_CHIPS, 2, ah, B)), _in_hbm(sib))


def _sum_chips(name, psum, recv, place_idx):
    _, ah, B = psum.shape
    br = _rows(ah, B * 4)

    def body(c_ref, p_ref, r0_ref, r1_ref, r2_ref, o_ref):
        o_ref[...] = ((p_ref[...].astype(F32) + r0_ref[...].astype(F32)) + r1_ref[...].astype(F32)) \
            + r2_ref[...].astype(F32)

    gs = pltpu.PrefetchScalarGridSpec(
        num_scalar_prefetch=1, grid=(ah // br,),
        in_specs=[pl.BlockSpec((None, br, B), lambda i, c: (c[0], i, 0))]
        + [pl.BlockSpec((None, br, B), functools.partial(lambda i, c, k: (k, i, 0), k=k)) for k in range(3)],
        out_specs=pl.BlockSpec((None, br, B), lambda i, c: (c[1], i, 0)))
    return pl.pallas_call(body, name=name, grid_spec=gs,
                          out_shape=pltpu.HBM((2, ah, B), F32),
                          compiler_params=pltpu.CompilerParams(vmem_limit_bytes=_vmem_limit(2 * br * B * (4 * 2 + 4))),
                          )(place_idx, _in_hbm(psum), *[_in_hbm(recv)] * 3)


def _cast_into(name, a, place_idx):
    A, B = a.shape
    br = _rows(A, B * 4)

    def body(c_ref, a_ref, o_ref):
        o_ref[...] = a_ref[...].astype(BF16)

    gs = pltpu.PrefetchScalarGridSpec(
        num_scalar_prefetch=1, grid=(A // br,),
        in_specs=[pl.BlockSpec((br, B), lambda i, c: (i, 0))],
        out_specs=pl.BlockSpec((None, br, B), lambda i, c: (c[0], i, 0)))
    return pl.pallas_call(body, name=name, grid_spec=gs,
                          out_shape=pltpu.HBM((N_CHIPS, A, B), BF16))(place_idx, a)


def _cast_behind(x, shards, place_idx, comms):
    n_steps = 16
    n = len(shards)
    xv = x
    svs = list(shards)

    def pace(v):
        for steps_per_block in (1, 2, 4, 8, 16):
            blocks = n_steps // steps_per_block
            if v.shape[0] % blocks == 0 and (v.shape[0] // blocks) % 16 == 0:
                return steps_per_block
        raise ValueError(v.shape)

    n_in, n_out = 2 + n, 1 + n
    cm_in_arrays, cm_out_shapes, cm_scratch, aliases = _comm_operands(comms, n_in, n_out)
    n_cm_in, n_cm_out = len(cm_in_arrays), len(cm_out_shapes)
    grid = (n_steps,)

    def body(c_ref, x_ref, *rest):
        s_refs = rest[:n]
        xo_ref = rest[n + n_cm_in]
        so_refs = rest[n + n_cm_in + 1:n + n_cm_in + 1 + n]
        o_end = n + n_cm_in + 1 + n
        hosted = _comm_refs(comms, rest[n:n + n_cm_in], rest[o_end:o_end + n_cm_out], rest[o_end + n_cm_out:])
        _comm_first_step(comms, hosted, grid)
        _comm_mid_step(comms, hosted, grid)
        xo_ref[...] = x_ref[...].astype(BF16)
        for s_ref, o_ref in zip(s_refs, so_refs):
            o_ref[...] = s_ref[...].astype(BF16)
        _comm_last_step(comms, hosted, grid)

    def rows(v):
        return v.shape[0] * pace(v) // n_steps

    def in_spec(v):
        return pl.BlockSpec((rows(v), v.shape[1]), functools.partial(lambda i, c, q: (i // q, 0), q=pace(v)))

    def shard_out_spec(v):
        return pl.BlockSpec((None, rows(v), v.shape[1]),
                            functools.partial(lambda i, c, q: (c[0], i // q, 0), q=pace(v)))

    gs = pltpu.PrefetchScalarGridSpec(
        num_scalar_prefetch=1, grid=grid,
        in_specs=[in_spec(v) for v in [xv] + svs] + [ANY] * n_cm_in,
        out_specs=[in_spec(xv)] + [shard_out_spec(v) for v in svs] + [ANY] * n_cm_out,
        scratch_shapes=cm_scratch)
    outs = pl.pallas_call(
        body, name="cast_behind", grid_spec=gs,
        out_shape=[jax.ShapeDtypeStruct(xv.shape, BF16)]
        + [pltpu.HBM((N_CHIPS,) + v.shape, BF16) for v in svs] + cm_out_shapes,
        input_output_aliases=aliases,
        compiler_params=pltpu.CompilerParams(
            dimension_semantics=("arbitrary",),
            vmem_limit_bytes=_vmem_limit(2 * sum(rows(v) * v.shape[1] * 6 for v in [xv] + svs))),
    )(place_idx, xv, *svs, *cm_in_arrays)
    _deliver(comms, outs[n_out:])
    return outs[0].reshape(x.shape), [o.reshape((N_CHIPS,) + s.shape) for o, s in zip(outs[1:n_out], shards)]


def _adamw_small(gsum, gconv, offsets, ws, ms, vs):
    n = len(ws)

    def body(*refs):
        g_ref, gc_ref = refs[0], refs[1]
        w_refs = refs[2:2 + n]
        m_refs = refs[2 + n:2 + 2 * n]
        v_refs = refs[2 + 2 * n:2 + 3 * n]
        outs = refs[2 + 3 * n:]
        for i in range(n):
            rows = w_refs[i].shape[0]
            g = gc_ref[...] if offsets[i] is None else g_ref[offsets[i]:offsets[i] + rows, :]
            d, m2, v2 = _adamw_math(w_refs[i][...], g, m_refs[i][...], v_refs[i][...])
            outs[4 * i][...] = g
            outs[4 * i + 1][...] = d
            outs[4 * i + 2][...] = m2
            outs[4 * i + 3][...] = v2

    vm = pl.BlockSpec(memory_space=pltpu.VMEM)
    out_shape = []
    for w in ws:
        out_shape += [jax.ShapeDtypeStruct(w.shape, F32)] * 4
    return pl.pallas_call(body, name="adamw_small", in_specs=[vm] * (2 + 3 * n), out_specs=[vm] * (4 * n),
                          out_shape=out_shape)(gsum, gconv, *ws, *ms, *vs)


def _ident(accs, tes, res):
    return [accs[0]]


def _ffn_up(name, hb, wg4, wu4, comms=()):
    S, D = hb.shape

    def epi(accs, tes, res):
        g, u = accs
        return [g, u, g * _sigmoid(g) * u]

    return _mm(name, 'nn', [hb], [wg4, wu4], [(0, 0, 0), (0, 1, 1)], 2, epi, [BF16] * 3,
               bm=_blk(S, 512), bn=wg4.shape[2], bk=D, b_blocked=True, n_outer=True, comms=comms)


def _ffn_gate(name, hb, wg4, comms=()):
    S, D = hb.shape
    return _mm(name, 'nn', [hb], [wg4], [(0, 0, 0)], 1, _ident, [BF16],
               bm=_blk(S, 1024), bn=wg4.shape[2], bk=D, b_blocked=True, n_outer=True, comms=comms)[0]


def _ffn_upmul(name, hb, wu4, gb, comms=()):
    S, D = hb.shape

    def epi(accs, tes, res):
        g = tes[0].astype(F32)
        return [accs[0], g * _sigmoid(g) * accs[0]]

    return _mm(name, 'nn', [hb], [wu4], [(0, 0, 0)], 1, epi, [BF16, BF16],
               bm=_blk(S, 1024), bn=wu4.shape[2], bk=D, tile_extras=[gb], b_blocked=True, n_outer=True, comms=comms)


def _ffn_down(name, ab, wd, hres, comms=()):
    S, F = ab.shape
    D = wd.shape[1]
    return _mm(name, 'nn', [ab], [wd], [(0, 0, 0)], 1, lambda accs, tes, res: [ALPHA * tes[0] + 0.5 * accs[0]],
               [F32], bm=_blk(S, 1024), bn=_blk(D, 512), bk=F, tile_extras=[hres], comms=comms)[0]


def _ffn_da(name, dsub, wd, gb, ub, comms=()):
    S, D = dsub.shape
    F = wd.shape[0]

    def epi(accs, tes, res):
        da = accs[0]
        g = tes[0].astype(F32)
        u = tes[1].astype(F32)
        s = _sigmoid(g)
        silu = g * s
        return [(da * u) * (s + silu * (1.0 - s)), da * silu]

    return _mm(name, 'nt', [dsub], [wd], [(0, 0, 0)], 1, epi, [BF16, BF16],
               bm=_blk(S, 1024), bn=_blk(F, 512), bk=D, tile_extras=[gb, ub], comms=comms)


def _dw_rows(name, ab, dsub, cs, comms=()):
    S, D = dsub.shape
    return _mm(name, 'tn', [ab], [dsub], [(0, 0, 0)], 1, _ident, [BF16],
               bm=cs, bn=_blk(D, 1024), bk=_blk(S, 2048), comms=comms)[0].reshape(N_CHIPS, cs, D)


def _dw_cols(name, hb, dxb, cs, comms=()):
    S, D = hb.shape
    return _mm(name, 'tn', [hb], [dxb], [(0, 0, 0)], 1, _ident, [BF16],
               bm=_blk(D, 1024), bn=cs, bk=_blk(S, 2048), out_blocked=True, comms=comms)[0]


def _ffn_dh(name, dgb, dub, wg4, wu4, dres, comms=()):
    S = dgb.shape[0]
    D = wg4.shape[1]
    return _mm(name, 'nt', [dgb, dub], [wg4, wu4], [(0, 0, 0), (1, 1, 0)], 1,
               lambda accs, tes, res: [accs[0] + tes[0]], [F32],
               bm=_blk(S, 1024), bn=_blk(D, 1024), bk=wg4.shape[2], tile_extras=[dres], b_blocked=True,
               comms=comms)[0]


def _step(p):
    x = p['x'][0]
    target = p['loss_target'][0]
    S, D = x.shape
    cx, cy, cc = lax.axis_index("x"), lax.axis_index("y"), lax.axis_index("c")
    chip = 2 * cx + cy
    c_idx = jnp.reshape(cc, (1,)).astype(jnp.int32)
    place_idx = jnp.stack([chip, cc]).astype(jnp.int32)

    w_s = p['w_s'][0]
    H = w_s.shape[0]
    DA = H * HEAD
    bsb = jnp.broadcast_to(p['b_s'][0][:, :, None], (H, HEAD, HEAD))
    conv_shard = p['conv_w'][0]
    conv8 = jnp.zeros((8, conv_shard.shape[1]), F32).at[:CONV_TAPS].set(conv_shard)

    W = {'ffa_gate': _cast_into("cast_ffa_gate", p['ffa_gate'][0], place_idx)}
    conv = {'full': lax.dynamic_update_slice(jnp.zeros((N_CHIPS,) + conv8.shape, F32), conv8[None], (chip, 0, 0))}

    def gathered(names, with_conv=False):
        def sink(al, fr):
            W.update(zip(names, al))
            if with_conv:
                conv['full'] = al[len(names)]
        return sink

    def c_gather(names, ici, fwd, with_conv=False):
        return _c_gather([W[n] for n in names], gathered(names, with_conv), ici, fwd,
                         conv['full'] if with_conv else None)

    rest = [n for n in BIG if n != 'ffa_gate']
    xb, casted = _cast_behind(x, [p[n][0] for n in rest], place_idx, comms=[c_gather(['ffa_gate'], True, True)])
    W.update(zip(rest, casted))
    cs_ff = W['ffa_gate'].shape[2]
    F = N_CHIPS * cs_ff

    ga = _ffn_gate("ffa_gate", xb, W['ffa_gate'],
                   comms=[c_gather(['ffa_up'], True, True), c_gather(['ffa_down'], True, False)])
    ua, aa = _ffn_upmul("ffa_up", xb, W['ffa_up'], ga,
                        comms=[c_gather(['ffa_down'], False, True),
                               c_gather(['w_in', 'ffc_gate'], True, False, with_conv=True)])
    cw8 = jnp.transpose(conv['full'], (1, 0, 2)).reshape(8, DA)
    wd_a = W['ffa_down'].reshape(F, D)
    r1 = _ffn_down("ffa_down", aa, wd_a, x,
                   comms=[c_gather(['w_in', 'ffc_gate'], False, True), c_gather(['w_out', 'ffc_down'], True, False)])
    h1, h1b = _ln_fwd("ln_a", r1, p['ln_a_g'], p['ln_a_b'])
    w_in4 = W['w_in']
    cs_in = w_in4.shape[2]
    z = _mm("w_in", 'nn', [h1b], [w_in4], [(0, 0, 0)], 1, lambda accs, tes, res: [accs[0] + res[0]], [F32],
            bm=_blk(S, 1024), bn=cs_in, bk=D, row_extras=[p['b_in']], b_blocked=True, n_outer=True,
            comms=[c_gather(['w_out', 'ffc_down'], False, True), c_gather(['ffc_up'], True, False)])[0]
    w_out = W['w_out'].reshape(DA * 2, D)
    yb = _mix_fwd(z, w_s, bsb, p['ln_v_g'], p['ln_v_b'], cw8, _blk(S, 2 * HEAD))
    r2 = _mm("w_out", 'nn', [yb], [w_out], [(0, 0, 0)], 1,
             lambda accs, tes, res: [accs[0] + res[0] + ALPHA * tes[0]], [F32],
             bm=_blk(S, 1024), bn=_blk(D, 1024), bk=2 * DA, tile_extras=[h1], row_extras=[p['b_out']],
             comms=[c_gather(['ffc_up'], False, True)])[0]
    h2, h2b = _ln_fwd("ln_m", r2, p['ln_m_g'], p['ln_m_b'])
    gc, uc, ac = _ffn_up("ffc_up", h2b, W['ffc_gate'], W['ffc_up'])
    wd_c = W['ffc_down'].reshape(F, D)
    r3 = _ffn_down("ffc_down", ac, wd_c, h2)

    pg, sg, sib, psum, recv, g2 = {}, {}, {}, {}, {}, {}

    def c_pair(names):
        return _c_rs_pair([pg[n] for n in names], lambda al, fr: sib.update(zip(names, fr)))

    def c_chips(names):
        return _c_rs_chips([psum[n] for n in names], lambda al, fr: recv.update(zip(names, fr)))

    def c_swap(names):
        return _c_exchange([g2[n] for n in names], lambda al, fr: g2.update(zip(names, al)))

    def sum_pair(*names):
        for n in names:
            psum[n] = _sum_pair(f"sum_pair_{n}", pg[n], sib[n], c_idx)

    def sum_chips(*names):
        for n in names:
            g2[n] = _sum_chips(f"sum_chips_{n}", psum[n], recv[n], place_idx)

    dres3, dsub3, sg['ln_c_g'], sg['ln_c_b'], lsum = _ln_loss_bwd(
        "ln_c_loss_bwd", r3, target, p['ln_c_g'], p['ln_c_b'], 0.5)

    dgc, duc = _ffn_da("ffc_da", dsub3, wd_c, gc, uc)
    pg['ffc_down'] = _dw_rows("ffc_dwd", ac, dsub3, cs_ff)
    pg['ffc_gate'] = _dw_cols("ffc_dwg", h2b, dgc, cs_ff)
    pg['ffc_up'] = _dw_cols("ffc_dwu", h2b, duc, cs_ff, comms=[c_pair(['ffc_down', 'ffc_gate'])])
    sum_pair('ffc_down', 'ffc_gate')
    dh2 = _ffn_dh("ffc_dh", dgc, duc, W['ffc_gate'], W['ffc_up'], dres3,
                  comms=[c_pair(['ffc_up']), c_chips(['ffc_down']), c_chips(['ffc_gate'])])
    sum_pair('ffc_up')
    sum_chips('ffc_down', 'ffc_gate')

    dres2, dmix, sg['ln_m_g'], sg['ln_m_b'], sg['b_out'] = _ln_bwd("ln_m_bwd", r2, dh2, p['ln_m_g'], 1.0)
    pg['w_out'] = _mm("dw_out", 'tn', [yb], [dmix], [(0, 0, 0)], 1, _ident, [BF16],
                      bm=_blk(2 * DA, 1024), bn=_blk(D, 1024),
                      bk=_blk(S, 2048))[0].reshape(N_CHIPS, 2 * DA // 4, D)
    dy = _mm("dy", 'nt', [dmix], [w_out], [(0, 0, 0)], 1, _ident, [F32],
             bm=_blk(S, 1024), bn=_blk(2 * DA, 1024), bk=D)[0]
    dz, dws, dbs, sg['ln_v_g'], sg['ln_v_b'], dcw, sg['b_in'] = _mix_bwd(
        z, dy, w_s, bsb, p['ln_v_g'], p['ln_v_b'], cw8, HEAD)
    sg['w_s'] = dws
    sg['b_s'] = dbs[:, :, 0]
    pg['w_in'] = _dw_cols("dw_in", h1b, dz, cs_in,
                          comms=[c_chips(['ffc_up']), c_pair(['w_out']), c_swap(['ffc_down', 'ffc_gate'])])
    sum_chips('ffc_up')
    sum_pair('w_out')
    dh1 = _mm("dh1", 'nt', [dz], [w_in4], [(0, 0, 0)], 1, lambda accs, tes, res: [accs[0] + tes[0]], [F32],
              bm=_blk(S, 1024), bn=_blk(D, 1024), bk=cs_in, tile_extras=[dres2], b_blocked=True,
              comms=[c_pair(['w_in']), c_chips(['w_out']), c_swap(['ffc_up'])])[0]
    sum_pair('w_in')
    sum_chips('w_out')

    dres1, dsub1, sg['ln_a_g'], sg['ln_a_b'], _ = _ln_bwd("ln_a_bwd", r1, dh1, p['ln_a_g'], 0.5)
    def as_rows(a):
        a = a.reshape(-1, 128)
        pad = (-a.shape[0]) % 8
        return jnp.pad(a, ((0, pad), (0, 0))) if pad else a

    pieces, offsets, off = [], {}, 0
    for n in SMALL + ['conv_w']:
        piece = as_rows(dcw[:CONV_TAPS] if n == 'conv_w' else sg[n])
        offsets[n] = off
        off += piece.shape[0]
        pieces.append(piece)
    packed = jnp.concatenate(pieces, axis=0)
    small = {'buf': jnp.zeros((8,) + packed.shape, F32).at[0].set(packed)}
    dga, dua = _ffn_da("ffa_da", dsub1, wd_a, ga, ua,
                       comms=[c_chips(['w_in']), c_swap(['w_out']),
                              _c_allgather_small(small['buf'], lambda al, fr: small.update(buf=al[0]))])
    sum_chips('w_in')
    gsum = _sum_small(small['buf'], jnp.reshape(4 * cx + 2 * cy + cc, (1,)).astype(jnp.int32))
    pg['ffa_down'] = _dw_rows("ffa_dwd", aa, dsub1, cs_ff)
    pg['ffa_gate'] = _dw_cols("ffa_dwg", xb, dga, cs_ff)
    pg['ffa_up'] = _dw_cols("ffa_dwu", xb, dua, cs_ff, comms=[c_pair(['ffa_down', 'ffa_gate']), c_swap(['w_in'])])
    sum_pair('ffa_down', 'ffa_gate')
    _comm_call("pair_ffa_up", [c_pair(['ffa_up'])])
    sum_pair('ffa_up')
    dx = _ffn_dh("ffa_dh", dga, dua, W['ffa_gate'], W['ffa_up'], dres1,
                 comms=[c_chips(['ffa_down']), c_chips(['ffa_gate']), c_chips(['ffa_up'])])
    sum_chips('ffa_down', 'ffa_gate', 'ffa_up')
    _comm_call("swap_ffa", [c_swap(['ffa_down', 'ffa_gate', 'ffa_up'])])

    out = {}
    for n in BIG:
        g, d, m2, v2 = _adamw(f"adamw_{n}", p[n][0], g2[n].reshape(p[n][0].shape), p['m_' + n][0], p['v_' + n][0])
        shp = p[n].shape
        out[n] = (g.reshape(shp), d.reshape(shp), m2.reshape(shp), v2.reshape(shp))

    loss = lax.psum(0.5 * jnp.sum(lsum) / D, ("x", "y", "c"))
    n_conv = CONV_TAPS * DA // 128
    conv_sum = gsum[offsets['conv_w']:offsets['conv_w'] + n_conv].reshape(CONV_TAPS, DA)
    cw = conv_shard.shape[1]
    gconv = lax.dynamic_slice_in_dim(conv_sum, (2 * cx + cy) * cw, cw, axis=1).reshape(-1, 128)
    names = SMALL + ['conv_w']
    rows128 = lambda a: a.reshape(-1, 128)
    res = _adamw_small(gsum, gconv, [offsets[n] if n != 'conv_w' else None for n in names],
                       [rows128(p[n]) for n in names], [rows128(p['m_' + n]) for n in names],
                       [rows128(p['v_' + n]) for n in names])
    for i, n in enumerate(names):
        out[n] = tuple(r.reshape(p[n].shape) for r in res[4 * i:4 * i + 4])

    return (loss, dx.reshape(p['x'].shape), *[out[n][0] for n in WEIGHTS], *[out[n][1] for n in WEIGHTS],
            *[out[n][2] for n in WEIGHTS], *[out[n][3] for n in WEIGHTS])


def kernel(x, ffa_gate, ffa_up, ffa_down, ln_a_g, ln_a_b, w_in, b_in, w_s, b_s, ln_v_g, ln_v_b, conv_w, w_out, b_out, ln_m_g, ln_m_b, ffc_gate, ffc_up, ffc_down, ln_c_g, ln_c_b, loss_target, m_ffa_gate, m_ffa_up, m_ffa_down, m_ln_a_g, m_ln_a_b, m_w_in, m_b_in, m_w_s, m_b_s, m_ln_v_g, m_ln_v_b, m_conv_w, m_w_out, m_b_out, m_ln_m_g, m_ln_m_b, m_ffc_gate, m_ffc_up, m_ffc_down, m_ln_c_g, m_ln_c_b, v_ffa_gate, v_ffa_up, v_ffa_down, v_ln_a_g, v_ln_a_b, v_w_in, v_b_in, v_w_s, v_b_s, v_ln_v_g, v_ln_v_b, v_conv_w, v_w_out, v_b_out, v_ln_m_g, v_ln_m_b, v_ffc_gate, v_ffc_up, v_ffc_down, v_ln_c_g, v_ln_c_b):
    return _step(dict(locals()))
```

```python
import functools
import math

import jax
import jax.numpy as jnp
from jax import lax
from jax.experimental import pallas as pl
from jax.experimental.pallas import tpu as pltpu

F32 = jnp.float32
BF16 = jnp.bfloat16
MESH = pl.DeviceIdType.MESH
ANY = pl.BlockSpec(memory_space=pl.ANY)

HEAD = 128
CONV_TAPS = 3
LN_EPS = 1e-5
ALPHA = float(2.0 ** 0.25)
ADAM_LR, ADAM_B1, ADAM_B2, ADAM_EPS, ADAM_WD, ADAM_STEP = 0.001, 0.9, 0.999, 1e-08, 0.01, 10

V7X_VMEM_BYTES = 64 * 2 ** 20
VMEM_COMPILER_RESERVE = 6 * 2 ** 20
VMEM_MIN_REQUEST = 32 * 2 ** 20
VMEM_ESTIMATE_SLACK = 1.3
N_CHIPS = 4
EPILOGUE_ROWS = 128

BIG = ['ffa_gate', 'ffa_up', 'ffa_down', 'w_in', 'w_out', 'ffc_gate', 'ffc_up', 'ffc_down']
WEIGHTS = ['ffa_gate', 'ffa_up', 'ffa_down', 'ln_a_g', 'ln_a_b', 'w_in', 'b_in', 'w_s', 'b_s', 'ln_v_g', 'ln_v_b',
           'conv_w', 'w_out', 'b_out', 'ln_m_g', 'ln_m_b', 'ffc_gate', 'ffc_up', 'ffc_down', 'ln_c_g', 'ln_c_b']
SMALL = [n for n in WEIGHTS if n not in BIG and n != 'conv_w']


def _vmem_limit(estimate_bytes):
    return int(min(max(estimate_bytes * VMEM_ESTIMATE_SLACK, VMEM_MIN_REQUEST),
                   V7X_VMEM_BYTES - VMEM_COMPILER_RESERVE))


def _blk(dim, pref, mult=128):
    if dim <= pref:
        return dim
    best = None
    for d in range(mult, pref + 1, mult):
        if dim % d == 0:
            best = d
    assert best is not None, (dim, pref)
    return best


def _rows(n_rows, row_bytes, target=4 * 2 ** 20):
    return _blk(n_rows, max(16, target // row_bytes), 16)


def _sigmoid(x):
    return 0.5 * jnp.tanh(0.5 * x) + 0.5


_GELU_C = math.sqrt(2.0 / math.pi)
_GELU_A = 0.044715


def _gelu(x):
    return (0.5 * x) * (1.0 + jnp.tanh(x * (_GELU_C + (_GELU_C * _GELU_A) * (x * x))))


def _gelu_and_grad(x):
    x2 = x * x
    t = jnp.tanh(x * (_GELU_C + (_GELU_C * _GELU_A) * x2))
    half_x = 0.5 * x
    one_plus_t = 1.0 + t
    dg = 0.5 * one_plus_t + half_x * (1.0 - t * t) * (_GELU_C + (3.0 * _GELU_C * _GELU_A) * x2)
    return half_x * one_plus_t, dg


def _ln_stats(r):
    mu = jnp.mean(r, axis=-1, keepdims=True)
    d = r - mu
    var = jnp.mean(d * d, axis=-1, keepdims=True)
    rstd = lax.rsqrt(var + LN_EPS)
    return d * rstd, rstd


_DIMS = {'nn': (((1,), (0,)), ((), ())), 'nt': (((1,), (1,)), ((), ())), 'tn': (((0,), (0,)), ((), ()))}


def _mm(name, mode, a_list, b_list, pairs, n_acc, epilogue, out_dtypes, bm, bn, bk,
        tile_extras=(), row_extras=(), b_blocked=False, out_blocked=False, n_outer=False, comms=()):
    a0, b0 = a_list[0], b_list[0]
    if mode == 'nn':
        M, K = a0.shape
        N = b0.shape[0] * b0.shape[2] if b_blocked else b0.shape[1]
    elif mode == 'nt':
        M, K = a0.shape
        N = b0.shape[1] if b_blocked else b0.shape[0]
    else:
        K, M = a0.shape
        N = b0.shape[1]
        assert not b_blocked
    assert M % bm == 0 and N % bn == 0 and K % bk == 0, (name, M, N, K, bm, bn, bk)
    gm, gn, gk = M // bm, N // bn, K // bk
    if b_blocked:
        assert (bn if mode == 'nn' else bk) == b0.shape[2], name
    dims = _DIMS[mode]

    def spec(shape, f):
        if n_outer:
            return pl.BlockSpec(shape, lambda g0, g1, g2: f(g1, g0, g2))
        return pl.BlockSpec(shape, f)

    if mode == 'tn':
        a_spec = spec((bk, bm), lambda i, j, k: (k, i))
    else:
        a_spec = spec((bm, bk), lambda i, j, k: (i, k))
    if mode == 'nn':
        b_spec = (spec((None, bk, bn), lambda i, j, k: (j, k, 0)) if b_blocked
                  else spec((bk, bn), lambda i, j, k: (k, j)))
    elif mode == 'nt':
        b_spec = (spec((None, bn, bk), lambda i, j, k: (k, j, 0)) if b_blocked
                  else spec((bn, bk), lambda i, j, k: (j, k)))
    else:
        b_spec = spec((bk, bn), lambda i, j, k: (k, j))
    te_spec = spec((bm, bn), lambda i, j, k: (i, j))
    re_spec = spec((1, bn), lambda i, j, k: (0, j))
    if out_blocked:
        o_spec = spec((None, bm, bn), lambda i, j, k: (j, i, 0))
        o_shape = (gn, M, bn)
    else:
        o_spec = te_spec
        o_shape = (M, N)

    na, nb, nte, nre, no = len(a_list), len(b_list), len(tile_extras), len(row_extras), len(out_dtypes)

    n_scratch_acc = n_acc if gk > 1 else 0
    grid = (gn, gm, gk) if n_outer else (gm, gn, gk)
    cm_in_arrays, cm_out_shapes, cm_scratch, aliases = _comm_operands(comms, na + nb + nte + nre, no)
    n_cm_in, n_cm_out = len(cm_in_arrays), len(cm_out_shapes)

    def body(*refs):
        a_refs = refs[:na]
        b_refs = refs[na:na + nb]
        te_refs = refs[na + nb:na + nb + nte]
        re_refs = refs[na + nb + nte:na + nb + nte + nre]
        n_in = na + nb + nte + nre
        o_refs = refs[n_in + n_cm_in:n_in + n_cm_in + no]
        acc_refs = refs[n_in + n_cm_in + no + n_cm_out:n_in + n_cm_in + no + n_cm_out + n_scratch_acc]
        hosted = _comm_refs(comms, refs[n_in:n_in + n_cm_in], refs[n_in + n_cm_in + no:n_in + n_cm_in + no + n_cm_out],
                            refs[n_in + n_cm_in + no + n_cm_out + n_scratch_acc:])
        _comm_first_step(comms, hosted, grid)
        _comm_mid_step(comms, hosted, grid)

        def finish(accs):
            slab = min(bm, EPILOGUE_ROWS)
            res = [r[...] for r in re_refs]
            for r0 in range(0, bm, slab):
                rows = slice(r0, r0 + slab)
                outs = epilogue([a[rows] for a in accs], [r[rows, :] for r in te_refs], res)
                for o_ref, o in zip(o_refs, outs):
                    o_ref[rows, :] = o.astype(o_ref.dtype)

        def products():
            parts = [None] * n_acc
            for ai, bi, ci in pairs:
                d = lax.dot_general(a_refs[ai][...], b_refs[bi][...], dims, preferred_element_type=F32)
                parts[ci] = d if parts[ci] is None else parts[ci] + d
            return parts

        if gk == 1:
            finish(products())
        else:
            kk = pl.program_id(2)

            @pl.when(kk == 0)
            def _():
                for r, part in zip(acc_refs, products()):
                    r[...] = part

            if gk > 2:
                @pl.when((kk > 0) & (kk < gk - 1))
                def _():
                    for r, part in zip(acc_refs, products()):
                        r[...] += part

            @pl.when(kk == gk - 1)
            def _():
                finish([r[...] + part for r, part in zip(acc_refs, products())])

        _comm_last_step(comms, hosted, grid)

    isz = lambda a: jnp.dtype(a.dtype).itemsize
    est = 2 * (sum(bm * bk * isz(a) for a in a_list) + sum(bk * bn * isz(b) for b in b_list)
               + sum(bm * bn * isz(t) for t in tile_extras)
               + sum(bm * bn * jnp.dtype(d).itemsize for d in out_dtypes))
    est += (2 * n_acc + 2) * bm * bn * 4
    outs = pl.pallas_call(
        body, name=name, grid=grid,
        in_specs=[a_spec] * na + [b_spec] * nb + [te_spec] * nte + [re_spec] * nre + [ANY] * len(cm_in_arrays),
        out_specs=[o_spec] * no + [ANY] * len(cm_out_shapes),
        out_shape=[jax.ShapeDtypeStruct(o_shape, d) for d in out_dtypes] + cm_out_shapes,
        scratch_shapes=[pltpu.VMEM((bm, bn), F32)] * n_scratch_acc + cm_scratch,
        input_output_aliases=aliases,
        compiler_params=pltpu.CompilerParams(dimension_semantics=("parallel", "parallel", "arbitrary"),
                                             vmem_limit_bytes=_vmem_limit(est)),
    )(*a_list, *b_list, *tile_extras, *row_extras, *cm_in_arrays)
    _deliver(comms, outs[no:])
    return outs[:no]


def _rowwise(name, fn, br, row_ins, vec_ins, row_outs, acc_outs=()):
    S = row_ins[0].shape[0]
    assert S % br == 0, (name, S, br)
    nr, nv, no, nacc = len(row_ins), len(vec_ins), len(row_outs), len(acc_outs)

    def body(*refs):
        rin = refs[:nr]
        vin = refs[nr:nr + nv]
        rout = refs[nr + nv:nr + nv + no]
        aout = refs[nr + nv + no:]
        if nacc:
            @pl.when(pl.program_id(0) == 0)
            def _():
                for a in aout:
                    a[...] = jnp.zeros_like(a)
        fn(rin, vin, rout, aout)

    def full_spec(shape):
        nd = len(shape)
        return pl.BlockSpec(shape, lambda i: (0,) * nd)

    est = 2 * (sum(br * a.shape[1] * jnp.dtype(a.dtype).itemsize for a in row_ins)
               + sum(br * c * jnp.dtype(d).itemsize for c, d in row_outs))
    est += 6 * br * max(a.shape[1] for a in row_ins) * 4
    return pl.pallas_call(
        body, name=name, grid=(S // br,),
        in_specs=[pl.BlockSpec((br, a.shape[1]), lambda i: (i, 0)) for a in row_ins]
        + [full_spec(v.shape) for v in vec_ins],
        out_specs=[pl.BlockSpec((br, c), lambda i: (i, 0)) for c, _ in row_outs]
        + [full_spec(s) for s, _ in acc_outs],
        out_shape=[jax.ShapeDtypeStruct((S, c), d) for c, d in row_outs]
        + [jax.ShapeDtypeStruct(s, d) for s, d in acc_outs],
        compiler_params=pltpu.CompilerParams(dimension_semantics=("arbitrary",),
                                             vmem_limit_bytes=_vmem_limit(est)),
    )(*row_ins, *vec_ins)


def _ln_fwd(name, r, g, b):
    def fn(rin, vin, rout, aout):
        xhat, _ = _ln_stats(rin[0][...])
        h = xhat * vin[0][...] + vin[1][...]
        rout[0][...] = h
        rout[1][...] = h.astype(BF16)
    D = r.shape[1]
    return _rowwise(name, fn, _rows(r.shape[0], D * 4), [r], [g, b], [(D, F32), (D, BF16)])


def _ln_loss_bwd(name, r, target, g, b, sub_scale):
    D = r.shape[1]

    def fn(rin, vin, rout, aout):
        xhat, rstd = _ln_stats(rin[0][...])
        gain = vin[0][...]
        err = xhat * gain + vin[1][...] - rin[1][...]
        dhv = err * (1.0 / D)
        dxh = dhv * gain
        m1 = jnp.mean(dxh, axis=-1, keepdims=True)
        m2 = jnp.mean(dxh * xhat, axis=-1, keepdims=True)
        dr = rstd * (dxh - m1 - xhat * m2)
        rout[0][...] = ALPHA * dr
        rout[1][...] = (sub_scale * dr).astype(BF16)
        aout[0][...] += jnp.sum(dhv * xhat, axis=0, keepdims=True)
        aout[1][...] += jnp.sum(dhv, axis=0, keepdims=True)
        aout[2][...] += jnp.sum(err * err, axis=0, keepdims=True)
    return _rowwise(name, fn, _rows(r.shape[0], D * 4, 2 * 2 ** 20), [r, target], [g, b],
                    [(D, F32), (D, BF16)], [((1, D), F32)] * 3)


def _ln_bwd(name, r, dh, g, sub_scale):
    D = r.shape[1]

    def fn(rin, vin, rout, aout):
        xhat, rstd = _ln_stats(rin[0][...])
        dhv = rin[1][...]
        dxh = dhv * vin[0][...]
        m1 = jnp.mean(dxh, axis=-1, keepdims=True)
        m2 = jnp.mean(dxh * xhat, axis=-1, keepdims=True)
        base = dxh - m1 - xhat * m2
        rout[0][...] = (ALPHA * rstd) * base
        dsub = (sub_scale * rstd) * base
        rout[1][...] = dsub.astype(BF16)
        aout[0][...] += jnp.sum(dhv * xhat, axis=0, keepdims=True)
        aout[1][...] += jnp.sum(dhv, axis=0, keepdims=True)
        if sub_scale == 1.0:
            aout[2][...] += jnp.sum(dsub, axis=0, keepdims=True)
    return _rowwise(name, fn, _rows(r.shape[0], D * 4, 2 * 2 ** 20), [r, dh], [g],
                    [(D, F32), (D, BF16)], [((1, D), F32)] * 3)


def _adamw_math(w, g, m, v):
    m2 = ADAM_B1 * m + (1.0 - ADAM_B1) * g
    v2 = ADAM_B2 * v + (1.0 - ADAM_B2) * (g * g)
    m_hat = m2 / (1.0 - ADAM_B1 ** ADAM_STEP)
    v_hat = v2 / (1.0 - ADAM_B2 ** ADAM_STEP)
    delta = -ADAM_LR * (m_hat / (jnp.sqrt(v_hat) + ADAM_EPS) + ADAM_WD * w)
    return delta, m2, v2


def _adamw(name, w, g, m, v):
    def fn(rin, vin, rout, aout):
        gv = rin[1][...]
        d, m2, v2 = _adamw_math(rin[0][...], gv, rin[2][...], rin[3][...])
        rout[0][...] = gv
        rout[1][...] = d
        rout[2][...] = m2
        rout[3][...] = v2
    C = w.shape[1]
    return _rowwise(name, fn, _rows(w.shape[0], C * 4, 2 * 2 ** 20), [w, g, m, v], [], [(C, F32)] * 4)


def _conv_shifted(hc, p1, p2, rowi):
    r1 = jnp.where(rowi == 0, p1, pltpu.roll(hc, 1, 0))
    r2 = jnp.where(rowi == 0, p2, jnp.where(rowi == 1, p1, pltpu.roll(hc, 2, 0)))
    return r1, r2


def _mix_fwd(z, ws, bsb, gv, bv, cw8, tm):
    S, DIN = z.shape
    DA = gv.shape[1]
    DB = DA
    H = DA // HEAD
    o = 2 * DA
    nch = tm // HEAD

    def body(z_ref, zp_ref, ws_ref, bsb_ref, gv_ref, bv_ref, cw_ref, y_ref):
        i = pl.program_id(0)
        ri = lax.broadcasted_iota(jnp.int32, (HEAD, HEAD), 0)
        ci = lax.broadcasted_iota(jnp.int32, (HEAD, HEAD), 1)
        tril = ri >= ci
        for h in range(H):
            cs = slice(h * HEAD, (h + 1) * HEAD)
            vs = slice(DA + h * HEAD, DA + (h + 1) * HEAD)
            wb = jnp.where(tril, ws_ref[h], 0.0).astype(BF16)
            for n in range(nch):
                rs = slice(n * HEAD, (n + 1) * HEAD)
                u = _gelu(z_ref[rs, cs])
                vhat, _ = _ln_stats(_gelu(z_ref[rs, vs]))
                vn = vhat * gv_ref[:, cs] + bv_ref[:, cs]
                mixed = jnp.dot(wb, vn.astype(BF16), preferred_element_type=F32) + bsb_ref[h]
                y_ref[rs, cs] = (u * mixed).astype(BF16)
        keep = (i > 0).astype(F32)
        rowi = lax.broadcasted_iota(jnp.int32, (tm, HEAD), 0)
        for s in range(DB // HEAD):
            ch = slice(s * HEAD, (s + 1) * HEAD)
            cs = slice(o + DB + s * HEAD, o + DB + (s + 1) * HEAD)
            xs = slice(o + 2 * DB + s * HEAD, o + 2 * DB + (s + 1) * HEAD)
            hc = z_ref[:, cs] * z_ref[:, xs]
            p1 = zp_ref[7:8, cs] * zp_ref[7:8, xs] * keep
            p2 = zp_ref[6:7, cs] * zp_ref[6:7, xs] * keep
            r1, r2 = _conv_shifted(hc, p1, p2, rowi)
            cv = cw_ref[0:1, ch] * r2 + cw_ref[1:2, ch] * r1 + cw_ref[2:3, ch] * hc
            y_ref[:, DA + s * HEAD:DA + (s + 1) * HEAD] = (z_ref[:, o + s * HEAD:o + (s + 1) * HEAD] * cv).astype(BF16)

    full3 = lambda i: (0, 0, 0)
    full2 = lambda i: (0, 0)
    est = 2 * (tm * DIN * 4 + tm * (DA + DB) * 2) + 10 * tm * DB * 4
    return pl.pallas_call(
        body, name="mix_fwd", grid=(S // tm,),
        in_specs=[pl.BlockSpec((tm, DIN), lambda i: (i, 0)),
                  pl.BlockSpec((8, DIN), lambda i: (jnp.maximum(i * (tm // 8) - 1, 0), 0)),
                  pl.BlockSpec(ws.shape, full3), pl.BlockSpec(bsb.shape, full3),
                  pl.BlockSpec(gv.shape, full2), pl.BlockSpec(bv.shape, full2), pl.BlockSpec(cw8.shape, full2)],
        out_specs=pl.BlockSpec((tm, DA + DB), lambda i: (i, 0)),
        out_shape=jax.ShapeDtypeStruct((S, DA + DB), BF16),
        compiler_params=pltpu.CompilerParams(dimension_semantics=("arbitrary",),
                                             vmem_limit_bytes=_vmem_limit(est)),
    )(z, z, ws, bsb, gv, bv, cw8)


def _mix_bwd(z, dy, ws, bsb, gv, bv, cw8, tm):
    S, DIN = z.shape
    DA = gv.shape[1]
    DB = DA
    H = DA // HEAD
    o = 2 * DA
    nch = tm // HEAD
    nblk = S // tm

    def body(z_ref, zp_ref, zn_ref, dy_ref, dyn_ref, ws_ref, bsb_ref, gv_ref, bv_ref, cw_ref,
             dz_ref, dws_ref, dbs_ref, dgv_ref, dbv_ref, dcw_ref, dbin_ref):
        i = pl.program_id(0)

        @pl.when(i == 0)
        def _():
            for r in (dws_ref, dbs_ref, dgv_ref, dbv_ref, dcw_ref, dbin_ref):
                r[...] = jnp.zeros_like(r)

        ri = lax.broadcasted_iota(jnp.int32, (HEAD, HEAD), 0)
        ci = lax.broadcasted_iota(jnp.int32, (HEAD, HEAD), 1)
        tril = ri >= ci
        for h in range(H):
            cs = slice(h * HEAD, (h + 1) * HEAD)
            vs = slice(DA + h * HEAD, DA + (h + 1) * HEAD)
            wb = jnp.where(tril, ws_ref[h], 0.0).astype(BF16)
            gvh = gv_ref[:, cs]
            dws_acc = jnp.zeros((HEAD, HEAD), F32)
            dbs_acc = jnp.zeros((HEAD, HEAD), F32)
            dgv_acc = jnp.zeros((1, HEAD), F32)
            dbv_acc = jnp.zeros((1, HEAD), F32)
            dbu_acc = jnp.zeros((1, HEAD), F32)
            dbvv_acc = jnp.zeros((1, HEAD), F32)
            for n in range(nch):
                rs = slice(n * HEAD, (n + 1) * HEAD)
                u, du_dz = _gelu_and_grad(z_ref[rs, cs])
                v, dv_dz = _gelu_and_grad(z_ref[rs, vs])
                vhat, rstd = _ln_stats(v)
                vnb = (vhat * gvh + bv_ref[:, cs]).astype(BF16)
                mixed = jnp.dot(wb, vnb, preferred_element_type=F32) + bsb_ref[h]
                dya = dy_ref[rs, cs]
                dzu = dya * mixed * du_dz
                dmx = dya * u
                dmxb = dmx.astype(BF16)
                dbs_acc += jnp.broadcast_to(jnp.sum(dmx, axis=1, keepdims=True), (HEAD, HEAD))
                dws_acc += lax.dot_general(dmxb, vnb, _DIMS['nt'], preferred_element_type=F32)
                dvn = lax.dot_general(wb, dmxb, _DIMS['tn'], preferred_element_type=F32)
                dgv_acc += jnp.sum(dvn * vhat, axis=0, keepdims=True)
                dbv_acc += jnp.sum(dvn, axis=0, keepdims=True)
                dxh = dvn * gvh
                m1 = jnp.mean(dxh, axis=-1, keepdims=True)
                m2 = jnp.mean(dxh * vhat, axis=-1, keepdims=True)
                dzv = rstd * (dxh - m1 - vhat * m2) * dv_dz
                dz_ref[rs, cs] = dzu.astype(BF16)
                dz_ref[rs, vs] = dzv.astype(BF16)
                dbu_acc += jnp.sum(dzu, axis=0, keepdims=True)
                dbvv_acc += jnp.sum(dzv, axis=0, keepdims=True)
            dws_ref[h] += jnp.where(tril, dws_acc, 0.0)
            dbs_ref[h] += dbs_acc
            dgv_ref[:, cs] += dgv_acc
            dbv_ref[:, cs] += dbv_acc
            dbin_ref[:, cs] += dbu_acc
            dbin_ref[:, vs] += dbvv_acc

        keep = (i > 0).astype(F32)
        more = (i < nblk - 1).astype(F32)
        rowi = lax.broadcasted_iota(jnp.int32, (tm, HEAD), 0)
        for s in range(DB // HEAD):
            ch = slice(s * HEAD, (s + 1) * HEAD)
            bs = slice(o + s * HEAD, o + (s + 1) * HEAD)
            cs = slice(o + DB + s * HEAD, o + DB + (s + 1) * HEAD)
            xs = slice(o + 2 * DB + s * HEAD, o + 2 * DB + (s + 1) * HEAD)
            ys = slice(DA + s * HEAD, DA + (s + 1) * HEAD)
            zb, zc, zx = z_ref[:, bs], z_ref[:, cs], z_ref[:, xs]
            hc = zc * zx
            p1 = zp_ref[7:8, cs] * zp_ref[7:8, xs] * keep
            p2 = zp_ref[6:7, cs] * zp_ref[6:7, xs] * keep
            r1, r2 = _conv_shifted(hc, p1, p2, rowi)
            w0, w1, w2 = cw_ref[0:1, ch], cw_ref[1:2, ch], cw_ref[2:3, ch]
            cv = w0 * r2 + w1 * r1 + w2 * hc
            dyb = dy_ref[:, ys]
            dzb = dyb * cv
            dcv = dyb * zb
            n0 = dyn_ref[0:1, ys] * zn_ref[0:1, bs] * more
            n1 = dyn_ref[1:2, ys] * zn_ref[1:2, bs] * more
            f1 = jnp.where(rowi == tm - 1, n0, pltpu.roll(dcv, tm - 1, 0))
            f2 = jnp.where(rowi == tm - 1, n1, jnp.where(rowi == tm - 2, n0, pltpu.roll(dcv, tm - 2, 0)))
            dhc = w2 * dcv + w1 * f1 + w0 * f2
            dzc = dhc * zx
            dzx = dhc * zc
            dcw_ref[0:1, ch] += jnp.sum(dcv * r2, axis=0, keepdims=True)
            dcw_ref[1:2, ch] += jnp.sum(dcv * r1, axis=0, keepdims=True)
            dcw_ref[2:3, ch] += jnp.sum(dcv * hc, axis=0, keepdims=True)
            dz_ref[:, bs] = dzb.astype(BF16)
            dz_ref[:, cs] = dzc.astype(BF16)
            dz_ref[:, xs] = dzx.astype(BF16)
            dbin_ref[:, bs] += jnp.sum(dzb, axis=0, keepdims=True)
            dbin_ref[:, cs] += jnp.sum(dzc, axis=0, keepdims=True)
            dbin_ref[:, xs] += jnp.sum(dzx, axis=0, keepdims=True)

    full3 = lambda i: (0, 0, 0)
    full2 = lambda i: (0, 0)
    prev8 = lambda i: (jnp.maximum(i * (tm // 8) - 1, 0), 0)
    next8 = lambda i: (jnp.minimum((i + 1) * (tm // 8), S // 8 - 1), 0)
    est = 2 * (tm * DIN * 4 + tm * (DA + DB) * 4 + tm * DIN * 2) + 16 * tm * DB * 4
    return pl.pallas_call(
        body, name="mix_bwd", grid=(nblk,),
        in_specs=[pl.BlockSpec((tm, DIN), lambda i: (i, 0)),
                  pl.BlockSpec((8, DIN), prev8), pl.BlockSpec((8, DIN), next8),
                  pl.BlockSpec((tm, DA + DB), lambda i: (i, 0)), pl.BlockSpec((8, DA + DB), next8),
                  pl.BlockSpec(ws.shape, full3), pl.BlockSpec(bsb.shape, full3),
                  pl.BlockSpec(gv.shape, full2), pl.BlockSpec(bv.shape, full2), pl.BlockSpec(cw8.shape, full2)],
        out_specs=[pl.BlockSpec((tm, DIN), lambda i: (i, 0)),
                   pl.BlockSpec(ws.shape, full3), pl.BlockSpec(ws.shape, full3),
                   pl.BlockSpec(gv.shape, full2), pl.BlockSpec(gv.shape, full2),
                   pl.BlockSpec(cw8.shape, full2), pl.BlockSpec((1, DIN), full2)],
        out_shape=[jax.ShapeDtypeStruct((S, DIN), BF16),
                   jax.ShapeDtypeStruct(ws.shape, F32), jax.ShapeDtypeStruct(ws.shape, F32),
                   jax.ShapeDtypeStruct(gv.shape, F32), jax.ShapeDtypeStruct(gv.shape, F32),
                   jax.ShapeDtypeStruct(cw8.shape, F32), jax.ShapeDtypeStruct((1, DIN), F32)],
        compiler_params=pltpu.CompilerParams(dimension_semantics=("arbitrary",),
                                             vmem_limit_bytes=_vmem_limit(est)),
    )(z, z, z, dy, dy, ws, bsb, gv, bv, cw8)


def _place():
    x, y, c = lax.axis_index("x"), lax.axis_index("y"), lax.axis_index("c")
    others = [(1 - x, y), (x, 1 - y), (1 - x, 1 - y)]
    return x, y, c, 2 * x + y, others, [2 * ox + oy for ox, oy in others]


def _half(c, rows_half):
    return pl.ds(pl.multiple_of(c * rows_half, 16), rows_half)


def _rcopy(src, dst, ssem, rsem, dev):
    return pltpu.make_async_remote_copy(src_ref=src, dst_ref=dst, send_sem=ssem, recv_sem=rsem,
                                        device_id=dev, device_id_type=MESH)


class _Comm:
    def __init__(self, aliased, ins, fresh, n_sems, stages, sink, relayed=False):
        self.aliased, self.ins, self.fresh = list(aliased), list(ins), list(fresh)
        self.n_sems, self.stages, self.sink, self.relayed = n_sems, stages, sink, relayed


def _in_hbm(a):
    return pltpu.with_memory_space_constraint(a, pltpu.HBM)


def _comm_operands(comms, n_in, n_out):
    arrays, shapes, scratch, aliases = [], [], [], {}
    for cm in comms:
        for a in cm.aliased:
            aliases[n_in + len(arrays)] = n_out + len(shapes)
            arrays.append(_in_hbm(a))
            shapes.append(pltpu.HBM(a.shape, a.dtype))
        arrays += [_in_hbm(a) for a in cm.ins]
        shapes += [pltpu.HBM(s.shape, s.dtype) for s in cm.fresh]
        scratch += [pltpu.SemaphoreType.DMA((cm.n_sems,))] * 2
    return arrays, shapes, scratch, aliases


def _comm_refs(comms, in_refs, out_refs, sem_refs):
    per, pi, po = [], 0, 0
    for i, cm in enumerate(comms):
        pi += len(cm.aliased)
        ins = in_refs[pi:pi + len(cm.ins)]
        pi += len(cm.ins)
        al = out_refs[po:po + len(cm.aliased)]
        po += len(cm.aliased)
        fr = out_refs[po:po + len(cm.fresh)]
        po += len(cm.fresh)
        per.append((al, ins, fr, sem_refs[2 * i], sem_refs[2 * i + 1]))
    return per


def _comm_first_step(comms, hosted, grid):
    if not comms:
        return
    at_first = functools.reduce(lambda a, b: a & b, [pl.program_id(d) == 0 for d in range(len(grid))])

    @pl.when(at_first)
    def _():
        for cm, refs in zip(comms, hosted):
            for d in cm.stages[0](*refs, False):
                d.start()


def _comm_mid_step(comms, hosted, grid):
    relayed = [(cm, refs) for cm, refs in zip(comms, hosted) if cm.relayed]
    if not relayed:
        return
    n_steps = math.prod(grid)
    assert n_steps >= 3, grid
    mid, at_mid = (2 * n_steps) // 3, None
    for d in reversed(range(len(grid))):
        here = pl.program_id(d) == mid % grid[d]
        at_mid = here if at_mid is None else at_mid & here
        mid //= grid[d]

    @pl.when(at_mid)
    def _():
        for cm, refs in relayed:
            for d in cm.stages[0](*refs, True):
                d.wait()
            for d in cm.stages[1](*refs, False):
                d.start()


def _comm_last_step(comms, hosted, grid):
    if not comms:
        return
    at_last = functools.reduce(lambda a, b: a & b, [pl.program_id(d) == g - 1 for d, g in enumerate(grid)])

    @pl.when(at_last)
    def _():
        for cm, refs in zip(comms, hosted):
            for d in cm.stages[1 if cm.relayed else 0](*refs, True):
                d.wait()
        for cm, refs in zip(comms, hosted):
            for stage in cm.stages[2 if cm.relayed else 1:]:
                for d in stage(*refs, False):
                    d.start()
                for d in stage(*refs, True):
                    d.wait()


def _deliver(comms, outs):
    pos = 0
    for cm in comms:
        al = outs[pos:pos + len(cm.aliased)]
        pos += len(cm.aliased)
        fr = outs[pos:pos + len(cm.fresh)]
        pos += len(cm.fresh)
        cm.sink(list(al), list(fr))


def _comm_call(name, comms):
    in_arrays, out_shapes, scratch, aliases = _comm_operands(comms, 0, 0)
    n_in, n_out = len(in_arrays), len(out_shapes)

    def body(*refs):
        hosted = _comm_refs(comms, refs[:n_in], refs[n_in:n_in + n_out], refs[n_in + n_out:])
        for cm, r in zip(comms, hosted):
            for stage in cm.stages:
                for d in stage(*r, False):
                    d.start()
                for d in stage(*r, True):
                    d.wait()

    outs = pl.pallas_call(body, name=name, in_specs=[ANY] * n_in, out_specs=[ANY] * n_out,
                          out_shape=out_shapes, scratch_shapes=scratch, input_output_aliases=aliases)(*in_arrays)
    _deliver(comms, outs)


def _c_gather(fulls, sink, ici, fwd, conv=None):
    n = len(fulls)
    n_direct = 2 * n + (3 if conv is not None else 0)
    n_ici = n_direct + n if ici else 0

    def stage_direct(al, ins, fr, ssem, rsem, wait):
        x, y, c, chip, others, ochips = _place()
        ds = []
        for m in range(n):
            ah = al[m].shape[1] // 2
            mine = al[m].at[chip, _half(c, ah)]
            for k in range(2):
                i = 2 * m + k
                ds.append(_rcopy(mine, al[m].at[ochips[k], _half(c, ah)], ssem.at[i], rsem.at[i], (x, y, c)) if wait
                          else _rcopy(mine, mine, ssem.at[i], rsem.at[i], (*others[k], c)))
        if conv is not None:
            mine = al[n].at[chip]
            for k in range(3):
                i = 2 * n + k
                ds.append(_rcopy(mine, al[n].at[ochips[k]], ssem.at[i], rsem.at[i], (x, y, c)) if wait
                          else _rcopy(mine, mine, ssem.at[i], rsem.at[i], (*others[k], c)))
        return ds

    def stage_relay(al, ins, fr, ssem, rsem, wait):
        x, y, c, chip, others, ochips = _place()
        got_from = ochips[1] + (ochips[0] - ochips[1]) * c
        send_to = (x + (1 - 2 * x) * (1 - c), y + (1 - 2 * y) * c, c)
        ds = []
        for m in range(n):
            ah = al[m].shape[1] // 2
            i = n_direct + m
            got = al[m].at[got_from, _half(c, ah)]
            ds.append(_rcopy(got, al[m].at[ochips[2], _half(c, ah)], ssem.at[i], rsem.at[i], (x, y, c)) if wait
                      else _rcopy(got, got, ssem.at[i], rsem.at[i], send_to))
        return ds

    def stage_fwd(al, ins, fr, ssem, rsem, wait):
        x, y, c, chip, others, ochips = _place()
        ds = []
        for m in range(n):
            ah = al[m].shape[1] // 2
            for k in range(3):
                i = n_ici + 3 * m + k
                got = al[m].at[ochips[k], _half(c, ah)]
                ds.append(_rcopy(got, al[m].at[ochips[k], _half(1 - c, ah)], ssem.at[i], rsem.at[i], (x, y, c)) if wait
                          else _rcopy(got, got, ssem.at[i], rsem.at[i], (x, y, 1 - c)))
        return ds

    stages = ([stage_direct, stage_relay] if ici else []) + ([stage_fwd] if fwd else [])
    return _Comm(list(fulls) + ([conv] if conv is not None else []), [], [], n_ici + (3 * n if fwd else 0),
                 stages, sink, relayed=ici)


def _c_rs_pair(parts, sink):
    def stage(al, ins, fr, ssem, rsem, wait):
        x, y, c, _, _, _ = _place()
        return [_rcopy(ins[m].at[:, _half(1 - c, ins[m].shape[1] // 2), :], fr[m], ssem.at[m], rsem.at[m],
                       (x, y, 1 - c)) for m in range(len(ins))]
    fresh = [jax.ShapeDtypeStruct((p.shape[0], p.shape[1] // 2, p.shape[2]), p.dtype) for p in parts]
    return _Comm([], parts, fresh, len(parts), [stage], sink)


def _c_rs_chips(psums, sink):
    def stage(al, ins, fr, ssem, rsem, wait):
        x, y, c, _, others, ochips = _place()
        return [_rcopy(ins[m].at[ochips[k]], fr[m].at[k], ssem.at[3 * m + k], rsem.at[3 * m + k], (*others[k], c))
                for m in range(len(ins)) for k in range(3)]
    fresh = [jax.ShapeDtypeStruct((3,) + p.shape[1:], p.dtype) for p in psums]
    return _Comm([], psums, fresh, 3 * len(psums), [stage], sink)


def _c_exchange(g2s, sink):
    def stage(al, ins, fr, ssem, rsem, wait):
        x, y, c, _, _, _ = _place()
        if wait:
            return [_rcopy(al[m].at[c], al[m].at[1 - c], ssem.at[m], rsem.at[m], (x, y, c)) for m in range(len(al))]
        return [_rcopy(al[m].at[c], al[m].at[c], ssem.at[m], rsem.at[m], (x, y, 1 - c)) for m in range(len(al))]
    return _Comm(g2s, [], [], len(g2s), [stage], sink)


def _c_allgather_small(buf, sink):
    def stage(al, ins, fr, ssem, rsem, wait):
        x, y, c, _, _, _ = _place()
        ds = []
        for r in range(1, 8):
            peer = (1 - x if r & 4 else x, 1 - y if r & 2 else y, 1 - c if r & 1 else c)
            ds.append(_rcopy(al[0].at[0], al[0].at[r], ssem.at[r - 1], rsem.at[r - 1], peer))
        return ds
    return _Comm([buf], [], [], 7, [stage], sink)


def _sum_small(buf, dev_idx):
    _, m, n = buf.shape

    def body(me_ref, b_ref, o_ref):
        me = me_ref[0]
        acc = b_ref[me]
        for d in range(1, 8):
            acc = acc + b_ref[jnp.bitwise_xor(me, d)]
        o_ref[...] = acc

    gs = pltpu.PrefetchScalarGridSpec(
        num_scalar_prefetch=1, grid=(1,),
        in_specs=[pl.BlockSpec(buf.shape, lambda i, me: (0, 0, 0))],
        out_specs=pl.BlockSpec((m, n), lambda i, me: (0, 0)))
    return pl.pallas_call(body, name="sum_small", grid_spec=gs,
                          out_shape=jax.ShapeDtypeStruct((m, n), F32))(dev_idx, buf)


def _sum_pair(name, part, sib, c_idx):
    _, A, B = part.shape
    ah = A // 2
    br = _rows(ah, B * 2)

    def body(c_ref, a_ref, b_ref, o_ref):
        o_ref[...] = (a_ref[...].astype(F32) + b_ref[...].astype(F32)).astype(BF16)

    gs = pltpu.PrefetchScalarGridSpec(
        num_scalar_prefetch=1, grid=(N_CHIPS, ah // br),
        in_specs=[pl.BlockSpec((None, None, br, B), lambda j, i, c: (j, c[0], i, 0)),
                  pl.BlockSpec((None, br, B), lambda j, i, c: (j, i, 0))],
        out_specs=pl.BlockSpec((None, br, B), lambda j, i, c: (j, i, 0)))
    return pl.pallas_call(body, name=name, grid_spec=gs,
                          out_shape=pltpu.HBM((N_CHIPS, ah, B), BF16),
                          compiler_params=pltpu.CompilerParams(vmem_limit_bytes=_vmem_limit(2 * 3 * br * B * 2)),
                          )(c_idx, _in_hbm(part.reshape(N_CHIPS, 2, ah, B)), _in_hbm(sib))


def _sum_chips(name, psum, recv, place_idx):
    _, ah, B = psum.shape
    br = _rows(ah, B * 4)

    def body(c_ref, p_ref, r0_ref, r1_ref, r2_ref, o_ref):
        o_ref[...] = ((p_ref[...].astype(F32) + r0_ref[...].astype(F32)) + r1_ref[...].astype(F32)) \
            + r2_ref[...].astype(F32)

    gs = pltpu.PrefetchScalarGridSpec(
        num_scalar_prefetch=1, grid=(ah // br,),
        in_specs=[pl.BlockSpec((None, br, B), lambda i, c: (c[0], i, 0))]
        + [pl.BlockSpec((None, br, B), functools.partial(lambda i, c, k: (k, i, 0), k=k)) for k in range(3)],
        out_specs=pl.BlockSpec((None, br, B), lambda i, c: (c[1], i, 0)))
    return pl.pallas_call(body, name=name, grid_spec=gs,
                          out_shape=pltpu.HBM((2, ah, B), F32),
                          compiler_params=pltpu.CompilerParams(vmem_limit_bytes=_vmem_limit(2 * br * B * (4 * 2 + 4))),
                          )(place_idx, _in_hbm(psum), *[_in_hbm(recv)] * 3)


def _cast_into(name, a, place_idx):
    A, B = a.shape
    br = _rows(A, B * 4)

    def body(c_ref, a_ref, o_ref):
        o_ref[...] = a_ref[...].astype(BF16)

    gs = pltpu.PrefetchScalarGridSpec(
        num_scalar_prefetch=1, grid=(A // br,),
        in_specs=[pl.BlockSpec((br, B), lambda i, c: (i, 0))],
        out_specs=pl.BlockSpec((None, br, B), lambda i, c: (c[0], i, 0)))
    return pl.pallas_call(body, name=name, grid_spec=gs,
                          out_shape=pltpu.HBM((N_CHIPS, A, B), BF16))(place_idx, a)


def _cast_behind(x, shards, place_idx, comms):
    n_steps = 16
    n = len(shards)
    xv = x
    svs = list(shards)

    def pace(v):
        for steps_per_block in (1, 2, 4, 8, 16):
            blocks = n_steps // steps_per_block
            if v.shape[0] % blocks == 0 and (v.shape[0] // blocks) % 16 == 0:
                return steps_per_block
        raise ValueError(v.shape)

    n_in, n_out = 2 + n, 1 + n
    cm_in_arrays, cm_out_shapes, cm_scratch, aliases = _comm_operands(comms, n_in, n_out)
    n_cm_in, n_cm_out = len(cm_in_arrays), len(cm_out_shapes)
    grid = (n_steps,)

    def body(c_ref, x_ref, *rest):
        s_refs = rest[:n]
        xo_ref = rest[n + n_cm_in]
        so_refs = rest[n + n_cm_in + 1:n + n_cm_in + 1 + n]
        o_end = n + n_cm_in + 1 + n
        hosted = _comm_refs(comms, rest[n:n + n_cm_in], rest[o_end:o_end + n_cm_out], rest[o_end + n_cm_out:])
        _comm_first_step(comms, hosted, grid)
        _comm_mid_step(comms, hosted, grid)
        xo_ref[...] = x_ref[...].astype(BF16)
        for s_ref, o_ref in zip(s_refs, so_refs):
            o_ref[...] = s_ref[...].astype(BF16)
        _comm_last_step(comms, hosted, grid)

    def rows(v):
        return v.shape[0] * pace(v) // n_steps

    def in_spec(v):
        return pl.BlockSpec((rows(v), v.shape[1]), functools.partial(lambda i, c, q: (i // q, 0), q=pace(v)))

    def shard_out_spec(v):
        return pl.BlockSpec((None, rows(v), v.shape[1]),
                            functools.partial(lambda i, c, q: (c[0], i // q, 0), q=pace(v)))

    gs = pltpu.PrefetchScalarGridSpec(
        num_scalar_prefetch=1, grid=grid,
        in_specs=[in_spec(v) for v in [xv] + svs] + [ANY] * n_cm_in,
        out_specs=[in_spec(xv)] + [shard_out_spec(v) for v in svs] + [ANY] * n_cm_out,
        scratch_shapes=cm_scratch)
    outs = pl.pallas_call(
        body, name="cast_behind", grid_spec=gs,
        out_shape=[jax.ShapeDtypeStruct(xv.shape, BF16)]
        + [pltpu.HBM((N_CHIPS,) + v.shape, BF16) for v in svs] + cm_out_shapes,
        input_output_aliases=aliases,
        compiler_params=pltpu.CompilerParams(
            dimension_semantics=("arbitrary",),
            vmem_limit_bytes=_vmem_limit(2 * sum(rows(v) * v.shape[1] * 6 for v in [xv] + svs))),
    )(place_idx, xv, *svs, *cm_in_arrays)
    _deliver(comms, outs[n_out:])
    return outs[0].reshape(x.shape), [o.reshape((N_CHIPS,) + s.shape) for o, s in zip(outs[1:n_out], shards)]


def _adamw_small(gsum, gconv, offsets, ws, ms, vs):
    n = len(ws)

    def body(*refs):
        g_ref, gc_ref = refs[0], refs[1]
        w_refs = refs[2:2 + n]
        m_refs = refs[2 + n:2 + 2 * n]
        v_refs = refs[2 + 2 * n:2 + 3 * n]
        outs = refs[2 + 3 * n:]
        for i in range(n):
            rows = w_refs[i].shape[0]
            g = gc_ref[...] if offsets[i] is None else g_ref[offsets[i]:offsets[i] + rows, :]
            d, m2, v2 = _adamw_math(w_refs[i][...], g, m_refs[i][...], v_refs[i][...])
            outs[4 * i][...] = g
            outs[4 * i + 1][...] = d
            outs[4 * i + 2][...] = m2
            outs[4 * i + 3][...] = v2

    vm = pl.BlockSpec(memory_space=pltpu.VMEM)
    out_shape = []
    for w in ws:
        out_shape += [jax.ShapeDtypeStruct(w.shape, F32)] * 4
    return pl.pallas_call(body, name="adamw_small", in_specs=[vm] * (2 + 3 * n), out_specs=[vm] * (4 * n),
                          out_shape=out_shape)(gsum, gconv, *ws, *ms, *vs)


def _ident(accs, tes, res):
    return [accs[0]]


def _ffn_up(name, hb, wg4, wu4, comms=()):
    S, D = hb.shape

    def epi(accs, tes, res):
        g, u = accs
        return [g, u, g * _sigmoid(g) * u]

    return _mm(name, 'nn', [hb], [wg4, wu4], [(0, 0, 0), (0, 1, 1)], 2, epi, [BF16] * 3,
               bm=_blk(S, 512), bn=wg4.shape[2], bk=D, b_blocked=True, n_outer=True, comms=comms)


def _ffn_gate(name, hb, wg4, comms=()):
    S, D = hb.shape
    return _mm(name, 'nn', [hb], [wg4], [(0, 0, 0)], 1, _ident, [BF16],
               bm=_blk(S, 1024), bn=wg4.shape[2], bk=D, b_blocked=True, n_outer=True, comms=comms)[0]


def _ffn_upmul(name, hb, wu4, gb, comms=()):
    S, D = hb.shape

    def epi(accs, tes, res):
        g = tes[0].astype(F32)
        return [accs[0], g * _sigmoid(g) * accs[0]]

    return _mm(name, 'nn', [hb], [wu4], [(0, 0, 0)], 1, epi, [BF16, BF16],
               bm=_blk(S, 1024), bn=wu4.shape[2], bk=D, tile_extras=[gb], b_blocked=True, n_outer=True, comms=comms)


def _ffn_down(name, ab, wd, hres, comms=()):
    S, F = ab.shape
    D = wd.shape[1]
    return _mm(name, 'nn', [ab], [wd], [(0, 0, 0)], 1, lambda accs, tes, res: [ALPHA * tes[0] + 0.5 * accs[0]],
               [F32], bm=_blk(S, 1024), bn=_blk(D, 512), bk=F, tile_extras=[hres], comms=comms)[0]


def _ffn_da(name, dsub, wd, gb, ub, comms=()):
    S, D = dsub.shape
    F = wd.shape[0]

    def epi(accs, tes, res):
        da = accs[0]
        g = tes[0].astype(F32)
        u = tes[1].astype(F32)
        s = _sigmoid(g)
        silu = g * s
        return [(da * u) * (s + silu * (1.0 - s)), da * silu]

    return _mm(name, 'nt', [dsub], [wd], [(0, 0, 0)], 1, epi, [BF16, BF16],
               bm=_blk(S, 1024), bn=_blk(F, 512), bk=D, tile_extras=[gb, ub], comms=comms)


def _dw_rows(name, ab, dsub, cs, comms=()):
    S, D = dsub.shape
    return _mm(name, 'tn', [ab], [dsub], [(0, 0, 0)], 1, _ident, [BF16],
               bm=cs, bn=_blk(D, 1024), bk=_blk(S, 2048), comms=comms)[0].reshape(N_CHIPS, cs, D)


def _dw_cols(name, hb, dxb, cs, comms=()):
    S, D = hb.shape
    return _mm(name, 'tn', [hb], [dxb], [(0, 0, 0)], 1, _ident, [BF16],
               bm=_blk(D, 1024), bn=cs, bk=_blk(S, 2048), out_blocked=True, comms=comms)[0]


def _ffn_dh(name, dgb, dub, wg4, wu4, dres, comms=()):
    S = dgb.shape[0]
    D = wg4.shape[1]
    return _mm(name, 'nt', [dgb, dub], [wg4, wu4], [(0, 0, 0), (1, 1, 0)], 1,
               lambda accs, tes, res: [accs[0] + tes[0]], [F32],
               bm=_blk(S, 1024), bn=_blk(D, 1024), bk=wg4.shape[2], tile_extras=[dres], b_blocked=True,
               comms=comms)[0]


def _step(p):
    x = p['x'][0]
    target = p['loss_target'][0]
    S, D = x.shape
    cx, cy, cc = lax.axis_index("x"), lax.axis_index("y"), lax.axis_index("c")
    chip = 2 * cx + cy
    c_idx = jnp.reshape(cc, (1,)).astype(jnp.int32)
    place_idx = jnp.stack([chip, cc]).astype(jnp.int32)

    w_s = p['w_s'][0]
    H = w_s.shape[0]
    DA = H * HEAD
    bsb = jnp.broadcast_to(p['b_s'][0][:, :, None], (H, HEAD, HEAD))
    conv_shard = p['conv_w'][0]
    conv8 = jnp.zeros((8, conv_shard.shape[1]), F32).at[:CONV_TAPS].set(conv_shard)

    W = {'ffa_gate': _cast_into("cast_ffa_gate", p['ffa_gate'][0], place_idx)}
    conv = {'full': lax.dynamic_update_slice(jnp.zeros((N_CHIPS,) + conv8.shape, F32), conv8[None], (chip, 0, 0))}

    def gathered(names, with_conv=False):
        def sink(al, fr):
            W.update(zip(names, al))
            if with_conv:
                conv['full'] = al[len(names)]
        return sink

    def c_gather(names, ici, fwd, with_conv=False):
        return _c_gather([W[n] for n in names], gathered(names, with_conv), ici, fwd,
                         conv['full'] if with_conv else None)

    rest = [n for n in BIG if n != 'ffa_gate']
    xb, casted = _cast_behind(x, [p[n][0] for n in rest], place_idx, comms=[c_gather(['ffa_gate'], True, True)])
    W.update(zip(rest, casted))
    cs_ff = W['ffa_gate'].shape[2]
    F = N_CHIPS * cs_ff

    ga = _ffn_gate("ffa_gate", xb, W['ffa_gate'],
                   comms=[c_gather(['ffa_up'], True, True), c_gather(['ffa_down'], True, False)])
    ua, aa = _ffn_upmul("ffa_up", xb, W['ffa_up'], ga,
                        comms=[c_gather(['ffa_down'], False, True),
                               c_gather(['w_in', 'ffc_gate'], True, False, with_conv=True)])
    cw8 = jnp.transpose(conv['full'], (1, 0, 2)).reshape(8, DA)
    wd_a = W['ffa_down'].reshape(F, D)
    r1 = _ffn_down("ffa_down", aa, wd_a, x,
                   comms=[c_gather(['w_in', 'ffc_gate'], False, True), c_gather(['w_out', 'ffc_down'], True, False)])
    h1, h1b = _ln_fwd("ln_a", r1, p['ln_a_g'], p['ln_a_b'])
    w_in4 = W['w_in']
    cs_in = w_in4.shape[2]
    z = _mm("w_in", 'nn', [h1b], [w_in4], [(0, 0, 0)], 1, lambda accs, tes, res: [accs[0] + res[0]], [F32],
            bm=_blk(S, 1024), bn=cs_in, bk=D, row_extras=[p['b_in']], b_blocked=True, n_outer=True,
            comms=[c_gather(['w_out', 'ffc_down'], False, True), c_gather(['ffc_up'], True, False)])[0]
    w_out = W['w_out'].reshape(DA * 2, D)
    yb = _mix_fwd(z, w_s, bsb, p['ln_v_g'], p['ln_v_b'], cw8, _blk(S, 2 * HEAD))
    r2 = _mm("w_out", 'nn', [yb], [w_out], [(0, 0, 0)], 1,
             lambda accs, tes, res: [accs[0] + res[0] + ALPHA * tes[0]], [F32],
             bm=_blk(S, 1024), bn=_blk(D, 1024), bk=2 * DA, tile_extras=[h1], row_extras=[p['b_out']],
             comms=[c_gather(['ffc_up'], False, True)])[0]
    h2, h2b = _ln_fwd("ln_m", r2, p['ln_m_g'], p['ln_m_b'])
    gc, uc, ac = _ffn_up("ffc_up", h2b, W['ffc_gate'], W['ffc_up'])
    wd_c = W['ffc_down'].reshape(F, D)
    r3 = _ffn_down("ffc_down", ac, wd_c, h2)

    pg, sg, sib, psum, recv, g2 = {}, {}, {}, {}, {}, {}

    def c_pair(names):
        return _c_rs_pair([pg[n] for n in names], lambda al, fr: sib.update(zip(names, fr)))

    def c_chips(names):
        return _c_rs_chips([psum[n] for n in names], lambda al, fr: recv.update(zip(names, fr)))

    def c_swap(names):
        return _c_exchange([g2[n] for n in names], lambda al, fr: g2.update(zip(names, al)))

    def sum_pair(*names):
        for n in names:
            psum[n] = _sum_pair(f"sum_pair_{n}", pg[n], sib[n], c_idx)

    def sum_chips(*names):
        for n in names:
            g2[n] = _sum_chips(f"sum_chips_{n}", psum[n], recv[n], place_idx)

    dres3, dsub3, sg['ln_c_g'], sg['ln_c_b'], lsum = _ln_loss_bwd(
        "ln_c_loss_bwd", r3, target, p['ln_c_g'], p['ln_c_b'], 0.5)

    dgc, duc = _ffn_da("ffc_da", dsub3, wd_c, gc, uc)
    pg['ffc_down'] = _dw_rows("ffc_dwd", ac, dsub3, cs_ff)
    pg['ffc_gate'] = _dw_cols("ffc_dwg", h2b, dgc, cs_ff)
    pg['ffc_up'] = _dw_cols("ffc_dwu", h2b, duc, cs_ff, comms=[c_pair(['ffc_down', 'ffc_gate'])])
    sum_pair('ffc_down', 'ffc_gate')
    dh2 = _ffn_dh("ffc_dh", dgc, duc, W['ffc_gate'], W['ffc_up'], dres3,
                  comms=[c_pair(['ffc_up']), c_chips(['ffc_down']), c_chips(['ffc_gate'])])
    sum_pair('ffc_up')
    sum_chips('ffc_down', 'ffc_gate')

    dres2, dmix, sg['ln_m_g'], sg['ln_m_b'], sg['b_out'] = _ln_bwd("ln_m_bwd", r2, dh2, p['ln_m_g'], 1.0)
    pg['w_out'] = _mm("dw_out", 'tn', [yb], [dmix], [(0, 0, 0)], 1, _ident, [BF16],
                      bm=_blk(2 * DA, 1024), bn=_blk(D, 1024),
                      bk=_blk(S, 2048))[0].reshape(N_CHIPS, 2 * DA // 4, D)
    dy = _mm("dy", 'nt', [dmix], [w_out], [(0, 0, 0)], 1, _ident, [F32],
             bm=_blk(S, 1024), bn=_blk(2 * DA, 1024), bk=D)[0]
    dz, dws, dbs, sg['ln_v_g'], sg['ln_v_b'], dcw, sg['b_in'] = _mix_bwd(
        z, dy, w_s, bsb, p['ln_v_g'], p['ln_v_b'], cw8, HEAD)
    sg['w_s'] = dws
    sg['b_s'] = dbs[:, :, 0]
    pg['w_in'] = _dw_cols("dw_in", h1b, dz, cs_in,
                          comms=[c_chips(['ffc_up']), c_pair(['w_out']), c_swap(['ffc_down', 'ffc_gate'])])
    sum_chips('ffc_up')
    sum_pair('w_out')
    dh1 = _mm("dh1", 'nt', [dz], [w_in4], [(0, 0, 0)], 1, lambda accs, tes, res: [accs[0] + tes[0]], [F32],
              bm=_blk(S, 1024), bn=_blk(D, 1024), bk=cs_in, tile_extras=[dres2], b_blocked=True,
              comms=[c_pair(['w_in']), c_chips(['w_out']), c_swap(['ffc_up'])])[0]
    sum_pair('w_in')
    sum_chips('w_out')

    dres1, dsub1, sg['ln_a_g'], sg['ln_a_b'], _ = _ln_bwd("ln_a_bwd", r1, dh1, p['ln_a_g'], 0.5)
    def as_rows(a):
        a = a.reshape(-1, 128)
        pad = (-a.shape[0]) % 8
        return jnp.pad(a, ((0, pad), (0, 0))) if pad else a

    pieces, offsets, off = [], {}, 0
    for n in SMALL + ['conv_w']:
        piece = as_rows(dcw[:CONV_TAPS] if n == 'conv_w' else sg[n])
        offsets[n] = off
        off += piece.shape[0]
        pieces.append(piece)
    packed = jnp.concatenate(pieces, axis=0)
    small = {'buf': jnp.zeros((8,) + packed.shape, F32).at[0].set(packed)}
    dga, dua = _ffn_da("ffa_da", dsub1, wd_a, ga, ua,
                       comms=[c_chips(['w_in']), c_swap(['w_out']),
                              _c_allgather_small(small['buf'], lambda al, fr: small.update(buf=al[0]))])
    sum_chips('w_in')
    gsum = _sum_small(small['buf'], jnp.reshape(4 * cx + 2 * cy + cc, (1,)).astype(jnp.int32))
    pg['ffa_down'] = _dw_rows("ffa_dwd", aa, dsub1, cs_ff)
    pg['ffa_gate'] = _dw_cols("ffa_dwg", xb, dga, cs_ff)
    pg['ffa_up'] = _dw_cols("ffa_dwu", xb, dua, cs_ff, comms=[c_pair(['ffa_down', 'ffa_gate']), c_swap(['w_in'])])
    sum_pair('ffa_down', 'ffa_gate')
    _comm_call("pair_ffa_up", [c_pair(['ffa_up'])])
    sum_pair('ffa_up')
    dx = _ffn_dh("ffa_dh", dga, dua, W['ffa_gate'], W['ffa_up'], dres1,
                 comms=[c_chips(['ffa_down']), c_chips(['ffa_gate']), c_chips(['ffa_up'])])
    sum_chips('ffa_down', 'ffa_gate', 'ffa_up')
    _comm_call("swap_ffa", [c_swap(['ffa_down', 'ffa_gate', 'ffa_up'])])

    out = {}
    for n in BIG:
        g, d, m2, v2 = _adamw(f"adamw_{n}", p[n][0], g2[n].reshape(p[n][0].shape), p['m_' + n][0], p['v_' + n][0])
        shp = p[n].shape
        out[n] = (g.reshape(shp), d.reshape(shp), m2.reshape(shp), v2.reshape(shp))

    loss = lax.psum(0.5 * jnp.sum(lsum) / D, ("x", "y", "c"))
    n_conv = CONV_TAPS * DA // 128
    conv_sum = gsum[offsets['conv_w']:offsets['conv_w'] + n_conv].reshape(CONV_TAPS, DA)
    cw = conv_shard.shape[1]
    gconv = lax.dynamic_slice_in_dim(conv_sum, (2 * cx + cy) * cw, cw, axis=1).reshape(-1, 128)
    names = SMALL + ['conv_w']
    rows128 = lambda a: a.reshape(-1, 128)
    res = _adamw_small(gsum, gconv, [offsets[n] if n != 'conv_w' else None for n in names],
                       [rows128(p[n]) for n in names], [rows128(p['m_' + n]) for n in names],
                       [rows128(p['v_' + n]) for n in names])
    for i, n in enumerate(names):
        out[n] = tuple(r.reshape(p[n].shape) for r in res[4 * i:4 * i + 4])

    return (loss, dx.reshape(p['x'].shape), *[out[n][0] for n in WEIGHTS], *[out[n][1] for n in WEIGHTS],
            *[out[n][2] for n in WEIGHTS], *[out[n][3] for n in WEIGHTS])


def kernel(x, ffa_gate, ffa_up, ffa_down, ln_a_g, ln_a_b, w_in, b_in, w_s, b_s, ln_v_g, ln_v_b, conv_w, w_out, b_out, ln_m_g, ln_m_b, ffc_gate, ffc_up, ffc_down, ln_c_g, ln_c_b, loss_target, m_ffa_gate, m_ffa_up, m_ffa_down, m_ln_a_g, m_ln_a_b, m_w_in, m_b_in, m_w_s, m_b_s, m_ln_v_g, m_ln_v_b, m_conv_w, m_w_out, m_b_out, m_ln_m_g, m_ln_m_b, m_ffc_gate, m_ffc_up, m_ffc_down, m_ln_c_g, m_ln_c_b, v_ffa_gate, v_ffa_up, v_ffa_down, v_ln_a_g, v_ln_a_b, v_w_in, v_b_in, v_w_s, v_b_s, v_ln_v_g, v_ln_v_b, v_conv_w, v_w_out, v_b_out, v_ln_m_g, v_ln_m_b, v_ffc_gate, v_ffc_up, v_ffc_down, v_ln_c_g, v_ln_c_b):
    return _step(dict(locals()))
```
